```python
import math
import jax, jax.numpy as jnp
from jax import lax
import numpy as np

D_MODEL = 1024
BATCH = 16
SEQ = 2048
DEPTH = 1

N_DIFF_HEADS = 4
DIFF_HEAD_DIM = 64
DIFF_V_DIM = 2 * DIFF_HEAD_DIM
DIFF_QK_WIDTH = N_DIFF_HEADS * 2 * DIFF_HEAD_DIM
DIFF_OUT_WIDTH = N_DIFF_HEADS * DIFF_V_DIM
Q_BLOCK = 128
GMLP_GROUPS = 4
GMLP_GROUP_DIM = 64
GMLP_WIDTH = GMLP_GROUPS * GMLP_GROUP_DIM
CHUNK = 128
MEM_LEN = 256
N_MEM_HEADS = 4
MEM_HEAD_DIM = 64
MEM_WIDTH = N_MEM_HEADS * MEM_HEAD_DIM
D_MIX = DIFF_OUT_WIDTH + GMLP_WIDTH + MEM_WIDTH
D_IN_PROJ = 2 * DIFF_QK_WIDTH + DIFF_OUT_WIDTH + 2 * GMLP_WIDTH + MEM_WIDTH
SPLITS = (DIFF_QK_WIDTH, 2 * DIFF_QK_WIDTH, 2 * DIFF_QK_WIDTH + DIFF_OUT_WIDTH,
          2 * DIFF_QK_WIDTH + DIFF_OUT_WIDTH + 2 * GMLP_WIDTH)
ROPE_THETA = 500000.0
ROT_DIM = DIFF_HEAD_DIM // 4
N_EXPERTS = 32
TOP_K = 4
D_FF = 1024
SWIGLU_LIMIT = 7.0
SWIGLU_ALPHA = 1.702
EXPERT_BLOCK = 256
EPS = 1e-6

kernel_name = 'hybrid_diffattn_sgu_memxattn_moe'


def rms_norm(x, g):
    xf = x.astype(jnp.float32)
    y = xf * lax.rsqrt(jnp.mean(xf * xf, axis=-1, keepdims=True) + EPS)
    return (y * g.astype(jnp.float32)).astype(x.dtype)


def layer_norm(x, g):
    xf = x.astype(jnp.float32)
    xc = xf - jnp.mean(xf, axis=-1, keepdims=True)
    y = xc * lax.rsqrt(jnp.mean(xc * xc, axis=-1, keepdims=True) + EPS)
    return (y * g.astype(jnp.float32)).astype(x.dtype)


def rope_tables(positions):
    inv_freq = ROPE_THETA ** (-jnp.arange(0, ROT_DIM, 2, dtype=jnp.float32) / ROT_DIM)
    ang = positions.astype(jnp.float32)[..., None] * inv_freq
    return jnp.cos(ang), jnp.sin(ang)


def apply_partial_rope(t, cos, sin):
    c = cos[:, :, None, None, :].astype(t.dtype)
    s = sin[:, :, None, None, :].astype(t.dtype)
    half = ROT_DIM // 2
    x1 = t[..., :half]
    x2 = t[..., half:ROT_DIM]
    rot = jnp.concatenate([x1 * c - x2 * s, x2 * c + x1 * s], axis=-1)
    return jnp.concatenate([rot, t[..., ROT_DIM:]], axis=-1)


def diff_attention(pq, pk, pv, cos, sin, g_q, g_k, lq1, lk1, lq2, lk2, g_sub, lambda_init):
    B, S, _ = pq.shape
    q = pq.reshape(B, S, N_DIFF_HEADS, 2, DIFF_HEAD_DIM)
    k = pk.reshape(B, S, N_DIFF_HEADS, 2, DIFF_HEAD_DIM)
    v = pv.reshape(B, S, N_DIFF_HEADS, DIFF_V_DIM)
    q = apply_partial_rope(rms_norm(q, g_q), cos, sin) * (DIFF_HEAD_DIM ** -0.5)
    k = apply_partial_rope(rms_norm(k, g_k), cos, sin)
    lam = (jnp.exp(jnp.sum(lq1.astype(jnp.float32) * lk1.astype(jnp.float32)))
           - jnp.exp(jnp.sum(lq2.astype(jnp.float32) * lk2.astype(jnp.float32)))
           + lambda_init)
    neg = jnp.finfo(jnp.float32).min
    outs = []
    for i in range(S // Q_BLOCK):
        k_end = (i + 1) * Q_BLOCK
        qs = q[:, i * Q_BLOCK:k_end]
        ks = k[:, :k_end]
        vs = v[:, :k_end]
        s = jnp.einsum('bqhcd,bkhcd->bhcqk', qs, ks).astype(jnp.float32)
        qpos = i * Q_BLOCK + jnp.arange(Q_BLOCK)
        kpos = jnp.arange(k_end)
        s = jnp.where(kpos[None, :] <= qpos[:, None], s, neg)
        p = jax.nn.softmax(s, axis=-1)
        a = p[:, :, 0] - lam * p[:, :, 1]
        outs.append(jnp.einsum('bhqk,bkhe->bqhe', a.astype(v.dtype), vs))
    o = jnp.concatenate(outs, axis=1)
    o = rms_norm(o, g_sub) * (1.0 - lambda_init)
    return o.reshape(B, S, DIFF_OUT_WIDTH)


def spatial_gating(pg, g_sgu, w_sp, b_sp):
    B, S, _ = pg.shape
    z = jax.nn.gelu(pg)
    u, vg = z[..., :GMLP_WIDTH], z[..., GMLP_WIDTH:]
    vg = layer_norm(vg, g_sgu).reshape(B, S // CHUNK, CHUNK, GMLP_GROUPS, GMLP_GROUP_DIM)
    w_causal = jnp.tril(w_sp).astype(vg.dtype)
    mixed = jnp.einsum('gts,bnsgc->bntgc', w_causal, vg) + b_sp.T.astype(vg.dtype)[None, None, :, :, None]
    return u * mixed.reshape(B, S, GMLP_WIDTH)


def memory_cross_attention(pc, mem, g_mem, w_kv, g_q, g_k):
    B, S, _ = pc.shape
    q = rms_norm(pc.reshape(B, S, N_MEM_HEADS, MEM_HEAD_DIM), g_q) * (MEM_HEAD_DIM ** -0.5)
    kv = rms_norm(mem, g_mem) @ w_kv
    M = mem.shape[1]
    k = rms_norm(kv[..., :MEM_WIDTH].reshape(B, M, N_MEM_HEADS, MEM_HEAD_DIM), g_k)
    v = kv[..., MEM_WIDTH:].reshape(B, M, N_MEM_HEADS, MEM_HEAD_DIM)
    s = jnp.einsum('bqhd,bkhd->bhqk', q, k).astype(jnp.float32)
    p = jax.nn.softmax(s, axis=-1).astype(v.dtype)
    return jnp.einsum('bhqk,bkhd->bqhd', p, v).reshape(B, S, MEM_WIDTH)


def moe_ffn(h, w_router, b_router, w1, b1, w2, b2):
    B, S, D = h.shape
    t = h.reshape(-1, D)
    N = t.shape[0]
    logits = (t @ w_router).astype(jnp.float32) + b_router.astype(jnp.float32)
    top_val, top_idx = lax.top_k(logits, TOP_K)
    gates = jax.nn.softmax(top_val, axis=-1).astype(t.dtype)
    A = N * TOP_K
    expert_flat = top_idx.reshape(-1)
    token_flat = jnp.arange(A) // TOP_K
    gate_flat = gates.reshape(-1)
    order = jnp.argsort(expert_flat)
    e_sorted = expert_flat[order]
    tok_sorted = token_flat[order]
    gate_sorted = gate_flat[order]
    counts = jnp.bincount(expert_flat, length=N_EXPERTS)
    padded = ((counts + EXPERT_BLOCK - 1) // EXPERT_BLOCK) * EXPERT_BLOCK
    starts = jnp.cumsum(counts) - counts
    pends = jnp.cumsum(padded)
    pstarts = pends - padded
    dest = pstarts[e_sorted] + jnp.arange(A) - starts[e_sorted]
    n_blocks = -(-A // EXPERT_BLOCK) + N_EXPERTS
    P = n_blocks * EXPERT_BLOCK
    x_buf = jnp.zeros((P, D), t.dtype).at[dest].set(t[tok_sorted])
    block_expert = jnp.clip(jnp.searchsorted(pends, jnp.arange(n_blocks) * EXPERT_BLOCK, side='right'),
                            0, N_EXPERTS - 1)

    def expert_block(args):
        xb, e = args
        hm = xb @ w1[e] + b1[e]
        glu = jnp.minimum(hm[:, :D_FF], SWIGLU_LIMIT)
        lin = jnp.clip(hm[:, D_FF:], -SWIGLU_LIMIT, SWIGLU_LIMIT)
        act = glu * jax.nn.sigmoid(SWIGLU_ALPHA * glu) * (lin + 1.0)
        return act @ w2[e] + b2[e]

    y_buf = lax.map(expert_block, (x_buf.reshape(n_blocks, EXPERT_BLOCK, D), block_expert)).reshape(P, D)
    out = jnp.zeros((N, D), t.dtype).at[tok_sorted].add(gate_sorted[:, None] * y_buf[dest])
    return out.reshape(B, S, D)


def setup_inputs(seed: int = 0) -> dict:
    key = jax.random.key(seed)
    ks = jax.random.split(key, 32)
    L = DEPTH
    f32 = jnp.float32

    def nrm(k, shape, scale):
        return jax.random.normal(k, shape, f32) * scale

    def gain(k, shape):
        return 1.0 + 0.01 * jax.random.normal(k, shape, f32)

    positions = (jnp.arange(SEQ, dtype=jnp.int32)[None, :]
                 + jax.random.randint(ks[2], (BATCH, 1), 0, 4096, dtype=jnp.int32))
    return {
        'x': jax.random.normal(ks[0], (BATCH, SEQ, D_MODEL), f32),
        'mem': jax.random.normal(ks[1], (BATCH, MEM_LEN, D_MODEL), f32),
        'positions': positions,
        'g_mix_norm': gain(ks[3], (L, D_MODEL)),
        'w_in': nrm(ks[4], (L, D_MODEL, D_IN_PROJ), D_MODEL ** -0.5),
        'g_dq': gain(ks[5], (L, DIFF_HEAD_DIM)),
        'g_dk': gain(ks[6], (L, DIFF_HEAD_DIM)),
        'lambda_q1': nrm(ks[7], (L, DIFF_HEAD_DIM), 0.1),
        'lambda_k1': nrm(ks[8], (L, DIFF_HEAD_DIM), 0.1),
        'lambda_q2': nrm(ks[9], (L, DIFF_HEAD_DIM), 0.1),
        'lambda_k2': nrm(ks[10], (L, DIFF_HEAD_DIM), 0.1),
        'g_subln': gain(ks[11], (L, DIFF_V_DIM)),
        'g_sgu': gain(ks[12], (L, GMLP_WIDTH)),
        'w_spatial': nrm(ks[13], (L, GMLP_GROUPS, CHUNK, CHUNK), CHUNK ** -0.5),
        'b_spatial': 1.0 + nrm(ks[14], (L, GMLP_GROUPS, CHUNK), 0.1),
        'g_mem_norm': gain(ks[15], (L, D_MODEL)),
        'w_mem_kv': nrm(ks[16], (L, D_MODEL, 2 * MEM_WIDTH), D_MODEL ** -0.5),
        'g_cq': gain(ks[17], (L, MEM_HEAD_DIM)),
        'g_ck': gain(ks[18], (L, MEM_HEAD_DIM)),
        'w_out': nrm(ks[19], (L, D_MIX, D_MODEL), D_MIX ** -0.5),
        'g_ffn_norm': gain(ks[20], (L, D_MODEL)),
        'w_router': nrm(ks[21], (L, D_MODEL, N_EXPERTS), D_MODEL ** -0.5),
        'b_router': nrm(ks[22], (L, N_EXPERTS), 0.01),
        'w_mlp1': nrm(ks[23], (L, N_EXPERTS, D_MODEL, 2 * D_FF), D_MODEL ** -0.5),
        'b_mlp1': nrm(ks[24], (L, N_EXPERTS, 2 * D_FF), 0.01),
        'w_mlp2': nrm(ks[25], (L, N_EXPERTS, D_FF, D_MODEL), D_FF ** -0.5),
        'b_mlp2': nrm(ks[26], (L, N_EXPERTS, D_MODEL), 0.01),
    }


def reference(x, mem, positions, g_mix_norm, w_in, g_dq, g_dk, lambda_q1, lambda_k1, lambda_q2,
              lambda_k2, g_subln, g_sgu, w_spatial, b_spatial, g_mem_norm, w_mem_kv, g_cq, g_ck,
              w_out, g_ffn_norm, w_router, b_router, w_mlp1, b_mlp1, w_mlp2, b_mlp2):
    cos, sin = rope_tables(positions)
    for l in range(DEPTH):
        lambda_init = 0.8 - 0.6 * math.exp(-0.3 * l)
        h = rms_norm(x, g_mix_norm[l])
        proj = h @ w_in[l]
        pq, pk, pv, pg, pc = jnp.split(proj, SPLITS, axis=-1)
        a = diff_attention(pq, pk, pv, cos, sin, g_dq[l], g_dk[l], lambda_q1[l], lambda_k1[l],
                           lambda_q2[l], lambda_k2[l], g_subln[l], lambda_init)
        g = spatial_gating(pg, g_sgu[l], w_spatial[l], b_spatial[l])
        c = memory_cross_attention(pc, mem, g_mem_norm[l], w_mem_kv[l], g_cq[l], g_ck[l])
        x = x + jnp.concatenate([a, g, c], axis=-1) @ w_out[l]
        x = x + moe_ffn(rms_norm(x, g_ffn_norm[l]), w_router[l], b_router[l],
                        w_mlp1[l], b_mlp1[l], w_mlp2[l], b_mlp2[l])
    return x
```

```python
import functools

import jax
import jax.numpy as jnp
from jax import lax
from jax.experimental import pallas as pl
from jax.experimental.pallas import tpu as pltpu
from jax.experimental.pallas import tpu_sc as plsc

F32 = jnp.float32
BF16 = jnp.bfloat16
I32 = jnp.int32
U32 = jnp.uint32

D_MODEL = 1024
N_DIFF_HEADS = 4
DIFF_HEAD_DIM = 64
DIFF_V_DIM = 128
DIFF_W = 512
GMLP_W = 256
GMLP_GROUPS = 4
CHUNK = 128
MEM_W = 256
N_MEM_HEADS = 4
HEAD_GROUP = 64
D_IN_PROJ = 2304
ROPE_THETA = 500000.0
ROT_DIM = 16
N_EXPERTS = 32
TOP_K = 4
D_FF = 1024
SWIGLU_LIMIT = 7.0
SWIGLU_ALPHA = 1.702
EPS = 1e-6
LAMBDA_INIT = 0.8 - 0.6

LANES = 128
ROW_BLOCK = 256
TOKEN_TILE = 512
ATTN_TILE = 256
VMEM_LIMIT = 56 * 1024 * 1024
NEG_BIG = -1e30


def _cparams(sem):
    return pltpu.CompilerParams(dimension_semantics=sem, vmem_limit_bytes=VMEM_LIMIT)


def _dot(a, b):
    return jnp.dot(a, b, preferred_element_type=F32)


def _dot_nt(a, b):
    return lax.dot_general(a, b, (((1,), (1,)), ((), ())), preferred_element_type=F32)


def _rms(x, gain):
    ms = jnp.mean(x * x, axis=-1, keepdims=True)
    return x * lax.rsqrt(ms + EPS) * gain


def _group_rms(t, ones_bd, gain):
    ss = _dot((t * t).astype(BF16), ones_bd)
    return t * lax.rsqrt(ss * (1.0 / HEAD_GROUP) + EPS) * gain


def _pack_bf16_pairs(v):
    w = v.shape[1] // 2
    bits = lax.bitcast_convert_type(v.astype(BF16).astype(F32), U32)
    return (bits[:, :w] & jnp.uint32(0xFFFF0000)) | (bits[:, w:] >> jnp.uint32(16))


def _unpack_bf16_pairs(words):
    hi = lax.bitcast_convert_type(words & jnp.uint32(0xFFFF0000), F32)
    lo = lax.bitcast_convert_type(words << jnp.uint32(16), F32)
    return hi, lo


def _mem_kv_kernel(mem_ref, gmem_ref, wkv_ref, gck_ref, ones_ref, kT_ref, v_ref):
    m = _rms(mem_ref[0], gmem_ref[...]).astype(BF16)
    kv = _dot(m, wkv_ref[...])
    k = _group_rms(kv[:, :MEM_W], ones_ref[...], gck_ref[...])
    kT_ref[0] = k.T.astype(BF16)
    v_ref[0] = kv[:, MEM_W:].astype(BF16)


def _gelu_tanh(x):
    return 0.5 * x * (1.0 + jnp.tanh(0.7978845608028654 * (x + 0.044715 * (x * x * x))))


def _mixer_in_kernel(x_ref, cos_ref, sin_ref, gmix_ref, win_ref, gq_ref, gk_ref, ones512_ref,
                     gsgu_ref, wsp_ref, bsp_ref, gcq_ref, ones256_ref, kT_ref, vm_ref,
                     q_out, k_out, v_out, gc_out):
    tm = x_ref.shape[0]
    hb = _rms(x_ref[...], gmix_ref[...]).astype(BF16)

    def proj(lo, hi):
        return _dot(hb, win_ref[:, lo:hi])

    lane = lax.broadcasted_iota(I32, (tm, LANES), 1)
    first_half = (lane % HEAD_GROUP) < (ROT_DIM // 2)
    cosb = cos_ref[...]
    sinb = sin_ref[...]

    def norm_rope(t, gain, out_ref):
        tn = _group_rms(t, ones512_ref[...], gain)
        for j in range(DIFF_W // LANES):
            c = tn[:, j * LANES:(j + 1) * LANES]
            partner = jnp.where(first_half, pltpu.roll(c, LANES - ROT_DIM // 2, 1), pltpu.roll(c, ROT_DIM // 2, 1))
            out_ref[:, j * LANES:(j + 1) * LANES] = (c * cosb + partner * sinb).astype(BF16)

    norm_rope(proj(0, DIFF_W), gq_ref[...] * (DIFF_HEAD_DIM ** -0.5), q_out)
    norm_rope(proj(DIFF_W, 2 * DIFF_W), gk_ref[...], k_out)
    v_out[...] = proj(2 * DIFF_W, 3 * DIFF_W).astype(BF16)

    z = _gelu_tanh(proj(3 * DIFF_W, 3 * DIFF_W + 2 * GMLP_W))
    u = z[:, :GMLP_W]
    vg = z[:, GMLP_W:]
    vc = vg - jnp.mean(vg, axis=-1, keepdims=True)
    vgn = (vc * lax.rsqrt(jnp.mean(vc * vc, axis=-1, keepdims=True) + EPS) * gsgu_ref[...]).astype(BF16)
    wrow = lax.broadcasted_iota(I32, (CHUNK, GMLP_GROUPS * CHUNK), 0)
    wcol = lax.broadcasted_iota(I32, (CHUNK, GMLP_GROUPS * CHUNK), 1) % CHUNK
    w_causal = jnp.where(wcol <= wrow, wsp_ref[...], 0.0).astype(BF16)
    grp = lax.broadcasted_iota(I32, (CHUNK, GMLP_W), 1) // HEAD_GROUP
    zero_b = jnp.zeros((CHUNK, GMLP_W), BF16)
    for r in range(tm // CHUNK):
        vchunk = vgn[r * CHUNK:(r + 1) * CHUNK, :]
        v_bd = jnp.concatenate([jnp.where(grp == g, vchunk, zero_b) for g in range(GMLP_GROUPS)], axis=0)
        mixed = _dot(w_causal, v_bd) + bsp_ref[...]
        gc_out[r * CHUNK:(r + 1) * CHUNK, 0:GMLP_W] = (u[r * CHUNK:(r + 1) * CHUNK, :] * mixed).astype(BF16)

    pc = proj(3 * DIFF_W + 2 * GMLP_W, D_IN_PROJ)
    qc = _group_rms(pc, ones256_ref[...], gcq_ref[...] * (HEAD_GROUP ** -0.5)).astype(BF16)
    hgrp = lax.broadcasted_iota(I32, (tm, MEM_W), 1) // HEAD_GROUP
    zero_q = jnp.zeros((tm, MEM_W), BF16)
    q_st = jnp.concatenate([jnp.where(hgrp == h, qc, zero_q) for h in range(N_MEM_HEADS)], axis=0)
    s = _dot(q_st, kT_ref[0])
    p = jnp.exp(s - jnp.max(s, axis=-1, keepdims=True))
    o = _dot(p.astype(BF16), vm_ref[0]) / jnp.sum(p, axis=-1, keepdims=True)
    c = jnp.zeros((tm, MEM_W), F32)
    for h in range(N_MEM_HEADS):
        c = c + jnp.where(hgrp == h, o[h * tm:(h + 1) * tm, :], 0.0)
    gc_out[:, GMLP_W:GMLP_W + MEM_W] = c.astype(BF16)


def _diff_attn_kernel(q_ref, k_ref, v_ref, lq1_ref, lk1_ref, lq2_ref, lk2_ref, gsub_ref, o_ref):
    tq = q_ref.shape[0]
    i = pl.program_id(2)
    q = q_ref[...]
    lane = lax.broadcasted_iota(I32, (tq, LANES), 1)
    zero = jnp.zeros_like(q)
    qs = jnp.concatenate([jnp.where(lane < DIFF_HEAD_DIM, q, zero), jnp.where(lane >= DIFF_HEAD_DIM, q, zero)], axis=0)

    def step(j, carry, causal):
        m, l, acc = carry
        kt = k_ref[pl.ds(pl.multiple_of(j * tq, tq), tq), :]
        vt = v_ref[pl.ds(pl.multiple_of(j * tq, tq), tq), :]
        s = _dot_nt(qs, kt)
        if causal:
            row = lax.broadcasted_iota(I32, (2 * tq, tq), 0) % tq
            col = lax.broadcasted_iota(I32, (2 * tq, tq), 1)
            s = jnp.where(col <= row, s, NEG_BIG)
        m_new = jnp.maximum(m, jnp.max(s, axis=-1, keepdims=True))
        alpha = jnp.exp(m - m_new)
        p = jnp.exp(s - m_new)
        l = alpha * l + jnp.sum(p, axis=-1, keepdims=True)
        acc = alpha * acc + _dot(p.astype(BF16), vt)
        return m_new, l, acc

    init = (jnp.full((2 * tq, 1), NEG_BIG, F32), jnp.zeros((2 * tq, 1), F32), jnp.zeros((2 * tq, DIFF_V_DIM), F32))
    carry = lax.fori_loop(0, i, lambda j, c: step(j, c, False), init)
    _, l, acc = step(i, carry, True)
    on = acc / l
    lam = (jnp.exp(jnp.sum(lq1_ref[...] * lk1_ref[...], axis=-1, keepdims=True))
           - jnp.exp(jnp.sum(lq2_ref[...] * lk2_ref[...], axis=-1, keepdims=True)) + LAMBDA_INIT)
    o = on[:tq] - lam * on[tq:]
    o_ref[...] = (_rms(o, gsub_ref[...]) * (1.0 - LAMBDA_INIT)).astype(BF16)


def _out_router_kernel(x_ref, a_ref, gc_ref, wo_ref, gffn_ref, wrh_ref, wrl_ref, br_ref,
                       x1_out, hpa_out, hpb_out, idx_out, gate_out, pos_out, cnt_out, carry_ref):
    tm = x_ref.shape[0]

    @pl.when(pl.program_id(0) == 0)
    def _():
        carry_ref[...] = jnp.zeros_like(carry_ref)

    x1 = x_ref[...] + _dot(a_ref[...], wo_ref[0:DIFF_W, :]) + _dot(gc_ref[...], wo_ref[DIFF_W:, :])
    x1_out[...] = x1
    h2 = _rms(x1, gffn_ref[...])
    hb = h2.astype(BF16)
    hpa_out[...] = _pack_bf16_pairs(h2[:, :D_MODEL // 2])
    hpb_out[...] = _pack_bf16_pairs(h2[:, D_MODEL // 2:])

    h_lo = (h2 - hb.astype(F32)).astype(BF16)
    logits = (_dot_nt(wrh_ref[...], hb) + _dot_nt(wrh_ref[...], h_lo) + _dot_nt(wrl_ref[...], hb)) + br_ref[...]

    eio = lax.broadcasted_iota(I32, (N_EXPERTS, tm), 0)
    vals, idxs, sels = [], [], []
    cur = logits
    for _ in range(TOP_K):
        m = jnp.max(cur, axis=0, keepdims=True)
        ik = jnp.min(jnp.where(cur == m, eio, N_EXPERTS), axis=0, keepdims=True)
        sel = eio == ik
        cur = jnp.where(sel, -jnp.inf, cur)
        vals.append(m)
        idxs.append(ik)
        sels.append(sel)
    es = [jnp.exp(v - vals[0]) for v in vals]
    tot = es[0] + es[1] + es[2] + es[3]
    gate_out[...] = jnp.concatenate([e / tot for e in es], axis=0)
    idx_out[...] = jnp.concatenate(idxs, axis=0)

    cnt = jnp.zeros((N_EXPERTS, tm), F32)
    for sel in sels:
        cnt = cnt + jnp.where(sel, 1.0, 0.0)
    before = lax.broadcasted_iota(I32, (tm, tm), 0) < lax.broadcasted_iota(I32, (tm, tm), 1)
    prefix = _dot(cnt.astype(BF16), jnp.where(before, 1.0, 0.0).astype(BF16))
    base = carry_ref[...] + prefix
    pos_out[...] = jnp.concatenate(
        [jnp.sum(jnp.where(sel, base, 0.0), axis=0, keepdims=True) for sel in sels], axis=0).astype(I32)
    carry_ref[...] = carry_ref[...] + jnp.sum(cnt, axis=1, keepdims=True)
    cnt_out[...] = carry_ref[...]


def _dest_kernel(cnt_ref, idx_ref, pos_ref, dest_out, be_out, nbu_out):
    idx = idx_ref[...]
    dest = pos_ref[...]
    bidx = lax.broadcasted_iota(I32, be_out.shape, 1)
    be = jnp.zeros(be_out.shape, I32)
    run = jnp.int32(0)
    for e in range(N_EXPERTS):
        dest = dest + jnp.where(idx == e, run, 0)
        run = run + ((cnt_ref[e] + (ROW_BLOCK - 1)) // ROW_BLOCK) * ROW_BLOCK
        be = be + jnp.where(bidx >= run // ROW_BLOCK, 1, 0)
    dest_out[...] = dest
    be_out[...] = jnp.minimum(be, N_EXPERTS - 1)
    nbu_out[...] = jnp.zeros(nbu_out.shape, I32) + run // ROW_BLOCK


def _expert_ffn_kernel(be_ref, nbu_ref, xa_ref, xb_ref, w1_ref, b1_ref, w2_ref, b2_ref, ya_ref, yb_ref,
                       w1s_ref, w2s_ref):
    b = pl.program_id(0)

    @pl.when(b < nbu_ref[0])
    def _():
        prev = be_ref[jnp.maximum(b - 1, 0)]

        @pl.when(jnp.logical_or(b == 0, be_ref[b] != prev))
        def _():
            w1s_ref[...] = w1_ref[0].astype(BF16)
            w2s_ref[...] = w2_ref[0].astype(BF16)

        q = D_MODEL // 4
        parts = _unpack_bf16_pairs(xa_ref[...]) + _unpack_bf16_pairs(xb_ref[...])
        hm = b1_ref[0]
        for j, part in enumerate(parts):
            hm = hm + _dot(part.astype(BF16), w1s_ref[j * q:(j + 1) * q, :])
        glu = jnp.minimum(hm[:, :D_FF], SWIGLU_LIMIT)
        lin = jnp.clip(hm[:, D_FF:], -SWIGLU_LIMIT, SWIGLU_LIMIT)
        act = glu * jax.nn.sigmoid(SWIGLU_ALPHA * glu) * (lin + 1.0)
        y = _dot(act.astype(BF16), w2s_ref[...]) + b2_ref[0]
        ya_ref[...] = _pack_bf16_pairs(y[:, :D_MODEL // 2])
        yb_ref[...] = _pack_bf16_pairs(y[:, D_MODEL // 2:])


SC_WINDOW = 128


def _sc_mesh():
    return plsc.VectorSubcoreMesh(core_axis_name="c", subcore_axis_name="s")


def _sc_gather_rows(table, idx_row):
    n = idx_row.shape[1]
    width = table.shape[1]

    @functools.partial(pl.kernel, out_type=jax.ShapeDtypeStruct((n, width), table.dtype), mesh=_sc_mesh(),
                       scratch_types=[])
    def gather_kernel(t_hbm, i_hbm, o_hbm):
        def body(i_vmem, o_vmem):
            pltpu.sync_copy(t_hbm.at[i_vmem.at[0]], o_vmem)

        pltpu.emit_pipeline(
            body,
            grid=(n // SC_WINDOW,),
            in_specs=[pl.BlockSpec((1, SC_WINDOW), lambda i: (0, i))],
            out_specs=[pl.BlockSpec((SC_WINDOW, width), lambda i: (i, 0))],
            core_axis_name=("c", "s"),
            dimension_semantics=(pltpu.PARALLEL,),
        )(i_hbm, o_hbm)

    return gather_kernel(table, idx_row)


def _sc_scatter_rows(rows, idx_rows, n_out):
    n, width = rows.shape

    @functools.partial(pl.kernel, out_type=jax.ShapeDtypeStruct((n_out, width), rows.dtype), mesh=_sc_mesh(),
                       scratch_types=[])
    def scatter_kernel(r_hbm, *refs):
        i_hbms, o_hbm = refs[:-1], refs[-1]

        def body(r_vmem, *i_vmems):
            for i_vmem in i_vmems:
                pltpu.sync_copy(r_vmem, o_hbm.at[i_vmem.at[0]])

        pltpu.emit_pipeline(
            body,
            grid=(n // SC_WINDOW,),
            in_specs=[pl.BlockSpec((SC_WINDOW, width), lambda i: (i, 0))]
            + [pl.BlockSpec((1, SC_WINDOW), lambda i: (0, i)) for _ in i_hbms],
            out_specs=[],
            core_axis_name=("c", "s"),
            dimension_semantics=(pltpu.PARALLEL,),
        )(r_hbm, *i_hbms)

    return scatter_kernel(rows, *idx_rows)


def _combine_kernel(x1_ref, yga_ref, ygb_ref, gate_ref, o_ref):
    q = D_MODEL // 4
    x1 = x1_ref[...]
    acc = [x1[:, j * q:(j + 1) * q] for j in range(4)]
    for k in range(TOP_K):
        parts = _unpack_bf16_pairs(yga_ref[k]) + _unpack_bf16_pairs(ygb_ref[k])
        g = gate_ref[:, k:k + 1]
        acc = [a + g * p for a, p in zip(acc, parts)]
    for j in range(4):
        o_ref[:, j * q:(j + 1) * q] = acc[j]


def _block_diag_ones(width):
    r = jnp.arange(width) // HEAD_GROUP
    return (r[:, None] == r[None, :]).astype(BF16)


def _rope_tables(positions):
    inv_freq = ROPE_THETA ** (-jnp.arange(0, ROT_DIM, 2, dtype=F32) / ROT_DIM)
    ang = positions.astype(F32).reshape(-1, 1) * inv_freq
    cos, sin = jnp.cos(ang), jnp.sin(ang)
    n = ang.shape[0]
    pad1 = jnp.ones((n, HEAD_GROUP - ROT_DIM), F32)
    pad0 = jnp.zeros((n, HEAD_GROUP - ROT_DIM), F32)
    cos64 = jnp.concatenate([cos, cos, pad1], axis=1)
    sin64 = jnp.concatenate([-sin, sin, pad0], axis=1)
    return jnp.tile(cos64, (1, LANES // HEAD_GROUP)), jnp.tile(sin64, (1, LANES // HEAD_GROUP))


def _full(shape):
    return pl.BlockSpec(shape, lambda *_: (0,) * len(shape))


def kernel(x, mem, positions, g_mix_norm, w_in, g_dq, g_dk, lambda_q1, lambda_k1, lambda_q2, lambda_k2, g_subln, g_sgu, w_spatial, b_spatial, g_mem_norm, w_mem_kv, g_cq, g_ck, w_out, g_ffn_norm, w_router, b_router, w_mlp1, b_mlp1, w_mlp2, b_mlp2):
    B, S, D = x.shape
    M = mem.shape[1]
    N = B * S
    tm = TOKEN_TILE
    assert D == D_MODEL and S % tm == 0 and S % ATTN_TILE == 0 and g_mix_norm.shape[0] == 1
    n_assign = N * TOP_K
    n_blocks = -(-n_assign // ROW_BLOCK) + N_EXPERTS
    n_rows = n_blocks * ROW_BLOCK
    nb_pad = -(-n_blocks // LANES) * LANES

    xf = x.reshape(N, D)
    cos_t, sin_t = _rope_tables(positions)
    ones512 = _block_diag_ones(DIFF_W)
    ones256 = _block_diag_ones(MEM_W)
    row = lambda v: v.reshape(1, -1).astype(F32)
    tile_row = lambda v, reps: jnp.tile(v.reshape(1, -1).astype(F32), (1, reps))

    kT, vm = pl.pallas_call(
        _mem_kv_kernel,
        grid=(B,),
        in_specs=[pl.BlockSpec((1, M, D), lambda b: (b, 0, 0)), _full((1, D)), _full((D, 2 * MEM_W)),
                  _full((1, MEM_W)), _full((MEM_W, MEM_W))],
        out_specs=[pl.BlockSpec((1, MEM_W, M), lambda b: (b, 0, 0)), pl.BlockSpec((1, M, MEM_W), lambda b: (b, 0, 0))],
        out_shape=[jax.ShapeDtypeStruct((B, MEM_W, M), BF16), jax.ShapeDtypeStruct((B, M, MEM_W), BF16)],
        compiler_params=_cparams(("parallel",)),
        name="mem_kv",
    )(mem, row(g_mem_norm[0]), w_mem_kv[0].astype(BF16), tile_row(g_ck[0], N_MEM_HEADS), ones256)

    w_sp_lanes = jnp.transpose(w_spatial[0], (1, 0, 2)).reshape(CHUNK, GMLP_GROUPS * CHUNK)
    b_sp_lanes = jnp.repeat(b_spatial[0].T, HEAD_GROUP, axis=1)
    tiles_per_batch = S // tm
    tok = lambda w: pl.BlockSpec((tm, w), lambda i: (i, 0))
    qn, kn, vv, gc = pl.pallas_call(
        _mixer_in_kernel,
        grid=(N // tm,),
        in_specs=[tok(D), tok(LANES), tok(LANES), _full((1, D)), _full((D, D_IN_PROJ)),
                  _full((1, DIFF_W)), _full((1, DIFF_W)), _full((DIFF_W, DIFF_W)),
                  _full((1, GMLP_W)), _full((CHUNK, GMLP_GROUPS * CHUNK)), _full((CHUNK, GMLP_W)),
                  _full((1, MEM_W)), _full((MEM_W, MEM_W)),
                  pl.BlockSpec((1, MEM_W, M), lambda i: (i // tiles_per_batch, 0, 0)),
                  pl.BlockSpec((1, M, MEM_W), lambda i: (i // tiles_per_batch, 0, 0))],
        out_specs=[tok(DIFF_W), tok(DIFF_W), tok(DIFF_W), tok(GMLP_W + MEM_W)],
        out_shape=[jax.ShapeDtypeStruct((N, DIFF_W), BF16)] * 3 + [jax.ShapeDtypeStruct((N, GMLP_W + MEM_W), BF16)],
        compiler_params=_cparams(("parallel",)),
        name="mixer_in",
    )(xf, cos_t, sin_t, row(g_mix_norm[0]), w_in[0].astype(BF16),
      tile_row(g_dq[0], 2 * N_DIFF_HEADS), tile_row(g_dk[0], 2 * N_DIFF_HEADS), ones512,
      row(g_sgu[0]), w_sp_lanes, b_sp_lanes, tile_row(g_cq[0], N_MEM_HEADS), ones256, kT, vm)

    tq = ATTN_TILE
    nq = S // tq
    head_q = pl.BlockSpec((tq, DIFF_V_DIM), lambda b, h, i: (b * nq + i, h))
    head_kv = pl.BlockSpec((S, DIFF_V_DIM), lambda b, h, i: (b, h))
    lam_spec = pl.BlockSpec((1, DIFF_HEAD_DIM), lambda b, h, i: (0, 0))
    attn = pl.pallas_call(
        _diff_attn_kernel,
        grid=(B, N_DIFF_HEADS, nq),
        in_specs=[head_q, head_kv, head_kv, lam_spec, lam_spec, lam_spec, lam_spec,
                  pl.BlockSpec((1, DIFF_V_DIM), lambda b, h, i: (0, 0))],
        out_specs=head_q,
        out_shape=jax.ShapeDtypeStruct((N, DIFF_W), BF16),
        compiler_params=_cparams(("parallel", "parallel", "arbitrary")),
        name="diff_attn",
    )(qn, kn, vv, row(lambda_q1[0]), row(lambda_k1[0]), row(lambda_q2[0]), row(lambda_k2[0]), row(g_subln[0]))

    wr = w_router[0].T.astype(F32)
    wr_hi = wr.astype(BF16)
    wr_lo = (wr - wr_hi.astype(F32)).astype(BF16)
    tokT = lambda: pl.BlockSpec((TOP_K, tm), lambda i: (0, i))
    hw = D // 4
    x1, hpa, hpb, idxT, gateT, posT, counts = pl.pallas_call(
        _out_router_kernel,
        grid=(N // tm,),
        in_specs=[tok(D), tok(DIFF_W), tok(GMLP_W + MEM_W), _full((D, D)), _full((1, D)),
                  _full((N_EXPERTS, D)), _full((N_EXPERTS, D)), _full((N_EXPERTS, 1))],
        out_specs=[tok(D), tok(hw), tok(hw), tokT(), tokT(), tokT(), _full((N_EXPERTS, 1))],
        out_shape=[jax.ShapeDtypeStruct((N, D), F32), jax.ShapeDtypeStruct((N, hw), U32),
                   jax.ShapeDtypeStruct((N, hw), U32), jax.ShapeDtypeStruct((TOP_K, N), I32), jax.ShapeDtypeStruct((TOP_K, N), F32),
                   jax.ShapeDtypeStruct((TOP_K, N), I32), jax.ShapeDtypeStruct((N_EXPERTS, 1), F32)],
        scratch_shapes=[pltpu.VMEM((N_EXPERTS, 1), F32)],
        compiler_params=_cparams(("arbitrary",)),
        name="out_router",
    )(xf, attn, gc, w_out[0].astype(BF16), row(g_ffn_norm[0]), wr_hi, wr_lo, b_router[0].reshape(N_EXPERTS, 1).astype(F32))

    destT, block_expert, nb_used = pl.pallas_call(
        _dest_kernel,
        grid_spec=pltpu.PrefetchScalarGridSpec(
            num_scalar_prefetch=1,
            grid=(1,),
            in_specs=[pl.BlockSpec((TOP_K, N), lambda i, c: (0, 0)), pl.BlockSpec((TOP_K, N), lambda i, c: (0, 0))],
            out_specs=[pl.BlockSpec((TOP_K, N), lambda i, c: (0, 0)), pl.BlockSpec((1, nb_pad), lambda i, c: (0, 0)),
                       pl.BlockSpec((1, LANES), lambda i, c: (0, 0))],
        ),
        out_shape=[jax.ShapeDtypeStruct((TOP_K, N), I32), jax.ShapeDtypeStruct((1, nb_pad), I32),
                   jax.ShapeDtypeStruct((1, LANES), I32)],
        compiler_params=_cparams(("arbitrary",)),
        name="dest",
    )(counts.reshape(N_EXPERTS).astype(I32), idxT, posT)

    dest_rows = [destT[k].reshape(1, N) for k in range(TOP_K)]
    xa_buf = _sc_scatter_rows(hpa, dest_rows, n_rows)
    xb_buf = _sc_scatter_rows(hpb, dest_rows, n_rows)

    last = lambda b, be, nbu: jnp.minimum(b, nbu[0] - 1)
    row_blk = lambda: pl.BlockSpec((ROW_BLOCK, hw), lambda b, be, nbu: (last(b, be, nbu), 0))
    ya_buf, yb_buf = pl.pallas_call(
        _expert_ffn_kernel,
        grid_spec=pltpu.PrefetchScalarGridSpec(
            num_scalar_prefetch=2,
            grid=(n_blocks,),
            in_specs=[row_blk(), row_blk(),
                      pl.BlockSpec((1, D, 2 * D_FF), lambda b, be, nbu: (be[b], 0, 0)),
                      pl.BlockSpec((1, 1, 2 * D_FF), lambda b, be, nbu: (be[b], 0, 0)),
                      pl.BlockSpec((1, D_FF, D), lambda b, be, nbu: (be[b], 0, 0)),
                      pl.BlockSpec((1, 1, D), lambda b, be, nbu: (be[b], 0, 0))],
            out_specs=[row_blk(), row_blk()],
            scratch_shapes=[pltpu.VMEM((D, 2 * D_FF), BF16), pltpu.VMEM((D_FF, D), BF16)],
        ),
        out_shape=[jax.ShapeDtypeStruct((n_rows, hw), U32)] * 2,
        compiler_params=_cparams(("arbitrary",)),
        name="expert_ffn",
    )(block_expert[0, :n_blocks], nb_used[0, :1], xa_buf, xb_buf, w_mlp1[0],
      b_mlp1[0].reshape(N_EXPERTS, 1, 2 * D_FF), w_mlp2[0], b_mlp2[0].reshape(N_EXPERTS, 1, D))

    dest_flat = destT.reshape(1, n_assign)
    yga = _sc_gather_rows(ya_buf, dest_flat).reshape(TOP_K, N, hw)
    ygb = _sc_gather_rows(yb_buf, dest_flat).reshape(TOP_K, N, hw)

    out = pl.pallas_call(
        _combine_kernel,
        grid=(N // tm,),
        in_specs=[tok(D), pl.BlockSpec((TOP_K, tm, hw), lambda i: (0, i, 0)),
                  pl.BlockSpec((TOP_K, tm, hw), lambda i: (0, i, 0)), pl.BlockSpec((tm, TOP_K), lambda i: (i, 0))],
        out_specs=tok(D),
        out_shape=jax.ShapeDtypeStruct((N, D), F32),
        compiler_params=_cparams(("parallel",)),
        name="combine",
    )(x1, yga, ygb, gateT.T)
    return out.reshape(B, S, D)
```

```python
import functools

import jax
import jax.numpy as jnp
from jax import lax
from jax.experimental import pallas as pl
from jax.experimental.pallas import tpu as pltpu
from jax.experimental.pallas import tpu_sc as plsc

F32 = jnp.float32
BF16 = jnp.bfloat16
I32 = jnp.int32
U32 = jnp.uint32

D_MODEL = 1024
N_DIFF_HEADS = 4
DIFF_HEAD_DIM = 64
DIFF_V_DIM = 128
DIFF_W = 512
GMLP_W = 256
GMLP_GROUPS = 4
CHUNK = 128
MEM_W = 256
N_MEM_HEADS = 4
HEAD_GROUP = 64
D_IN_PROJ = 2304
ROPE_THETA = 500000.0
ROT_DIM = 16
N_EXPERTS = 32
TOP_K = 4
D_FF = 1024
SWIGLU_LIMIT = 7.0
SWIGLU_ALPHA = 1.702
EPS = 1e-6
LAMBDA_INIT = 0.8 - 0.6

LANES = 128
ROW_BLOCK = 256
TOKEN_TILE = 512
ATTN_TILE = 256
VMEM_LIMIT = 56 * 1024 * 1024
NEG_BIG = -1e30
LOG2_E = 1.4426950408889634


def _cparams(sem):
    return pltpu.CompilerParams(dimension_semantics=sem, vmem_limit_bytes=VMEM_LIMIT)


def _dot(a, b):
    return jnp.dot(a, b, preferred_element_type=F32)


def _dot_nt(a, b):
    return lax.dot_general(a, b, (((1,), (1,)), ((), ())), preferred_element_type=F32)


def _rms(x, gain):
    ms = jnp.mean(x * x, axis=-1, keepdims=True)
    return x * lax.rsqrt(ms + EPS) * gain


def _group_rms(t, ones_bd, gain):
    ss = _dot((t * t).astype(BF16), ones_bd)
    return t * lax.rsqrt(ss * (1.0 / HEAD_GROUP) + EPS) * gain


def _pack_bf16_pairs(v):
    w = v.shape[1] // 2
    bits = lax.bitcast_convert_type(v.astype(BF16).astype(F32), U32)
    return (bits[:, :w] & jnp.uint32(0xFFFF0000)) | (bits[:, w:] >> jnp.uint32(16))


def _unpack_bf16_pairs(words):
    hi = lax.bitcast_convert_type(words & jnp.uint32(0xFFFF0000), F32)
    lo = lax.bitcast_convert_type(words << jnp.uint32(16), F32)
    return hi, lo


def _mem_kv_kernel(mem_ref, gmem_ref, wkv_ref, gck_ref, ones_ref, kT_ref, v_ref):
    m = _rms(mem_ref[0], gmem_ref[...]).astype(BF16)
    kv = _dot(m, wkv_ref[...])
    k = _group_rms(kv[:, :MEM_W], ones_ref[...], gck_ref[...])
    kT_ref[0] = k.T.astype(BF16)
    v_ref[0] = kv[:, MEM_W:].astype(BF16)


def _gelu_tanh(x):
    return 0.5 * x * (1.0 + jnp.tanh(0.7978845608028654 * (x + 0.044715 * (x * x * x))))


def _mixer_in_kernel(x_ref, cos_ref, sin_ref, gmix_ref, win_ref, gq_ref, gk_ref, ones512_ref,
                     gsgu_ref, wsp_ref, bsp_ref, gcq_ref, ones256_ref, kT_ref, vm_ref,
                     q_out, k_out, v_out, gc_out):
    tm = x_ref.shape[0]
    hb = _rms(x_ref[...], gmix_ref[...]).astype(BF16)

    def proj(lo, hi):
        return _dot(hb, win_ref[:, lo:hi])

    lane = lax.broadcasted_iota(I32, (tm, LANES), 1)
    first_half = (lane % HEAD_GROUP) < (ROT_DIM // 2)
    cosb = cos_ref[...]
    sinb = sin_ref[...]

    def norm_rope(t, gain, out_ref):
        tn = _group_rms(t, ones512_ref[...], gain)
        for j in range(DIFF_W // LANES):
            c = tn[:, j * LANES:(j + 1) * LANES]
            partner = jnp.where(first_half, pltpu.roll(c, LANES - ROT_DIM // 2, 1), pltpu.roll(c, ROT_DIM // 2, 1))
            out_ref[:, j * LANES:(j + 1) * LANES] = (c * cosb + partner * sinb).astype(BF16)

    norm_rope(proj(0, DIFF_W), gq_ref[...] * (DIFF_HEAD_DIM ** -0.5 * LOG2_E), q_out)
    norm_rope(proj(DIFF_W, 2 * DIFF_W), gk_ref[...], k_out)
    v_out[...] = proj(2 * DIFF_W, 3 * DIFF_W).astype(BF16)

    z = _gelu_tanh(proj(3 * DIFF_W, 3 * DIFF_W + 2 * GMLP_W))
    u = z[:, :GMLP_W]
    vg = z[:, GMLP_W:]
    vc = vg - jnp.mean(vg, axis=-1, keepdims=True)
    vgn = (vc * lax.rsqrt(jnp.mean(vc * vc, axis=-1, keepdims=True) + EPS) * gsgu_ref[...]).astype(BF16)
    wrow = lax.broadcasted_iota(I32, (CHUNK, GMLP_GROUPS * CHUNK), 0)
    wcol = lax.broadcasted_iota(I32, (CHUNK, GMLP_GROUPS * CHUNK), 1) % CHUNK
    w_causal = jnp.where(wcol <= wrow, wsp_ref[...], 0.0).astype(BF16)
    grp = lax.broadcasted_iota(I32, (CHUNK, GMLP_W), 1) // HEAD_GROUP
    zero_b = jnp.zeros((CHUNK, GMLP_W), BF16)
    for r in range(tm // CHUNK):
        vchunk = vgn[r * CHUNK:(r + 1) * CHUNK, :]
        v_bd = jnp.concatenate([jnp.where(grp == g, vchunk, zero_b) for g in range(GMLP_GROUPS)], axis=0)
        mixed = _dot(w_causal, v_bd) + bsp_ref[...]
        gc_out[r * CHUNK:(r + 1) * CHUNK, 0:GMLP_W] = (u[r * CHUNK:(r + 1) * CHUNK, :] * mixed).astype(BF16)

    pc = proj(3 * DIFF_W + 2 * GMLP_W, D_IN_PROJ)
    qc = _group_rms(pc, ones256_ref[...], gcq_ref[...] * (HEAD_GROUP ** -0.5)).astype(BF16)
    hgrp = lax.broadcasted_iota(I32, (tm, MEM_W), 1) // HEAD_GROUP
    zero_q = jnp.zeros((tm, MEM_W), BF16)
    q_st = jnp.concatenate([jnp.where(hgrp == h, qc, zero_q) for h in range(N_MEM_HEADS)], axis=0)
    s = _dot(q_st, kT_ref[0])
    p = jnp.exp(s - jnp.max(s, axis=-1, keepdims=True))
    o = _dot(p.astype(BF16), vm_ref[0]) / jnp.sum(p, axis=-1, keepdims=True)
    c = jnp.zeros((tm, MEM_W), F32)
    for h in range(N_MEM_HEADS):
        c = c + jnp.where(hgrp == h, o[h * tm:(h + 1) * tm, :], 0.0)
    gc_out[:, GMLP_W:GMLP_W + MEM_W] = c.astype(BF16)


def _diff_attn_kernel(q_ref, k_ref, v_ref, lq1_ref, lk1_ref, lq2_ref, lk2_ref, gsub_ref, o_ref,
                      s0_ref, s1_ref, m_ref, l_ref, acc_ref):
    tq = q_ref.shape[0]
    i = pl.program_id(2)
    q = q_ref[...]
    lane = lax.broadcasted_iota(I32, (tq, LANES), 1)
    zero = jnp.zeros_like(q)
    qs = jnp.concatenate([jnp.where(lane < DIFF_HEAD_DIM, q, zero), jnp.where(lane >= DIFF_HEAD_DIM, q, zero)], axis=0)

    def scores(t, s_ref):
        kt = k_ref[pl.ds(pl.multiple_of(t * tq, tq), tq), :]
        s_ref[...] = _dot_nt(kt, qs)

    def update(t, s_ref, causal):
        s = s_ref[...]
        if causal:
            key = lax.broadcasted_iota(I32, (tq, 2 * tq), 0)
            qry = lax.broadcasted_iota(I32, (tq, 2 * tq), 1) % tq
            s = jnp.where(key <= qry, s, NEG_BIG)
        m = m_ref[...]
        m_new = jnp.maximum(m, jnp.max(s, axis=0, keepdims=True))
        alpha = jnp.exp2(m - m_new)
        p = jnp.exp2(s - m_new)
        m_ref[...] = m_new
        l_ref[...] = alpha * l_ref[...] + jnp.sum(p, axis=0, keepdims=True)
        vt = v_ref[pl.ds(pl.multiple_of(t * tq, tq), tq), :]
        pv = lax.dot_general(vt, p.astype(BF16), (((0,), (0,)), ((), ())), preferred_element_type=F32)
        acc_ref[...] = alpha * acc_ref[...] + pv

    m_ref[...] = jnp.full(m_ref.shape, NEG_BIG, F32)
    l_ref[...] = jnp.zeros(l_ref.shape, F32)
    acc_ref[...] = jnp.zeros(acc_ref.shape, F32)
    scores(0, s0_ref)

    def pair(pidx, carry):
        t = 2 * pidx
        scores(t + 1, s1_ref)
        update(t, s0_ref, False)
        scores(t + 2, s0_ref)
        update(t + 1, s1_ref, False)
        return carry

    lax.fori_loop(0, i // 2, pair, 0)

    @pl.when(i % 2 == 0)
    def _():
        update(i, s0_ref, True)

    @pl.when(i % 2 == 1)
    def _():
        scores(i, s1_ref)
        update(i - 1, s0_ref, False)
        update(i, s1_ref, True)

    on = acc_ref[...] / l_ref[...]
    lam = (jnp.exp(jnp.sum(lq1_ref[...] * lk1_ref[...], axis=-1, keepdims=True))
           - jnp.exp(jnp.sum(lq2_ref[...] * lk2_ref[...], axis=-1, keepdims=True)) + LAMBDA_INIT)
    o = on[:, :tq] - lam * on[:, tq:]
    ms = jnp.mean(o * o, axis=0, keepdims=True)
    o = o * lax.rsqrt(ms + EPS) * gsub_ref[...] * (1.0 - LAMBDA_INIT)
    o_ref[...] = o.T.astype(BF16)


def _out_router_kernel(x_ref, a_ref, gc_ref, wo_ref, gffn_ref, wrh_ref, wrl_ref, br_ref,
                       x1_out, hpa_out, hpb_out, idx_out, gate_out, pos_out, cnt_out, carry_ref):
    tm = x_ref.shape[0]

    @pl.when(pl.program_id(0) == 0)
    def _():
        carry_ref[...] = jnp.zeros_like(carry_ref)

    x1 = x_ref[...] + _dot(a_ref[...], wo_ref[0:DIFF_W, :]) + _dot(gc_ref[...], wo_ref[DIFF_W:, :])
    x1_out[...] = x1
    h2 = _rms(x1, gffn_ref[...])
    hb = h2.astype(BF16)
    hpa_out[...] = _pack_bf16_pairs(h2[:, :D_MODEL // 2])
    hpb_out[...] = _pack_bf16_pairs(h2[:, D_MODEL // 2:])

    h_lo = (h2 - hb.astype(F32)).astype(BF16)
    logits = (_dot_nt(wrh_ref[...], hb) + _dot_nt(wrh_ref[...], h_lo) + _dot_nt(wrl_ref[...], hb)) + br_ref[...]

    eio = lax.broadcasted_iota(I32, (N_EXPERTS, tm), 0)
    vals, idxs, sels = [], [], []
    cur = logits
    for _ in range(TOP_K):
        m = jnp.max(cur, axis=0, keepdims=True)
        ik = jnp.min(jnp.where(cur == m, eio, N_EXPERTS), axis=0, keepdims=True)
        sel = eio == ik
        cur = jnp.where(sel, -jnp.inf, cur)
        vals.append(m)
        idxs.append(ik)
        sels.append(sel)
    es = [jnp.exp(v - vals[0]) for v in vals]
    tot = es[0] + es[1] + es[2] + es[3]
    gate_out[...] = jnp.concatenate([e / tot for e in es], axis=0)
    idx_out[...] = jnp.concatenate(idxs, axis=0)

    cnt = jnp.zeros((N_EXPERTS, tm), F32)
    for sel in sels:
        cnt = cnt + jnp.where(sel, 1.0, 0.0)
    before = lax.broadcasted_iota(I32, (tm, tm), 0) < lax.broadcasted_iota(I32, (tm, tm), 1)
    prefix = _dot(cnt.astype(BF16), jnp.where(before, 1.0, 0.0).astype(BF16))
    base = carry_ref[...] + prefix
    pos_out[...] = jnp.concatenate(
        [jnp.sum(jnp.where(sel, base, 0.0), axis=0, keepdims=True) for sel in sels], axis=0).astype(I32)
    carry_ref[...] = carry_ref[...] + jnp.sum(cnt, axis=1, keepdims=True)
    cnt_out[...] = carry_ref[...]


def _dest_kernel(cnt_ref, idx_ref, pos_ref, dest_out, be_out, nbu_out):
    idx = idx_ref[...]
    dest = pos_ref[...]
    bidx = lax.broadcasted_iota(I32, be_out.shape, 1)
    be = jnp.zeros(be_out.shape, I32)
    run = jnp.int32(0)
    for e in range(N_EXPERTS):
        dest = dest + jnp.where(idx == e, run, 0)
        run = run + ((cnt_ref[e] + (ROW_BLOCK - 1)) // ROW_BLOCK) * ROW_BLOCK
        be = be + jnp.where(bidx >= run // ROW_BLOCK, 1, 0)
    dest_out[...] = dest
    be_out[...] = jnp.minimum(be, N_EXPERTS - 1)
    nbu_out[...] = jnp.zeros(nbu_out.shape, I32) + run // ROW_BLOCK


def _expert_ffn_kernel(be_ref, nbu_ref, xa_ref, xb_ref, w1_ref, b1_ref, w2_ref, b2_ref, ya_ref, yb_ref,
                       w1s_ref, w2s_ref):
    b = pl.program_id(0)

    @pl.when(b < nbu_ref[0])
    def _():
        prev = be_ref[jnp.maximum(b - 1, 0)]

        @pl.when(jnp.logical_or(b == 0, be_ref[b] != prev))
        def _():
            w1s_ref[...] = w1_ref[0].astype(BF16)
            w2s_ref[...] = w2_ref[0].astype(BF16)

        q = D_MODEL // 4
        parts = _unpack_bf16_pairs(xa_ref[...]) + _unpack_bf16_pairs(xb_ref[...])
        hm = b1_ref[0]
        for j, part in enumerate(parts):
            hm = hm + _dot(part.astype(BF16), w1s_ref[j * q:(j + 1) * q, :])
        glu = jnp.minimum(hm[:, :D_FF], SWIGLU_LIMIT)
        lin = jnp.clip(hm[:, D_FF:], -SWIGLU_LIMIT, SWIGLU_LIMIT)
        act = glu * jax.nn.sigmoid(SWIGLU_ALPHA * glu) * (lin + 1.0)
        y = _dot(act.astype(BF16), w2s_ref[...]) + b2_ref[0]
        ya_ref[...] = _pack_bf16_pairs(y[:, :D_MODEL // 2])
        yb_ref[...] = _pack_bf16_pairs(y[:, D_MODEL // 2:])


SC_WINDOW = 128


def _sc_mesh():
    return plsc.VectorSubcoreMesh(core_axis_name="c", subcore_axis_name="s")


def _sc_gather_rows(table, idx_row):
    n = idx_row.shape[1]
    width = table.shape[1]

    @functools.partial(pl.kernel, out_type=jax.ShapeDtypeStruct((n, width), table.dtype), mesh=_sc_mesh(),
                       scratch_types=[])
    def gather_kernel(t_hbm, i_hbm, o_hbm):
        def body(i_vmem, o_vmem):
            pltpu.sync_copy(t_hbm.at[i_vmem.at[0]], o_vmem)

        pltpu.emit_pipeline(
            body,
            grid=(n // SC_WINDOW,),
            in_specs=[pl.BlockSpec((1, SC_WINDOW), lambda i: (0, i))],
            out_specs=[pl.BlockSpec((SC_WINDOW, width), lambda i: (i, 0))],
            core_axis_name=("c", "s"),
            dimension_semantics=(pltpu.PARALLEL,),
        )(i_hbm, o_hbm)

    return gather_kernel(table, idx_row)


def _sc_scatter_rows(rows, idx_rows, n_out):
    n, width = rows.shape

    @functools.partial(pl.kernel, out_type=jax.ShapeDtypeStruct((n_out, width), rows.dtype), mesh=_sc_mesh(),
                       scratch_types=[])
    def scatter_kernel(r_hbm, *refs):
        i_hbms, o_hbm = refs[:-1], refs[-1]

        def body(r_vmem, *i_vmems):
            for i_vmem in i_vmems:
                pltpu.sync_copy(r_vmem, o_hbm.at[i_vmem.at[0]])

        pltpu.emit_pipeline(
            body,
            grid=(n // SC_WINDOW,),
            in_specs=[pl.BlockSpec((SC_WINDOW, width), lambda i: (i, 0))]
            + [pl.BlockSpec((1, SC_WINDOW), lambda i: (0, i)) for _ in i_hbms],
            out_specs=[],
            core_axis_name=("c", "s"),
            dimension_semantics=(pltpu.PARALLEL,),
        )(r_hbm, *i_hbms)

    return scatter_kernel(rows, *idx_rows)


def _combine_kernel(x1_ref, yga_ref, ygb_ref, gate_ref, o_ref):
    q = D_MODEL // 4
    x1 = x1_ref[...]
    acc = [x1[:, j * q:(j + 1) * q] for j in range(4)]
    for k in range(TOP_K):
        parts = _unpack_bf16_pairs(yga_ref[k]) + _unpack_bf16_pairs(ygb_ref[k])
        g = gate_ref[:, k:k + 1]
        acc = [a + g * p for a, p in zip(acc, parts)]
    for j in range(4):
        o_ref[:, j * q:(j + 1) * q] = acc[j]


def _block_diag_ones(width):
    r = jnp.arange(width) // HEAD_GROUP
    return (r[:, None] == r[None, :]).astype(BF16)


def _rope_tables(positions):
    inv_freq = ROPE_THETA ** (-jnp.arange(0, ROT_DIM, 2, dtype=F32) / ROT_DIM)
    ang = positions.astype(F32).reshape(-1, 1) * inv_freq
    cos, sin = jnp.cos(ang), jnp.sin(ang)
    n = ang.shape[0]
    pad1 = jnp.ones((n, HEAD_GROUP - ROT_DIM), F32)
    pad0 = jnp.zeros((n, HEAD_GROUP - ROT_DIM), F32)
    cos64 = jnp.concatenate([cos, cos, pad1], axis=1)
    sin64 = jnp.concatenate([-sin, sin, pad0], axis=1)
    return jnp.tile(cos64, (1, LANES // HEAD_GROUP)), jnp.tile(sin64, (1, LANES // HEAD_GROUP))


def _full(shape):
    return pl.BlockSpec(shape, lambda *_: (0,) * len(shape))


def kernel(x, mem, positions, g_mix_norm, w_in, g_dq, g_dk, lambda_q1, lambda_k1, lambda_q2, lambda_k2, g_subln, g_sgu, w_spatial, b_spatial, g_mem_norm, w_mem_kv, g_cq, g_ck, w_out, g_ffn_norm, w_router, b_router, w_mlp1, b_mlp1, w_mlp2, b_mlp2):
    B, S, D = x.shape
    M = mem.shape[1]
    N = B * S
    tm = TOKEN_TILE
    assert D == D_MODEL and S % tm == 0 and S % ATTN_TILE == 0 and g_mix_norm.shape[0] == 1
    n_assign = N * TOP_K
    n_blocks = -(-n_assign // ROW_BLOCK) + N_EXPERTS
    n_rows = n_blocks * ROW_BLOCK
    nb_pad = -(-n_blocks // LANES) * LANES

    xf = x.reshape(N, D)
    cos_t, sin_t = _rope_tables(positions)
    ones512 = _block_diag_ones(DIFF_W)
    ones256 = _block_diag_ones(MEM_W)
    row = lambda v: v.reshape(1, -1).astype(F32)
    tile_row = lambda v, reps: jnp.tile(v.reshape(1, -1).astype(F32), (1, reps))

    kT, vm = pl.pallas_call(
        _mem_kv_kernel,
        grid=(B,),
        in_specs=[pl.BlockSpec((1, M, D), lambda b: (b, 0, 0)), _full((1, D)), _full((D, 2 * MEM_W)),
                  _full((1, MEM_W)), _full((MEM_W, MEM_W))],
        out_specs=[pl.BlockSpec((1, MEM_W, M), lambda b: (b, 0, 0)), pl.BlockSpec((1, M, MEM_W), lambda b: (b, 0, 0))],
        out_shape=[jax.ShapeDtypeStruct((B, MEM_W, M), BF16), jax.ShapeDtypeStruct((B, M, MEM_W), BF16)],
        compiler_params=_cparams(("parallel",)),
        name="mem_kv",
    )(mem, row(g_mem_norm[0]), w_mem_kv[0].astype(BF16), tile_row(g_ck[0], N_MEM_HEADS), ones256)

    w_sp_lanes = jnp.transpose(w_spatial[0], (1, 0, 2)).reshape(CHUNK, GMLP_GROUPS * CHUNK)
    b_sp_lanes = jnp.repeat(b_spatial[0].T, HEAD_GROUP, axis=1)
    tiles_per_batch = S // tm
    tok = lambda w: pl.BlockSpec((tm, w), lambda i: (i, 0))
    qn, kn, vv, gc = pl.pallas_call(
        _mixer_in_kernel,
        grid=(N // tm,),
        in_specs=[tok(D), tok(LANES), tok(LANES), _full((1, D)), _full((D, D_IN_PROJ)),
                  _full((1, DIFF_W)), _full((1, DIFF_W)), _full((DIFF_W, DIFF_W)),
                  _full((1, GMLP_W)), _full((CHUNK, GMLP_GROUPS * CHUNK)), _full((CHUNK, GMLP_W)),
                  _full((1, MEM_W)), _full((MEM_W, MEM_W)),
                  pl.BlockSpec((1, MEM_W, M), lambda i: (i // tiles_per_batch, 0, 0)),
                  pl.BlockSpec((1, M, MEM_W), lambda i: (i // tiles_per_batch, 0, 0))],
        out_specs=[tok(DIFF_W), tok(DIFF_W), tok(DIFF_W), tok(GMLP_W + MEM_W)],
        out_shape=[jax.ShapeDtypeStruct((N, DIFF_W), BF16)] * 3 + [jax.ShapeDtypeStruct((N, GMLP_W + MEM_W), BF16)],
        compiler_params=_cparams(("parallel",)),
        name="mixer_in",
    )(xf, cos_t, sin_t, row(g_mix_norm[0]), w_in[0].astype(BF16),
      tile_row(g_dq[0], 2 * N_DIFF_HEADS), tile_row(g_dk[0], 2 * N_DIFF_HEADS), ones512,
      row(g_sgu[0]), w_sp_lanes, b_sp_lanes, tile_row(g_cq[0], N_MEM_HEADS), ones256, kT, vm)

    tq = ATTN_TILE
    nq = S // tq
    head_q = pl.BlockSpec((tq, DIFF_V_DIM), lambda b, h, i: (b * nq + i, h))
    head_kv = pl.BlockSpec((S, DIFF_V_DIM), lambda b, h, i: (b, h))
    lam_spec = pl.BlockSpec((1, DIFF_HEAD_DIM), lambda b, h, i: (0, 0))
    attn = pl.pallas_call(
        _diff_attn_kernel,
        grid=(B, N_DIFF_HEADS, nq),
        in_specs=[head_q, head_kv, head_kv, lam_spec, lam_spec, lam_spec, lam_spec,
                  pl.BlockSpec((DIFF_V_DIM, 1), lambda b, h, i: (0, 0))],
        out_specs=head_q,
        out_shape=jax.ShapeDtypeStruct((N, DIFF_W), BF16),
        scratch_shapes=[pltpu.VMEM((tq, 2 * tq), F32), pltpu.VMEM((tq, 2 * tq), F32), pltpu.VMEM((1, 2 * tq), F32),
                        pltpu.VMEM((1, 2 * tq), F32), pltpu.VMEM((DIFF_V_DIM, 2 * tq), F32)],
        compiler_params=_cparams(("parallel", "parallel", "arbitrary")),
        name="diff_attn",
    )(qn, kn, vv, row(lambda_q1[0]), row(lambda_k1[0]), row(lambda_q2[0]), row(lambda_k2[0]),
      g_subln[0].reshape(DIFF_V_DIM, 1).astype(F32))

    wr = w_router[0].T.astype(F32)
    wr_hi = wr.astype(BF16)
    wr_lo = (wr - wr_hi.astype(F32)).astype(BF16)
    tokT = lambda: pl.BlockSpec((TOP_K, tm), lambda i: (0, i))
    hw = D // 4
    x1, hpa, hpb, idxT, gateT, posT, counts = pl.pallas_call(
        _out_router_kernel,
        grid=(N // tm,),
        in_specs=[tok(D), tok(DIFF_W), tok(GMLP_W + MEM_W), _full((D, D)), _full((1, D)),
                  _full((N_EXPERTS, D)), _full((N_EXPERTS, D)), _full((N_EXPERTS, 1))],
        out_specs=[tok(D), tok(hw), tok(hw), tokT(), tokT(), tokT(), _full((N_EXPERTS, 1))],
        out_shape=[jax.ShapeDtypeStruct((N, D), F32), jax.ShapeDtypeStruct((N, hw), U32),
                   jax.ShapeDtypeStruct((N, hw), U32), jax.ShapeDtypeStruct((TOP_K, N), I32), jax.ShapeDtypeStruct((TOP_K, N), F32),
                   jax.ShapeDtypeStruct((TOP_K, N), I32), jax.ShapeDtypeStruct((N_EXPERTS, 1), F32)],
        scratch_shapes=[pltpu.VMEM((N_EXPERTS, 1), F32)],
        compiler_params=_cparams(("arbitrary",)),
        name="out_router",
    )(xf, attn, gc, w_out[0].astype(BF16), row(g_ffn_norm[0]), wr_hi, wr_lo, b_router[0].reshape(N_EXPERTS, 1).astype(F32))

    destT, block_expert, nb_used = pl.pallas_call(
        _dest_kernel,
        grid_spec=pltpu.PrefetchScalarGridSpec(
            num_scalar_prefetch=1,
            grid=(1,),
            in_specs=[pl.BlockSpec((TOP_K, N), lambda i, c: (0, 0)), pl.BlockSpec((TOP_K, N), lambda i, c: (0, 0))],
            out_specs=[pl.BlockSpec((TOP_K, N), lambda i, c: (0, 0)), pl.BlockSpec((1, nb_pad), lambda i, c: (0, 0)),
                       pl.BlockSpec((1, LANES), lambda i, c: (0, 0))],
        ),
        out_shape=[jax.ShapeDtypeStruct((TOP_K, N), I32), jax.ShapeDtypeStruct((1, nb_pad), I32),
                   jax.ShapeDtypeStruct((1, LANES), I32)],
        compiler_params=_cparams(("arbitrary",)),
        name="dest",
    )(counts.reshape(N_EXPERTS).astype(I32), idxT, posT)

    dest_rows = [destT[k].reshape(1, N) for k in range(TOP_K)]
    xa_buf = _sc_scatter_rows(hpa, dest_rows, n_rows)
    xb_buf = _sc_scatter_rows(hpb, dest_rows, n_rows)

    last = lambda b, be, nbu: jnp.minimum(b, nbu[0] - 1)
    row_blk = lambda: pl.BlockSpec((ROW_BLOCK, hw), lambda b, be, nbu: (last(b, be, nbu), 0))
    ya_buf, yb_buf = pl.pallas_call(
        _expert_ffn_kernel,
        grid_spec=pltpu.PrefetchScalarGridSpec(
            num_scalar_prefetch=2,
            grid=(n_blocks,),
            in_specs=[row_blk(), row_blk(),
                      pl.BlockSpec((1, D, 2 * D_FF), lambda b, be, nbu: (be[b], 0, 0)),
                      pl.BlockSpec((1, 1, 2 * D_FF), lambda b, be, nbu: (be[b], 0, 0)),
                      pl.BlockSpec((1, D_FF, D), lambda b, be, nbu: (be[b], 0, 0)),
                      pl.BlockSpec((1, 1, D), lambda b, be, nbu: (be[b], 0, 0))],
            out_specs=[row_blk(), row_blk()],
            scratch_shapes=[pltpu.VMEM((D, 2 * D_FF), BF16), pltpu.VMEM((D_FF, D), BF16)],
        ),
        out_shape=[jax.ShapeDtypeStruct((n_rows, hw), U32)] * 2,
        compiler_params=_cparams(("arbitrary",)),
        name="expert_ffn",
    )(block_expert[0, :n_blocks], nb_used[0, :1], xa_buf, xb_buf, w_mlp1[0],
      b_mlp1[0].reshape(N_EXPERTS, 1, 2 * D_FF), w_mlp2[0], b_mlp2[0].reshape(N_EXPERTS, 1, D))

    dest_flat = destT.reshape(1, n_assign)
    yga = _sc_gather_rows(ya_buf, dest_flat).reshape(TOP_K, N, hw)
    ygb = _sc_gather_rows(yb_buf, dest_flat).reshape(TOP_K, N, hw)

    out = pl.pallas_call(
        _combine_kernel,
        grid=(N // tm,),
        in_specs=[tok(D), pl.BlockSpec((TOP_K, tm, hw), lambda i: (0, i, 0)),
                  pl.BlockSpec((TOP_K, tm, hw), lambda i: (0, i, 0)), pl.BlockSpec((tm, TOP_K), lambda i: (i, 0))],
        out_specs=tok(D),
        out_shape=jax.ShapeDtypeStruct((N, D), F32),
        compiler_params=_cparams(("parallel",)),
        name="combine",
    )(x1, yga, ygb, gateT.T)
    return out.reshape(B, S, D)
```

```python
import functools

import jax
import jax.numpy as jnp
from jax import lax
from jax.experimental import pallas as pl
from jax.experimental.pallas import tpu as pltpu
from jax.experimental.pallas import tpu_sc as plsc

F32 = jnp.float32
BF16 = jnp.bfloat16
I32 = jnp.int32
U32 = jnp.uint32

D_MODEL = 1024
N_DIFF_HEADS = 4
DIFF_HEAD_DIM = 64
DIFF_V_DIM = 128
DIFF_W = 512
GMLP_W = 256
GMLP_GROUPS = 4
CHUNK = 128
MEM_W = 256
N_MEM_HEADS = 4
HEAD_GROUP = 64
D_IN_PROJ = 2304
ROPE_THETA = 500000.0
ROT_DIM = 16
N_EXPERTS = 32
TOP_K = 4
D_FF = 1024
SWIGLU_LIMIT = 7.0
SWIGLU_ALPHA = 1.702
EPS = 1e-6
LAMBDA_INIT = 0.8 - 0.6

LANES = 128
ROW_BLOCK = 512
TOKEN_TILE = 512
ATTN_TILE = 256
VMEM_LIMIT = 56 * 1024 * 1024
NEG_BIG = -1e30
LOG2_E = 1.4426950408889634


def _cparams(sem):
    return pltpu.CompilerParams(dimension_semantics=sem, vmem_limit_bytes=VMEM_LIMIT)


def _dot(a, b):
    return jnp.dot(a, b, preferred_element_type=F32)


def _dot_nt(a, b):
    return lax.dot_general(a, b, (((1,), (1,)), ((), ())), preferred_element_type=F32)


def _rms(x, gain):
    ms = jnp.mean(x * x, axis=-1, keepdims=True)
    return x * lax.rsqrt(ms + EPS) * gain


def _group_rms(t, ones_bd, gain):
    w = ones_bd.shape[0]
    chunks = []
    for j in range(t.shape[1] // w):
        c = t[:, j * w:(j + 1) * w]
        ss = _dot((c * c).astype(BF16), ones_bd)
        chunks.append(c * lax.rsqrt(ss * (1.0 / HEAD_GROUP) + EPS))
    return (chunks[0] if len(chunks) == 1 else jnp.concatenate(chunks, axis=1)) * gain


def _pack_bf16_pairs(v):
    w = v.shape[1] // 2
    bits = lax.bitcast_convert_type(v.astype(BF16).astype(F32), U32)
    return (bits[:, :w] & jnp.uint32(0xFFFF0000)) | (bits[:, w:] >> jnp.uint32(16))


def _unpack_bf16_pairs(words):
    hi = lax.bitcast_convert_type(words & jnp.uint32(0xFFFF0000), F32)
    lo = lax.bitcast_convert_type(words << jnp.uint32(16), F32)
    return hi, lo


def _mem_kv_kernel(mem_ref, gmem_ref, wkv_ref, gck_ref, ones_ref, kT_ref, v_ref):
    m = _rms(mem_ref[0], gmem_ref[...]).astype(BF16)
    kv = _dot(m, wkv_ref[...])
    k = _group_rms(kv[:, :MEM_W], ones_ref[...], gck_ref[...])
    kT_ref[0] = k.T.astype(BF16)
    v_ref[0] = kv[:, MEM_W:].astype(BF16)


def _gelu_tanh(x):
    return 0.5 * x * (1.0 + jnp.tanh(0.7978845608028654 * (x + 0.044715 * (x * x * x))))


def _mixer_in_kernel(x_ref, cos_ref, sin_ref, gmix_ref, win_ref, gq_ref, gk_ref,
                     gsgu_ref, wsp_ref, bsp_ref, gcq_ref, ones256_ref, kT_ref, vm_ref,
                     q_out, k_out, v_out, gc_out):
    tm = x_ref.shape[0]
    hb = _rms(x_ref[...], gmix_ref[...]).astype(BF16)

    def proj(lo, hi):
        return _dot(hb, win_ref[:, lo:hi])

    lane = lax.broadcasted_iota(I32, (tm, LANES), 1)
    first_half = (lane % HEAD_GROUP) < (ROT_DIM // 2)
    cosb = cos_ref[...]
    sinb = sin_ref[...]

    def norm_rope(t, gain, out_ref):
        tn = _group_rms(t, ones256_ref[...], gain)
        for j in range(DIFF_W // LANES):
            c = tn[:, j * LANES:(j + 1) * LANES]
            partner = jnp.where(first_half, pltpu.roll(c, LANES - ROT_DIM // 2, 1), pltpu.roll(c, ROT_DIM // 2, 1))
            out_ref[:, j * LANES:(j + 1) * LANES] = (c * cosb + partner * sinb).astype(BF16)

    norm_rope(proj(0, DIFF_W), gq_ref[...] * (DIFF_HEAD_DIM ** -0.5 * LOG2_E), q_out)
    norm_rope(proj(DIFF_W, 2 * DIFF_W), gk_ref[...], k_out)
    v_out[...] = proj(2 * DIFF_W, 3 * DIFF_W).astype(BF16)

    z = _gelu_tanh(proj(3 * DIFF_W, 3 * DIFF_W + 2 * GMLP_W))
    u = z[:, :GMLP_W]
    vg = z[:, GMLP_W:]
    vc = vg - jnp.mean(vg, axis=-1, keepdims=True)
    vgn = (vc * lax.rsqrt(jnp.mean(vc * vc, axis=-1, keepdims=True) + EPS) * gsgu_ref[...]).astype(BF16)
    wrow = lax.broadcasted_iota(I32, (CHUNK, GMLP_GROUPS * CHUNK), 0)
    wcol = lax.broadcasted_iota(I32, (CHUNK, GMLP_GROUPS * CHUNK), 1) % CHUNK
    w_causal = jnp.where(wcol <= wrow, wsp_ref[...], 0.0).astype(BF16)
    grp = lax.broadcasted_iota(I32, (CHUNK, GMLP_W), 1) // HEAD_GROUP
    zero_b = jnp.zeros((CHUNK, GMLP_W), BF16)
    for r in range(tm // CHUNK):
        vchunk = vgn[r * CHUNK:(r + 1) * CHUNK, :]
        v_bd = jnp.concatenate([jnp.where(grp == g, vchunk, zero_b) for g in range(GMLP_GROUPS)], axis=0)
        mixed = _dot(w_causal, v_bd) + bsp_ref[...]
        gc_out[r * CHUNK:(r + 1) * CHUNK, 0:GMLP_W] = (u[r * CHUNK:(r + 1) * CHUNK, :] * mixed).astype(BF16)

    pc = proj(3 * DIFF_W + 2 * GMLP_W, D_IN_PROJ)
    qc = _group_rms(pc, ones256_ref[...], gcq_ref[...] * (HEAD_GROUP ** -0.5)).astype(BF16)
    hgrp = lax.broadcasted_iota(I32, (tm, MEM_W), 1) // HEAD_GROUP
    zero_q = jnp.zeros((tm, MEM_W), BF16)
    q_st = jnp.concatenate([jnp.where(hgrp == h, qc, zero_q) for h in range(N_MEM_HEADS)], axis=0)
    s = _dot(q_st, kT_ref[0])
    p = jnp.exp(s - jnp.max(s, axis=-1, keepdims=True))
    o = _dot(p.astype(BF16), vm_ref[0]) / jnp.sum(p, axis=-1, keepdims=True)
    c = jnp.zeros((tm, MEM_W), F32)
    for h in range(N_MEM_HEADS):
        c = c + jnp.where(hgrp == h, o[h * tm:(h + 1) * tm, :], 0.0)
    gc_out[:, GMLP_W:GMLP_W + MEM_W] = c.astype(BF16)


def _diff_attn_kernel(q_ref, k_ref, v_ref, lq1_ref, lk1_ref, lq2_ref, lk2_ref, gsub_ref, o_ref,
                      s0_ref, s1_ref, m_ref, l_ref, acc_ref):
    tq = q_ref.shape[0]
    i = pl.program_id(1)
    lane = lax.broadcasted_iota(I32, (tq, LANES), 1)
    heads = range(N_DIFF_HEADS)
    hl = lambda h: slice(h * DIFF_V_DIM, (h + 1) * DIFF_V_DIM)

    def stacked_q(h):
        q = q_ref[:, hl(h)]
        zero = jnp.zeros_like(q)
        return jnp.concatenate([jnp.where(lane < DIFF_HEAD_DIM, q, zero), jnp.where(lane >= DIFF_HEAD_DIM, q, zero)],
                               axis=0)

    qs = [stacked_q(h) for h in heads]

    def scores(t, s_ref):
        rows = pl.ds(pl.multiple_of(t * tq, tq), tq)
        for h in heads:
            s_ref[h] = _dot_nt(k_ref[rows, hl(h)], qs[h])

    def update(t, s_ref, causal):
        rows = pl.ds(pl.multiple_of(t * tq, tq), tq)
        for h in heads:
            s = s_ref[h]
            if causal:
                key = lax.broadcasted_iota(I32, (tq, 2 * tq), 0)
                qry = lax.broadcasted_iota(I32, (tq, 2 * tq), 1) % tq
                s = jnp.where(key <= qry, s, NEG_BIG)
            m = m_ref[h]
            m_new = jnp.maximum(m, jnp.max(s, axis=0, keepdims=True))
            alpha = jnp.exp2(m - m_new)
            p = jnp.exp2(s - m_new)
            m_ref[h] = m_new
            l_ref[h] = alpha * l_ref[h] + jnp.sum(p, axis=0, keepdims=True)
            pv = lax.dot_general(v_ref[rows, hl(h)], p.astype(BF16), (((0,), (0,)), ((), ())),
                                 preferred_element_type=F32)
            acc_ref[h] = alpha * acc_ref[h] + pv

    m_ref[...] = jnp.full(m_ref.shape, NEG_BIG, F32)
    l_ref[...] = jnp.zeros(l_ref.shape, F32)
    acc_ref[...] = jnp.zeros(acc_ref.shape, F32)
    scores(0, s0_ref)

    def pair(pidx, carry):
        t = 2 * pidx
        scores(t + 1, s1_ref)
        update(t, s0_ref, False)
        scores(t + 2, s0_ref)
        update(t + 1, s1_ref, False)
        return carry

    lax.fori_loop(0, i // 2, pair, 0)

    @pl.when(i % 2 == 0)
    def _():
        update(i, s0_ref, True)

    @pl.when(i % 2 == 1)
    def _():
        scores(i, s1_ref)
        update(i - 1, s0_ref, False)
        update(i, s1_ref, True)

    lam = (jnp.exp(jnp.sum(lq1_ref[...] * lk1_ref[...], axis=-1, keepdims=True))
           - jnp.exp(jnp.sum(lq2_ref[...] * lk2_ref[...], axis=-1, keepdims=True)) + LAMBDA_INIT)
    for h in heads:
        on = acc_ref[h] / l_ref[h]
        o = on[:, :tq] - lam * on[:, tq:]
        ms = jnp.mean(o * o, axis=0, keepdims=True)
        o = o * lax.rsqrt(ms + EPS) * gsub_ref[...] * (1.0 - LAMBDA_INIT)
        o_ref[:, hl(h)] = o.T.astype(BF16)


def _out_router_kernel(x_ref, a_ref, gc_ref, wo_ref, gffn_ref, wrh_ref, wrl_ref, br_ref,
                       x1_out, hpa_out, hpb_out, idx_out, gate_out, pos_out, cnt_out, carry_ref):
    tm = x_ref.shape[0]

    @pl.when(pl.program_id(0) == 0)
    def _():
        carry_ref[...] = jnp.zeros_like(carry_ref)

    x1 = x_ref[...] + _dot(a_ref[...], wo_ref[0:DIFF_W, :]) + _dot(gc_ref[...], wo_ref[DIFF_W:, :])
    x1_out[...] = x1
    h2 = _rms(x1, gffn_ref[...])
    hb = h2.astype(BF16)
    hpa_out[...] = _pack_bf16_pairs(h2[:, :D_MODEL // 2])
    hpb_out[...] = _pack_bf16_pairs(h2[:, D_MODEL // 2:])

    h_lo = (h2 - hb.astype(F32)).astype(BF16)
    logits = (_dot_nt(wrh_ref[...], hb) + _dot_nt(wrh_ref[...], h_lo) + _dot_nt(wrl_ref[...], hb)) + br_ref[...]

    eio = lax.broadcasted_iota(I32, (N_EXPERTS, tm), 0)
    vals, idxs, sels = [], [], []
    cur = logits
    for _ in range(TOP_K):
        m = jnp.max(cur, axis=0, keepdims=True)
        ik = jnp.min(jnp.where(cur == m, eio, N_EXPERTS), axis=0, keepdims=True)
        sel = eio == ik
        cur = jnp.where(sel, -jnp.inf, cur)
        vals.append(m)
        idxs.append(ik)
        sels.append(sel)
    es = [jnp.exp(v - vals[0]) for v in vals]
    tot = es[0] + es[1] + es[2] + es[3]
    gate_out[...] = jnp.concatenate([e / tot for e in es], axis=0)
    idx_out[...] = jnp.concatenate(idxs, axis=0)

    cnt = jnp.zeros((N_EXPERTS, tm), F32)
    for sel in sels:
        cnt = cnt + jnp.where(sel, 1.0, 0.0)
    before = lax.broadcasted_iota(I32, (tm, tm), 0) < lax.broadcasted_iota(I32, (tm, tm), 1)
    prefix = _dot(cnt.astype(BF16), jnp.where(before, 1.0, 0.0).astype(BF16))
    base = carry_ref[...] + prefix
    pos_out[...] = jnp.concatenate(
        [jnp.sum(jnp.where(sel, base, 0.0), axis=0, keepdims=True) for sel in sels], axis=0).astype(I32)
    carry_ref[...] = carry_ref[...] + jnp.sum(cnt, axis=1, keepdims=True)
    cnt_out[...] = carry_ref[...]


def _dest_kernel(cnt_ref, idx_ref, pos_ref, dest_out, be_out, nbu_out):
    idx = idx_ref[...]
    dest = pos_ref[...]
    bidx = lax.broadcasted_iota(I32, be_out.shape, 1)
    be = jnp.zeros(be_out.shape, I32)
    run = jnp.int32(0)
    for e in range(N_EXPERTS):
        dest = dest + jnp.where(idx == e, run, 0)
        run = run + ((cnt_ref[e] + (ROW_BLOCK - 1)) // ROW_BLOCK) * ROW_BLOCK
        be = be + jnp.where(bidx >= run // ROW_BLOCK, 1, 0)
    dest_out[...] = dest
    be_out[...] = jnp.minimum(be, N_EXPERTS - 1)
    nbu_out[...] = jnp.zeros(nbu_out.shape, I32) + run // ROW_BLOCK


def _expert_ffn_kernel(be_ref, nbu_ref, xa_ref, xb_ref, w1_ref, b1_ref, w2_ref, b2_ref, ya_ref, yb_ref,
                       w1s_ref, w2s_ref):
    b = pl.program_id(0)

    @pl.when(b < nbu_ref[0])
    def _():
        prev = be_ref[jnp.maximum(b - 1, 0)]

        @pl.when(jnp.logical_or(b == 0, be_ref[b] != prev))
        def _():
            w1s_ref[...] = w1_ref[0].astype(BF16)
            w2s_ref[...] = w2_ref[0].astype(BF16)

        parts = _unpack_bf16_pairs(xa_ref[...]) + _unpack_bf16_pairs(xb_ref[...])
        xrow = jnp.concatenate([p.astype(BF16) for p in parts], axis=1)
        hm = _dot(xrow, w1s_ref[...]) + b1_ref[0]
        glu = jnp.minimum(hm[:, :D_FF], SWIGLU_LIMIT)
        lin = jnp.clip(hm[:, D_FF:], -SWIGLU_LIMIT, SWIGLU_LIMIT)
        act = glu * jax.nn.sigmoid(SWIGLU_ALPHA * glu) * (lin + 1.0)
        y = _dot(act.astype(BF16), w2s_ref[...]) + b2_ref[0]
        ya_ref[...] = _pack_bf16_pairs(y[:, :D_MODEL // 2])
        yb_ref[...] = _pack_bf16_pairs(y[:, D_MODEL // 2:])


SC_WINDOW = 128


def _sc_mesh():
    return plsc.VectorSubcoreMesh(core_axis_name="c", subcore_axis_name="s")


def _sc_gather_rows(table, idx_row):
    n = idx_row.shape[1]
    width = table.shape[1]

    @functools.partial(pl.kernel, out_type=jax.ShapeDtypeStruct((n, width), table.dtype), mesh=_sc_mesh(),
                       scratch_types=[])
    def gather_kernel(t_hbm, i_hbm, o_hbm):
        def body(i_vmem, o_vmem):
            pltpu.sync_copy(t_hbm.at[i_vmem.at[0]], o_vmem)

        pltpu.emit_pipeline(
            body,
            grid=(n // SC_WINDOW,),
            in_specs=[pl.BlockSpec((1, SC_WINDOW), lambda i: (0, i))],
            out_specs=[pl.BlockSpec((SC_WINDOW, width), lambda i: (i, 0))],
            core_axis_name=("c", "s"),
            dimension_semantics=(pltpu.PARALLEL,),
        )(i_hbm, o_hbm)

    return gather_kernel(table, idx_row)


def _sc_scatter_rows(rows, idx_rows, n_out):
    n, width = rows.shape

    @functools.partial(pl.kernel, out_type=jax.ShapeDtypeStruct((n_out, width), rows.dtype), mesh=_sc_mesh(),
                       scratch_types=[])
    def scatter_kernel(r_hbm, *refs):
        i_hbms, o_hbm = refs[:-1], refs[-1]

        def body(r_vmem, *i_vmems):
            for i_vmem in i_vmems:
                pltpu.sync_copy(r_vmem, o_hbm.at[i_vmem.at[0]])

        pltpu.emit_pipeline(
            body,
            grid=(n // SC_WINDOW,),
            in_specs=[pl.BlockSpec((SC_WINDOW, width), lambda i: (i, 0))]
            + [pl.BlockSpec((1, SC_WINDOW), lambda i: (0, i)) for _ in i_hbms],
            out_specs=[],
            core_axis_name=("c", "s"),
            dimension_semantics=(pltpu.PARALLEL,),
        )(r_hbm, *i_hbms)

    return scatter_kernel(rows, *idx_rows)


def _combine_kernel(x1_ref, yga_ref, ygb_ref, gate_ref, o_ref):
    q = D_MODEL // 4
    x1 = x1_ref[...]
    acc = [x1[:, j * q:(j + 1) * q] for j in range(4)]
    for k in range(TOP_K):
        parts = _unpack_bf16_pairs(yga_ref[k]) + _unpack_bf16_pairs(ygb_ref[k])
        g = gate_ref[:, k:k + 1]
        acc = [a + g * p for a, p in zip(acc, parts)]
    for j in range(4):
        o_ref[:, j * q:(j + 1) * q] = acc[j]


def _block_diag_ones(width):
    r = jnp.arange(width) // HEAD_GROUP
    return (r[:, None] == r[None, :]).astype(BF16)


def _rope_tables(positions):
    inv_freq = ROPE_THETA ** (-jnp.arange(0, ROT_DIM, 2, dtype=F32) / ROT_DIM)
    ang = positions.astype(F32).reshape(-1, 1) * inv_freq
    cos, sin = jnp.cos(ang), jnp.sin(ang)
    n = ang.shape[0]
    pad1 = jnp.ones((n, HEAD_GROUP - ROT_DIM), F32)
    pad0 = jnp.zeros((n, HEAD_GROUP - ROT_DIM), F32)
    cos64 = jnp.concatenate([cos, cos, pad1], axis=1)
    sin64 = jnp.concatenate([-sin, sin, pad0], axis=1)
    return jnp.tile(cos64, (1, LANES // HEAD_GROUP)), jnp.tile(sin64, (1, LANES // HEAD_GROUP))


def _full(shape):
    return pl.BlockSpec(shape, lambda *_: (0,) * len(shape))


def kernel(x, mem, positions, g_mix_norm, w_in, g_dq, g_dk, lambda_q1, lambda_k1, lambda_q2, lambda_k2, g_subln, g_sgu, w_spatial, b_spatial, g_mem_norm, w_mem_kv, g_cq, g_ck, w_out, g_ffn_norm, w_router, b_router, w_mlp1, b_mlp1, w_mlp2, b_mlp2):
    B, S, D = x.shape
    M = mem.shape[1]
    N = B * S
    tm = TOKEN_TILE
    assert D == D_MODEL and S % tm == 0 and S % ATTN_TILE == 0 and g_mix_norm.shape[0] == 1
    n_assign = N * TOP_K
    n_blocks = -(-n_assign // ROW_BLOCK) + N_EXPERTS
    n_rows = n_blocks * ROW_BLOCK
    nb_pad = -(-n_blocks // LANES) * LANES

    xf = x.reshape(N, D)
    cos_t, sin_t = _rope_tables(positions)
    ones256 = _block_diag_ones(MEM_W)
    row = lambda v: v.reshape(1, -1).astype(F32)
    tile_row = lambda v, reps: jnp.tile(v.reshape(1, -1).astype(F32), (1, reps))

    kT, vm = pl.pallas_call(
        _mem_kv_kernel,
        grid=(B,),
        in_specs=[pl.BlockSpec((1, M, D), lambda b: (b, 0, 0)), _full((1, D)), _full((D, 2 * MEM_W)),
                  _full((1, MEM_W)), _full((MEM_W, MEM_W))],
        out_specs=[pl.BlockSpec((1, MEM_W, M), lambda b: (b, 0, 0)), pl.BlockSpec((1, M, MEM_W), lambda b: (b, 0, 0))],
        out_shape=[jax.ShapeDtypeStruct((B, MEM_W, M), BF16), jax.ShapeDtypeStruct((B, M, MEM_W), BF16)],
        compiler_params=_cparams(("parallel",)),
        name="mem_kv",
    )(mem, row(g_mem_norm[0]), w_mem_kv[0].astype(BF16), tile_row(g_ck[0], N_MEM_HEADS), ones256)

    w_sp_lanes = jnp.transpose(w_spatial[0], (1, 0, 2)).reshape(CHUNK, GMLP_GROUPS * CHUNK)
    b_sp_lanes = jnp.repeat(b_spatial[0].T, HEAD_GROUP, axis=1)
    tiles_per_batch = S // tm
    tok = lambda w: pl.BlockSpec((tm, w), lambda i: (i, 0))
    qn, kn, vv, gc = pl.pallas_call(
        _mixer_in_kernel,
        grid=(N // tm,),
        in_specs=[tok(D), tok(LANES), tok(LANES), _full((1, D)), _full((D, D_IN_PROJ)),
                  _full((1, DIFF_W)), _full((1, DIFF_W)),
                  _full((1, GMLP_W)), _full((CHUNK, GMLP_GROUPS * CHUNK)), _full((CHUNK, GMLP_W)),
                  _full((1, MEM_W)), _full((MEM_W, MEM_W)),
                  pl.BlockSpec((1, MEM_W, M), lambda i: (i // tiles_per_batch, 0, 0)),
                  pl.BlockSpec((1, M, MEM_W), lambda i: (i // tiles_per_batch, 0, 0))],
        out_specs=[tok(DIFF_W), tok(DIFF_W), tok(DIFF_W), tok(GMLP_W + MEM_W)],
        out_shape=[jax.ShapeDtypeStruct((N, DIFF_W), BF16)] * 3 + [jax.ShapeDtypeStruct((N, GMLP_W + MEM_W), BF16)],
        compiler_params=_cparams(("parallel",)),
        name="mixer_in",
    )(xf, cos_t, sin_t, row(g_mix_norm[0]), w_in[0].astype(BF16),
      tile_row(g_dq[0], 2 * N_DIFF_HEADS), tile_row(g_dk[0], 2 * N_DIFF_HEADS),
      row(g_sgu[0]), w_sp_lanes, b_sp_lanes, tile_row(g_cq[0], N_MEM_HEADS), ones256, kT, vm)

    tq = ATTN_TILE
    nq = S // tq
    head_q = pl.BlockSpec((tq, DIFF_W), lambda b, i: (b * nq + i, 0))
    head_kv = pl.BlockSpec((S, DIFF_W), lambda b, i: (b, 0))
    lam_spec = pl.BlockSpec((1, DIFF_HEAD_DIM), lambda b, i: (0, 0))
    nh = N_DIFF_HEADS
    attn = pl.pallas_call(
        _diff_attn_kernel,
        grid=(B, nq),
        in_specs=[head_q, head_kv, head_kv, lam_spec, lam_spec, lam_spec, lam_spec,
                  pl.BlockSpec((DIFF_V_DIM, 1), lambda b, i: (0, 0))],
        out_specs=head_q,
        out_shape=jax.ShapeDtypeStruct((N, DIFF_W), BF16),
        scratch_shapes=[pltpu.VMEM((nh, tq, 2 * tq), F32), pltpu.VMEM((nh, tq, 2 * tq), F32),
                        pltpu.VMEM((nh, 1, 2 * tq), F32), pltpu.VMEM((nh, 1, 2 * tq), F32),
                        pltpu.VMEM((nh, DIFF_V_DIM, 2 * tq), F32)],
        compiler_params=_cparams(("parallel", "arbitrary")),
        name="diff_attn",
    )(qn, kn, vv, row(lambda_q1[0]), row(lambda_k1[0]), row(lambda_q2[0]), row(lambda_k2[0]),
      g_subln[0].reshape(DIFF_V_DIM, 1).astype(F32))

    wr = w_router[0].T.astype(F32)
    wr_hi = wr.astype(BF16)
    wr_lo = (wr - wr_hi.astype(F32)).astype(BF16)
    tokT = lambda: pl.BlockSpec((TOP_K, tm), lambda i: (0, i))
    hw = D // 4
    x1, hpa, hpb, idxT, gateT, posT, counts = pl.pallas_call(
        _out_router_kernel,
        grid=(N // tm,),
        in_specs=[tok(D), tok(DIFF_W), tok(GMLP_W + MEM_W), _full((D, D)), _full((1, D)),
                  _full((N_EXPERTS, D)), _full((N_EXPERTS, D)), _full((N_EXPERTS, 1))],
        out_specs=[tok(D), tok(hw), tok(hw), tokT(), tokT(), tokT(), _full((N_EXPERTS, 1))],
        out_shape=[jax.ShapeDtypeStruct((N, D), F32), jax.ShapeDtypeStruct((N, hw), U32),
                   jax.ShapeDtypeStruct((N, hw), U32), jax.ShapeDtypeStruct((TOP_K, N), I32), jax.ShapeDtypeStruct((TOP_K, N), F32),
                   jax.ShapeDtypeStruct((TOP_K, N), I32), jax.ShapeDtypeStruct((N_EXPERTS, 1), F32)],
        scratch_shapes=[pltpu.VMEM((N_EXPERTS, 1), F32)],
        compiler_params=_cparams(("arbitrary",)),
        name="out_router",
    )(xf, attn, gc, w_out[0].astype(BF16), row(g_ffn_norm[0]), wr_hi, wr_lo, b_router[0].reshape(N_EXPERTS, 1).astype(F32))

    destT, block_expert, nb_used = pl.pallas_call(
        _dest_kernel,
        grid_spec=pltpu.PrefetchScalarGridSpec(
            num_scalar_prefetch=1,
            grid=(1,),
            in_specs=[pl.BlockSpec((TOP_K, N), lambda i, c: (0, 0)), pl.BlockSpec((TOP_K, N), lambda i, c: (0, 0))],
            out_specs=[pl.BlockSpec((TOP_K, N), lambda i, c: (0, 0)), pl.BlockSpec((1, nb_pad), lambda i, c: (0, 0)),
                       pl.BlockSpec((1, LANES), lambda i, c: (0, 0))],
        ),
        out_shape=[jax.ShapeDtypeStruct((TOP_K, N), I32), jax.ShapeDtypeStruct((1, nb_pad), I32),
                   jax.ShapeDtypeStruct((1, LANES), I32)],
        compiler_params=_cparams(("arbitrary",)),
        name="dest",
    )(counts.reshape(N_EXPERTS).astype(I32), idxT, posT)

    dest_rows = [destT[k].reshape(1, N) for k in range(TOP_K)]
    xa_buf = _sc_scatter_rows(hpa, dest_rows, n_rows)
    xb_buf = _sc_scatter_rows(hpb, dest_rows, n_rows)

    last = lambda b, be, nbu: jnp.minimum(b, nbu[0] - 1)
    row_blk = lambda: pl.BlockSpec((ROW_BLOCK, hw), lambda b, be, nbu: (last(b, be, nbu), 0))
    ya_buf, yb_buf = pl.pallas_call(
        _expert_ffn_kernel,
        grid_spec=pltpu.PrefetchScalarGridSpec(
            num_scalar_prefetch=2,
            grid=(n_blocks,),
            in_specs=[row_blk(), row_blk(),
                      pl.BlockSpec((1, D, 2 * D_FF), lambda b, be, nbu: (be[b], 0, 0)),
                      pl.BlockSpec((1, 1, 2 * D_FF), lambda b, be, nbu: (be[b], 0, 0)),
                      pl.BlockSpec((1, D_FF, D), lambda b, be, nbu: (be[b], 0, 0)),
                      pl.BlockSpec((1, 1, D), lambda b, be, nbu: (be[b], 0, 0))],
            out_specs=[row_blk(), row_blk()],
            scratch_shapes=[pltpu.VMEM((D, 2 * D_FF), BF16), pltpu.VMEM((D_FF, D), BF16)],
        ),
        out_shape=[jax.ShapeDtypeStruct((n_rows, hw), U32)] * 2,
        compiler_params=_cparams(("arbitrary",)),
        name="expert_ffn",
    )(block_expert[0, :n_blocks], nb_used[0, :1], xa_buf, xb_buf, w_mlp1[0],
      b_mlp1[0].reshape(N_EXPERTS, 1, 2 * D_FF), w_mlp2[0], b_mlp2[0].reshape(N_EXPERTS, 1, D))

    dest_flat = destT.reshape(1, n_assign)
    yga = _sc_gather_rows(ya_buf, dest_flat).reshape(TOP_K, N, hw)
    ygb = _sc_gather_rows(yb_buf, dest_flat).reshape(TOP_K, N, hw)

    out = pl.pallas_call(
        _combine_kernel,
        grid=(N // tm,),
        in_specs=[tok(D), pl.BlockSpec((TOP_K, tm, hw), lambda i: (0, i, 0)),
                  pl.BlockSpec((TOP_K, tm, hw), lambda i: (0, i, 0)), pl.BlockSpec((tm, TOP_K), lambda i: (i, 0))],
        out_specs=tok(D),
        out_shape=jax.ShapeDtypeStruct((N, D), F32),
        compiler_params=_cparams(("parallel",)),
        name="combine",
    )(x1, yga, ygb, gateT.T)
    return out.reshape(B, S, D)
```

```python
import functools

import jax
import jax.numpy as jnp
from jax import lax
from jax.experimental import pallas as pl
from jax.experimental.pallas import tpu as pltpu
from jax.experimental.pallas import tpu_sc as plsc

F32 = jnp.float32
BF16 = jnp.bfloat16
I32 = jnp.int32
U32 = jnp.uint32

D_MODEL = 1024
N_DIFF_HEADS = 4
DIFF_HEAD_DIM = 64
DIFF_V_DIM = 128
DIFF_W = 512
GMLP_W = 256
GMLP_GROUPS = 4
CHUNK = 128
MEM_W = 256
N_MEM_HEADS = 4
HEAD_GROUP = 64
D_IN_PROJ = 2304
ROPE_THETA = 500000.0
ROT_DIM = 16
N_EXPERTS = 32
TOP_K = 4
D_FF = 1024
SWIGLU_LIMIT = 7.0
SWIGLU_ALPHA = 1.702
EPS = 1e-6
LAMBDA_INIT = 0.8 - 0.6

LANES = 128
ROW_BLOCK = 512
TOKEN_TILE = 512
ROUTER_TILE = 1024
ATTN_TILE = 256
ATTN_EXT_ROWS = 128 + 16
VMEM_LIMIT = 56 * 1024 * 1024
NEG_BIG = -1e30
LOG2_E = 1.4426950408889634


def _cparams(sem):
    return pltpu.CompilerParams(dimension_semantics=sem, vmem_limit_bytes=VMEM_LIMIT)


def _dot(a, b):
    return jnp.dot(a, b, preferred_element_type=F32)


def _dot_nt(a, b):
    return lax.dot_general(a, b, (((1,), (1,)), ((), ())), preferred_element_type=F32)


def _rms(x, gain):
    ms = jnp.mean(x * x, axis=-1, keepdims=True)
    return x * lax.rsqrt(ms + EPS) * gain


def _group_rms(t, ones_bd, gain):
    w = ones_bd.shape[0]
    chunks = []
    for j in range(t.shape[1] // w):
        c = t[:, j * w:(j + 1) * w]
        ss = _dot((c * c).astype(BF16), ones_bd)
        chunks.append(c * lax.rsqrt(ss * (1.0 / HEAD_GROUP) + EPS))
    return (chunks[0] if len(chunks) == 1 else jnp.concatenate(chunks, axis=1)) * gain


def _pack_bf16_pairs(v):
    w = v.shape[1] // 2
    bits = lax.bitcast_convert_type(v.astype(BF16).astype(F32), U32)
    return (bits[:, :w] & jnp.uint32(0xFFFF0000)) | (bits[:, w:] >> jnp.uint32(16))


def _unpack_bf16_pairs(words):
    hi = lax.bitcast_convert_type(words & jnp.uint32(0xFFFF0000), F32)
    lo = lax.bitcast_convert_type(words << jnp.uint32(16), F32)
    return hi, lo


def _mem_kv_kernel(mem_ref, gmem_ref, wkv_ref, gck_ref, ones_ref, kT_ref, v_ref):
    m = _rms(mem_ref[0], gmem_ref[...]).astype(BF16)
    kv = _dot(m, wkv_ref[...])
    k = _group_rms(kv[:, :MEM_W], ones_ref[...], gck_ref[...])
    kT_ref[0] = k.T.astype(BF16)
    v_ref[0] = kv[:, MEM_W:].astype(BF16)


def _gelu_tanh(x):
    return 0.5 * x * (1.0 + jnp.tanh(0.7978845608028654 * (x + 0.044715 * (x * x * x))))


def _mixer_in_kernel(x_ref, cos_ref, sin_ref, gmix_ref, win_ref, gq_ref, gk_ref,
                     gsgu_ref, wsp_ref, bsp_ref, gcq_ref, ones256_ref, kT_ref, vm_ref,
                     q_out, k_out, v_out, gc_out):
    tm = x_ref.shape[0]
    hb = _rms(x_ref[...], gmix_ref[...]).astype(BF16)

    def proj(lo, hi):
        return _dot(hb, win_ref[:, lo:hi])

    lane = lax.broadcasted_iota(I32, (tm, LANES), 1)
    first_half = (lane % HEAD_GROUP) < (ROT_DIM // 2)
    cosb = cos_ref[...]
    sinb = sin_ref[...]

    def norm_rope(t, gain, out_ref):
        tn = _group_rms(t, ones256_ref[...], gain)
        for j in range(DIFF_W // LANES):
            c = tn[:, j * LANES:(j + 1) * LANES]
            partner = jnp.where(first_half, pltpu.roll(c, LANES - ROT_DIM // 2, 1), pltpu.roll(c, ROT_DIM // 2, 1))
            out_ref[:, j * LANES:(j + 1) * LANES] = (c * cosb + partner * sinb).astype(BF16)

    norm_rope(proj(0, DIFF_W), gq_ref[...] * (DIFF_HEAD_DIM ** -0.5 * LOG2_E), q_out)
    norm_rope(proj(DIFF_W, 2 * DIFF_W), gk_ref[...], k_out)
    v_out[...] = proj(2 * DIFF_W, 3 * DIFF_W).astype(BF16)

    z = _gelu_tanh(proj(3 * DIFF_W, 3 * DIFF_W + 2 * GMLP_W))
    u = z[:, :GMLP_W]
    vg = z[:, GMLP_W:]
    vc = vg - jnp.mean(vg, axis=-1, keepdims=True)
    vgn = (vc * lax.rsqrt(jnp.mean(vc * vc, axis=-1, keepdims=True) + EPS) * gsgu_ref[...]).astype(BF16)
    wrow = lax.broadcasted_iota(I32, (CHUNK, GMLP_GROUPS * CHUNK), 0)
    wcol = lax.broadcasted_iota(I32, (CHUNK, GMLP_GROUPS * CHUNK), 1) % CHUNK
    w_causal = jnp.where(wcol <= wrow, wsp_ref[...], 0.0).astype(BF16)
    grp = lax.broadcasted_iota(I32, (CHUNK, GMLP_W), 1) // HEAD_GROUP
    zero_b = jnp.zeros((CHUNK, GMLP_W), BF16)
    for r in range(tm // CHUNK):
        vchunk = vgn[r * CHUNK:(r + 1) * CHUNK, :]
        v_bd = jnp.concatenate([jnp.where(grp == g, vchunk, zero_b) for g in range(GMLP_GROUPS)], axis=0)
        mixed = _dot(w_causal, v_bd) + bsp_ref[...]
        gc_out[r * CHUNK:(r + 1) * CHUNK, 0:GMLP_W] = (u[r * CHUNK:(r + 1) * CHUNK, :] * mixed).astype(BF16)

    pc = proj(3 * DIFF_W + 2 * GMLP_W, D_IN_PROJ)
    qc = _group_rms(pc, ones256_ref[...], gcq_ref[...] * (HEAD_GROUP ** -0.5)).astype(BF16)
    hgrp = lax.broadcasted_iota(I32, (tm, MEM_W), 1) // HEAD_GROUP
    zero_q = jnp.zeros((tm, MEM_W), BF16)
    q_st = jnp.concatenate([jnp.where(hgrp == h, qc, zero_q) for h in range(N_MEM_HEADS)], axis=0)
    s = _dot(q_st, kT_ref[0])
    p = jnp.exp(s - jnp.max(s, axis=-1, keepdims=True))
    o = _dot(p.astype(BF16), vm_ref[0]) / jnp.sum(p, axis=-1, keepdims=True)
    c = jnp.zeros((tm, MEM_W), F32)
    for h in range(N_MEM_HEADS):
        c = c + jnp.where(hgrp == h, o[h * tm:(h + 1) * tm, :], 0.0)
    gc_out[:, GMLP_W:GMLP_W + MEM_W] = c.astype(BF16)


def _diff_attn_kernel(q_ref, k_ref, v_ref, lq1_ref, lk1_ref, lq2_ref, lk2_ref, gsub_ref, o_ref,
                      s0_ref, s1_ref, m_ref, acc_ref, vT_ref):
    tq = q_ref.shape[0]
    seq = k_ref.shape[0]
    i = pl.program_id(1)
    lane = lax.broadcasted_iota(I32, (tq, LANES), 1)
    heads = range(N_DIFF_HEADS)
    hl = lambda h: slice(h * DIFF_V_DIM, (h + 1) * DIFF_V_DIM)
    ext_rows = vT_ref.shape[1]

    @pl.when(i == 0)
    def _():
        ones_row = jnp.where(lax.broadcasted_iota(I32, (ext_rows - DIFF_V_DIM, seq), 0) == 0, 1.0, 0.0).astype(BF16)
        for h in heads:
            for c in range(seq // tq):
                vT_ref[h, 0:DIFF_V_DIM, c * tq:(c + 1) * tq] = v_ref[c * tq:(c + 1) * tq, hl(h)].T
            vT_ref[h, DIFF_V_DIM:ext_rows, :] = ones_row

    def stacked_qT(h):
        q = q_ref[:, hl(h)]
        zero = jnp.zeros_like(q)
        return jnp.concatenate([jnp.where(lane < DIFF_HEAD_DIM, q, zero), jnp.where(lane >= DIFF_HEAD_DIM, q, zero)],
                               axis=0).T

    qsT = [stacked_qT(h) for h in heads]

    def scores(t, s_ref):
        rows = pl.ds(pl.multiple_of(t * tq, tq), tq)
        for h in heads:
            s_ref[h] = _dot(k_ref[rows, hl(h)], qsT[h])

    def update(t, s_ref, causal):
        cols = pl.ds(pl.multiple_of(t * tq, tq), tq)
        for h in heads:
            s = s_ref[h]
            if causal:
                key = lax.broadcasted_iota(I32, (tq, 2 * tq), 0)
                qry = lax.broadcasted_iota(I32, (tq, 2 * tq), 1) % tq
                s = jnp.where(key <= qry, s, NEG_BIG)
            m = m_ref[h]
            m_new = jnp.maximum(m, jnp.max(s, axis=0, keepdims=True))
            alpha = jnp.exp2(m - m_new)
            p = jnp.exp2(s - m_new)
            m_ref[h] = m_new
            acc_ref[h] = alpha * acc_ref[h] + _dot(vT_ref[h, :, cols], p.astype(BF16))

    m_ref[...] = jnp.full(m_ref.shape, NEG_BIG, F32)
    acc_ref[...] = jnp.zeros(acc_ref.shape, F32)
    scores(0, s0_ref)

    def pair(pidx, carry):
        t = 2 * pidx
        scores(t + 1, s1_ref)
        update(t, s0_ref, False)
        scores(t + 2, s0_ref)
        update(t + 1, s1_ref, False)
        return carry

    lax.fori_loop(0, i // 2, pair, 0)

    @pl.when(i % 2 == 0)
    def _():
        update(i, s0_ref, True)

    @pl.when(i % 2 == 1)
    def _():
        scores(i, s1_ref)
        update(i - 1, s0_ref, False)
        update(i, s1_ref, True)

    lam = (jnp.exp(jnp.sum(lq1_ref[...] * lk1_ref[...], axis=-1, keepdims=True))
           - jnp.exp(jnp.sum(lq2_ref[...] * lk2_ref[...], axis=-1, keepdims=True)) + LAMBDA_INIT)
    for h in heads:
        on = acc_ref[h, 0:DIFF_V_DIM, :] / acc_ref[h, DIFF_V_DIM:DIFF_V_DIM + 1, :]
        o = on[:, :tq] - lam * on[:, tq:]
        ms = jnp.mean(o * o, axis=0, keepdims=True)
        o = o * lax.rsqrt(ms + EPS) * gsub_ref[...] * (1.0 - LAMBDA_INIT)
        o_ref[:, hl(h)] = o.T.astype(BF16)


def _out_router_kernel(x_ref, a_ref, gc_ref, wo_ref, gffn_ref, wrh_ref, wrl_ref, br_ref, before_ref,
                       x1_out, hpa_out, hpb_out, idx_out, gate_out, pos_out, cnt_out, carry_ref):
    tm = x_ref.shape[0]

    @pl.when(pl.program_id(0) == 0)
    def _():
        carry_ref[...] = jnp.zeros_like(carry_ref)

    sub = before_ref.shape[0]
    wr_stack = jnp.concatenate([wrh_ref[...], wrl_ref[...]], axis=0)
    eio = lax.broadcasted_iota(I32, (N_EXPERTS, sub), 0)
    before_b = before_ref[...]
    carry = carry_ref[...]
    for r in range(tm // sub):
        rows = slice(r * sub, (r + 1) * sub)
        mix = jnp.concatenate([a_ref[rows, :], gc_ref[rows, :]], axis=1)
        x1 = x_ref[rows, :] + _dot(mix, wo_ref[...])
        x1_out[rows, :] = x1
        h2 = _rms(x1, gffn_ref[...])
        hb = h2.astype(BF16)
        hpa_out[rows, :] = _pack_bf16_pairs(h2[:, :D_MODEL // 2])
        hpb_out[rows, :] = _pack_bf16_pairs(h2[:, D_MODEL // 2:])

        h_lo = (h2 - hb.astype(F32)).astype(BF16)
        both = _dot_nt(wr_stack, hb)
        logits = (both[:N_EXPERTS] + both[N_EXPERTS:]) + _dot_nt(wrh_ref[...], h_lo) + br_ref[...]

        vals, idxs, sels = [], [], []
        cur = logits
        for _ in range(TOP_K):
            m = jnp.max(cur, axis=0, keepdims=True)
            ik = jnp.min(jnp.where(cur == m, eio, N_EXPERTS), axis=0, keepdims=True)
            sel = eio == ik
            cur = jnp.where(sel, -jnp.inf, cur)
            vals.append(m)
            idxs.append(ik)
            sels.append(sel)
        es = [jnp.exp(v - vals[0]) for v in vals]
        tot = es[0] + es[1] + es[2] + es[3]
        gate_out[:, rows] = jnp.concatenate([e / tot for e in es], axis=0)
        idx_out[:, rows] = jnp.concatenate(idxs, axis=0)

        cnt = jnp.zeros((N_EXPERTS, sub), F32)
        for sel in sels:
            cnt = cnt + jnp.where(sel, 1.0, 0.0)
        base = carry + _dot(cnt.astype(BF16), before_b)
        pos_out[:, rows] = jnp.concatenate(
            [jnp.sum(jnp.where(sel, base, 0.0), axis=0, keepdims=True) for sel in sels], axis=0).astype(I32)
        carry = carry + jnp.sum(cnt, axis=1, keepdims=True)
    carry_ref[...] = carry
    cnt_out[...] = carry


def _dest_kernel(cnt_ref, idx_ref, pos_ref, dest_out, be_out, nbu_out):
    idx = idx_ref[...]
    dest = pos_ref[...]
    bidx = lax.broadcasted_iota(I32, be_out.shape, 1)
    be = jnp.zeros(be_out.shape, I32)
    run = jnp.int32(0)
    for e in range(N_EXPERTS):
        dest = dest + jnp.where(idx == e, run, 0)
        run = run + ((cnt_ref[e] + (ROW_BLOCK - 1)) // ROW_BLOCK) * ROW_BLOCK
        be = be + jnp.where(bidx >= run // ROW_BLOCK, 1, 0)
    dest_out[...] = dest
    be_out[...] = jnp.minimum(be, N_EXPERTS - 1)
    nbu_out[...] = jnp.zeros(nbu_out.shape, I32) + run // ROW_BLOCK


def _expert_ffn_kernel(be_ref, nbu_ref, xa_ref, xb_ref, w1_ref, b1_ref, w2_ref, b2_ref, ya_ref, yb_ref,
                       w1s_ref, w2s_ref):
    b = pl.program_id(0)

    @pl.when(b < nbu_ref[0])
    def _():
        prev = be_ref[jnp.maximum(b - 1, 0)]

        @pl.when(jnp.logical_or(b == 0, be_ref[b] != prev))
        def _():
            w1s_ref[...] = w1_ref[0].astype(BF16)
            w2s_ref[...] = w2_ref[0].astype(BF16)

        parts = _unpack_bf16_pairs(xa_ref[...]) + _unpack_bf16_pairs(xb_ref[...])
        xrow = jnp.concatenate([p.astype(BF16) for p in parts], axis=1)
        hm = _dot(xrow, w1s_ref[...]) + b1_ref[0]
        glu = jnp.minimum(hm[:, :D_FF], SWIGLU_LIMIT)
        lin = jnp.clip(hm[:, D_FF:], -SWIGLU_LIMIT, SWIGLU_LIMIT)
        act = glu * jax.nn.sigmoid(SWIGLU_ALPHA * glu) * (lin + 1.0)
        y = _dot(act.astype(BF16), w2s_ref[...]) + b2_ref[0]
        ya_ref[...] = _pack_bf16_pairs(y[:, :D_MODEL // 2])
        yb_ref[...] = _pack_bf16_pairs(y[:, D_MODEL // 2:])


SC_WINDOW = 128


def _sc_mesh():
    return plsc.VectorSubcoreMesh(core_axis_name="c", subcore_axis_name="s")


def _sc_gather_rows(table, idx_row):
    n = idx_row.shape[1]
    width = table.shape[1]

    @functools.partial(pl.kernel, out_type=jax.ShapeDtypeStruct((n, width), table.dtype), mesh=_sc_mesh(),
                       scratch_types=[])
    def gather_kernel(t_hbm, i_hbm, o_hbm):
        def body(i_vmem, o_vmem):
            pltpu.sync_copy(t_hbm.at[i_vmem.at[0]], o_vmem)

        pltpu.emit_pipeline(
            body,
            grid=(n // SC_WINDOW,),
            in_specs=[pl.BlockSpec((1, SC_WINDOW), lambda i: (0, i))],
            out_specs=[pl.BlockSpec((SC_WINDOW, width), lambda i: (i, 0))],
            core_axis_name=("c", "s"),
            dimension_semantics=(pltpu.PARALLEL,),
        )(i_hbm, o_hbm)

    return gather_kernel(table, idx_row)


def _sc_scatter_rows(rows, idx_rows, n_out):
    n, width = rows.shape

    @functools.partial(pl.kernel, out_type=jax.ShapeDtypeStruct((n_out, width), rows.dtype), mesh=_sc_mesh(),
                       scratch_types=[])
    def scatter_kernel(r_hbm, *refs):
        i_hbms, o_hbm = refs[:-1], refs[-1]

        def body(r_vmem, *i_vmems):
            for i_vmem in i_vmems:
                pltpu.sync_copy(r_vmem, o_hbm.at[i_vmem.at[0]])

        pltpu.emit_pipeline(
            body,
            grid=(n // SC_WINDOW,),
            in_specs=[pl.BlockSpec((SC_WINDOW, width), lambda i: (i, 0))]
            + [pl.BlockSpec((1, SC_WINDOW), lambda i: (0, i)) for _ in i_hbms],
            out_specs=[],
            core_axis_name=("c", "s"),
            dimension_semantics=(pltpu.PARALLEL,),
        )(r_hbm, *i_hbms)

    return scatter_kernel(rows, *idx_rows)


def _combine_kernel(x1_ref, yga_ref, ygb_ref, gate_ref, o_ref):
    q = D_MODEL // 4
    x1 = x1_ref[...]
    acc = [x1[:, j * q:(j + 1) * q] for j in range(4)]
    for k in range(TOP_K):
        parts = _unpack_bf16_pairs(yga_ref[k]) + _unpack_bf16_pairs(ygb_ref[k])
        g = gate_ref[:, k:k + 1]
        acc = [a + g * p for a, p in zip(acc, parts)]
    for j in range(4):
        o_ref[:, j * q:(j + 1) * q] = acc[j]


def _block_diag_ones(width):
    r = jnp.arange(width) // HEAD_GROUP
    return (r[:, None] == r[None, :]).astype(BF16)


def _rope_tables(positions):
    inv_freq = ROPE_THETA ** (-jnp.arange(0, ROT_DIM, 2, dtype=F32) / ROT_DIM)
    ang = positions.astype(F32).reshape(-1, 1) * inv_freq
    cos, sin = jnp.cos(ang), jnp.sin(ang)
    n = ang.shape[0]
    pad1 = jnp.ones((n, HEAD_GROUP - ROT_DIM), F32)
    pad0 = jnp.zeros((n, HEAD_GROUP - ROT_DIM), F32)
    cos64 = jnp.concatenate([cos, cos, pad1], axis=1)
    sin64 = jnp.concatenate([-sin, sin, pad0], axis=1)
    return jnp.tile(cos64, (1, LANES // HEAD_GROUP)), jnp.tile(sin64, (1, LANES // HEAD_GROUP))


def _full(shape):
    return pl.BlockSpec(shape, lambda *_: (0,) * len(shape))


def kernel(x, mem, positions, g_mix_norm, w_in, g_dq, g_dk, lambda_q1, lambda_k1, lambda_q2, lambda_k2, g_subln, g_sgu, w_spatial, b_spatial, g_mem_norm, w_mem_kv, g_cq, g_ck, w_out, g_ffn_norm, w_router, b_router, w_mlp1, b_mlp1, w_mlp2, b_mlp2):
    B, S, D = x.shape
    M = mem.shape[1]
    N = B * S
    tm = TOKEN_TILE
    assert D == D_MODEL and S % tm == 0 and S % ATTN_TILE == 0 and g_mix_norm.shape[0] == 1
    n_assign = N * TOP_K
    n_blocks = -(-n_assign // ROW_BLOCK) + N_EXPERTS
    n_rows = n_blocks * ROW_BLOCK
    nb_pad = -(-n_blocks // LANES) * LANES

    xf = x.reshape(N, D)
    cos_t, sin_t = _rope_tables(positions)
    ones256 = _block_diag_ones(MEM_W)
    row = lambda v: v.reshape(1, -1).astype(F32)
    tile_row = lambda v, reps: jnp.tile(v.reshape(1, -1).astype(F32), (1, reps))

    kT, vm = pl.pallas_call(
        _mem_kv_kernel,
        grid=(B,),
        in_specs=[pl.BlockSpec((1, M, D), lambda b: (b, 0, 0)), _full((1, D)), _full((D, 2 * MEM_W)),
                  _full((1, MEM_W)), _full((MEM_W, MEM_W))],
        out_specs=[pl.BlockSpec((1, MEM_W, M), lambda b: (b, 0, 0)), pl.BlockSpec((1, M, MEM_W), lambda b: (b, 0, 0))],
        out_shape=[jax.ShapeDtypeStruct((B, MEM_W, M), BF16), jax.ShapeDtypeStruct((B, M, MEM_W), BF16)],
        compiler_params=_cparams(("parallel",)),
        name="mem_kv",
    )(mem, row(g_mem_norm[0]), w_mem_kv[0].astype(BF16), tile_row(g_ck[0], N_MEM_HEADS), ones256)

    w_sp_lanes = jnp.transpose(w_spatial[0], (1, 0, 2)).reshape(CHUNK, GMLP_GROUPS * CHUNK)
    b_sp_lanes = jnp.repeat(b_spatial[0].T, HEAD_GROUP, axis=1)
    tiles_per_batch = S // tm
    tok = lambda w: pl.BlockSpec((tm, w), lambda i: (i, 0))
    qn, kn, vv, gc = pl.pallas_call(
        _mixer_in_kernel,
        grid=(N // tm,),
        in_specs=[tok(D), tok(LANES), tok(LANES), _full((1, D)), _full((D, D_IN_PROJ)),
                  _full((1, DIFF_W)), _full((1, DIFF_W)),
                  _full((1, GMLP_W)), _full((CHUNK, GMLP_GROUPS * CHUNK)), _full((CHUNK, GMLP_W)),
                  _full((1, MEM_W)), _full((MEM_W, MEM_W)),
                  pl.BlockSpec((1, MEM_W, M), lambda i: (i // tiles_per_batch, 0, 0)),
                  pl.BlockSpec((1, M, MEM_W), lambda i: (i // tiles_per_batch, 0, 0))],
        out_specs=[tok(DIFF_W), tok(DIFF_W), tok(DIFF_W), tok(GMLP_W + MEM_W)],
        out_shape=[jax.ShapeDtypeStruct((N, DIFF_W), BF16)] * 3 + [jax.ShapeDtypeStruct((N, GMLP_W + MEM_W), BF16)],
        compiler_params=_cparams(("parallel",)),
        name="mixer_in",
    )(xf, cos_t, sin_t, row(g_mix_norm[0]), w_in[0].astype(BF16),
      tile_row(g_dq[0], 2 * N_DIFF_HEADS), tile_row(g_dk[0], 2 * N_DIFF_HEADS),
      row(g_sgu[0]), w_sp_lanes, b_sp_lanes, tile_row(g_cq[0], N_MEM_HEADS), ones256, kT, vm)

    tq = ATTN_TILE
    nq = S // tq
    head_q = pl.BlockSpec((tq, DIFF_W), lambda b, i: (b * nq + i, 0))
    head_kv = pl.BlockSpec((S, DIFF_W), lambda b, i: (b, 0))
    lam_spec = pl.BlockSpec((1, DIFF_HEAD_DIM), lambda b, i: (0, 0))
    nh = N_DIFF_HEADS
    attn = pl.pallas_call(
        _diff_attn_kernel,
        grid=(B, nq),
        in_specs=[head_q, head_kv, head_kv, lam_spec, lam_spec, lam_spec, lam_spec,
                  pl.BlockSpec((DIFF_V_DIM, 1), lambda b, i: (0, 0))],
        out_specs=head_q,
        out_shape=jax.ShapeDtypeStruct((N, DIFF_W), BF16),
        scratch_shapes=[pltpu.VMEM((nh, tq, 2 * tq), F32), pltpu.VMEM((nh, tq, 2 * tq), F32),
                        pltpu.VMEM((nh, 1, 2 * tq), F32), pltpu.VMEM((nh, ATTN_EXT_ROWS, 2 * tq), F32),
                        pltpu.VMEM((nh, ATTN_EXT_ROWS, S), BF16)],
        compiler_params=_cparams(("parallel", "arbitrary")),
        name="diff_attn",
    )(qn, kn, vv, row(lambda_q1[0]), row(lambda_k1[0]), row(lambda_q2[0]), row(lambda_k2[0]),
      g_subln[0].reshape(DIFF_V_DIM, 1).astype(F32))

    wr = w_router[0].T.astype(F32)
    wr_hi = wr.astype(BF16)
    wr_lo = (wr - wr_hi.astype(F32)).astype(BF16)
    tr = ROUTER_TILE
    assert N % tr == 0
    rtok = lambda w: pl.BlockSpec((tr, w), lambda i: (i, 0))
    tokT = lambda: pl.BlockSpec((TOP_K, tr), lambda i: (0, i))
    before = (jnp.arange(tr)[:, None] < jnp.arange(tr)[None, :]).astype(BF16)
    hw = D // 4
    x1, hpa, hpb, idxT, gateT, posT, counts = pl.pallas_call(
        _out_router_kernel,
        grid=(N // tr,),
        in_specs=[rtok(D), rtok(DIFF_W), rtok(GMLP_W + MEM_W), _full((D, D)), _full((1, D)),
                  _full((N_EXPERTS, D)), _full((N_EXPERTS, D)), _full((N_EXPERTS, 1)), _full((tr, tr))],
        out_specs=[rtok(D), rtok(hw), rtok(hw), tokT(), tokT(), tokT(), _full((N_EXPERTS, 1))],
        out_shape=[jax.ShapeDtypeStruct((N, D), F32), jax.ShapeDtypeStruct((N, hw), U32),
                   jax.ShapeDtypeStruct((N, hw), U32), jax.ShapeDtypeStruct((TOP_K, N), I32), jax.ShapeDtypeStruct((TOP_K, N), F32),
                   jax.ShapeDtypeStruct((TOP_K, N), I32), jax.ShapeDtypeStruct((N_EXPERTS, 1), F32)],
        scratch_shapes=[pltpu.VMEM((N_EXPERTS, 1), F32)],
        compiler_params=_cparams(("arbitrary",)),
        name="out_router",
    )(xf, attn, gc, w_out[0].astype(BF16), row(g_ffn_norm[0]), wr_hi, wr_lo, b_router[0].reshape(N_EXPERTS, 1).astype(F32), before)

    destT, block_expert, nb_used = pl.pallas_call(
        _dest_kernel,
        grid_spec=pltpu.PrefetchScalarGridSpec(
            num_scalar_prefetch=1,
            grid=(1,),
            in_specs=[pl.BlockSpec((TOP_K, N), lambda i, c: (0, 0)), pl.BlockSpec((TOP_K, N), lambda i, c: (0, 0))],
            out_specs=[pl.BlockSpec((TOP_K, N), lambda i, c: (0, 0)), pl.BlockSpec((1, nb_pad), lambda i, c: (0, 0)),
                       pl.BlockSpec((1, LANES), lambda i, c: (0, 0))],
        ),
        out_shape=[jax.ShapeDtypeStruct((TOP_K, N), I32), jax.ShapeDtypeStruct((1, nb_pad), I32),
                   jax.ShapeDtypeStruct((1, LANES), I32)],
        compiler_params=_cparams(("arbitrary",)),
        name="dest",
    )(counts.reshape(N_EXPERTS).astype(I32), idxT, posT)

    dest_rows = [destT[k].reshape(1, N) for k in range(TOP_K)]
    xa_buf = _sc_scatter_rows(hpa, dest_rows, n_rows)
    xb_buf = _sc_scatter_rows(hpb, dest_rows, n_rows)

    last = lambda b, be, nbu: jnp.minimum(b, nbu[0] - 1)
    row_blk = lambda: pl.BlockSpec((ROW_BLOCK, hw), lambda b, be, nbu: (last(b, be, nbu), 0))
    ya_buf, yb_buf = pl.pallas_call(
        _expert_ffn_kernel,
        grid_spec=pltpu.PrefetchScalarGridSpec(
            num_scalar_prefetch=2,
            grid=(n_blocks,),
            in_specs=[row_blk(), row_blk(),
                      pl.BlockSpec((1, D, 2 * D_FF), lambda b, be, nbu: (be[b], 0, 0)),
                      pl.BlockSpec((1, 1, 2 * D_FF), lambda b, be, nbu: (be[b], 0, 0)),
                      pl.BlockSpec((1, D_FF, D), lambda b, be, nbu: (be[b], 0, 0)),
                      pl.BlockSpec((1, 1, D), lambda b, be, nbu: (be[b], 0, 0))],
            out_specs=[row_blk(), row_blk()],
            scratch_shapes=[pltpu.VMEM((D, 2 * D_FF), BF16), pltpu.VMEM((D_FF, D), BF16)],
        ),
        out_shape=[jax.ShapeDtypeStruct((n_rows, hw), U32)] * 2,
        compiler_params=_cparams(("arbitrary",)),
        name="expert_ffn",
    )(block_expert[0, :n_blocks], nb_used[0, :1], xa_buf, xb_buf, w_mlp1[0],
      b_mlp1[0].reshape(N_EXPERTS, 1, 2 * D_FF), w_mlp2[0], b_mlp2[0].reshape(N_EXPERTS, 1, D))

    dest_flat = destT.reshape(1, n_assign)
    yga = _sc_gather_rows(ya_buf, dest_flat).reshape(TOP_K, N, hw)
    ygb = _sc_gather_rows(yb_buf, dest_flat).reshape(TOP_K, N, hw)

    out = pl.pallas_call(
        _combine_kernel,
        grid=(N // tm,),
        in_specs=[tok(D), pl.BlockSpec((TOP_K, tm, hw), lambda i: (0, i, 0)),
                  pl.BlockSpec((TOP_K, tm, hw), lambda i: (0, i, 0)), pl.BlockSpec((tm, TOP_K), lambda i: (i, 0))],
        out_specs=tok(D),
        out_shape=jax.ShapeDtypeStruct((N, D), F32),
        compiler_params=_cparams(("parallel",)),
        name="combine",
    )(x1, yga, ygb, gateT.T)
    return out.reshape(B, S, D)
```

```python
import functools

import jax
import jax.numpy as jnp
from jax import lax
from jax.experimental import pallas as pl
from jax.experimental.pallas import tpu as pltpu
from jax.experimental.pallas import tpu_sc as plsc

F32 = jnp.float32
BF16 = jnp.bfloat16
I32 = jnp.int32
U32 = jnp.uint32

D_MODEL = 1024
N_DIFF_HEADS = 4
DIFF_HEAD_DIM = 64
DIFF_V_DIM = 128
DIFF_W = 512
GMLP_W = 256
GMLP_GROUPS = 4
CHUNK = 128
MEM_W = 256
N_MEM_HEADS = 4
HEAD_GROUP = 64
D_IN_PROJ = 2304
ROPE_THETA = 500000.0
ROT_DIM = 16
N_EXPERTS = 32
TOP_K = 4
D_FF = 1024
SWIGLU_LIMIT = 7.0
SWIGLU_ALPHA = 1.702
EPS = 1e-6
LAMBDA_INIT = 0.8 - 0.6

LANES = 128
ROW_BLOCK = 512
TOKEN_TILE = 512
ROUTER_TILE = 1024
GATE_ROWS = 16
ATTN_TILE = 256
ATTN_EXT_ROWS = 128 + 16
VMEM_LIMIT = 56 * 1024 * 1024
NEG_BIG = -1e30
LOG2_E = 1.4426950408889634


def _cparams(sem):
    return pltpu.CompilerParams(dimension_semantics=sem, vmem_limit_bytes=VMEM_LIMIT)


def _dot(a, b):
    return jnp.dot(a, b, preferred_element_type=F32)


def _dot_nt(a, b):
    return lax.dot_general(a, b, (((1,), (1,)), ((), ())), preferred_element_type=F32)


def _rms(x, gain):
    ms = jnp.mean(x * x, axis=-1, keepdims=True)
    return x * lax.rsqrt(ms + EPS) * gain


def _group_rms(t, ones_bd, gain):
    w = ones_bd.shape[0]
    chunks = []
    for j in range(t.shape[1] // w):
        c = t[:, j * w:(j + 1) * w]
        ss = _dot((c * c).astype(BF16), ones_bd)
        chunks.append(c * lax.rsqrt(ss * (1.0 / HEAD_GROUP) + EPS))
    return (chunks[0] if len(chunks) == 1 else jnp.concatenate(chunks, axis=1)) * gain


def _pack_bf16_pairs(v):
    w = v.shape[1] // 2
    bits = lax.bitcast_convert_type(v.astype(BF16).astype(F32), U32)
    return (bits[:, :w] & jnp.uint32(0xFFFF0000)) | (bits[:, w:] >> jnp.uint32(16))


def _unpack_bf16_pairs(words):
    hi = lax.bitcast_convert_type(words & jnp.uint32(0xFFFF0000), F32)
    lo = lax.bitcast_convert_type(words << jnp.uint32(16), F32)
    return hi, lo


def _mem_kv_kernel(mem_ref, gmem_ref, wkv_ref, gck_ref, ones_ref, kT_ref, v_ref):
    m = _rms(mem_ref[0], gmem_ref[...]).astype(BF16)
    kv = _dot(m, wkv_ref[...])
    k = _group_rms(kv[:, :MEM_W], ones_ref[...], gck_ref[...])
    kT_ref[0] = k.T.astype(BF16)
    v_ref[0] = kv[:, MEM_W:].astype(BF16)


def _gelu_tanh(x):
    return 0.5 * x * (1.0 + jnp.tanh(0.7978845608028654 * (x + 0.044715 * (x * x * x))))


def _mixer_in_kernel(x_ref, cs_ref, spread_ref, gmix_ref, win_ref, gq_ref, gk_ref,
                     gsgu_ref, wsp_ref, bsp_ref, gcq_ref, ones256_ref, kT_ref, vm_ref,
                     q_out, k_out, v_out, gc_out):
    tm = x_ref.shape[0]
    hb = _rms(x_ref[...], gmix_ref[...]).astype(BF16)

    def proj(lo, hi):
        return _dot(hb, win_ref[:, lo:hi])

    lane = lax.broadcasted_iota(I32, (tm, LANES), 1)
    first_half = (lane % HEAD_GROUP) < (ROT_DIM // 2)
    tab = lax.dot_general(cs_ref[...], spread_ref[...], (((0,), (0,)), ((), ())), preferred_element_type=F32)
    cosb = tab[:, :LANES] + jnp.where((lane % HEAD_GROUP) >= ROT_DIM, 1.0, 0.0)
    sinb = tab[:, LANES:]

    def norm_rope(t, gain, out_ref):
        tn = _group_rms(t, ones256_ref[...], gain)
        for j in range(DIFF_W // LANES):
            c = tn[:, j * LANES:(j + 1) * LANES]
            partner = jnp.where(first_half, pltpu.roll(c, LANES - ROT_DIM // 2, 1), pltpu.roll(c, ROT_DIM // 2, 1))
            out_ref[:, j * LANES:(j + 1) * LANES] = (c * cosb + partner * sinb).astype(BF16)

    norm_rope(proj(0, DIFF_W), gq_ref[...] * (DIFF_HEAD_DIM ** -0.5 * LOG2_E), q_out)
    norm_rope(proj(DIFF_W, 2 * DIFF_W), gk_ref[...], k_out)
    v_out[...] = proj(2 * DIFF_W, 3 * DIFF_W).astype(BF16)

    z = _gelu_tanh(proj(3 * DIFF_W, 3 * DIFF_W + 2 * GMLP_W))
    u = z[:, :GMLP_W]
    vg = z[:, GMLP_W:]
    vc = vg - jnp.mean(vg, axis=-1, keepdims=True)
    vgn = (vc * lax.rsqrt(jnp.mean(vc * vc, axis=-1, keepdims=True) + EPS) * gsgu_ref[...]).astype(BF16)
    wrow = lax.broadcasted_iota(I32, (CHUNK, GMLP_GROUPS * CHUNK), 0)
    wcol = lax.broadcasted_iota(I32, (CHUNK, GMLP_GROUPS * CHUNK), 1) % CHUNK
    w_causal = jnp.where(wcol <= wrow, wsp_ref[...], 0.0).astype(BF16)
    grp = lax.broadcasted_iota(I32, (CHUNK, GMLP_W), 1) // HEAD_GROUP
    zero_b = jnp.zeros((CHUNK, GMLP_W), BF16)
    for r in range(tm // CHUNK):
        vchunk = vgn[r * CHUNK:(r + 1) * CHUNK, :]
        v_bd = jnp.concatenate([jnp.where(grp == g, vchunk, zero_b) for g in range(GMLP_GROUPS)], axis=0)
        mixed = _dot(w_causal, v_bd) + bsp_ref[...]
        gc_out[r * CHUNK:(r + 1) * CHUNK, 0:GMLP_W] = (u[r * CHUNK:(r + 1) * CHUNK, :] * mixed).astype(BF16)

    pc = proj(3 * DIFF_W + 2 * GMLP_W, D_IN_PROJ)
    qc = _group_rms(pc, ones256_ref[...], gcq_ref[...] * (HEAD_GROUP ** -0.5)).astype(BF16)
    hgrp = lax.broadcasted_iota(I32, (tm, MEM_W), 1) // HEAD_GROUP
    zero_q = jnp.zeros((tm, MEM_W), BF16)
    q_st = jnp.concatenate([jnp.where(hgrp == h, qc, zero_q) for h in range(N_MEM_HEADS)], axis=0)
    s = _dot(q_st, kT_ref[0])
    p = jnp.exp(s - jnp.max(s, axis=-1, keepdims=True))
    o = _dot(p.astype(BF16), vm_ref[0]) / jnp.sum(p, axis=-1, keepdims=True)
    c = jnp.zeros((tm, MEM_W), F32)
    for h in range(N_MEM_HEADS):
        c = c + jnp.where(hgrp == h, o[h * tm:(h + 1) * tm, :], 0.0)
    gc_out[:, GMLP_W:GMLP_W + MEM_W] = c.astype(BF16)


def _diff_attn_kernel(q_ref, k_ref, v_ref, lq1_ref, lk1_ref, lq2_ref, lk2_ref, gsub_ref, o_ref,
                      s0_ref, s1_ref, m_ref, acc_ref, vT_ref):
    tq = q_ref.shape[0]
    seq = k_ref.shape[0]
    i = pl.program_id(1)
    lane = lax.broadcasted_iota(I32, (tq, LANES), 1)
    heads = range(N_DIFF_HEADS)
    hl = lambda h: slice(h * DIFF_V_DIM, (h + 1) * DIFF_V_DIM)
    ext_rows = vT_ref.shape[1]

    @pl.when(i == 0)
    def _():
        ones_row = jnp.where(lax.broadcasted_iota(I32, (ext_rows - DIFF_V_DIM, seq), 0) == 0, 1.0, 0.0).astype(BF16)
        for h in heads:
            for c in range(seq // tq):
                vT_ref[h, 0:DIFF_V_DIM, c * tq:(c + 1) * tq] = v_ref[c * tq:(c + 1) * tq, hl(h)].T
            vT_ref[h, DIFF_V_DIM:ext_rows, :] = ones_row

    def stacked_qT(h):
        q = q_ref[:, hl(h)]
        zero = jnp.zeros_like(q)
        return jnp.concatenate([jnp.where(lane < DIFF_HEAD_DIM, q, zero), jnp.where(lane >= DIFF_HEAD_DIM, q, zero)],
                               axis=0).T

    qsT = [stacked_qT(h) for h in heads]

    def scores(t, s_ref):
        rows = pl.ds(pl.multiple_of(t * tq, tq), tq)
        for h in heads:
            s_ref[h] = _dot(k_ref[rows, hl(h)], qsT[h])

    def update(t, s_ref, causal):
        cols = pl.ds(pl.multiple_of(t * tq, tq), tq)
        for h in heads:
            s = s_ref[h]
            if causal:
                key = lax.broadcasted_iota(I32, (tq, 2 * tq), 0)
                qry = lax.broadcasted_iota(I32, (tq, 2 * tq), 1) % tq
                s = jnp.where(key <= qry, s, NEG_BIG)
            m = m_ref[h]
            m_new = jnp.maximum(m, jnp.max(s, axis=0, keepdims=True))
            alpha = jnp.exp2(m - m_new)
            p = jnp.exp2(s - m_new)
            m_ref[h] = m_new
            acc_ref[h] = alpha * acc_ref[h] + _dot(vT_ref[h, :, cols], p.astype(BF16))

    m_ref[...] = jnp.full(m_ref.shape, NEG_BIG, F32)
    acc_ref[...] = jnp.zeros(acc_ref.shape, F32)
    scores(0, s0_ref)

    def pair(pidx, carry):
        t = 2 * pidx
        scores(t + 1, s1_ref)
        update(t, s0_ref, False)
        scores(t + 2, s0_ref)
        update(t + 1, s1_ref, False)
        return carry

    lax.fori_loop(0, i // 2, pair, 0)

    @pl.when(i % 2 == 0)
    def _():
        update(i, s0_ref, True)

    @pl.when(i % 2 == 1)
    def _():
        scores(i, s1_ref)
        update(i - 1, s0_ref, False)
        update(i, s1_ref, True)

    lam = (jnp.exp(jnp.sum(lq1_ref[...] * lk1_ref[...], axis=-1, keepdims=True))
           - jnp.exp(jnp.sum(lq2_ref[...] * lk2_ref[...], axis=-1, keepdims=True)) + LAMBDA_INIT)
    for h in heads:
        on = acc_ref[h, 0:DIFF_V_DIM, :] / acc_ref[h, DIFF_V_DIM:DIFF_V_DIM + 1, :]
        o = on[:, :tq] - lam * on[:, tq:]
        ms = jnp.mean(o * o, axis=0, keepdims=True)
        o = o * lax.rsqrt(ms + EPS) * gsub_ref[...] * (1.0 - LAMBDA_INIT)
        o_ref[:, hl(h)] = o.T.astype(BF16)


def _out_router_kernel(x_ref, a_ref, gc_ref, wo_ref, gffn_ref, wrh_ref, wrl_ref, br_ref, before_ref,
                       x1_out, hpa_out, hpb_out, idx_out, gate_out, pos_out, cnt_out, carry_ref):
    tm = x_ref.shape[0]

    @pl.when(pl.program_id(0) == 0)
    def _():
        carry_ref[...] = jnp.zeros_like(carry_ref)

    sub = before_ref.shape[0]
    wr_stack = jnp.concatenate([wrh_ref[...], wrl_ref[...]], axis=0)
    eio = lax.broadcasted_iota(I32, (N_EXPERTS, sub), 0)
    before_b = before_ref[...]
    carry = carry_ref[...]
    for r in range(tm // sub):
        rows = slice(r * sub, (r + 1) * sub)
        mix = jnp.concatenate([a_ref[rows, :], gc_ref[rows, :]], axis=1)
        x1 = x_ref[rows, :] + _dot(mix, wo_ref[...])
        x1_out[rows, :] = x1
        h2 = _rms(x1, gffn_ref[...])
        hb = h2.astype(BF16)
        hpa_out[rows, :] = _pack_bf16_pairs(h2[:, :D_MODEL // 2])
        hpb_out[rows, :] = _pack_bf16_pairs(h2[:, D_MODEL // 2:])

        h_lo = (h2 - hb.astype(F32)).astype(BF16)
        both = _dot_nt(wr_stack, hb)
        logits = (both[:N_EXPERTS] + both[N_EXPERTS:]) + _dot_nt(wrh_ref[...], h_lo) + br_ref[...]

        vals, idxs, sels = [], [], []
        cur = logits
        for _ in range(TOP_K):
            m = jnp.max(cur, axis=0, keepdims=True)
            ik = jnp.min(jnp.where(cur == m, eio, N_EXPERTS), axis=0, keepdims=True)
            sel = eio == ik
            cur = jnp.where(sel, -jnp.inf, cur)
            vals.append(m)
            idxs.append(ik)
            sels.append(sel)
        es = [jnp.exp(v - vals[0]) for v in vals]
        tot = es[0] + es[1] + es[2] + es[3]
        gates = jnp.concatenate([e / tot for e in es], axis=0)
        g_hi = gates.astype(BF16)
        g_lo = (gates - g_hi.astype(F32)).astype(BF16)
        gate_out[:, rows] = jnp.concatenate(
            [g_hi, g_lo, jnp.zeros((GATE_ROWS - 2 * TOP_K, sub), BF16)], axis=0)
        idx_out[:, rows] = jnp.concatenate(idxs, axis=0)

        cnt = jnp.zeros((N_EXPERTS, sub), F32)
        for sel in sels:
            cnt = cnt + jnp.where(sel, 1.0, 0.0)
        base = carry + _dot(cnt.astype(BF16), before_b)
        pos_out[:, rows] = jnp.concatenate(
            [jnp.sum(jnp.where(sel, base, 0.0), axis=0, keepdims=True) for sel in sels], axis=0).astype(I32)
        carry = carry + jnp.sum(cnt, axis=1, keepdims=True)
    carry_ref[...] = carry
    cnt_out[...] = carry


def _dest_kernel(cnt_ref, idx_ref, pos_ref, dest_out, be_out, nbu_out):
    idx = idx_ref[...]
    dest = pos_ref[...]
    bidx = lax.broadcasted_iota(I32, be_out.shape, 1)
    be = jnp.zeros(be_out.shape, I32)
    run = jnp.int32(0)
    for e in range(N_EXPERTS):
        dest = dest + jnp.where(idx == e, run, 0)
        run = run + ((cnt_ref[e] + (ROW_BLOCK - 1)) // ROW_BLOCK) * ROW_BLOCK
        be = be + jnp.where(bidx >= run // ROW_BLOCK, 1, 0)
    dest_out[...] = dest
    be_out[...] = jnp.minimum(be, N_EXPERTS - 1)
    nbu_out[...] = jnp.zeros(nbu_out.shape, I32) + run // ROW_BLOCK


def _expert_ffn_kernel(be_ref, nbu_ref, xa_ref, xb_ref, w1_ref, b1_ref, w2_ref, b2_ref, ya_ref, yb_ref,
                       w1s_ref, w2s_ref):
    b = pl.program_id(0)

    @pl.when(b < nbu_ref[0])
    def _():
        prev = be_ref[jnp.maximum(b - 1, 0)]

        @pl.when(jnp.logical_or(b == 0, be_ref[b] != prev))
        def _():
            w1s_ref[...] = w1_ref[0].astype(BF16)
            w2s_ref[...] = w2_ref[0].astype(BF16)

        parts = _unpack_bf16_pairs(xa_ref[...]) + _unpack_bf16_pairs(xb_ref[...])
        xrow = jnp.concatenate([p.astype(BF16) for p in parts], axis=1)
        hm = _dot(xrow, w1s_ref[...]) + b1_ref[0]
        glu = jnp.minimum(hm[:, :D_FF], SWIGLU_LIMIT)
        lin = jnp.clip(hm[:, D_FF:], -SWIGLU_LIMIT, SWIGLU_LIMIT)
        act = glu * jax.nn.sigmoid(SWIGLU_ALPHA * glu) * (lin + 1.0)
        y = _dot(act.astype(BF16), w2s_ref[...]) + b2_ref[0]
        ya_ref[...] = _pack_bf16_pairs(y[:, :D_MODEL // 2])
        yb_ref[...] = _pack_bf16_pairs(y[:, D_MODEL // 2:])


SC_WINDOW = 128


def _sc_mesh():
    return plsc.VectorSubcoreMesh(core_axis_name="c", subcore_axis_name="s")


def _sc_gather_rows(table, idx_row):
    n = idx_row.shape[1]
    width = table.shape[1]

    @functools.partial(pl.kernel, out_type=jax.ShapeDtypeStruct((n, width), table.dtype), mesh=_sc_mesh(),
                       scratch_types=[])
    def gather_kernel(t_hbm, i_hbm, o_hbm):
        def body(i_vmem, o_vmem):
            pltpu.sync_copy(t_hbm.at[i_vmem.at[0]], o_vmem)

        pltpu.emit_pipeline(
            body,
            grid=(n // SC_WINDOW,),
            in_specs=[pl.BlockSpec((1, SC_WINDOW), lambda i: (0, i))],
            out_specs=[pl.BlockSpec((SC_WINDOW, width), lambda i: (i, 0))],
            core_axis_name=("c", "s"),
            dimension_semantics=(pltpu.PARALLEL,),
        )(i_hbm, o_hbm)

    return gather_kernel(table, idx_row)


def _sc_scatter_rows(rows, idx_rows, n_out):
    n, width = rows.shape

    @functools.partial(pl.kernel, out_type=jax.ShapeDtypeStruct((n_out, width), rows.dtype), mesh=_sc_mesh(),
                       scratch_types=[])
    def scatter_kernel(r_hbm, *refs):
        i_hbms, o_hbm = refs[:-1], refs[-1]

        def body(r_vmem, *i_vmems):
            for i_vmem in i_vmems:
                pltpu.sync_copy(r_vmem, o_hbm.at[i_vmem.at[0]])

        pltpu.emit_pipeline(
            body,
            grid=(n // SC_WINDOW,),
            in_specs=[pl.BlockSpec((SC_WINDOW, width), lambda i: (i, 0))]
            + [pl.BlockSpec((1, SC_WINDOW), lambda i: (0, i)) for _ in i_hbms],
            out_specs=[],
            core_axis_name=("c", "s"),
            dimension_semantics=(pltpu.PARALLEL,),
        )(r_hbm, *i_hbms)

    return scatter_kernel(rows, *idx_rows)


def _combine_kernel(x1_ref, yga_ref, ygb_ref, gate_ref, spread_ref, o_ref):
    q = D_MODEL // 4
    x1 = x1_ref[...]
    acc = [x1[:, j * q:(j + 1) * q] for j in range(4)]
    gfull = lax.dot_general(gate_ref[...], spread_ref[...], (((0,), (0,)), ((), ())), preferred_element_type=F32)
    for k in range(TOP_K):
        parts = _unpack_bf16_pairs(yga_ref[k]) + _unpack_bf16_pairs(ygb_ref[k])
        g = jnp.tile(gfull[:, k * LANES:(k + 1) * LANES], (1, q // LANES))
        acc = [a + g * p for a, p in zip(acc, parts)]
    for j in range(4):
        o_ref[:, j * q:(j + 1) * q] = acc[j]


def _block_diag_ones(width):
    r = jnp.arange(width) // HEAD_GROUP
    return (r[:, None] == r[None, :]).astype(BF16)


def _rope_tables(positions):
    half = ROT_DIM // 2
    inv_freq = ROPE_THETA ** (-jnp.arange(0, ROT_DIM, 2, dtype=F32) / ROT_DIM)
    ang = inv_freq[:, None] * positions.astype(F32).reshape(1, -1)
    cs = jnp.concatenate([jnp.cos(ang), jnp.sin(ang)], axis=0)
    cs_hi = cs.astype(BF16)
    cs_lo = (cs - cs_hi.astype(F32)).astype(BF16)
    lane = jnp.arange(LANES) % HEAD_GROUP
    j = jnp.arange(half)[:, None]
    lo_half = (lane[None, :] == j).astype(F32)
    hi_half = (lane[None, :] == j + half).astype(F32)
    spread = jnp.concatenate([
        jnp.concatenate([lo_half + hi_half, jnp.zeros((half, LANES), F32)], axis=1),
        jnp.concatenate([jnp.zeros((half, LANES), F32), hi_half - lo_half], axis=1)], axis=0)
    return jnp.concatenate([cs_hi, cs_lo], axis=0), jnp.concatenate([spread, spread], axis=0).astype(BF16)


def _full(shape):
    return pl.BlockSpec(shape, lambda *_: (0,) * len(shape))


def kernel(x, mem, positions, g_mix_norm, w_in, g_dq, g_dk, lambda_q1, lambda_k1, lambda_q2, lambda_k2, g_subln, g_sgu, w_spatial, b_spatial, g_mem_norm, w_mem_kv, g_cq, g_ck, w_out, g_ffn_norm, w_router, b_router, w_mlp1, b_mlp1, w_mlp2, b_mlp2):
    B, S, D = x.shape
    M = mem.shape[1]
    N = B * S
    tm = TOKEN_TILE
    assert D == D_MODEL and S % tm == 0 and S % ATTN_TILE == 0 and g_mix_norm.shape[0] == 1
    n_assign = N * TOP_K
    n_blocks = -(-n_assign // ROW_BLOCK) + N_EXPERTS
    n_rows = n_blocks * ROW_BLOCK
    nb_pad = -(-n_blocks // LANES) * LANES

    xf = x.reshape(N, D)
    rope_cs, rope_spread = _rope_tables(positions)
    ones256 = _block_diag_ones(MEM_W)
    row = lambda v: v.reshape(1, -1).astype(F32)
    tile_row = lambda v, reps: jnp.tile(v.reshape(1, -1).astype(F32), (1, reps))

    kT, vm = pl.pallas_call(
        _mem_kv_kernel,
        grid=(B,),
        in_specs=[pl.BlockSpec((1, M, D), lambda b: (b, 0, 0)), _full((1, D)), _full((D, 2 * MEM_W)),
                  _full((1, MEM_W)), _full((MEM_W, MEM_W))],
        out_specs=[pl.BlockSpec((1, MEM_W, M), lambda b: (b, 0, 0)), pl.BlockSpec((1, M, MEM_W), lambda b: (b, 0, 0))],
        out_shape=[jax.ShapeDtypeStruct((B, MEM_W, M), BF16), jax.ShapeDtypeStruct((B, M, MEM_W), BF16)],
        compiler_params=_cparams(("parallel",)),
        name="mem_kv",
    )(mem, row(g_mem_norm[0]), w_mem_kv[0].astype(BF16), tile_row(g_ck[0], N_MEM_HEADS), ones256)

    w_sp_lanes = jnp.transpose(w_spatial[0], (1, 0, 2)).reshape(CHUNK, GMLP_GROUPS * CHUNK)
    b_sp_lanes = jnp.repeat(b_spatial[0].T, HEAD_GROUP, axis=1)
    tiles_per_batch = S // tm
    tok = lambda w: pl.BlockSpec((tm, w), lambda i: (i, 0))
    qn, kn, vv, gc = pl.pallas_call(
        _mixer_in_kernel,
        grid=(N // tm,),
        in_specs=[tok(D), pl.BlockSpec((4 * ROT_DIM // 2, tm), lambda i: (0, i)), _full((4 * ROT_DIM // 2, 2 * LANES)),
                  _full((1, D)), _full((D, D_IN_PROJ)),
                  _full((1, DIFF_W)), _full((1, DIFF_W)),
                  _full((1, GMLP_W)), _full((CHUNK, GMLP_GROUPS * CHUNK)), _full((CHUNK, GMLP_W)),
                  _full((1, MEM_W)), _full((MEM_W, MEM_W)),
                  pl.BlockSpec((1, MEM_W, M), lambda i: (i // tiles_per_batch, 0, 0)),
                  pl.BlockSpec((1, M, MEM_W), lambda i: (i // tiles_per_batch, 0, 0))],
        out_specs=[tok(DIFF_W), tok(DIFF_W), tok(DIFF_W), tok(GMLP_W + MEM_W)],
        out_shape=[jax.ShapeDtypeStruct((N, DIFF_W), BF16)] * 3 + [jax.ShapeDtypeStruct((N, GMLP_W + MEM_W), BF16)],
        compiler_params=_cparams(("parallel",)),
        name="mixer_in",
    )(xf, rope_cs, rope_spread, row(g_mix_norm[0]), w_in[0].astype(BF16),
      tile_row(g_dq[0], 2 * N_DIFF_HEADS), tile_row(g_dk[0], 2 * N_DIFF_HEADS),
      row(g_sgu[0]), w_sp_lanes, b_sp_lanes, tile_row(g_cq[0], N_MEM_HEADS), ones256, kT, vm)

    tq = ATTN_TILE
    nq = S // tq
    head_q = pl.BlockSpec((tq, DIFF_W), lambda b, i: (b * nq + i, 0))
    head_kv = pl.BlockSpec((S, DIFF_W), lambda b, i: (b, 0))
    lam_spec = pl.BlockSpec((1, DIFF_HEAD_DIM), lambda b, i: (0, 0))
    nh = N_DIFF_HEADS
    attn = pl.pallas_call(
        _diff_attn_kernel,
        grid=(B, nq),
        in_specs=[head_q, head_kv, head_kv, lam_spec, lam_spec, lam_spec, lam_spec,
                  pl.BlockSpec((DIFF_V_DIM, 1), lambda b, i: (0, 0))],
        out_specs=head_q,
        out_shape=jax.ShapeDtypeStruct((N, DIFF_W), BF16),
        scratch_shapes=[pltpu.VMEM((nh, tq, 2 * tq), F32), pltpu.VMEM((nh, tq, 2 * tq), F32),
                        pltpu.VMEM((nh, 1, 2 * tq), F32), pltpu.VMEM((nh, ATTN_EXT_ROWS, 2 * tq), F32),
                        pltpu.VMEM((nh, ATTN_EXT_ROWS, S), BF16)],
        compiler_params=_cparams(("parallel", "arbitrary")),
        name="diff_attn",
    )(qn, kn, vv, row(lambda_q1[0]), row(lambda_k1[0]), row(lambda_q2[0]), row(lambda_k2[0]),
      g_subln[0].reshape(DIFF_V_DIM, 1).astype(F32))

    wr = w_router[0].T.astype(F32)
    wr_hi = wr.astype(BF16)
    wr_lo = (wr - wr_hi.astype(F32)).astype(BF16)
    tr = ROUTER_TILE
    assert N % tr == 0
    rtok = lambda w: pl.BlockSpec((tr, w), lambda i: (i, 0))
    tokT = lambda: pl.BlockSpec((TOP_K, tr), lambda i: (0, i))
    before = (jnp.arange(tr)[:, None] < jnp.arange(tr)[None, :]).astype(BF16)
    hw = D // 4
    x1, hpa, hpb, idxT, gateT, posT, counts = pl.pallas_call(
        _out_router_kernel,
        grid=(N // tr,),
        in_specs=[rtok(D), rtok(DIFF_W), rtok(GMLP_W + MEM_W), _full((D, D)), _full((1, D)),
                  _full((N_EXPERTS, D)), _full((N_EXPERTS, D)), _full((N_EXPERTS, 1)), _full((tr, tr))],
        out_specs=[rtok(D), rtok(hw), rtok(hw), tokT(), pl.BlockSpec((GATE_ROWS, tr), lambda i: (0, i)), tokT(),
                   _full((N_EXPERTS, 1))],
        out_shape=[jax.ShapeDtypeStruct((N, D), F32), jax.ShapeDtypeStruct((N, hw), U32),
                   jax.ShapeDtypeStruct((N, hw), U32), jax.ShapeDtypeStruct((TOP_K, N), I32),
                   jax.ShapeDtypeStruct((GATE_ROWS, N), BF16),
                   jax.ShapeDtypeStruct((TOP_K, N), I32), jax.ShapeDtypeStruct((N_EXPERTS, 1), F32)],
        scratch_shapes=[pltpu.VMEM((N_EXPERTS, 1), F32)],
        compiler_params=_cparams(("arbitrary",)),
        name="out_router",
    )(xf, attn, gc, w_out[0].astype(BF16), row(g_ffn_norm[0]), wr_hi, wr_lo, b_router[0].reshape(N_EXPERTS, 1).astype(F32), before)

    destT, block_expert, nb_used = pl.pallas_call(
        _dest_kernel,
        grid_spec=pltpu.PrefetchScalarGridSpec(
            num_scalar_prefetch=1,
            grid=(1,),
            in_specs=[pl.BlockSpec((TOP_K, N), lambda i, c: (0, 0)), pl.BlockSpec((TOP_K, N), lambda i, c: (0, 0))],
            out_specs=[pl.BlockSpec((TOP_K, N), lambda i, c: (0, 0)), pl.BlockSpec((1, nb_pad), lambda i, c: (0, 0)),
                       pl.BlockSpec((1, LANES), lambda i, c: (0, 0))],
        ),
        out_shape=[jax.ShapeDtypeStruct((TOP_K, N), I32), jax.ShapeDtypeStruct((1, nb_pad), I32),
                   jax.ShapeDtypeStruct((1, LANES), I32)],
        compiler_params=_cparams(("arbitrary",)),
        name="dest",
    )(counts.reshape(N_EXPERTS).astype(I32), idxT, posT)

    dest_rows = [destT[k].reshape(1, N) for k in range(TOP_K)]
    xa_buf = _sc_scatter_rows(hpa, dest_rows, n_rows)
    xb_buf = _sc_scatter_rows(hpb, dest_rows, n_rows)

    last = lambda b, be, nbu: jnp.minimum(b, nbu[0] - 1)
    row_blk = lambda: pl.BlockSpec((ROW_BLOCK, hw), lambda b, be, nbu: (last(b, be, nbu), 0))
    ya_buf, yb_buf = pl.pallas_call(
        _expert_ffn_kernel,
        grid_spec=pltpu.PrefetchScalarGridSpec(
            num_scalar_prefetch=2,
            grid=(n_blocks,),
            in_specs=[row_blk(), row_blk(),
                      pl.BlockSpec((1, D, 2 * D_FF), lambda b, be, nbu: (be[b], 0, 0)),
                      pl.BlockSpec((1, 1, 2 * D_FF), lambda b, be, nbu: (be[b], 0, 0)),
                      pl.BlockSpec((1, D_FF, D), lambda b, be, nbu: (be[b], 0, 0)),
                      pl.BlockSpec((1, 1, D), lambda b, be, nbu: (be[b], 0, 0))],
            out_specs=[row_blk(), row_blk()],
            scratch_shapes=[pltpu.VMEM((D, 2 * D_FF), BF16), pltpu.VMEM((D_FF, D), BF16)],
        ),
        out_shape=[jax.ShapeDtypeStruct((n_rows, hw), U32)] * 2,
        compiler_params=_cparams(("arbitrary",)),
        name="expert_ffn",
    )(block_expert[0, :n_blocks], nb_used[0, :1], xa_buf, xb_buf, w_mlp1[0],
      b_mlp1[0].reshape(N_EXPERTS, 1, 2 * D_FF), w_mlp2[0], b_mlp2[0].reshape(N_EXPERTS, 1, D))

    dest_flat = destT.reshape(1, n_assign)
    yga = _sc_gather_rows(ya_buf, dest_flat).reshape(TOP_K, N, hw)
    ygb = _sc_gather_rows(yb_buf, dest_flat).reshape(TOP_K, N, hw)

    gate_row = jnp.arange(GATE_ROWS)[:, None]
    gate_spread = ((gate_row < 2 * TOP_K) & (gate_row % TOP_K == jnp.arange(TOP_K * LANES)[None, :] // LANES)).astype(BF16)
    out = pl.pallas_call(
        _combine_kernel,
        grid=(N // tm,),
        in_specs=[tok(D), pl.BlockSpec((TOP_K, tm, hw), lambda i: (0, i, 0)),
                  pl.BlockSpec((TOP_K, tm, hw), lambda i: (0, i, 0)), pl.BlockSpec((GATE_ROWS, tm), lambda i: (0, i)),
                  _full((GATE_ROWS, TOP_K * LANES))],
        out_specs=tok(D),
        out_shape=jax.ShapeDtypeStruct((N, D), F32),
        compiler_params=_cparams(("parallel",)),
        name="combine",
    )(x1, yga, ygb, gateT, gate_spread)
    return out.reshape(B, S, D)
```

```python
import functools

import jax
import jax.numpy as jnp
from jax import lax
from jax.experimental import pallas as pl
from jax.experimental.pallas import tpu as pltpu
from jax.experimental.pallas import tpu_sc as plsc

F32 = jnp.float32
BF16 = jnp.bfloat16
I32 = jnp.int32
U32 = jnp.uint32

D_MODEL = 1024
N_DIFF_HEADS = 4
DIFF_HEAD_DIM = 64
DIFF_V_DIM = 128
DIFF_W = 512
GMLP_W = 256
GMLP_GROUPS = 4
CHUNK = 128
MEM_W = 256
N_MEM_HEADS = 4
HEAD_GROUP = 64
D_IN_PROJ = 2304
ROPE_THETA = 500000.0
ROT_DIM = 16
N_EXPERTS = 32
TOP_K = 4
D_FF = 1024
SWIGLU_LIMIT = 7.0
SWIGLU_ALPHA = 1.702
EPS = 1e-6
LAMBDA_INIT = 0.8 - 0.6

LANES = 128
ROW_BLOCK = 512
TOKEN_TILE = 512
LAYER_PARTS = 2
ROUTER_TILE = 1024
GATE_ROWS = 16
ATTN_TILE = 256
ATTN_EXT_ROWS = 128 + 16
VMEM_LIMIT = 56 * 1024 * 1024
NEG_BIG = -1e30
LOG2_E = 1.4426950408889634


def _cparams(sem):
    return pltpu.CompilerParams(dimension_semantics=sem, vmem_limit_bytes=VMEM_LIMIT)


def _dot(a, b):
    return jnp.dot(a, b, preferred_element_type=F32)


def _dot_nt(a, b):
    return lax.dot_general(a, b, (((1,), (1,)), ((), ())), preferred_element_type=F32)


def _rms(x, gain):
    ms = jnp.mean(x * x, axis=-1, keepdims=True)
    return x * lax.rsqrt(ms + EPS) * gain


def _group_rms(t, ones_bd, gain):
    w = ones_bd.shape[0]
    chunks = []
    for j in range(t.shape[1] // w):
        c = t[:, j * w:(j + 1) * w]
        ss = _dot((c * c).astype(BF16), ones_bd)
        chunks.append(c * lax.rsqrt(ss * (1.0 / HEAD_GROUP) + EPS))
    return (chunks[0] if len(chunks) == 1 else jnp.concatenate(chunks, axis=1)) * gain


def _pack_bf16_pairs(v):
    w = v.shape[1] // 2
    bits = lax.bitcast_convert_type(v.astype(BF16).astype(F32), U32)
    return (bits[:, :w] & jnp.uint32(0xFFFF0000)) | (bits[:, w:] >> jnp.uint32(16))


def _unpack_bf16_pairs(words):
    hi = lax.bitcast_convert_type(words & jnp.uint32(0xFFFF0000), F32)
    lo = lax.bitcast_convert_type(words << jnp.uint32(16), F32)
    return hi, lo


def _mem_kv_kernel(mem_ref, gmem_ref, wkv_ref, gck_ref, ones_ref, kT_ref, v_ref):
    m = _rms(mem_ref[0], gmem_ref[...]).astype(BF16)
    kv = _dot(m, wkv_ref[...])
    k = _group_rms(kv[:, :MEM_W], ones_ref[...], gck_ref[...])
    kT_ref[0] = k.T.astype(BF16)
    v_ref[0] = kv[:, MEM_W:].astype(BF16)


def _gelu_tanh(x):
    return 0.5 * x * (1.0 + jnp.tanh(0.7978845608028654 * (x + 0.044715 * (x * x * x))))


def _mixer_in_kernel(x_ref, cs_ref, spread_ref, gmix_ref, win_ref, gq_ref, gk_ref,
                     gsgu_ref, wsp_ref, bsp_ref, gcq_ref, ones256_ref, kT_ref, vm_ref,
                     q_out, k_out, v_out, gc_out):
    tm = x_ref.shape[0]
    hb = _rms(x_ref[...], gmix_ref[...]).astype(BF16)

    def proj(lo, hi):
        return _dot(hb, win_ref[:, lo:hi])

    lane = lax.broadcasted_iota(I32, (tm, LANES), 1)
    first_half = (lane % HEAD_GROUP) < (ROT_DIM // 2)
    tab = lax.dot_general(cs_ref[...], spread_ref[...], (((0,), (0,)), ((), ())), preferred_element_type=F32)
    cosb = tab[:, :LANES] + jnp.where((lane % HEAD_GROUP) >= ROT_DIM, 1.0, 0.0)
    sinb = tab[:, LANES:]

    def norm_rope(t, gain, out_ref):
        tn = _group_rms(t, ones256_ref[...], gain)
        for j in range(DIFF_W // LANES):
            c = tn[:, j * LANES:(j + 1) * LANES]
            partner = jnp.where(first_half, pltpu.roll(c, LANES - ROT_DIM // 2, 1), pltpu.roll(c, ROT_DIM // 2, 1))
            out_ref[:, j * LANES:(j + 1) * LANES] = (c * cosb + partner * sinb).astype(BF16)

    norm_rope(proj(0, DIFF_W), gq_ref[...] * (DIFF_HEAD_DIM ** -0.5 * LOG2_E), q_out)
    norm_rope(proj(DIFF_W, 2 * DIFF_W), gk_ref[...], k_out)
    v_out[...] = proj(2 * DIFF_W, 3 * DIFF_W).astype(BF16)

    z = _gelu_tanh(proj(3 * DIFF_W, 3 * DIFF_W + 2 * GMLP_W))
    u = z[:, :GMLP_W]
    vg = z[:, GMLP_W:]
    vc = vg - jnp.mean(vg, axis=-1, keepdims=True)
    vgn = (vc * lax.rsqrt(jnp.mean(vc * vc, axis=-1, keepdims=True) + EPS) * gsgu_ref[...]).astype(BF16)
    wrow = lax.broadcasted_iota(I32, (CHUNK, GMLP_GROUPS * CHUNK), 0)
    wcol = lax.broadcasted_iota(I32, (CHUNK, GMLP_GROUPS * CHUNK), 1) % CHUNK
    w_causal = jnp.where(wcol <= wrow, wsp_ref[...], 0.0).astype(BF16)
    grp = lax.broadcasted_iota(I32, (CHUNK, GMLP_W), 1) // HEAD_GROUP
    zero_b = jnp.zeros((CHUNK, GMLP_W), BF16)
    for r in range(tm // CHUNK):
        vchunk = vgn[r * CHUNK:(r + 1) * CHUNK, :]
        v_bd = jnp.concatenate([jnp.where(grp == g, vchunk, zero_b) for g in range(GMLP_GROUPS)], axis=0)
        mixed = _dot(w_causal, v_bd) + bsp_ref[...]
        gc_out[r * CHUNK:(r + 1) * CHUNK, 0:GMLP_W] = (u[r * CHUNK:(r + 1) * CHUNK, :] * mixed).astype(BF16)

    pc = proj(3 * DIFF_W + 2 * GMLP_W, D_IN_PROJ)
    qc = _group_rms(pc, ones256_ref[...], gcq_ref[...] * (HEAD_GROUP ** -0.5)).astype(BF16)
    hgrp = lax.broadcasted_iota(I32, (tm, MEM_W), 1) // HEAD_GROUP
    zero_q = jnp.zeros((tm, MEM_W), BF16)
    q_st = jnp.concatenate([jnp.where(hgrp == h, qc, zero_q) for h in range(N_MEM_HEADS)], axis=0)
    s = _dot(q_st, kT_ref[0])
    p = jnp.exp(s - jnp.max(s, axis=-1, keepdims=True))
    o = _dot(p.astype(BF16), vm_ref[0]) / jnp.sum(p, axis=-1, keepdims=True)
    c = jnp.zeros((tm, MEM_W), F32)
    for h in range(N_MEM_HEADS):
        c = c + jnp.where(hgrp == h, o[h * tm:(h + 1) * tm, :], 0.0)
    gc_out[:, GMLP_W:GMLP_W + MEM_W] = c.astype(BF16)


def _diff_attn_kernel(q_ref, k_ref, v_ref, lq1_ref, lk1_ref, lq2_ref, lk2_ref, gsub_ref, o_ref,
                      s0_ref, s1_ref, m_ref, acc_ref, vT_ref):
    tq = q_ref.shape[0]
    seq = k_ref.shape[0]
    i = pl.program_id(1)
    lane = lax.broadcasted_iota(I32, (tq, LANES), 1)
    heads = range(N_DIFF_HEADS)
    hl = lambda h: slice(h * DIFF_V_DIM, (h + 1) * DIFF_V_DIM)
    ext_rows = vT_ref.shape[1]

    @pl.when(i == 0)
    def _():
        ones_row = jnp.where(lax.broadcasted_iota(I32, (ext_rows - DIFF_V_DIM, seq), 0) == 0, 1.0, 0.0).astype(BF16)
        for h in heads:
            for c in range(seq // tq):
                vT_ref[h, 0:DIFF_V_DIM, c * tq:(c + 1) * tq] = v_ref[c * tq:(c + 1) * tq, hl(h)].T
            vT_ref[h, DIFF_V_DIM:ext_rows, :] = ones_row

    def stacked_qT(h):
        q = q_ref[:, hl(h)]
        zero = jnp.zeros_like(q)
        return jnp.concatenate([jnp.where(lane < DIFF_HEAD_DIM, q, zero), jnp.where(lane >= DIFF_HEAD_DIM, q, zero)],
                               axis=0).T

    qsT = [stacked_qT(h) for h in heads]

    def scores(t, s_ref):
        rows = pl.ds(pl.multiple_of(t * tq, tq), tq)
        for h in heads:
            s_ref[h] = _dot(k_ref[rows, hl(h)], qsT[h])

    def update(t, s_ref, causal):
        cols = pl.ds(pl.multiple_of(t * tq, tq), tq)
        for h in heads:
            s = s_ref[h]
            if causal:
                key = lax.broadcasted_iota(I32, (tq, 2 * tq), 0)
                qry = lax.broadcasted_iota(I32, (tq, 2 * tq), 1) % tq
                s = jnp.where(key <= qry, s, NEG_BIG)
            m = m_ref[h]
            m_new = jnp.maximum(m, jnp.max(s, axis=0, keepdims=True))
            alpha = jnp.exp2(m - m_new)
            p = jnp.exp2(s - m_new)
            m_ref[h] = m_new
            acc_ref[h] = alpha * acc_ref[h] + _dot(vT_ref[h, :, cols], p.astype(BF16))

    m_ref[...] = jnp.full(m_ref.shape, NEG_BIG, F32)
    acc_ref[...] = jnp.zeros(acc_ref.shape, F32)
    scores(0, s0_ref)

    def pair(pidx, carry):
        t = 2 * pidx
        scores(t + 1, s1_ref)
        update(t, s0_ref, False)
        scores(t + 2, s0_ref)
        update(t + 1, s1_ref, False)
        return carry

    lax.fori_loop(0, i // 2, pair, 0)

    @pl.when(i % 2 == 0)
    def _():
        update(i, s0_ref, True)

    @pl.when(i % 2 == 1)
    def _():
        scores(i, s1_ref)
        update(i - 1, s0_ref, False)
        update(i, s1_ref, True)

    lam = (jnp.exp(jnp.sum(lq1_ref[...] * lk1_ref[...], axis=-1, keepdims=True))
           - jnp.exp(jnp.sum(lq2_ref[...] * lk2_ref[...], axis=-1, keepdims=True)) + LAMBDA_INIT)
    for h in heads:
        on = acc_ref[h, 0:DIFF_V_DIM, :] / acc_ref[h, DIFF_V_DIM:DIFF_V_DIM + 1, :]
        o = on[:, :tq] - lam * on[:, tq:]
        ms = jnp.mean(o * o, axis=0, keepdims=True)
        o = o * lax.rsqrt(ms + EPS) * gsub_ref[...] * (1.0 - LAMBDA_INIT)
        o_ref[:, hl(h)] = o.T.astype(BF16)


def _out_router_kernel(x_ref, a_ref, gc_ref, wo_ref, gffn_ref, wrh_ref, wrl_ref, br_ref, before_ref,
                       x1_out, hpa_out, hpb_out, idx_out, gate_out, pos_out, cnt_out, carry_ref):
    tm = x_ref.shape[0]

    @pl.when(pl.program_id(0) == 0)
    def _():
        carry_ref[...] = jnp.zeros_like(carry_ref)

    sub = before_ref.shape[0]
    wr_stack = jnp.concatenate([wrh_ref[...], wrl_ref[...]], axis=0)
    eio = lax.broadcasted_iota(I32, (N_EXPERTS, sub), 0)
    before_b = before_ref[...]
    carry = carry_ref[...]
    for r in range(tm // sub):
        rows = slice(r * sub, (r + 1) * sub)
        mix = jnp.concatenate([a_ref[rows, :], gc_ref[rows, :]], axis=1)
        x1 = x_ref[rows, :] + _dot(mix, wo_ref[...])
        x1_out[rows, :] = x1
        h2 = _rms(x1, gffn_ref[...])
        hb = h2.astype(BF16)
        hpa_out[rows, :] = _pack_bf16_pairs(h2[:, :D_MODEL // 2])
        hpb_out[rows, :] = _pack_bf16_pairs(h2[:, D_MODEL // 2:])

        h_lo = (h2 - hb.astype(F32)).astype(BF16)
        both = _dot_nt(wr_stack, hb)
        logits = (both[:N_EXPERTS] + both[N_EXPERTS:]) + _dot_nt(wrh_ref[...], h_lo) + br_ref[...]

        vals, idxs, sels = [], [], []
        cur = logits
        for _ in range(TOP_K):
            m = jnp.max(cur, axis=0, keepdims=True)
            ik = jnp.min(jnp.where(cur == m, eio, N_EXPERTS), axis=0, keepdims=True)
            sel = eio == ik
            cur = jnp.where(sel, -jnp.inf, cur)
            vals.append(m)
            idxs.append(ik)
            sels.append(sel)
        es = [jnp.exp(v - vals[0]) for v in vals]
        tot = es[0] + es[1] + es[2] + es[3]
        gates = jnp.concatenate([e / tot for e in es], axis=0)
        g_hi = gates.astype(BF16)
        g_lo = (gates - g_hi.astype(F32)).astype(BF16)
        gate_out[:, rows] = jnp.concatenate(
            [g_hi, g_lo, jnp.zeros((GATE_ROWS - 2 * TOP_K, sub), BF16)], axis=0)
        idx_out[:, rows] = jnp.concatenate(idxs, axis=0)

        cnt = jnp.zeros((N_EXPERTS, sub), F32)
        for sel in sels:
            cnt = cnt + jnp.where(sel, 1.0, 0.0)
        base = carry + _dot(cnt.astype(BF16), before_b)
        pos_out[:, rows] = jnp.concatenate(
            [jnp.sum(jnp.where(sel, base, 0.0), axis=0, keepdims=True) for sel in sels], axis=0).astype(I32)
        carry = carry + jnp.sum(cnt, axis=1, keepdims=True)
    carry_ref[...] = carry
    cnt_out[...] = carry


def _dest_kernel(cnt_ref, idx_ref, pos_ref, dest_out, be_out, nbu_out):
    idx = idx_ref[...]
    dest = pos_ref[...]
    bidx = lax.broadcasted_iota(I32, be_out.shape, 1)
    be = jnp.zeros(be_out.shape, I32)
    run = jnp.int32(0)
    for e in range(N_EXPERTS):
        dest = dest + jnp.where(idx == e, run, 0)
        run = run + ((cnt_ref[e] + (ROW_BLOCK - 1)) // ROW_BLOCK) * ROW_BLOCK
        be = be + jnp.where(bidx >= run // ROW_BLOCK, 1, 0)
    dest_out[...] = dest
    be_out[...] = jnp.minimum(be, N_EXPERTS - 1)
    nbu_out[...] = jnp.zeros(nbu_out.shape, I32) + run // ROW_BLOCK


def _expert_ffn_kernel(be_ref, nbu_ref, xa_ref, xb_ref, w1_ref, b1_ref, w2_ref, b2_ref, ya_ref, yb_ref,
                       w1s_ref, w2s_ref):
    b = pl.program_id(0)

    @pl.when(b < nbu_ref[0])
    def _():
        prev = be_ref[jnp.maximum(b - 1, 0)]

        @pl.when(jnp.logical_or(b == 0, be_ref[b] != prev))
        def _():
            w1s_ref[...] = w1_ref[0].astype(BF16)
            w2s_ref[...] = w2_ref[0].astype(BF16)

        parts = _unpack_bf16_pairs(xa_ref[...]) + _unpack_bf16_pairs(xb_ref[...])
        xrow = jnp.concatenate([p.astype(BF16) for p in parts], axis=1)
        hm = _dot(xrow, w1s_ref[...]) + b1_ref[0]
        glu = jnp.minimum(hm[:, :D_FF], SWIGLU_LIMIT)
        lin = jnp.clip(hm[:, D_FF:], -SWIGLU_LIMIT, SWIGLU_LIMIT)
        act = glu * jax.nn.sigmoid(SWIGLU_ALPHA * glu) * (lin + 1.0)
        y = _dot(act.astype(BF16), w2s_ref[...]) + b2_ref[0]
        ya_ref[...] = _pack_bf16_pairs(y[:, :D_MODEL // 2])
        yb_ref[...] = _pack_bf16_pairs(y[:, D_MODEL // 2:])


SC_WINDOW = 128


def _sc_mesh():
    return plsc.VectorSubcoreMesh(core_axis_name="c", subcore_axis_name="s")


def _sc_gather_rows(table, idx_row):
    n = idx_row.shape[1]
    width = table.shape[1]

    @functools.partial(pl.kernel, out_type=jax.ShapeDtypeStruct((n, width), table.dtype), mesh=_sc_mesh(),
                       scratch_types=[])
    def gather_kernel(t_hbm, i_hbm, o_hbm):
        def body(i_vmem, o_vmem):
            pltpu.sync_copy(t_hbm.at[i_vmem.at[0]], o_vmem)

        pltpu.emit_pipeline(
            body,
            grid=(n // SC_WINDOW,),
            in_specs=[pl.BlockSpec((1, SC_WINDOW), lambda i: (0, i))],
            out_specs=[pl.BlockSpec((SC_WINDOW, width), lambda i: (i, 0))],
            core_axis_name=("c", "s"),
            dimension_semantics=(pltpu.PARALLEL,),
        )(i_hbm, o_hbm)

    return gather_kernel(table, idx_row)


def _sc_scatter_rows(rows, idx_rows, n_out):
    n, width = rows.shape

    @functools.partial(pl.kernel, out_type=jax.ShapeDtypeStruct((n_out, width), rows.dtype), mesh=_sc_mesh(),
                       scratch_types=[])
    def scatter_kernel(r_hbm, *refs):
        i_hbms, o_hbm = refs[:-1], refs[-1]

        def body(r_vmem, *i_vmems):
            for i_vmem in i_vmems:
                pltpu.sync_copy(r_vmem, o_hbm.at[i_vmem.at[0]])

        pltpu.emit_pipeline(
            body,
            grid=(n // SC_WINDOW,),
            in_specs=[pl.BlockSpec((SC_WINDOW, width), lambda i: (i, 0))]
            + [pl.BlockSpec((1, SC_WINDOW), lambda i: (0, i)) for _ in i_hbms],
            out_specs=[],
            core_axis_name=("c", "s"),
            dimension_semantics=(pltpu.PARALLEL,),
        )(r_hbm, *i_hbms)

    return scatter_kernel(rows, *idx_rows)


def _combine_kernel(x1_ref, yga_ref, ygb_ref, gate_ref, spread_ref, o_ref):
    q = D_MODEL // 4
    x1 = x1_ref[...]
    acc = [x1[:, j * q:(j + 1) * q] for j in range(4)]
    gfull = lax.dot_general(gate_ref[...], spread_ref[...], (((0,), (0,)), ((), ())), preferred_element_type=F32)
    for k in range(TOP_K):
        parts = _unpack_bf16_pairs(yga_ref[k]) + _unpack_bf16_pairs(ygb_ref[k])
        g = jnp.tile(gfull[:, k * LANES:(k + 1) * LANES], (1, q // LANES))
        acc = [a + g * p for a, p in zip(acc, parts)]
    for j in range(4):
        o_ref[:, j * q:(j + 1) * q] = acc[j]


def _combine_into_kernel(x1_ref, yga_ref, ygb_ref, gate_ref, spread_ref, prev_ref, o_ref):
    del prev_ref
    _combine_kernel(x1_ref, yga_ref, ygb_ref, gate_ref, spread_ref, o_ref)


def _block_diag_ones(width):
    r = jnp.arange(width) // HEAD_GROUP
    return (r[:, None] == r[None, :]).astype(BF16)


def _rope_tables(positions):
    half = ROT_DIM // 2
    inv_freq = ROPE_THETA ** (-jnp.arange(0, ROT_DIM, 2, dtype=F32) / ROT_DIM)
    ang = inv_freq[:, None] * positions.astype(F32).reshape(1, -1)
    cs = jnp.concatenate([jnp.cos(ang), jnp.sin(ang)], axis=0)
    cs_hi = cs.astype(BF16)
    cs_lo = (cs - cs_hi.astype(F32)).astype(BF16)
    lane = jnp.arange(LANES) % HEAD_GROUP
    j = jnp.arange(half)[:, None]
    lo_half = (lane[None, :] == j).astype(F32)
    hi_half = (lane[None, :] == j + half).astype(F32)
    spread = jnp.concatenate([
        jnp.concatenate([lo_half + hi_half, jnp.zeros((half, LANES), F32)], axis=1),
        jnp.concatenate([jnp.zeros((half, LANES), F32), hi_half - lo_half], axis=1)], axis=0)
    return jnp.concatenate([cs_hi, cs_lo], axis=0), jnp.concatenate([spread, spread], axis=0).astype(BF16)


def _full(shape):
    return pl.BlockSpec(shape, lambda *_: (0,) * len(shape))


def kernel(x, mem, positions, g_mix_norm, w_in, g_dq, g_dk, lambda_q1, lambda_k1, lambda_q2, lambda_k2, g_subln, g_sgu, w_spatial, b_spatial, g_mem_norm, w_mem_kv, g_cq, g_ck, w_out, g_ffn_norm, w_router, b_router, w_mlp1, b_mlp1, w_mlp2, b_mlp2):
    B, S, D = x.shape
    M = mem.shape[1]
    tm = TOKEN_TILE
    tr = ROUTER_TILE
    assert D == D_MODEL and S % tm == 0 and S % ATTN_TILE == 0 and g_mix_norm.shape[0] == 1
    n_parts = LAYER_PARTS if B % LAYER_PARTS == 0 and (B // LAYER_PARTS) * S % tr == 0 else 1
    Bp = B // n_parts
    N = Bp * S
    n_assign = N * TOP_K
    n_blocks = -(-n_assign // ROW_BLOCK) + N_EXPERTS
    n_rows = n_blocks * ROW_BLOCK
    nb_pad = -(-n_blocks // LANES) * LANES

    xf = x.reshape(B * S, D)
    rope_cs, rope_spread = _rope_tables(positions)
    ones256 = _block_diag_ones(MEM_W)
    row = lambda v: v.reshape(1, -1).astype(F32)
    tile_row = lambda v, reps: jnp.tile(v.reshape(1, -1).astype(F32), (1, reps))
    w_in_b, w_out_b, w_kv_b = w_in[0].astype(BF16), w_out[0].astype(BF16), w_mem_kv[0].astype(BF16)
    w_sp_lanes = jnp.transpose(w_spatial[0], (1, 0, 2)).reshape(CHUNK, GMLP_GROUPS * CHUNK)
    b_sp_lanes = jnp.repeat(b_spatial[0].T, HEAD_GROUP, axis=1)
    wr = w_router[0].T.astype(F32)
    wr_hi = wr.astype(BF16)
    wr_lo = (wr - wr_hi.astype(F32)).astype(BF16)
    before = (jnp.arange(tr)[:, None] < jnp.arange(tr)[None, :]).astype(BF16)
    gate_row = jnp.arange(GATE_ROWS)[:, None]
    gate_spread = ((gate_row < 2 * TOP_K) & (gate_row % TOP_K == jnp.arange(TOP_K * LANES)[None, :] // LANES)).astype(BF16)
    b1r = b_mlp1[0].reshape(N_EXPERTS, 1, 2 * D_FF)
    b2r = b_mlp2[0].reshape(N_EXPERTS, 1, D)

    out = None
    for part in range(n_parts):
        b_off = part * Bp
        t_off = part * (N // tm)
        r_off = part * (N // tr)

        kT, vm = pl.pallas_call(
            _mem_kv_kernel,
            grid=(Bp,),
            in_specs=[pl.BlockSpec((1, M, D), lambda b: (b + b_off, 0, 0)), _full((1, D)), _full((D, 2 * MEM_W)),
                      _full((1, MEM_W)), _full((MEM_W, MEM_W))],
            out_specs=[pl.BlockSpec((1, MEM_W, M), lambda b: (b, 0, 0)), pl.BlockSpec((1, M, MEM_W), lambda b: (b, 0, 0))],
            out_shape=[jax.ShapeDtypeStruct((Bp, MEM_W, M), BF16), jax.ShapeDtypeStruct((Bp, M, MEM_W), BF16)],
            compiler_params=_cparams(("parallel",)),
            name="mem_kv",
        )(mem, row(g_mem_norm[0]), w_kv_b, tile_row(g_ck[0], N_MEM_HEADS), ones256)

        tiles_per_batch = S // tm
        tok = lambda w: pl.BlockSpec((tm, w), lambda i: (i, 0))
        tok_in = lambda w: pl.BlockSpec((tm, w), lambda i: (i + t_off, 0))
        qn, kn, vv, gc = pl.pallas_call(
            _mixer_in_kernel,
            grid=(N // tm,),
            in_specs=[tok_in(D), pl.BlockSpec((4 * ROT_DIM // 2, tm), lambda i: (0, i + t_off)),
                      _full((4 * ROT_DIM // 2, 2 * LANES)),
                      _full((1, D)), _full((D, D_IN_PROJ)),
                      _full((1, DIFF_W)), _full((1, DIFF_W)),
                      _full((1, GMLP_W)), _full((CHUNK, GMLP_GROUPS * CHUNK)), _full((CHUNK, GMLP_W)),
                      _full((1, MEM_W)), _full((MEM_W, MEM_W)),
                      pl.BlockSpec((1, MEM_W, M), lambda i: (i // tiles_per_batch, 0, 0)),
                      pl.BlockSpec((1, M, MEM_W), lambda i: (i // tiles_per_batch, 0, 0))],
            out_specs=[tok(DIFF_W), tok(DIFF_W), tok(DIFF_W), tok(GMLP_W + MEM_W)],
            out_shape=[jax.ShapeDtypeStruct((N, DIFF_W), BF16)] * 3 + [jax.ShapeDtypeStruct((N, GMLP_W + MEM_W), BF16)],
            compiler_params=_cparams(("parallel",)),
            name="mixer_in",
        )(xf, rope_cs, rope_spread, row(g_mix_norm[0]), w_in_b,
          tile_row(g_dq[0], 2 * N_DIFF_HEADS), tile_row(g_dk[0], 2 * N_DIFF_HEADS),
          row(g_sgu[0]), w_sp_lanes, b_sp_lanes, tile_row(g_cq[0], N_MEM_HEADS), ones256, kT, vm)

        tq = ATTN_TILE
        nq = S // tq
        head_q = pl.BlockSpec((tq, DIFF_W), lambda b, i: (b * nq + i, 0))
        head_kv = pl.BlockSpec((S, DIFF_W), lambda b, i: (b, 0))
        lam_spec = pl.BlockSpec((1, DIFF_HEAD_DIM), lambda b, i: (0, 0))
        nh = N_DIFF_HEADS
        attn = pl.pallas_call(
            _diff_attn_kernel,
            grid=(Bp, nq),
            in_specs=[head_q, head_kv, head_kv, lam_spec, lam_spec, lam_spec, lam_spec,
                      pl.BlockSpec((DIFF_V_DIM, 1), lambda b, i: (0, 0))],
            out_specs=head_q,
            out_shape=jax.ShapeDtypeStruct((N, DIFF_W), BF16),
            scratch_shapes=[pltpu.VMEM((nh, tq, 2 * tq), F32), pltpu.VMEM((nh, tq, 2 * tq), F32),
                            pltpu.VMEM((nh, 1, 2 * tq), F32), pltpu.VMEM((nh, ATTN_EXT_ROWS, 2 * tq), F32),
                            pltpu.VMEM((nh, ATTN_EXT_ROWS, S), BF16)],
            compiler_params=_cparams(("parallel", "arbitrary")),
            name="diff_attn",
        )(qn, kn, vv, row(lambda_q1[0]), row(lambda_k1[0]), row(lambda_q2[0]), row(lambda_k2[0]),
          g_subln[0].reshape(DIFF_V_DIM, 1).astype(F32))

        rtok = lambda w: pl.BlockSpec((tr, w), lambda i: (i, 0))
        rtok_in = lambda w: pl.BlockSpec((tr, w), lambda i: (i + r_off, 0))
        tokT = lambda: pl.BlockSpec((TOP_K, tr), lambda i: (0, i))
        hw = D // 4
        x1, hpa, hpb, idxT, gateT, posT, counts = pl.pallas_call(
            _out_router_kernel,
            grid=(N // tr,),
            in_specs=[rtok_in(D), rtok(DIFF_W), rtok(GMLP_W + MEM_W), _full((D, D)), _full((1, D)),
                      _full((N_EXPERTS, D)), _full((N_EXPERTS, D)), _full((N_EXPERTS, 1)), _full((tr, tr))],
            out_specs=[rtok(D), rtok(hw), rtok(hw), tokT(), pl.BlockSpec((GATE_ROWS, tr), lambda i: (0, i)), tokT(),
                       _full((N_EXPERTS, 1))],
            out_shape=[jax.ShapeDtypeStruct((N, D), F32), jax.ShapeDtypeStruct((N, hw), U32),
                       jax.ShapeDtypeStruct((N, hw), U32), jax.ShapeDtypeStruct((TOP_K, N), I32),
                       jax.ShapeDtypeStruct((GATE_ROWS, N), BF16),
                       jax.ShapeDtypeStruct((TOP_K, N), I32), jax.ShapeDtypeStruct((N_EXPERTS, 1), F32)],
            scratch_shapes=[pltpu.VMEM((N_EXPERTS, 1), F32)],
            compiler_params=_cparams(("arbitrary",)),
            name="out_router",
        )(xf, attn, gc, w_out_b, row(g_ffn_norm[0]), wr_hi, wr_lo, b_router[0].reshape(N_EXPERTS, 1).astype(F32), before)

        destT, block_expert, nb_used = pl.pallas_call(
            _dest_kernel,
            grid_spec=pltpu.PrefetchScalarGridSpec(
                num_scalar_prefetch=1,
                grid=(1,),
                in_specs=[pl.BlockSpec((TOP_K, N), lambda i, c: (0, 0)), pl.BlockSpec((TOP_K, N), lambda i, c: (0, 0))],
                out_specs=[pl.BlockSpec((TOP_K, N), lambda i, c: (0, 0)), pl.BlockSpec((1, nb_pad), lambda i, c: (0, 0)),
                           pl.BlockSpec((1, LANES), lambda i, c: (0, 0))],
            ),
            out_shape=[jax.ShapeDtypeStruct((TOP_K, N), I32), jax.ShapeDtypeStruct((1, nb_pad), I32),
                       jax.ShapeDtypeStruct((1, LANES), I32)],
            compiler_params=_cparams(("arbitrary",)),
            name="dest",
        )(counts.reshape(N_EXPERTS).astype(I32), idxT, posT)

        dest_rows = [destT[k].reshape(1, N) for k in range(TOP_K)]
        xa_buf = _sc_scatter_rows(hpa, dest_rows, n_rows)
        xb_buf = _sc_scatter_rows(hpb, dest_rows, n_rows)

        last = lambda b, be, nbu: jnp.minimum(b, nbu[0] - 1)
        row_blk = lambda: pl.BlockSpec((ROW_BLOCK, hw), lambda b, be, nbu: (last(b, be, nbu), 0))
        ya_buf, yb_buf = pl.pallas_call(
            _expert_ffn_kernel,
            grid_spec=pltpu.PrefetchScalarGridSpec(
                num_scalar_prefetch=2,
                grid=(n_blocks,),
                in_specs=[row_blk(), row_blk(),
                          pl.BlockSpec((1, D, 2 * D_FF), lambda b, be, nbu: (be[b], 0, 0)),
                          pl.BlockSpec((1, 1, 2 * D_FF), lambda b, be, nbu: (be[b], 0, 0)),
                          pl.BlockSpec((1, D_FF, D), lambda b, be, nbu: (be[b], 0, 0)),
                          pl.BlockSpec((1, 1, D), lambda b, be, nbu: (be[b], 0, 0))],
                out_specs=[row_blk(), row_blk()],
                scratch_shapes=[pltpu.VMEM((D, 2 * D_FF), BF16), pltpu.VMEM((D_FF, D), BF16)],
            ),
            out_shape=[jax.ShapeDtypeStruct((n_rows, hw), U32)] * 2,
            compiler_params=_cparams(("arbitrary",)),
            name="expert_ffn",
        )(block_expert[0, :n_blocks], nb_used[0, :1], xa_buf, xb_buf, w_mlp1[0], b1r, w_mlp2[0], b2r)

        dest_flat = destT.reshape(1, n_assign)
        yga = _sc_gather_rows(ya_buf, dest_flat).reshape(TOP_K, N, hw)
        ygb = _sc_gather_rows(yb_buf, dest_flat).reshape(TOP_K, N, hw)

        prev = () if out is None else (out,)
        out = pl.pallas_call(
            _combine_kernel if out is None else _combine_into_kernel,
            grid=(N // tm,),
            in_specs=[tok(D), pl.BlockSpec((TOP_K, tm, hw), lambda i: (0, i, 0)),
                      pl.BlockSpec((TOP_K, tm, hw), lambda i: (0, i, 0)), pl.BlockSpec((GATE_ROWS, tm), lambda i: (0, i)),
                      _full((GATE_ROWS, TOP_K * LANES))] + [pl.BlockSpec(memory_space=pl.ANY)] * len(prev),
            out_specs=tok_in(D),
            out_shape=jax.ShapeDtypeStruct((B * S, D), F32),
            input_output_aliases={5: 0} if prev else {},
            compiler_params=_cparams(("parallel",)),
            name="combine",
        )(x1, yga, ygb, gateT, gate_spread, *prev)
    return out.reshape(B, S, D)
```

```python
import functools

import jax
import jax.numpy as jnp
from jax import lax
from jax.experimental import pallas as pl
from jax.experimental.pallas import tpu as pltpu
from jax.experimental.pallas import tpu_sc as plsc

F32 = jnp.float32
BF16 = jnp.bfloat16
I32 = jnp.int32
U32 = jnp.uint32

D_MODEL = 1024
N_DIFF_HEADS = 4
DIFF_HEAD_DIM = 64
DIFF_V_DIM = 128
DIFF_W = 512
GMLP_W = 256
GMLP_GROUPS = 4
CHUNK = 128
MEM_W = 256
N_MEM_HEADS = 4
HEAD_GROUP = 64
D_IN_PROJ = 2304
ROPE_THETA = 500000.0
ROT_DIM = 16
N_EXPERTS = 32
TOP_K = 4
D_FF = 1024
SWIGLU_LIMIT = 7.0
SWIGLU_ALPHA = 1.702
EPS = 1e-6
LAMBDA_INIT = 0.8 - 0.6

LANES = 128
ROW_BLOCK = 512
TOKEN_TILE = 512
LAYER_PARTS = 2
ROUTER_TILE = 1024
GATE_ROWS = 16
ATTN_TILE = 256
ATTN_EXT_ROWS = 128 + 16
VMEM_LIMIT = 56 * 1024 * 1024
NEG_BIG = -1e30
LOG2_E = 1.4426950408889634


def _cparams(sem):
    return pltpu.CompilerParams(dimension_semantics=sem, vmem_limit_bytes=VMEM_LIMIT)


def _dot(a, b):
    return jnp.dot(a, b, preferred_element_type=F32)


def _dot_nt(a, b):
    return lax.dot_general(a, b, (((1,), (1,)), ((), ())), preferred_element_type=F32)


def _rms(x, gain):
    ms = jnp.mean(x * x, axis=-1, keepdims=True)
    return x * lax.rsqrt(ms + EPS) * gain


def _group_rms(t, ones_bd, gain):
    w = ones_bd.shape[0]
    chunks = []
    for j in range(t.shape[1] // w):
        c = t[:, j * w:(j + 1) * w]
        ss = _dot((c * c).astype(BF16), ones_bd)
        chunks.append(c * lax.rsqrt(ss * (1.0 / HEAD_GROUP) + EPS))
    return (chunks[0] if len(chunks) == 1 else jnp.concatenate(chunks, axis=1)) * gain


def _pack_bf16_pairs(v):
    w = v.shape[1] // 2
    bits = lax.bitcast_convert_type(v.astype(BF16).astype(F32), U32)
    return (bits[:, :w] & jnp.uint32(0xFFFF0000)) | (bits[:, w:] >> jnp.uint32(16))


def _unpack_bf16_pairs(words):
    hi = lax.bitcast_convert_type(words & jnp.uint32(0xFFFF0000), F32)
    lo = lax.bitcast_convert_type(words << jnp.uint32(16), F32)
    return hi, lo


def _mem_kv_kernel(mem_ref, gmem_ref, wkv_ref, gck_ref, ones_ref, kT_ref, v_ref):
    m = _rms(mem_ref[0], gmem_ref[...]).astype(BF16)
    kv = _dot(m, wkv_ref[...])
    k = _group_rms(kv[:, :MEM_W], ones_ref[...], gck_ref[...])
    kT_ref[0] = k.T.astype(BF16)
    v_ref[0] = kv[:, MEM_W:].astype(BF16)


def _gelu_tanh(x):
    return 0.5 * x * (1.0 + jnp.tanh(0.7978845608028654 * (x + 0.044715 * (x * x * x))))


def _mixer_in_kernel(x_ref, cs_ref, spread_ref, gmix_ref, win_ref, gq_ref, gk_ref,
                     gsgu_ref, wsp_ref, bsp_ref, gcq_ref, ones256_ref, kT_ref, vm_ref,
                     q_out, k_out, v_out, gc_out):
    tm = x_ref.shape[0]
    hb = _rms(x_ref[...], gmix_ref[...]).astype(BF16)

    def proj(lo, hi):
        return _dot(hb, win_ref[:, lo:hi])

    lane = lax.broadcasted_iota(I32, (tm, LANES), 1)
    first_half = (lane % HEAD_GROUP) < (ROT_DIM // 2)
    tab = lax.dot_general(cs_ref[...], spread_ref[...], (((0,), (0,)), ((), ())), preferred_element_type=F32)
    cosb = tab[:, :LANES] + jnp.where((lane % HEAD_GROUP) >= ROT_DIM, 1.0, 0.0)
    sinb = tab[:, LANES:]

    def norm_rope(t, gain, out_ref):
        tn = _group_rms(t, ones256_ref[...], gain)
        for j in range(DIFF_W // LANES):
            c = tn[:, j * LANES:(j + 1) * LANES]
            partner = jnp.where(first_half, pltpu.roll(c, LANES - ROT_DIM // 2, 1), pltpu.roll(c, ROT_DIM // 2, 1))
            out_ref[:, j * LANES:(j + 1) * LANES] = (c * cosb + partner * sinb).astype(BF16)

    norm_rope(proj(0, DIFF_W), gq_ref[...] * (DIFF_HEAD_DIM ** -0.5 * LOG2_E), q_out)
    norm_rope(proj(DIFF_W, 2 * DIFF_W), gk_ref[...], k_out)
    v_out[...] = proj(2 * DIFF_W, 3 * DIFF_W).astype(BF16)

    z = _gelu_tanh(proj(3 * DIFF_W, 3 * DIFF_W + 2 * GMLP_W))
    u = z[:, :GMLP_W]
    vg = z[:, GMLP_W:]
    vc = vg - jnp.mean(vg, axis=-1, keepdims=True)
    vgn = (vc * lax.rsqrt(jnp.mean(vc * vc, axis=-1, keepdims=True) + EPS) * gsgu_ref[...]).astype(BF16)
    wrow = lax.broadcasted_iota(I32, (CHUNK, GMLP_GROUPS * CHUNK), 0)
    wcol = lax.broadcasted_iota(I32, (CHUNK, GMLP_GROUPS * CHUNK), 1) % CHUNK
    w_causal = jnp.where(wcol <= wrow, wsp_ref[...], 0.0).astype(BF16)
    grp = lax.broadcasted_iota(I32, (CHUNK, GMLP_W), 1) // HEAD_GROUP
    zero_b = jnp.zeros((CHUNK, GMLP_W), BF16)
    for r in range(tm // CHUNK):
        vchunk = vgn[r * CHUNK:(r + 1) * CHUNK, :]
        v_bd = jnp.concatenate([jnp.where(grp == g, vchunk, zero_b) for g in range(GMLP_GROUPS)], axis=0)
        mixed = _dot(w_causal, v_bd) + bsp_ref[...]
        gc_out[r * CHUNK:(r + 1) * CHUNK, 0:GMLP_W] = (u[r * CHUNK:(r + 1) * CHUNK, :] * mixed).astype(BF16)

    pc = proj(3 * DIFF_W + 2 * GMLP_W, D_IN_PROJ)
    qc = _group_rms(pc, ones256_ref[...], gcq_ref[...] * (HEAD_GROUP ** -0.5)).astype(BF16)
    hgrp = lax.broadcasted_iota(I32, (tm, MEM_W), 1) // HEAD_GROUP
    zero_q = jnp.zeros((tm, MEM_W), BF16)
    q_st = jnp.concatenate([jnp.where(hgrp == h, qc, zero_q) for h in range(N_MEM_HEADS)], axis=0)
    s = _dot(q_st, kT_ref[0])
    p = jnp.exp(s - jnp.max(s, axis=-1, keepdims=True))
    o = _dot(p.astype(BF16), vm_ref[0]) / jnp.sum(p, axis=-1, keepdims=True)
    c = jnp.zeros((tm, MEM_W), F32)
    for h in range(N_MEM_HEADS):
        c = c + jnp.where(hgrp == h, o[h * tm:(h + 1) * tm, :], 0.0)
    gc_out[:, GMLP_W:GMLP_W + MEM_W] = c.astype(BF16)


def _diff_attn_kernel(q_ref, k_ref, v_ref, lq1_ref, lk1_ref, lq2_ref, lk2_ref, gsub_ref, o_ref,
                      s0_ref, s1_ref, m_ref, acc_ref, vT_ref):
    tq = q_ref.shape[0]
    seq = k_ref.shape[0]
    i = pl.program_id(1)
    lane = lax.broadcasted_iota(I32, (tq, LANES), 1)
    heads = range(N_DIFF_HEADS)
    hl = lambda h: slice(h * DIFF_V_DIM, (h + 1) * DIFF_V_DIM)
    ext_rows = vT_ref.shape[1]

    @pl.when(i == 0)
    def _():
        ones_row = jnp.where(lax.broadcasted_iota(I32, (ext_rows - DIFF_V_DIM, seq), 0) == 0, 1.0, 0.0).astype(BF16)
        for h in heads:
            for c in range(seq // tq):
                vT_ref[h, 0:DIFF_V_DIM, c * tq:(c + 1) * tq] = v_ref[c * tq:(c + 1) * tq, hl(h)].T
            vT_ref[h, DIFF_V_DIM:ext_rows, :] = ones_row

    def stacked_qT(h):
        q = q_ref[:, hl(h)]
        zero = jnp.zeros_like(q)
        return jnp.concatenate([jnp.where(lane < DIFF_HEAD_DIM, q, zero), jnp.where(lane >= DIFF_HEAD_DIM, q, zero)],
                               axis=0).T

    qsT = [stacked_qT(h) for h in heads]

    def scores(t, s_ref):
        rows = pl.ds(pl.multiple_of(t * tq, tq), tq)
        for h in heads:
            s_ref[h] = _dot(k_ref[rows, hl(h)], qsT[h])

    def update(t, s_ref, causal):
        cols = pl.ds(pl.multiple_of(t * tq, tq), tq)
        for h in heads:
            s = s_ref[h]
            if causal:
                key = lax.broadcasted_iota(I32, (tq, 2 * tq), 0)
                qry = lax.broadcasted_iota(I32, (tq, 2 * tq), 1) % tq
                s = jnp.where(key <= qry, s, NEG_BIG)
            m = m_ref[h]
            m_new = jnp.maximum(m, jnp.max(s, axis=0, keepdims=True))
            alpha = jnp.exp2(m - m_new)
            p = jnp.exp2(s - m_new)
            m_ref[h] = m_new
            acc_ref[h] = alpha * acc_ref[h] + _dot(vT_ref[h, :, cols], p.astype(BF16))

    m_ref[...] = jnp.full(m_ref.shape, NEG_BIG, F32)
    acc_ref[...] = jnp.zeros(acc_ref.shape, F32)
    scores(0, s0_ref)

    def pair(pidx, carry):
        t = 2 * pidx
        scores(t + 1, s1_ref)
        update(t, s0_ref, False)
        scores(t + 2, s0_ref)
        update(t + 1, s1_ref, False)
        return carry

    lax.fori_loop(0, i // 2, pair, 0)

    @pl.when(i % 2 == 0)
    def _():
        update(i, s0_ref, True)

    @pl.when(i % 2 == 1)
    def _():
        scores(i, s1_ref)
        update(i - 1, s0_ref, False)
        update(i, s1_ref, True)

    lam = (jnp.exp(jnp.sum(lq1_ref[...] * lk1_ref[...], axis=-1, keepdims=True))
           - jnp.exp(jnp.sum(lq2_ref[...] * lk2_ref[...], axis=-1, keepdims=True)) + LAMBDA_INIT)
    for h in heads:
        on = acc_ref[h, 0:DIFF_V_DIM, :] / acc_ref[h, DIFF_V_DIM:DIFF_V_DIM + 1, :]
        o = on[:, :tq] - lam * on[:, tq:]
        ms = jnp.mean(o * o, axis=0, keepdims=True)
        o = o * lax.rsqrt(ms + EPS) * gsub_ref[...] * (1.0 - LAMBDA_INIT)
        o_ref[:, hl(h)] = o.T.astype(BF16)


def _out_router_kernel(x_ref, a_ref, gc_ref, wo_ref, gffn_ref, wrh_ref, wrl_ref, br_ref, before_ref,
                       x1_out, hpa_out, hpb_out, idx_out, gate_out, pos_out, cnt_out, carry_ref):
    tm = x_ref.shape[0]

    @pl.when(pl.program_id(0) == 0)
    def _():
        carry_ref[...] = jnp.zeros_like(carry_ref)

    sub = before_ref.shape[0]
    wr_stack = jnp.concatenate([wrh_ref[...], wrl_ref[...]], axis=0)
    eio = lax.broadcasted_iota(I32, (N_EXPERTS, sub), 0)
    before_b = before_ref[...]
    carry = carry_ref[...]
    for r in range(tm // sub):
        rows = slice(r * sub, (r + 1) * sub)
        mix = jnp.concatenate([a_ref[rows, :], gc_ref[rows, :]], axis=1)
        x1 = x_ref[rows, :] + _dot(mix, wo_ref[...])
        x1_out[rows, :] = x1
        h2 = _rms(x1, gffn_ref[...])
        hb = h2.astype(BF16)
        hpa_out[rows, :] = _pack_bf16_pairs(h2[:, :D_MODEL // 2])
        hpb_out[rows, :] = _pack_bf16_pairs(h2[:, D_MODEL // 2:])

        h_lo = (h2 - hb.astype(F32)).astype(BF16)
        both = _dot_nt(wr_stack, hb)
        logits = (both[:N_EXPERTS] + both[N_EXPERTS:]) + _dot_nt(wrh_ref[...], h_lo) + br_ref[...]

        vals, idxs, sels = [], [], []
        cur = logits
        for _ in range(TOP_K):
            m = jnp.max(cur, axis=0, keepdims=True)
            ik = jnp.min(jnp.where(cur == m, eio, N_EXPERTS), axis=0, keepdims=True)
            sel = eio == ik
            cur = jnp.where(sel, -jnp.inf, cur)
            vals.append(m)
            idxs.append(ik)
            sels.append(sel)
        es = [jnp.exp(v - vals[0]) for v in vals]
        tot = es[0] + es[1] + es[2] + es[3]
        gates = jnp.concatenate([e / tot for e in es], axis=0)
        g_hi = gates.astype(BF16)
        g_lo = (gates - g_hi.astype(F32)).astype(BF16)
        gate_out[:, rows] = jnp.concatenate(
            [g_hi, g_lo, jnp.zeros((GATE_ROWS - 2 * TOP_K, sub), BF16)], axis=0)
        idx_out[:, rows] = jnp.concatenate(idxs, axis=0)

        cnt = jnp.zeros((N_EXPERTS, sub), F32)
        for sel in sels:
            cnt = cnt + jnp.where(sel, 1.0, 0.0)
        base = carry + _dot(cnt.astype(BF16), before_b)
        pos_out[:, rows] = jnp.concatenate(
            [jnp.sum(jnp.where(sel, base, 0.0), axis=0, keepdims=True) for sel in sels], axis=0).astype(I32)
        carry = carry + jnp.sum(cnt, axis=1, keepdims=True)
    carry_ref[...] = carry
    cnt_out[...] = carry


def _dest_kernel(cnt_ref, idx_ref, pos_ref, dest_out, be_out, nbu_out):
    idx = idx_ref[...]
    dest = pos_ref[...]
    bidx = lax.broadcasted_iota(I32, be_out.shape, 1)
    be = jnp.zeros(be_out.shape, I32)
    run = jnp.int32(0)
    for e in range(N_EXPERTS):
        dest = dest + jnp.where(idx == e, run, 0)
        run = run + ((cnt_ref[e] + (ROW_BLOCK - 1)) // ROW_BLOCK) * ROW_BLOCK
        be = be + jnp.where(bidx >= run // ROW_BLOCK, 1, 0)
    dest_out[...] = dest
    be_out[...] = jnp.minimum(be, N_EXPERTS - 1)
    nbu_out[...] = jnp.zeros(nbu_out.shape, I32) + run // ROW_BLOCK


def _expert_ffn_kernel(be_ref, nbu_ref, nxt_ref, xa_ref, xb_ref, w1_hbm, b1_ref, w2_hbm, b2_ref, ya_ref, yb_ref,
                       w1f_ref, w2f_ref, w1s_ref, w2s_ref, slot_ref, sem_ref):
    b = pl.program_id(0)

    def weight_copies(e, slot):
        return (pltpu.make_async_copy(w1_hbm.at[e], w1f_ref.at[slot], sem_ref.at[slot, 0]),
                pltpu.make_async_copy(w2_hbm.at[e], w2f_ref.at[slot], sem_ref.at[slot, 1]))

    @pl.when(b < nbu_ref[0])
    def _():
        e = be_ref[b]

        @pl.when(b == 0)
        def _():
            slot_ref[0] = 0
            for cp in weight_copies(e, 0):
                cp.start()

        @pl.when(jnp.logical_or(b == 0, e != be_ref[jnp.maximum(b - 1, 0)]))
        def _():
            slot = slot_ref[0]
            for cp in weight_copies(e, slot):
                cp.wait()
            w1s_ref[...] = w1f_ref[slot].astype(BF16)
            w2s_ref[...] = w2f_ref[slot].astype(BF16)
            nxt = nxt_ref[e]

            @pl.when(nxt >= 0)
            def _():
                for cp in weight_copies(nxt, 1 - slot):
                    cp.start()

            slot_ref[0] = 1 - slot

        parts = _unpack_bf16_pairs(xa_ref[...]) + _unpack_bf16_pairs(xb_ref[...])
        xrow = jnp.concatenate([p.astype(BF16) for p in parts], axis=1)
        hm = _dot(xrow, w1s_ref[...]) + b1_ref[0]
        glu = jnp.minimum(hm[:, :D_FF], SWIGLU_LIMIT)
        lin = jnp.clip(hm[:, D_FF:], -SWIGLU_LIMIT, SWIGLU_LIMIT)
        act = glu * jax.nn.sigmoid(SWIGLU_ALPHA * glu) * (lin + 1.0)
        y = _dot(act.astype(BF16), w2s_ref[...]) + b2_ref[0]
        ya_ref[...] = _pack_bf16_pairs(y[:, :D_MODEL // 2])
        yb_ref[...] = _pack_bf16_pairs(y[:, D_MODEL // 2:])


SC_WINDOW = 128


def _sc_mesh():
    return plsc.VectorSubcoreMesh(core_axis_name="c", subcore_axis_name="s")


def _sc_gather_rows(table, idx_row):
    n = idx_row.shape[1]
    width = table.shape[1]

    @functools.partial(pl.kernel, out_type=jax.ShapeDtypeStruct((n, width), table.dtype), mesh=_sc_mesh(),
                       scratch_types=[])
    def gather_kernel(t_hbm, i_hbm, o_hbm):
        def body(i_vmem, o_vmem):
            pltpu.sync_copy(t_hbm.at[i_vmem.at[0]], o_vmem)

        pltpu.emit_pipeline(
            body,
            grid=(n // SC_WINDOW,),
            in_specs=[pl.BlockSpec((1, SC_WINDOW), lambda i: (0, i))],
            out_specs=[pl.BlockSpec((SC_WINDOW, width), lambda i: (i, 0))],
            core_axis_name=("c", "s"),
            dimension_semantics=(pltpu.PARALLEL,),
        )(i_hbm, o_hbm)

    return gather_kernel(table, idx_row)


def _sc_scatter_rows(rows, idx_rows, n_out):
    n, width = rows.shape

    @functools.partial(pl.kernel, out_type=jax.ShapeDtypeStruct((n_out, width), rows.dtype), mesh=_sc_mesh(),
                       scratch_types=[])
    def scatter_kernel(r_hbm, *refs):
        i_hbms, o_hbm = refs[:-1], refs[-1]

        def body(r_vmem, *i_vmems):
            for i_vmem in i_vmems:
                pltpu.sync_copy(r_vmem, o_hbm.at[i_vmem.at[0]])

        pltpu.emit_pipeline(
            body,
            grid=(n // SC_WINDOW,),
            in_specs=[pl.BlockSpec((SC_WINDOW, width), lambda i: (i, 0))]
            + [pl.BlockSpec((1, SC_WINDOW), lambda i: (0, i)) for _ in i_hbms],
            out_specs=[],
            core_axis_name=("c", "s"),
            dimension_semantics=(pltpu.PARALLEL,),
        )(r_hbm, *i_hbms)

    return scatter_kernel(rows, *idx_rows)


def _combine_kernel(x1_ref, yga_ref, ygb_ref, gate_ref, spread_ref, o_ref):
    q = D_MODEL // 4
    x1 = x1_ref[...]
    acc = [x1[:, j * q:(j + 1) * q] for j in range(4)]
    gfull = lax.dot_general(gate_ref[...], spread_ref[...], (((0,), (0,)), ((), ())), preferred_element_type=F32)
    for k in range(TOP_K):
        parts = _unpack_bf16_pairs(yga_ref[k]) + _unpack_bf16_pairs(ygb_ref[k])
        g = jnp.tile(gfull[:, k * LANES:(k + 1) * LANES], (1, q // LANES))
        acc = [a + g * p for a, p in zip(acc, parts)]
    for j in range(4):
        o_ref[:, j * q:(j + 1) * q] = acc[j]


def _combine_into_kernel(x1_ref, yga_ref, ygb_ref, gate_ref, spread_ref, prev_ref, o_ref):
    del prev_ref
    _combine_kernel(x1_ref, yga_ref, ygb_ref, gate_ref, spread_ref, o_ref)


def _block_diag_ones(width):
    r = jnp.arange(width) // HEAD_GROUP
    return (r[:, None] == r[None, :]).astype(BF16)


def _rope_tables(positions):
    half = ROT_DIM // 2
    inv_freq = ROPE_THETA ** (-jnp.arange(0, ROT_DIM, 2, dtype=F32) / ROT_DIM)
    ang = inv_freq[:, None] * positions.astype(F32).reshape(1, -1)
    cs = jnp.concatenate([jnp.cos(ang), jnp.sin(ang)], axis=0)
    cs_hi = cs.astype(BF16)
    cs_lo = (cs - cs_hi.astype(F32)).astype(BF16)
    lane = jnp.arange(LANES) % HEAD_GROUP
    j = jnp.arange(half)[:, None]
    lo_half = (lane[None, :] == j).astype(F32)
    hi_half = (lane[None, :] == j + half).astype(F32)
    spread = jnp.concatenate([
        jnp.concatenate([lo_half + hi_half, jnp.zeros((half, LANES), F32)], axis=1),
        jnp.concatenate([jnp.zeros((half, LANES), F32), hi_half - lo_half], axis=1)], axis=0)
    return jnp.concatenate([cs_hi, cs_lo], axis=0), jnp.concatenate([spread, spread], axis=0).astype(BF16)


def _full(shape):
    return pl.BlockSpec(shape, lambda *_: (0,) * len(shape))


def kernel(x, mem, positions, g_mix_norm, w_in, g_dq, g_dk, lambda_q1, lambda_k1, lambda_q2, lambda_k2, g_subln, g_sgu, w_spatial, b_spatial, g_mem_norm, w_mem_kv, g_cq, g_ck, w_out, g_ffn_norm, w_router, b_router, w_mlp1, b_mlp1, w_mlp2, b_mlp2):
    B, S, D = x.shape
    M = mem.shape[1]
    tm = TOKEN_TILE
    tr = ROUTER_TILE
    assert D == D_MODEL and S % tm == 0 and S % ATTN_TILE == 0 and g_mix_norm.shape[0] == 1
    n_parts = LAYER_PARTS if B % LAYER_PARTS == 0 and (B // LAYER_PARTS) * S % tr == 0 else 1
    Bp = B // n_parts
    N = Bp * S
    n_assign = N * TOP_K
    n_blocks = -(-n_assign // ROW_BLOCK) + N_EXPERTS
    n_rows = n_blocks * ROW_BLOCK
    nb_pad = -(-n_blocks // LANES) * LANES

    xf = x.reshape(B * S, D)
    rope_cs, rope_spread = _rope_tables(positions)
    ones256 = _block_diag_ones(MEM_W)
    row = lambda v: v.reshape(1, -1).astype(F32)
    tile_row = lambda v, reps: jnp.tile(v.reshape(1, -1).astype(F32), (1, reps))
    w_in_b, w_out_b, w_kv_b = w_in[0].astype(BF16), w_out[0].astype(BF16), w_mem_kv[0].astype(BF16)
    w_sp_lanes = jnp.transpose(w_spatial[0], (1, 0, 2)).reshape(CHUNK, GMLP_GROUPS * CHUNK)
    b_sp_lanes = jnp.repeat(b_spatial[0].T, HEAD_GROUP, axis=1)
    wr = w_router[0].T.astype(F32)
    wr_hi = wr.astype(BF16)
    wr_lo = (wr - wr_hi.astype(F32)).astype(BF16)
    before = (jnp.arange(tr)[:, None] < jnp.arange(tr)[None, :]).astype(BF16)
    gate_row = jnp.arange(GATE_ROWS)[:, None]
    gate_spread = ((gate_row < 2 * TOP_K) & (gate_row % TOP_K == jnp.arange(TOP_K * LANES)[None, :] // LANES)).astype(BF16)
    b1r = b_mlp1[0].reshape(N_EXPERTS, 1, 2 * D_FF)
    b2r = b_mlp2[0].reshape(N_EXPERTS, 1, D)

    out = None
    for part in range(n_parts):
        b_off = part * Bp
        t_off = part * (N // tm)
        r_off = part * (N // tr)

        kT, vm = pl.pallas_call(
            _mem_kv_kernel,
            grid=(Bp,),
            in_specs=[pl.BlockSpec((1, M, D), lambda b: (b + b_off, 0, 0)), _full((1, D)), _full((D, 2 * MEM_W)),
                      _full((1, MEM_W)), _full((MEM_W, MEM_W))],
            out_specs=[pl.BlockSpec((1, MEM_W, M), lambda b: (b, 0, 0)), pl.BlockSpec((1, M, MEM_W), lambda b: (b, 0, 0))],
            out_shape=[jax.ShapeDtypeStruct((Bp, MEM_W, M), BF16), jax.ShapeDtypeStruct((Bp, M, MEM_W), BF16)],
            compiler_params=_cparams(("parallel",)),
            name="mem_kv",
        )(mem, row(g_mem_norm[0]), w_kv_b, tile_row(g_ck[0], N_MEM_HEADS), ones256)

        tiles_per_batch = S // tm
        tok = lambda w: pl.BlockSpec((tm, w), lambda i: (i, 0))
        tok_in = lambda w: pl.BlockSpec((tm, w), lambda i: (i + t_off, 0))
        qn, kn, vv, gc = pl.pallas_call(
            _mixer_in_kernel,
            grid=(N // tm,),
            in_specs=[tok_in(D), pl.BlockSpec((4 * ROT_DIM // 2, tm), lambda i: (0, i + t_off)),
                      _full((4 * ROT_DIM // 2, 2 * LANES)),
                      _full((1, D)), _full((D, D_IN_PROJ)),
                      _full((1, DIFF_W)), _full((1, DIFF_W)),
                      _full((1, GMLP_W)), _full((CHUNK, GMLP_GROUPS * CHUNK)), _full((CHUNK, GMLP_W)),
                      _full((1, MEM_W)), _full((MEM_W, MEM_W)),
                      pl.BlockSpec((1, MEM_W, M), lambda i: (i // tiles_per_batch, 0, 0)),
                      pl.BlockSpec((1, M, MEM_W), lambda i: (i // tiles_per_batch, 0, 0))],
            out_specs=[tok(DIFF_W), tok(DIFF_W), tok(DIFF_W), tok(GMLP_W + MEM_W)],
            out_shape=[jax.ShapeDtypeStruct((N, DIFF_W), BF16)] * 3 + [jax.ShapeDtypeStruct((N, GMLP_W + MEM_W), BF16)],
            compiler_params=_cparams(("parallel",)),
            name="mixer_in",
        )(xf, rope_cs, rope_spread, row(g_mix_norm[0]), w_in_b,
          tile_row(g_dq[0], 2 * N_DIFF_HEADS), tile_row(g_dk[0], 2 * N_DIFF_HEADS),
          row(g_sgu[0]), w_sp_lanes, b_sp_lanes, tile_row(g_cq[0], N_MEM_HEADS), ones256, kT, vm)

        tq = ATTN_TILE
        nq = S // tq
        head_q = pl.BlockSpec((tq, DIFF_W), lambda b, i: (b * nq + i, 0))
        head_kv = pl.BlockSpec((S, DIFF_W), lambda b, i: (b, 0))
        lam_spec = pl.BlockSpec((1, DIFF_HEAD_DIM), lambda b, i: (0, 0))
        nh = N_DIFF_HEADS
        attn = pl.pallas_call(
            _diff_attn_kernel,
            grid=(Bp, nq),
            in_specs=[head_q, head_kv, head_kv, lam_spec, lam_spec, lam_spec, lam_spec,
                      pl.BlockSpec((DIFF_V_DIM, 1), lambda b, i: (0, 0))],
            out_specs=head_q,
            out_shape=jax.ShapeDtypeStruct((N, DIFF_W), BF16),
            scratch_shapes=[pltpu.VMEM((nh, tq, 2 * tq), F32), pltpu.VMEM((nh, tq, 2 * tq), F32),
                            pltpu.VMEM((nh, 1, 2 * tq), F32), pltpu.VMEM((nh, ATTN_EXT_ROWS, 2 * tq), F32),
                            pltpu.VMEM((nh, ATTN_EXT_ROWS, S), BF16)],
            compiler_params=_cparams(("parallel", "arbitrary")),
            name="diff_attn",
        )(qn, kn, vv, row(lambda_q1[0]), row(lambda_k1[0]), row(lambda_q2[0]), row(lambda_k2[0]),
          g_subln[0].reshape(DIFF_V_DIM, 1).astype(F32))

        rtok = lambda w: pl.BlockSpec((tr, w), lambda i: (i, 0))
        rtok_in = lambda w: pl.BlockSpec((tr, w), lambda i: (i + r_off, 0))
        tokT = lambda: pl.BlockSpec((TOP_K, tr), lambda i: (0, i))
        hw = D // 4
        x1, hpa, hpb, idxT, gateT, posT, counts = pl.pallas_call(
            _out_router_kernel,
            grid=(N // tr,),
            in_specs=[rtok_in(D), rtok(DIFF_W), rtok(GMLP_W + MEM_W), _full((D, D)), _full((1, D)),
                      _full((N_EXPERTS, D)), _full((N_EXPERTS, D)), _full((N_EXPERTS, 1)), _full((tr, tr))],
            out_specs=[rtok(D), rtok(hw), rtok(hw), tokT(), pl.BlockSpec((GATE_ROWS, tr), lambda i: (0, i)), tokT(),
                       _full((N_EXPERTS, 1))],
            out_shape=[jax.ShapeDtypeStruct((N, D), F32), jax.ShapeDtypeStruct((N, hw), U32),
                       jax.ShapeDtypeStruct((N, hw), U32), jax.ShapeDtypeStruct((TOP_K, N), I32),
                       jax.ShapeDtypeStruct((GATE_ROWS, N), BF16),
                       jax.ShapeDtypeStruct((TOP_K, N), I32), jax.ShapeDtypeStruct((N_EXPERTS, 1), F32)],
            scratch_shapes=[pltpu.VMEM((N_EXPERTS, 1), F32)],
            compiler_params=_cparams(("arbitrary",)),
            name="out_router",
        )(xf, attn, gc, w_out_b, row(g_ffn_norm[0]), wr_hi, wr_lo, b_router[0].reshape(N_EXPERTS, 1).astype(F32), before)

        destT, block_expert, nb_used = pl.pallas_call(
            _dest_kernel,
            grid_spec=pltpu.PrefetchScalarGridSpec(
                num_scalar_prefetch=1,
                grid=(1,),
                in_specs=[pl.BlockSpec((TOP_K, N), lambda i, c: (0, 0)), pl.BlockSpec((TOP_K, N), lambda i, c: (0, 0))],
                out_specs=[pl.BlockSpec((TOP_K, N), lambda i, c: (0, 0)), pl.BlockSpec((1, nb_pad), lambda i, c: (0, 0)),
                           pl.BlockSpec((1, LANES), lambda i, c: (0, 0))],
            ),
            out_shape=[jax.ShapeDtypeStruct((TOP_K, N), I32), jax.ShapeDtypeStruct((1, nb_pad), I32),
                       jax.ShapeDtypeStruct((1, LANES), I32)],
            compiler_params=_cparams(("arbitrary",)),
            name="dest",
        )(counts.reshape(N_EXPERTS).astype(I32), idxT, posT)

        dest_rows = [destT[k].reshape(1, N) for k in range(TOP_K)]
        xa_buf = _sc_scatter_rows(hpa, dest_rows, n_rows)
        xb_buf = _sc_scatter_rows(hpb, dest_rows, n_rows)

        cnt_i = counts.reshape(N_EXPERTS).astype(I32)
        owner = jnp.where(cnt_i > 0, jnp.arange(N_EXPERTS, dtype=I32), N_EXPERTS)
        later = jnp.concatenate([lax.cummin(owner[::-1])[::-1][1:], jnp.full((1,), N_EXPERTS, I32)])
        next_expert = jnp.where(later < N_EXPERTS, later, -1)
        last = lambda b, be, nbu, nxt: jnp.minimum(b, nbu[0] - 1)
        row_blk = lambda: pl.BlockSpec((ROW_BLOCK, hw), lambda b, be, nbu, nxt: (last(b, be, nbu, nxt), 0))
        ya_buf, yb_buf = pl.pallas_call(
            _expert_ffn_kernel,
            grid_spec=pltpu.PrefetchScalarGridSpec(
                num_scalar_prefetch=3,
                grid=(n_blocks,),
                in_specs=[row_blk(), row_blk(),
                          pl.BlockSpec(memory_space=pl.ANY),
                          pl.BlockSpec((1, 1, 2 * D_FF), lambda b, be, nbu, nxt: (be[b], 0, 0)),
                          pl.BlockSpec(memory_space=pl.ANY),
                          pl.BlockSpec((1, 1, D), lambda b, be, nbu, nxt: (be[b], 0, 0))],
                out_specs=[row_blk(), row_blk()],
                scratch_shapes=[pltpu.VMEM((2, D, 2 * D_FF), F32), pltpu.VMEM((2, D_FF, D), F32),
                                pltpu.VMEM((D, 2 * D_FF), BF16), pltpu.VMEM((D_FF, D), BF16),
                                pltpu.SMEM((1,), I32), pltpu.SemaphoreType.DMA((2, 2))],
            ),
            out_shape=[jax.ShapeDtypeStruct((n_rows, hw), U32)] * 2,
            compiler_params=_cparams(("arbitrary",)),
            name="expert_ffn",
        )(block_expert[0, :n_blocks], nb_used[0, :1], next_expert, xa_buf, xb_buf, w_mlp1[0], b1r, w_mlp2[0], b2r)

        dest_flat = destT.reshape(1, n_assign)
        yga = _sc_gather_rows(ya_buf, dest_flat).reshape(TOP_K, N, hw)
        ygb = _sc_gather_rows(yb_buf, dest_flat).reshape(TOP_K, N, hw)

        prev = () if out is None else (out,)
        out = pl.pallas_call(
            _combine_kernel if out is None else _combine_into_kernel,
            grid=(N // tm,),
            in_specs=[tok(D), pl.BlockSpec((TOP_K, tm, hw), lambda i: (0, i, 0)),
                      pl.BlockSpec((TOP_K, tm, hw), lambda i: (0, i, 0)), pl.BlockSpec((GATE_ROWS, tm), lambda i: (0, i)),
                      _full((GATE_ROWS, TOP_K * LANES))] + [pl.BlockSpec(memory_space=pl.ANY)] * len(prev),
            out_specs=tok_in(D),
            out_shape=jax.ShapeDtypeStruct((B * S, D), F32),
            input_output_aliases={5: 0} if prev else {},
            compiler_params=_cparams(("parallel",)),
            name="combine",
        )(x1, yga, ygb, gateT, gate_spread, *prev)
    return out.reshape(B, S, D)
```

```python
import functools

import jax
import jax.numpy as jnp
from jax import lax
from jax.experimental import pallas as pl
from jax.experimental.pallas import tpu as pltpu
from jax.experimental.pallas import tpu_sc as plsc

F32 = jnp.float32
BF16 = jnp.bfloat16
I32 = jnp.int32
U32 = jnp.uint32

D_MODEL = 1024
N_DIFF_HEADS = 4
DIFF_HEAD_DIM = 64
DIFF_V_DIM = 128
DIFF_W = 512
GMLP_W = 256
GMLP_GROUPS = 4
CHUNK = 128
MEM_W = 256
N_MEM_HEADS = 4
HEAD_GROUP = 64
D_IN_PROJ = 2304
ROPE_THETA = 500000.0
ROT_DIM = 16
N_EXPERTS = 32
TOP_K = 4
D_FF = 1024
SWIGLU_LIMIT = 7.0
SWIGLU_ALPHA = 1.702
EPS = 1e-6
LAMBDA_INIT = 0.8 - 0.6

LANES = 128
ROW_BLOCK = 512
TOKEN_TILE = 512
LAYER_PARTS = 2
TAIL_CHUNKS = 4
ROUTER_TILE = 1024
GATE_ROWS = 16
ATTN_TILE = 256
ATTN_EXT_ROWS = 128 + 16
VMEM_LIMIT = 56 * 1024 * 1024
NEG_BIG = -1e30
LOG2_E = 1.4426950408889634


def _cparams(sem):
    return pltpu.CompilerParams(dimension_semantics=sem, vmem_limit_bytes=VMEM_LIMIT)


def _dot(a, b):
    return jnp.dot(a, b, preferred_element_type=F32)


def _dot_nt(a, b):
    return lax.dot_general(a, b, (((1,), (1,)), ((), ())), preferred_element_type=F32)


def _rms(x, gain):
    ms = jnp.mean(x * x, axis=-1, keepdims=True)
    return x * lax.rsqrt(ms + EPS) * gain


def _group_rms(t, ones_bd, gain):
    w = ones_bd.shape[0]
    chunks = []
    for j in range(t.shape[1] // w):
        c = t[:, j * w:(j + 1) * w]
        ss = _dot((c * c).astype(BF16), ones_bd)
        chunks.append(c * lax.rsqrt(ss * (1.0 / HEAD_GROUP) + EPS))
    return (chunks[0] if len(chunks) == 1 else jnp.concatenate(chunks, axis=1)) * gain


def _pack_bf16_pairs(v):
    w = v.shape[1] // 2
    bits = lax.bitcast_convert_type(v.astype(BF16).astype(F32), U32)
    return (bits[:, :w] & jnp.uint32(0xFFFF0000)) | (bits[:, w:] >> jnp.uint32(16))


def _unpack_bf16_pairs(words):
    hi = lax.bitcast_convert_type(words & jnp.uint32(0xFFFF0000), F32)
    lo = lax.bitcast_convert_type(words << jnp.uint32(16), F32)
    return hi, lo


def _mem_kv_kernel(mem_ref, gmem_ref, wkv_ref, gck_ref, ones_ref, kT_ref, v_ref):
    m = _rms(mem_ref[0], gmem_ref[...]).astype(BF16)
    kv = _dot(m, wkv_ref[...])
    k = _group_rms(kv[:, :MEM_W], ones_ref[...], gck_ref[...])
    kT_ref[0] = k.T.astype(BF16)
    v_ref[0] = kv[:, MEM_W:].astype(BF16)


def _gelu_tanh(x):
    return 0.5 * x * (1.0 + jnp.tanh(0.7978845608028654 * (x + 0.044715 * (x * x * x))))


def _mixer_in_kernel(x_ref, cs_ref, spread_ref, gmix_ref, win_ref, gq_ref, gk_ref,
                     gsgu_ref, wsp_ref, bsp_ref, gcq_ref, ones256_ref, kT_ref, vm_ref,
                     q_out, k_out, v_out, gc_out):
    tm = x_ref.shape[0]
    hb = _rms(x_ref[...], gmix_ref[...]).astype(BF16)

    def proj(lo, hi):
        return _dot(hb, win_ref[:, lo:hi])

    lane = lax.broadcasted_iota(I32, (tm, LANES), 1)
    first_half = (lane % HEAD_GROUP) < (ROT_DIM // 2)
    tab = lax.dot_general(cs_ref[...], spread_ref[...], (((0,), (0,)), ((), ())), preferred_element_type=F32)
    cosb = tab[:, :LANES] + jnp.where((lane % HEAD_GROUP) >= ROT_DIM, 1.0, 0.0)
    sinb = tab[:, LANES:]

    def norm_rope(t, gain, out_ref):
        tn = _group_rms(t, ones256_ref[...], gain)
        for j in range(DIFF_W // LANES):
            c = tn[:, j * LANES:(j + 1) * LANES]
            partner = jnp.where(first_half, pltpu.roll(c, LANES - ROT_DIM // 2, 1), pltpu.roll(c, ROT_DIM // 2, 1))
            out_ref[:, j * LANES:(j + 1) * LANES] = (c * cosb + partner * sinb).astype(BF16)

    norm_rope(proj(0, DIFF_W), gq_ref[...] * (DIFF_HEAD_DIM ** -0.5 * LOG2_E), q_out)
    norm_rope(proj(DIFF_W, 2 * DIFF_W), gk_ref[...], k_out)
    v_out[...] = proj(2 * DIFF_W, 3 * DIFF_W).astype(BF16)

    z = _gelu_tanh(proj(3 * DIFF_W, 3 * DIFF_W + 2 * GMLP_W))
    u = z[:, :GMLP_W]
    vg = z[:, GMLP_W:]
    vc = vg - jnp.mean(vg, axis=-1, keepdims=True)
    vgn = (vc * lax.rsqrt(jnp.mean(vc * vc, axis=-1, keepdims=True) + EPS) * gsgu_ref[...]).astype(BF16)
    wrow = lax.broadcasted_iota(I32, (CHUNK, GMLP_GROUPS * CHUNK), 0)
    wcol = lax.broadcasted_iota(I32, (CHUNK, GMLP_GROUPS * CHUNK), 1) % CHUNK
    w_causal = jnp.where(wcol <= wrow, wsp_ref[...], 0.0).astype(BF16)
    grp = lax.broadcasted_iota(I32, (CHUNK, GMLP_W), 1) // HEAD_GROUP
    zero_b = jnp.zeros((CHUNK, GMLP_W), BF16)
    for r in range(tm // CHUNK):
        vchunk = vgn[r * CHUNK:(r + 1) * CHUNK, :]
        v_bd = jnp.concatenate([jnp.where(grp == g, vchunk, zero_b) for g in range(GMLP_GROUPS)], axis=0)
        mixed = _dot(w_causal, v_bd) + bsp_ref[...]
        gc_out[r * CHUNK:(r + 1) * CHUNK, 0:GMLP_W] = (u[r * CHUNK:(r + 1) * CHUNK, :] * mixed).astype(BF16)

    pc = proj(3 * DIFF_W + 2 * GMLP_W, D_IN_PROJ)
    qc = _group_rms(pc, ones256_ref[...], gcq_ref[...] * (HEAD_GROUP ** -0.5)).astype(BF16)
    hgrp = lax.broadcasted_iota(I32, (tm, MEM_W), 1) // HEAD_GROUP
    zero_q = jnp.zeros((tm, MEM_W), BF16)
    q_st = jnp.concatenate([jnp.where(hgrp == h, qc, zero_q) for h in range(N_MEM_HEADS)], axis=0)
    s = _dot(q_st, kT_ref[0])
    p = jnp.exp(s - jnp.max(s, axis=-1, keepdims=True))
    o = _dot(p.astype(BF16), vm_ref[0]) / jnp.sum(p, axis=-1, keepdims=True)
    c = jnp.zeros((tm, MEM_W), F32)
    for h in range(N_MEM_HEADS):
        c = c + jnp.where(hgrp == h, o[h * tm:(h + 1) * tm, :], 0.0)
    gc_out[:, GMLP_W:GMLP_W + MEM_W] = c.astype(BF16)


def _diff_attn_kernel(q_ref, k_ref, v_ref, lq1_ref, lk1_ref, lq2_ref, lk2_ref, gsub_ref, o_ref,
                      s0_ref, s1_ref, m_ref, acc_ref, vT_ref):
    tq = q_ref.shape[0]
    seq = k_ref.shape[0]
    i = pl.program_id(1)
    lane = lax.broadcasted_iota(I32, (tq, LANES), 1)
    heads = range(N_DIFF_HEADS)
    hl = lambda h: slice(h * DIFF_V_DIM, (h + 1) * DIFF_V_DIM)
    ext_rows = vT_ref.shape[1]

    @pl.when(i == 0)
    def _():
        ones_row = jnp.where(lax.broadcasted_iota(I32, (ext_rows - DIFF_V_DIM, seq), 0) == 0, 1.0, 0.0).astype(BF16)
        for h in heads:
            for c in range(seq // tq):
                vT_ref[h, 0:DIFF_V_DIM, c * tq:(c + 1) * tq] = v_ref[c * tq:(c + 1) * tq, hl(h)].T
            vT_ref[h, DIFF_V_DIM:ext_rows, :] = ones_row

    def stacked_qT(h):
        q = q_ref[:, hl(h)]
        zero = jnp.zeros_like(q)
        return jnp.concatenate([jnp.where(lane < DIFF_HEAD_DIM, q, zero), jnp.where(lane >= DIFF_HEAD_DIM, q, zero)],
                               axis=0).T

    qsT = [stacked_qT(h) for h in heads]

    def scores(t, s_ref):
        rows = pl.ds(pl.multiple_of(t * tq, tq), tq)
        for h in heads:
            s_ref[h] = _dot(k_ref[rows, hl(h)], qsT[h])

    def update(t, s_ref, causal):
        cols = pl.ds(pl.multiple_of(t * tq, tq), tq)
        for h in heads:
            s = s_ref[h]
            if causal:
                key = lax.broadcasted_iota(I32, (tq, 2 * tq), 0)
                qry = lax.broadcasted_iota(I32, (tq, 2 * tq), 1) % tq
                s = jnp.where(key <= qry, s, NEG_BIG)
            m = m_ref[h]
            m_new = jnp.maximum(m, jnp.max(s, axis=0, keepdims=True))
            alpha = jnp.exp2(m - m_new)
            p = jnp.exp2(s - m_new)
            m_ref[h] = m_new
            acc_ref[h] = alpha * acc_ref[h] + _dot(vT_ref[h, :, cols], p.astype(BF16))

    m_ref[...] = jnp.full(m_ref.shape, NEG_BIG, F32)
    acc_ref[...] = jnp.zeros(acc_ref.shape, F32)
    scores(0, s0_ref)

    def pair(pidx, carry):
        t = 2 * pidx
        scores(t + 1, s1_ref)
        update(t, s0_ref, False)
        scores(t + 2, s0_ref)
        update(t + 1, s1_ref, False)
        return carry

    lax.fori_loop(0, i // 2, pair, 0)

    @pl.when(i % 2 == 0)
    def _():
        update(i, s0_ref, True)

    @pl.when(i % 2 == 1)
    def _():
        scores(i, s1_ref)
        update(i - 1, s0_ref, False)
        update(i, s1_ref, True)

    lam = (jnp.exp(jnp.sum(lq1_ref[...] * lk1_ref[...], axis=-1, keepdims=True))
           - jnp.exp(jnp.sum(lq2_ref[...] * lk2_ref[...], axis=-1, keepdims=True)) + LAMBDA_INIT)
    for h in heads:
        on = acc_ref[h, 0:DIFF_V_DIM, :] * (1.0 / acc_ref[h, DIFF_V_DIM:DIFF_V_DIM + 1, :])
        o = on[:, :tq] - lam * on[:, tq:]
        ms = jnp.mean(o * o, axis=0, keepdims=True)
        o = o * lax.rsqrt(ms + EPS) * gsub_ref[...] * (1.0 - LAMBDA_INIT)
        o_ref[:, hl(h)] = o.T.astype(BF16)


def _out_router_kernel(x_ref, a_ref, gc_ref, wo_ref, gffn_ref, wrh_ref, wrl_ref, br_ref, before_ref,
                       x1_out, hpa_out, hpb_out, idx_out, gate_out, pos_out, cnt_out, carry_ref):
    tm = x_ref.shape[0]

    @pl.when(pl.program_id(0) == 0)
    def _():
        carry_ref[...] = jnp.zeros_like(carry_ref)

    sub = before_ref.shape[0]
    wr_stack = jnp.concatenate([wrh_ref[...], wrl_ref[...]], axis=0)
    eio = lax.broadcasted_iota(I32, (N_EXPERTS, sub), 0)
    before_b = before_ref[...]
    carry = carry_ref[...]
    for r in range(tm // sub):
        rows = slice(r * sub, (r + 1) * sub)
        mix = jnp.concatenate([a_ref[rows, :], gc_ref[rows, :]], axis=1)
        x1 = x_ref[rows, :] + _dot(mix, wo_ref[...])
        x1_out[rows, :] = x1
        h2 = _rms(x1, gffn_ref[...])
        hb = h2.astype(BF16)
        hpa_out[rows, :] = _pack_bf16_pairs(h2[:, :D_MODEL // 2])
        hpb_out[rows, :] = _pack_bf16_pairs(h2[:, D_MODEL // 2:])

        h_lo = (h2 - hb.astype(F32)).astype(BF16)
        both = _dot_nt(wr_stack, hb)
        logits = (both[:N_EXPERTS] + both[N_EXPERTS:]) + _dot_nt(wrh_ref[...], h_lo) + br_ref[...]

        vals, idxs, sels = [], [], []
        cur = logits
        for _ in range(TOP_K):
            m = jnp.max(cur, axis=0, keepdims=True)
            ik = jnp.min(jnp.where(cur == m, eio, N_EXPERTS), axis=0, keepdims=True)
            sel = eio == ik
            cur = jnp.where(sel, -jnp.inf, cur)
            vals.append(m)
            idxs.append(ik)
            sels.append(sel)
        es = [jnp.exp(v - vals[0]) for v in vals]
        tot = es[0] + es[1] + es[2] + es[3]
        gates = jnp.concatenate([e / tot for e in es], axis=0)
        g_hi = gates.astype(BF16)
        g_lo = (gates - g_hi.astype(F32)).astype(BF16)
        gate_out[:, rows] = jnp.concatenate(
            [g_hi, g_lo, jnp.zeros((GATE_ROWS - 2 * TOP_K, sub), BF16)], axis=0)
        idx_out[:, rows] = jnp.concatenate(idxs, axis=0)

        cnt = jnp.zeros((N_EXPERTS, sub), F32)
        for sel in sels:
            cnt = cnt + jnp.where(sel, 1.0, 0.0)
        base = carry + _dot(cnt.astype(BF16), before_b)
        pos_out[:, rows] = jnp.concatenate(
            [jnp.sum(jnp.where(sel, base, 0.0), axis=0, keepdims=True) for sel in sels], axis=0).astype(I32)
        carry = carry + jnp.sum(cnt, axis=1, keepdims=True)
    carry_ref[...] = carry
    cnt_out[...] = carry


def _dest_kernel(cnt_ref, idx_ref, pos_ref, dest_out, be_out, nbu_out):
    idx = idx_ref[...]
    dest = pos_ref[...]
    bidx = lax.broadcasted_iota(I32, be_out.shape, 1)
    be = jnp.zeros(be_out.shape, I32)
    run = jnp.int32(0)
    for e in range(N_EXPERTS):
        dest = dest + jnp.where(idx == e, run, 0)
        run = run + ((cnt_ref[e] + (ROW_BLOCK - 1)) // ROW_BLOCK) * ROW_BLOCK
        be = be + jnp.where(bidx >= run // ROW_BLOCK, 1, 0)
    dest_out[...] = dest
    be_out[...] = jnp.minimum(be, N_EXPERTS - 1)
    nbu_out[...] = jnp.zeros(nbu_out.shape, I32) + run // ROW_BLOCK


def _expert_ffn_kernel(be_ref, nbu_ref, nxt_ref, xa_ref, xb_ref, w1_hbm, b1_ref, w2_hbm, b2_ref, ya_ref, yb_ref,
                       w1f_ref, w2f_ref, w1s_ref, w2s_ref, slot_ref, sem_ref):
    b = pl.program_id(0)

    def weight_copies(e, slot):
        return (pltpu.make_async_copy(w1_hbm.at[e], w1f_ref.at[slot], sem_ref.at[slot, 0]),
                pltpu.make_async_copy(w2_hbm.at[e], w2f_ref.at[slot], sem_ref.at[slot, 1]))

    @pl.when(b < nbu_ref[0])
    def _():
        e = be_ref[b]

        @pl.when(b == 0)
        def _():
            slot_ref[0] = 0
            for cp in weight_copies(e, 0):
                cp.start()

        @pl.when(jnp.logical_or(b == 0, e != be_ref[jnp.maximum(b - 1, 0)]))
        def _():
            slot = slot_ref[0]
            for cp in weight_copies(e, slot):
                cp.wait()
            w1s_ref[...] = w1f_ref[slot].astype(BF16)
            w2s_ref[...] = w2f_ref[slot].astype(BF16)
            nxt = nxt_ref[e]

            @pl.when(nxt >= 0)
            def _():
                for cp in weight_copies(nxt, 1 - slot):
                    cp.start()

            slot_ref[0] = 1 - slot

        parts = _unpack_bf16_pairs(xa_ref[...]) + _unpack_bf16_pairs(xb_ref[...])
        xrow = jnp.concatenate([p.astype(BF16) for p in parts], axis=1)
        hm = _dot(xrow, w1s_ref[...]) + b1_ref[0]
        glu = jnp.minimum(hm[:, :D_FF], SWIGLU_LIMIT)
        lin = jnp.clip(hm[:, D_FF:], -SWIGLU_LIMIT, SWIGLU_LIMIT)
        act = glu * jax.nn.sigmoid(SWIGLU_ALPHA * glu) * (lin + 1.0)
        y = _dot(act.astype(BF16), w2s_ref[...]) + b2_ref[0]
        ya_ref[...] = _pack_bf16_pairs(y[:, :D_MODEL // 2])
        yb_ref[...] = _pack_bf16_pairs(y[:, D_MODEL // 2:])


SC_WINDOW = 128


def _sc_mesh():
    return plsc.VectorSubcoreMesh(core_axis_name="c", subcore_axis_name="s")


def _sc_gather_rows(table, idx_row):
    n = idx_row.shape[1]
    width = table.shape[1]

    @functools.partial(pl.kernel, out_type=jax.ShapeDtypeStruct((n, width), table.dtype), mesh=_sc_mesh(),
                       scratch_types=[])
    def gather_kernel(t_hbm, i_hbm, o_hbm):
        def body(i_vmem, o_vmem):
            pltpu.sync_copy(t_hbm.at[i_vmem.at[0]], o_vmem)

        pltpu.emit_pipeline(
            body,
            grid=(n // SC_WINDOW,),
            in_specs=[pl.BlockSpec((1, SC_WINDOW), lambda i: (0, i))],
            out_specs=[pl.BlockSpec((SC_WINDOW, width), lambda i: (i, 0))],
            core_axis_name=("c", "s"),
            dimension_semantics=(pltpu.PARALLEL,),
        )(i_hbm, o_hbm)

    return gather_kernel(table, idx_row)


def _sc_scatter_rows(rows, idx_rows, n_out):
    n, width = rows.shape

    @functools.partial(pl.kernel, out_type=jax.ShapeDtypeStruct((n_out, width), rows.dtype), mesh=_sc_mesh(),
                       scratch_types=[])
    def scatter_kernel(r_hbm, *refs):
        i_hbms, o_hbm = refs[:-1], refs[-1]

        def body(r_vmem, *i_vmems):
            for i_vmem in i_vmems:
                pltpu.sync_copy(r_vmem, o_hbm.at[i_vmem.at[0]])

        pltpu.emit_pipeline(
            body,
            grid=(n // SC_WINDOW,),
            in_specs=[pl.BlockSpec((SC_WINDOW, width), lambda i: (i, 0))]
            + [pl.BlockSpec((1, SC_WINDOW), lambda i: (0, i)) for _ in i_hbms],
            out_specs=[],
            core_axis_name=("c", "s"),
            dimension_semantics=(pltpu.PARALLEL,),
        )(r_hbm, *i_hbms)

    return scatter_kernel(rows, *idx_rows)


def _combine_kernel(x1_ref, yga_ref, ygb_ref, gate_ref, spread_ref, o_ref):
    q = D_MODEL // 4
    x1 = x1_ref[...]
    acc = [x1[:, j * q:(j + 1) * q] for j in range(4)]
    gfull = lax.dot_general(gate_ref[...], spread_ref[...], (((0,), (0,)), ((), ())), preferred_element_type=F32)
    for k in range(TOP_K):
        parts = _unpack_bf16_pairs(yga_ref[k]) + _unpack_bf16_pairs(ygb_ref[k])
        g = jnp.tile(gfull[:, k * LANES:(k + 1) * LANES], (1, q // LANES))
        acc = [a + g * p for a, p in zip(acc, parts)]
    for j in range(4):
        o_ref[:, j * q:(j + 1) * q] = acc[j]


def _combine_into_kernel(x1_ref, yga_ref, ygb_ref, gate_ref, spread_ref, prev_ref, o_ref):
    del prev_ref
    _combine_kernel(x1_ref, yga_ref, ygb_ref, gate_ref, spread_ref, o_ref)


def _block_diag_ones(width):
    r = jnp.arange(width) // HEAD_GROUP
    return (r[:, None] == r[None, :]).astype(BF16)


def _rope_tables(positions):
    half = ROT_DIM // 2
    inv_freq = ROPE_THETA ** (-jnp.arange(0, ROT_DIM, 2, dtype=F32) / ROT_DIM)
    ang = inv_freq[:, None] * positions.astype(F32).reshape(1, -1)
    cs = jnp.concatenate([jnp.cos(ang), jnp.sin(ang)], axis=0)
    cs_hi = cs.astype(BF16)
    cs_lo = (cs - cs_hi.astype(F32)).astype(BF16)
    lane = jnp.arange(LANES) % HEAD_GROUP
    j = jnp.arange(half)[:, None]
    lo_half = (lane[None, :] == j).astype(F32)
    hi_half = (lane[None, :] == j + half).astype(F32)
    spread = jnp.concatenate([
        jnp.concatenate([lo_half + hi_half, jnp.zeros((half, LANES), F32)], axis=1),
        jnp.concatenate([jnp.zeros((half, LANES), F32), hi_half - lo_half], axis=1)], axis=0)
    return jnp.concatenate([cs_hi, cs_lo], axis=0), jnp.concatenate([spread, spread], axis=0).astype(BF16)


def _full(shape):
    return pl.BlockSpec(shape, lambda *_: (0,) * len(shape))


def kernel(x, mem, positions, g_mix_norm, w_in, g_dq, g_dk, lambda_q1, lambda_k1, lambda_q2, lambda_k2, g_subln, g_sgu, w_spatial, b_spatial, g_mem_norm, w_mem_kv, g_cq, g_ck, w_out, g_ffn_norm, w_router, b_router, w_mlp1, b_mlp1, w_mlp2, b_mlp2):
    B, S, D = x.shape
    M = mem.shape[1]
    tm = TOKEN_TILE
    tr = ROUTER_TILE
    assert D == D_MODEL and S % tm == 0 and S % ATTN_TILE == 0 and g_mix_norm.shape[0] == 1
    n_parts = LAYER_PARTS if B % LAYER_PARTS == 0 and (B // LAYER_PARTS) * S % tr == 0 else 1
    Bp = B // n_parts
    N = Bp * S
    n_assign = N * TOP_K
    n_blocks = -(-n_assign // ROW_BLOCK) + N_EXPERTS
    n_rows = n_blocks * ROW_BLOCK
    nb_pad = -(-n_blocks // LANES) * LANES

    xf = x.reshape(B * S, D)
    rope_cs, rope_spread = _rope_tables(positions)
    ones256 = _block_diag_ones(MEM_W)
    row = lambda v: v.reshape(1, -1).astype(F32)
    tile_row = lambda v, reps: jnp.tile(v.reshape(1, -1).astype(F32), (1, reps))
    w_in_b, w_out_b, w_kv_b = w_in[0].astype(BF16), w_out[0].astype(BF16), w_mem_kv[0].astype(BF16)
    w_sp_lanes = jnp.transpose(w_spatial[0], (1, 0, 2)).reshape(CHUNK, GMLP_GROUPS * CHUNK)
    b_sp_lanes = jnp.repeat(b_spatial[0].T, HEAD_GROUP, axis=1)
    wr = w_router[0].T.astype(F32)
    wr_hi = wr.astype(BF16)
    wr_lo = (wr - wr_hi.astype(F32)).astype(BF16)
    before = (jnp.arange(tr)[:, None] < jnp.arange(tr)[None, :]).astype(BF16)
    gate_row = jnp.arange(GATE_ROWS)[:, None]
    gate_spread = ((gate_row < 2 * TOP_K) & (gate_row % TOP_K == jnp.arange(TOP_K * LANES)[None, :] // LANES)).astype(BF16)
    b1r = b_mlp1[0].reshape(N_EXPERTS, 1, 2 * D_FF)
    b2r = b_mlp2[0].reshape(N_EXPERTS, 1, D)

    out = None
    for part in range(n_parts):
        b_off = part * Bp
        t_off = part * (N // tm)
        r_off = part * (N // tr)

        kT, vm = pl.pallas_call(
            _mem_kv_kernel,
            grid=(Bp,),
            in_specs=[pl.BlockSpec((1, M, D), lambda b: (b + b_off, 0, 0)), _full((1, D)), _full((D, 2 * MEM_W)),
                      _full((1, MEM_W)), _full((MEM_W, MEM_W))],
            out_specs=[pl.BlockSpec((1, MEM_W, M), lambda b: (b, 0, 0)), pl.BlockSpec((1, M, MEM_W), lambda b: (b, 0, 0))],
            out_shape=[jax.ShapeDtypeStruct((Bp, MEM_W, M), BF16), jax.ShapeDtypeStruct((Bp, M, MEM_W), BF16)],
            compiler_params=_cparams(("parallel",)),
            name="mem_kv",
        )(mem, row(g_mem_norm[0]), w_kv_b, tile_row(g_ck[0], N_MEM_HEADS), ones256)

        tiles_per_batch = S // tm
        tok = lambda w: pl.BlockSpec((tm, w), lambda i: (i, 0))
        tok_in = lambda w: pl.BlockSpec((tm, w), lambda i: (i + t_off, 0))
        qn, kn, vv, gc = pl.pallas_call(
            _mixer_in_kernel,
            grid=(N // tm,),
            in_specs=[tok_in(D), pl.BlockSpec((4 * ROT_DIM // 2, tm), lambda i: (0, i + t_off)),
                      _full((4 * ROT_DIM // 2, 2 * LANES)),
                      _full((1, D)), _full((D, D_IN_PROJ)),
                      _full((1, DIFF_W)), _full((1, DIFF_W)),
                      _full((1, GMLP_W)), _full((CHUNK, GMLP_GROUPS * CHUNK)), _full((CHUNK, GMLP_W)),
                      _full((1, MEM_W)), _full((MEM_W, MEM_W)),
                      pl.BlockSpec((1, MEM_W, M), lambda i: (i // tiles_per_batch, 0, 0)),
                      pl.BlockSpec((1, M, MEM_W), lambda i: (i // tiles_per_batch, 0, 0))],
            out_specs=[tok(DIFF_W), tok(DIFF_W), tok(DIFF_W), tok(GMLP_W + MEM_W)],
            out_shape=[jax.ShapeDtypeStruct((N, DIFF_W), BF16)] * 3 + [jax.ShapeDtypeStruct((N, GMLP_W + MEM_W), BF16)],
            compiler_params=_cparams(("parallel",)),
            name="mixer_in",
        )(xf, rope_cs, rope_spread, row(g_mix_norm[0]), w_in_b,
          tile_row(g_dq[0], 2 * N_DIFF_HEADS), tile_row(g_dk[0], 2 * N_DIFF_HEADS),
          row(g_sgu[0]), w_sp_lanes, b_sp_lanes, tile_row(g_cq[0], N_MEM_HEADS), ones256, kT, vm)

        tq = ATTN_TILE
        nq = S // tq
        head_q = pl.BlockSpec((tq, DIFF_W), lambda b, i: (b * nq + i, 0))
        head_kv = pl.BlockSpec((S, DIFF_W), lambda b, i: (b, 0))
        lam_spec = pl.BlockSpec((1, DIFF_HEAD_DIM), lambda b, i: (0, 0))
        nh = N_DIFF_HEADS
        attn = pl.pallas_call(
            _diff_attn_kernel,
            grid=(Bp, nq),
            in_specs=[head_q, head_kv, head_kv, lam_spec, lam_spec, lam_spec, lam_spec,
                      pl.BlockSpec((DIFF_V_DIM, 1), lambda b, i: (0, 0))],
            out_specs=head_q,
            out_shape=jax.ShapeDtypeStruct((N, DIFF_W), BF16),
            scratch_shapes=[pltpu.VMEM((nh, tq, 2 * tq), F32), pltpu.VMEM((nh, tq, 2 * tq), F32),
                            pltpu.VMEM((nh, 1, 2 * tq), F32), pltpu.VMEM((nh, ATTN_EXT_ROWS, 2 * tq), F32),
                            pltpu.VMEM((nh, ATTN_EXT_ROWS, S), BF16)],
            compiler_params=_cparams(("parallel", "arbitrary")),
            name="diff_attn",
        )(qn, kn, vv, row(lambda_q1[0]), row(lambda_k1[0]), row(lambda_q2[0]), row(lambda_k2[0]),
          g_subln[0].reshape(DIFF_V_DIM, 1).astype(F32))

        rtok = lambda w: pl.BlockSpec((tr, w), lambda i: (i, 0))
        rtok_in = lambda w: pl.BlockSpec((tr, w), lambda i: (i + r_off, 0))
        tokT = lambda: pl.BlockSpec((TOP_K, tr), lambda i: (0, i))
        hw = D // 4
        x1, hpa, hpb, idxT, gateT, posT, counts = pl.pallas_call(
            _out_router_kernel,
            grid=(N // tr,),
            in_specs=[rtok_in(D), rtok(DIFF_W), rtok(GMLP_W + MEM_W), _full((D, D)), _full((1, D)),
                      _full((N_EXPERTS, D)), _full((N_EXPERTS, D)), _full((N_EXPERTS, 1)), _full((tr, tr))],
            out_specs=[rtok(D), rtok(hw), rtok(hw), tokT(), pl.BlockSpec((GATE_ROWS, tr), lambda i: (0, i)), tokT(),
                       _full((N_EXPERTS, 1))],
            out_shape=[jax.ShapeDtypeStruct((N, D), F32), jax.ShapeDtypeStruct((N, hw), U32),
                       jax.ShapeDtypeStruct((N, hw), U32), jax.ShapeDtypeStruct((TOP_K, N), I32),
                       jax.ShapeDtypeStruct((GATE_ROWS, N), BF16),
                       jax.ShapeDtypeStruct((TOP_K, N), I32), jax.ShapeDtypeStruct((N_EXPERTS, 1), F32)],
            scratch_shapes=[pltpu.VMEM((N_EXPERTS, 1), F32)],
            compiler_params=_cparams(("arbitrary",)),
            name="out_router",
        )(xf, attn, gc, w_out_b, row(g_ffn_norm[0]), wr_hi, wr_lo, b_router[0].reshape(N_EXPERTS, 1).astype(F32), before)

        destT, block_expert, nb_used = pl.pallas_call(
            _dest_kernel,
            grid_spec=pltpu.PrefetchScalarGridSpec(
                num_scalar_prefetch=1,
                grid=(1,),
                in_specs=[pl.BlockSpec((TOP_K, N), lambda i, c: (0, 0)), pl.BlockSpec((TOP_K, N), lambda i, c: (0, 0))],
                out_specs=[pl.BlockSpec((TOP_K, N), lambda i, c: (0, 0)), pl.BlockSpec((1, nb_pad), lambda i, c: (0, 0)),
                           pl.BlockSpec((1, LANES), lambda i, c: (0, 0))],
            ),
            out_shape=[jax.ShapeDtypeStruct((TOP_K, N), I32), jax.ShapeDtypeStruct((1, nb_pad), I32),
                       jax.ShapeDtypeStruct((1, LANES), I32)],
            compiler_params=_cparams(("arbitrary",)),
            name="dest",
        )(counts.reshape(N_EXPERTS).astype(I32), idxT, posT)

        dest_rows = [destT[k].reshape(1, N) for k in range(TOP_K)]
        xa_buf = _sc_scatter_rows(hpa, dest_rows, n_rows)
        xb_buf = _sc_scatter_rows(hpb, dest_rows, n_rows)

        cnt_i = counts.reshape(N_EXPERTS).astype(I32)
        owner = jnp.where(cnt_i > 0, jnp.arange(N_EXPERTS, dtype=I32), N_EXPERTS)
        later = jnp.concatenate([lax.cummin(owner[::-1])[::-1][1:], jnp.full((1,), N_EXPERTS, I32)])
        next_expert = jnp.where(later < N_EXPERTS, later, -1)
        last = lambda b, be, nbu, nxt: jnp.minimum(b, nbu[0] - 1)
        row_blk = lambda: pl.BlockSpec((ROW_BLOCK, hw), lambda b, be, nbu, nxt: (last(b, be, nbu, nxt), 0))
        ya_buf, yb_buf = pl.pallas_call(
            _expert_ffn_kernel,
            grid_spec=pltpu.PrefetchScalarGridSpec(
                num_scalar_prefetch=3,
                grid=(n_blocks,),
                in_specs=[row_blk(), row_blk(),
                          pl.BlockSpec(memory_space=pl.ANY),
                          pl.BlockSpec((1, 1, 2 * D_FF), lambda b, be, nbu, nxt: (be[b], 0, 0)),
                          pl.BlockSpec(memory_space=pl.ANY),
                          pl.BlockSpec((1, 1, D), lambda b, be, nbu, nxt: (be[b], 0, 0))],
                out_specs=[row_blk(), row_blk()],
                scratch_shapes=[pltpu.VMEM((2, D, 2 * D_FF), F32), pltpu.VMEM((2, D_FF, D), F32),
                                pltpu.VMEM((D, 2 * D_FF), BF16), pltpu.VMEM((D_FF, D), BF16),
                                pltpu.SMEM((1,), I32), pltpu.SemaphoreType.DMA((2, 2))],
            ),
            out_shape=[jax.ShapeDtypeStruct((n_rows, hw), U32)] * 2,
            compiler_params=_cparams(("arbitrary",)),
            name="expert_ffn",
        )(block_expert[0, :n_blocks], nb_used[0, :1], next_expert, xa_buf, xb_buf, w_mlp1[0], b1r, w_mlp2[0], b2r)

        n_chunks = TAIL_CHUNKS if part == n_parts - 1 and N % (TAIL_CHUNKS * tm) == 0 else 1
        nc = N // n_chunks
        for c in range(n_chunks):
            c_off = c * (nc // tm)
            dest_c = destT[:, c * nc:(c + 1) * nc].reshape(1, TOP_K * nc)
            yga = _sc_gather_rows(ya_buf, dest_c).reshape(TOP_K, nc, hw)
            ygb = _sc_gather_rows(yb_buf, dest_c).reshape(TOP_K, nc, hw)
            prev = () if out is None else (out,)
            out = pl.pallas_call(
                _combine_kernel if out is None else _combine_into_kernel,
                grid=(nc // tm,),
                in_specs=[pl.BlockSpec((tm, D), lambda i: (i + c_off, 0)),
                          pl.BlockSpec((TOP_K, tm, hw), lambda i: (0, i, 0)),
                          pl.BlockSpec((TOP_K, tm, hw), lambda i: (0, i, 0)),
                          pl.BlockSpec((GATE_ROWS, tm), lambda i: (0, i + c_off)),
                          _full((GATE_ROWS, TOP_K * LANES))] + [pl.BlockSpec(memory_space=pl.ANY)] * len(prev),
                out_specs=pl.BlockSpec((tm, D), lambda i: (i + t_off + c_off, 0)),
                out_shape=jax.ShapeDtypeStruct((B * S, D), F32),
                input_output_aliases={5: 0} if prev else {},
                compiler_params=_cparams(("parallel",)),
                name="combine",
            )(x1, yga, ygb, gateT, gate_spread, *prev)
    return out.reshape(B, S, D)
```

```python
import functools

import jax
import jax.numpy as jnp
from jax import lax
from jax.experimental import pallas as pl
from jax.experimental.pallas import tpu as pltpu
from jax.experimental.pallas import tpu_sc as plsc

F32 = jnp.float32
BF16 = jnp.bfloat16
I32 = jnp.int32
U32 = jnp.uint32

D_MODEL = 1024
N_DIFF_HEADS = 4
DIFF_HEAD_DIM = 64
DIFF_V_DIM = 128
DIFF_W = 512
GMLP_W = 256
GMLP_GROUPS = 4
CHUNK = 128
MEM_W = 256
N_MEM_HEADS = 4
HEAD_GROUP = 64
D_IN_PROJ = 2304
ROPE_THETA = 500000.0
ROT_DIM = 16
N_EXPERTS = 32
TOP_K = 4
D_FF = 1024
SWIGLU_LIMIT = 7.0
SWIGLU_ALPHA = 1.702
EPS = 1e-6
LAMBDA_INIT = 0.8 - 0.6

LANES = 128
ROW_BLOCK = 512
TOKEN_TILE = 512
LAYER_PARTS = 1
ROUTER_TILE = 1024
GATE_ROWS = 16
ATTN_TILE = 256
ATTN_EXT_ROWS = 128 + 16
VMEM_LIMIT = 56 * 1024 * 1024
NEG_BIG = -1e30
LOG2_E = 1.4426950408889634


def _cparams(sem):
    return pltpu.CompilerParams(dimension_semantics=sem, vmem_limit_bytes=VMEM_LIMIT)


def _dot(a, b):
    return jnp.dot(a, b, preferred_element_type=F32)


def _dot_nt(a, b):
    return lax.dot_general(a, b, (((1,), (1,)), ((), ())), preferred_element_type=F32)


def _rms(x, gain):
    ms = jnp.mean(x * x, axis=-1, keepdims=True)
    return x * lax.rsqrt(ms + EPS) * gain


def _group_rms(t, ones_bd, gain):
    w = ones_bd.shape[0]
    chunks = []
    for j in range(t.shape[1] // w):
        c = t[:, j * w:(j + 1) * w]
        ss = _dot((c * c).astype(BF16), ones_bd)
        chunks.append(c * lax.rsqrt(ss * (1.0 / HEAD_GROUP) + EPS))
    return (chunks[0] if len(chunks) == 1 else jnp.concatenate(chunks, axis=1)) * gain


def _pack_bf16_pairs(v):
    w = v.shape[1] // 2
    bits = lax.bitcast_convert_type(v.astype(BF16).astype(F32), U32)
    return (bits[:, :w] & jnp.uint32(0xFFFF0000)) | (bits[:, w:] >> jnp.uint32(16))


def _unpack_bf16_pairs(words):
    hi = lax.bitcast_convert_type(words & jnp.uint32(0xFFFF0000), F32)
    lo = lax.bitcast_convert_type(words << jnp.uint32(16), F32)
    return hi, lo


def _mem_kv_kernel(mem_ref, gmem_ref, wkv_ref, gck_ref, ones_ref, kT_ref, v_ref):
    m = _rms(mem_ref[0], gmem_ref[...]).astype(BF16)
    kv = _dot(m, wkv_ref[...])
    k = _group_rms(kv[:, :MEM_W], ones_ref[...], gck_ref[...])
    kT_ref[0] = k.T.astype(BF16)
    v_ref[0] = kv[:, MEM_W:].astype(BF16)


def _gelu_tanh(x):
    return 0.5 * x * (1.0 + jnp.tanh(0.7978845608028654 * (x + 0.044715 * (x * x * x))))


def _mixer_in_kernel(x_ref, cs_ref, spread_ref, gmix_ref, win_ref, gq_ref, gk_ref,
                     gsgu_ref, wsp_ref, bsp_ref, gcq_ref, ones256_ref, kT_ref, vm_ref,
                     q_out, k_out, v_out, gc_out):
    tm = x_ref.shape[0]
    hb = _rms(x_ref[...], gmix_ref[...]).astype(BF16)

    def proj(lo, hi):
        return _dot(hb, win_ref[:, lo:hi])

    lane = lax.broadcasted_iota(I32, (tm, LANES), 1)
    first_half = (lane % HEAD_GROUP) < (ROT_DIM // 2)
    tab = lax.dot_general(cs_ref[...], spread_ref[...], (((0,), (0,)), ((), ())), preferred_element_type=F32)
    cosb = tab[:, :LANES] + jnp.where((lane % HEAD_GROUP) >= ROT_DIM, 1.0, 0.0)
    sinb = tab[:, LANES:]

    def norm_rope(t, gain, out_ref):
        tn = _group_rms(t, ones256_ref[...], gain)
        for j in range(DIFF_W // LANES):
            c = tn[:, j * LANES:(j + 1) * LANES]
            partner = jnp.where(first_half, pltpu.roll(c, LANES - ROT_DIM // 2, 1), pltpu.roll(c, ROT_DIM // 2, 1))
            out_ref[:, j * LANES:(j + 1) * LANES] = (c * cosb + partner * sinb).astype(BF16)

    norm_rope(proj(0, DIFF_W), gq_ref[...] * (DIFF_HEAD_DIM ** -0.5 * LOG2_E), q_out)
    norm_rope(proj(DIFF_W, 2 * DIFF_W), gk_ref[...], k_out)
    v_out[...] = proj(2 * DIFF_W, 3 * DIFF_W).astype(BF16)

    z = _gelu_tanh(proj(3 * DIFF_W, 3 * DIFF_W + 2 * GMLP_W))
    u = z[:, :GMLP_W]
    vg = z[:, GMLP_W:]
    vc = vg - jnp.mean(vg, axis=-1, keepdims=True)
    vgn = (vc * lax.rsqrt(jnp.mean(vc * vc, axis=-1, keepdims=True) + EPS) * gsgu_ref[...]).astype(BF16)
    wrow = lax.broadcasted_iota(I32, (CHUNK, GMLP_GROUPS * CHUNK), 0)
    wcol = lax.broadcasted_iota(I32, (CHUNK, GMLP_GROUPS * CHUNK), 1) % CHUNK
    w_causal = jnp.where(wcol <= wrow, wsp_ref[...], 0.0).astype(BF16)
    grp = lax.broadcasted_iota(I32, (CHUNK, GMLP_W), 1) // HEAD_GROUP
    zero_b = jnp.zeros((CHUNK, GMLP_W), BF16)
    for r in range(tm // CHUNK):
        vchunk = vgn[r * CHUNK:(r + 1) * CHUNK, :]
        v_bd = jnp.concatenate([jnp.where(grp == g, vchunk, zero_b) for g in range(GMLP_GROUPS)], axis=0)
        mixed = _dot(w_causal, v_bd) + bsp_ref[...]
        gc_out[r * CHUNK:(r + 1) * CHUNK, 0:GMLP_W] = (u[r * CHUNK:(r + 1) * CHUNK, :] * mixed).astype(BF16)

    pc = proj(3 * DIFF_W + 2 * GMLP_W, D_IN_PROJ)
    qc = _group_rms(pc, ones256_ref[...], gcq_ref[...] * (HEAD_GROUP ** -0.5)).astype(BF16)
    hgrp = lax.broadcasted_iota(I32, (tm, MEM_W), 1) // HEAD_GROUP
    zero_q = jnp.zeros((tm, MEM_W), BF16)
    q_st = jnp.concatenate([jnp.where(hgrp == h, qc, zero_q) for h in range(N_MEM_HEADS)], axis=0)
    s = _dot(q_st, kT_ref[0])
    p = jnp.exp(s - jnp.max(s, axis=-1, keepdims=True))
    o = _dot(p.astype(BF16), vm_ref[0]) / jnp.sum(p, axis=-1, keepdims=True)
    c = jnp.zeros((tm, MEM_W), F32)
    for h in range(N_MEM_HEADS):
        c = c + jnp.where(hgrp == h, o[h * tm:(h + 1) * tm, :], 0.0)
    gc_out[:, GMLP_W:GMLP_W + MEM_W] = c.astype(BF16)


def _diff_attn_kernel(q_ref, k_ref, v_ref, lq1_ref, lk1_ref, lq2_ref, lk2_ref, gsub_ref, o_ref,
                      s0_ref, s1_ref, m_ref, acc_ref, vT_ref):
    tq = q_ref.shape[0]
    seq = k_ref.shape[0]
    i = pl.program_id(1)
    lane = lax.broadcasted_iota(I32, (tq, LANES), 1)
    heads = range(N_DIFF_HEADS)
    hl = lambda h: slice(h * DIFF_V_DIM, (h + 1) * DIFF_V_DIM)
    ext_rows = vT_ref.shape[1]

    @pl.when(i == 0)
    def _():
        ones_row = jnp.where(lax.broadcasted_iota(I32, (ext_rows - DIFF_V_DIM, seq), 0) == 0, 1.0, 0.0).astype(BF16)
        for h in heads:
            for c in range(seq // tq):
                vT_ref[h, 0:DIFF_V_DIM, c * tq:(c + 1) * tq] = v_ref[c * tq:(c + 1) * tq, hl(h)].T
            vT_ref[h, DIFF_V_DIM:ext_rows, :] = ones_row

    def stacked_qT(h):
        q = q_ref[:, hl(h)]
        zero = jnp.zeros_like(q)
        return jnp.concatenate([jnp.where(lane < DIFF_HEAD_DIM, q, zero), jnp.where(lane >= DIFF_HEAD_DIM, q, zero)],
                               axis=0).T

    qsT = [stacked_qT(h) for h in heads]

    def scores(t, s_ref):
        rows = pl.ds(pl.multiple_of(t * tq, tq), tq)
        for h in heads:
            s_ref[h] = _dot(k_ref[rows, hl(h)], qsT[h])

    def update(t, s_ref, causal):
        cols = pl.ds(pl.multiple_of(t * tq, tq), tq)
        for h in heads:
            s = s_ref[h]
            if causal:
                key = lax.broadcasted_iota(I32, (tq, 2 * tq), 0)
                qry = lax.broadcasted_iota(I32, (tq, 2 * tq), 1) % tq
                s = jnp.where(key <= qry, s, NEG_BIG)
            m = m_ref[h]
            m_new = jnp.maximum(m, jnp.max(s, axis=0, keepdims=True))
            alpha = jnp.exp2(m - m_new)
            p = jnp.exp2(s - m_new)
            m_ref[h] = m_new
            acc_ref[h] = alpha * acc_ref[h] + _dot(vT_ref[h, :, cols], p.astype(BF16))

    m_ref[...] = jnp.full(m_ref.shape, NEG_BIG, F32)
    acc_ref[...] = jnp.zeros(acc_ref.shape, F32)
    scores(0, s0_ref)

    def pair(pidx, carry):
        t = 2 * pidx
        scores(t + 1, s1_ref)
        update(t, s0_ref, False)
        scores(t + 2, s0_ref)
        update(t + 1, s1_ref, False)
        return carry

    lax.fori_loop(0, i // 2, pair, 0)

    @pl.when(i % 2 == 0)
    def _():
        update(i, s0_ref, True)

    @pl.when(i % 2 == 1)
    def _():
        scores(i, s1_ref)
        update(i - 1, s0_ref, False)
        update(i, s1_ref, True)

    lam = (jnp.exp(jnp.sum(lq1_ref[...] * lk1_ref[...], axis=-1, keepdims=True))
           - jnp.exp(jnp.sum(lq2_ref[...] * lk2_ref[...], axis=-1, keepdims=True)) + LAMBDA_INIT)
    for h in heads:
        on = acc_ref[h, 0:DIFF_V_DIM, :] * (1.0 / acc_ref[h, DIFF_V_DIM:DIFF_V_DIM + 1, :])
        o = on[:, :tq] - lam * on[:, tq:]
        ms = jnp.mean(o * o, axis=0, keepdims=True)
        o = o * lax.rsqrt(ms + EPS) * gsub_ref[...] * (1.0 - LAMBDA_INIT)
        o_ref[:, hl(h)] = o.T.astype(BF16)


def _out_router_kernel(x_ref, a_ref, gc_ref, wo_ref, gffn_ref, wrh_ref, wrl_ref, br_ref, before_ref,
                       x1_out, hpa_out, hpb_out, idx_out, gate_out, pos_out, cnt_out, carry_ref):
    tm = x_ref.shape[0]

    @pl.when(pl.program_id(0) == 0)
    def _():
        carry_ref[...] = jnp.zeros_like(carry_ref)

    sub = before_ref.shape[0]
    wr_stack = jnp.concatenate([wrh_ref[...], wrl_ref[...]], axis=0)
    eio = lax.broadcasted_iota(I32, (N_EXPERTS, sub), 0)
    before_b = before_ref[...]
    carry = carry_ref[...]
    for r in range(tm // sub):
        rows = slice(r * sub, (r + 1) * sub)
        mix = jnp.concatenate([a_ref[rows, :], gc_ref[rows, :]], axis=1)
        x1 = x_ref[rows, :] + _dot(mix, wo_ref[...])
        x1_out[rows, :] = x1
        h2 = _rms(x1, gffn_ref[...])
        hb = h2.astype(BF16)
        hpa_out[rows, :] = _pack_bf16_pairs(h2[:, :D_MODEL // 2])
        hpb_out[rows, :] = _pack_bf16_pairs(h2[:, D_MODEL // 2:])

        h_lo = (h2 - hb.astype(F32)).astype(BF16)
        both = _dot_nt(wr_stack, hb)
        logits = (both[:N_EXPERTS] + both[N_EXPERTS:]) + _dot_nt(wrh_ref[...], h_lo) + br_ref[...]

        vals, idxs, sels = [], [], []
        cur = logits
        for _ in range(TOP_K):
            m = jnp.max(cur, axis=0, keepdims=True)
            ik = jnp.min(jnp.where(cur == m, eio, N_EXPERTS), axis=0, keepdims=True)
            sel = eio == ik
            cur = jnp.where(sel, -jnp.inf, cur)
            vals.append(m)
            idxs.append(ik)
            sels.append(sel)
        es = [jnp.exp(v - vals[0]) for v in vals]
        tot = es[0] + es[1] + es[2] + es[3]
        gates = jnp.concatenate([e / tot for e in es], axis=0)
        g_hi = gates.astype(BF16)
        g_lo = (gates - g_hi.astype(F32)).astype(BF16)
        gate_out[:, rows] = jnp.concatenate(
            [g_hi, g_lo, jnp.zeros((GATE_ROWS - 2 * TOP_K, sub), BF16)], axis=0)
        idx_out[:, rows] = jnp.concatenate(idxs, axis=0)

        cnt = jnp.zeros((N_EXPERTS, sub), F32)
        for sel in sels:
            cnt = cnt + jnp.where(sel, 1.0, 0.0)
        base = carry + _dot(cnt.astype(BF16), before_b)
        pos_out[:, rows] = jnp.concatenate(
            [jnp.sum(jnp.where(sel, base, 0.0), axis=0, keepdims=True) for sel in sels], axis=0).astype(I32)
        carry = carry + jnp.sum(cnt, axis=1, keepdims=True)
    carry_ref[...] = carry
    cnt_out[...] = carry


def _dest_kernel(cnt_ref, idx_ref, pos_ref, dest_out, be_out, nbu_out):
    idx = idx_ref[...]
    dest = pos_ref[...]
    bidx = lax.broadcasted_iota(I32, be_out.shape, 1)
    be = jnp.zeros(be_out.shape, I32)
    run = jnp.int32(0)
    for e in range(N_EXPERTS):
        dest = dest + jnp.where(idx == e, run, 0)
        run = run + ((cnt_ref[e] + (ROW_BLOCK - 1)) // ROW_BLOCK) * ROW_BLOCK
        be = be + jnp.where(bidx >= run // ROW_BLOCK, 1, 0)
    dest_out[...] = dest
    be_out[...] = jnp.minimum(be, N_EXPERTS - 1)
    nbu_out[...] = jnp.zeros(nbu_out.shape, I32) + run // ROW_BLOCK


def _expert_ffn_kernel(be_ref, nbu_ref, nxt_ref, xa_ref, xb_ref, w1_hbm, b1_ref, w2_hbm, b2_ref, ya_ref, yb_ref,
                       w1f_ref, w2f_ref, w1s_ref, w2s_ref, slot_ref, sem_ref):
    b = pl.program_id(0)

    def weight_copies(e, slot):
        return (pltpu.make_async_copy(w1_hbm.at[e], w1f_ref.at[slot], sem_ref.at[slot, 0]),
                pltpu.make_async_copy(w2_hbm.at[e], w2f_ref.at[slot], sem_ref.at[slot, 1]))

    @pl.when(b < nbu_ref[0])
    def _():
        e = be_ref[b]

        @pl.when(b == 0)
        def _():
            slot_ref[0] = 0
            for cp in weight_copies(e, 0):
                cp.start()

        @pl.when(jnp.logical_or(b == 0, e != be_ref[jnp.maximum(b - 1, 0)]))
        def _():
            slot = slot_ref[0]
            for cp in weight_copies(e, slot):
                cp.wait()
            w1s_ref[...] = w1f_ref[slot].astype(BF16)
            w2s_ref[...] = w2f_ref[slot].astype(BF16)
            nxt = nxt_ref[e]

            @pl.when(nxt >= 0)
            def _():
                for cp in weight_copies(nxt, 1 - slot):
                    cp.start()

            slot_ref[0] = 1 - slot

        parts = _unpack_bf16_pairs(xa_ref[...]) + _unpack_bf16_pairs(xb_ref[...])
        xrow = jnp.concatenate([p.astype(BF16) for p in parts], axis=1)
        hm = _dot(xrow, w1s_ref[...]) + b1_ref[0]
        glu = jnp.minimum(hm[:, :D_FF], SWIGLU_LIMIT)
        lin = jnp.clip(hm[:, D_FF:], -SWIGLU_LIMIT, SWIGLU_LIMIT)
        act = glu * jax.nn.sigmoid(SWIGLU_ALPHA * glu) * (lin + 1.0)
        y = _dot(act.astype(BF16), w2s_ref[...]) + b2_ref[0]
        ya_ref[...] = _pack_bf16_pairs(y[:, :D_MODEL // 2])
        yb_ref[...] = _pack_bf16_pairs(y[:, D_MODEL // 2:])


SC_WINDOW = 128


def _sc_mesh():
    return plsc.VectorSubcoreMesh(core_axis_name="c", subcore_axis_name="s")


def _sc_gather_rows(table, idx_row):
    n = idx_row.shape[1]
    width = table.shape[1]

    @functools.partial(pl.kernel, out_type=jax.ShapeDtypeStruct((n, width), table.dtype), mesh=_sc_mesh(),
                       scratch_types=[])
    def gather_kernel(t_hbm, i_hbm, o_hbm):
        def body(i_vmem, o_vmem):
            pltpu.sync_copy(t_hbm.at[i_vmem.at[0]], o_vmem)

        pltpu.emit_pipeline(
            body,
            grid=(n // SC_WINDOW,),
            in_specs=[pl.BlockSpec((1, SC_WINDOW), lambda i: (0, i))],
            out_specs=[pl.BlockSpec((SC_WINDOW, width), lambda i: (i, 0))],
            core_axis_name=("c", "s"),
            dimension_semantics=(pltpu.PARALLEL,),
        )(i_hbm, o_hbm)

    return gather_kernel(table, idx_row)


def _sc_scatter_rows(rows, idx_rows, n_out):
    n, width = rows.shape

    @functools.partial(pl.kernel, out_type=jax.ShapeDtypeStruct((n_out, width), rows.dtype), mesh=_sc_mesh(),
                       scratch_types=[])
    def scatter_kernel(r_hbm, *refs):
        i_hbms, o_hbm = refs[:-1], refs[-1]

        def body(r_vmem, *i_vmems):
            for i_vmem in i_vmems:
                pltpu.sync_copy(r_vmem, o_hbm.at[i_vmem.at[0]])

        pltpu.emit_pipeline(
            body,
            grid=(n // SC_WINDOW,),
            in_specs=[pl.BlockSpec((SC_WINDOW, width), lambda i: (i, 0))]
            + [pl.BlockSpec((1, SC_WINDOW), lambda i: (0, i)) for _ in i_hbms],
            out_specs=[],
            core_axis_name=("c", "s"),
            dimension_semantics=(pltpu.PARALLEL,),
        )(r_hbm, *i_hbms)

    return scatter_kernel(rows, *idx_rows)


def _combine_kernel(x1_ref, yga_ref, ygb_ref, gate_ref, spread_ref, o_ref):
    q = D_MODEL // 4
    x1 = x1_ref[...]
    acc = [x1[:, j * q:(j + 1) * q] for j in range(4)]
    gfull = lax.dot_general(gate_ref[...], spread_ref[...], (((0,), (0,)), ((), ())), preferred_element_type=F32)
    for k in range(TOP_K):
        parts = _unpack_bf16_pairs(yga_ref[k]) + _unpack_bf16_pairs(ygb_ref[k])
        g = jnp.tile(gfull[:, k * LANES:(k + 1) * LANES], (1, q // LANES))
        acc = [a + g * p for a, p in zip(acc, parts)]
    for j in range(4):
        o_ref[:, j * q:(j + 1) * q] = acc[j]


def _combine_into_kernel(x1_ref, yga_ref, ygb_ref, gate_ref, spread_ref, prev_ref, o_ref):
    del prev_ref
    _combine_kernel(x1_ref, yga_ref, ygb_ref, gate_ref, spread_ref, o_ref)


def _block_diag_ones(width):
    r = jnp.arange(width) // HEAD_GROUP
    return (r[:, None] == r[None, :]).astype(BF16)


def _rope_tables(positions):
    half = ROT_DIM // 2
    inv_freq = ROPE_THETA ** (-jnp.arange(0, ROT_DIM, 2, dtype=F32) / ROT_DIM)
    ang = inv_freq[:, None] * positions.astype(F32).reshape(1, -1)
    cs = jnp.concatenate([jnp.cos(ang), jnp.sin(ang)], axis=0)
    cs_hi = cs.astype(BF16)
    cs_lo = (cs - cs_hi.astype(F32)).astype(BF16)
    lane = jnp.arange(LANES) % HEAD_GROUP
    j = jnp.arange(half)[:, None]
    lo_half = (lane[None, :] == j).astype(F32)
    hi_half = (lane[None, :] == j + half).astype(F32)
    spread = jnp.concatenate([
        jnp.concatenate([lo_half + hi_half, jnp.zeros((half, LANES), F32)], axis=1),
        jnp.concatenate([jnp.zeros((half, LANES), F32), hi_half - lo_half], axis=1)], axis=0)
    return jnp.concatenate([cs_hi, cs_lo], axis=0), jnp.concatenate([spread, spread], axis=0).astype(BF16)


def _full(shape):
    return pl.BlockSpec(shape, lambda *_: (0,) * len(shape))


def kernel(x, mem, positions, g_mix_norm, w_in, g_dq, g_dk, lambda_q1, lambda_k1, lambda_q2, lambda_k2, g_subln, g_sgu, w_spatial, b_spatial, g_mem_norm, w_mem_kv, g_cq, g_ck, w_out, g_ffn_norm, w_router, b_router, w_mlp1, b_mlp1, w_mlp2, b_mlp2):
    B, S, D = x.shape
    M = mem.shape[1]
    tm = TOKEN_TILE
    tr = ROUTER_TILE
    assert D == D_MODEL and S % tm == 0 and S % ATTN_TILE == 0 and g_mix_norm.shape[0] == 1
    n_parts = LAYER_PARTS if B % LAYER_PARTS == 0 and (B // LAYER_PARTS) * S % tr == 0 else 1
    Bp = B // n_parts
    N = Bp * S
    n_assign = N * TOP_K
    n_blocks = -(-n_assign // ROW_BLOCK) + N_EXPERTS
    n_rows = n_blocks * ROW_BLOCK
    nb_pad = -(-n_blocks // LANES) * LANES

    xf = x.reshape(B * S, D)
    rope_cs, rope_spread = _rope_tables(positions)
    ones256 = _block_diag_ones(MEM_W)
    row = lambda v: v.reshape(1, -1).astype(F32)
    tile_row = lambda v, reps: jnp.tile(v.reshape(1, -1).astype(F32), (1, reps))
    w_in_b, w_out_b, w_kv_b = w_in[0].astype(BF16), w_out[0].astype(BF16), w_mem_kv[0].astype(BF16)
    w_sp_lanes = jnp.transpose(w_spatial[0], (1, 0, 2)).reshape(CHUNK, GMLP_GROUPS * CHUNK)
    b_sp_lanes = jnp.repeat(b_spatial[0].T, HEAD_GROUP, axis=1)
    wr = w_router[0].T.astype(F32)
    wr_hi = wr.astype(BF16)
    wr_lo = (wr - wr_hi.astype(F32)).astype(BF16)
    before = (jnp.arange(tr)[:, None] < jnp.arange(tr)[None, :]).astype(BF16)
    gate_row = jnp.arange(GATE_ROWS)[:, None]
    gate_spread = ((gate_row < 2 * TOP_K) & (gate_row % TOP_K == jnp.arange(TOP_K * LANES)[None, :] // LANES)).astype(BF16)
    b1r = b_mlp1[0].reshape(N_EXPERTS, 1, 2 * D_FF)
    b2r = b_mlp2[0].reshape(N_EXPERTS, 1, D)

    out = None
    for part in range(n_parts):
        b_off = part * Bp
        t_off = part * (N // tm)
        r_off = part * (N // tr)

        kT, vm = pl.pallas_call(
            _mem_kv_kernel,
            grid=(Bp,),
            in_specs=[pl.BlockSpec((1, M, D), lambda b: (b + b_off, 0, 0)), _full((1, D)), _full((D, 2 * MEM_W)),
                      _full((1, MEM_W)), _full((MEM_W, MEM_W))],
            out_specs=[pl.BlockSpec((1, MEM_W, M), lambda b: (b, 0, 0)), pl.BlockSpec((1, M, MEM_W), lambda b: (b, 0, 0))],
            out_shape=[jax.ShapeDtypeStruct((Bp, MEM_W, M), BF16), jax.ShapeDtypeStruct((Bp, M, MEM_W), BF16)],
            compiler_params=_cparams(("parallel",)),
            name="mem_kv",
        )(mem, row(g_mem_norm[0]), w_kv_b, tile_row(g_ck[0], N_MEM_HEADS), ones256)

        tiles_per_batch = S // tm
        tok = lambda w: pl.BlockSpec((tm, w), lambda i: (i, 0))
        tok_in = lambda w: pl.BlockSpec((tm, w), lambda i: (i + t_off, 0))
        qn, kn, vv, gc = pl.pallas_call(
            _mixer_in_kernel,
            grid=(N // tm,),
            in_specs=[tok_in(D), pl.BlockSpec((4 * ROT_DIM // 2, tm), lambda i: (0, i + t_off)),
                      _full((4 * ROT_DIM // 2, 2 * LANES)),
                      _full((1, D)), _full((D, D_IN_PROJ)),
                      _full((1, DIFF_W)), _full((1, DIFF_W)),
                      _full((1, GMLP_W)), _full((CHUNK, GMLP_GROUPS * CHUNK)), _full((CHUNK, GMLP_W)),
                      _full((1, MEM_W)), _full((MEM_W, MEM_W)),
                      pl.BlockSpec((1, MEM_W, M), lambda i: (i // tiles_per_batch, 0, 0)),
                      pl.BlockSpec((1, M, MEM_W), lambda i: (i // tiles_per_batch, 0, 0))],
            out_specs=[tok(DIFF_W), tok(DIFF_W), tok(DIFF_W), tok(GMLP_W + MEM_W)],
            out_shape=[jax.ShapeDtypeStruct((N, DIFF_W), BF16)] * 3 + [jax.ShapeDtypeStruct((N, GMLP_W + MEM_W), BF16)],
            compiler_params=_cparams(("parallel",)),
            name="mixer_in",
        )(xf, rope_cs, rope_spread, row(g_mix_norm[0]), w_in_b,
          tile_row(g_dq[0], 2 * N_DIFF_HEADS), tile_row(g_dk[0], 2 * N_DIFF_HEADS),
          row(g_sgu[0]), w_sp_lanes, b_sp_lanes, tile_row(g_cq[0], N_MEM_HEADS), ones256, kT, vm)

        tq = ATTN_TILE
        nq = S // tq
        head_q = pl.BlockSpec((tq, DIFF_W), lambda b, i: (b * nq + i, 0))
        head_kv = pl.BlockSpec((S, DIFF_W), lambda b, i: (b, 0))
        lam_spec = pl.BlockSpec((1, DIFF_HEAD_DIM), lambda b, i: (0, 0))
        nh = N_DIFF_HEADS
        attn = pl.pallas_call(
            _diff_attn_kernel,
            grid=(Bp, nq),
            in_specs=[head_q, head_kv, head_kv, lam_spec, lam_spec, lam_spec, lam_spec,
                      pl.BlockSpec((DIFF_V_DIM, 1), lambda b, i: (0, 0))],
            out_specs=head_q,
            out_shape=jax.ShapeDtypeStruct((N, DIFF_W), BF16),
            scratch_shapes=[pltpu.VMEM((nh, tq, 2 * tq), F32), pltpu.VMEM((nh, tq, 2 * tq), F32),
                            pltpu.VMEM((nh, 1, 2 * tq), F32), pltpu.VMEM((nh, ATTN_EXT_ROWS, 2 * tq), F32),
                            pltpu.VMEM((nh, ATTN_EXT_ROWS, S), BF16)],
            compiler_params=_cparams(("parallel", "arbitrary")),
            name="diff_attn",
        )(qn, kn, vv, row(lambda_q1[0]), row(lambda_k1[0]), row(lambda_q2[0]), row(lambda_k2[0]),
          g_subln[0].reshape(DIFF_V_DIM, 1).astype(F32))

        rtok = lambda w: pl.BlockSpec((tr, w), lambda i: (i, 0))
        rtok_in = lambda w: pl.BlockSpec((tr, w), lambda i: (i + r_off, 0))
        tokT = lambda: pl.BlockSpec((TOP_K, tr), lambda i: (0, i))
        hw = D // 4
        x1, hpa, hpb, idxT, gateT, posT, counts = pl.pallas_call(
            _out_router_kernel,
            grid=(N // tr,),
            in_specs=[rtok_in(D), rtok(DIFF_W), rtok(GMLP_W + MEM_W), _full((D, D)), _full((1, D)),
                      _full((N_EXPERTS, D)), _full((N_EXPERTS, D)), _full((N_EXPERTS, 1)), _full((tr, tr))],
            out_specs=[rtok(D), rtok(hw), rtok(hw), tokT(), pl.BlockSpec((GATE_ROWS, tr), lambda i: (0, i)), tokT(),
                       _full((N_EXPERTS, 1))],
            out_shape=[jax.ShapeDtypeStruct((N, D), F32), jax.ShapeDtypeStruct((N, hw), U32),
                       jax.ShapeDtypeStruct((N, hw), U32), jax.ShapeDtypeStruct((TOP_K, N), I32),
                       jax.ShapeDtypeStruct((GATE_ROWS, N), BF16),
                       jax.ShapeDtypeStruct((TOP_K, N), I32), jax.ShapeDtypeStruct((N_EXPERTS, 1), F32)],
            scratch_shapes=[pltpu.VMEM((N_EXPERTS, 1), F32)],
            compiler_params=_cparams(("arbitrary",)),
            name="out_router",
        )(xf, attn, gc, w_out_b, row(g_ffn_norm[0]), wr_hi, wr_lo, b_router[0].reshape(N_EXPERTS, 1).astype(F32), before)

        destT, block_expert, nb_used = pl.pallas_call(
            _dest_kernel,
            grid_spec=pltpu.PrefetchScalarGridSpec(
                num_scalar_prefetch=1,
                grid=(1,),
                in_specs=[pl.BlockSpec((TOP_K, N), lambda i, c: (0, 0)), pl.BlockSpec((TOP_K, N), lambda i, c: (0, 0))],
                out_specs=[pl.BlockSpec((TOP_K, N), lambda i, c: (0, 0)), pl.BlockSpec((1, nb_pad), lambda i, c: (0, 0)),
                           pl.BlockSpec((1, LANES), lambda i, c: (0, 0))],
            ),
            out_shape=[jax.ShapeDtypeStruct((TOP_K, N), I32), jax.ShapeDtypeStruct((1, nb_pad), I32),
                       jax.ShapeDtypeStruct((1, LANES), I32)],
            compiler_params=_cparams(("arbitrary",)),
            name="dest",
        )(counts.reshape(N_EXPERTS).astype(I32), idxT, posT)

        dest_rows = [destT[k].reshape(1, N) for k in range(TOP_K)]
        xa_buf = _sc_scatter_rows(hpa, dest_rows, n_rows)
        xb_buf = _sc_scatter_rows(hpb, dest_rows, n_rows)

        cnt_i = counts.reshape(N_EXPERTS).astype(I32)
        owner = jnp.where(cnt_i > 0, jnp.arange(N_EXPERTS, dtype=I32), N_EXPERTS)
        later = jnp.concatenate([lax.cummin(owner[::-1])[::-1][1:], jnp.full((1,), N_EXPERTS, I32)])
        next_expert = jnp.where(later < N_EXPERTS, later, -1)
        last = lambda b, be, nbu, nxt: jnp.minimum(b, nbu[0] - 1)
        row_blk = lambda: pl.BlockSpec((ROW_BLOCK, hw), lambda b, be, nbu, nxt: (last(b, be, nbu, nxt), 0))
        ya_buf, yb_buf = pl.pallas_call(
            _expert_ffn_kernel,
            grid_spec=pltpu.PrefetchScalarGridSpec(
                num_scalar_prefetch=3,
                grid=(n_blocks,),
                in_specs=[row_blk(), row_blk(),
                          pl.BlockSpec(memory_space=pl.ANY),
                          pl.BlockSpec((1, 1, 2 * D_FF), lambda b, be, nbu, nxt: (be[b], 0, 0)),
                          pl.BlockSpec(memory_space=pl.ANY),
                          pl.BlockSpec((1, 1, D), lambda b, be, nbu, nxt: (be[b], 0, 0))],
                out_specs=[row_blk(), row_blk()],
                scratch_shapes=[pltpu.VMEM((2, D, 2 * D_FF), F32), pltpu.VMEM((2, D_FF, D), F32),
                                pltpu.VMEM((D, 2 * D_FF), BF16), pltpu.VMEM((D_FF, D), BF16),
                                pltpu.SMEM((1,), I32), pltpu.SemaphoreType.DMA((2, 2))],
            ),
            out_shape=[jax.ShapeDtypeStruct((n_rows, hw), U32)] * 2,
            compiler_params=_cparams(("arbitrary",)),
            name="expert_ffn",
        )(block_expert[0, :n_blocks], nb_used[0, :1], next_expert, xa_buf, xb_buf, w_mlp1[0], b1r, w_mlp2[0], b2r)

        dest_flat = destT.reshape(1, n_assign)
        yga = _sc_gather_rows(ya_buf, dest_flat).reshape(TOP_K, N, hw)
        ygb = _sc_gather_rows(yb_buf, dest_flat).reshape(TOP_K, N, hw)

        prev = () if out is None else (out,)
        out = pl.pallas_call(
            _combine_kernel if out is None else _combine_into_kernel,
            grid=(N // tm,),
            in_specs=[tok(D), pl.BlockSpec((TOP_K, tm, hw), lambda i: (0, i, 0)),
                      pl.BlockSpec((TOP_K, tm, hw), lambda i: (0, i, 0)), pl.BlockSpec((GATE_ROWS, tm), lambda i: (0, i)),
                      _full((GATE_ROWS, TOP_K * LANES))] + [pl.BlockSpec(memory_space=pl.ANY)] * len(prev),
            out_specs=tok_in(D),
            out_shape=jax.ShapeDtypeStruct((B * S, D), F32),
            input_output_aliases={5: 0} if prev else {},
            compiler_params=_cparams(("parallel",)),
            name="combine",
        )(x1, yga, ygb, gateT, gate_spread, *prev)
    return out.reshape(B, S, D)
```

```python
import functools

import jax
import jax.numpy as jnp
from jax import lax
from jax.experimental import pallas as pl
from jax.experimental.pallas import tpu as pltpu
from jax.experimental.pallas import tpu_sc as plsc

F32 = jnp.float32
BF16 = jnp.bfloat16
I32 = jnp.int32
U32 = jnp.uint32

D_MODEL = 1024
N_DIFF_HEADS = 4
DIFF_HEAD_DIM = 64
DIFF_V_DIM = 128
DIFF_W = 512
GMLP_W = 256
GMLP_GROUPS = 4
CHUNK = 128
MEM_W = 256
N_MEM_HEADS = 4
HEAD_GROUP = 64
D_IN_PROJ = 2304
ROPE_THETA = 500000.0
ROT_DIM = 16
N_EXPERTS = 32
TOP_K = 4
D_FF = 1024
SWIGLU_LIMIT = 7.0
SWIGLU_ALPHA = 1.702
EPS = 1e-6
LAMBDA_INIT = 0.8 - 0.6

LANES = 128
ROW_BLOCK = 512
TOKEN_TILE = 512
LAYER_PARTS = 1
ROUTER_TILE = 1024
GATE_ROWS = 16
ATTN_TILE = 256
ATTN_EXT_ROWS = 128 + 16
VMEM_LIMIT = 56 * 1024 * 1024
NEG_BIG = -1e30
LOG2_E = 1.4426950408889634


def _cparams(sem):
    return pltpu.CompilerParams(dimension_semantics=sem, vmem_limit_bytes=VMEM_LIMIT)


def _dot(a, b):
    return jnp.dot(a, b, preferred_element_type=F32)


def _dot_nt(a, b):
    return lax.dot_general(a, b, (((1,), (1,)), ((), ())), preferred_element_type=F32)


def _rms(x, gain):
    ms = jnp.mean(x * x, axis=-1, keepdims=True)
    return x * lax.rsqrt(ms + EPS) * gain


def _group_rms(t, ones_bd, gain):
    w = ones_bd.shape[0]
    chunks = []
    for j in range(t.shape[1] // w):
        c = t[:, j * w:(j + 1) * w]
        ss = _dot((c * c).astype(BF16), ones_bd)
        chunks.append(c * lax.rsqrt(ss * (1.0 / HEAD_GROUP) + EPS))
    return (chunks[0] if len(chunks) == 1 else jnp.concatenate(chunks, axis=1)) * gain


def _pack_bf16_pairs(v):
    w = v.shape[1] // 2
    bits = lax.bitcast_convert_type(v.astype(BF16).astype(F32), U32)
    return (bits[:, :w] & jnp.uint32(0xFFFF0000)) | (bits[:, w:] >> jnp.uint32(16))


def _unpack_bf16_pairs(words):
    hi = lax.bitcast_convert_type(words & jnp.uint32(0xFFFF0000), F32)
    lo = lax.bitcast_convert_type(words << jnp.uint32(16), F32)
    return hi, lo


def _mem_kv_kernel(mem_ref, gmem_ref, wkv_ref, gck_ref, ones_ref, kT_ref, v_ref):
    m = _rms(mem_ref[0], gmem_ref[...]).astype(BF16)
    kv = _dot(m, wkv_ref[...])
    k = _group_rms(kv[:, :MEM_W], ones_ref[...], gck_ref[...])
    kT_ref[0] = k.T.astype(BF16)
    v_ref[0] = kv[:, MEM_W:].astype(BF16)


def _gelu_tanh(x):
    return 0.5 * x * (1.0 + jnp.tanh(0.7978845608028654 * (x + 0.044715 * (x * x * x))))


def _mixer_in_kernel(x_ref, cs_ref, spread_ref, gmix_ref, win_ref, gq_ref, gk_ref,
                     gsgu_ref, wsp_ref, bsp_ref, gcq_ref, ones256_ref, kT_ref, vm_ref,
                     q_out, k_out, v_out, gc_out):
    tm = x_ref.shape[0]
    hb = _rms(x_ref[...], gmix_ref[...]).astype(BF16)

    def proj(lo, hi):
        return _dot(hb, win_ref[:, lo:hi])

    lane = lax.broadcasted_iota(I32, (tm, LANES), 1)
    first_half = (lane % HEAD_GROUP) < (ROT_DIM // 2)
    tab = lax.dot_general(cs_ref[...], spread_ref[...], (((0,), (0,)), ((), ())), preferred_element_type=F32)
    cosb = tab[:, :LANES] + jnp.where((lane % HEAD_GROUP) >= ROT_DIM, 1.0, 0.0)
    sinb = tab[:, LANES:]

    def norm_rope(t, gain, out_ref):
        tn = _group_rms(t, ones256_ref[...], gain)
        for j in range(DIFF_W // LANES):
            c = tn[:, j * LANES:(j + 1) * LANES]
            partner = jnp.where(first_half, pltpu.roll(c, LANES - ROT_DIM // 2, 1), pltpu.roll(c, ROT_DIM // 2, 1))
            out_ref[:, j * LANES:(j + 1) * LANES] = (c * cosb + partner * sinb).astype(BF16)

    norm_rope(proj(0, DIFF_W), gq_ref[...] * (DIFF_HEAD_DIM ** -0.5 * LOG2_E), q_out)
    norm_rope(proj(DIFF_W, 2 * DIFF_W), gk_ref[...], k_out)
    v_out[...] = proj(2 * DIFF_W, 3 * DIFF_W).astype(BF16)

    z = _gelu_tanh(proj(3 * DIFF_W, 3 * DIFF_W + 2 * GMLP_W))
    u = z[:, :GMLP_W]
    vg = z[:, GMLP_W:]
    vc = vg - jnp.mean(vg, axis=-1, keepdims=True)
    vgn = (vc * lax.rsqrt(jnp.mean(vc * vc, axis=-1, keepdims=True) + EPS) * gsgu_ref[...]).astype(BF16)
    wrow = lax.broadcasted_iota(I32, (CHUNK, GMLP_GROUPS * CHUNK), 0)
    wcol = lax.broadcasted_iota(I32, (CHUNK, GMLP_GROUPS * CHUNK), 1) % CHUNK
    w_causal = jnp.where(wcol <= wrow, wsp_ref[...], 0.0).astype(BF16)
    grp = lax.broadcasted_iota(I32, (CHUNK, GMLP_W), 1) // HEAD_GROUP
    zero_b = jnp.zeros((CHUNK, GMLP_W), BF16)
    for r in range(tm // CHUNK):
        vchunk = vgn[r * CHUNK:(r + 1) * CHUNK, :]
        v_bd = jnp.concatenate([jnp.where(grp == g, vchunk, zero_b) for g in range(GMLP_GROUPS)], axis=0)
        mixed = _dot(w_causal, v_bd) + bsp_ref[...]
        gc_out[r * CHUNK:(r + 1) * CHUNK, 0:GMLP_W] = (u[r * CHUNK:(r + 1) * CHUNK, :] * mixed).astype(BF16)

    pc = proj(3 * DIFF_W + 2 * GMLP_W, D_IN_PROJ)
    qc = _group_rms(pc, ones256_ref[...], gcq_ref[...] * (HEAD_GROUP ** -0.5)).astype(BF16)
    hgrp = lax.broadcasted_iota(I32, (tm, MEM_W), 1) // HEAD_GROUP
    zero_q = jnp.zeros((tm, MEM_W), BF16)
    q_st = jnp.concatenate([jnp.where(hgrp == h, qc, zero_q) for h in range(N_MEM_HEADS)], axis=0)
    s = _dot(q_st, kT_ref[0])
    p = jnp.exp(s - jnp.max(s, axis=-1, keepdims=True))
    o = _dot(p.astype(BF16), vm_ref[0]) / jnp.sum(p, axis=-1, keepdims=True)
    c = jnp.zeros((tm, MEM_W), F32)
    for h in range(N_MEM_HEADS):
        c = c + jnp.where(hgrp == h, o[h * tm:(h + 1) * tm, :], 0.0)
    gc_out[:, GMLP_W:GMLP_W + MEM_W] = c.astype(BF16)


def _diff_attn_kernel(q_ref, k_ref, v_ref, lq1_ref, lk1_ref, lq2_ref, lk2_ref, gsub_ref, o_ref,
                      s0_ref, s1_ref, m_ref, acc_ref, vT_ref):
    tq = q_ref.shape[0]
    seq = k_ref.shape[0]
    i = pl.program_id(1)
    lane = lax.broadcasted_iota(I32, (tq, LANES), 1)
    heads = range(N_DIFF_HEADS)
    hl = lambda h: slice(h * DIFF_V_DIM, (h + 1) * DIFF_V_DIM)
    ext_rows = vT_ref.shape[1]

    @pl.when(i == 0)
    def _():
        ones_row = jnp.where(lax.broadcasted_iota(I32, (ext_rows - DIFF_V_DIM, seq), 0) == 0, 1.0, 0.0).astype(BF16)
        for h in heads:
            for c in range(seq // tq):
                vT_ref[h, 0:DIFF_V_DIM, c * tq:(c + 1) * tq] = v_ref[c * tq:(c + 1) * tq, hl(h)].T
            vT_ref[h, DIFF_V_DIM:ext_rows, :] = ones_row

    def stacked_qT(h):
        q = q_ref[:, hl(h)]
        zero = jnp.zeros_like(q)
        return jnp.concatenate([jnp.where(lane < DIFF_HEAD_DIM, q, zero), jnp.where(lane >= DIFF_HEAD_DIM, q, zero)],
                               axis=0).T

    qsT = [stacked_qT(h) for h in heads]

    def scores(t, s_ref):
        rows = pl.ds(pl.multiple_of(t * tq, tq), tq)
        for h in heads:
            s_ref[h] = _dot(k_ref[rows, hl(h)], qsT[h])

    def update(t, s_ref, causal):
        cols = pl.ds(pl.multiple_of(t * tq, tq), tq)
        for h in heads:
            s = s_ref[h]
            if causal:
                key = lax.broadcasted_iota(I32, (tq, 2 * tq), 0)
                qry = lax.broadcasted_iota(I32, (tq, 2 * tq), 1) % tq
                s = jnp.where(key <= qry, s, NEG_BIG)
            m = m_ref[h]
            m_new = jnp.maximum(m, jnp.max(s, axis=0, keepdims=True))
            alpha = jnp.exp2(m - m_new)
            p = jnp.exp2(s - m_new)
            m_ref[h] = m_new
            acc_ref[h] = alpha * acc_ref[h] + _dot(vT_ref[h, :, cols], p.astype(BF16))

    m_ref[...] = jnp.full(m_ref.shape, NEG_BIG, F32)
    acc_ref[...] = jnp.zeros(acc_ref.shape, F32)
    scores(0, s0_ref)

    def pair(pidx, carry):
        t = 2 * pidx
        scores(t + 1, s1_ref)
        update(t, s0_ref, False)
        scores(t + 2, s0_ref)
        update(t + 1, s1_ref, False)
        return carry

    lax.fori_loop(0, i // 2, pair, 0)

    @pl.when(i % 2 == 0)
    def _():
        update(i, s0_ref, True)

    @pl.when(i % 2 == 1)
    def _():
        scores(i, s1_ref)
        update(i - 1, s0_ref, False)
        update(i, s1_ref, True)

    lam = (jnp.exp(jnp.sum(lq1_ref[...] * lk1_ref[...], axis=-1, keepdims=True))
           - jnp.exp(jnp.sum(lq2_ref[...] * lk2_ref[...], axis=-1, keepdims=True)) + LAMBDA_INIT)
    for h in heads:
        on = acc_ref[h, 0:DIFF_V_DIM, :] * (1.0 / acc_ref[h, DIFF_V_DIM:DIFF_V_DIM + 1, :])
        o = on[:, :tq] - lam * on[:, tq:]
        ms = jnp.mean(o * o, axis=0, keepdims=True)
        o = o * lax.rsqrt(ms + EPS) * gsub_ref[...] * (1.0 - LAMBDA_INIT)
        o_ref[:, hl(h)] = o.T.astype(BF16)


def _out_router_kernel(x_ref, a_ref, gc_ref, wo_ref, gffn_ref, wrh_ref, wrl_ref, br_ref, before_ref,
                       x1_out, hpa_out, hpb_out, idx_out, gate_out, pos_out, cnt_out, carry_ref):
    tm = x_ref.shape[0]

    @pl.when(pl.program_id(0) == 0)
    def _():
        carry_ref[...] = jnp.zeros_like(carry_ref)

    sub = before_ref.shape[0]
    wr_stack = jnp.concatenate([wrh_ref[...], wrl_ref[...]], axis=0)
    eio = lax.broadcasted_iota(I32, (N_EXPERTS, sub), 0)
    before_b = before_ref[...]
    carry = carry_ref[...]
    for r in range(tm // sub):
        rows = slice(r * sub, (r + 1) * sub)
        mix = jnp.concatenate([a_ref[rows, :], gc_ref[rows, :]], axis=1)
        x1 = x_ref[rows, :] + _dot(mix, wo_ref[...])
        x1_out[rows, :] = x1
        h2 = _rms(x1, gffn_ref[...])
        hb = h2.astype(BF16)
        hpa_out[rows, :] = _pack_bf16_pairs(h2[:, :D_MODEL // 2])
        hpb_out[rows, :] = _pack_bf16_pairs(h2[:, D_MODEL // 2:])

        h_lo = (h2 - hb.astype(F32)).astype(BF16)
        both = _dot_nt(wr_stack, hb)
        logits = (both[:N_EXPERTS] + both[N_EXPERTS:]) + _dot_nt(wrh_ref[...], h_lo) + br_ref[...]

        vals, idxs, sels = [], [], []
        cur = logits
        for _ in range(TOP_K):
            m = jnp.max(cur, axis=0, keepdims=True)
            ik = jnp.min(jnp.where(cur == m, eio, N_EXPERTS), axis=0, keepdims=True)
            sel = eio == ik
            cur = jnp.where(sel, -jnp.inf, cur)
            vals.append(m)
            idxs.append(ik)
            sels.append(sel)
        es = [jnp.exp(v - vals[0]) for v in vals]
        tot = es[0] + es[1] + es[2] + es[3]
        gates = jnp.concatenate([e / tot for e in es], axis=0)
        g_hi = gates.astype(BF16)
        g_lo = (gates - g_hi.astype(F32)).astype(BF16)
        gate_out[:, rows] = jnp.concatenate(
            [g_hi, g_lo, jnp.zeros((GATE_ROWS - 2 * TOP_K, sub), BF16)], axis=0)
        idx_out[:, rows] = jnp.concatenate(idxs, axis=0)

        cnt = jnp.zeros((N_EXPERTS, sub), F32)
        for sel in sels:
            cnt = cnt + jnp.where(sel, 1.0, 0.0)
        base = carry + _dot(cnt.astype(BF16), before_b)
        pos_out[:, rows] = jnp.concatenate(
            [jnp.sum(jnp.where(sel, base, 0.0), axis=0, keepdims=True) for sel in sels], axis=0).astype(I32)
        carry = carry + jnp.sum(cnt, axis=1, keepdims=True)
    carry_ref[...] = carry
    cnt_out[...] = carry


def _dest_kernel(cnt_ref, idx_ref, pos_ref, dest_out, be_out, nbu_out):
    idx = idx_ref[...]
    dest = pos_ref[...]
    bidx = lax.broadcasted_iota(I32, be_out.shape, 1)
    be = jnp.zeros(be_out.shape, I32)
    run = jnp.int32(0)
    for e in range(N_EXPERTS):
        dest = dest + jnp.where(idx == e, run, 0)
        run = run + ((cnt_ref[e] + (ROW_BLOCK - 1)) // ROW_BLOCK) * ROW_BLOCK
        be = be + jnp.where(bidx >= run // ROW_BLOCK, 1, 0)
    dest_out[...] = dest
    be_out[...] = jnp.minimum(be, N_EXPERTS - 1)
    nbu_out[...] = jnp.zeros(nbu_out.shape, I32) + run // ROW_BLOCK


def _expert_ffn_kernel(be_ref, nbu_ref, nxt_ref, xa_ref, xb_ref, w1_hbm, b1_ref, w2_hbm, b2_ref, ya_ref, yb_ref,
                       w1f_ref, w2f_ref, w1s_ref, w2s_ref, slot_ref, sem_ref):
    b = pl.program_id(0)

    def weight_copies(e, slot):
        return (pltpu.make_async_copy(w1_hbm.at[e], w1f_ref.at[slot], sem_ref.at[slot, 0]),
                pltpu.make_async_copy(w2_hbm.at[e], w2f_ref.at[slot], sem_ref.at[slot, 1]))

    @pl.when(b < nbu_ref[0])
    def _():
        e = be_ref[b]

        @pl.when(b == 0)
        def _():
            slot_ref[0] = 0
            for cp in weight_copies(e, 0):
                cp.start()

        @pl.when(jnp.logical_or(b == 0, e != be_ref[jnp.maximum(b - 1, 0)]))
        def _():
            slot = slot_ref[0]
            for cp in weight_copies(e, slot):
                cp.wait()
            w1s_ref[...] = w1f_ref[slot].astype(BF16)
            w2s_ref[...] = w2f_ref[slot].astype(BF16)
            nxt = nxt_ref[e]

            @pl.when(nxt >= 0)
            def _():
                for cp in weight_copies(nxt, 1 - slot):
                    cp.start()

            slot_ref[0] = 1 - slot

        parts = _unpack_bf16_pairs(xa_ref[...]) + _unpack_bf16_pairs(xb_ref[...])
        xrow = jnp.concatenate([p.astype(BF16) for p in parts], axis=1)
        hm = _dot(xrow, w1s_ref[...]) + b1_ref[0]
        glu = jnp.minimum(hm[:, :D_FF], SWIGLU_LIMIT)
        lin = jnp.clip(hm[:, D_FF:], -SWIGLU_LIMIT, SWIGLU_LIMIT)
        act = glu * jax.nn.sigmoid(SWIGLU_ALPHA * glu) * (lin + 1.0)
        y = _dot(act.astype(BF16), w2s_ref[...]) + b2_ref[0]
        ya_ref[...] = _pack_bf16_pairs(y[:, :D_MODEL // 2])
        yb_ref[...] = _pack_bf16_pairs(y[:, D_MODEL // 2:])


SC_WINDOW = 128


def _sc_mesh():
    return plsc.VectorSubcoreMesh(core_axis_name="c", subcore_axis_name="s")


def _sc_gather_rows(table, idx_row):
    n = idx_row.shape[1]
    width = table.shape[1]

    @functools.partial(pl.kernel, out_type=jax.ShapeDtypeStruct((n, width), table.dtype), mesh=_sc_mesh(),
                       scratch_types=[])
    def gather_kernel(t_hbm, i_hbm, o_hbm):
        def body(i_vmem, o_vmem):
            pltpu.sync_copy(t_hbm.at[i_vmem.at[0]], o_vmem)

        pltpu.emit_pipeline(
            body,
            grid=(n // SC_WINDOW,),
            in_specs=[pl.BlockSpec((1, SC_WINDOW), lambda i: (0, i))],
            out_specs=[pl.BlockSpec((SC_WINDOW, width), lambda i: (i, 0))],
            core_axis_name=("c", "s"),
            dimension_semantics=(pltpu.PARALLEL,),
        )(i_hbm, o_hbm)

    return gather_kernel(table, idx_row)


def _sc_scatter_rows(rows, idx_rows, n_out):
    n, width = rows.shape

    @functools.partial(pl.kernel, out_type=jax.ShapeDtypeStruct((n_out, width), rows.dtype), mesh=_sc_mesh(),
                       scratch_types=[])
    def scatter_kernel(r_hbm, *refs):
        i_hbms, o_hbm = refs[:-1], refs[-1]

        def body(r_vmem, *i_vmems):
            for i_vmem in i_vmems:
                pltpu.sync_copy(r_vmem, o_hbm.at[i_vmem.at[0]])

        pltpu.emit_pipeline(
            body,
            grid=(n // SC_WINDOW,),
            in_specs=[pl.BlockSpec((SC_WINDOW, width), lambda i: (i, 0))]
            + [pl.BlockSpec((1, SC_WINDOW), lambda i: (0, i)) for _ in i_hbms],
            out_specs=[],
            core_axis_name=("c", "s"),
            dimension_semantics=(pltpu.PARALLEL,),
        )(r_hbm, *i_hbms)

    return scatter_kernel(rows, *idx_rows)


def _combine_kernel(x1_ref, yg_ref, gate_ref, spread_ref, o_ref):
    q = D_MODEL // 4
    x1 = x1_ref[...]
    acc = [x1[:, :q], x1[:, q:]]
    gfull = lax.dot_general(gate_ref[...], spread_ref[...], (((0,), (0,)), ((), ())), preferred_element_type=F32)
    for k in range(TOP_K):
        parts = _unpack_bf16_pairs(yg_ref[k])
        g = jnp.tile(gfull[:, k * LANES:(k + 1) * LANES], (1, q // LANES))
        acc = [a + g * p for a, p in zip(acc, parts)]
    o_ref[:, :q] = acc[0]
    o_ref[:, q:] = acc[1]


def _combine_into_kernel(x1_ref, yg_ref, gate_ref, spread_ref, prev_ref, o_ref):
    del prev_ref
    _combine_kernel(x1_ref, yg_ref, gate_ref, spread_ref, o_ref)


def _block_diag_ones(width):
    r = jnp.arange(width) // HEAD_GROUP
    return (r[:, None] == r[None, :]).astype(BF16)


def _rope_tables(positions):
    half = ROT_DIM // 2
    inv_freq = ROPE_THETA ** (-jnp.arange(0, ROT_DIM, 2, dtype=F32) / ROT_DIM)
    ang = inv_freq[:, None] * positions.astype(F32).reshape(1, -1)
    cs = jnp.concatenate([jnp.cos(ang), jnp.sin(ang)], axis=0)
    cs_hi = cs.astype(BF16)
    cs_lo = (cs - cs_hi.astype(F32)).astype(BF16)
    lane = jnp.arange(LANES) % HEAD_GROUP
    j = jnp.arange(half)[:, None]
    lo_half = (lane[None, :] == j).astype(F32)
    hi_half = (lane[None, :] == j + half).astype(F32)
    spread = jnp.concatenate([
        jnp.concatenate([lo_half + hi_half, jnp.zeros((half, LANES), F32)], axis=1),
        jnp.concatenate([jnp.zeros((half, LANES), F32), hi_half - lo_half], axis=1)], axis=0)
    return jnp.concatenate([cs_hi, cs_lo], axis=0), jnp.concatenate([spread, spread], axis=0).astype(BF16)


def _full(shape):
    return pl.BlockSpec(shape, lambda *_: (0,) * len(shape))


def kernel(x, mem, positions, g_mix_norm, w_in, g_dq, g_dk, lambda_q1, lambda_k1, lambda_q2, lambda_k2, g_subln, g_sgu, w_spatial, b_spatial, g_mem_norm, w_mem_kv, g_cq, g_ck, w_out, g_ffn_norm, w_router, b_router, w_mlp1, b_mlp1, w_mlp2, b_mlp2):
    B, S, D = x.shape
    M = mem.shape[1]
    tm = TOKEN_TILE
    tr = ROUTER_TILE
    assert D == D_MODEL and S % tm == 0 and S % ATTN_TILE == 0 and g_mix_norm.shape[0] == 1
    n_parts = LAYER_PARTS if B % LAYER_PARTS == 0 and (B // LAYER_PARTS) * S % tr == 0 else 1
    Bp = B // n_parts
    N = Bp * S
    n_assign = N * TOP_K
    n_blocks = -(-n_assign // ROW_BLOCK) + N_EXPERTS
    n_rows = n_blocks * ROW_BLOCK
    nb_pad = -(-n_blocks // LANES) * LANES

    xf = x.reshape(B * S, D)
    rope_cs, rope_spread = _rope_tables(positions)
    ones256 = _block_diag_ones(MEM_W)
    row = lambda v: v.reshape(1, -1).astype(F32)
    tile_row = lambda v, reps: jnp.tile(v.reshape(1, -1).astype(F32), (1, reps))
    w_in_b, w_out_b, w_kv_b = w_in[0].astype(BF16), w_out[0].astype(BF16), w_mem_kv[0].astype(BF16)
    w_sp_lanes = jnp.transpose(w_spatial[0], (1, 0, 2)).reshape(CHUNK, GMLP_GROUPS * CHUNK)
    b_sp_lanes = jnp.repeat(b_spatial[0].T, HEAD_GROUP, axis=1)
    wr = w_router[0].T.astype(F32)
    wr_hi = wr.astype(BF16)
    wr_lo = (wr - wr_hi.astype(F32)).astype(BF16)
    before = (jnp.arange(tr)[:, None] < jnp.arange(tr)[None, :]).astype(BF16)
    gate_row = jnp.arange(GATE_ROWS)[:, None]
    gate_spread = ((gate_row < 2 * TOP_K) & (gate_row % TOP_K == jnp.arange(TOP_K * LANES)[None, :] // LANES)).astype(BF16)
    b1r = b_mlp1[0].reshape(N_EXPERTS, 1, 2 * D_FF)
    b2r = b_mlp2[0].reshape(N_EXPERTS, 1, D)

    out = None
    for part in range(n_parts):
        b_off = part * Bp
        t_off = part * (N // tm)
        r_off = part * (N // tr)

        kT, vm = pl.pallas_call(
            _mem_kv_kernel,
            grid=(Bp,),
            in_specs=[pl.BlockSpec((1, M, D), lambda b: (b + b_off, 0, 0)), _full((1, D)), _full((D, 2 * MEM_W)),
                      _full((1, MEM_W)), _full((MEM_W, MEM_W))],
            out_specs=[pl.BlockSpec((1, MEM_W, M), lambda b: (b, 0, 0)), pl.BlockSpec((1, M, MEM_W), lambda b: (b, 0, 0))],
            out_shape=[jax.ShapeDtypeStruct((Bp, MEM_W, M), BF16), jax.ShapeDtypeStruct((Bp, M, MEM_W), BF16)],
            compiler_params=_cparams(("parallel",)),
            name="mem_kv",
        )(mem, row(g_mem_norm[0]), w_kv_b, tile_row(g_ck[0], N_MEM_HEADS), ones256)

        tiles_per_batch = S // tm
        tok = lambda w: pl.BlockSpec((tm, w), lambda i: (i, 0))
        tok_in = lambda w: pl.BlockSpec((tm, w), lambda i: (i + t_off, 0))
        qn, kn, vv, gc = pl.pallas_call(
            _mixer_in_kernel,
            grid=(N // tm,),
            in_specs=[tok_in(D), pl.BlockSpec((4 * ROT_DIM // 2, tm), lambda i: (0, i + t_off)),
                      _full((4 * ROT_DIM // 2, 2 * LANES)),
                      _full((1, D)), _full((D, D_IN_PROJ)),
                      _full((1, DIFF_W)), _full((1, DIFF_W)),
                      _full((1, GMLP_W)), _full((CHUNK, GMLP_GROUPS * CHUNK)), _full((CHUNK, GMLP_W)),
                      _full((1, MEM_W)), _full((MEM_W, MEM_W)),
                      pl.BlockSpec((1, MEM_W, M), lambda i: (i // tiles_per_batch, 0, 0)),
                      pl.BlockSpec((1, M, MEM_W), lambda i: (i // tiles_per_batch, 0, 0))],
            out_specs=[tok(DIFF_W), tok(DIFF_W), tok(DIFF_W), tok(GMLP_W + MEM_W)],
            out_shape=[jax.ShapeDtypeStruct((N, DIFF_W), BF16)] * 3 + [jax.ShapeDtypeStruct((N, GMLP_W + MEM_W), BF16)],
            compiler_params=_cparams(("parallel",)),
            name="mixer_in",
        )(xf, rope_cs, rope_spread, row(g_mix_norm[0]), w_in_b,
          tile_row(g_dq[0], 2 * N_DIFF_HEADS), tile_row(g_dk[0], 2 * N_DIFF_HEADS),
          row(g_sgu[0]), w_sp_lanes, b_sp_lanes, tile_row(g_cq[0], N_MEM_HEADS), ones256, kT, vm)

        tq = ATTN_TILE
        nq = S // tq
        head_q = pl.BlockSpec((tq, DIFF_W), lambda b, i: (b * nq + i, 0))
        head_kv = pl.BlockSpec((S, DIFF_W), lambda b, i: (b, 0))
        lam_spec = pl.BlockSpec((1, DIFF_HEAD_DIM), lambda b, i: (0, 0))
        nh = N_DIFF_HEADS
        attn = pl.pallas_call(
            _diff_attn_kernel,
            grid=(Bp, nq),
            in_specs=[head_q, head_kv, head_kv, lam_spec, lam_spec, lam_spec, lam_spec,
                      pl.BlockSpec((DIFF_V_DIM, 1), lambda b, i: (0, 0))],
            out_specs=head_q,
            out_shape=jax.ShapeDtypeStruct((N, DIFF_W), BF16),
            scratch_shapes=[pltpu.VMEM((nh, tq, 2 * tq), F32), pltpu.VMEM((nh, tq, 2 * tq), F32),
                            pltpu.VMEM((nh, 1, 2 * tq), F32), pltpu.VMEM((nh, ATTN_EXT_ROWS, 2 * tq), F32),
                            pltpu.VMEM((nh, ATTN_EXT_ROWS, S), BF16)],
            compiler_params=_cparams(("parallel", "arbitrary")),
            name="diff_attn",
        )(qn, kn, vv, row(lambda_q1[0]), row(lambda_k1[0]), row(lambda_q2[0]), row(lambda_k2[0]),
          g_subln[0].reshape(DIFF_V_DIM, 1).astype(F32))

        rtok = lambda w: pl.BlockSpec((tr, w), lambda i: (i, 0))
        rtok_in = lambda w: pl.BlockSpec((tr, w), lambda i: (i + r_off, 0))
        tokT = lambda: pl.BlockSpec((TOP_K, tr), lambda i: (0, i))
        hw = D // 4
        x1, hpa, hpb, idxT, gateT, posT, counts = pl.pallas_call(
            _out_router_kernel,
            grid=(N // tr,),
            in_specs=[rtok_in(D), rtok(DIFF_W), rtok(GMLP_W + MEM_W), _full((D, D)), _full((1, D)),
                      _full((N_EXPERTS, D)), _full((N_EXPERTS, D)), _full((N_EXPERTS, 1)), _full((tr, tr))],
            out_specs=[rtok(D), rtok(hw), rtok(hw), tokT(), pl.BlockSpec((GATE_ROWS, tr), lambda i: (0, i)), tokT(),
                       _full((N_EXPERTS, 1))],
            out_shape=[jax.ShapeDtypeStruct((N, D), F32), jax.ShapeDtypeStruct((N, hw), U32),
                       jax.ShapeDtypeStruct((N, hw), U32), jax.ShapeDtypeStruct((TOP_K, N), I32),
                       jax.ShapeDtypeStruct((GATE_ROWS, N), BF16),
                       jax.ShapeDtypeStruct((TOP_K, N), I32), jax.ShapeDtypeStruct((N_EXPERTS, 1), F32)],
            scratch_shapes=[pltpu.VMEM((N_EXPERTS, 1), F32)],
            compiler_params=_cparams(("arbitrary",)),
            name="out_router",
        )(xf, attn, gc, w_out_b, row(g_ffn_norm[0]), wr_hi, wr_lo, b_router[0].reshape(N_EXPERTS, 1).astype(F32), before)

        destT, block_expert, nb_used = pl.pallas_call(
            _dest_kernel,
            grid_spec=pltpu.PrefetchScalarGridSpec(
                num_scalar_prefetch=1,
                grid=(1,),
                in_specs=[pl.BlockSpec((TOP_K, N), lambda i, c: (0, 0)), pl.BlockSpec((TOP_K, N), lambda i, c: (0, 0))],
                out_specs=[pl.BlockSpec((TOP_K, N), lambda i, c: (0, 0)), pl.BlockSpec((1, nb_pad), lambda i, c: (0, 0)),
                           pl.BlockSpec((1, LANES), lambda i, c: (0, 0))],
            ),
            out_shape=[jax.ShapeDtypeStruct((TOP_K, N), I32), jax.ShapeDtypeStruct((1, nb_pad), I32),
                       jax.ShapeDtypeStruct((1, LANES), I32)],
            compiler_params=_cparams(("arbitrary",)),
            name="dest",
        )(counts.reshape(N_EXPERTS).astype(I32), idxT, posT)

        dest_rows = [destT[k].reshape(1, N) for k in range(TOP_K)]
        xa_buf = _sc_scatter_rows(hpa, dest_rows, n_rows)
        xb_buf = _sc_scatter_rows(hpb, dest_rows, n_rows)

        cnt_i = counts.reshape(N_EXPERTS).astype(I32)
        owner = jnp.where(cnt_i > 0, jnp.arange(N_EXPERTS, dtype=I32), N_EXPERTS)
        later = jnp.concatenate([lax.cummin(owner[::-1])[::-1][1:], jnp.full((1,), N_EXPERTS, I32)])
        next_expert = jnp.where(later < N_EXPERTS, later, -1)
        last = lambda b, be, nbu, nxt: jnp.minimum(b, nbu[0] - 1)
        row_blk = lambda: pl.BlockSpec((ROW_BLOCK, hw), lambda b, be, nbu, nxt: (last(b, be, nbu, nxt), 0))
        ya_buf, yb_buf = pl.pallas_call(
            _expert_ffn_kernel,
            grid_spec=pltpu.PrefetchScalarGridSpec(
                num_scalar_prefetch=3,
                grid=(n_blocks,),
                in_specs=[row_blk(), row_blk(),
                          pl.BlockSpec(memory_space=pl.ANY),
                          pl.BlockSpec((1, 1, 2 * D_FF), lambda b, be, nbu, nxt: (be[b], 0, 0)),
                          pl.BlockSpec(memory_space=pl.ANY),
                          pl.BlockSpec((1, 1, D), lambda b, be, nbu, nxt: (be[b], 0, 0))],
                out_specs=[row_blk(), row_blk()],
                scratch_shapes=[pltpu.VMEM((2, D, 2 * D_FF), F32), pltpu.VMEM((2, D_FF, D), F32),
                                pltpu.VMEM((D, 2 * D_FF), BF16), pltpu.VMEM((D_FF, D), BF16),
                                pltpu.SMEM((1,), I32), pltpu.SemaphoreType.DMA((2, 2))],
            ),
            out_shape=[jax.ShapeDtypeStruct((n_rows, hw), U32)] * 2,
            compiler_params=_cparams(("arbitrary",)),
            name="expert_ffn",
        )(block_expert[0, :n_blocks], nb_used[0, :1], next_expert, xa_buf, xb_buf, w_mlp1[0], b1r, w_mlp2[0], b2r)

        dest_flat = destT.reshape(1, n_assign)
        for half, y_buf in enumerate((ya_buf, yb_buf)):
            yg = _sc_gather_rows(y_buf, dest_flat).reshape(TOP_K, N, hw)
            prev = () if out is None else (out,)
            out = pl.pallas_call(
                _combine_kernel if out is None else _combine_into_kernel,
                grid=(N // tm,),
                in_specs=[pl.BlockSpec((tm, D // 2), lambda i: (i, half)),
                          pl.BlockSpec((TOP_K, tm, hw), lambda i: (0, i, 0)),
                          pl.BlockSpec((GATE_ROWS, tm), lambda i: (0, i)),
                          _full((GATE_ROWS, TOP_K * LANES))] + [pl.BlockSpec(memory_space=pl.ANY)] * len(prev),
                out_specs=pl.BlockSpec((tm, D // 2), lambda i: (i + t_off, half)),
                out_shape=jax.ShapeDtypeStruct((B * S, D), F32),
                input_output_aliases={4: 0} if prev else {},
                compiler_params=_cparams(("parallel",)),
                name="combine",
            )(x1, yg, gateT, gate_spread, *prev)
    return out.reshape(B, S, D)
```

```python
import functools

import jax
import jax.numpy as jnp
from jax import lax
from jax.experimental import pallas as pl
from jax.experimental.pallas import tpu as pltpu
from jax.experimental.pallas import tpu_sc as plsc

F32 = jnp.float32
BF16 = jnp.bfloat16
I32 = jnp.int32
U32 = jnp.uint32

D_MODEL = 1024
N_DIFF_HEADS = 4
DIFF_HEAD_DIM = 64
DIFF_V_DIM = 128
DIFF_W = 512
GMLP_W = 256
GMLP_GROUPS = 4
CHUNK = 128
MEM_W = 256
N_MEM_HEADS = 4
HEAD_GROUP = 64
D_IN_PROJ = 2304
ROPE_THETA = 500000.0
ROT_DIM = 16
N_EXPERTS = 32
TOP_K = 4
D_FF = 1024
SWIGLU_LIMIT = 7.0
SWIGLU_ALPHA = 1.702
EPS = 1e-6
LAMBDA_INIT = 0.8 - 0.6

LANES = 128
ROW_BLOCK = 1024
FFN_SUB_ROWS = 256
TOKEN_TILE = 512
LAYER_PARTS = 1
ROUTER_TILE = 1024
GATE_ROWS = 16
ATTN_TILE = 256
ATTN_EXT_ROWS = 128 + 16
VMEM_LIMIT = 56 * 1024 * 1024
NEG_BIG = -1e30
LOG2_E = 1.4426950408889634


def _cparams(sem):
    return pltpu.CompilerParams(dimension_semantics=sem, vmem_limit_bytes=VMEM_LIMIT)


def _dot(a, b):
    return jnp.dot(a, b, preferred_element_type=F32)


def _dot_nt(a, b):
    return lax.dot_general(a, b, (((1,), (1,)), ((), ())), preferred_element_type=F32)


def _rms(x, gain):
    ms = jnp.mean(x * x, axis=-1, keepdims=True)
    return x * lax.rsqrt(ms + EPS) * gain


def _group_rms(t, ones_bd, gain):
    w = ones_bd.shape[0]
    chunks = []
    for j in range(t.shape[1] // w):
        c = t[:, j * w:(j + 1) * w]
        ss = _dot((c * c).astype(BF16), ones_bd)
        chunks.append(c * lax.rsqrt(ss * (1.0 / HEAD_GROUP) + EPS))
    return (chunks[0] if len(chunks) == 1 else jnp.concatenate(chunks, axis=1)) * gain


def _pack_bf16_pairs(v):
    w = v.shape[1] // 2
    bits = lax.bitcast_convert_type(v.astype(BF16).astype(F32), U32)
    return (bits[:, :w] & jnp.uint32(0xFFFF0000)) | (bits[:, w:] >> jnp.uint32(16))


def _unpack_bf16_pairs(words):
    hi = lax.bitcast_convert_type(words & jnp.uint32(0xFFFF0000), F32)
    lo = lax.bitcast_convert_type(words << jnp.uint32(16), F32)
    return hi, lo


def _mem_kv_kernel(mem_ref, gmem_ref, wkv_ref, gck_ref, ones_ref, kT_ref, v_ref):
    m = _rms(mem_ref[0], gmem_ref[...]).astype(BF16)
    kv = _dot(m, wkv_ref[...])
    k = _group_rms(kv[:, :MEM_W], ones_ref[...], gck_ref[...])
    kT_ref[0] = k.T.astype(BF16)
    v_ref[0] = kv[:, MEM_W:].astype(BF16)


def _gelu_tanh(x):
    return 0.5 * x * (1.0 + jnp.tanh(0.7978845608028654 * (x + 0.044715 * (x * x * x))))


def _mixer_in_kernel(x_ref, cs_ref, spread_ref, gmix_ref, win_ref, gq_ref, gk_ref,
                     gsgu_ref, wsp_ref, bsp_ref, gcq_ref, ones256_ref, kT_ref, vm_ref,
                     q_out, k_out, v_out, gc_out):
    tm = x_ref.shape[0]
    hb = _rms(x_ref[...], gmix_ref[...]).astype(BF16)

    def proj(lo, hi):
        return _dot(hb, win_ref[:, lo:hi])

    lane = lax.broadcasted_iota(I32, (tm, LANES), 1)
    first_half = (lane % HEAD_GROUP) < (ROT_DIM // 2)
    tab = lax.dot_general(cs_ref[...], spread_ref[...], (((0,), (0,)), ((), ())), preferred_element_type=F32)
    cosb = tab[:, :LANES] + jnp.where((lane % HEAD_GROUP) >= ROT_DIM, 1.0, 0.0)
    sinb = tab[:, LANES:]

    def norm_rope(t, gain, out_ref):
        tn = _group_rms(t, ones256_ref[...], gain)
        for j in range(DIFF_W // LANES):
            c = tn[:, j * LANES:(j + 1) * LANES]
            partner = jnp.where(first_half, pltpu.roll(c, LANES - ROT_DIM // 2, 1), pltpu.roll(c, ROT_DIM // 2, 1))
            out_ref[:, j * LANES:(j + 1) * LANES] = (c * cosb + partner * sinb).astype(BF16)

    norm_rope(proj(0, DIFF_W), gq_ref[...] * (DIFF_HEAD_DIM ** -0.5 * LOG2_E), q_out)
    norm_rope(proj(DIFF_W, 2 * DIFF_W), gk_ref[...], k_out)
    v_out[...] = proj(2 * DIFF_W, 3 * DIFF_W).astype(BF16)

    z = _gelu_tanh(proj(3 * DIFF_W, 3 * DIFF_W + 2 * GMLP_W))
    u = z[:, :GMLP_W]
    vg = z[:, GMLP_W:]
    vc = vg - jnp.mean(vg, axis=-1, keepdims=True)
    vgn = (vc * lax.rsqrt(jnp.mean(vc * vc, axis=-1, keepdims=True) + EPS) * gsgu_ref[...]).astype(BF16)
    wrow = lax.broadcasted_iota(I32, (CHUNK, GMLP_GROUPS * CHUNK), 0)
    wcol = lax.broadcasted_iota(I32, (CHUNK, GMLP_GROUPS * CHUNK), 1) % CHUNK
    w_causal = jnp.where(wcol <= wrow, wsp_ref[...], 0.0).astype(BF16)
    grp = lax.broadcasted_iota(I32, (CHUNK, GMLP_W), 1) // HEAD_GROUP
    zero_b = jnp.zeros((CHUNK, GMLP_W), BF16)
    for r in range(tm // CHUNK):
        vchunk = vgn[r * CHUNK:(r + 1) * CHUNK, :]
        v_bd = jnp.concatenate([jnp.where(grp == g, vchunk, zero_b) for g in range(GMLP_GROUPS)], axis=0)
        mixed = _dot(w_causal, v_bd) + bsp_ref[...]
        gc_out[r * CHUNK:(r + 1) * CHUNK, 0:GMLP_W] = (u[r * CHUNK:(r + 1) * CHUNK, :] * mixed).astype(BF16)

    pc = proj(3 * DIFF_W + 2 * GMLP_W, D_IN_PROJ)
    qc = _group_rms(pc, ones256_ref[...], gcq_ref[...] * (HEAD_GROUP ** -0.5)).astype(BF16)
    hgrp = lax.broadcasted_iota(I32, (tm, MEM_W), 1) // HEAD_GROUP
    zero_q = jnp.zeros((tm, MEM_W), BF16)
    q_st = jnp.concatenate([jnp.where(hgrp == h, qc, zero_q) for h in range(N_MEM_HEADS)], axis=0)
    s = _dot(q_st, kT_ref[0])
    p = jnp.exp(s - jnp.max(s, axis=-1, keepdims=True))
    o = _dot(p.astype(BF16), vm_ref[0]) / jnp.sum(p, axis=-1, keepdims=True)
    c = jnp.zeros((tm, MEM_W), F32)
    for h in range(N_MEM_HEADS):
        c = c + jnp.where(hgrp == h, o[h * tm:(h + 1) * tm, :], 0.0)
    gc_out[:, GMLP_W:GMLP_W + MEM_W] = c.astype(BF16)


def _diff_attn_kernel(q_ref, k_ref, v_ref, lq1_ref, lk1_ref, lq2_ref, lk2_ref, gsub_ref, o_ref,
                      s0_ref, s1_ref, m_ref, acc_ref, vT_ref):
    tq = q_ref.shape[0]
    seq = k_ref.shape[0]
    i = pl.program_id(1)
    lane = lax.broadcasted_iota(I32, (tq, LANES), 1)
    heads = range(N_DIFF_HEADS)
    hl = lambda h: slice(h * DIFF_V_DIM, (h + 1) * DIFF_V_DIM)
    ext_rows = vT_ref.shape[1]

    @pl.when(i == 0)
    def _():
        ones_row = jnp.where(lax.broadcasted_iota(I32, (ext_rows - DIFF_V_DIM, seq), 0) == 0, 1.0, 0.0).astype(BF16)
        for h in heads:
            for c in range(seq // tq):
                vT_ref[h, 0:DIFF_V_DIM, c * tq:(c + 1) * tq] = v_ref[c * tq:(c + 1) * tq, hl(h)].T
            vT_ref[h, DIFF_V_DIM:ext_rows, :] = ones_row

    def stacked_qT(h):
        q = q_ref[:, hl(h)]
        zero = jnp.zeros_like(q)
        return jnp.concatenate([jnp.where(lane < DIFF_HEAD_DIM, q, zero), jnp.where(lane >= DIFF_HEAD_DIM, q, zero)],
                               axis=0).T

    qsT = [stacked_qT(h) for h in heads]

    def scores(t, s_ref):
        rows = pl.ds(pl.multiple_of(t * tq, tq), tq)
        for h in heads:
            s_ref[h] = _dot(k_ref[rows, hl(h)], qsT[h])

    def update(t, s_ref, causal):
        cols = pl.ds(pl.multiple_of(t * tq, tq), tq)
        for h in heads:
            s = s_ref[h]
            if causal:
                key = lax.broadcasted_iota(I32, (tq, 2 * tq), 0)
                qry = lax.broadcasted_iota(I32, (tq, 2 * tq), 1) % tq
                s = jnp.where(key <= qry, s, NEG_BIG)
            m = m_ref[h]
            m_new = jnp.maximum(m, jnp.max(s, axis=0, keepdims=True))
            alpha = jnp.exp2(m - m_new)
            p = jnp.exp2(s - m_new)
            m_ref[h] = m_new
            acc_ref[h] = alpha * acc_ref[h] + _dot(vT_ref[h, :, cols], p.astype(BF16))

    m_ref[...] = jnp.full(m_ref.shape, NEG_BIG, F32)
    acc_ref[...] = jnp.zeros(acc_ref.shape, F32)
    scores(0, s0_ref)

    def pair(pidx, carry):
        t = 2 * pidx
        scores(t + 1, s1_ref)
        update(t, s0_ref, False)
        scores(t + 2, s0_ref)
        update(t + 1, s1_ref, False)
        return carry

    lax.fori_loop(0, i // 2, pair, 0)

    @pl.when(i % 2 == 0)
    def _():
        update(i, s0_ref, True)

    @pl.when(i % 2 == 1)
    def _():
        scores(i, s1_ref)
        update(i - 1, s0_ref, False)
        update(i, s1_ref, True)

    lam = (jnp.exp(jnp.sum(lq1_ref[...] * lk1_ref[...], axis=-1, keepdims=True))
           - jnp.exp(jnp.sum(lq2_ref[...] * lk2_ref[...], axis=-1, keepdims=True)) + LAMBDA_INIT)
    for h in heads:
        on = acc_ref[h, 0:DIFF_V_DIM, :] * (1.0 / acc_ref[h, DIFF_V_DIM:DIFF_V_DIM + 1, :])
        o = on[:, :tq] - lam * on[:, tq:]
        ms = jnp.mean(o * o, axis=0, keepdims=True)
        o = o * lax.rsqrt(ms + EPS) * gsub_ref[...] * (1.0 - LAMBDA_INIT)
        o_ref[:, hl(h)] = o.T.astype(BF16)


def _out_router_kernel(x_ref, a_ref, gc_ref, wo_ref, gffn_ref, wrh_ref, wrl_ref, br_ref, before_ref,
                       x1_out, hpa_out, hpb_out, idx_out, gate_out, pos_out, cnt_out, carry_ref):
    tm = x_ref.shape[0]

    @pl.when(pl.program_id(0) == 0)
    def _():
        carry_ref[...] = jnp.zeros_like(carry_ref)

    sub = before_ref.shape[0]
    wr_stack = jnp.concatenate([wrh_ref[...], wrl_ref[...]], axis=0)
    eio = lax.broadcasted_iota(I32, (N_EXPERTS, sub), 0)
    before_b = before_ref[...]
    carry = carry_ref[...]
    for r in range(tm // sub):
        rows = slice(r * sub, (r + 1) * sub)
        mix = jnp.concatenate([a_ref[rows, :], gc_ref[rows, :]], axis=1)
        x1 = x_ref[rows, :] + _dot(mix, wo_ref[...])
        x1_out[rows, :] = x1
        h2 = _rms(x1, gffn_ref[...])
        hb = h2.astype(BF16)
        hpa_out[rows, :] = _pack_bf16_pairs(h2[:, :D_MODEL // 2])
        hpb_out[rows, :] = _pack_bf16_pairs(h2[:, D_MODEL // 2:])

        h_lo = (h2 - hb.astype(F32)).astype(BF16)
        both = _dot_nt(wr_stack, hb)
        logits = (both[:N_EXPERTS] + both[N_EXPERTS:]) + _dot_nt(wrh_ref[...], h_lo) + br_ref[...]

        vals, idxs, sels = [], [], []
        cur = logits
        for _ in range(TOP_K):
            m = jnp.max(cur, axis=0, keepdims=True)
            ik = jnp.min(jnp.where(cur == m, eio, N_EXPERTS), axis=0, keepdims=True)
            sel = eio == ik
            cur = jnp.where(sel, -jnp.inf, cur)
            vals.append(m)
            idxs.append(ik)
            sels.append(sel)
        es = [jnp.exp(v - vals[0]) for v in vals]
        tot = es[0] + es[1] + es[2] + es[3]
        gates = jnp.concatenate([e / tot for e in es], axis=0)
        g_hi = gates.astype(BF16)
        g_lo = (gates - g_hi.astype(F32)).astype(BF16)
        gate_out[:, rows] = jnp.concatenate(
            [g_hi, g_lo, jnp.zeros((GATE_ROWS - 2 * TOP_K, sub), BF16)], axis=0)
        idx_out[:, rows] = jnp.concatenate(idxs, axis=0)

        cnt = jnp.zeros((N_EXPERTS, sub), F32)
        for sel in sels:
            cnt = cnt + jnp.where(sel, 1.0, 0.0)
        base = carry + _dot(cnt.astype(BF16), before_b)
        pos_out[:, rows] = jnp.concatenate(
            [jnp.sum(jnp.where(sel, base, 0.0), axis=0, keepdims=True) for sel in sels], axis=0).astype(I32)
        carry = carry + jnp.sum(cnt, axis=1, keepdims=True)
    carry_ref[...] = carry
    cnt_out[...] = carry


def _dest_kernel(cnt_ref, idx_ref, pos_ref, dest_out, be_out, valid_out, nbu_out):
    idx = idx_ref[...]
    dest = pos_ref[...]
    bidx = lax.broadcasted_iota(I32, be_out.shape, 1)
    be = jnp.zeros(be_out.shape, I32)
    valid = jnp.zeros(be_out.shape, I32)
    run = jnp.int32(0)
    for e in range(N_EXPERTS):
        dest = dest + jnp.where(idx == e, run, 0)
        first_block = run // ROW_BLOCK
        run = run + ((cnt_ref[e] + (ROW_BLOCK - 1)) // ROW_BLOCK) * ROW_BLOCK
        mine = jnp.logical_and(bidx >= first_block, bidx < run // ROW_BLOCK)
        valid = jnp.where(mine, jnp.clip(cnt_ref[e] - (bidx - first_block) * ROW_BLOCK, 0, ROW_BLOCK), valid)
        be = be + jnp.where(bidx >= run // ROW_BLOCK, 1, 0)
    dest_out[...] = dest
    be_out[...] = jnp.minimum(be, N_EXPERTS - 1)
    valid_out[...] = valid
    nbu_out[...] = jnp.zeros(nbu_out.shape, I32) + run // ROW_BLOCK


def _expert_ffn_kernel(be_ref, nbu_ref, nxt_ref, valid_ref, xa_ref, xb_ref, w1_hbm, b1_ref, w2_hbm, b2_ref,
                       ya_ref, yb_ref, w1f_ref, w2f_ref, w1s_ref, w2s_ref, sem_ref):
    b = pl.program_id(0)

    def weight_copies(e):
        return (pltpu.make_async_copy(w1_hbm.at[e], w1f_ref, sem_ref.at[0]),
                pltpu.make_async_copy(w2_hbm.at[e], w2f_ref, sem_ref.at[1]))

    def mlp(rows):
        parts = _unpack_bf16_pairs(xa_ref[0:rows, :]) + _unpack_bf16_pairs(xb_ref[0:rows, :])
        xrow = jnp.concatenate([p.astype(BF16) for p in parts], axis=1)
        hm = _dot(xrow, w1s_ref[...]) + b1_ref[0]
        glu = jnp.minimum(hm[:, :D_FF], SWIGLU_LIMIT)
        lin = jnp.clip(hm[:, D_FF:], -SWIGLU_LIMIT, SWIGLU_LIMIT)
        act = glu * jax.nn.sigmoid(SWIGLU_ALPHA * glu) * (lin + 1.0)
        y = _dot(act.astype(BF16), w2s_ref[...]) + b2_ref[0]
        ya_ref[0:rows, :] = _pack_bf16_pairs(y[:, :D_MODEL // 2])
        yb_ref[0:rows, :] = _pack_bf16_pairs(y[:, D_MODEL // 2:])

    @pl.when(b < nbu_ref[0])
    def _():
        e = be_ref[b]

        @pl.when(b == 0)
        def _():
            for cp in weight_copies(e):
                cp.start()

        @pl.when(jnp.logical_or(b == 0, e != be_ref[jnp.maximum(b - 1, 0)]))
        def _():
            for cp in weight_copies(e):
                cp.wait()
            w1s_ref[...] = w1f_ref[...].astype(BF16)
            w2s_ref[...] = w2f_ref[...].astype(BF16)
            nxt = nxt_ref[e]

            @pl.when(nxt >= 0)
            def _():
                for cp in weight_copies(nxt):
                    cp.start()

        n_sub = (valid_ref[b] + (FFN_SUB_ROWS - 1)) // FFN_SUB_ROWS
        for k in range(1, ROW_BLOCK // FFN_SUB_ROWS + 1):
            pl.when(n_sub == k)(functools.partial(mlp, k * FFN_SUB_ROWS))


SC_WINDOW = 128


def _sc_mesh():
    return plsc.VectorSubcoreMesh(core_axis_name="c", subcore_axis_name="s")


def _sc_gather_rows(table, idx_row):
    n = idx_row.shape[1]
    width = table.shape[1]

    @functools.partial(pl.kernel, out_type=jax.ShapeDtypeStruct((n, width), table.dtype), mesh=_sc_mesh(),
                       scratch_types=[])
    def gather_kernel(t_hbm, i_hbm, o_hbm):
        def body(i_vmem, o_vmem):
            pltpu.sync_copy(t_hbm.at[i_vmem.at[0]], o_vmem)

        pltpu.emit_pipeline(
            body,
            grid=(n // SC_WINDOW,),
            in_specs=[pl.BlockSpec((1, SC_WINDOW), lambda i: (0, i))],
            out_specs=[pl.BlockSpec((SC_WINDOW, width), lambda i: (i, 0))],
            core_axis_name=("c", "s"),
            dimension_semantics=(pltpu.PARALLEL,),
        )(i_hbm, o_hbm)

    return gather_kernel(table, idx_row)


def _sc_scatter_rows(rows, idx_rows, n_out):
    n, width = rows.shape

    @functools.partial(pl.kernel, out_type=jax.ShapeDtypeStruct((n_out, width), rows.dtype), mesh=_sc_mesh(),
                       scratch_types=[])
    def scatter_kernel(r_hbm, *refs):
        i_hbms, o_hbm = refs[:-1], refs[-1]

        def body(r_vmem, *i_vmems):
            for i_vmem in i_vmems:
                pltpu.sync_copy(r_vmem, o_hbm.at[i_vmem.at[0]])

        pltpu.emit_pipeline(
            body,
            grid=(n // SC_WINDOW,),
            in_specs=[pl.BlockSpec((SC_WINDOW, width), lambda i: (i, 0))]
            + [pl.BlockSpec((1, SC_WINDOW), lambda i: (0, i)) for _ in i_hbms],
            out_specs=[],
            core_axis_name=("c", "s"),
            dimension_semantics=(pltpu.PARALLEL,),
        )(r_hbm, *i_hbms)

    return scatter_kernel(rows, *idx_rows)


def _combine_kernel(x1_ref, yg_ref, gate_ref, spread_ref, o_ref):
    q = D_MODEL // 4
    x1 = x1_ref[...]
    acc = [x1[:, :q], x1[:, q:]]
    gfull = lax.dot_general(gate_ref[...], spread_ref[...], (((0,), (0,)), ((), ())), preferred_element_type=F32)
    for k in range(TOP_K):
        parts = _unpack_bf16_pairs(yg_ref[k])
        g = jnp.tile(gfull[:, k * LANES:(k + 1) * LANES], (1, q // LANES))
        acc = [a + g * p for a, p in zip(acc, parts)]
    o_ref[:, :q] = acc[0]
    o_ref[:, q:] = acc[1]


def _combine_into_kernel(x1_ref, yg_ref, gate_ref, spread_ref, prev_ref, o_ref):
    del prev_ref
    _combine_kernel(x1_ref, yg_ref, gate_ref, spread_ref, o_ref)


def _block_diag_ones(width):
    r = jnp.arange(width) // HEAD_GROUP
    return (r[:, None] == r[None, :]).astype(BF16)


def _rope_tables(positions):
    half = ROT_DIM // 2
    inv_freq = ROPE_THETA ** (-jnp.arange(0, ROT_DIM, 2, dtype=F32) / ROT_DIM)
    ang = inv_freq[:, None] * positions.astype(F32).reshape(1, -1)
    cs = jnp.concatenate([jnp.cos(ang), jnp.sin(ang)], axis=0)
    cs_hi = cs.astype(BF16)
    cs_lo = (cs - cs_hi.astype(F32)).astype(BF16)
    lane = jnp.arange(LANES) % HEAD_GROUP
    j = jnp.arange(half)[:, None]
    lo_half = (lane[None, :] == j).astype(F32)
    hi_half = (lane[None, :] == j + half).astype(F32)
    spread = jnp.concatenate([
        jnp.concatenate([lo_half + hi_half, jnp.zeros((half, LANES), F32)], axis=1),
        jnp.concatenate([jnp.zeros((half, LANES), F32), hi_half - lo_half], axis=1)], axis=0)
    return jnp.concatenate([cs_hi, cs_lo], axis=0), jnp.concatenate([spread, spread], axis=0).astype(BF16)


def _full(shape):
    return pl.BlockSpec(shape, lambda *_: (0,) * len(shape))


def kernel(x, mem, positions, g_mix_norm, w_in, g_dq, g_dk, lambda_q1, lambda_k1, lambda_q2, lambda_k2, g_subln, g_sgu, w_spatial, b_spatial, g_mem_norm, w_mem_kv, g_cq, g_ck, w_out, g_ffn_norm, w_router, b_router, w_mlp1, b_mlp1, w_mlp2, b_mlp2):
    B, S, D = x.shape
    M = mem.shape[1]
    tm = TOKEN_TILE
    tr = ROUTER_TILE
    assert D == D_MODEL and S % tm == 0 and S % ATTN_TILE == 0 and g_mix_norm.shape[0] == 1
    n_parts = LAYER_PARTS if B % LAYER_PARTS == 0 and (B // LAYER_PARTS) * S % tr == 0 else 1
    Bp = B // n_parts
    N = Bp * S
    n_assign = N * TOP_K
    n_blocks = -(-n_assign // ROW_BLOCK) + N_EXPERTS
    n_rows = n_blocks * ROW_BLOCK
    nb_pad = -(-n_blocks // LANES) * LANES

    xf = x.reshape(B * S, D)
    rope_cs, rope_spread = _rope_tables(positions)
    ones256 = _block_diag_ones(MEM_W)
    row = lambda v: v.reshape(1, -1).astype(F32)
    tile_row = lambda v, reps: jnp.tile(v.reshape(1, -1).astype(F32), (1, reps))
    w_in_b, w_out_b, w_kv_b = w_in[0].astype(BF16), w_out[0].astype(BF16), w_mem_kv[0].astype(BF16)
    w_sp_lanes = jnp.transpose(w_spatial[0], (1, 0, 2)).reshape(CHUNK, GMLP_GROUPS * CHUNK)
    b_sp_lanes = jnp.repeat(b_spatial[0].T, HEAD_GROUP, axis=1)
    wr = w_router[0].T.astype(F32)
    wr_hi = wr.astype(BF16)
    wr_lo = (wr - wr_hi.astype(F32)).astype(BF16)
    before = (jnp.arange(tr)[:, None] < jnp.arange(tr)[None, :]).astype(BF16)
    gate_row = jnp.arange(GATE_ROWS)[:, None]
    gate_spread = ((gate_row < 2 * TOP_K) & (gate_row % TOP_K == jnp.arange(TOP_K * LANES)[None, :] // LANES)).astype(BF16)
    b1r = b_mlp1[0].reshape(N_EXPERTS, 1, 2 * D_FF)
    b2r = b_mlp2[0].reshape(N_EXPERTS, 1, D)

    out = None
    for part in range(n_parts):
        b_off = part * Bp
        t_off = part * (N // tm)
        r_off = part * (N // tr)

        kT, vm = pl.pallas_call(
            _mem_kv_kernel,
            grid=(Bp,),
            in_specs=[pl.BlockSpec((1, M, D), lambda b: (b + b_off, 0, 0)), _full((1, D)), _full((D, 2 * MEM_W)),
                      _full((1, MEM_W)), _full((MEM_W, MEM_W))],
            out_specs=[pl.BlockSpec((1, MEM_W, M), lambda b: (b, 0, 0)), pl.BlockSpec((1, M, MEM_W), lambda b: (b, 0, 0))],
            out_shape=[jax.ShapeDtypeStruct((Bp, MEM_W, M), BF16), jax.ShapeDtypeStruct((Bp, M, MEM_W), BF16)],
            compiler_params=_cparams(("parallel",)),
            name="mem_kv",
        )(mem, row(g_mem_norm[0]), w_kv_b, tile_row(g_ck[0], N_MEM_HEADS), ones256)

        tiles_per_batch = S // tm
        tok = lambda w: pl.BlockSpec((tm, w), lambda i: (i, 0))
        tok_in = lambda w: pl.BlockSpec((tm, w), lambda i: (i + t_off, 0))
        qn, kn, vv, gc = pl.pallas_call(
            _mixer_in_kernel,
            grid=(N // tm,),
            in_specs=[tok_in(D), pl.BlockSpec((4 * ROT_DIM // 2, tm), lambda i: (0, i + t_off)),
                      _full((4 * ROT_DIM // 2, 2 * LANES)),
                      _full((1, D)), _full((D, D_IN_PROJ)),
                      _full((1, DIFF_W)), _full((1, DIFF_W)),
                      _full((1, GMLP_W)), _full((CHUNK, GMLP_GROUPS * CHUNK)), _full((CHUNK, GMLP_W)),
                      _full((1, MEM_W)), _full((MEM_W, MEM_W)),
                      pl.BlockSpec((1, MEM_W, M), lambda i: (i // tiles_per_batch, 0, 0)),
                      pl.BlockSpec((1, M, MEM_W), lambda i: (i // tiles_per_batch, 0, 0))],
            out_specs=[tok(DIFF_W), tok(DIFF_W), tok(DIFF_W), tok(GMLP_W + MEM_W)],
            out_shape=[jax.ShapeDtypeStruct((N, DIFF_W), BF16)] * 3 + [jax.ShapeDtypeStruct((N, GMLP_W + MEM_W), BF16)],
            compiler_params=_cparams(("parallel",)),
            name="mixer_in",
        )(xf, rope_cs, rope_spread, row(g_mix_norm[0]), w_in_b,
          tile_row(g_dq[0], 2 * N_DIFF_HEADS), tile_row(g_dk[0], 2 * N_DIFF_HEADS),
          row(g_sgu[0]), w_sp_lanes, b_sp_lanes, tile_row(g_cq[0], N_MEM_HEADS), ones256, kT, vm)

        tq = ATTN_TILE
        nq = S // tq
        head_q = pl.BlockSpec((tq, DIFF_W), lambda b, i: (b * nq + i, 0))
        head_kv = pl.BlockSpec((S, DIFF_W), lambda b, i: (b, 0))
        lam_spec = pl.BlockSpec((1, DIFF_HEAD_DIM), lambda b, i: (0, 0))
        nh = N_DIFF_HEADS
        attn = pl.pallas_call(
            _diff_attn_kernel,
            grid=(Bp, nq),
            in_specs=[head_q, head_kv, head_kv, lam_spec, lam_spec, lam_spec, lam_spec,
                      pl.BlockSpec((DIFF_V_DIM, 1), lambda b, i: (0, 0))],
            out_specs=head_q,
            out_shape=jax.ShapeDtypeStruct((N, DIFF_W), BF16),
            scratch_shapes=[pltpu.VMEM((nh, tq, 2 * tq), F32), pltpu.VMEM((nh, tq, 2 * tq), F32),
                            pltpu.VMEM((nh, 1, 2 * tq), F32), pltpu.VMEM((nh, ATTN_EXT_ROWS, 2 * tq), F32),
                            pltpu.VMEM((nh, ATTN_EXT_ROWS, S), BF16)],
            compiler_params=_cparams(("parallel", "arbitrary")),
            name="diff_attn",
        )(qn, kn, vv, row(lambda_q1[0]), row(lambda_k1[0]), row(lambda_q2[0]), row(lambda_k2[0]),
          g_subln[0].reshape(DIFF_V_DIM, 1).astype(F32))

        rtok = lambda w: pl.BlockSpec((tr, w), lambda i: (i, 0))
        rtok_in = lambda w: pl.BlockSpec((tr, w), lambda i: (i + r_off, 0))
        tokT = lambda: pl.BlockSpec((TOP_K, tr), lambda i: (0, i))
        hw = D // 4
        x1, hpa, hpb, idxT, gateT, posT, counts = pl.pallas_call(
            _out_router_kernel,
            grid=(N // tr,),
            in_specs=[rtok_in(D), rtok(DIFF_W), rtok(GMLP_W + MEM_W), _full((D, D)), _full((1, D)),
                      _full((N_EXPERTS, D)), _full((N_EXPERTS, D)), _full((N_EXPERTS, 1)), _full((tr, tr))],
            out_specs=[rtok(D), rtok(hw), rtok(hw), tokT(), pl.BlockSpec((GATE_ROWS, tr), lambda i: (0, i)), tokT(),
                       _full((N_EXPERTS, 1))],
            out_shape=[jax.ShapeDtypeStruct((N, D), F32), jax.ShapeDtypeStruct((N, hw), U32),
                       jax.ShapeDtypeStruct((N, hw), U32), jax.ShapeDtypeStruct((TOP_K, N), I32),
                       jax.ShapeDtypeStruct((GATE_ROWS, N), BF16),
                       jax.ShapeDtypeStruct((TOP_K, N), I32), jax.ShapeDtypeStruct((N_EXPERTS, 1), F32)],
            scratch_shapes=[pltpu.VMEM((N_EXPERTS, 1), F32)],
            compiler_params=_cparams(("arbitrary",)),
            name="out_router",
        )(xf, attn, gc, w_out_b, row(g_ffn_norm[0]), wr_hi, wr_lo, b_router[0].reshape(N_EXPERTS, 1).astype(F32), before)

        destT, block_expert, block_valid, nb_used = pl.pallas_call(
            _dest_kernel,
            grid_spec=pltpu.PrefetchScalarGridSpec(
                num_scalar_prefetch=1,
                grid=(1,),
                in_specs=[pl.BlockSpec((TOP_K, N), lambda i, c: (0, 0)), pl.BlockSpec((TOP_K, N), lambda i, c: (0, 0))],
                out_specs=[pl.BlockSpec((TOP_K, N), lambda i, c: (0, 0)), pl.BlockSpec((1, nb_pad), lambda i, c: (0, 0)),
                           pl.BlockSpec((1, nb_pad), lambda i, c: (0, 0)), pl.BlockSpec((1, LANES), lambda i, c: (0, 0))],
            ),
            out_shape=[jax.ShapeDtypeStruct((TOP_K, N), I32), jax.ShapeDtypeStruct((1, nb_pad), I32),
                       jax.ShapeDtypeStruct((1, nb_pad), I32), jax.ShapeDtypeStruct((1, LANES), I32)],
            compiler_params=_cparams(("arbitrary",)),
            name="dest",
        )(counts.reshape(N_EXPERTS).astype(I32), idxT, posT)

        dest_rows = [destT[k].reshape(1, N) for k in range(TOP_K)]
        xa_buf = _sc_scatter_rows(hpa, dest_rows, n_rows)
        xb_buf = _sc_scatter_rows(hpb, dest_rows, n_rows)

        cnt_i = counts.reshape(N_EXPERTS).astype(I32)
        owner = jnp.where(cnt_i > 0, jnp.arange(N_EXPERTS, dtype=I32), N_EXPERTS)
        later = jnp.concatenate([lax.cummin(owner[::-1])[::-1][1:], jnp.full((1,), N_EXPERTS, I32)])
        next_expert = jnp.where(later < N_EXPERTS, later, -1)
        last = lambda b, be, nbu, nxt, valid: jnp.minimum(b, nbu[0] - 1)
        row_blk = lambda: pl.BlockSpec((ROW_BLOCK, hw), lambda b, be, nbu, nxt, valid: (last(b, be, nbu, nxt, valid), 0))
        ya_buf, yb_buf = pl.pallas_call(
            _expert_ffn_kernel,
            grid_spec=pltpu.PrefetchScalarGridSpec(
                num_scalar_prefetch=4,
                grid=(n_blocks,),
                in_specs=[row_blk(), row_blk(),
                          pl.BlockSpec(memory_space=pl.ANY),
                          pl.BlockSpec((1, 1, 2 * D_FF), lambda b, be, nbu, nxt, valid: (be[b], 0, 0)),
                          pl.BlockSpec(memory_space=pl.ANY),
                          pl.BlockSpec((1, 1, D), lambda b, be, nbu, nxt, valid: (be[b], 0, 0))],
                out_specs=[row_blk(), row_blk()],
                scratch_shapes=[pltpu.VMEM((D, 2 * D_FF), F32), pltpu.VMEM((D_FF, D), F32),
                                pltpu.VMEM((D, 2 * D_FF), BF16), pltpu.VMEM((D_FF, D), BF16),
                                pltpu.SemaphoreType.DMA((2,))],
            ),
            out_shape=[jax.ShapeDtypeStruct((n_rows, hw), U32)] * 2,
            compiler_params=_cparams(("arbitrary",)),
            name="expert_ffn",
        )(block_expert[0, :n_blocks], nb_used[0, :1], next_expert, block_valid[0, :n_blocks], xa_buf, xb_buf,
          w_mlp1[0], b1r, w_mlp2[0], b2r)

        dest_flat = destT.reshape(1, n_assign)
        for half, y_buf in enumerate((ya_buf, yb_buf)):
            yg = _sc_gather_rows(y_buf, dest_flat).reshape(TOP_K, N, hw)
            prev = () if out is None else (out,)
            out = pl.pallas_call(
                _combine_kernel if out is None else _combine_into_kernel,
                grid=(N // tm,),
                in_specs=[pl.BlockSpec((tm, D // 2), lambda i: (i, half)),
                          pl.BlockSpec((TOP_K, tm, hw), lambda i: (0, i, 0)),
                          pl.BlockSpec((GATE_ROWS, tm), lambda i: (0, i)),
                          _full((GATE_ROWS, TOP_K * LANES))] + [pl.BlockSpec(memory_space=pl.ANY)] * len(prev),
                out_specs=pl.BlockSpec((tm, D // 2), lambda i: (i + t_off, half)),
                out_shape=jax.ShapeDtypeStruct((B * S, D), F32),
                input_output_aliases={4: 0} if prev else {},
                compiler_params=_cparams(("parallel",)),
                name="combine",
            )(x1, yg, gateT, gate_spread, *prev)
    return out.reshape(B, S, D)
```

```python
import functools

import jax
import jax.numpy as jnp
from jax import lax
from jax.experimental import pallas as pl
from jax.experimental.pallas import tpu as pltpu
from jax.experimental.pallas import tpu_sc as plsc

F32 = jnp.float32
BF16 = jnp.bfloat16
I32 = jnp.int32
U32 = jnp.uint32

D_MODEL = 1024
N_DIFF_HEADS = 4
DIFF_HEAD_DIM = 64
DIFF_V_DIM = 128
DIFF_W = 512
GMLP_W = 256
GMLP_GROUPS = 4
CHUNK = 128
MEM_W = 256
N_MEM_HEADS = 4
HEAD_GROUP = 64
D_IN_PROJ = 2304
ROPE_THETA = 500000.0
ROT_DIM = 16
N_EXPERTS = 32
TOP_K = 4
D_FF = 1024
SWIGLU_LIMIT = 7.0
SWIGLU_ALPHA = 1.702
EPS = 1e-6
LAMBDA_INIT = 0.8 - 0.6

LANES = 128
ROW_BLOCK = 1024
FFN_SUB_ROWS = 256
TOKEN_TILE = 512
LAYER_PARTS = 2
ROUTER_TILE = 1024
GATE_ROWS = 16
ATTN_TILE = 256
ATTN_EXT_ROWS = 128 + 16
VMEM_LIMIT = 56 * 1024 * 1024
NEG_BIG = -1e30
LOG2_E = 1.4426950408889634


def _cparams(sem):
    return pltpu.CompilerParams(dimension_semantics=sem, vmem_limit_bytes=VMEM_LIMIT)


def _dot(a, b):
    return jnp.dot(a, b, preferred_element_type=F32)


def _dot_nt(a, b):
    return lax.dot_general(a, b, (((1,), (1,)), ((), ())), preferred_element_type=F32)


def _rms(x, gain):
    ms = jnp.mean(x * x, axis=-1, keepdims=True)
    return x * lax.rsqrt(ms + EPS) * gain


def _group_rms(t, ones_bd, gain):
    w = ones_bd.shape[0]
    chunks = []
    for j in range(t.shape[1] // w):
        c = t[:, j * w:(j + 1) * w]
        ss = _dot((c * c).astype(BF16), ones_bd)
        chunks.append(c * lax.rsqrt(ss * (1.0 / HEAD_GROUP) + EPS))
    return (chunks[0] if len(chunks) == 1 else jnp.concatenate(chunks, axis=1)) * gain


def _pack_bf16_pairs(v):
    w = v.shape[1] // 2
    bits = lax.bitcast_convert_type(v.astype(BF16).astype(F32), U32)
    return (bits[:, :w] & jnp.uint32(0xFFFF0000)) | (bits[:, w:] >> jnp.uint32(16))


def _unpack_bf16_pairs(words):
    hi = lax.bitcast_convert_type(words & jnp.uint32(0xFFFF0000), F32)
    lo = lax.bitcast_convert_type(words << jnp.uint32(16), F32)
    return hi, lo


def _mem_kv_kernel(mem_ref, gmem_ref, wkv_ref, gck_ref, ones_ref, kT_ref, v_ref):
    m = _rms(mem_ref[0], gmem_ref[...]).astype(BF16)
    kv = _dot(m, wkv_ref[...])
    k = _group_rms(kv[:, :MEM_W], ones_ref[...], gck_ref[...])
    kT_ref[0] = k.T.astype(BF16)
    v_ref[0] = kv[:, MEM_W:].astype(BF16)


def _gelu_tanh(x):
    return 0.5 * x * (1.0 + jnp.tanh(0.7978845608028654 * (x + 0.044715 * (x * x * x))))


def _mixer_in_kernel(x_ref, cs_ref, spread_ref, gmix_ref, win_ref, gq_ref, gk_ref,
                     gsgu_ref, wsp_ref, bsp_ref, gcq_ref, ones256_ref, kT_ref, vm_ref,
                     q_out, k_out, v_out, gc_out):
    tm = x_ref.shape[0]
    hb = _rms(x_ref[...], gmix_ref[...]).astype(BF16)

    def proj(lo, hi):
        return _dot(hb, win_ref[:, lo:hi])

    lane = lax.broadcasted_iota(I32, (tm, LANES), 1)
    first_half = (lane % HEAD_GROUP) < (ROT_DIM // 2)
    tab = lax.dot_general(cs_ref[...], spread_ref[...], (((0,), (0,)), ((), ())), preferred_element_type=F32)
    cosb = tab[:, :LANES] + jnp.where((lane % HEAD_GROUP) >= ROT_DIM, 1.0, 0.0)
    sinb = tab[:, LANES:]

    def norm_rope(t, gain, out_ref):
        tn = _group_rms(t, ones256_ref[...], gain)
        for j in range(DIFF_W // LANES):
            c = tn[:, j * LANES:(j + 1) * LANES]
            partner = jnp.where(first_half, pltpu.roll(c, LANES - ROT_DIM // 2, 1), pltpu.roll(c, ROT_DIM // 2, 1))
            out_ref[:, j * LANES:(j + 1) * LANES] = (c * cosb + partner * sinb).astype(BF16)

    norm_rope(proj(0, DIFF_W), gq_ref[...] * (DIFF_HEAD_DIM ** -0.5 * LOG2_E), q_out)
    norm_rope(proj(DIFF_W, 2 * DIFF_W), gk_ref[...], k_out)
    v_out[...] = proj(2 * DIFF_W, 3 * DIFF_W).astype(BF16)

    z = _gelu_tanh(proj(3 * DIFF_W, 3 * DIFF_W + 2 * GMLP_W))
    u = z[:, :GMLP_W]
    vg = z[:, GMLP_W:]
    vc = vg - jnp.mean(vg, axis=-1, keepdims=True)
    vgn = (vc * lax.rsqrt(jnp.mean(vc * vc, axis=-1, keepdims=True) + EPS) * gsgu_ref[...]).astype(BF16)
    wrow = lax.broadcasted_iota(I32, (CHUNK, GMLP_GROUPS * CHUNK), 0)
    wcol = lax.broadcasted_iota(I32, (CHUNK, GMLP_GROUPS * CHUNK), 1) % CHUNK
    w_causal = jnp.where(wcol <= wrow, wsp_ref[...], 0.0).astype(BF16)
    grp = lax.broadcasted_iota(I32, (CHUNK, GMLP_W), 1) // HEAD_GROUP
    zero_b = jnp.zeros((CHUNK, GMLP_W), BF16)
    for r in range(tm // CHUNK):
        vchunk = vgn[r * CHUNK:(r + 1) * CHUNK, :]
        v_bd = jnp.concatenate([jnp.where(grp == g, vchunk, zero_b) for g in range(GMLP_GROUPS)], axis=0)
        mixed = _dot(w_causal, v_bd) + bsp_ref[...]
        gc_out[r * CHUNK:(r + 1) * CHUNK, 0:GMLP_W] = (u[r * CHUNK:(r + 1) * CHUNK, :] * mixed).astype(BF16)

    pc = proj(3 * DIFF_W + 2 * GMLP_W, D_IN_PROJ)
    qc = _group_rms(pc, ones256_ref[...], gcq_ref[...] * (HEAD_GROUP ** -0.5)).astype(BF16)
    hgrp = lax.broadcasted_iota(I32, (tm, MEM_W), 1) // HEAD_GROUP
    zero_q = jnp.zeros((tm, MEM_W), BF16)
    q_st = jnp.concatenate([jnp.where(hgrp == h, qc, zero_q) for h in range(N_MEM_HEADS)], axis=0)
    s = _dot(q_st, kT_ref[0])
    p = jnp.exp(s - jnp.max(s, axis=-1, keepdims=True))
    o = _dot(p.astype(BF16), vm_ref[0]) / jnp.sum(p, axis=-1, keepdims=True)
    c = jnp.zeros((tm, MEM_W), F32)
    for h in range(N_MEM_HEADS):
        c = c + jnp.where(hgrp == h, o[h * tm:(h + 1) * tm, :], 0.0)
    gc_out[:, GMLP_W:GMLP_W + MEM_W] = c.astype(BF16)


def _diff_attn_kernel(q_ref, k_ref, v_ref, lq1_ref, lk1_ref, lq2_ref, lk2_ref, gsub_ref, o_ref,
                      s0_ref, s1_ref, m_ref, acc_ref, vT_ref):
    tq = q_ref.shape[0]
    seq = k_ref.shape[0]
    i = pl.program_id(1)
    lane = lax.broadcasted_iota(I32, (tq, LANES), 1)
    heads = range(N_DIFF_HEADS)
    hl = lambda h: slice(h * DIFF_V_DIM, (h + 1) * DIFF_V_DIM)
    ext_rows = vT_ref.shape[1]

    @pl.when(i == 0)
    def _():
        ones_row = jnp.where(lax.broadcasted_iota(I32, (ext_rows - DIFF_V_DIM, seq), 0) == 0, 1.0, 0.0).astype(BF16)
        for h in heads:
            for c in range(seq // tq):
                vT_ref[h, 0:DIFF_V_DIM, c * tq:(c + 1) * tq] = v_ref[c * tq:(c + 1) * tq, hl(h)].T
            vT_ref[h, DIFF_V_DIM:ext_rows, :] = ones_row

    def stacked_qT(h):
        q = q_ref[:, hl(h)]
        zero = jnp.zeros_like(q)
        return jnp.concatenate([jnp.where(lane < DIFF_HEAD_DIM, q, zero), jnp.where(lane >= DIFF_HEAD_DIM, q, zero)],
                               axis=0).T

    qsT = [stacked_qT(h) for h in heads]

    def scores(t, s_ref):
        rows = pl.ds(pl.multiple_of(t * tq, tq), tq)
        for h in heads:
            s_ref[h] = _dot(k_ref[rows, hl(h)], qsT[h])

    def update(t, s_ref, causal):
        cols = pl.ds(pl.multiple_of(t * tq, tq), tq)
        for h in heads:
            s = s_ref[h]
            if causal:
                key = lax.broadcasted_iota(I32, (tq, 2 * tq), 0)
                qry = lax.broadcasted_iota(I32, (tq, 2 * tq), 1) % tq
                s = jnp.where(key <= qry, s, NEG_BIG)
            m = m_ref[h]
            m_new = jnp.maximum(m, jnp.max(s, axis=0, keepdims=True))
            alpha = jnp.exp2(m - m_new)
            p = jnp.exp2(s - m_new)
            m_ref[h] = m_new
            acc_ref[h] = alpha * acc_ref[h] + _dot(vT_ref[h, :, cols], p.astype(BF16))

    m_ref[...] = jnp.full(m_ref.shape, NEG_BIG, F32)
    acc_ref[...] = jnp.zeros(acc_ref.shape, F32)
    scores(0, s0_ref)

    def pair(pidx, carry):
        t = 2 * pidx
        scores(t + 1, s1_ref)
        update(t, s0_ref, False)
        scores(t + 2, s0_ref)
        update(t + 1, s1_ref, False)
        return carry

    lax.fori_loop(0, i // 2, pair, 0)

    @pl.when(i % 2 == 0)
    def _():
        update(i, s0_ref, True)

    @pl.when(i % 2 == 1)
    def _():
        scores(i, s1_ref)
        update(i - 1, s0_ref, False)
        update(i, s1_ref, True)

    lam = (jnp.exp(jnp.sum(lq1_ref[...] * lk1_ref[...], axis=-1, keepdims=True))
           - jnp.exp(jnp.sum(lq2_ref[...] * lk2_ref[...], axis=-1, keepdims=True)) + LAMBDA_INIT)
    for h in heads:
        on = acc_ref[h, 0:DIFF_V_DIM, :] * (1.0 / acc_ref[h, DIFF_V_DIM:DIFF_V_DIM + 1, :])
        o = on[:, :tq] - lam * on[:, tq:]
        ms = jnp.mean(o * o, axis=0, keepdims=True)
        o = o * lax.rsqrt(ms + EPS) * gsub_ref[...] * (1.0 - LAMBDA_INIT)
        o_ref[:, hl(h)] = o.T.astype(BF16)


def _out_router_kernel(x_ref, a_ref, gc_ref, wo_ref, gffn_ref, wrh_ref, wrl_ref, br_ref, before_ref,
                       x1_out, hpa_out, hpb_out, idx_out, gate_out, pos_out, cnt_out, carry_ref):
    tm = x_ref.shape[0]

    @pl.when(pl.program_id(0) == 0)
    def _():
        carry_ref[...] = jnp.zeros_like(carry_ref)

    sub = before_ref.shape[0]
    wr_stack = jnp.concatenate([wrh_ref[...], wrl_ref[...]], axis=0)
    eio = lax.broadcasted_iota(I32, (N_EXPERTS, sub), 0)
    before_b = before_ref[...]
    carry = carry_ref[...]
    for r in range(tm // sub):
        rows = slice(r * sub, (r + 1) * sub)
        mix = jnp.concatenate([a_ref[rows, :], gc_ref[rows, :]], axis=1)
        x1 = x_ref[rows, :] + _dot(mix, wo_ref[...])
        x1_out[rows, :] = x1
        h2 = _rms(x1, gffn_ref[...])
        hb = h2.astype(BF16)
        hpa_out[rows, :] = _pack_bf16_pairs(h2[:, :D_MODEL // 2])
        hpb_out[rows, :] = _pack_bf16_pairs(h2[:, D_MODEL // 2:])

        h_lo = (h2 - hb.astype(F32)).astype(BF16)
        both = _dot_nt(wr_stack, hb)
        logits = (both[:N_EXPERTS] + both[N_EXPERTS:]) + _dot_nt(wrh_ref[...], h_lo) + br_ref[...]

        vals, idxs, sels = [], [], []
        cur = logits
        for _ in range(TOP_K):
            m = jnp.max(cur, axis=0, keepdims=True)
            ik = jnp.min(jnp.where(cur == m, eio, N_EXPERTS), axis=0, keepdims=True)
            sel = eio == ik
            cur = jnp.where(sel, -jnp.inf, cur)
            vals.append(m)
            idxs.append(ik)
            sels.append(sel)
        es = [jnp.exp(v - vals[0]) for v in vals]
        tot = es[0] + es[1] + es[2] + es[3]
        gates = jnp.concatenate([e / tot for e in es], axis=0)
        g_hi = gates.astype(BF16)
        g_lo = (gates - g_hi.astype(F32)).astype(BF16)
        gate_out[:, rows] = jnp.concatenate(
            [g_hi, g_lo, jnp.zeros((GATE_ROWS - 2 * TOP_K, sub), BF16)], axis=0)
        idx_out[:, rows] = jnp.concatenate(idxs, axis=0)

        cnt = jnp.zeros((N_EXPERTS, sub), F32)
        for sel in sels:
            cnt = cnt + jnp.where(sel, 1.0, 0.0)
        base = carry + _dot(cnt.astype(BF16), before_b)
        pos_out[:, rows] = jnp.concatenate(
            [jnp.sum(jnp.where(sel, base, 0.0), axis=0, keepdims=True) for sel in sels], axis=0).astype(I32)
        carry = carry + jnp.sum(cnt, axis=1, keepdims=True)
    carry_ref[...] = carry
    cnt_out[...] = carry


def _dest_kernel(cnt_ref, idx_ref, pos_ref, dest_out, be_out, valid_out, nbu_out):
    idx = idx_ref[...]
    dest = pos_ref[...]
    bidx = lax.broadcasted_iota(I32, be_out.shape, 1)
    be = jnp.zeros(be_out.shape, I32)
    valid = jnp.zeros(be_out.shape, I32)
    run = jnp.int32(0)
    for e in range(N_EXPERTS):
        dest = dest + jnp.where(idx == e, run, 0)
        first_block = run // ROW_BLOCK
        run = run + ((cnt_ref[e] + (ROW_BLOCK - 1)) // ROW_BLOCK) * ROW_BLOCK
        mine = jnp.logical_and(bidx >= first_block, bidx < run // ROW_BLOCK)
        valid = jnp.where(mine, jnp.clip(cnt_ref[e] - (bidx - first_block) * ROW_BLOCK, 0, ROW_BLOCK), valid)
        be = be + jnp.where(bidx >= run // ROW_BLOCK, 1, 0)
    dest_out[...] = dest
    be_out[...] = jnp.minimum(be, N_EXPERTS - 1)
    valid_out[...] = valid
    nbu_out[...] = jnp.zeros(nbu_out.shape, I32) + run // ROW_BLOCK


def _expert_ffn_kernel(be_ref, nbu_ref, nxt_ref, valid_ref, xa_ref, xb_ref, w1_hbm, b1_ref, w2_hbm, b2_ref,
                       ya_ref, yb_ref, w1f_ref, w2f_ref, w1s_ref, w2s_ref, sem_ref):
    b = pl.program_id(0)

    def weight_copies(e):
        return (pltpu.make_async_copy(w1_hbm.at[e], w1f_ref, sem_ref.at[0]),
                pltpu.make_async_copy(w2_hbm.at[e], w2f_ref, sem_ref.at[1]))

    def mlp(rows):
        parts = _unpack_bf16_pairs(xa_ref[0:rows, :]) + _unpack_bf16_pairs(xb_ref[0:rows, :])
        xrow = jnp.concatenate([p.astype(BF16) for p in parts], axis=1)
        hm = _dot(xrow, w1s_ref[...]) + b1_ref[0]
        glu = jnp.minimum(hm[:, :D_FF], SWIGLU_LIMIT)
        lin = jnp.clip(hm[:, D_FF:], -SWIGLU_LIMIT, SWIGLU_LIMIT)
        act = glu * jax.nn.sigmoid(SWIGLU_ALPHA * glu) * (lin + 1.0)
        y = _dot(act.astype(BF16), w2s_ref[...]) + b2_ref[0]
        ya_ref[0:rows, :] = _pack_bf16_pairs(y[:, :D_MODEL // 2])
        yb_ref[0:rows, :] = _pack_bf16_pairs(y[:, D_MODEL // 2:])

    @pl.when(b < nbu_ref[0])
    def _():
        e = be_ref[b]

        @pl.when(b == 0)
        def _():
            for cp in weight_copies(e):
                cp.start()

        @pl.when(jnp.logical_or(b == 0, e != be_ref[jnp.maximum(b - 1, 0)]))
        def _():
            for cp in weight_copies(e):
                cp.wait()
            w1s_ref[...] = w1f_ref[...].astype(BF16)
            w2s_ref[...] = w2f_ref[...].astype(BF16)
            nxt = nxt_ref[e]

            @pl.when(nxt >= 0)
            def _():
                for cp in weight_copies(nxt):
                    cp.start()

        n_sub = (valid_ref[b] + (FFN_SUB_ROWS - 1)) // FFN_SUB_ROWS
        for k in range(1, ROW_BLOCK // FFN_SUB_ROWS + 1):
            pl.when(n_sub == k)(functools.partial(mlp, k * FFN_SUB_ROWS))


SC_WINDOW = 128


def _sc_mesh():
    return plsc.VectorSubcoreMesh(core_axis_name="c", subcore_axis_name="s")


def _sc_gather_rows(table, idx_row):
    n = idx_row.shape[1]
    width = table.shape[1]

    @functools.partial(pl.kernel, out_type=jax.ShapeDtypeStruct((n, width), table.dtype), mesh=_sc_mesh(),
                       scratch_types=[])
    def gather_kernel(t_hbm, i_hbm, o_hbm):
        def body(i_vmem, o_vmem):
            pltpu.sync_copy(t_hbm.at[i_vmem.at[0]], o_vmem)

        pltpu.emit_pipeline(
            body,
            grid=(n // SC_WINDOW,),
            in_specs=[pl.BlockSpec((1, SC_WINDOW), lambda i: (0, i))],
            out_specs=[pl.BlockSpec((SC_WINDOW, width), lambda i: (i, 0))],
            core_axis_name=("c", "s"),
            dimension_semantics=(pltpu.PARALLEL,),
        )(i_hbm, o_hbm)

    return gather_kernel(table, idx_row)


def _sc_scatter_rows(rows, idx_rows, n_out):
    n, width = rows.shape

    @functools.partial(pl.kernel, out_type=jax.ShapeDtypeStruct((n_out, width), rows.dtype), mesh=_sc_mesh(),
                       scratch_types=[])
    def scatter_kernel(r_hbm, *refs):
        i_hbms, o_hbm = refs[:-1], refs[-1]

        def body(r_vmem, *i_vmems):
            for i_vmem in i_vmems:
                pltpu.sync_copy(r_vmem, o_hbm.at[i_vmem.at[0]])

        pltpu.emit_pipeline(
            body,
            grid=(n // SC_WINDOW,),
            in_specs=[pl.BlockSpec((SC_WINDOW, width), lambda i: (i, 0))]
            + [pl.BlockSpec((1, SC_WINDOW), lambda i: (0, i)) for _ in i_hbms],
            out_specs=[],
            core_axis_name=("c", "s"),
            dimension_semantics=(pltpu.PARALLEL,),
        )(r_hbm, *i_hbms)

    return scatter_kernel(rows, *idx_rows)


def _combine_kernel(x1_ref, yg_ref, gate_ref, spread_ref, o_ref):
    q = D_MODEL // 4
    x1 = x1_ref[...]
    acc = [x1[:, :q], x1[:, q:]]
    gfull = lax.dot_general(gate_ref[...], spread_ref[...], (((0,), (0,)), ((), ())), preferred_element_type=F32)
    for k in range(TOP_K):
        parts = _unpack_bf16_pairs(yg_ref[k])
        g = jnp.tile(gfull[:, k * LANES:(k + 1) * LANES], (1, q // LANES))
        acc = [a + g * p for a, p in zip(acc, parts)]
    o_ref[:, :q] = acc[0]
    o_ref[:, q:] = acc[1]


def _combine_into_kernel(x1_ref, yg_ref, gate_ref, spread_ref, prev_ref, o_ref):
    del prev_ref
    _combine_kernel(x1_ref, yg_ref, gate_ref, spread_ref, o_ref)


def _block_diag_ones(width):
    r = jnp.arange(width) // HEAD_GROUP
    return (r[:, None] == r[None, :]).astype(BF16)


def _rope_tables(positions):
    half = ROT_DIM // 2
    inv_freq = ROPE_THETA ** (-jnp.arange(0, ROT_DIM, 2, dtype=F32) / ROT_DIM)
    ang = inv_freq[:, None] * positions.astype(F32).reshape(1, -1)
    cs = jnp.concatenate([jnp.cos(ang), jnp.sin(ang)], axis=0)
    cs_hi = cs.astype(BF16)
    cs_lo = (cs - cs_hi.astype(F32)).astype(BF16)
    lane = jnp.arange(LANES) % HEAD_GROUP
    j = jnp.arange(half)[:, None]
    lo_half = (lane[None, :] == j).astype(F32)
    hi_half = (lane[None, :] == j + half).astype(F32)
    spread = jnp.concatenate([
        jnp.concatenate([lo_half + hi_half, jnp.zeros((half, LANES), F32)], axis=1),
        jnp.concatenate([jnp.zeros((half, LANES), F32), hi_half - lo_half], axis=1)], axis=0)
    return jnp.concatenate([cs_hi, cs_lo], axis=0), jnp.concatenate([spread, spread], axis=0).astype(BF16)


def _full(shape):
    return pl.BlockSpec(shape, lambda *_: (0,) * len(shape))


def kernel(x, mem, positions, g_mix_norm, w_in, g_dq, g_dk, lambda_q1, lambda_k1, lambda_q2, lambda_k2, g_subln, g_sgu, w_spatial, b_spatial, g_mem_norm, w_mem_kv, g_cq, g_ck, w_out, g_ffn_norm, w_router, b_router, w_mlp1, b_mlp1, w_mlp2, b_mlp2):
    B, S, D = x.shape
    M = mem.shape[1]
    tm = TOKEN_TILE
    tr = ROUTER_TILE
    assert D == D_MODEL and S % tm == 0 and S % ATTN_TILE == 0 and g_mix_norm.shape[0] == 1
    n_parts = LAYER_PARTS if B % LAYER_PARTS == 0 and (B // LAYER_PARTS) * S % tr == 0 else 1
    Bp = B // n_parts
    N = Bp * S
    n_assign = N * TOP_K
    n_blocks = -(-n_assign // ROW_BLOCK) + N_EXPERTS
    n_rows = n_blocks * ROW_BLOCK
    nb_pad = -(-n_blocks // LANES) * LANES

    xf = x.reshape(B * S, D)
    rope_cs, rope_spread = _rope_tables(positions)
    ones256 = _block_diag_ones(MEM_W)
    row = lambda v: v.reshape(1, -1).astype(F32)
    tile_row = lambda v, reps: jnp.tile(v.reshape(1, -1).astype(F32), (1, reps))
    w_in_b, w_out_b, w_kv_b = w_in[0].astype(BF16), w_out[0].astype(BF16), w_mem_kv[0].astype(BF16)
    w_sp_lanes = jnp.transpose(w_spatial[0], (1, 0, 2)).reshape(CHUNK, GMLP_GROUPS * CHUNK)
    b_sp_lanes = jnp.repeat(b_spatial[0].T, HEAD_GROUP, axis=1)
    wr = w_router[0].T.astype(F32)
    wr_hi = wr.astype(BF16)
    wr_lo = (wr - wr_hi.astype(F32)).astype(BF16)
    before = (jnp.arange(tr)[:, None] < jnp.arange(tr)[None, :]).astype(BF16)
    gate_row = jnp.arange(GATE_ROWS)[:, None]
    gate_spread = ((gate_row < 2 * TOP_K) & (gate_row % TOP_K == jnp.arange(TOP_K * LANES)[None, :] // LANES)).astype(BF16)
    b1r = b_mlp1[0].reshape(N_EXPERTS, 1, 2 * D_FF)
    b2r = b_mlp2[0].reshape(N_EXPERTS, 1, D)

    out = None
    for part in range(n_parts):
        b_off = part * Bp
        t_off = part * (N // tm)
        r_off = part * (N // tr)

        kT, vm = pl.pallas_call(
            _mem_kv_kernel,
            grid=(Bp,),
            in_specs=[pl.BlockSpec((1, M, D), lambda b: (b + b_off, 0, 0)), _full((1, D)), _full((D, 2 * MEM_W)),
                      _full((1, MEM_W)), _full((MEM_W, MEM_W))],
            out_specs=[pl.BlockSpec((1, MEM_W, M), lambda b: (b, 0, 0)), pl.BlockSpec((1, M, MEM_W), lambda b: (b, 0, 0))],
            out_shape=[jax.ShapeDtypeStruct((Bp, MEM_W, M), BF16), jax.ShapeDtypeStruct((Bp, M, MEM_W), BF16)],
            compiler_params=_cparams(("parallel",)),
            name="mem_kv",
        )(mem, row(g_mem_norm[0]), w_kv_b, tile_row(g_ck[0], N_MEM_HEADS), ones256)

        tiles_per_batch = S // tm
        tok = lambda w: pl.BlockSpec((tm, w), lambda i: (i, 0))
        tok_in = lambda w: pl.BlockSpec((tm, w), lambda i: (i + t_off, 0))
        qn, kn, vv, gc = pl.pallas_call(
            _mixer_in_kernel,
            grid=(N // tm,),
            in_specs=[tok_in(D), pl.BlockSpec((4 * ROT_DIM // 2, tm), lambda i: (0, i + t_off)),
                      _full((4 * ROT_DIM // 2, 2 * LANES)),
                      _full((1, D)), _full((D, D_IN_PROJ)),
                      _full((1, DIFF_W)), _full((1, DIFF_W)),
                      _full((1, GMLP_W)), _full((CHUNK, GMLP_GROUPS * CHUNK)), _full((CHUNK, GMLP_W)),
                      _full((1, MEM_W)), _full((MEM_W, MEM_W)),
                      pl.BlockSpec((1, MEM_W, M), lambda i: (i // tiles_per_batch, 0, 0)),
                      pl.BlockSpec((1, M, MEM_W), lambda i: (i // tiles_per_batch, 0, 0))],
            out_specs=[tok(DIFF_W), tok(DIFF_W), tok(DIFF_W), tok(GMLP_W + MEM_W)],
            out_shape=[jax.ShapeDtypeStruct((N, DIFF_W), BF16)] * 3 + [jax.ShapeDtypeStruct((N, GMLP_W + MEM_W), BF16)],
            compiler_params=_cparams(("parallel",)),
            name="mixer_in",
        )(xf, rope_cs, rope_spread, row(g_mix_norm[0]), w_in_b,
          tile_row(g_dq[0], 2 * N_DIFF_HEADS), tile_row(g_dk[0], 2 * N_DIFF_HEADS),
          row(g_sgu[0]), w_sp_lanes, b_sp_lanes, tile_row(g_cq[0], N_MEM_HEADS), ones256, kT, vm)

        tq = ATTN_TILE
        nq = S // tq
        head_q = pl.BlockSpec((tq, DIFF_W), lambda b, i: (b * nq + i, 0))
        head_kv = pl.BlockSpec((S, DIFF_W), lambda b, i: (b, 0))
        lam_spec = pl.BlockSpec((1, DIFF_HEAD_DIM), lambda b, i: (0, 0))
        nh = N_DIFF_HEADS
        attn = pl.pallas_call(
            _diff_attn_kernel,
            grid=(Bp, nq),
            in_specs=[head_q, head_kv, head_kv, lam_spec, lam_spec, lam_spec, lam_spec,
                      pl.BlockSpec((DIFF_V_DIM, 1), lambda b, i: (0, 0))],
            out_specs=head_q,
            out_shape=jax.ShapeDtypeStruct((N, DIFF_W), BF16),
            scratch_shapes=[pltpu.VMEM((nh, tq, 2 * tq), F32), pltpu.VMEM((nh, tq, 2 * tq), F32),
                            pltpu.VMEM((nh, 1, 2 * tq), F32), pltpu.VMEM((nh, ATTN_EXT_ROWS, 2 * tq), F32),
                            pltpu.VMEM((nh, ATTN_EXT_ROWS, S), BF16)],
            compiler_params=_cparams(("parallel", "arbitrary")),
            name="diff_attn",
        )(qn, kn, vv, row(lambda_q1[0]), row(lambda_k1[0]), row(lambda_q2[0]), row(lambda_k2[0]),
          g_subln[0].reshape(DIFF_V_DIM, 1).astype(F32))

        rtok = lambda w: pl.BlockSpec((tr, w), lambda i: (i, 0))
        rtok_in = lambda w: pl.BlockSpec((tr, w), lambda i: (i + r_off, 0))
        tokT = lambda: pl.BlockSpec((TOP_K, tr), lambda i: (0, i))
        hw = D // 4
        x1, hpa, hpb, idxT, gateT, posT, counts = pl.pallas_call(
            _out_router_kernel,
            grid=(N // tr,),
            in_specs=[rtok_in(D), rtok(DIFF_W), rtok(GMLP_W + MEM_W), _full((D, D)), _full((1, D)),
                      _full((N_EXPERTS, D)), _full((N_EXPERTS, D)), _full((N_EXPERTS, 1)), _full((tr, tr))],
            out_specs=[rtok(D), rtok(hw), rtok(hw), tokT(), pl.BlockSpec((GATE_ROWS, tr), lambda i: (0, i)), tokT(),
                       _full((N_EXPERTS, 1))],
            out_shape=[jax.ShapeDtypeStruct((N, D), F32), jax.ShapeDtypeStruct((N, hw), U32),
                       jax.ShapeDtypeStruct((N, hw), U32), jax.ShapeDtypeStruct((TOP_K, N), I32),
                       jax.ShapeDtypeStruct((GATE_ROWS, N), BF16),
                       jax.ShapeDtypeStruct((TOP_K, N), I32), jax.ShapeDtypeStruct((N_EXPERTS, 1), F32)],
            scratch_shapes=[pltpu.VMEM((N_EXPERTS, 1), F32)],
            compiler_params=_cparams(("arbitrary",)),
            name="out_router",
        )(xf, attn, gc, w_out_b, row(g_ffn_norm[0]), wr_hi, wr_lo, b_router[0].reshape(N_EXPERTS, 1).astype(F32), before)

        destT, block_expert, block_valid, nb_used = pl.pallas_call(
            _dest_kernel,
            grid_spec=pltpu.PrefetchScalarGridSpec(
                num_scalar_prefetch=1,
                grid=(1,),
                in_specs=[pl.BlockSpec((TOP_K, N), lambda i, c: (0, 0)), pl.BlockSpec((TOP_K, N), lambda i, c: (0, 0))],
                out_specs=[pl.BlockSpec((TOP_K, N), lambda i, c: (0, 0)), pl.BlockSpec((1, nb_pad), lambda i, c: (0, 0)),
                           pl.BlockSpec((1, nb_pad), lambda i, c: (0, 0)), pl.BlockSpec((1, LANES), lambda i, c: (0, 0))],
            ),
            out_shape=[jax.ShapeDtypeStruct((TOP_K, N), I32), jax.ShapeDtypeStruct((1, nb_pad), I32),
                       jax.ShapeDtypeStruct((1, nb_pad), I32), jax.ShapeDtypeStruct((1, LANES), I32)],
            compiler_params=_cparams(("arbitrary",)),
            name="dest",
        )(counts.reshape(N_EXPERTS).astype(I32), idxT, posT)

        dest_rows = [destT[k].reshape(1, N) for k in range(TOP_K)]
        xa_buf = _sc_scatter_rows(hpa, dest_rows, n_rows)
        xb_buf = _sc_scatter_rows(hpb, dest_rows, n_rows)

        cnt_i = counts.reshape(N_EXPERTS).astype(I32)
        owner = jnp.where(cnt_i > 0, jnp.arange(N_EXPERTS, dtype=I32), N_EXPERTS)
        later = jnp.concatenate([lax.cummin(owner[::-1])[::-1][1:], jnp.full((1,), N_EXPERTS, I32)])
        next_expert = jnp.where(later < N_EXPERTS, later, -1)
        last = lambda b, be, nbu, nxt, valid: jnp.minimum(b, nbu[0] - 1)
        row_blk = lambda: pl.BlockSpec((ROW_BLOCK, hw), lambda b, be, nbu, nxt, valid: (last(b, be, nbu, nxt, valid), 0))
        ya_buf, yb_buf = pl.pallas_call(
            _expert_ffn_kernel,
            grid_spec=pltpu.PrefetchScalarGridSpec(
                num_scalar_prefetch=4,
                grid=(n_blocks,),
                in_specs=[row_blk(), row_blk(),
                          pl.BlockSpec(memory_space=pl.ANY),
                          pl.BlockSpec((1, 1, 2 * D_FF), lambda b, be, nbu, nxt, valid: (be[b], 0, 0)),
                          pl.BlockSpec(memory_space=pl.ANY),
                          pl.BlockSpec((1, 1, D), lambda b, be, nbu, nxt, valid: (be[b], 0, 0))],
                out_specs=[row_blk(), row_blk()],
                scratch_shapes=[pltpu.VMEM((D, 2 * D_FF), F32), pltpu.VMEM((D_FF, D), F32),
                                pltpu.VMEM((D, 2 * D_FF), BF16), pltpu.VMEM((D_FF, D), BF16),
                                pltpu.SemaphoreType.DMA((2,))],
            ),
            out_shape=[jax.ShapeDtypeStruct((n_rows, hw), U32)] * 2,
            compiler_params=_cparams(("arbitrary",)),
            name="expert_ffn",
        )(block_expert[0, :n_blocks], nb_used[0, :1], next_expert, block_valid[0, :n_blocks], xa_buf, xb_buf,
          w_mlp1[0], b1r, w_mlp2[0], b2r)

        dest_flat = destT.reshape(1, n_assign)
        for half, y_buf in enumerate((ya_buf, yb_buf)):
            yg = _sc_gather_rows(y_buf, dest_flat).reshape(TOP_K, N, hw)
            prev = () if out is None else (out,)
            out = pl.pallas_call(
                _combine_kernel if out is None else _combine_into_kernel,
                grid=(N // tm,),
                in_specs=[pl.BlockSpec((tm, D // 2), lambda i: (i, half)),
                          pl.BlockSpec((TOP_K, tm, hw), lambda i: (0, i, 0)),
                          pl.BlockSpec((GATE_ROWS, tm), lambda i: (0, i)),
                          _full((GATE_ROWS, TOP_K * LANES))] + [pl.BlockSpec(memory_space=pl.ANY)] * len(prev),
                out_specs=pl.BlockSpec((tm, D // 2), lambda i: (i + t_off, half)),
                out_shape=jax.ShapeDtypeStruct((B * S, D), F32),
                input_output_aliases={4: 0} if prev else {},
                compiler_params=_cparams(("parallel",)),
                name="combine",
            )(x1, yg, gateT, gate_spread, *prev)
    return out.reshape(B, S, D)
```

```python
import functools

import jax
import jax.numpy as jnp
from jax import lax
from jax.experimental import pallas as pl
from jax.experimental.pallas import tpu as pltpu
from jax.experimental.pallas import tpu_sc as plsc

F32 = jnp.float32
BF16 = jnp.bfloat16
I32 = jnp.int32
U32 = jnp.uint32

D_MODEL = 1024
N_DIFF_HEADS = 4
DIFF_HEAD_DIM = 64
DIFF_V_DIM = 128
DIFF_W = 512
GMLP_W = 256
GMLP_GROUPS = 4
CHUNK = 128
MEM_W = 256
N_MEM_HEADS = 4
HEAD_GROUP = 64
D_IN_PROJ = 2304
ROPE_THETA = 500000.0
ROT_DIM = 16
N_EXPERTS = 32
TOP_K = 4
D_FF = 1024
SWIGLU_LIMIT = 7.0
SWIGLU_ALPHA = 1.702
EPS = 1e-6
LAMBDA_INIT = 0.8 - 0.6

LANES = 128
ROW_BLOCK = 1024
FFN_SUB_ROWS = 256
TOKEN_TILE = 512
LAST_PART_DIVISOR = 4
ROUTER_TILE = 1024
GATE_ROWS = 16
ATTN_TILE = 256
ATTN_EXT_ROWS = 128 + 16
VMEM_LIMIT = 56 * 1024 * 1024
NEG_BIG = -1e30
LOG2_E = 1.4426950408889634


def _cparams(sem):
    return pltpu.CompilerParams(dimension_semantics=sem, vmem_limit_bytes=VMEM_LIMIT)


def _dot(a, b):
    return jnp.dot(a, b, preferred_element_type=F32)


def _dot_nt(a, b):
    return lax.dot_general(a, b, (((1,), (1,)), ((), ())), preferred_element_type=F32)


def _rms(x, gain):
    ms = jnp.mean(x * x, axis=-1, keepdims=True)
    return x * lax.rsqrt(ms + EPS) * gain


def _group_rms(t, ones_bd, gain):
    w = ones_bd.shape[0]
    chunks = []
    for j in range(t.shape[1] // w):
        c = t[:, j * w:(j + 1) * w]
        ss = _dot((c * c).astype(BF16), ones_bd)
        chunks.append(c * lax.rsqrt(ss * (1.0 / HEAD_GROUP) + EPS))
    return (chunks[0] if len(chunks) == 1 else jnp.concatenate(chunks, axis=1)) * gain


def _pack_bf16_pairs(v):
    w = v.shape[1] // 2
    bits = lax.bitcast_convert_type(v.astype(BF16).astype(F32), U32)
    return (bits[:, :w] & jnp.uint32(0xFFFF0000)) | (bits[:, w:] >> jnp.uint32(16))


def _unpack_bf16_pairs(words):
    hi = lax.bitcast_convert_type(words & jnp.uint32(0xFFFF0000), F32)
    lo = lax.bitcast_convert_type(words << jnp.uint32(16), F32)
    return hi, lo


def _mem_kv_kernel(mem_ref, gmem_ref, wkv_ref, gck_ref, ones_ref, kT_ref, v_ref):
    m = _rms(mem_ref[0], gmem_ref[...]).astype(BF16)
    kv = _dot(m, wkv_ref[...])
    k = _group_rms(kv[:, :MEM_W], ones_ref[...], gck_ref[...])
    kT_ref[0] = k.T.astype(BF16)
    v_ref[0] = kv[:, MEM_W:].astype(BF16)


def _gelu_tanh(x):
    return 0.5 * x * (1.0 + jnp.tanh(0.7978845608028654 * (x + 0.044715 * (x * x * x))))


def _mixer_in_kernel(x_ref, cs_ref, spread_ref, gmix_ref, win_ref, gq_ref, gk_ref,
                     gsgu_ref, wsp_ref, bsp_ref, gcq_ref, ones256_ref, kT_ref, vm_ref,
                     q_out, k_out, v_out, gc_out):
    tm = x_ref.shape[0]
    hb = _rms(x_ref[...], gmix_ref[...]).astype(BF16)

    def proj(lo, hi):
        return _dot(hb, win_ref[:, lo:hi])

    lane = lax.broadcasted_iota(I32, (tm, LANES), 1)
    first_half = (lane % HEAD_GROUP) < (ROT_DIM // 2)
    tab = lax.dot_general(cs_ref[...], spread_ref[...], (((0,), (0,)), ((), ())), preferred_element_type=F32)
    cosb = tab[:, :LANES] + jnp.where((lane % HEAD_GROUP) >= ROT_DIM, 1.0, 0.0)
    sinb = tab[:, LANES:]

    def norm_rope(t, gain, out_ref):
        tn = _group_rms(t, ones256_ref[...], gain)
        for j in range(DIFF_W // LANES):
            c = tn[:, j * LANES:(j + 1) * LANES]
            partner = jnp.where(first_half, pltpu.roll(c, LANES - ROT_DIM // 2, 1), pltpu.roll(c, ROT_DIM // 2, 1))
            out_ref[:, j * LANES:(j + 1) * LANES] = (c * cosb + partner * sinb).astype(BF16)

    norm_rope(proj(0, DIFF_W), gq_ref[...] * (DIFF_HEAD_DIM ** -0.5 * LOG2_E), q_out)
    norm_rope(proj(DIFF_W, 2 * DIFF_W), gk_ref[...], k_out)
    v_out[...] = proj(2 * DIFF_W, 3 * DIFF_W).astype(BF16)

    z = _gelu_tanh(proj(3 * DIFF_W, 3 * DIFF_W + 2 * GMLP_W))
    u = z[:, :GMLP_W]
    vg = z[:, GMLP_W:]
    vc = vg - jnp.mean(vg, axis=-1, keepdims=True)
    vgn = (vc * lax.rsqrt(jnp.mean(vc * vc, axis=-1, keepdims=True) + EPS) * gsgu_ref[...]).astype(BF16)
    wrow = lax.broadcasted_iota(I32, (CHUNK, GMLP_GROUPS * CHUNK), 0)
    wcol = lax.broadcasted_iota(I32, (CHUNK, GMLP_GROUPS * CHUNK), 1) % CHUNK
    w_causal = jnp.where(wcol <= wrow, wsp_ref[...], 0.0).astype(BF16)
    grp = lax.broadcasted_iota(I32, (CHUNK, GMLP_W), 1) // HEAD_GROUP
    zero_b = jnp.zeros((CHUNK, GMLP_W), BF16)
    for r in range(tm // CHUNK):
        vchunk = vgn[r * CHUNK:(r + 1) * CHUNK, :]
        v_bd = jnp.concatenate([jnp.where(grp == g, vchunk, zero_b) for g in range(GMLP_GROUPS)], axis=0)
        mixed = _dot(w_causal, v_bd) + bsp_ref[...]
        gc_out[r * CHUNK:(r + 1) * CHUNK, 0:GMLP_W] = (u[r * CHUNK:(r + 1) * CHUNK, :] * mixed).astype(BF16)

    pc = proj(3 * DIFF_W + 2 * GMLP_W, D_IN_PROJ)
    qc = _group_rms(pc, ones256_ref[...], gcq_ref[...] * (HEAD_GROUP ** -0.5)).astype(BF16)
    hgrp = lax.broadcasted_iota(I32, (tm, MEM_W), 1) // HEAD_GROUP
    zero_q = jnp.zeros((tm, MEM_W), BF16)
    q_st = jnp.concatenate([jnp.where(hgrp == h, qc, zero_q) for h in range(N_MEM_HEADS)], axis=0)
    s = _dot(q_st, kT_ref[0])
    p = jnp.exp(s - jnp.max(s, axis=-1, keepdims=True))
    o = _dot(p.astype(BF16), vm_ref[0]) / jnp.sum(p, axis=-1, keepdims=True)
    c = jnp.zeros((tm, MEM_W), F32)
    for h in range(N_MEM_HEADS):
        c = c + jnp.where(hgrp == h, o[h * tm:(h + 1) * tm, :], 0.0)
    gc_out[:, GMLP_W:GMLP_W + MEM_W] = c.astype(BF16)


def _diff_attn_kernel(q_ref, k_ref, v_ref, lq1_ref, lk1_ref, lq2_ref, lk2_ref, gsub_ref, o_ref,
                      s0_ref, s1_ref, m_ref, acc_ref, vT_ref):
    tq = q_ref.shape[0]
    seq = k_ref.shape[0]
    i = pl.program_id(1)
    lane = lax.broadcasted_iota(I32, (tq, LANES), 1)
    heads = range(N_DIFF_HEADS)
    hl = lambda h: slice(h * DIFF_V_DIM, (h + 1) * DIFF_V_DIM)
    ext_rows = vT_ref.shape[1]

    @pl.when(i == 0)
    def _():
        ones_row = jnp.where(lax.broadcasted_iota(I32, (ext_rows - DIFF_V_DIM, seq), 0) == 0, 1.0, 0.0).astype(BF16)
        for h in heads:
            for c in range(seq // tq):
                vT_ref[h, 0:DIFF_V_DIM, c * tq:(c + 1) * tq] = v_ref[c * tq:(c + 1) * tq, hl(h)].T
            vT_ref[h, DIFF_V_DIM:ext_rows, :] = ones_row

    def stacked_qT(h):
        q = q_ref[:, hl(h)]
        zero = jnp.zeros_like(q)
        return jnp.concatenate([jnp.where(lane < DIFF_HEAD_DIM, q, zero), jnp.where(lane >= DIFF_HEAD_DIM, q, zero)],
                               axis=0).T

    qsT = [stacked_qT(h) for h in heads]

    def scores(t, s_ref):
        rows = pl.ds(pl.multiple_of(t * tq, tq), tq)
        for h in heads:
            s_ref[h] = _dot(k_ref[rows, hl(h)], qsT[h])

    def update(t, s_ref, causal):
        cols = pl.ds(pl.multiple_of(t * tq, tq), tq)
        for h in heads:
            s = s_ref[h]
            if causal:
                key = lax.broadcasted_iota(I32, (tq, 2 * tq), 0)
                qry = lax.broadcasted_iota(I32, (tq, 2 * tq), 1) % tq
                s = jnp.where(key <= qry, s, NEG_BIG)
            m = m_ref[h]
            m_new = jnp.maximum(m, jnp.max(s, axis=0, keepdims=True))
            alpha = jnp.exp2(m - m_new)
            p = jnp.exp2(s - m_new)
            m_ref[h] = m_new
            acc_ref[h] = alpha * acc_ref[h] + _dot(vT_ref[h, :, cols], p.astype(BF16))

    m_ref[...] = jnp.full(m_ref.shape, NEG_BIG, F32)
    acc_ref[...] = jnp.zeros(acc_ref.shape, F32)
    scores(0, s0_ref)

    def pair(pidx, carry):
        t = 2 * pidx
        scores(t + 1, s1_ref)
        update(t, s0_ref, False)
        scores(t + 2, s0_ref)
        update(t + 1, s1_ref, False)
        return carry

    lax.fori_loop(0, i // 2, pair, 0)

    @pl.when(i % 2 == 0)
    def _():
        update(i, s0_ref, True)

    @pl.when(i % 2 == 1)
    def _():
        scores(i, s1_ref)
        update(i - 1, s0_ref, False)
        update(i, s1_ref, True)

    lam = (jnp.exp(jnp.sum(lq1_ref[...] * lk1_ref[...], axis=-1, keepdims=True))
           - jnp.exp(jnp.sum(lq2_ref[...] * lk2_ref[...], axis=-1, keepdims=True)) + LAMBDA_INIT)
    for h in heads:
        on = acc_ref[h, 0:DIFF_V_DIM, :] * (1.0 / acc_ref[h, DIFF_V_DIM:DIFF_V_DIM + 1, :])
        o = on[:, :tq] - lam * on[:, tq:]
        ms = jnp.mean(o * o, axis=0, keepdims=True)
        o = o * lax.rsqrt(ms + EPS) * gsub_ref[...] * (1.0 - LAMBDA_INIT)
        o_ref[:, hl(h)] = o.T.astype(BF16)


def _out_router_kernel(x_ref, a_ref, gc_ref, wo_ref, gffn_ref, wrh_ref, wrl_ref, br_ref, before_ref,
                       x1_out, hpa_out, hpb_out, idx_out, gate_out, pos_out, cnt_out, carry_ref):
    tm = x_ref.shape[0]

    @pl.when(pl.program_id(0) == 0)
    def _():
        carry_ref[...] = jnp.zeros_like(carry_ref)

    sub = before_ref.shape[0]
    wr_stack = jnp.concatenate([wrh_ref[...], wrl_ref[...]], axis=0)
    eio = lax.broadcasted_iota(I32, (N_EXPERTS, sub), 0)
    before_b = before_ref[...]
    carry = carry_ref[...]
    for r in range(tm // sub):
        rows = slice(r * sub, (r + 1) * sub)
        mix = jnp.concatenate([a_ref[rows, :], gc_ref[rows, :]], axis=1)
        x1 = x_ref[rows, :] + _dot(mix, wo_ref[...])
        x1_out[rows, :] = x1
        h2 = _rms(x1, gffn_ref[...])
        hb = h2.astype(BF16)
        hpa_out[rows, :] = _pack_bf16_pairs(h2[:, :D_MODEL // 2])
        hpb_out[rows, :] = _pack_bf16_pairs(h2[:, D_MODEL // 2:])

        h_lo = (h2 - hb.astype(F32)).astype(BF16)
        both = _dot_nt(wr_stack, hb)
        logits = (both[:N_EXPERTS] + both[N_EXPERTS:]) + _dot_nt(wrh_ref[...], h_lo) + br_ref[...]

        vals, idxs, sels = [], [], []
        cur = logits
        for _ in range(TOP_K):
            m = jnp.max(cur, axis=0, keepdims=True)
            ik = jnp.min(jnp.where(cur == m, eio, N_EXPERTS), axis=0, keepdims=True)
            sel = eio == ik
            cur = jnp.where(sel, -jnp.inf, cur)
            vals.append(m)
            idxs.append(ik)
            sels.append(sel)
        es = [jnp.exp(v - vals[0]) for v in vals]
        tot = es[0] + es[1] + es[2] + es[3]
        gates = jnp.concatenate([e / tot for e in es], axis=0)
        g_hi = gates.astype(BF16)
        g_lo = (gates - g_hi.astype(F32)).astype(BF16)
        gate_out[:, rows] = jnp.concatenate(
            [g_hi, g_lo, jnp.zeros((GATE_ROWS - 2 * TOP_K, sub), BF16)], axis=0)
        idx_out[:, rows] = jnp.concatenate(idxs, axis=0)

        cnt = jnp.zeros((N_EXPERTS, sub), F32)
        for sel in sels:
            cnt = cnt + jnp.where(sel, 1.0, 0.0)
        base = carry + _dot(cnt.astype(BF16), before_b)
        pos_out[:, rows] = jnp.concatenate(
            [jnp.sum(jnp.where(sel, base, 0.0), axis=0, keepdims=True) for sel in sels], axis=0).astype(I32)
        carry = carry + jnp.sum(cnt, axis=1, keepdims=True)
    carry_ref[...] = carry
    cnt_out[...] = carry


def _dest_kernel(cnt_ref, idx_ref, pos_ref, dest_out, be_out, valid_out, nbu_out):
    idx = idx_ref[...]
    dest = pos_ref[...]
    bidx = lax.broadcasted_iota(I32, be_out.shape, 1)
    be = jnp.zeros(be_out.shape, I32)
    valid = jnp.zeros(be_out.shape, I32)
    run = jnp.int32(0)
    for e in range(N_EXPERTS):
        dest = dest + jnp.where(idx == e, run, 0)
        first_block = run // ROW_BLOCK
        run = run + ((cnt_ref[e] + (ROW_BLOCK - 1)) // ROW_BLOCK) * ROW_BLOCK
        mine = jnp.logical_and(bidx >= first_block, bidx < run // ROW_BLOCK)
        valid = jnp.where(mine, jnp.clip(cnt_ref[e] - (bidx - first_block) * ROW_BLOCK, 0, ROW_BLOCK), valid)
        be = be + jnp.where(bidx >= run // ROW_BLOCK, 1, 0)
    dest_out[...] = dest
    be_out[...] = jnp.minimum(be, N_EXPERTS - 1)
    valid_out[...] = valid
    nbu_out[...] = jnp.zeros(nbu_out.shape, I32) + run // ROW_BLOCK


def _expert_ffn_kernel(be_ref, nbu_ref, nxt_ref, valid_ref, xa_ref, xb_ref, w1_hbm, b1_ref, w2_hbm, b2_ref,
                       ya_ref, yb_ref, w1f_ref, w2f_ref, w1s_ref, w2s_ref, sem_ref):
    b = pl.program_id(0)

    def weight_copies(e):
        return (pltpu.make_async_copy(w1_hbm.at[e], w1f_ref, sem_ref.at[0]),
                pltpu.make_async_copy(w2_hbm.at[e], w2f_ref, sem_ref.at[1]))

    def mlp(rows):
        parts = _unpack_bf16_pairs(xa_ref[0:rows, :]) + _unpack_bf16_pairs(xb_ref[0:rows, :])
        xrow = jnp.concatenate([p.astype(BF16) for p in parts], axis=1)
        hm = _dot(xrow, w1s_ref[...]) + b1_ref[0]
        glu = jnp.minimum(hm[:, :D_FF], SWIGLU_LIMIT)
        lin = jnp.clip(hm[:, D_FF:], -SWIGLU_LIMIT, SWIGLU_LIMIT)
        act = glu * jax.nn.sigmoid(SWIGLU_ALPHA * glu) * (lin + 1.0)
        y = _dot(act.astype(BF16), w2s_ref[...]) + b2_ref[0]
        ya_ref[0:rows, :] = _pack_bf16_pairs(y[:, :D_MODEL // 2])
        yb_ref[0:rows, :] = _pack_bf16_pairs(y[:, D_MODEL // 2:])

    @pl.when(b < nbu_ref[0])
    def _():
        e = be_ref[b]

        @pl.when(b == 0)
        def _():
            for cp in weight_copies(e):
                cp.start()

        @pl.when(jnp.logical_or(b == 0, e != be_ref[jnp.maximum(b - 1, 0)]))
        def _():
            for cp in weight_copies(e):
                cp.wait()
            w1s_ref[...] = w1f_ref[...].astype(BF16)
            w2s_ref[...] = w2f_ref[...].astype(BF16)
            nxt = nxt_ref[e]

            @pl.when(nxt >= 0)
            def _():
                for cp in weight_copies(nxt):
                    cp.start()

        n_sub = (valid_ref[b] + (FFN_SUB_ROWS - 1)) // FFN_SUB_ROWS
        for k in range(1, ROW_BLOCK // FFN_SUB_ROWS + 1):
            pl.when(n_sub == k)(functools.partial(mlp, k * FFN_SUB_ROWS))


SC_WINDOW = 128


def _sc_mesh():
    return plsc.VectorSubcoreMesh(core_axis_name="c", subcore_axis_name="s")


def _sc_gather_rows(table, idx_row):
    n = idx_row.shape[1]
    width = table.shape[1]

    @functools.partial(pl.kernel, out_type=jax.ShapeDtypeStruct((n, width), table.dtype), mesh=_sc_mesh(),
                       scratch_types=[])
    def gather_kernel(t_hbm, i_hbm, o_hbm):
        def body(i_vmem, o_vmem):
            pltpu.sync_copy(t_hbm.at[i_vmem.at[0]], o_vmem)

        pltpu.emit_pipeline(
            body,
            grid=(n // SC_WINDOW,),
            in_specs=[pl.BlockSpec((1, SC_WINDOW), lambda i: (0, i))],
            out_specs=[pl.BlockSpec((SC_WINDOW, width), lambda i: (i, 0))],
            core_axis_name=("c", "s"),
            dimension_semantics=(pltpu.PARALLEL,),
        )(i_hbm, o_hbm)

    return gather_kernel(table, idx_row)


def _sc_scatter_rows(rows, idx_rows, n_out):
    n, width = rows.shape

    @functools.partial(pl.kernel, out_type=jax.ShapeDtypeStruct((n_out, width), rows.dtype), mesh=_sc_mesh(),
                       scratch_types=[])
    def scatter_kernel(r_hbm, *refs):
        i_hbms, o_hbm = refs[:-1], refs[-1]

        def body(r_vmem, *i_vmems):
            for i_vmem in i_vmems:
                pltpu.sync_copy(r_vmem, o_hbm.at[i_vmem.at[0]])

        pltpu.emit_pipeline(
            body,
            grid=(n // SC_WINDOW,),
            in_specs=[pl.BlockSpec((SC_WINDOW, width), lambda i: (i, 0))]
            + [pl.BlockSpec((1, SC_WINDOW), lambda i: (0, i)) for _ in i_hbms],
            out_specs=[],
            core_axis_name=("c", "s"),
            dimension_semantics=(pltpu.PARALLEL,),
        )(r_hbm, *i_hbms)

    return scatter_kernel(rows, *idx_rows)


def _combine_kernel(x1_ref, yg_ref, gate_ref, spread_ref, o_ref):
    q = D_MODEL // 4
    x1 = x1_ref[...]
    acc = [x1[:, :q], x1[:, q:]]
    gfull = lax.dot_general(gate_ref[...], spread_ref[...], (((0,), (0,)), ((), ())), preferred_element_type=F32)
    for k in range(TOP_K):
        parts = _unpack_bf16_pairs(yg_ref[k])
        g = jnp.tile(gfull[:, k * LANES:(k + 1) * LANES], (1, q // LANES))
        acc = [a + g * p for a, p in zip(acc, parts)]
    o_ref[:, :q] = acc[0]
    o_ref[:, q:] = acc[1]


def _combine_into_kernel(x1_ref, yg_ref, gate_ref, spread_ref, prev_ref, o_ref):
    del prev_ref
    _combine_kernel(x1_ref, yg_ref, gate_ref, spread_ref, o_ref)


def _block_diag_ones(width):
    r = jnp.arange(width) // HEAD_GROUP
    return (r[:, None] == r[None, :]).astype(BF16)


def _rope_tables(positions):
    half = ROT_DIM // 2
    inv_freq = ROPE_THETA ** (-jnp.arange(0, ROT_DIM, 2, dtype=F32) / ROT_DIM)
    ang = inv_freq[:, None] * positions.astype(F32).reshape(1, -1)
    cs = jnp.concatenate([jnp.cos(ang), jnp.sin(ang)], axis=0)
    cs_hi = cs.astype(BF16)
    cs_lo = (cs - cs_hi.astype(F32)).astype(BF16)
    lane = jnp.arange(LANES) % HEAD_GROUP
    j = jnp.arange(half)[:, None]
    lo_half = (lane[None, :] == j).astype(F32)
    hi_half = (lane[None, :] == j + half).astype(F32)
    spread = jnp.concatenate([
        jnp.concatenate([lo_half + hi_half, jnp.zeros((half, LANES), F32)], axis=1),
        jnp.concatenate([jnp.zeros((half, LANES), F32), hi_half - lo_half], axis=1)], axis=0)
    return jnp.concatenate([cs_hi, cs_lo], axis=0), jnp.concatenate([spread, spread], axis=0).astype(BF16)


def _full(shape):
    return pl.BlockSpec(shape, lambda *_: (0,) * len(shape))


def kernel(x, mem, positions, g_mix_norm, w_in, g_dq, g_dk, lambda_q1, lambda_k1, lambda_q2, lambda_k2, g_subln, g_sgu, w_spatial, b_spatial, g_mem_norm, w_mem_kv, g_cq, g_ck, w_out, g_ffn_norm, w_router, b_router, w_mlp1, b_mlp1, w_mlp2, b_mlp2):
    B, S, D = x.shape
    M = mem.shape[1]
    tm = TOKEN_TILE
    tr = ROUTER_TILE
    assert D == D_MODEL and S % tm == 0 and S % ATTN_TILE == 0 and g_mix_norm.shape[0] == 1
    last_batches = B // LAST_PART_DIVISOR
    part_batches = (B - last_batches, last_batches) if last_batches > 0 and S % tr == 0 else (B,)

    xf = x.reshape(B * S, D)
    rope_cs, rope_spread = _rope_tables(positions)
    ones256 = _block_diag_ones(MEM_W)
    row = lambda v: v.reshape(1, -1).astype(F32)
    tile_row = lambda v, reps: jnp.tile(v.reshape(1, -1).astype(F32), (1, reps))
    w_in_b, w_out_b, w_kv_b = w_in[0].astype(BF16), w_out[0].astype(BF16), w_mem_kv[0].astype(BF16)
    w_sp_lanes = jnp.transpose(w_spatial[0], (1, 0, 2)).reshape(CHUNK, GMLP_GROUPS * CHUNK)
    b_sp_lanes = jnp.repeat(b_spatial[0].T, HEAD_GROUP, axis=1)
    wr = w_router[0].T.astype(F32)
    wr_hi = wr.astype(BF16)
    wr_lo = (wr - wr_hi.astype(F32)).astype(BF16)
    before = (jnp.arange(tr)[:, None] < jnp.arange(tr)[None, :]).astype(BF16)
    gate_row = jnp.arange(GATE_ROWS)[:, None]
    gate_spread = ((gate_row < 2 * TOP_K) & (gate_row % TOP_K == jnp.arange(TOP_K * LANES)[None, :] // LANES)).astype(BF16)
    b1r = b_mlp1[0].reshape(N_EXPERTS, 1, 2 * D_FF)
    b2r = b_mlp2[0].reshape(N_EXPERTS, 1, D)

    out = None
    b_off = 0
    for Bp in part_batches:
        N = Bp * S
        assert N % tr == 0 and N % tm == 0
        n_assign = N * TOP_K
        n_blocks = -(-n_assign // ROW_BLOCK) + N_EXPERTS
        n_rows = n_blocks * ROW_BLOCK
        nb_pad = -(-n_blocks // LANES) * LANES
        t_off = b_off * S // tm
        r_off = b_off * S // tr

        kT, vm = pl.pallas_call(
            _mem_kv_kernel,
            grid=(Bp,),
            in_specs=[pl.BlockSpec((1, M, D), lambda b: (b + b_off, 0, 0)), _full((1, D)), _full((D, 2 * MEM_W)),
                      _full((1, MEM_W)), _full((MEM_W, MEM_W))],
            out_specs=[pl.BlockSpec((1, MEM_W, M), lambda b: (b, 0, 0)), pl.BlockSpec((1, M, MEM_W), lambda b: (b, 0, 0))],
            out_shape=[jax.ShapeDtypeStruct((Bp, MEM_W, M), BF16), jax.ShapeDtypeStruct((Bp, M, MEM_W), BF16)],
            compiler_params=_cparams(("parallel",)),
            name="mem_kv",
        )(mem, row(g_mem_norm[0]), w_kv_b, tile_row(g_ck[0], N_MEM_HEADS), ones256)

        tiles_per_batch = S // tm
        tok = lambda w: pl.BlockSpec((tm, w), lambda i: (i, 0))
        tok_in = lambda w: pl.BlockSpec((tm, w), lambda i: (i + t_off, 0))
        qn, kn, vv, gc = pl.pallas_call(
            _mixer_in_kernel,
            grid=(N // tm,),
            in_specs=[tok_in(D), pl.BlockSpec((4 * ROT_DIM // 2, tm), lambda i: (0, i + t_off)),
                      _full((4 * ROT_DIM // 2, 2 * LANES)),
                      _full((1, D)), _full((D, D_IN_PROJ)),
                      _full((1, DIFF_W)), _full((1, DIFF_W)),
                      _full((1, GMLP_W)), _full((CHUNK, GMLP_GROUPS * CHUNK)), _full((CHUNK, GMLP_W)),
                      _full((1, MEM_W)), _full((MEM_W, MEM_W)),
                      pl.BlockSpec((1, MEM_W, M), lambda i: (i // tiles_per_batch, 0, 0)),
                      pl.BlockSpec((1, M, MEM_W), lambda i: (i // tiles_per_batch, 0, 0))],
            out_specs=[tok(DIFF_W), tok(DIFF_W), tok(DIFF_W), tok(GMLP_W + MEM_W)],
            out_shape=[jax.ShapeDtypeStruct((N, DIFF_W), BF16)] * 3 + [jax.ShapeDtypeStruct((N, GMLP_W + MEM_W), BF16)],
            compiler_params=_cparams(("parallel",)),
            name="mixer_in",
        )(xf, rope_cs, rope_spread, row(g_mix_norm[0]), w_in_b,
          tile_row(g_dq[0], 2 * N_DIFF_HEADS), tile_row(g_dk[0], 2 * N_DIFF_HEADS),
          row(g_sgu[0]), w_sp_lanes, b_sp_lanes, tile_row(g_cq[0], N_MEM_HEADS), ones256, kT, vm)

        tq = ATTN_TILE
        nq = S // tq
        head_q = pl.BlockSpec((tq, DIFF_W), lambda b, i: (b * nq + i, 0))
        head_kv = pl.BlockSpec((S, DIFF_W), lambda b, i: (b, 0))
        lam_spec = pl.BlockSpec((1, DIFF_HEAD_DIM), lambda b, i: (0, 0))
        nh = N_DIFF_HEADS
        attn = pl.pallas_call(
            _diff_attn_kernel,
            grid=(Bp, nq),
            in_specs=[head_q, head_kv, head_kv, lam_spec, lam_spec, lam_spec, lam_spec,
                      pl.BlockSpec((DIFF_V_DIM, 1), lambda b, i: (0, 0))],
            out_specs=head_q,
            out_shape=jax.ShapeDtypeStruct((N, DIFF_W), BF16),
            scratch_shapes=[pltpu.VMEM((nh, tq, 2 * tq), F32), pltpu.VMEM((nh, tq, 2 * tq), F32),
                            pltpu.VMEM((nh, 1, 2 * tq), F32), pltpu.VMEM((nh, ATTN_EXT_ROWS, 2 * tq), F32),
                            pltpu.VMEM((nh, ATTN_EXT_ROWS, S), BF16)],
            compiler_params=_cparams(("parallel", "arbitrary")),
            name="diff_attn",
        )(qn, kn, vv, row(lambda_q1[0]), row(lambda_k1[0]), row(lambda_q2[0]), row(lambda_k2[0]),
          g_subln[0].reshape(DIFF_V_DIM, 1).astype(F32))

        rtok = lambda w: pl.BlockSpec((tr, w), lambda i: (i, 0))
        rtok_in = lambda w: pl.BlockSpec((tr, w), lambda i: (i + r_off, 0))
        tokT = lambda: pl.BlockSpec((TOP_K, tr), lambda i: (0, i))
        hw = D // 4
        x1, hpa, hpb, idxT, gateT, posT, counts = pl.pallas_call(
            _out_router_kernel,
            grid=(N // tr,),
            in_specs=[rtok_in(D), rtok(DIFF_W), rtok(GMLP_W + MEM_W), _full((D, D)), _full((1, D)),
                      _full((N_EXPERTS, D)), _full((N_EXPERTS, D)), _full((N_EXPERTS, 1)), _full((tr, tr))],
            out_specs=[rtok(D), rtok(hw), rtok(hw), tokT(), pl.BlockSpec((GATE_ROWS, tr), lambda i: (0, i)), tokT(),
                       _full((N_EXPERTS, 1))],
            out_shape=[jax.ShapeDtypeStruct((N, D), F32), jax.ShapeDtypeStruct((N, hw), U32),
                       jax.ShapeDtypeStruct((N, hw), U32), jax.ShapeDtypeStruct((TOP_K, N), I32),
                       jax.ShapeDtypeStruct((GATE_ROWS, N), BF16),
                       jax.ShapeDtypeStruct((TOP_K, N), I32), jax.ShapeDtypeStruct((N_EXPERTS, 1), F32)],
            scratch_shapes=[pltpu.VMEM((N_EXPERTS, 1), F32)],
            compiler_params=_cparams(("arbitrary",)),
            name="out_router",
        )(xf, attn, gc, w_out_b, row(g_ffn_norm[0]), wr_hi, wr_lo, b_router[0].reshape(N_EXPERTS, 1).astype(F32), before)

        destT, block_expert, block_valid, nb_used = pl.pallas_call(
            _dest_kernel,
            grid_spec=pltpu.PrefetchScalarGridSpec(
                num_scalar_prefetch=1,
                grid=(1,),
                in_specs=[pl.BlockSpec((TOP_K, N), lambda i, c: (0, 0)), pl.BlockSpec((TOP_K, N), lambda i, c: (0, 0))],
                out_specs=[pl.BlockSpec((TOP_K, N), lambda i, c: (0, 0)), pl.BlockSpec((1, nb_pad), lambda i, c: (0, 0)),
                           pl.BlockSpec((1, nb_pad), lambda i, c: (0, 0)), pl.BlockSpec((1, LANES), lambda i, c: (0, 0))],
            ),
            out_shape=[jax.ShapeDtypeStruct((TOP_K, N), I32), jax.ShapeDtypeStruct((1, nb_pad), I32),
                       jax.ShapeDtypeStruct((1, nb_pad), I32), jax.ShapeDtypeStruct((1, LANES), I32)],
            compiler_params=_cparams(("arbitrary",)),
            name="dest",
        )(counts.reshape(N_EXPERTS).astype(I32), idxT, posT)

        dest_rows = [destT[k].reshape(1, N) for k in range(TOP_K)]
        xa_buf = _sc_scatter_rows(hpa, dest_rows, n_rows)
        xb_buf = _sc_scatter_rows(hpb, dest_rows, n_rows)

        cnt_i = counts.reshape(N_EXPERTS).astype(I32)
        owner = jnp.where(cnt_i > 0, jnp.arange(N_EXPERTS, dtype=I32), N_EXPERTS)
        later = jnp.concatenate([lax.cummin(owner[::-1])[::-1][1:], jnp.full((1,), N_EXPERTS, I32)])
        next_expert = jnp.where(later < N_EXPERTS, later, -1)
        last = lambda b, be, nbu, nxt, valid: jnp.minimum(b, nbu[0] - 1)
        row_blk = lambda: pl.BlockSpec((ROW_BLOCK, hw), lambda b, be, nbu, nxt, valid: (last(b, be, nbu, nxt, valid), 0))
        ya_buf, yb_buf = pl.pallas_call(
            _expert_ffn_kernel,
            grid_spec=pltpu.PrefetchScalarGridSpec(
                num_scalar_prefetch=4,
                grid=(n_blocks,),
                in_specs=[row_blk(), row_blk(),
                          pl.BlockSpec(memory_space=pl.ANY),
                          pl.BlockSpec((1, 1, 2 * D_FF), lambda b, be, nbu, nxt, valid: (be[b], 0, 0)),
                          pl.BlockSpec(memory_space=pl.ANY),
                          pl.BlockSpec((1, 1, D), lambda b, be, nbu, nxt, valid: (be[b], 0, 0))],
                out_specs=[row_blk(), row_blk()],
                scratch_shapes=[pltpu.VMEM((D, 2 * D_FF), F32), pltpu.VMEM((D_FF, D), F32),
                                pltpu.VMEM((D, 2 * D_FF), BF16), pltpu.VMEM((D_FF, D), BF16),
                                pltpu.SemaphoreType.DMA((2,))],
            ),
            out_shape=[jax.ShapeDtypeStruct((n_rows, hw), U32)] * 2,
            compiler_params=_cparams(("arbitrary",)),
            name="expert_ffn",
        )(block_expert[0, :n_blocks], nb_used[0, :1], next_expert, block_valid[0, :n_blocks], xa_buf, xb_buf,
          w_mlp1[0], b1r, w_mlp2[0], b2r)

        dest_flat = destT.reshape(1, n_assign)
        for half, y_buf in enumerate((ya_buf, yb_buf)):
            yg = _sc_gather_rows(y_buf, dest_flat).reshape(TOP_K, N, hw)
            prev = () if out is None else (out,)
            out = pl.pallas_call(
                _combine_kernel if out is None else _combine_into_kernel,
                grid=(N // tm,),
                in_specs=[pl.BlockSpec((tm, D // 2), lambda i: (i, half)),
                          pl.BlockSpec((TOP_K, tm, hw), lambda i: (0, i, 0)),
                          pl.BlockSpec((GATE_ROWS, tm), lambda i: (0, i)),
                          _full((GATE_ROWS, TOP_K * LANES))] + [pl.BlockSpec(memory_space=pl.ANY)] * len(prev),
                out_specs=pl.BlockSpec((tm, D // 2), lambda i: (i + t_off, half)),
                out_shape=jax.ShapeDtypeStruct((B * S, D), F32),
                input_output_aliases={4: 0} if prev else {},
                compiler_params=_cparams(("parallel",)),
                name="combine",
            )(x1, yg, gateT, gate_spread, *prev)
        b_off += Bp
    return out.reshape(B, S, D)
```

```python
import functools

import jax
import jax.numpy as jnp
from jax import lax
from jax.experimental import pallas as pl
from jax.experimental.pallas import tpu as pltpu
from jax.experimental.pallas import tpu_sc as plsc

F32 = jnp.float32
BF16 = jnp.bfloat16
I32 = jnp.int32
U32 = jnp.uint32

D_MODEL = 1024
N_DIFF_HEADS = 4
DIFF_HEAD_DIM = 64
DIFF_V_DIM = 128
DIFF_W = 512
GMLP_W = 256
GMLP_GROUPS = 4
CHUNK = 128
MEM_W = 256
N_MEM_HEADS = 4
HEAD_GROUP = 64
D_IN_PROJ = 2304
ROPE_THETA = 500000.0
ROT_DIM = 16
N_EXPERTS = 32
TOP_K = 4
D_FF = 1024
SWIGLU_LIMIT = 7.0
SWIGLU_ALPHA = 1.702
EPS = 1e-6
LAMBDA_INIT = 0.8 - 0.6

LANES = 128
ROW_BLOCK = 1024
FFN_SUB_ROWS = 256
TOKEN_TILE = 1024
LAST_PART_DIVISOR = 2
ROUTER_TILE = 1024
GATE_ROWS = 16
ATTN_TILE = 256
ATTN_EXT_ROWS = 128 + 16
VMEM_LIMIT = 56 * 1024 * 1024
NEG_BIG = -1e30
LOG2_E = 1.4426950408889634


def _cparams(sem):
    return pltpu.CompilerParams(dimension_semantics=sem, vmem_limit_bytes=VMEM_LIMIT)


def _dot(a, b):
    return jnp.dot(a, b, preferred_element_type=F32)


def _dot_nt(a, b):
    return lax.dot_general(a, b, (((1,), (1,)), ((), ())), preferred_element_type=F32)


def _rms(x, gain):
    ms = jnp.mean(x * x, axis=-1, keepdims=True)
    return x * lax.rsqrt(ms + EPS) * gain


def _group_rms(t, ones_bd, gain):
    w = ones_bd.shape[0]
    chunks = []
    for j in range(t.shape[1] // w):
        c = t[:, j * w:(j + 1) * w]
        ss = _dot((c * c).astype(BF16), ones_bd)
        chunks.append(c * lax.rsqrt(ss * (1.0 / HEAD_GROUP) + EPS))
    return (chunks[0] if len(chunks) == 1 else jnp.concatenate(chunks, axis=1)) * gain


def _pack_bf16_pairs(v):
    w = v.shape[1] // 2
    bits = lax.bitcast_convert_type(v.astype(BF16).astype(F32), U32)
    return (bits[:, :w] & jnp.uint32(0xFFFF0000)) | (bits[:, w:] >> jnp.uint32(16))


def _unpack_bf16_pairs(words):
    hi = lax.bitcast_convert_type(words & jnp.uint32(0xFFFF0000), F32)
    lo = lax.bitcast_convert_type(words << jnp.uint32(16), F32)
    return hi, lo


def _mem_kv_kernel(mem_ref, gmem_ref, wkv_ref, gck_ref, ones_ref, kT_ref, v_ref):
    m = _rms(mem_ref[0], gmem_ref[...]).astype(BF16)
    kv = _dot(m, wkv_ref[...])
    k = _group_rms(kv[:, :MEM_W], ones_ref[...], gck_ref[...])
    kT_ref[0] = k.T.astype(BF16)
    v_ref[0] = kv[:, MEM_W:].astype(BF16)


def _gelu_tanh(x):
    return 0.5 * x * (1.0 + jnp.tanh(0.7978845608028654 * (x + 0.044715 * (x * x * x))))


def _mixer_in_kernel(x_ref, cs_ref, spread_ref, gmix_ref, win_ref, gq_ref, gk_ref,
                     gsgu_ref, wsp_ref, bsp_ref, gcq_ref, ones256_ref, kT_ref, vm_ref,
                     q_out, k_out, v_out, gc_out):
    tm = x_ref.shape[0]
    hb = _rms(x_ref[...], gmix_ref[...]).astype(BF16)

    def proj(lo, hi):
        return _dot(hb, win_ref[:, lo:hi])

    lane = lax.broadcasted_iota(I32, (tm, LANES), 1)
    first_half = (lane % HEAD_GROUP) < (ROT_DIM // 2)
    tab = lax.dot_general(cs_ref[...], spread_ref[...], (((0,), (0,)), ((), ())), preferred_element_type=F32)
    cosb = tab[:, :LANES] + jnp.where((lane % HEAD_GROUP) >= ROT_DIM, 1.0, 0.0)
    sinb = tab[:, LANES:]

    def norm_rope(t, gain, out_ref):
        tn = _group_rms(t, ones256_ref[...], gain)
        for j in range(DIFF_W // LANES):
            c = tn[:, j * LANES:(j + 1) * LANES]
            partner = jnp.where(first_half, pltpu.roll(c, LANES - ROT_DIM // 2, 1), pltpu.roll(c, ROT_DIM // 2, 1))
            out_ref[:, j * LANES:(j + 1) * LANES] = (c * cosb + partner * sinb).astype(BF16)

    norm_rope(proj(0, DIFF_W), gq_ref[...] * (DIFF_HEAD_DIM ** -0.5 * LOG2_E), q_out)
    norm_rope(proj(DIFF_W, 2 * DIFF_W), gk_ref[...], k_out)
    v_out[...] = proj(2 * DIFF_W, 3 * DIFF_W).astype(BF16)

    z = _gelu_tanh(proj(3 * DIFF_W, 3 * DIFF_W + 2 * GMLP_W))
    u = z[:, :GMLP_W]
    vg = z[:, GMLP_W:]
    vc = vg - jnp.mean(vg, axis=-1, keepdims=True)
    vgn = (vc * lax.rsqrt(jnp.mean(vc * vc, axis=-1, keepdims=True) + EPS) * gsgu_ref[...]).astype(BF16)
    wrow = lax.broadcasted_iota(I32, (CHUNK, GMLP_GROUPS * CHUNK), 0)
    wcol = lax.broadcasted_iota(I32, (CHUNK, GMLP_GROUPS * CHUNK), 1) % CHUNK
    w_causal = jnp.where(wcol <= wrow, wsp_ref[...], 0.0).astype(BF16)
    grp = lax.broadcasted_iota(I32, (CHUNK, GMLP_W), 1) // HEAD_GROUP
    zero_b = jnp.zeros((CHUNK, GMLP_W), BF16)
    for r in range(tm // CHUNK):
        vchunk = vgn[r * CHUNK:(r + 1) * CHUNK, :]
        v_bd = jnp.concatenate([jnp.where(grp == g, vchunk, zero_b) for g in range(GMLP_GROUPS)], axis=0)
        mixed = _dot(w_causal, v_bd) + bsp_ref[...]
        gc_out[r * CHUNK:(r + 1) * CHUNK, 0:GMLP_W] = (u[r * CHUNK:(r + 1) * CHUNK, :] * mixed).astype(BF16)

    pc = proj(3 * DIFF_W + 2 * GMLP_W, D_IN_PROJ)
    qc = _group_rms(pc, ones256_ref[...], gcq_ref[...] * (HEAD_GROUP ** -0.5)).astype(BF16)
    hgrp = lax.broadcasted_iota(I32, (tm, MEM_W), 1) // HEAD_GROUP
    zero_q = jnp.zeros((tm, MEM_W), BF16)
    q_st = jnp.concatenate([jnp.where(hgrp == h, qc, zero_q) for h in range(N_MEM_HEADS)], axis=0)
    s = _dot(q_st, kT_ref[0])
    p = jnp.exp(s - jnp.max(s, axis=-1, keepdims=True))
    o = _dot(p.astype(BF16), vm_ref[0]) / jnp.sum(p, axis=-1, keepdims=True)
    c = jnp.zeros((tm, MEM_W), F32)
    for h in range(N_MEM_HEADS):
        c = c + jnp.where(hgrp == h, o[h * tm:(h + 1) * tm, :], 0.0)
    gc_out[:, GMLP_W:GMLP_W + MEM_W] = c.astype(BF16)


def _diff_attn_kernel(q_ref, k_ref, v_ref, lq1_ref, lk1_ref, lq2_ref, lk2_ref, gsub_ref, o_ref,
                      s0_ref, s1_ref, m_ref, acc_ref, vT_ref):
    tq = q_ref.shape[0]
    seq = k_ref.shape[0]
    i = pl.program_id(1)
    lane = lax.broadcasted_iota(I32, (tq, LANES), 1)
    heads = range(N_DIFF_HEADS)
    hl = lambda h: slice(h * DIFF_V_DIM, (h + 1) * DIFF_V_DIM)
    ext_rows = vT_ref.shape[1]

    @pl.when(i == 0)
    def _():
        ones_row = jnp.where(lax.broadcasted_iota(I32, (ext_rows - DIFF_V_DIM, seq), 0) == 0, 1.0, 0.0).astype(BF16)
        for h in heads:
            for c in range(seq // tq):
                vT_ref[h, 0:DIFF_V_DIM, c * tq:(c + 1) * tq] = v_ref[c * tq:(c + 1) * tq, hl(h)].T
            vT_ref[h, DIFF_V_DIM:ext_rows, :] = ones_row

    def stacked_qT(h):
        q = q_ref[:, hl(h)]
        zero = jnp.zeros_like(q)
        return jnp.concatenate([jnp.where(lane < DIFF_HEAD_DIM, q, zero), jnp.where(lane >= DIFF_HEAD_DIM, q, zero)],
                               axis=0).T

    qsT = [stacked_qT(h) for h in heads]

    def scores(t, s_ref):
        rows = pl.ds(pl.multiple_of(t * tq, tq), tq)
        for h in heads:
            s_ref[h] = _dot(k_ref[rows, hl(h)], qsT[h])

    def update(t, s_ref, causal):
        cols = pl.ds(pl.multiple_of(t * tq, tq), tq)
        for h in heads:
            s = s_ref[h]
            if causal:
                key = lax.broadcasted_iota(I32, (tq, 2 * tq), 0)
                qry = lax.broadcasted_iota(I32, (tq, 2 * tq), 1) % tq
                s = jnp.where(key <= qry, s, NEG_BIG)
            m = m_ref[h]
            m_new = jnp.maximum(m, jnp.max(s, axis=0, keepdims=True))
            alpha = jnp.exp2(m - m_new)
            p = jnp.exp2(s - m_new)
            m_ref[h] = m_new
            acc_ref[h] = alpha * acc_ref[h] + _dot(vT_ref[h, :, cols], p.astype(BF16))

    m_ref[...] = jnp.full(m_ref.shape, NEG_BIG, F32)
    acc_ref[...] = jnp.zeros(acc_ref.shape, F32)
    scores(0, s0_ref)

    def pair(pidx, carry):
        t = 2 * pidx
        scores(t + 1, s1_ref)
        update(t, s0_ref, False)
        scores(t + 2, s0_ref)
        update(t + 1, s1_ref, False)
        return carry

    lax.fori_loop(0, i // 2, pair, 0)

    @pl.when(i % 2 == 0)
    def _():
        update(i, s0_ref, True)

    @pl.when(i % 2 == 1)
    def _():
        scores(i, s1_ref)
        update(i - 1, s0_ref, False)
        update(i, s1_ref, True)

    lam = (jnp.exp(jnp.sum(lq1_ref[...] * lk1_ref[...], axis=-1, keepdims=True))
           - jnp.exp(jnp.sum(lq2_ref[...] * lk2_ref[...], axis=-1, keepdims=True)) + LAMBDA_INIT)
    for h in heads:
        on = acc_ref[h, 0:DIFF_V_DIM, :] * (1.0 / acc_ref[h, DIFF_V_DIM:DIFF_V_DIM + 1, :])
        o = on[:, :tq] - lam * on[:, tq:]
        ms = jnp.mean(o * o, axis=0, keepdims=True)
        o = o * lax.rsqrt(ms + EPS) * gsub_ref[...] * (1.0 - LAMBDA_INIT)
        o_ref[:, hl(h)] = o.T.astype(BF16)


def _out_router_kernel(x_ref, a_ref, gc_ref, wo_ref, gffn_ref, wrh_ref, wrl_ref, br_ref, before_ref,
                       x1_out, hpa_out, hpb_out, idx_out, gate_out, pos_out, cnt_out, carry_ref):
    tm = x_ref.shape[0]

    @pl.when(pl.program_id(0) == 0)
    def _():
        carry_ref[...] = jnp.zeros_like(carry_ref)

    sub = before_ref.shape[0]
    wr_stack = jnp.concatenate([wrh_ref[...], wrl_ref[...]], axis=0)
    eio = lax.broadcasted_iota(I32, (N_EXPERTS, sub), 0)
    before_b = before_ref[...]
    carry = carry_ref[...]
    for r in range(tm // sub):
        rows = slice(r * sub, (r + 1) * sub)
        mix = jnp.concatenate([a_ref[rows, :], gc_ref[rows, :]], axis=1)
        x1 = x_ref[rows, :] + _dot(mix, wo_ref[...])
        x1_out[rows, :] = x1
        h2 = _rms(x1, gffn_ref[...])
        hb = h2.astype(BF16)
        hpa_out[rows, :] = _pack_bf16_pairs(h2[:, :D_MODEL // 2])
        hpb_out[rows, :] = _pack_bf16_pairs(h2[:, D_MODEL // 2:])

        h_lo = (h2 - hb.astype(F32)).astype(BF16)
        both = _dot_nt(wr_stack, hb)
        logits = (both[:N_EXPERTS] + both[N_EXPERTS:]) + _dot_nt(wrh_ref[...], h_lo) + br_ref[...]

        vals, idxs, sels = [], [], []
        cur = logits
        for _ in range(TOP_K):
            m = jnp.max(cur, axis=0, keepdims=True)
            ik = jnp.min(jnp.where(cur == m, eio, N_EXPERTS), axis=0, keepdims=True)
            sel = eio == ik
            cur = jnp.where(sel, -jnp.inf, cur)
            vals.append(m)
            idxs.append(ik)
            sels.append(sel)
        es = [jnp.exp(v - vals[0]) for v in vals]
        tot = es[0] + es[1] + es[2] + es[3]
        gates = jnp.concatenate([e / tot for e in es], axis=0)
        g_hi = gates.astype(BF16)
        g_lo = (gates - g_hi.astype(F32)).astype(BF16)
        gate_out[:, rows] = jnp.concatenate(
            [g_hi, g_lo, jnp.zeros((GATE_ROWS - 2 * TOP_K, sub), BF16)], axis=0)
        idx_out[:, rows] = jnp.concatenate(idxs, axis=0)

        cnt = jnp.zeros((N_EXPERTS, sub), F32)
        for sel in sels:
            cnt = cnt + jnp.where(sel, 1.0, 0.0)
        base = carry + _dot(cnt.astype(BF16), before_b)
        pos_out[:, rows] = jnp.concatenate(
            [jnp.sum(jnp.where(sel, base, 0.0), axis=0, keepdims=True) for sel in sels], axis=0).astype(I32)
        carry = carry + jnp.sum(cnt, axis=1, keepdims=True)
    carry_ref[...] = carry
    cnt_out[...] = carry


def _dest_kernel(cnt_ref, idx_ref, pos_ref, dest_out, be_out, valid_out, nbu_out):
    idx = idx_ref[...]
    dest = pos_ref[...]
    bidx = lax.broadcasted_iota(I32, be_out.shape, 1)
    be = jnp.zeros(be_out.shape, I32)
    valid = jnp.zeros(be_out.shape, I32)
    run = jnp.int32(0)
    for e in range(N_EXPERTS):
        dest = dest + jnp.where(idx == e, run, 0)
        first_block = run // ROW_BLOCK
        run = run + ((cnt_ref[e] + (ROW_BLOCK - 1)) // ROW_BLOCK) * ROW_BLOCK
        mine = jnp.logical_and(bidx >= first_block, bidx < run // ROW_BLOCK)
        valid = jnp.where(mine, jnp.clip(cnt_ref[e] - (bidx - first_block) * ROW_BLOCK, 0, ROW_BLOCK), valid)
        be = be + jnp.where(bidx >= run // ROW_BLOCK, 1, 0)
    dest_out[...] = dest
    be_out[...] = jnp.minimum(be, N_EXPERTS - 1)
    valid_out[...] = valid
    nbu_out[...] = jnp.zeros(nbu_out.shape, I32) + run // ROW_BLOCK


def _expert_ffn_kernel(be_ref, nbu_ref, nxt_ref, valid_ref, xa_ref, xb_ref, w1_hbm, b1_ref, w2_hbm, b2_ref,
                       ya_ref, yb_ref, w1f_ref, w2f_ref, w1s_ref, w2s_ref, sem_ref):
    b = pl.program_id(0)

    def weight_copies(e):
        return (pltpu.make_async_copy(w1_hbm.at[e], w1f_ref, sem_ref.at[0]),
                pltpu.make_async_copy(w2_hbm.at[e], w2f_ref, sem_ref.at[1]))

    def mlp(rows):
        parts = _unpack_bf16_pairs(xa_ref[0:rows, :]) + _unpack_bf16_pairs(xb_ref[0:rows, :])
        xrow = jnp.concatenate([p.astype(BF16) for p in parts], axis=1)
        hm = _dot(xrow, w1s_ref[...]) + b1_ref[0]
        glu = jnp.minimum(hm[:, :D_FF], SWIGLU_LIMIT)
        lin = jnp.clip(hm[:, D_FF:], -SWIGLU_LIMIT, SWIGLU_LIMIT)
        act = glu * jax.nn.sigmoid(SWIGLU_ALPHA * glu) * (lin + 1.0)
        y = _dot(act.astype(BF16), w2s_ref[...]) + b2_ref[0]
        ya_ref[0:rows, :] = _pack_bf16_pairs(y[:, :D_MODEL // 2])
        yb_ref[0:rows, :] = _pack_bf16_pairs(y[:, D_MODEL // 2:])

    @pl.when(b < nbu_ref[0])
    def _():
        e = be_ref[b]

        @pl.when(b == 0)
        def _():
            for cp in weight_copies(e):
                cp.start()

        @pl.when(jnp.logical_or(b == 0, e != be_ref[jnp.maximum(b - 1, 0)]))
        def _():
            for cp in weight_copies(e):
                cp.wait()
            w1s_ref[...] = w1f_ref[...].astype(BF16)
            w2s_ref[...] = w2f_ref[...].astype(BF16)
            nxt = nxt_ref[e]

            @pl.when(nxt >= 0)
            def _():
                for cp in weight_copies(nxt):
                    cp.start()

        n_sub = (valid_ref[b] + (FFN_SUB_ROWS - 1)) // FFN_SUB_ROWS
        for k in range(1, ROW_BLOCK // FFN_SUB_ROWS + 1):
            pl.when(n_sub == k)(functools.partial(mlp, k * FFN_SUB_ROWS))


SC_WINDOW = 128


def _sc_mesh():
    return plsc.VectorSubcoreMesh(core_axis_name="c", subcore_axis_name="s")


def _sc_gather_rows(table, idx_row):
    n = idx_row.shape[1]
    width = table.shape[1]

    @functools.partial(pl.kernel, out_type=jax.ShapeDtypeStruct((n, width), table.dtype), mesh=_sc_mesh(),
                       scratch_types=[])
    def gather_kernel(t_hbm, i_hbm, o_hbm):
        def body(i_vmem, o_vmem):
            pltpu.sync_copy(t_hbm.at[i_vmem.at[0]], o_vmem)

        pltpu.emit_pipeline(
            body,
            grid=(n // SC_WINDOW,),
            in_specs=[pl.BlockSpec((1, SC_WINDOW), lambda i: (0, i))],
            out_specs=[pl.BlockSpec((SC_WINDOW, width), lambda i: (i, 0))],
            core_axis_name=("c", "s"),
            dimension_semantics=(pltpu.PARALLEL,),
        )(i_hbm, o_hbm)

    return gather_kernel(table, idx_row)


def _sc_scatter_rows(rows, idx_rows, n_out):
    n, width = rows.shape

    @functools.partial(pl.kernel, out_type=jax.ShapeDtypeStruct((n_out, width), rows.dtype), mesh=_sc_mesh(),
                       scratch_types=[])
    def scatter_kernel(r_hbm, *refs):
        i_hbms, o_hbm = refs[:-1], refs[-1]

        def body(r_vmem, *i_vmems):
            for i_vmem in i_vmems:
                pltpu.sync_copy(r_vmem, o_hbm.at[i_vmem.at[0]])

        pltpu.emit_pipeline(
            body,
            grid=(n // SC_WINDOW,),
            in_specs=[pl.BlockSpec((SC_WINDOW, width), lambda i: (i, 0))]
            + [pl.BlockSpec((1, SC_WINDOW), lambda i: (0, i)) for _ in i_hbms],
            out_specs=[],
            core_axis_name=("c", "s"),
            dimension_semantics=(pltpu.PARALLEL,),
        )(r_hbm, *i_hbms)

    return scatter_kernel(rows, *idx_rows)


def _combine_kernel(x1_ref, yg_ref, gate_ref, spread_ref, o_ref):
    q = D_MODEL // 4
    x1 = x1_ref[...]
    acc = [x1[:, :q], x1[:, q:]]
    gfull = lax.dot_general(gate_ref[...], spread_ref[...], (((0,), (0,)), ((), ())), preferred_element_type=F32)
    for k in range(TOP_K):
        parts = _unpack_bf16_pairs(yg_ref[k])
        g = jnp.tile(gfull[:, k * LANES:(k + 1) * LANES], (1, q // LANES))
        acc = [a + g * p for a, p in zip(acc, parts)]
    o_ref[:, :q] = acc[0]
    o_ref[:, q:] = acc[1]


def _combine_into_kernel(x1_ref, yg_ref, gate_ref, spread_ref, prev_ref, o_ref):
    del prev_ref
    _combine_kernel(x1_ref, yg_ref, gate_ref, spread_ref, o_ref)


def _block_diag_ones(width):
    r = jnp.arange(width) // HEAD_GROUP
    return (r[:, None] == r[None, :]).astype(BF16)


def _rope_tables(positions):
    half = ROT_DIM // 2
    inv_freq = ROPE_THETA ** (-jnp.arange(0, ROT_DIM, 2, dtype=F32) / ROT_DIM)
    ang = inv_freq[:, None] * positions.astype(F32).reshape(1, -1)
    cs = jnp.concatenate([jnp.cos(ang), jnp.sin(ang)], axis=0)
    cs_hi = cs.astype(BF16)
    cs_lo = (cs - cs_hi.astype(F32)).astype(BF16)
    lane = jnp.arange(LANES) % HEAD_GROUP
    j = jnp.arange(half)[:, None]
    lo_half = (lane[None, :] == j).astype(F32)
    hi_half = (lane[None, :] == j + half).astype(F32)
    spread = jnp.concatenate([
        jnp.concatenate([lo_half + hi_half, jnp.zeros((half, LANES), F32)], axis=1),
        jnp.concatenate([jnp.zeros((half, LANES), F32), hi_half - lo_half], axis=1)], axis=0)
    return jnp.concatenate([cs_hi, cs_lo], axis=0), jnp.concatenate([spread, spread], axis=0).astype(BF16)


def _full(shape):
    return pl.BlockSpec(shape, lambda *_: (0,) * len(shape))


def kernel(x, mem, positions, g_mix_norm, w_in, g_dq, g_dk, lambda_q1, lambda_k1, lambda_q2, lambda_k2, g_subln, g_sgu, w_spatial, b_spatial, g_mem_norm, w_mem_kv, g_cq, g_ck, w_out, g_ffn_norm, w_router, b_router, w_mlp1, b_mlp1, w_mlp2, b_mlp2):
    B, S, D = x.shape
    M = mem.shape[1]
    tm = TOKEN_TILE
    tr = ROUTER_TILE
    assert D == D_MODEL and S % tm == 0 and S % ATTN_TILE == 0 and g_mix_norm.shape[0] == 1
    last_batches = B // LAST_PART_DIVISOR
    part_batches = (B - last_batches, last_batches) if last_batches > 0 and S % tr == 0 else (B,)

    xf = x.reshape(B * S, D)
    rope_cs, rope_spread = _rope_tables(positions)
    ones256 = _block_diag_ones(MEM_W)
    row = lambda v: v.reshape(1, -1).astype(F32)
    tile_row = lambda v, reps: jnp.tile(v.reshape(1, -1).astype(F32), (1, reps))
    w_in_b, w_out_b, w_kv_b = w_in[0].astype(BF16), w_out[0].astype(BF16), w_mem_kv[0].astype(BF16)
    w_sp_lanes = jnp.transpose(w_spatial[0], (1, 0, 2)).reshape(CHUNK, GMLP_GROUPS * CHUNK)
    b_sp_lanes = jnp.repeat(b_spatial[0].T, HEAD_GROUP, axis=1)
    wr = w_router[0].T.astype(F32)
    wr_hi = wr.astype(BF16)
    wr_lo = (wr - wr_hi.astype(F32)).astype(BF16)
    before = (jnp.arange(tr)[:, None] < jnp.arange(tr)[None, :]).astype(BF16)
    gate_row = jnp.arange(GATE_ROWS)[:, None]
    gate_spread = ((gate_row < 2 * TOP_K) & (gate_row % TOP_K == jnp.arange(TOP_K * LANES)[None, :] // LANES)).astype(BF16)
    b1r = b_mlp1[0].reshape(N_EXPERTS, 1, 2 * D_FF)
    b2r = b_mlp2[0].reshape(N_EXPERTS, 1, D)

    out = None
    b_off = 0
    for Bp in part_batches:
        N = Bp * S
        assert N % tr == 0 and N % tm == 0
        n_assign = N * TOP_K
        n_blocks = -(-n_assign // ROW_BLOCK) + N_EXPERTS
        n_rows = n_blocks * ROW_BLOCK
        nb_pad = -(-n_blocks // LANES) * LANES
        t_off = b_off * S // tm
        r_off = b_off * S // tr

        kT, vm = pl.pallas_call(
            _mem_kv_kernel,
            grid=(Bp,),
            in_specs=[pl.BlockSpec((1, M, D), lambda b: (b + b_off, 0, 0)), _full((1, D)), _full((D, 2 * MEM_W)),
                      _full((1, MEM_W)), _full((MEM_W, MEM_W))],
            out_specs=[pl.BlockSpec((1, MEM_W, M), lambda b: (b, 0, 0)), pl.BlockSpec((1, M, MEM_W), lambda b: (b, 0, 0))],
            out_shape=[jax.ShapeDtypeStruct((Bp, MEM_W, M), BF16), jax.ShapeDtypeStruct((Bp, M, MEM_W), BF16)],
            compiler_params=_cparams(("parallel",)),
            name="mem_kv",
        )(mem, row(g_mem_norm[0]), w_kv_b, tile_row(g_ck[0], N_MEM_HEADS), ones256)

        tiles_per_batch = S // tm
        tok = lambda w: pl.BlockSpec((tm, w), lambda i: (i, 0))
        tok_in = lambda w: pl.BlockSpec((tm, w), lambda i: (i + t_off, 0))
        qn, kn, vv, gc = pl.pallas_call(
            _mixer_in_kernel,
            grid=(N // tm,),
            in_specs=[tok_in(D), pl.BlockSpec((4 * ROT_DIM // 2, tm), lambda i: (0, i + t_off)),
                      _full((4 * ROT_DIM // 2, 2 * LANES)),
                      _full((1, D)), _full((D, D_IN_PROJ)),
                      _full((1, DIFF_W)), _full((1, DIFF_W)),
                      _full((1, GMLP_W)), _full((CHUNK, GMLP_GROUPS * CHUNK)), _full((CHUNK, GMLP_W)),
                      _full((1, MEM_W)), _full((MEM_W, MEM_W)),
                      pl.BlockSpec((1, MEM_W, M), lambda i: (i // tiles_per_batch, 0, 0)),
                      pl.BlockSpec((1, M, MEM_W), lambda i: (i // tiles_per_batch, 0, 0))],
            out_specs=[tok(DIFF_W), tok(DIFF_W), tok(DIFF_W), tok(GMLP_W + MEM_W)],
            out_shape=[jax.ShapeDtypeStruct((N, DIFF_W), BF16)] * 3 + [jax.ShapeDtypeStruct((N, GMLP_W + MEM_W), BF16)],
            compiler_params=_cparams(("parallel",)),
            name="mixer_in",
        )(xf, rope_cs, rope_spread, row(g_mix_norm[0]), w_in_b,
          tile_row(g_dq[0], 2 * N_DIFF_HEADS), tile_row(g_dk[0], 2 * N_DIFF_HEADS),
          row(g_sgu[0]), w_sp_lanes, b_sp_lanes, tile_row(g_cq[0], N_MEM_HEADS), ones256, kT, vm)

        tq = ATTN_TILE
        nq = S // tq
        head_q = pl.BlockSpec((tq, DIFF_W), lambda b, i: (b * nq + i, 0))
        head_kv = pl.BlockSpec((S, DIFF_W), lambda b, i: (b, 0))
        lam_spec = pl.BlockSpec((1, DIFF_HEAD_DIM), lambda b, i: (0, 0))
        nh = N_DIFF_HEADS
        attn = pl.pallas_call(
            _diff_attn_kernel,
            grid=(Bp, nq),
            in_specs=[head_q, head_kv, head_kv, lam_spec, lam_spec, lam_spec, lam_spec,
                      pl.BlockSpec((DIFF_V_DIM, 1), lambda b, i: (0, 0))],
            out_specs=head_q,
            out_shape=jax.ShapeDtypeStruct((N, DIFF_W), BF16),
            scratch_shapes=[pltpu.VMEM((nh, tq, 2 * tq), F32), pltpu.VMEM((nh, tq, 2 * tq), F32),
                            pltpu.VMEM((nh, 1, 2 * tq), F32), pltpu.VMEM((nh, ATTN_EXT_ROWS, 2 * tq), F32),
                            pltpu.VMEM((nh, ATTN_EXT_ROWS, S), BF16)],
            compiler_params=_cparams(("parallel", "arbitrary")),
            name="diff_attn",
        )(qn, kn, vv, row(lambda_q1[0]), row(lambda_k1[0]), row(lambda_q2[0]), row(lambda_k2[0]),
          g_subln[0].reshape(DIFF_V_DIM, 1).astype(F32))

        rtok = lambda w: pl.BlockSpec((tr, w), lambda i: (i, 0))
        rtok_in = lambda w: pl.BlockSpec((tr, w), lambda i: (i + r_off, 0))
        tokT = lambda: pl.BlockSpec((TOP_K, tr), lambda i: (0, i))
        hw = D // 4
        x1, hpa, hpb, idxT, gateT, posT, counts = pl.pallas_call(
            _out_router_kernel,
            grid=(N // tr,),
            in_specs=[rtok_in(D), rtok(DIFF_W), rtok(GMLP_W + MEM_W), _full((D, D)), _full((1, D)),
                      _full((N_EXPERTS, D)), _full((N_EXPERTS, D)), _full((N_EXPERTS, 1)), _full((tr, tr))],
            out_specs=[rtok(D), rtok(hw), rtok(hw), tokT(), pl.BlockSpec((GATE_ROWS, tr), lambda i: (0, i)), tokT(),
                       _full((N_EXPERTS, 1))],
            out_shape=[jax.ShapeDtypeStruct((N, D), F32), jax.ShapeDtypeStruct((N, hw), U32),
                       jax.ShapeDtypeStruct((N, hw), U32), jax.ShapeDtypeStruct((TOP_K, N), I32),
                       jax.ShapeDtypeStruct((GATE_ROWS, N), BF16),
                       jax.ShapeDtypeStruct((TOP_K, N), I32), jax.ShapeDtypeStruct((N_EXPERTS, 1), F32)],
            scratch_shapes=[pltpu.VMEM((N_EXPERTS, 1), F32)],
            compiler_params=_cparams(("arbitrary",)),
            name="out_router",
        )(xf, attn, gc, w_out_b, row(g_ffn_norm[0]), wr_hi, wr_lo, b_router[0].reshape(N_EXPERTS, 1).astype(F32), before)

        destT, block_expert, block_valid, nb_used = pl.pallas_call(
            _dest_kernel,
            grid_spec=pltpu.PrefetchScalarGridSpec(
                num_scalar_prefetch=1,
                grid=(1,),
                in_specs=[pl.BlockSpec((TOP_K, N), lambda i, c: (0, 0)), pl.BlockSpec((TOP_K, N), lambda i, c: (0, 0))],
                out_specs=[pl.BlockSpec((TOP_K, N), lambda i, c: (0, 0)), pl.BlockSpec((1, nb_pad), lambda i, c: (0, 0)),
                           pl.BlockSpec((1, nb_pad), lambda i, c: (0, 0)), pl.BlockSpec((1, LANES), lambda i, c: (0, 0))],
            ),
            out_shape=[jax.ShapeDtypeStruct((TOP_K, N), I32), jax.ShapeDtypeStruct((1, nb_pad), I32),
                       jax.ShapeDtypeStruct((1, nb_pad), I32), jax.ShapeDtypeStruct((1, LANES), I32)],
            compiler_params=_cparams(("arbitrary",)),
            name="dest",
        )(counts.reshape(N_EXPERTS).astype(I32), idxT, posT)

        dest_rows = [destT[k].reshape(1, N) for k in range(TOP_K)]
        xa_buf = _sc_scatter_rows(hpa, dest_rows, n_rows)
        xb_buf = _sc_scatter_rows(hpb, dest_rows, n_rows)

        cnt_i = counts.reshape(N_EXPERTS).astype(I32)
        owner = jnp.where(cnt_i > 0, jnp.arange(N_EXPERTS, dtype=I32), N_EXPERTS)
        later = jnp.concatenate([lax.cummin(owner[::-1])[::-1][1:], jnp.full((1,), N_EXPERTS, I32)])
        next_expert = jnp.where(later < N_EXPERTS, later, -1)
        last = lambda b, be, nbu, nxt, valid: jnp.minimum(b, nbu[0] - 1)
        row_blk = lambda: pl.BlockSpec((ROW_BLOCK, hw), lambda b, be, nbu, nxt, valid: (last(b, be, nbu, nxt, valid), 0))
        ya_buf, yb_buf = pl.pallas_call(
            _expert_ffn_kernel,
            grid_spec=pltpu.PrefetchScalarGridSpec(
                num_scalar_prefetch=4,
                grid=(n_blocks,),
                in_specs=[row_blk(), row_blk(),
                          pl.BlockSpec(memory_space=pl.ANY),
                          pl.BlockSpec((1, 1, 2 * D_FF), lambda b, be, nbu, nxt, valid: (be[b], 0, 0)),
                          pl.BlockSpec(memory_space=pl.ANY),
                          pl.BlockSpec((1, 1, D), lambda b, be, nbu, nxt, valid: (be[b], 0, 0))],
                out_specs=[row_blk(), row_blk()],
                scratch_shapes=[pltpu.VMEM((D, 2 * D_FF), F32), pltpu.VMEM((D_FF, D), F32),
                                pltpu.VMEM((D, 2 * D_FF), BF16), pltpu.VMEM((D_FF, D), BF16),
                                pltpu.SemaphoreType.DMA((2,))],
            ),
            out_shape=[jax.ShapeDtypeStruct((n_rows, hw), U32)] * 2,
            compiler_params=_cparams(("arbitrary",)),
            name="expert_ffn",
        )(block_expert[0, :n_blocks], nb_used[0, :1], next_expert, block_valid[0, :n_blocks], xa_buf, xb_buf,
          w_mlp1[0], b1r, w_mlp2[0], b2r)

        dest_flat = destT.reshape(1, n_assign)
        for half, y_buf in enumerate((ya_buf, yb_buf)):
            yg = _sc_gather_rows(y_buf, dest_flat).reshape(TOP_K, N, hw)
            prev = () if out is None else (out,)
            out = pl.pallas_call(
                _combine_kernel if out is None else _combine_into_kernel,
                grid=(N // tm,),
                in_specs=[pl.BlockSpec((tm, D // 2), lambda i: (i, half)),
                          pl.BlockSpec((TOP_K, tm, hw), lambda i: (0, i, 0)),
                          pl.BlockSpec((GATE_ROWS, tm), lambda i: (0, i)),
                          _full((GATE_ROWS, TOP_K * LANES))] + [pl.BlockSpec(memory_space=pl.ANY)] * len(prev),
                out_specs=pl.BlockSpec((tm, D // 2), lambda i: (i + t_off, half)),
                out_shape=jax.ShapeDtypeStruct((B * S, D), F32),
                input_output_aliases={4: 0} if prev else {},
                compiler_params=_cparams(("parallel",)),
                name="combine",
            )(x1, yg, gateT, gate_spread, *prev)
        b_off += Bp
    return out.reshape(B, S, D)
```

```python
import functools

import jax
import jax.numpy as jnp
from jax import lax
from jax.experimental import pallas as pl
from jax.experimental.pallas import tpu as pltpu
from jax.experimental.pallas import tpu_sc as plsc

F32 = jnp.float32
BF16 = jnp.bfloat16
I32 = jnp.int32
U32 = jnp.uint32

D_MODEL = 1024
N_DIFF_HEADS = 4
DIFF_HEAD_DIM = 64
DIFF_V_DIM = 128
DIFF_W = 512
GMLP_W = 256
GMLP_GROUPS = 4
CHUNK = 128
MEM_W = 256
N_MEM_HEADS = 4
HEAD_GROUP = 64
D_IN_PROJ = 2304
ROPE_THETA = 500000.0
ROT_DIM = 16
N_EXPERTS = 32
TOP_K = 4
D_FF = 1024
SWIGLU_LIMIT = 7.0
SWIGLU_ALPHA = 1.702
EPS = 1e-6
LAMBDA_INIT = 0.8 - 0.6

LANES = 128
ROW_BLOCK = 1024
FFN_SUB_ROWS = 256
TOKEN_TILE = 1024
COMBINE_TILE = 512
LAST_PART_DIVISOR = 2
ROUTER_TILE = 1024
GATE_ROWS = 16
ATTN_TILE = 256
ATTN_EXT_ROWS = 128 + 16
VMEM_LIMIT = 56 * 1024 * 1024
NEG_BIG = -1e30
LOG2_E = 1.4426950408889634


def _cparams(sem):
    return pltpu.CompilerParams(dimension_semantics=sem, vmem_limit_bytes=VMEM_LIMIT)


def _dot(a, b):
    return jnp.dot(a, b, preferred_element_type=F32)


def _dot_nt(a, b):
    return lax.dot_general(a, b, (((1,), (1,)), ((), ())), preferred_element_type=F32)


def _rms(x, gain):
    ms = jnp.mean(x * x, axis=-1, keepdims=True)
    return x * lax.rsqrt(ms + EPS) * gain


def _group_rms(t, ones_bd, gain):
    w = ones_bd.shape[0]
    chunks = []
    for j in range(t.shape[1] // w):
        c = t[:, j * w:(j + 1) * w]
        ss = _dot((c * c).astype(BF16), ones_bd)
        chunks.append(c * lax.rsqrt(ss * (1.0 / HEAD_GROUP) + EPS))
    return (chunks[0] if len(chunks) == 1 else jnp.concatenate(chunks, axis=1)) * gain


def _pack_bf16_pairs(v):
    w = v.shape[1] // 2
    bits = lax.bitcast_convert_type(v.astype(BF16).astype(F32), U32)
    return (bits[:, :w] & jnp.uint32(0xFFFF0000)) | (bits[:, w:] >> jnp.uint32(16))


def _unpack_bf16_pairs(words):
    hi = lax.bitcast_convert_type(words & jnp.uint32(0xFFFF0000), F32)
    lo = lax.bitcast_convert_type(words << jnp.uint32(16), F32)
    return hi, lo


def _mem_kv_kernel(mem_ref, gmem_ref, wkv_ref, gck_ref, ones_ref, kT_ref, v_ref):
    m = _rms(mem_ref[0], gmem_ref[...]).astype(BF16)
    kv = _dot(m, wkv_ref[...])
    k = _group_rms(kv[:, :MEM_W], ones_ref[...], gck_ref[...])
    kT_ref[0] = k.T.astype(BF16)
    v_ref[0] = kv[:, MEM_W:].astype(BF16)


def _gelu_tanh(x):
    return 0.5 * x * (1.0 + jnp.tanh(0.7978845608028654 * (x + 0.044715 * (x * x * x))))


def _mixer_in_kernel(x_ref, cs_ref, spread_ref, gmix_ref, win_ref, gq_ref, gk_ref,
                     gsgu_ref, wsp_ref, bsp_ref, gcq_ref, ones256_ref, kT_ref, vm_ref,
                     q_out, k_out, v_out, gc_out):
    tm = x_ref.shape[0]
    hb = _rms(x_ref[...], gmix_ref[...]).astype(BF16)

    def proj(lo, hi):
        return _dot(hb, win_ref[:, lo:hi])

    lane = lax.broadcasted_iota(I32, (tm, LANES), 1)
    first_half = (lane % HEAD_GROUP) < (ROT_DIM // 2)
    tab = lax.dot_general(cs_ref[...], spread_ref[...], (((0,), (0,)), ((), ())), preferred_element_type=F32)
    cosb = tab[:, :LANES] + jnp.where((lane % HEAD_GROUP) >= ROT_DIM, 1.0, 0.0)
    sinb = tab[:, LANES:]

    def norm_rope(t, gain, out_ref):
        tn = _group_rms(t, ones256_ref[...], gain)
        for j in range(DIFF_W // LANES):
            c = tn[:, j * LANES:(j + 1) * LANES]
            partner = jnp.where(first_half, pltpu.roll(c, LANES - ROT_DIM // 2, 1), pltpu.roll(c, ROT_DIM // 2, 1))
            out_ref[:, j * LANES:(j + 1) * LANES] = (c * cosb + partner * sinb).astype(BF16)

    norm_rope(proj(0, DIFF_W), gq_ref[...] * (DIFF_HEAD_DIM ** -0.5 * LOG2_E), q_out)
    norm_rope(proj(DIFF_W, 2 * DIFF_W), gk_ref[...], k_out)
    v_out[...] = proj(2 * DIFF_W, 3 * DIFF_W).astype(BF16)

    z = _gelu_tanh(proj(3 * DIFF_W, 3 * DIFF_W + 2 * GMLP_W))
    u = z[:, :GMLP_W]
    vg = z[:, GMLP_W:]
    vc = vg - jnp.mean(vg, axis=-1, keepdims=True)
    vgn = (vc * lax.rsqrt(jnp.mean(vc * vc, axis=-1, keepdims=True) + EPS) * gsgu_ref[...]).astype(BF16)
    wrow = lax.broadcasted_iota(I32, (CHUNK, GMLP_GROUPS * CHUNK), 0)
    wcol = lax.broadcasted_iota(I32, (CHUNK, GMLP_GROUPS * CHUNK), 1) % CHUNK
    w_causal = jnp.where(wcol <= wrow, wsp_ref[...], 0.0).astype(BF16)
    grp = lax.broadcasted_iota(I32, (CHUNK, GMLP_W), 1) // HEAD_GROUP
    zero_b = jnp.zeros((CHUNK, GMLP_W), BF16)
    for r in range(tm // CHUNK):
        vchunk = vgn[r * CHUNK:(r + 1) * CHUNK, :]
        v_bd = jnp.concatenate([jnp.where(grp == g, vchunk, zero_b) for g in range(GMLP_GROUPS)], axis=0)
        mixed = _dot(w_causal, v_bd) + bsp_ref[...]
        gc_out[r * CHUNK:(r + 1) * CHUNK, 0:GMLP_W] = (u[r * CHUNK:(r + 1) * CHUNK, :] * mixed).astype(BF16)

    pc = proj(3 * DIFF_W + 2 * GMLP_W, D_IN_PROJ)
    qc = _group_rms(pc, ones256_ref[...], gcq_ref[...] * (HEAD_GROUP ** -0.5)).astype(BF16)
    hgrp = lax.broadcasted_iota(I32, (tm, MEM_W), 1) // HEAD_GROUP
    zero_q = jnp.zeros((tm, MEM_W), BF16)
    q_st = jnp.concatenate([jnp.where(hgrp == h, qc, zero_q) for h in range(N_MEM_HEADS)], axis=0)
    s = _dot(q_st, kT_ref[0])
    p = jnp.exp(s - jnp.max(s, axis=-1, keepdims=True))
    o = _dot(p.astype(BF16), vm_ref[0]) / jnp.sum(p, axis=-1, keepdims=True)
    c = jnp.zeros((tm, MEM_W), F32)
    for h in range(N_MEM_HEADS):
        c = c + jnp.where(hgrp == h, o[h * tm:(h + 1) * tm, :], 0.0)
    gc_out[:, GMLP_W:GMLP_W + MEM_W] = c.astype(BF16)


def _diff_attn_kernel(q_ref, k_ref, v_ref, lq1_ref, lk1_ref, lq2_ref, lk2_ref, gsub_ref, o_ref,
                      s0_ref, s1_ref, m_ref, acc_ref, vT_ref):
    tq = q_ref.shape[0]
    seq = k_ref.shape[0]
    i = pl.program_id(1)
    lane = lax.broadcasted_iota(I32, (tq, LANES), 1)
    heads = range(N_DIFF_HEADS)
    hl = lambda h: slice(h * DIFF_V_DIM, (h + 1) * DIFF_V_DIM)
    ext_rows = vT_ref.shape[1]

    @pl.when(i == 0)
    def _():
        ones_row = jnp.where(lax.broadcasted_iota(I32, (ext_rows - DIFF_V_DIM, seq), 0) == 0, 1.0, 0.0).astype(BF16)
        for h in heads:
            for c in range(seq // tq):
                vT_ref[h, 0:DIFF_V_DIM, c * tq:(c + 1) * tq] = v_ref[c * tq:(c + 1) * tq, hl(h)].T
            vT_ref[h, DIFF_V_DIM:ext_rows, :] = ones_row

    def stacked_qT(h):
        q = q_ref[:, hl(h)]
        zero = jnp.zeros_like(q)
        return jnp.concatenate([jnp.where(lane < DIFF_HEAD_DIM, q, zero), jnp.where(lane >= DIFF_HEAD_DIM, q, zero)],
                               axis=0).T

    qsT = [stacked_qT(h) for h in heads]

    def scores(t, s_ref):
        rows = pl.ds(pl.multiple_of(t * tq, tq), tq)
        for h in heads:
            s_ref[h] = _dot(k_ref[rows, hl(h)], qsT[h])

    def update(t, s_ref, causal):
        cols = pl.ds(pl.multiple_of(t * tq, tq), tq)
        for h in heads:
            s = s_ref[h]
            if causal:
                key = lax.broadcasted_iota(I32, (tq, 2 * tq), 0)
                qry = lax.broadcasted_iota(I32, (tq, 2 * tq), 1) % tq
                s = jnp.where(key <= qry, s, NEG_BIG)
            m = m_ref[h]
            m_new = jnp.maximum(m, jnp.max(s, axis=0, keepdims=True))
            alpha = jnp.exp2(m - m_new)
            p = jnp.exp2(s - m_new)
            m_ref[h] = m_new
            acc_ref[h] = alpha * acc_ref[h] + _dot(vT_ref[h, :, cols], p.astype(BF16))

    m_ref[...] = jnp.full(m_ref.shape, NEG_BIG, F32)
    acc_ref[...] = jnp.zeros(acc_ref.shape, F32)
    scores(0, s0_ref)

    def pair(pidx, carry):
        t = 2 * pidx
        scores(t + 1, s1_ref)
        update(t, s0_ref, False)
        scores(t + 2, s0_ref)
        update(t + 1, s1_ref, False)
        return carry

    lax.fori_loop(0, i // 2, pair, 0)

    @pl.when(i % 2 == 0)
    def _():
        update(i, s0_ref, True)

    @pl.when(i % 2 == 1)
    def _():
        scores(i, s1_ref)
        update(i - 1, s0_ref, False)
        update(i, s1_ref, True)

    lam = (jnp.exp(jnp.sum(lq1_ref[...] * lk1_ref[...], axis=-1, keepdims=True))
           - jnp.exp(jnp.sum(lq2_ref[...] * lk2_ref[...], axis=-1, keepdims=True)) + LAMBDA_INIT)
    for h in heads:
        on = acc_ref[h, 0:DIFF_V_DIM, :] * (1.0 / acc_ref[h, DIFF_V_DIM:DIFF_V_DIM + 1, :])
        o = on[:, :tq] - lam * on[:, tq:]
        ms = jnp.mean(o * o, axis=0, keepdims=True)
        o = o * lax.rsqrt(ms + EPS) * gsub_ref[...] * (1.0 - LAMBDA_INIT)
        o_ref[:, hl(h)] = o.T.astype(BF16)


def _out_router_kernel(x_ref, a_ref, gc_ref, wo_ref, gffn_ref, wrh_ref, wrl_ref, br_ref, before_ref,
                       x1_out, hpa_out, hpb_out, idx_out, gate_out, pos_out, cnt_out, carry_ref):
    tm = x_ref.shape[0]

    @pl.when(pl.program_id(0) == 0)
    def _():
        carry_ref[...] = jnp.zeros_like(carry_ref)

    sub = before_ref.shape[0]
    wr_stack = jnp.concatenate([wrh_ref[...], wrl_ref[...]], axis=0)
    eio = lax.broadcasted_iota(I32, (N_EXPERTS, sub), 0)
    before_b = before_ref[...]
    carry = carry_ref[...]
    for r in range(tm // sub):
        rows = slice(r * sub, (r + 1) * sub)
        mix = jnp.concatenate([a_ref[rows, :], gc_ref[rows, :]], axis=1)
        x1 = x_ref[rows, :] + _dot(mix, wo_ref[...])
        x1_out[rows, :] = x1
        h2 = _rms(x1, gffn_ref[...])
        hb = h2.astype(BF16)
        hpa_out[rows, :] = _pack_bf16_pairs(h2[:, :D_MODEL // 2])
        hpb_out[rows, :] = _pack_bf16_pairs(h2[:, D_MODEL // 2:])

        h_lo = (h2 - hb.astype(F32)).astype(BF16)
        both = _dot_nt(wr_stack, hb)
        logits = (both[:N_EXPERTS] + both[N_EXPERTS:]) + _dot_nt(wrh_ref[...], h_lo) + br_ref[...]

        vals, idxs, sels = [], [], []
        cur = logits
        for _ in range(TOP_K):
            m = jnp.max(cur, axis=0, keepdims=True)
            ik = jnp.min(jnp.where(cur == m, eio, N_EXPERTS), axis=0, keepdims=True)
            sel = eio == ik
            cur = jnp.where(sel, -jnp.inf, cur)
            vals.append(m)
            idxs.append(ik)
            sels.append(sel)
        es = [jnp.exp(v - vals[0]) for v in vals]
        tot = es[0] + es[1] + es[2] + es[3]
        gates = jnp.concatenate([e / tot for e in es], axis=0)
        g_hi = gates.astype(BF16)
        g_lo = (gates - g_hi.astype(F32)).astype(BF16)
        gate_out[:, rows] = jnp.concatenate(
            [g_hi, g_lo, jnp.zeros((GATE_ROWS - 2 * TOP_K, sub), BF16)], axis=0)
        idx_out[:, rows] = jnp.concatenate(idxs, axis=0)

        cnt = jnp.zeros((N_EXPERTS, sub), F32)
        for sel in sels:
            cnt = cnt + jnp.where(sel, 1.0, 0.0)
        base = carry + _dot(cnt.astype(BF16), before_b)
        pos_out[:, rows] = jnp.concatenate(
            [jnp.sum(jnp.where(sel, base, 0.0), axis=0, keepdims=True) for sel in sels], axis=0).astype(I32)
        carry = carry + jnp.sum(cnt, axis=1, keepdims=True)
    carry_ref[...] = carry
    cnt_out[...] = carry


def _dest_kernel(cnt_ref, idx_ref, pos_ref, dest_out, be_out, valid_out, nbu_out):
    idx = idx_ref[...]
    dest = pos_ref[...]
    bidx = lax.broadcasted_iota(I32, be_out.shape, 1)
    be = jnp.zeros(be_out.shape, I32)
    valid = jnp.zeros(be_out.shape, I32)
    run = jnp.int32(0)
    for e in range(N_EXPERTS):
        dest = dest + jnp.where(idx == e, run, 0)
        first_block = run // ROW_BLOCK
        run = run + ((cnt_ref[e] + (ROW_BLOCK - 1)) // ROW_BLOCK) * ROW_BLOCK
        mine = jnp.logical_and(bidx >= first_block, bidx < run // ROW_BLOCK)
        valid = jnp.where(mine, jnp.clip(cnt_ref[e] - (bidx - first_block) * ROW_BLOCK, 0, ROW_BLOCK), valid)
        be = be + jnp.where(bidx >= run // ROW_BLOCK, 1, 0)
    dest_out[...] = dest
    be_out[...] = jnp.minimum(be, N_EXPERTS - 1)
    valid_out[...] = valid
    nbu_out[...] = jnp.zeros(nbu_out.shape, I32) + run // ROW_BLOCK


def _expert_ffn_kernel(be_ref, nbu_ref, nxt_ref, valid_ref, xa_ref, xb_ref, w1_hbm, b1_ref, w2_hbm, b2_ref,
                       y_ref, w1f_ref, w2f_ref, w1s_ref, w2s_ref, sem_ref):
    b = pl.program_id(0)

    def weight_copies(e):
        return (pltpu.make_async_copy(w1_hbm.at[e], w1f_ref, sem_ref.at[0]),
                pltpu.make_async_copy(w2_hbm.at[e], w2f_ref, sem_ref.at[1]))

    def mlp(rows):
        parts = _unpack_bf16_pairs(xa_ref[0:rows, :]) + _unpack_bf16_pairs(xb_ref[0:rows, :])
        xrow = jnp.concatenate([p.astype(BF16) for p in parts], axis=1)
        hm = _dot(xrow, w1s_ref[...]) + b1_ref[0]
        glu = jnp.minimum(hm[:, :D_FF], SWIGLU_LIMIT)
        lin = jnp.clip(hm[:, D_FF:], -SWIGLU_LIMIT, SWIGLU_LIMIT)
        act = glu * jax.nn.sigmoid(SWIGLU_ALPHA * glu) * (lin + 1.0)
        y = _dot(act.astype(BF16), w2s_ref[...]) + b2_ref[0]
        hw = D_MODEL // 4
        y_ref[0:rows, 0:hw] = _pack_bf16_pairs(y[:, :D_MODEL // 2])
        y_ref[0:rows, hw:] = _pack_bf16_pairs(y[:, D_MODEL // 2:])

    @pl.when(b < nbu_ref[0])
    def _():
        e = be_ref[b]

        @pl.when(b == 0)
        def _():
            for cp in weight_copies(e):
                cp.start()

        @pl.when(jnp.logical_or(b == 0, e != be_ref[jnp.maximum(b - 1, 0)]))
        def _():
            for cp in weight_copies(e):
                cp.wait()
            w1s_ref[...] = w1f_ref[...].astype(BF16)
            w2s_ref[...] = w2f_ref[...].astype(BF16)
            nxt = nxt_ref[e]

            @pl.when(nxt >= 0)
            def _():
                for cp in weight_copies(nxt):
                    cp.start()

        n_sub = (valid_ref[b] + (FFN_SUB_ROWS - 1)) // FFN_SUB_ROWS
        for k in range(1, ROW_BLOCK // FFN_SUB_ROWS + 1):
            pl.when(n_sub == k)(functools.partial(mlp, k * FFN_SUB_ROWS))


SC_WINDOW = 128
SC_GATHER_WINDOW = 64


def _sc_mesh():
    return plsc.VectorSubcoreMesh(core_axis_name="c", subcore_axis_name="s")


def _sc_gather_rows(table, idx):
    n = idx.shape[0]
    width = table.shape[1]
    win = SC_GATHER_WINDOW

    @functools.partial(pl.kernel, out_type=jax.ShapeDtypeStruct((n, width), table.dtype), mesh=_sc_mesh(),
                       scratch_types=[])
    def gather_kernel(t_hbm, i_hbm, o_hbm):
        def body(i_vmem, o_vmem):
            pltpu.sync_copy(t_hbm.at[i_vmem.at[0]], o_vmem)

        pltpu.emit_pipeline(
            body,
            grid=(n // win,),
            in_specs=[pl.BlockSpec((1, win), lambda i: (i, 0))],
            out_specs=[pl.BlockSpec((win, width), lambda i: (i, 0))],
            core_axis_name=("c", "s"),
            dimension_semantics=(pltpu.PARALLEL,),
        )(i_hbm, o_hbm)

    return gather_kernel(table, idx.reshape(n // win, win))


def _sc_scatter_rows(rows, idx_rows, n_out):
    n, width = rows.shape

    @functools.partial(pl.kernel, out_type=jax.ShapeDtypeStruct((n_out, width), rows.dtype), mesh=_sc_mesh(),
                       scratch_types=[])
    def scatter_kernel(r_hbm, *refs):
        i_hbms, o_hbm = refs[:-1], refs[-1]

        def body(r_vmem, *i_vmems):
            for i_vmem in i_vmems:
                pltpu.sync_copy(r_vmem, o_hbm.at[i_vmem.at[0]])

        pltpu.emit_pipeline(
            body,
            grid=(n // SC_WINDOW,),
            in_specs=[pl.BlockSpec((SC_WINDOW, width), lambda i: (i, 0))]
            + [pl.BlockSpec((1, SC_WINDOW), lambda i: (0, i)) for _ in i_hbms],
            out_specs=[],
            core_axis_name=("c", "s"),
            dimension_semantics=(pltpu.PARALLEL,),
        )(r_hbm, *i_hbms)

    return scatter_kernel(rows, *idx_rows)


def _combine_kernel(x1_ref, yg_ref, gate_ref, spread_ref, o_ref):
    q = D_MODEL // 4
    x1 = x1_ref[...]
    acc = [x1[:, j * q:(j + 1) * q] for j in range(4)]
    gfull = lax.dot_general(gate_ref[...], spread_ref[...], (((0,), (0,)), ((), ())), preferred_element_type=F32)
    for k in range(TOP_K):
        parts = _unpack_bf16_pairs(yg_ref[k, :, 0:q]) + _unpack_bf16_pairs(yg_ref[k, :, q:])
        g = jnp.tile(gfull[:, k * LANES:(k + 1) * LANES], (1, q // LANES))
        acc = [a + g * p for a, p in zip(acc, parts)]
    for j in range(4):
        o_ref[:, j * q:(j + 1) * q] = acc[j]


def _combine_into_kernel(x1_ref, yg_ref, gate_ref, spread_ref, prev_ref, o_ref):
    del prev_ref
    _combine_kernel(x1_ref, yg_ref, gate_ref, spread_ref, o_ref)


def _block_diag_ones(width):
    r = jnp.arange(width) // HEAD_GROUP
    return (r[:, None] == r[None, :]).astype(BF16)


def _rope_tables(positions):
    half = ROT_DIM // 2
    inv_freq = ROPE_THETA ** (-jnp.arange(0, ROT_DIM, 2, dtype=F32) / ROT_DIM)
    ang = inv_freq[:, None] * positions.astype(F32).reshape(1, -1)
    cs = jnp.concatenate([jnp.cos(ang), jnp.sin(ang)], axis=0)
    cs_hi = cs.astype(BF16)
    cs_lo = (cs - cs_hi.astype(F32)).astype(BF16)
    lane = jnp.arange(LANES) % HEAD_GROUP
    j = jnp.arange(half)[:, None]
    lo_half = (lane[None, :] == j).astype(F32)
    hi_half = (lane[None, :] == j + half).astype(F32)
    spread = jnp.concatenate([
        jnp.concatenate([lo_half + hi_half, jnp.zeros((half, LANES), F32)], axis=1),
        jnp.concatenate([jnp.zeros((half, LANES), F32), hi_half - lo_half], axis=1)], axis=0)
    return jnp.concatenate([cs_hi, cs_lo], axis=0), jnp.concatenate([spread, spread], axis=0).astype(BF16)


def _full(shape):
    return pl.BlockSpec(shape, lambda *_: (0,) * len(shape))


def kernel(x, mem, positions, g_mix_norm, w_in, g_dq, g_dk, lambda_q1, lambda_k1, lambda_q2, lambda_k2, g_subln, g_sgu, w_spatial, b_spatial, g_mem_norm, w_mem_kv, g_cq, g_ck, w_out, g_ffn_norm, w_router, b_router, w_mlp1, b_mlp1, w_mlp2, b_mlp2):
    B, S, D = x.shape
    M = mem.shape[1]
    tm = TOKEN_TILE
    tr = ROUTER_TILE
    assert D == D_MODEL and S % tm == 0 and S % ATTN_TILE == 0 and g_mix_norm.shape[0] == 1
    last_batches = B // LAST_PART_DIVISOR
    part_batches = (B - last_batches, last_batches) if last_batches > 0 and S % tr == 0 else (B,)

    xf = x.reshape(B * S, D)
    rope_cs, rope_spread = _rope_tables(positions)
    ones256 = _block_diag_ones(MEM_W)
    row = lambda v: v.reshape(1, -1).astype(F32)
    tile_row = lambda v, reps: jnp.tile(v.reshape(1, -1).astype(F32), (1, reps))
    w_in_b, w_out_b, w_kv_b = w_in[0].astype(BF16), w_out[0].astype(BF16), w_mem_kv[0].astype(BF16)
    w_sp_lanes = jnp.transpose(w_spatial[0], (1, 0, 2)).reshape(CHUNK, GMLP_GROUPS * CHUNK)
    b_sp_lanes = jnp.repeat(b_spatial[0].T, HEAD_GROUP, axis=1)
    wr = w_router[0].T.astype(F32)
    wr_hi = wr.astype(BF16)
    wr_lo = (wr - wr_hi.astype(F32)).astype(BF16)
    before = (jnp.arange(tr)[:, None] < jnp.arange(tr)[None, :]).astype(BF16)
    gate_row = jnp.arange(GATE_ROWS)[:, None]
    gate_spread = ((gate_row < 2 * TOP_K) & (gate_row % TOP_K == jnp.arange(TOP_K * LANES)[None, :] // LANES)).astype(BF16)
    b1r = b_mlp1[0].reshape(N_EXPERTS, 1, 2 * D_FF)
    b2r = b_mlp2[0].reshape(N_EXPERTS, 1, D)

    out = None
    b_off = 0
    for Bp in part_batches:
        N = Bp * S
        assert N % tr == 0 and N % tm == 0
        n_assign = N * TOP_K
        n_blocks = -(-n_assign // ROW_BLOCK) + N_EXPERTS
        n_rows = n_blocks * ROW_BLOCK
        nb_pad = -(-n_blocks // LANES) * LANES
        t_off = b_off * S // tm
        r_off = b_off * S // tr

        kT, vm = pl.pallas_call(
            _mem_kv_kernel,
            grid=(Bp,),
            in_specs=[pl.BlockSpec((1, M, D), lambda b: (b + b_off, 0, 0)), _full((1, D)), _full((D, 2 * MEM_W)),
                      _full((1, MEM_W)), _full((MEM_W, MEM_W))],
            out_specs=[pl.BlockSpec((1, MEM_W, M), lambda b: (b, 0, 0)), pl.BlockSpec((1, M, MEM_W), lambda b: (b, 0, 0))],
            out_shape=[jax.ShapeDtypeStruct((Bp, MEM_W, M), BF16), jax.ShapeDtypeStruct((Bp, M, MEM_W), BF16)],
            compiler_params=_cparams(("parallel",)),
            name="mem_kv",
        )(mem, row(g_mem_norm[0]), w_kv_b, tile_row(g_ck[0], N_MEM_HEADS), ones256)

        tiles_per_batch = S // tm
        tok = lambda w: pl.BlockSpec((tm, w), lambda i: (i, 0))
        tok_in = lambda w: pl.BlockSpec((tm, w), lambda i: (i + t_off, 0))
        qn, kn, vv, gc = pl.pallas_call(
            _mixer_in_kernel,
            grid=(N // tm,),
            in_specs=[tok_in(D), pl.BlockSpec((4 * ROT_DIM // 2, tm), lambda i: (0, i + t_off)),
                      _full((4 * ROT_DIM // 2, 2 * LANES)),
                      _full((1, D)), _full((D, D_IN_PROJ)),
                      _full((1, DIFF_W)), _full((1, DIFF_W)),
                      _full((1, GMLP_W)), _full((CHUNK, GMLP_GROUPS * CHUNK)), _full((CHUNK, GMLP_W)),
                      _full((1, MEM_W)), _full((MEM_W, MEM_W)),
                      pl.BlockSpec((1, MEM_W, M), lambda i: (i // tiles_per_batch, 0, 0)),
                      pl.BlockSpec((1, M, MEM_W), lambda i: (i // tiles_per_batch, 0, 0))],
            out_specs=[tok(DIFF_W), tok(DIFF_W), tok(DIFF_W), tok(GMLP_W + MEM_W)],
            out_shape=[jax.ShapeDtypeStruct((N, DIFF_W), BF16)] * 3 + [jax.ShapeDtypeStruct((N, GMLP_W + MEM_W), BF16)],
            compiler_params=_cparams(("parallel",)),
            name="mixer_in",
        )(xf, rope_cs, rope_spread, row(g_mix_norm[0]), w_in_b,
          tile_row(g_dq[0], 2 * N_DIFF_HEADS), tile_row(g_dk[0], 2 * N_DIFF_HEADS),
          row(g_sgu[0]), w_sp_lanes, b_sp_lanes, tile_row(g_cq[0], N_MEM_HEADS), ones256, kT, vm)

        tq = ATTN_TILE
        nq = S // tq
        head_q = pl.BlockSpec((tq, DIFF_W), lambda b, i: (b * nq + i, 0))
        head_kv = pl.BlockSpec((S, DIFF_W), lambda b, i: (b, 0))
        lam_spec = pl.BlockSpec((1, DIFF_HEAD_DIM), lambda b, i: (0, 0))
        nh = N_DIFF_HEADS
        attn = pl.pallas_call(
            _diff_attn_kernel,
            grid=(Bp, nq),
            in_specs=[head_q, head_kv, head_kv, lam_spec, lam_spec, lam_spec, lam_spec,
                      pl.BlockSpec((DIFF_V_DIM, 1), lambda b, i: (0, 0))],
            out_specs=head_q,
            out_shape=jax.ShapeDtypeStruct((N, DIFF_W), BF16),
            scratch_shapes=[pltpu.VMEM((nh, tq, 2 * tq), F32), pltpu.VMEM((nh, tq, 2 * tq), F32),
                            pltpu.VMEM((nh, 1, 2 * tq), F32), pltpu.VMEM((nh, ATTN_EXT_ROWS, 2 * tq), F32),
                            pltpu.VMEM((nh, ATTN_EXT_ROWS, S), BF16)],
            compiler_params=_cparams(("parallel", "arbitrary")),
            name="diff_attn",
        )(qn, kn, vv, row(lambda_q1[0]), row(lambda_k1[0]), row(lambda_q2[0]), row(lambda_k2[0]),
          g_subln[0].reshape(DIFF_V_DIM, 1).astype(F32))

        rtok = lambda w: pl.BlockSpec((tr, w), lambda i: (i, 0))
        rtok_in = lambda w: pl.BlockSpec((tr, w), lambda i: (i + r_off, 0))
        tokT = lambda: pl.BlockSpec((TOP_K, tr), lambda i: (0, i))
        hw = D // 4
        x1, hpa, hpb, idxT, gateT, posT, counts = pl.pallas_call(
            _out_router_kernel,
            grid=(N // tr,),
            in_specs=[rtok_in(D), rtok(DIFF_W), rtok(GMLP_W + MEM_W), _full((D, D)), _full((1, D)),
                      _full((N_EXPERTS, D)), _full((N_EXPERTS, D)), _full((N_EXPERTS, 1)), _full((tr, tr))],
            out_specs=[rtok(D), rtok(hw), rtok(hw), tokT(), pl.BlockSpec((GATE_ROWS, tr), lambda i: (0, i)), tokT(),
                       _full((N_EXPERTS, 1))],
            out_shape=[jax.ShapeDtypeStruct((N, D), F32), jax.ShapeDtypeStruct((N, hw), U32),
                       jax.ShapeDtypeStruct((N, hw), U32), jax.ShapeDtypeStruct((TOP_K, N), I32),
                       jax.ShapeDtypeStruct((GATE_ROWS, N), BF16),
                       jax.ShapeDtypeStruct((TOP_K, N), I32), jax.ShapeDtypeStruct((N_EXPERTS, 1), F32)],
            scratch_shapes=[pltpu.VMEM((N_EXPERTS, 1), F32)],
            compiler_params=_cparams(("arbitrary",)),
            name="out_router",
        )(xf, attn, gc, w_out_b, row(g_ffn_norm[0]), wr_hi, wr_lo, b_router[0].reshape(N_EXPERTS, 1).astype(F32), before)

        destT, block_expert, block_valid, nb_used = pl.pallas_call(
            _dest_kernel,
            grid_spec=pltpu.PrefetchScalarGridSpec(
                num_scalar_prefetch=1,
                grid=(1,),
                in_specs=[pl.BlockSpec((TOP_K, N), lambda i, c: (0, 0)), pl.BlockSpec((TOP_K, N), lambda i, c: (0, 0))],
                out_specs=[pl.BlockSpec((TOP_K, N), lambda i, c: (0, 0)), pl.BlockSpec((1, nb_pad), lambda i, c: (0, 0)),
                           pl.BlockSpec((1, nb_pad), lambda i, c: (0, 0)), pl.BlockSpec((1, LANES), lambda i, c: (0, 0))],
            ),
            out_shape=[jax.ShapeDtypeStruct((TOP_K, N), I32), jax.ShapeDtypeStruct((1, nb_pad), I32),
                       jax.ShapeDtypeStruct((1, nb_pad), I32), jax.ShapeDtypeStruct((1, LANES), I32)],
            compiler_params=_cparams(("arbitrary",)),
            name="dest",
        )(counts.reshape(N_EXPERTS).astype(I32), idxT, posT)

        dest_rows = [destT[k].reshape(1, N) for k in range(TOP_K)]
        xa_buf = _sc_scatter_rows(hpa, dest_rows, n_rows)
        xb_buf = _sc_scatter_rows(hpb, dest_rows, n_rows)

        cnt_i = counts.reshape(N_EXPERTS).astype(I32)
        owner = jnp.where(cnt_i > 0, jnp.arange(N_EXPERTS, dtype=I32), N_EXPERTS)
        later = jnp.concatenate([lax.cummin(owner[::-1])[::-1][1:], jnp.full((1,), N_EXPERTS, I32)])
        next_expert = jnp.where(later < N_EXPERTS, later, -1)
        last = lambda b, be, nbu, nxt, valid: jnp.minimum(b, nbu[0] - 1)
        row_blk = lambda: pl.BlockSpec((ROW_BLOCK, hw), lambda b, be, nbu, nxt, valid: (last(b, be, nbu, nxt, valid), 0))
        y_blk = pl.BlockSpec((ROW_BLOCK, 2 * hw), lambda b, be, nbu, nxt, valid: (last(b, be, nbu, nxt, valid), 0))
        y_buf = pl.pallas_call(
            _expert_ffn_kernel,
            grid_spec=pltpu.PrefetchScalarGridSpec(
                num_scalar_prefetch=4,
                grid=(n_blocks,),
                in_specs=[row_blk(), row_blk(),
                          pl.BlockSpec(memory_space=pl.ANY),
                          pl.BlockSpec((1, 1, 2 * D_FF), lambda b, be, nbu, nxt, valid: (be[b], 0, 0)),
                          pl.BlockSpec(memory_space=pl.ANY),
                          pl.BlockSpec((1, 1, D), lambda b, be, nbu, nxt, valid: (be[b], 0, 0))],
                out_specs=y_blk,
                scratch_shapes=[pltpu.VMEM((D, 2 * D_FF), F32), pltpu.VMEM((D_FF, D), F32),
                                pltpu.VMEM((D, 2 * D_FF), BF16), pltpu.VMEM((D_FF, D), BF16),
                                pltpu.SemaphoreType.DMA((2,))],
            ),
            out_shape=jax.ShapeDtypeStruct((n_rows, 2 * hw), U32),
            compiler_params=_cparams(("arbitrary",)),
            name="expert_ffn",
        )(block_expert[0, :n_blocks], nb_used[0, :1], next_expert, block_valid[0, :n_blocks], xa_buf, xb_buf,
          w_mlp1[0], b1r, w_mlp2[0], b2r)

        yg = _sc_gather_rows(y_buf, destT.reshape(n_assign)).reshape(TOP_K, N, 2 * hw)
        tc = COMBINE_TILE
        prev = () if out is None else (out,)
        out = pl.pallas_call(
            _combine_kernel if out is None else _combine_into_kernel,
            grid=(N // tc,),
            in_specs=[pl.BlockSpec((tc, D), lambda i: (i, 0)),
                      pl.BlockSpec((TOP_K, tc, 2 * hw), lambda i: (0, i, 0)),
                      pl.BlockSpec((GATE_ROWS, tc), lambda i: (0, i)),
                      _full((GATE_ROWS, TOP_K * LANES))] + [pl.BlockSpec(memory_space=pl.ANY)] * len(prev),
            out_specs=pl.BlockSpec((tc, D), lambda i: (i + b_off * S // tc, 0)),
            out_shape=jax.ShapeDtypeStruct((B * S, D), F32),
            input_output_aliases={4: 0} if prev else {},
            compiler_params=_cparams(("parallel",)),
            name="combine",
        )(x1, yg, gateT, gate_spread, *prev)
        b_off += Bp
    return out.reshape(B, S, D)
```

```python
import functools

import jax
import jax.numpy as jnp
from jax import lax
from jax.experimental import pallas as pl
from jax.experimental.pallas import tpu as pltpu
from jax.experimental.pallas import tpu_sc as plsc

F32 = jnp.float32
BF16 = jnp.bfloat16
I32 = jnp.int32
U32 = jnp.uint32

D_MODEL = 1024
N_DIFF_HEADS = 4
DIFF_HEAD_DIM = 64
DIFF_V_DIM = 128
DIFF_W = 512
GMLP_W = 256
GMLP_GROUPS = 4
CHUNK = 128
MEM_W = 256
N_MEM_HEADS = 4
HEAD_GROUP = 64
D_IN_PROJ = 2304
ROPE_THETA = 500000.0
ROT_DIM = 16
ROPE_STACK_ROWS = 2 * ROT_DIM
N_EXPERTS = 32
TOP_K = 4
D_FF = 1024
SWIGLU_LIMIT = 7.0
SWIGLU_ALPHA = 1.702
EPS = 1e-6
LAMBDA_INIT = 0.8 - 0.6

LANES = 128
ROW_BLOCK = 1024
FFN_SUB_ROWS = 256
TOKEN_TILE = 1024
COMBINE_TILE = 512
MEM_BATCHES = 4
LAST_PART_DIVISOR = 2
ROUTER_TILE = 1024
GATE_ROWS = 16
ATTN_TILE = 256
ATTN_EXT_ROWS = 128 + 16
VMEM_LIMIT = 56 * 1024 * 1024
NEG_BIG = -1e30
LOG2_E = 1.4426950408889634


def _cparams(sem):
    return pltpu.CompilerParams(dimension_semantics=sem, vmem_limit_bytes=VMEM_LIMIT)


def _dot(a, b):
    return jnp.dot(a, b, preferred_element_type=F32)


def _dot_nt(a, b):
    return lax.dot_general(a, b, (((1,), (1,)), ((), ())), preferred_element_type=F32)


def _rms(x, gain):
    ms = jnp.mean(x * x, axis=-1, keepdims=True)
    return x * lax.rsqrt(ms + EPS) * gain


def _group_rms(t, ones_bd, gain):
    w = ones_bd.shape[0]
    chunks = []
    for j in range(t.shape[1] // w):
        c = t[:, j * w:(j + 1) * w]
        ss = _dot((c * c).astype(BF16), ones_bd)
        chunks.append(c * lax.rsqrt(ss * (1.0 / HEAD_GROUP) + EPS))
    return (chunks[0] if len(chunks) == 1 else jnp.concatenate(chunks, axis=1)) * gain


def _pack_bf16_pairs(v):
    w = v.shape[1] // 2
    bits = lax.bitcast_convert_type(v.astype(BF16).astype(F32), U32)
    return (bits[:, :w] & jnp.uint32(0xFFFF0000)) | (bits[:, w:] >> jnp.uint32(16))


def _unpack_bf16_pairs(words):
    hi = lax.bitcast_convert_type(words & jnp.uint32(0xFFFF0000), F32)
    lo = lax.bitcast_convert_type(words << jnp.uint32(16), F32)
    return hi, lo


def _mem_kv_kernel(mem_ref, gmem_ref, wkv_ref, gck_ref, ones_ref, kT_ref, v_ref):
    for j in range(mem_ref.shape[0]):
        m = _rms(mem_ref[j], gmem_ref[...]).astype(BF16)
        kv = _dot(m, wkv_ref[...])
        k = _group_rms(kv[:, :MEM_W], ones_ref[...], gck_ref[...])
        kT_ref[j] = k.T.astype(BF16)
        v_ref[j] = kv[:, MEM_W:].astype(BF16)


def _gelu_tanh(x):
    return 0.5 * x * (1.0 + jnp.tanh(0.7978845608028654 * (x + 0.044715 * (x * x * x))))


def _mixer_in_kernel(x_ref, cs_ref, spread_ref, gmix_ref, win_ref, gq_ref, gk_ref,
                     gsgu_ref, wsp_ref, bsp_ref, gcq_ref, ones256_ref, kT_ref, vm_ref,
                     q_out, k_out, v_out, gc_out):
    tm = x_ref.shape[0]
    hb = _rms(x_ref[...], gmix_ref[...]).astype(BF16)

    def proj(lo, hi):
        return _dot(hb, win_ref[:, lo:hi])

    lane = lax.broadcasted_iota(I32, (tm, LANES), 1)
    first_half = (lane % HEAD_GROUP) < (ROT_DIM // 2)
    tab = lax.dot_general(cs_ref[...], spread_ref[...], (((0,), (0,)), ((), ())), preferred_element_type=F32)
    cosb = tab[:, :LANES] + jnp.where((lane % HEAD_GROUP) >= ROT_DIM, 1.0, 0.0)
    sinb = tab[:, LANES:]

    def norm_rope(t, gain, out_ref):
        tn = _group_rms(t, ones256_ref[...], gain)
        for j in range(DIFF_W // LANES):
            c = tn[:, j * LANES:(j + 1) * LANES]
            partner = jnp.where(first_half, pltpu.roll(c, LANES - ROT_DIM // 2, 1), pltpu.roll(c, ROT_DIM // 2, 1))
            out_ref[:, j * LANES:(j + 1) * LANES] = (c * cosb + partner * sinb).astype(BF16)

    norm_rope(proj(0, DIFF_W), gq_ref[...] * (DIFF_HEAD_DIM ** -0.5 * LOG2_E), q_out)
    norm_rope(proj(DIFF_W, 2 * DIFF_W), gk_ref[...], k_out)
    v_out[...] = proj(2 * DIFF_W, 3 * DIFF_W).astype(BF16)

    z = _gelu_tanh(proj(3 * DIFF_W, 3 * DIFF_W + 2 * GMLP_W))
    u = z[:, :GMLP_W]
    vg = z[:, GMLP_W:]
    vc = vg - jnp.mean(vg, axis=-1, keepdims=True)
    vgn = (vc * lax.rsqrt(jnp.mean(vc * vc, axis=-1, keepdims=True) + EPS) * gsgu_ref[...]).astype(BF16)
    wrow = lax.broadcasted_iota(I32, (CHUNK, GMLP_GROUPS * CHUNK), 0)
    wcol = lax.broadcasted_iota(I32, (CHUNK, GMLP_GROUPS * CHUNK), 1) % CHUNK
    w_causal = jnp.where(wcol <= wrow, wsp_ref[...], 0.0).astype(BF16)
    grp = lax.broadcasted_iota(I32, (CHUNK, GMLP_W), 1) // HEAD_GROUP
    zero_b = jnp.zeros((CHUNK, GMLP_W), BF16)
    for r in range(tm // CHUNK):
        vchunk = vgn[r * CHUNK:(r + 1) * CHUNK, :]
        v_bd = jnp.concatenate([jnp.where(grp == g, vchunk, zero_b) for g in range(GMLP_GROUPS)], axis=0)
        mixed = _dot(w_causal, v_bd) + bsp_ref[...]
        gc_out[r * CHUNK:(r + 1) * CHUNK, 0:GMLP_W] = (u[r * CHUNK:(r + 1) * CHUNK, :] * mixed).astype(BF16)

    pc = proj(3 * DIFF_W + 2 * GMLP_W, D_IN_PROJ)
    qc = _group_rms(pc, ones256_ref[...], gcq_ref[...] * (HEAD_GROUP ** -0.5)).astype(BF16)
    hgrp = lax.broadcasted_iota(I32, (tm, MEM_W), 1) // HEAD_GROUP
    zero_q = jnp.zeros((tm, MEM_W), BF16)
    q_st = jnp.concatenate([jnp.where(hgrp == h, qc, zero_q) for h in range(N_MEM_HEADS)], axis=0)
    s = _dot(q_st, kT_ref[0])
    p = jnp.exp(s - jnp.max(s, axis=-1, keepdims=True))
    o = _dot(p.astype(BF16), vm_ref[0]) / jnp.sum(p, axis=-1, keepdims=True)
    c = jnp.zeros((tm, MEM_W), F32)
    for h in range(N_MEM_HEADS):
        c = c + jnp.where(hgrp == h, o[h * tm:(h + 1) * tm, :], 0.0)
    gc_out[:, GMLP_W:GMLP_W + MEM_W] = c.astype(BF16)


def _diff_attn_kernel(q_ref, k_ref, v_ref, lq1_ref, lk1_ref, lq2_ref, lk2_ref, gsub_ref, o_ref,
                      s0_ref, s1_ref, m_ref, acc_ref, vT_ref):
    tq = q_ref.shape[0]
    seq = k_ref.shape[0]
    i = pl.program_id(1)
    lane = lax.broadcasted_iota(I32, (tq, LANES), 1)
    heads = range(N_DIFF_HEADS)
    hl = lambda h: slice(h * DIFF_V_DIM, (h + 1) * DIFF_V_DIM)
    ext_rows = vT_ref.shape[1]

    @pl.when(i == 0)
    def _():
        ones_row = jnp.where(lax.broadcasted_iota(I32, (ext_rows - DIFF_V_DIM, seq), 0) == 0, 1.0, 0.0).astype(BF16)
        for h in heads:
            for c in range(seq // tq):
                vT_ref[h, 0:DIFF_V_DIM, c * tq:(c + 1) * tq] = v_ref[c * tq:(c + 1) * tq, hl(h)].T
            vT_ref[h, DIFF_V_DIM:ext_rows, :] = ones_row

    def stacked_qT(h):
        q = q_ref[:, hl(h)]
        zero = jnp.zeros_like(q)
        return jnp.concatenate([jnp.where(lane < DIFF_HEAD_DIM, q, zero), jnp.where(lane >= DIFF_HEAD_DIM, q, zero)],
                               axis=0).T

    qsT = [stacked_qT(h) for h in heads]

    def scores(t, s_ref):
        rows = pl.ds(pl.multiple_of(t * tq, tq), tq)
        for h in heads:
            s_ref[h] = _dot(k_ref[rows, hl(h)], qsT[h])

    def update(t, s_ref, causal):
        cols = pl.ds(pl.multiple_of(t * tq, tq), tq)
        for h in heads:
            s = s_ref[h]
            if causal:
                key = lax.broadcasted_iota(I32, (tq, 2 * tq), 0)
                qry = lax.broadcasted_iota(I32, (tq, 2 * tq), 1) % tq
                s = jnp.where(key <= qry, s, NEG_BIG)
            m = m_ref[h]
            m_new = jnp.maximum(m, jnp.max(s, axis=0, keepdims=True))
            alpha = jnp.exp2(m - m_new)
            p = jnp.exp2(s - m_new)
            m_ref[h] = m_new
            acc_ref[h] = alpha * acc_ref[h] + _dot(vT_ref[h, :, cols], p.astype(BF16))

    m_ref[...] = jnp.full(m_ref.shape, NEG_BIG, F32)
    acc_ref[...] = jnp.zeros(acc_ref.shape, F32)
    scores(0, s0_ref)

    def pair(pidx, carry):
        t = 2 * pidx
        scores(t + 1, s1_ref)
        update(t, s0_ref, False)
        scores(t + 2, s0_ref)
        update(t + 1, s1_ref, False)
        return carry

    lax.fori_loop(0, i // 2, pair, 0)

    @pl.when(i % 2 == 0)
    def _():
        update(i, s0_ref, True)

    @pl.when(i % 2 == 1)
    def _():
        scores(i, s1_ref)
        update(i - 1, s0_ref, False)
        update(i, s1_ref, True)

    lam = (jnp.exp(jnp.sum(lq1_ref[...] * lk1_ref[...], axis=-1, keepdims=True))
           - jnp.exp(jnp.sum(lq2_ref[...] * lk2_ref[...], axis=-1, keepdims=True)) + LAMBDA_INIT)
    for h in heads:
        on = acc_ref[h, 0:DIFF_V_DIM, :] * (1.0 / acc_ref[h, DIFF_V_DIM:DIFF_V_DIM + 1, :])
        o = on[:, :tq] - lam * on[:, tq:]
        ms = jnp.mean(o * o, axis=0, keepdims=True)
        o = o * lax.rsqrt(ms + EPS) * gsub_ref[...] * (1.0 - LAMBDA_INIT)
        o_ref[:, hl(h)] = o.T.astype(BF16)


def _out_router_kernel(x_ref, a_ref, gc_ref, wo_ref, gffn_ref, wrh_ref, wrl_ref, br_ref, before_ref,
                       x1_out, hpa_out, hpb_out, idx_out, gate_out, pos_out, cnt_out, carry_ref):
    tm = x_ref.shape[0]

    @pl.when(pl.program_id(0) == 0)
    def _():
        carry_ref[...] = jnp.zeros_like(carry_ref)

    sub = before_ref.shape[0]
    wr_stack = jnp.concatenate([wrh_ref[...], wrl_ref[...]], axis=0)
    eio = lax.broadcasted_iota(I32, (N_EXPERTS, sub), 0)
    before_b = before_ref[...]
    carry = carry_ref[...]
    for r in range(tm // sub):
        rows = slice(r * sub, (r + 1) * sub)
        mix = jnp.concatenate([a_ref[rows, :], gc_ref[rows, :]], axis=1)
        x1 = x_ref[rows, :] + _dot(mix, wo_ref[...])
        x1_out[rows, :] = x1
        h2 = _rms(x1, gffn_ref[...])
        hb = h2.astype(BF16)
        hpa_out[rows, :] = _pack_bf16_pairs(h2[:, :D_MODEL // 2])
        hpb_out[rows, :] = _pack_bf16_pairs(h2[:, D_MODEL // 2:])

        h_lo = (h2 - hb.astype(F32)).astype(BF16)
        both = _dot_nt(wr_stack, hb)
        logits = (both[:N_EXPERTS] + both[N_EXPERTS:]) + _dot_nt(wrh_ref[...], h_lo) + br_ref[...]

        vals, idxs, sels = [], [], []
        cur = logits
        for _ in range(TOP_K):
            m = jnp.max(cur, axis=0, keepdims=True)
            ik = jnp.min(jnp.where(cur == m, eio, N_EXPERTS), axis=0, keepdims=True)
            sel = eio == ik
            cur = jnp.where(sel, -jnp.inf, cur)
            vals.append(m)
            idxs.append(ik)
            sels.append(sel)
        es = [jnp.exp(v - vals[0]) for v in vals]
        tot = es[0] + es[1] + es[2] + es[3]
        gates = jnp.concatenate([e / tot for e in es], axis=0)
        g_hi = gates.astype(BF16)
        g_lo = (gates - g_hi.astype(F32)).astype(BF16)
        gate_out[:, rows] = jnp.concatenate(
            [g_hi, g_lo, jnp.zeros((GATE_ROWS - 2 * TOP_K, sub), BF16)], axis=0)
        idx_out[:, rows] = jnp.concatenate(idxs, axis=0)

        cnt = jnp.zeros((N_EXPERTS, sub), F32)
        for sel in sels:
            cnt = cnt + jnp.where(sel, 1.0, 0.0)
        base = carry + _dot(cnt.astype(BF16), before_b)
        pos_out[:, rows] = jnp.concatenate(
            [jnp.sum(jnp.where(sel, base, 0.0), axis=0, keepdims=True) for sel in sels], axis=0).astype(I32)
        carry = carry + jnp.sum(cnt, axis=1, keepdims=True)
    carry_ref[...] = carry
    cnt_out[...] = carry


def _dest_kernel(cnt_ref, idx_ref, pos_ref, dest_out, be_out, valid_out, nbu_out):
    idx = idx_ref[...]
    dest = pos_ref[...]
    bidx = lax.broadcasted_iota(I32, be_out.shape, 1)
    be = jnp.zeros(be_out.shape, I32)
    valid = jnp.zeros(be_out.shape, I32)
    run = jnp.int32(0)
    for e in range(N_EXPERTS):
        dest = dest + jnp.where(idx == e, run, 0)
        first_block = run // ROW_BLOCK
        run = run + ((cnt_ref[e] + (ROW_BLOCK - 1)) // ROW_BLOCK) * ROW_BLOCK
        mine = jnp.logical_and(bidx >= first_block, bidx < run // ROW_BLOCK)
        valid = jnp.where(mine, jnp.clip(cnt_ref[e] - (bidx - first_block) * ROW_BLOCK, 0, ROW_BLOCK), valid)
        be = be + jnp.where(bidx >= run // ROW_BLOCK, 1, 0)
    dest_out[...] = dest
    be_out[...] = jnp.minimum(be, N_EXPERTS - 1)
    valid_out[...] = valid
    nbu_out[...] = jnp.zeros(nbu_out.shape, I32) + run // ROW_BLOCK


def _expert_ffn_kernel(be_ref, nbu_ref, nxt_ref, valid_ref, xa_ref, xb_ref, w1_hbm, b1_ref, w2_hbm, b2_ref,
                       y_ref, w1f_ref, w2f_ref, w1s_ref, w2s_ref, sem_ref):
    b = pl.program_id(0)

    def weight_copies(e):
        return (pltpu.make_async_copy(w1_hbm.at[e], w1f_ref, sem_ref.at[0]),
                pltpu.make_async_copy(w2_hbm.at[e], w2f_ref, sem_ref.at[1]))

    def mlp(rows):
        parts = _unpack_bf16_pairs(xa_ref[0:rows, :]) + _unpack_bf16_pairs(xb_ref[0:rows, :])
        xrow = jnp.concatenate([p.astype(BF16) for p in parts], axis=1)
        hm = _dot(xrow, w1s_ref[...]) + b1_ref[0]
        glu = jnp.minimum(hm[:, :D_FF], SWIGLU_LIMIT)
        lin = jnp.clip(hm[:, D_FF:], -SWIGLU_LIMIT, SWIGLU_LIMIT)
        act = glu * jax.nn.sigmoid(SWIGLU_ALPHA * glu) * (lin + 1.0)
        y = _dot(act.astype(BF16), w2s_ref[...]) + b2_ref[0]
        hw = D_MODEL // 4
        y_ref[0:rows, 0:hw] = _pack_bf16_pairs(y[:, :D_MODEL // 2])
        y_ref[0:rows, hw:] = _pack_bf16_pairs(y[:, D_MODEL // 2:])

    @pl.when(b < nbu_ref[0])
    def _():
        e = be_ref[b]

        @pl.when(b == 0)
        def _():
            for cp in weight_copies(e):
                cp.start()

        @pl.when(jnp.logical_or(b == 0, e != be_ref[jnp.maximum(b - 1, 0)]))
        def _():
            for cp in weight_copies(e):
                cp.wait()
            w1s_ref[...] = w1f_ref[...].astype(BF16)
            w2s_ref[...] = w2f_ref[...].astype(BF16)
            nxt = nxt_ref[e]

            @pl.when(nxt >= 0)
            def _():
                for cp in weight_copies(nxt):
                    cp.start()

        n_sub = (valid_ref[b] + (FFN_SUB_ROWS - 1)) // FFN_SUB_ROWS
        for k in range(1, ROW_BLOCK // FFN_SUB_ROWS + 1):
            pl.when(n_sub == k)(functools.partial(mlp, k * FFN_SUB_ROWS))


SC_WINDOW = 128
SC_GATHER_WINDOW = 64


def _sc_mesh():
    return plsc.VectorSubcoreMesh(core_axis_name="c", subcore_axis_name="s")


def _sc_gather_rows(table, idx):
    n = idx.shape[0]
    width = table.shape[1]
    win = SC_GATHER_WINDOW

    @functools.partial(pl.kernel, out_type=jax.ShapeDtypeStruct((n, width), table.dtype), mesh=_sc_mesh(),
                       scratch_types=[])
    def gather_kernel(t_hbm, i_hbm, o_hbm):
        def body(i_vmem, o_vmem):
            pltpu.sync_copy(t_hbm.at[i_vmem.at[0]], o_vmem)

        pltpu.emit_pipeline(
            body,
            grid=(n // win,),
            in_specs=[pl.BlockSpec((1, win), lambda i: (i, 0))],
            out_specs=[pl.BlockSpec((win, width), lambda i: (i, 0))],
            core_axis_name=("c", "s"),
            dimension_semantics=(pltpu.PARALLEL,),
        )(i_hbm, o_hbm)

    return gather_kernel(table, idx.reshape(n // win, win))


def _sc_scatter_rows(rows, idx_rows, n_out):
    n, width = rows.shape

    @functools.partial(pl.kernel, out_type=jax.ShapeDtypeStruct((n_out, width), rows.dtype), mesh=_sc_mesh(),
                       scratch_types=[])
    def scatter_kernel(r_hbm, *refs):
        i_hbms, o_hbm = refs[:-1], refs[-1]

        def body(r_vmem, *i_vmems):
            for i_vmem in i_vmems:
                pltpu.sync_copy(r_vmem, o_hbm.at[i_vmem.at[0]])

        pltpu.emit_pipeline(
            body,
            grid=(n // SC_WINDOW,),
            in_specs=[pl.BlockSpec((SC_WINDOW, width), lambda i: (i, 0))]
            + [pl.BlockSpec((1, SC_WINDOW), lambda i: (0, i)) for _ in i_hbms],
            out_specs=[],
            core_axis_name=("c", "s"),
            dimension_semantics=(pltpu.PARALLEL,),
        )(r_hbm, *i_hbms)

    return scatter_kernel(rows, *idx_rows)


def _combine_kernel(x1_ref, yg_ref, gate_ref, spread_ref, o_ref):
    q = D_MODEL // 4
    x1 = x1_ref[...]
    acc = [x1[:, j * q:(j + 1) * q] for j in range(4)]
    gfull = lax.dot_general(gate_ref[...], spread_ref[...], (((0,), (0,)), ((), ())), preferred_element_type=F32)
    for k in range(TOP_K):
        parts = _unpack_bf16_pairs(yg_ref[k, :, 0:q]) + _unpack_bf16_pairs(yg_ref[k, :, q:])
        g = jnp.tile(gfull[:, k * LANES:(k + 1) * LANES], (1, q // LANES))
        acc = [a + g * p for a, p in zip(acc, parts)]
    for j in range(4):
        o_ref[:, j * q:(j + 1) * q] = acc[j]


def _combine_into_kernel(x1_ref, yg_ref, gate_ref, spread_ref, prev_ref, o_ref):
    del prev_ref
    _combine_kernel(x1_ref, yg_ref, gate_ref, spread_ref, o_ref)


def _block_diag_ones(width):
    r = jnp.arange(width) // HEAD_GROUP
    return (r[:, None] == r[None, :]).astype(BF16)


def _rope_tables(positions):
    half = ROT_DIM // 2
    inv_freq = ROPE_THETA ** (-jnp.arange(0, ROT_DIM, 2, dtype=F32) / ROT_DIM)
    ang = inv_freq[:, None] * positions.astype(F32).reshape(1, -1)
    cs = jnp.concatenate([jnp.cos(ang), jnp.sin(ang)], axis=0)
    cs_hi = cs.astype(BF16)
    cs_lo = (cs - cs_hi.astype(F32)).astype(BF16)
    lane = jnp.arange(LANES) % HEAD_GROUP
    j = jnp.arange(half)[:, None]
    lo_half = (lane[None, :] == j).astype(F32)
    hi_half = (lane[None, :] == j + half).astype(F32)
    spread = jnp.concatenate([
        jnp.concatenate([lo_half + hi_half, jnp.zeros((half, LANES), F32)], axis=1),
        jnp.concatenate([jnp.zeros((half, LANES), F32), hi_half - lo_half], axis=1)], axis=0)
    return jnp.concatenate([cs_hi, cs_lo], axis=0), jnp.concatenate([spread, spread], axis=0).astype(BF16)


def _full(shape):
    return pl.BlockSpec(shape, lambda *_: (0,) * len(shape))


def kernel(x, mem, positions, g_mix_norm, w_in, g_dq, g_dk, lambda_q1, lambda_k1, lambda_q2, lambda_k2, g_subln, g_sgu, w_spatial, b_spatial, g_mem_norm, w_mem_kv, g_cq, g_ck, w_out, g_ffn_norm, w_router, b_router, w_mlp1, b_mlp1, w_mlp2, b_mlp2):
    B, S, D = x.shape
    M = mem.shape[1]
    tm = TOKEN_TILE
    tr = ROUTER_TILE
    assert D == D_MODEL and S % tm == 0 and S % ATTN_TILE == 0 and g_mix_norm.shape[0] == 1
    last_batches = B // LAST_PART_DIVISOR
    part_batches = (B - last_batches, last_batches) if last_batches > 0 and S % tr == 0 else (B,)

    xf = x.reshape(B * S, D)
    rope_cs, rope_spread = _rope_tables(positions)
    ones256 = _block_diag_ones(MEM_W)
    row = lambda v: v.reshape(1, -1).astype(F32)
    tile_row = lambda v, reps: jnp.tile(v.reshape(1, -1).astype(F32), (1, reps))
    w_in_b, w_out_b, w_kv_b = w_in[0].astype(BF16), w_out[0].astype(BF16), w_mem_kv[0].astype(BF16)
    w_sp_lanes = jnp.transpose(w_spatial[0], (1, 0, 2)).reshape(CHUNK, GMLP_GROUPS * CHUNK)
    b_sp_lanes = jnp.repeat(b_spatial[0].T, HEAD_GROUP, axis=1)
    wr = w_router[0].T.astype(F32)
    wr_hi = wr.astype(BF16)
    wr_lo = (wr - wr_hi.astype(F32)).astype(BF16)
    before = (jnp.arange(tr)[:, None] < jnp.arange(tr)[None, :]).astype(BF16)
    gate_row = jnp.arange(GATE_ROWS)[:, None]
    gate_spread = ((gate_row < 2 * TOP_K) & (gate_row % TOP_K == jnp.arange(TOP_K * LANES)[None, :] // LANES)).astype(BF16)
    b1r = b_mlp1[0].reshape(N_EXPERTS, 1, 2 * D_FF)
    b2r = b_mlp2[0].reshape(N_EXPERTS, 1, D)

    out = None
    b_off = 0
    for Bp in part_batches:
        N = Bp * S
        assert N % tr == 0 and N % tm == 0
        n_assign = N * TOP_K
        n_blocks = -(-n_assign // ROW_BLOCK) + N_EXPERTS
        n_rows = n_blocks * ROW_BLOCK
        nb_pad = -(-n_blocks // LANES) * LANES
        t_off = b_off * S // tm
        r_off = b_off * S // tr

        mb = MEM_BATCHES if Bp % MEM_BATCHES == 0 and b_off % MEM_BATCHES == 0 else 1
        kT, vm = pl.pallas_call(
            _mem_kv_kernel,
            grid=(Bp // mb,),
            in_specs=[pl.BlockSpec((mb, M, D), lambda b: (b + b_off // mb, 0, 0)), _full((1, D)), _full((D, 2 * MEM_W)),
                      _full((1, MEM_W)), _full((MEM_W, MEM_W))],
            out_specs=[pl.BlockSpec((mb, MEM_W, M), lambda b: (b, 0, 0)),
                       pl.BlockSpec((mb, M, MEM_W), lambda b: (b, 0, 0))],
            out_shape=[jax.ShapeDtypeStruct((Bp, MEM_W, M), BF16), jax.ShapeDtypeStruct((Bp, M, MEM_W), BF16)],
            compiler_params=_cparams(("parallel",)),
            name="mem_kv",
        )(mem, row(g_mem_norm[0]), w_kv_b, tile_row(g_ck[0], N_MEM_HEADS), ones256)

        tiles_per_batch = S // tm
        tok = lambda w: pl.BlockSpec((tm, w), lambda i: (i, 0))
        tok_in = lambda w: pl.BlockSpec((tm, w), lambda i: (i + t_off, 0))
        qn, kn, vv, gc = pl.pallas_call(
            _mixer_in_kernel,
            grid=(N // tm,),
            in_specs=[tok_in(D), pl.BlockSpec((ROPE_STACK_ROWS, tm), lambda i: (0, i + t_off)),
                      _full((ROPE_STACK_ROWS, 2 * LANES)),
                      _full((1, D)), _full((D, D_IN_PROJ)),
                      _full((1, DIFF_W)), _full((1, DIFF_W)),
                      _full((1, GMLP_W)), _full((CHUNK, GMLP_GROUPS * CHUNK)), _full((CHUNK, GMLP_W)),
                      _full((1, MEM_W)), _full((MEM_W, MEM_W)),
                      pl.BlockSpec((1, MEM_W, M), lambda i: (i // tiles_per_batch, 0, 0)),
                      pl.BlockSpec((1, M, MEM_W), lambda i: (i // tiles_per_batch, 0, 0))],
            out_specs=[tok(DIFF_W), tok(DIFF_W), tok(DIFF_W), tok(GMLP_W + MEM_W)],
            out_shape=[jax.ShapeDtypeStruct((N, DIFF_W), BF16)] * 3 + [jax.ShapeDtypeStruct((N, GMLP_W + MEM_W), BF16)],
            compiler_params=_cparams(("parallel",)),
            name="mixer_in",
        )(xf, rope_cs, rope_spread, row(g_mix_norm[0]), w_in_b,
          tile_row(g_dq[0], 2 * N_DIFF_HEADS), tile_row(g_dk[0], 2 * N_DIFF_HEADS),
          row(g_sgu[0]), w_sp_lanes, b_sp_lanes, tile_row(g_cq[0], N_MEM_HEADS), ones256, kT, vm)

        tq = ATTN_TILE
        nq = S // tq
        head_q = pl.BlockSpec((tq, DIFF_W), lambda b, i: (b * nq + i, 0))
        head_kv = pl.BlockSpec((S, DIFF_W), lambda b, i: (b, 0))
        lam_spec = pl.BlockSpec((1, DIFF_HEAD_DIM), lambda b, i: (0, 0))
        nh = N_DIFF_HEADS
        attn = pl.pallas_call(
            _diff_attn_kernel,
            grid=(Bp, nq),
            in_specs=[head_q, head_kv, head_kv, lam_spec, lam_spec, lam_spec, lam_spec,
                      pl.BlockSpec((DIFF_V_DIM, 1), lambda b, i: (0, 0))],
            out_specs=head_q,
            out_shape=jax.ShapeDtypeStruct((N, DIFF_W), BF16),
            scratch_shapes=[pltpu.VMEM((nh, tq, 2 * tq), F32), pltpu.VMEM((nh, tq, 2 * tq), F32),
                            pltpu.VMEM((nh, 1, 2 * tq), F32), pltpu.VMEM((nh, ATTN_EXT_ROWS, 2 * tq), F32),
                            pltpu.VMEM((nh, ATTN_EXT_ROWS, S), BF16)],
            compiler_params=_cparams(("parallel", "arbitrary")),
            name="diff_attn",
        )(qn, kn, vv, row(lambda_q1[0]), row(lambda_k1[0]), row(lambda_q2[0]), row(lambda_k2[0]),
          g_subln[0].reshape(DIFF_V_DIM, 1).astype(F32))

        rtok = lambda w: pl.BlockSpec((tr, w), lambda i: (i, 0))
        rtok_in = lambda w: pl.BlockSpec((tr, w), lambda i: (i + r_off, 0))
        tokT = lambda: pl.BlockSpec((TOP_K, tr), lambda i: (0, i))
        hw = D // 4
        x1, hpa, hpb, idxT, gateT, posT, counts = pl.pallas_call(
            _out_router_kernel,
            grid=(N // tr,),
            in_specs=[rtok_in(D), rtok(DIFF_W), rtok(GMLP_W + MEM_W), _full((D, D)), _full((1, D)),
                      _full((N_EXPERTS, D)), _full((N_EXPERTS, D)), _full((N_EXPERTS, 1)), _full((tr, tr))],
            out_specs=[rtok(D), rtok(hw), rtok(hw), tokT(), pl.BlockSpec((GATE_ROWS, tr), lambda i: (0, i)), tokT(),
                       _full((N_EXPERTS, 1))],
            out_shape=[jax.ShapeDtypeStruct((N, D), F32), jax.ShapeDtypeStruct((N, hw), U32),
                       jax.ShapeDtypeStruct((N, hw), U32), jax.ShapeDtypeStruct((TOP_K, N), I32),
                       jax.ShapeDtypeStruct((GATE_ROWS, N), BF16),
                       jax.ShapeDtypeStruct((TOP_K, N), I32), jax.ShapeDtypeStruct((N_EXPERTS, 1), F32)],
            scratch_shapes=[pltpu.VMEM((N_EXPERTS, 1), F32)],
            compiler_params=_cparams(("arbitrary",)),
            name="out_router",
        )(xf, attn, gc, w_out_b, row(g_ffn_norm[0]), wr_hi, wr_lo, b_router[0].reshape(N_EXPERTS, 1).astype(F32), before)

        destT, block_expert, block_valid, nb_used = pl.pallas_call(
            _dest_kernel,
            grid_spec=pltpu.PrefetchScalarGridSpec(
                num_scalar_prefetch=1,
                grid=(1,),
                in_specs=[pl.BlockSpec((TOP_K, N), lambda i, c: (0, 0)), pl.BlockSpec((TOP_K, N), lambda i, c: (0, 0))],
                out_specs=[pl.BlockSpec((TOP_K, N), lambda i, c: (0, 0)), pl.BlockSpec((1, nb_pad), lambda i, c: (0, 0)),
                           pl.BlockSpec((1, nb_pad), lambda i, c: (0, 0)), pl.BlockSpec((1, LANES), lambda i, c: (0, 0))],
            ),
            out_shape=[jax.ShapeDtypeStruct((TOP_K, N), I32), jax.ShapeDtypeStruct((1, nb_pad), I32),
                       jax.ShapeDtypeStruct((1, nb_pad), I32), jax.ShapeDtypeStruct((1, LANES), I32)],
            compiler_params=_cparams(("arbitrary",)),
            name="dest",
        )(counts.reshape(N_EXPERTS).astype(I32), idxT, posT)

        dest_rows = [destT[k].reshape(1, N) for k in range(TOP_K)]
        xa_buf = _sc_scatter_rows(hpa, dest_rows, n_rows)
        xb_buf = _sc_scatter_rows(hpb, dest_rows, n_rows)

        cnt_i = counts.reshape(N_EXPERTS).astype(I32)
        owner = jnp.where(cnt_i > 0, jnp.arange(N_EXPERTS, dtype=I32), N_EXPERTS)
        later = jnp.concatenate([lax.cummin(owner[::-1])[::-1][1:], jnp.full((1,), N_EXPERTS, I32)])
        next_expert = jnp.where(later < N_EXPERTS, later, -1)
        last = lambda b, be, nbu, nxt, valid: jnp.minimum(b, nbu[0] - 1)
        row_blk = lambda: pl.BlockSpec((ROW_BLOCK, hw), lambda b, be, nbu, nxt, valid: (last(b, be, nbu, nxt, valid), 0))
        y_blk = pl.BlockSpec((ROW_BLOCK, 2 * hw), lambda b, be, nbu, nxt, valid: (last(b, be, nbu, nxt, valid), 0))
        y_buf = pl.pallas_call(
            _expert_ffn_kernel,
            grid_spec=pltpu.PrefetchScalarGridSpec(
                num_scalar_prefetch=4,
                grid=(n_blocks,),
                in_specs=[row_blk(), row_blk(),
                          pl.BlockSpec(memory_space=pl.ANY),
                          pl.BlockSpec((1, 1, 2 * D_FF), lambda b, be, nbu, nxt, valid: (be[b], 0, 0)),
                          pl.BlockSpec(memory_space=pl.ANY),
                          pl.BlockSpec((1, 1, D), lambda b, be, nbu, nxt, valid: (be[b], 0, 0))],
                out_specs=y_blk,
                scratch_shapes=[pltpu.VMEM((D, 2 * D_FF), F32), pltpu.VMEM((D_FF, D), F32),
                                pltpu.VMEM((D, 2 * D_FF), BF16), pltpu.VMEM((D_FF, D), BF16),
                                pltpu.SemaphoreType.DMA((2,))],
            ),
            out_shape=jax.ShapeDtypeStruct((n_rows, 2 * hw), U32),
            compiler_params=_cparams(("arbitrary",)),
            name="expert_ffn",
        )(block_expert[0, :n_blocks], nb_used[0, :1], next_expert, block_valid[0, :n_blocks], xa_buf, xb_buf,
          w_mlp1[0], b1r, w_mlp2[0], b2r)

        yg = _sc_gather_rows(y_buf, destT.reshape(n_assign)).reshape(TOP_K, N, 2 * hw)
        tc = COMBINE_TILE
        prev = () if out is None else (out,)
        out = pl.pallas_call(
            _combine_kernel if out is None else _combine_into_kernel,
            grid=(N // tc,),
            in_specs=[pl.BlockSpec((tc, D), lambda i: (i, 0)),
                      pl.BlockSpec((TOP_K, tc, 2 * hw), lambda i: (0, i, 0)),
                      pl.BlockSpec((GATE_ROWS, tc), lambda i: (0, i)),
                      _full((GATE_ROWS, TOP_K * LANES))] + [pl.BlockSpec(memory_space=pl.ANY)] * len(prev),
            out_specs=pl.BlockSpec((tc, D), lambda i: (i + b_off * S // tc, 0)),
            out_shape=jax.ShapeDtypeStruct((B * S, D), F32),
            input_output_aliases={4: 0} if prev else {},
            compiler_params=_cparams(("parallel",)),
            name="combine",
        )(x1, yg, gateT, gate_spread, *prev)
        b_off += Bp
    return out.reshape(B, S, D)
```

```python
import functools

import jax
import jax.numpy as jnp
from jax import lax
from jax.experimental import pallas as pl
from jax.experimental.pallas import tpu as pltpu
from jax.experimental.pallas import tpu_sc as plsc

F32 = jnp.float32
BF16 = jnp.bfloat16
I32 = jnp.int32
U32 = jnp.uint32

D_MODEL = 1024
N_DIFF_HEADS = 4
DIFF_HEAD_DIM = 64
DIFF_V_DIM = 128
DIFF_W = 512
GMLP_W = 256
GMLP_GROUPS = 4
CHUNK = 128
MEM_W = 256
N_MEM_HEADS = 4
HEAD_GROUP = 64
D_IN_PROJ = 2304
ROPE_THETA = 500000.0
ROT_DIM = 16
ROPE_STACK_ROWS = 2 * ROT_DIM
N_EXPERTS = 32
TOP_K = 4
D_FF = 1024
SWIGLU_LIMIT = 7.0
SWIGLU_ALPHA = 1.702
EPS = 1e-6
LAMBDA_INIT = 0.8 - 0.6

LANES = 128
ROW_BLOCK = 1024
FFN_SUB_ROWS = 256
TOKEN_TILE = 1024
COMBINE_TILE = 1024
MEM_BATCHES = 4
LAST_PART_DIVISOR = 2
ROUTER_TILE = 1024
GATE_ROWS = 16
ATTN_TILE = 256
ATTN_EXT_ROWS = 128 + 16
VMEM_LIMIT = 56 * 1024 * 1024
NEG_BIG = -1e30
LOG2_E = 1.4426950408889634


def _cparams(sem):
    return pltpu.CompilerParams(dimension_semantics=sem, vmem_limit_bytes=VMEM_LIMIT)


def _dot(a, b):
    return jnp.dot(a, b, preferred_element_type=F32)


def _dot_nt(a, b):
    return lax.dot_general(a, b, (((1,), (1,)), ((), ())), preferred_element_type=F32)


def _rms(x, gain):
    ms = jnp.mean(x * x, axis=-1, keepdims=True)
    return x * lax.rsqrt(ms + EPS) * gain


def _group_rms(t, ones_bd, gain):
    w = ones_bd.shape[0]
    chunks = []
    for j in range(t.shape[1] // w):
        c = t[:, j * w:(j + 1) * w]
        ss = _dot((c * c).astype(BF16), ones_bd)
        chunks.append(c * lax.rsqrt(ss * (1.0 / HEAD_GROUP) + EPS))
    return (chunks[0] if len(chunks) == 1 else jnp.concatenate(chunks, axis=1)) * gain


def _pack_bf16_pairs(v):
    w = v.shape[1] // 2
    bits = lax.bitcast_convert_type(v.astype(BF16).astype(F32), U32)
    return (bits[:, :w] & jnp.uint32(0xFFFF0000)) | (bits[:, w:] >> jnp.uint32(16))


def _unpack_bf16_pairs(words):
    hi = lax.bitcast_convert_type(words & jnp.uint32(0xFFFF0000), F32)
    lo = lax.bitcast_convert_type(words << jnp.uint32(16), F32)
    return hi, lo


def _mem_kv_kernel(mem_ref, gmem_ref, wkv_ref, gck_ref, ones_ref, kT_ref, v_ref):
    for j in range(mem_ref.shape[0]):
        m = _rms(mem_ref[j], gmem_ref[...]).astype(BF16)
        kv = _dot(m, wkv_ref[...])
        k = _group_rms(kv[:, :MEM_W], ones_ref[...], gck_ref[...])
        kT_ref[j] = k.T.astype(BF16)
        v_ref[j] = kv[:, MEM_W:].astype(BF16)


def _gelu_tanh(x):
    return 0.5 * x * (1.0 + jnp.tanh(0.7978845608028654 * (x + 0.044715 * (x * x * x))))


def _mixer_in_kernel(x_ref, cs_ref, spread_ref, gmix_ref, win_ref, gq_ref, gk_ref,
                     gsgu_ref, wsp_ref, bsp_ref, gcq_ref, ones256_ref, kT_ref, vm_ref,
                     q_out, k_out, v_out, gc_out):
    tm = x_ref.shape[0]
    hb = _rms(x_ref[...], gmix_ref[...]).astype(BF16)

    def proj(lo, hi):
        return _dot(hb, win_ref[:, lo:hi])

    lane = lax.broadcasted_iota(I32, (tm, LANES), 1)
    first_half = (lane % HEAD_GROUP) < (ROT_DIM // 2)
    tab = lax.dot_general(cs_ref[...], spread_ref[...], (((0,), (0,)), ((), ())), preferred_element_type=F32)
    cosb = tab[:, :LANES] + jnp.where((lane % HEAD_GROUP) >= ROT_DIM, 1.0, 0.0)
    sinb = tab[:, LANES:]

    def norm_rope(t, gain, out_ref):
        tn = _group_rms(t, ones256_ref[...], gain)
        for j in range(DIFF_W // LANES):
            c = tn[:, j * LANES:(j + 1) * LANES]
            partner = jnp.where(first_half, pltpu.roll(c, LANES - ROT_DIM // 2, 1), pltpu.roll(c, ROT_DIM // 2, 1))
            out_ref[:, j * LANES:(j + 1) * LANES] = (c * cosb + partner * sinb).astype(BF16)

    norm_rope(proj(0, DIFF_W), gq_ref[...] * (DIFF_HEAD_DIM ** -0.5 * LOG2_E), q_out)
    norm_rope(proj(DIFF_W, 2 * DIFF_W), gk_ref[...], k_out)
    v_out[...] = proj(2 * DIFF_W, 3 * DIFF_W).astype(BF16)

    z = _gelu_tanh(proj(3 * DIFF_W, 3 * DIFF_W + 2 * GMLP_W))
    u = z[:, :GMLP_W]
    vg = z[:, GMLP_W:]
    vc = vg - jnp.mean(vg, axis=-1, keepdims=True)
    vgn = (vc * lax.rsqrt(jnp.mean(vc * vc, axis=-1, keepdims=True) + EPS) * gsgu_ref[...]).astype(BF16)
    wrow = lax.broadcasted_iota(I32, (CHUNK, GMLP_GROUPS * CHUNK), 0)
    wcol = lax.broadcasted_iota(I32, (CHUNK, GMLP_GROUPS * CHUNK), 1) % CHUNK
    w_causal = jnp.where(wcol <= wrow, wsp_ref[...], 0.0).astype(BF16)
    grp = lax.broadcasted_iota(I32, (CHUNK, GMLP_W), 1) // HEAD_GROUP
    zero_b = jnp.zeros((CHUNK, GMLP_W), BF16)
    for r in range(tm // CHUNK):
        vchunk = vgn[r * CHUNK:(r + 1) * CHUNK, :]
        v_bd = jnp.concatenate([jnp.where(grp == g, vchunk, zero_b) for g in range(GMLP_GROUPS)], axis=0)
        mixed = _dot(w_causal, v_bd) + bsp_ref[...]
        gc_out[r * CHUNK:(r + 1) * CHUNK, 0:GMLP_W] = (u[r * CHUNK:(r + 1) * CHUNK, :] * mixed).astype(BF16)

    pc = proj(3 * DIFF_W + 2 * GMLP_W, D_IN_PROJ)
    qc = _group_rms(pc, ones256_ref[...], gcq_ref[...] * (HEAD_GROUP ** -0.5)).astype(BF16)
    hgrp = lax.broadcasted_iota(I32, (tm, MEM_W), 1) // HEAD_GROUP
    zero_q = jnp.zeros((tm, MEM_W), BF16)
    q_st = jnp.concatenate([jnp.where(hgrp == h, qc, zero_q) for h in range(N_MEM_HEADS)], axis=0)
    s = _dot(q_st, kT_ref[0])
    p = jnp.exp(s - jnp.max(s, axis=-1, keepdims=True))
    o = _dot(p.astype(BF16), vm_ref[0]) / jnp.sum(p, axis=-1, keepdims=True)
    c = jnp.zeros((tm, MEM_W), F32)
    for h in range(N_MEM_HEADS):
        c = c + jnp.where(hgrp == h, o[h * tm:(h + 1) * tm, :], 0.0)
    gc_out[:, GMLP_W:GMLP_W + MEM_W] = c.astype(BF16)


def _diff_attn_kernel(q_ref, k_ref, v_ref, lq1_ref, lk1_ref, lq2_ref, lk2_ref, gsub_ref, o_ref,
                      s0_ref, s1_ref, m_ref, acc_ref, vT_ref):
    tq = q_ref.shape[0]
    seq = k_ref.shape[0]
    i = pl.program_id(1)
    feat = lax.broadcasted_iota(I32, (DIFF_V_DIM, tq), 0)
    heads = range(N_DIFF_HEADS)
    hl = lambda h: slice(h * DIFF_V_DIM, (h + 1) * DIFF_V_DIM)
    ext_rows = vT_ref.shape[1]

    @pl.when(i == 0)
    def _():
        ones_row = jnp.where(lax.broadcasted_iota(I32, (ext_rows - DIFF_V_DIM, seq), 0) == 0, 1.0, 0.0).astype(BF16)
        for h in heads:
            for c in range(seq // tq):
                vT_ref[h, 0:DIFF_V_DIM, c * tq:(c + 1) * tq] = v_ref[c * tq:(c + 1) * tq, hl(h)].T
            vT_ref[h, DIFF_V_DIM:ext_rows, :] = ones_row

    def stacked_qT(h):
        qT = q_ref[:, hl(h)].T
        zero = jnp.zeros_like(qT)
        return jnp.concatenate([jnp.where(feat < DIFF_HEAD_DIM, qT, zero), jnp.where(feat >= DIFF_HEAD_DIM, qT, zero)],
                               axis=1)

    qsT = [stacked_qT(h) for h in heads]

    def scores(t, s_ref):
        rows = pl.ds(pl.multiple_of(t * tq, tq), tq)
        for h in heads:
            s_ref[h] = _dot(k_ref[rows, hl(h)], qsT[h])

    def update(t, s_ref, causal):
        cols = pl.ds(pl.multiple_of(t * tq, tq), tq)
        for h in heads:
            s = s_ref[h]
            if causal:
                key = lax.broadcasted_iota(I32, (tq, 2 * tq), 0)
                qry = lax.broadcasted_iota(I32, (tq, 2 * tq), 1) % tq
                s = jnp.where(key <= qry, s, NEG_BIG)
            m = m_ref[h]
            m_new = jnp.maximum(m, jnp.max(s, axis=0, keepdims=True))
            alpha = jnp.exp2(m - m_new)
            p = jnp.exp2(s - m_new)
            m_ref[h] = m_new
            acc_ref[h] = alpha * acc_ref[h] + _dot(vT_ref[h, :, cols], p.astype(BF16))

    m_ref[...] = jnp.full(m_ref.shape, NEG_BIG, F32)

    @pl.when(jnp.logical_and(pl.program_id(0) == 0, i == 0))
    def _():
        acc_ref[...] = jnp.zeros(acc_ref.shape, F32)

    scores(0, s0_ref)

    def pair(pidx, carry):
        t = 2 * pidx
        scores(t + 1, s1_ref)
        update(t, s0_ref, False)
        scores(t + 2, s0_ref)
        update(t + 1, s1_ref, False)
        return carry

    lax.fori_loop(0, i // 2, pair, 0)

    @pl.when(i % 2 == 0)
    def _():
        update(i, s0_ref, True)

    @pl.when(i % 2 == 1)
    def _():
        scores(i, s1_ref)
        update(i - 1, s0_ref, False)
        update(i, s1_ref, True)

    lam = (jnp.exp(jnp.sum(lq1_ref[...] * lk1_ref[...], axis=-1, keepdims=True))
           - jnp.exp(jnp.sum(lq2_ref[...] * lk2_ref[...], axis=-1, keepdims=True)) + LAMBDA_INIT)
    for h in heads:
        on = acc_ref[h, 0:DIFF_V_DIM, :] * (1.0 / acc_ref[h, DIFF_V_DIM:DIFF_V_DIM + 1, :])
        o = on[:, :tq] - lam * on[:, tq:]
        ms = jnp.mean(o * o, axis=0, keepdims=True)
        o = o * lax.rsqrt(ms + EPS) * gsub_ref[...] * (1.0 - LAMBDA_INIT)
        o_ref[:, hl(h)] = o.T.astype(BF16)


def _out_router_kernel(x_ref, a_ref, gc_ref, wo_ref, gffn_ref, wrh_ref, wrl_ref, br_ref, before_ref,
                       x1_out, hpa_out, hpb_out, idx_out, gate_out, pos_out, cnt_out, carry_ref):
    tm = x_ref.shape[0]

    @pl.when(pl.program_id(0) == 0)
    def _():
        carry_ref[...] = jnp.zeros_like(carry_ref)

    sub = before_ref.shape[0]
    wr_stack = jnp.concatenate([wrh_ref[...], wrl_ref[...]], axis=0)
    eio = lax.broadcasted_iota(I32, (N_EXPERTS, sub), 0)
    before_b = before_ref[...]
    carry = carry_ref[...]
    for r in range(tm // sub):
        rows = slice(r * sub, (r + 1) * sub)
        mix = jnp.concatenate([a_ref[rows, :], gc_ref[rows, :]], axis=1)
        x1 = x_ref[rows, :] + _dot(mix, wo_ref[...])
        x1_out[rows, :] = x1
        h2 = _rms(x1, gffn_ref[...])
        hb = h2.astype(BF16)
        hpa_out[rows, :] = _pack_bf16_pairs(h2[:, :D_MODEL // 2])
        hpb_out[rows, :] = _pack_bf16_pairs(h2[:, D_MODEL // 2:])

        h_lo = (h2 - hb.astype(F32)).astype(BF16)
        both = _dot_nt(wr_stack, hb)
        logits = (both[:N_EXPERTS] + both[N_EXPERTS:]) + _dot_nt(wrh_ref[...], h_lo) + br_ref[...]

        vals, idxs, sels = [], [], []
        cur = logits
        for _ in range(TOP_K):
            m = jnp.max(cur, axis=0, keepdims=True)
            ik = jnp.min(jnp.where(cur == m, eio, N_EXPERTS), axis=0, keepdims=True)
            sel = eio == ik
            cur = jnp.where(sel, -jnp.inf, cur)
            vals.append(m)
            idxs.append(ik)
            sels.append(sel)
        es = [jnp.exp(v - vals[0]) for v in vals]
        tot = es[0] + es[1] + es[2] + es[3]
        gates = jnp.concatenate([e / tot for e in es], axis=0)
        g_hi = gates.astype(BF16)
        g_lo = (gates - g_hi.astype(F32)).astype(BF16)
        gate_out[:, rows] = jnp.concatenate(
            [g_hi, g_lo, jnp.zeros((GATE_ROWS - 2 * TOP_K, sub), BF16)], axis=0)
        idx_out[:, rows] = jnp.concatenate(idxs, axis=0)

        cnt = jnp.zeros((N_EXPERTS, sub), F32)
        for sel in sels:
            cnt = cnt + jnp.where(sel, 1.0, 0.0)
        base = carry + _dot(cnt.astype(BF16), before_b)
        pos_out[:, rows] = jnp.concatenate(
            [jnp.sum(jnp.where(sel, base, 0.0), axis=0, keepdims=True) for sel in sels], axis=0).astype(I32)
        carry = carry + jnp.sum(cnt, axis=1, keepdims=True)
    carry_ref[...] = carry
    cnt_out[...] = carry


def _dest_kernel(cnt_ref, idx_ref, pos_ref, dest_out, be_out, valid_out, nbu_out):
    idx = idx_ref[...]
    dest = pos_ref[...]
    bidx = lax.broadcasted_iota(I32, be_out.shape, 1)
    be = jnp.zeros(be_out.shape, I32)
    valid = jnp.zeros(be_out.shape, I32)
    run = jnp.int32(0)
    for e in range(N_EXPERTS):
        dest = dest + jnp.where(idx == e, run, 0)
        first_block = run // ROW_BLOCK
        run = run + ((cnt_ref[e] + (ROW_BLOCK - 1)) // ROW_BLOCK) * ROW_BLOCK
        mine = jnp.logical_and(bidx >= first_block, bidx < run // ROW_BLOCK)
        valid = jnp.where(mine, jnp.clip(cnt_ref[e] - (bidx - first_block) * ROW_BLOCK, 0, ROW_BLOCK), valid)
        be = be + jnp.where(bidx >= run // ROW_BLOCK, 1, 0)
    dest_out[...] = dest
    be_out[...] = jnp.minimum(be, N_EXPERTS - 1)
    valid_out[...] = valid
    nbu_out[...] = jnp.zeros(nbu_out.shape, I32) + run // ROW_BLOCK


def _expert_ffn_kernel(be_ref, nbu_ref, nxt_ref, valid_ref, xa_ref, xb_ref, w1_hbm, b1_ref, w2_hbm, b2_ref,
                       y_ref, w1f_ref, w2f_ref, w1s_ref, w2s_ref, sem_ref):
    b = pl.program_id(0)

    def weight_copies(e):
        return (pltpu.make_async_copy(w1_hbm.at[e], w1f_ref, sem_ref.at[0]),
                pltpu.make_async_copy(w2_hbm.at[e], w2f_ref, sem_ref.at[1]))

    def mlp(rows):
        parts = _unpack_bf16_pairs(xa_ref[0:rows, :]) + _unpack_bf16_pairs(xb_ref[0:rows, :])
        xrow = jnp.concatenate([p.astype(BF16) for p in parts], axis=1)
        hm = _dot(xrow, w1s_ref[...]) + b1_ref[0]
        glu = jnp.minimum(hm[:, :D_FF], SWIGLU_LIMIT)
        lin = jnp.clip(hm[:, D_FF:], -SWIGLU_LIMIT, SWIGLU_LIMIT)
        act = glu * jax.nn.sigmoid(SWIGLU_ALPHA * glu) * (lin + 1.0)
        y = _dot(act.astype(BF16), w2s_ref[...]) + b2_ref[0]
        hw = D_MODEL // 4
        y_ref[0:rows, 0:hw] = _pack_bf16_pairs(y[:, :D_MODEL // 2])
        y_ref[0:rows, hw:] = _pack_bf16_pairs(y[:, D_MODEL // 2:])

    @pl.when(b < nbu_ref[0])
    def _():
        e = be_ref[b]

        @pl.when(b == 0)
        def _():
            for cp in weight_copies(e):
                cp.start()

        @pl.when(jnp.logical_or(b == 0, e != be_ref[jnp.maximum(b - 1, 0)]))
        def _():
            for cp in weight_copies(e):
                cp.wait()
            w1s_ref[...] = w1f_ref[...].astype(BF16)
            w2s_ref[...] = w2f_ref[...].astype(BF16)
            nxt = nxt_ref[e]

            @pl.when(nxt >= 0)
            def _():
                for cp in weight_copies(nxt):
                    cp.start()

        n_sub = (valid_ref[b] + (FFN_SUB_ROWS - 1)) // FFN_SUB_ROWS
        for k in range(1, ROW_BLOCK // FFN_SUB_ROWS + 1):
            pl.when(n_sub == k)(functools.partial(mlp, k * FFN_SUB_ROWS))


SC_WINDOW = 128
SC_GATHER_WINDOW = 64


def _sc_mesh():
    return plsc.VectorSubcoreMesh(core_axis_name="c", subcore_axis_name="s")


def _sc_gather_rows(table, idx):
    n = idx.shape[0]
    width = table.shape[1]
    win = SC_GATHER_WINDOW

    @functools.partial(pl.kernel, out_type=jax.ShapeDtypeStruct((n, width), table.dtype), mesh=_sc_mesh(),
                       scratch_types=[])
    def gather_kernel(t_hbm, i_hbm, o_hbm):
        def body(i_vmem, o_vmem):
            pltpu.sync_copy(t_hbm.at[i_vmem.at[0]], o_vmem)

        pltpu.emit_pipeline(
            body,
            grid=(n // win,),
            in_specs=[pl.BlockSpec((1, win), lambda i: (i, 0))],
            out_specs=[pl.BlockSpec((win, width), lambda i: (i, 0))],
            core_axis_name=("c", "s"),
            dimension_semantics=(pltpu.PARALLEL,),
        )(i_hbm, o_hbm)

    return gather_kernel(table, idx.reshape(n // win, win))


def _sc_scatter_rows(rows, idx_rows, n_out):
    n, width = rows.shape

    @functools.partial(pl.kernel, out_type=jax.ShapeDtypeStruct((n_out, width), rows.dtype), mesh=_sc_mesh(),
                       scratch_types=[])
    def scatter_kernel(r_hbm, *refs):
        i_hbms, o_hbm = refs[:-1], refs[-1]

        def body(r_vmem, *i_vmems):
            for i_vmem in i_vmems:
                pltpu.sync_copy(r_vmem, o_hbm.at[i_vmem.at[0]])

        pltpu.emit_pipeline(
            body,
            grid=(n // SC_WINDOW,),
            in_specs=[pl.BlockSpec((SC_WINDOW, width), lambda i: (i, 0))]
            + [pl.BlockSpec((1, SC_WINDOW), lambda i: (0, i)) for _ in i_hbms],
            out_specs=[],
            core_axis_name=("c", "s"),
            dimension_semantics=(pltpu.PARALLEL,),
        )(r_hbm, *i_hbms)

    return scatter_kernel(rows, *idx_rows)


def _combine_kernel(x1_ref, yg_ref, gate_ref, spread_ref, o_ref):
    q = D_MODEL // 4
    x1 = x1_ref[...]
    acc = [x1[:, j * q:(j + 1) * q] for j in range(4)]
    gfull = lax.dot_general(gate_ref[...], spread_ref[...], (((0,), (0,)), ((), ())), preferred_element_type=F32)
    for k in range(TOP_K):
        parts = _unpack_bf16_pairs(yg_ref[k, :, 0:q]) + _unpack_bf16_pairs(yg_ref[k, :, q:])
        g = jnp.tile(gfull[:, k * LANES:(k + 1) * LANES], (1, q // LANES))
        acc = [a + g * p for a, p in zip(acc, parts)]
    for j in range(4):
        o_ref[:, j * q:(j + 1) * q] = acc[j]


def _combine_into_kernel(x1_ref, yg_ref, gate_ref, spread_ref, prev_ref, o_ref):
    del prev_ref
    _combine_kernel(x1_ref, yg_ref, gate_ref, spread_ref, o_ref)


def _block_diag_ones(width):
    r = jnp.arange(width) // HEAD_GROUP
    return (r[:, None] == r[None, :]).astype(BF16)


def _rope_tables(positions):
    half = ROT_DIM // 2
    inv_freq = ROPE_THETA ** (-jnp.arange(0, ROT_DIM, 2, dtype=F32) / ROT_DIM)
    ang = inv_freq[:, None] * positions.astype(F32).reshape(1, -1)
    cs = jnp.concatenate([jnp.cos(ang), jnp.sin(ang)], axis=0)
    cs_hi = cs.astype(BF16)
    cs_lo = (cs - cs_hi.astype(F32)).astype(BF16)
    lane = jnp.arange(LANES) % HEAD_GROUP
    j = jnp.arange(half)[:, None]
    lo_half = (lane[None, :] == j).astype(F32)
    hi_half = (lane[None, :] == j + half).astype(F32)
    spread = jnp.concatenate([
        jnp.concatenate([lo_half + hi_half, jnp.zeros((half, LANES), F32)], axis=1),
        jnp.concatenate([jnp.zeros((half, LANES), F32), hi_half - lo_half], axis=1)], axis=0)
    return jnp.concatenate([cs_hi, cs_lo], axis=0), jnp.concatenate([spread, spread], axis=0).astype(BF16)


def _full(shape):
    return pl.BlockSpec(shape, lambda *_: (0,) * len(shape))


def kernel(x, mem, positions, g_mix_norm, w_in, g_dq, g_dk, lambda_q1, lambda_k1, lambda_q2, lambda_k2, g_subln, g_sgu, w_spatial, b_spatial, g_mem_norm, w_mem_kv, g_cq, g_ck, w_out, g_ffn_norm, w_router, b_router, w_mlp1, b_mlp1, w_mlp2, b_mlp2):
    B, S, D = x.shape
    M = mem.shape[1]
    tm = TOKEN_TILE
    tr = ROUTER_TILE
    assert D == D_MODEL and S % tm == 0 and S % ATTN_TILE == 0 and g_mix_norm.shape[0] == 1
    last_batches = B // LAST_PART_DIVISOR
    part_batches = (B - last_batches, last_batches) if last_batches > 0 and S % tr == 0 else (B,)

    xf = x.reshape(B * S, D)
    rope_cs, rope_spread = _rope_tables(positions)
    ones256 = _block_diag_ones(MEM_W)
    row = lambda v: v.reshape(1, -1).astype(F32)
    tile_row = lambda v, reps: jnp.tile(v.reshape(1, -1).astype(F32), (1, reps))
    w_in_b, w_out_b, w_kv_b = w_in[0].astype(BF16), w_out[0].astype(BF16), w_mem_kv[0].astype(BF16)
    w_sp_lanes = jnp.transpose(w_spatial[0], (1, 0, 2)).reshape(CHUNK, GMLP_GROUPS * CHUNK)
    b_sp_lanes = jnp.repeat(b_spatial[0].T, HEAD_GROUP, axis=1)
    wr = w_router[0].T.astype(F32)
    wr_hi = wr.astype(BF16)
    wr_lo = (wr - wr_hi.astype(F32)).astype(BF16)
    before = (jnp.arange(tr)[:, None] < jnp.arange(tr)[None, :]).astype(BF16)
    gate_row = jnp.arange(GATE_ROWS)[:, None]
    gate_spread = ((gate_row < 2 * TOP_K) & (gate_row % TOP_K == jnp.arange(TOP_K * LANES)[None, :] // LANES)).astype(BF16)
    b1r = b_mlp1[0].reshape(N_EXPERTS, 1, 2 * D_FF)
    b2r = b_mlp2[0].reshape(N_EXPERTS, 1, D)

    out = None
    b_off = 0
    for Bp in part_batches:
        N = Bp * S
        assert N % tr == 0 and N % tm == 0
        n_assign = N * TOP_K
        n_blocks = -(-n_assign // ROW_BLOCK) + N_EXPERTS
        n_rows = n_blocks * ROW_BLOCK
        nb_pad = -(-n_blocks // LANES) * LANES
        t_off = b_off * S // tm
        r_off = b_off * S // tr

        mb = MEM_BATCHES if Bp % MEM_BATCHES == 0 and b_off % MEM_BATCHES == 0 else 1
        kT, vm = pl.pallas_call(
            _mem_kv_kernel,
            grid=(Bp // mb,),
            in_specs=[pl.BlockSpec((mb, M, D), lambda b: (b + b_off // mb, 0, 0)), _full((1, D)), _full((D, 2 * MEM_W)),
                      _full((1, MEM_W)), _full((MEM_W, MEM_W))],
            out_specs=[pl.BlockSpec((mb, MEM_W, M), lambda b: (b, 0, 0)),
                       pl.BlockSpec((mb, M, MEM_W), lambda b: (b, 0, 0))],
            out_shape=[jax.ShapeDtypeStruct((Bp, MEM_W, M), BF16), jax.ShapeDtypeStruct((Bp, M, MEM_W), BF16)],
            compiler_params=_cparams(("parallel",)),
            name="mem_kv",
        )(mem, row(g_mem_norm[0]), w_kv_b, tile_row(g_ck[0], N_MEM_HEADS), ones256)

        tiles_per_batch = S // tm
        tok = lambda w: pl.BlockSpec((tm, w), lambda i: (i, 0))
        tok_in = lambda w: pl.BlockSpec((tm, w), lambda i: (i + t_off, 0))
        qn, kn, vv, gc = pl.pallas_call(
            _mixer_in_kernel,
            grid=(N // tm,),
            in_specs=[tok_in(D), pl.BlockSpec((ROPE_STACK_ROWS, tm), lambda i: (0, i + t_off)),
                      _full((ROPE_STACK_ROWS, 2 * LANES)),
                      _full((1, D)), _full((D, D_IN_PROJ)),
                      _full((1, DIFF_W)), _full((1, DIFF_W)),
                      _full((1, GMLP_W)), _full((CHUNK, GMLP_GROUPS * CHUNK)), _full((CHUNK, GMLP_W)),
                      _full((1, MEM_W)), _full((MEM_W, MEM_W)),
                      pl.BlockSpec((1, MEM_W, M), lambda i: (i // tiles_per_batch, 0, 0)),
                      pl.BlockSpec((1, M, MEM_W), lambda i: (i // tiles_per_batch, 0, 0))],
            out_specs=[tok(DIFF_W), tok(DIFF_W), tok(DIFF_W), tok(GMLP_W + MEM_W)],
            out_shape=[jax.ShapeDtypeStruct((N, DIFF_W), BF16)] * 3 + [jax.ShapeDtypeStruct((N, GMLP_W + MEM_W), BF16)],
            compiler_params=_cparams(("parallel",)),
            name="mixer_in",
        )(xf, rope_cs, rope_spread, row(g_mix_norm[0]), w_in_b,
          tile_row(g_dq[0], 2 * N_DIFF_HEADS), tile_row(g_dk[0], 2 * N_DIFF_HEADS),
          row(g_sgu[0]), w_sp_lanes, b_sp_lanes, tile_row(g_cq[0], N_MEM_HEADS), ones256, kT, vm)

        tq = ATTN_TILE
        nq = S // tq
        head_q = pl.BlockSpec((tq, DIFF_W), lambda b, i: (b * nq + i, 0))
        head_kv = pl.BlockSpec((S, DIFF_W), lambda b, i: (b, 0))
        lam_spec = pl.BlockSpec((1, DIFF_HEAD_DIM), lambda b, i: (0, 0))
        nh = N_DIFF_HEADS
        attn = pl.pallas_call(
            _diff_attn_kernel,
            grid=(Bp, nq),
            in_specs=[head_q, head_kv, head_kv, lam_spec, lam_spec, lam_spec, lam_spec,
                      pl.BlockSpec((DIFF_V_DIM, 1), lambda b, i: (0, 0))],
            out_specs=head_q,
            out_shape=jax.ShapeDtypeStruct((N, DIFF_W), BF16),
            scratch_shapes=[pltpu.VMEM((nh, tq, 2 * tq), F32), pltpu.VMEM((nh, tq, 2 * tq), F32),
                            pltpu.VMEM((nh, 1, 2 * tq), F32), pltpu.VMEM((nh, ATTN_EXT_ROWS, 2 * tq), F32),
                            pltpu.VMEM((nh, ATTN_EXT_ROWS, S), BF16)],
            compiler_params=_cparams(("arbitrary", "arbitrary")),
            name="diff_attn",
        )(qn, kn, vv, row(lambda_q1[0]), row(lambda_k1[0]), row(lambda_q2[0]), row(lambda_k2[0]),
          g_subln[0].reshape(DIFF_V_DIM, 1).astype(F32))

        rtok = lambda w: pl.BlockSpec((tr, w), lambda i: (i, 0))
        rtok_in = lambda w: pl.BlockSpec((tr, w), lambda i: (i + r_off, 0))
        tokT = lambda: pl.BlockSpec((TOP_K, tr), lambda i: (0, i))
        hw = D // 4
        x1, hpa, hpb, idxT, gateT, posT, counts = pl.pallas_call(
            _out_router_kernel,
            grid=(N // tr,),
            in_specs=[rtok_in(D), rtok(DIFF_W), rtok(GMLP_W + MEM_W), _full((D, D)), _full((1, D)),
                      _full((N_EXPERTS, D)), _full((N_EXPERTS, D)), _full((N_EXPERTS, 1)), _full((tr, tr))],
            out_specs=[rtok(D), rtok(hw), rtok(hw), tokT(), pl.BlockSpec((GATE_ROWS, tr), lambda i: (0, i)), tokT(),
                       _full((N_EXPERTS, 1))],
            out_shape=[jax.ShapeDtypeStruct((N, D), F32), jax.ShapeDtypeStruct((N, hw), U32),
                       jax.ShapeDtypeStruct((N, hw), U32), jax.ShapeDtypeStruct((TOP_K, N), I32),
                       jax.ShapeDtypeStruct((GATE_ROWS, N), BF16),
                       jax.ShapeDtypeStruct((TOP_K, N), I32), jax.ShapeDtypeStruct((N_EXPERTS, 1), F32)],
            scratch_shapes=[pltpu.VMEM((N_EXPERTS, 1), F32)],
            compiler_params=_cparams(("arbitrary",)),
            name="out_router",
        )(xf, attn, gc, w_out_b, row(g_ffn_norm[0]), wr_hi, wr_lo, b_router[0].reshape(N_EXPERTS, 1).astype(F32), before)

        destT, block_expert, block_valid, nb_used = pl.pallas_call(
            _dest_kernel,
            grid_spec=pltpu.PrefetchScalarGridSpec(
                num_scalar_prefetch=1,
                grid=(1,),
                in_specs=[pl.BlockSpec((TOP_K, N), lambda i, c: (0, 0)), pl.BlockSpec((TOP_K, N), lambda i, c: (0, 0))],
                out_specs=[pl.BlockSpec((TOP_K, N), lambda i, c: (0, 0)), pl.BlockSpec((1, nb_pad), lambda i, c: (0, 0)),
                           pl.BlockSpec((1, nb_pad), lambda i, c: (0, 0)), pl.BlockSpec((1, LANES), lambda i, c: (0, 0))],
            ),
            out_shape=[jax.ShapeDtypeStruct((TOP_K, N), I32), jax.ShapeDtypeStruct((1, nb_pad), I32),
                       jax.ShapeDtypeStruct((1, nb_pad), I32), jax.ShapeDtypeStruct((1, LANES), I32)],
            compiler_params=_cparams(("arbitrary",)),
            name="dest",
        )(counts.reshape(N_EXPERTS).astype(I32), idxT, posT)

        dest_rows = [destT[k].reshape(1, N) for k in range(TOP_K)]
        xa_buf = _sc_scatter_rows(hpa, dest_rows, n_rows)
        xb_buf = _sc_scatter_rows(hpb, dest_rows, n_rows)

        cnt_i = counts.reshape(N_EXPERTS).astype(I32)
        owner = jnp.where(cnt_i > 0, jnp.arange(N_EXPERTS, dtype=I32), N_EXPERTS)
        later = jnp.concatenate([lax.cummin(owner[::-1])[::-1][1:], jnp.full((1,), N_EXPERTS, I32)])
        next_expert = jnp.where(later < N_EXPERTS, later, -1)
        last = lambda b, be, nbu, nxt, valid: jnp.minimum(b, nbu[0] - 1)
        row_blk = lambda: pl.BlockSpec((ROW_BLOCK, hw), lambda b, be, nbu, nxt, valid: (last(b, be, nbu, nxt, valid), 0))
        y_blk = pl.BlockSpec((ROW_BLOCK, 2 * hw), lambda b, be, nbu, nxt, valid: (last(b, be, nbu, nxt, valid), 0))
        y_buf = pl.pallas_call(
            _expert_ffn_kernel,
            grid_spec=pltpu.PrefetchScalarGridSpec(
                num_scalar_prefetch=4,
                grid=(n_blocks,),
                in_specs=[row_blk(), row_blk(),
                          pl.BlockSpec(memory_space=pl.ANY),
                          pl.BlockSpec((1, 1, 2 * D_FF), lambda b, be, nbu, nxt, valid: (be[b], 0, 0)),
                          pl.BlockSpec(memory_space=pl.ANY),
                          pl.BlockSpec((1, 1, D), lambda b, be, nbu, nxt, valid: (be[b], 0, 0))],
                out_specs=y_blk,
                scratch_shapes=[pltpu.VMEM((D, 2 * D_FF), F32), pltpu.VMEM((D_FF, D), F32),
                                pltpu.VMEM((D, 2 * D_FF), BF16), pltpu.VMEM((D_FF, D), BF16),
                                pltpu.SemaphoreType.DMA((2,))],
            ),
            out_shape=jax.ShapeDtypeStruct((n_rows, 2 * hw), U32),
            compiler_params=_cparams(("arbitrary",)),
            name="expert_ffn",
        )(block_expert[0, :n_blocks], nb_used[0, :1], next_expert, block_valid[0, :n_blocks], xa_buf, xb_buf,
          w_mlp1[0], b1r, w_mlp2[0], b2r)

        yg = _sc_gather_rows(y_buf, destT.reshape(n_assign)).reshape(TOP_K, N, 2 * hw)
        tc = COMBINE_TILE
        prev = () if out is None else (out,)
        out = pl.pallas_call(
            _combine_kernel if out is None else _combine_into_kernel,
            grid=(N // tc,),
            in_specs=[pl.BlockSpec((tc, D), lambda i: (i, 0)),
                      pl.BlockSpec((TOP_K, tc, 2 * hw), lambda i: (0, i, 0)),
                      pl.BlockSpec((GATE_ROWS, tc), lambda i: (0, i)),
                      _full((GATE_ROWS, TOP_K * LANES))] + [pl.BlockSpec(memory_space=pl.ANY)] * len(prev),
            out_specs=pl.BlockSpec((tc, D), lambda i: (i + b_off * S // tc, 0)),
            out_shape=jax.ShapeDtypeStruct((B * S, D), F32),
            input_output_aliases={4: 0} if prev else {},
            compiler_params=_cparams(("parallel",)),
            name="combine",
        )(x1, yg, gateT, gate_spread, *prev)
        b_off += Bp
    return out.reshape(B, S, D)
```

```python
import functools

import jax
import jax.numpy as jnp
from jax import lax
from jax.experimental import pallas as pl
from jax.experimental.pallas import tpu as pltpu
from jax.experimental.pallas import tpu_sc as plsc

F32 = jnp.float32
BF16 = jnp.bfloat16
I32 = jnp.int32
U32 = jnp.uint32

D_MODEL = 1024
N_DIFF_HEADS = 4
DIFF_HEAD_DIM = 64
DIFF_V_DIM = 128
DIFF_W = 512
GMLP_W = 256
GMLP_GROUPS = 4
CHUNK = 128
MEM_W = 256
N_MEM_HEADS = 4
HEAD_GROUP = 64
D_IN_PROJ = 2304
ROPE_THETA = 500000.0
ROT_DIM = 16
ROPE_STACK_ROWS = 2 * ROT_DIM
N_EXPERTS = 32
TOP_K = 4
D_FF = 1024
SWIGLU_LIMIT = 7.0
SWIGLU_ALPHA = 1.702
EPS = 1e-6
LAMBDA_INIT = 0.8 - 0.6

LANES = 128
ROW_BLOCK = 1024
FFN_SUB_ROWS = 256
TOKEN_TILE = 1024
COMBINE_TILE = 1024
MEM_BATCHES = 4
LAST_PART_DIVISOR = 2
ROUTER_TILE = 1024
GATE_ROWS = 16
ATTN_TILE = 256
ATTN_EXT_ROWS = 128 + 16
VMEM_LIMIT = 56 * 1024 * 1024
NEG_BIG = -1e30
LOG2_E = 1.4426950408889634


def _cparams(sem):
    return pltpu.CompilerParams(dimension_semantics=sem, vmem_limit_bytes=VMEM_LIMIT)


def _dot(a, b):
    return jnp.dot(a, b, preferred_element_type=F32)


def _dot_nt(a, b):
    return lax.dot_general(a, b, (((1,), (1,)), ((), ())), preferred_element_type=F32)


def _rms(x, gain):
    ms = jnp.mean(x * x, axis=-1, keepdims=True)
    return x * lax.rsqrt(ms + EPS) * gain


def _group_rms(t, ones_bd, gain):
    w = ones_bd.shape[0]
    chunks = []
    for j in range(t.shape[1] // w):
        c = t[:, j * w:(j + 1) * w]
        ss = _dot((c * c).astype(BF16), ones_bd)
        chunks.append(c * lax.rsqrt(ss * (1.0 / HEAD_GROUP) + EPS))
    return (chunks[0] if len(chunks) == 1 else jnp.concatenate(chunks, axis=1)) * gain


def _pack_bf16_pairs(v):
    w = v.shape[1] // 2
    bits = lax.bitcast_convert_type(v.astype(BF16).astype(F32), U32)
    return (bits[:, :w] & jnp.uint32(0xFFFF0000)) | (bits[:, w:] >> jnp.uint32(16))


def _unpack_bf16_pairs(words):
    hi = lax.bitcast_convert_type(words & jnp.uint32(0xFFFF0000), F32)
    lo = lax.bitcast_convert_type(words << jnp.uint32(16), F32)
    return hi, lo


def _mem_kv_kernel(mem_ref, gmem_ref, wkv_ref, gck_ref, ones_ref, kT_ref, v_ref):
    for j in range(mem_ref.shape[0]):
        m = _rms(mem_ref[j], gmem_ref[...]).astype(BF16)
        kv = _dot(m, wkv_ref[...])
        k = _group_rms(kv[:, :MEM_W], ones_ref[...], gck_ref[...])
        kT_ref[j] = k.T.astype(BF16)
        v_ref[j] = kv[:, MEM_W:].astype(BF16)


def _gelu_tanh(x):
    return 0.5 * x * (1.0 + jnp.tanh(0.7978845608028654 * (x + 0.044715 * (x * x * x))))


def _mixer_in_kernel(x_ref, cs_ref, spread_ref, gmix_ref, win_ref, gq_ref, gk_ref,
                     gsgu_ref, wsp_ref, bsp_ref, gcq_ref, ones256_ref, kT_ref, vm_ref,
                     q_out, k_out, v_out, gc_out):
    tm = x_ref.shape[0]
    hb = _rms(x_ref[...], gmix_ref[...]).astype(BF16)

    def proj(lo, hi):
        return _dot(hb, win_ref[:, lo:hi])

    lane = lax.broadcasted_iota(I32, (tm, LANES), 1)
    first_half = (lane % HEAD_GROUP) < (ROT_DIM // 2)
    tab = lax.dot_general(cs_ref[...], spread_ref[...], (((0,), (0,)), ((), ())), preferred_element_type=F32)
    cosb = tab[:, :LANES] + jnp.where((lane % HEAD_GROUP) >= ROT_DIM, 1.0, 0.0)
    sinb = tab[:, LANES:]

    def norm_rope(t, gain, out_ref):
        tn = _group_rms(t, ones256_ref[...], gain)
        for j in range(DIFF_W // LANES):
            c = tn[:, j * LANES:(j + 1) * LANES]
            partner = jnp.where(first_half, pltpu.roll(c, LANES - ROT_DIM // 2, 1), pltpu.roll(c, ROT_DIM // 2, 1))
            out_ref[:, j * LANES:(j + 1) * LANES] = (c * cosb + partner * sinb).astype(BF16)

    norm_rope(proj(0, DIFF_W), gq_ref[...] * (DIFF_HEAD_DIM ** -0.5 * LOG2_E), q_out)
    norm_rope(proj(DIFF_W, 2 * DIFF_W), gk_ref[...], k_out)
    v_out[...] = proj(2 * DIFF_W, 3 * DIFF_W).astype(BF16)

    z = _gelu_tanh(proj(3 * DIFF_W, 3 * DIFF_W + 2 * GMLP_W))
    u = z[:, :GMLP_W]
    vg = z[:, GMLP_W:]
    vc = vg - jnp.mean(vg, axis=-1, keepdims=True)
    vgn = (vc * lax.rsqrt(jnp.mean(vc * vc, axis=-1, keepdims=True) + EPS) * gsgu_ref[...]).astype(BF16)
    wrow = lax.broadcasted_iota(I32, (CHUNK, GMLP_GROUPS * CHUNK), 0)
    wcol = lax.broadcasted_iota(I32, (CHUNK, GMLP_GROUPS * CHUNK), 1) % CHUNK
    w_causal = jnp.where(wcol <= wrow, wsp_ref[...], 0.0).astype(BF16)
    grp = lax.broadcasted_iota(I32, (CHUNK, GMLP_W), 1) // HEAD_GROUP
    zero_b = jnp.zeros((CHUNK, GMLP_W), BF16)
    for r in range(tm // CHUNK):
        vchunk = vgn[r * CHUNK:(r + 1) * CHUNK, :]
        v_bd = jnp.concatenate([jnp.where(grp == g, vchunk, zero_b) for g in range(GMLP_GROUPS)], axis=0)
        mixed = _dot(w_causal, v_bd) + bsp_ref[...]
        gc_out[r * CHUNK:(r + 1) * CHUNK, 0:GMLP_W] = (u[r * CHUNK:(r + 1) * CHUNK, :] * mixed).astype(BF16)

    pc = proj(3 * DIFF_W + 2 * GMLP_W, D_IN_PROJ)
    qc = _group_rms(pc, ones256_ref[...], gcq_ref[...] * (HEAD_GROUP ** -0.5)).astype(BF16)
    hgrp = lax.broadcasted_iota(I32, (tm, MEM_W), 1) // HEAD_GROUP
    zero_q = jnp.zeros((tm, MEM_W), BF16)
    q_st = jnp.concatenate([jnp.where(hgrp == h, qc, zero_q) for h in range(N_MEM_HEADS)], axis=0)
    s = _dot(q_st, kT_ref[0])
    p = jnp.exp(s - jnp.max(s, axis=-1, keepdims=True))
    o = _dot(p.astype(BF16), vm_ref[0]) / jnp.sum(p, axis=-1, keepdims=True)
    c = jnp.zeros((tm, MEM_W), F32)
    for h in range(N_MEM_HEADS):
        c = c + jnp.where(hgrp == h, o[h * tm:(h + 1) * tm, :], 0.0)
    gc_out[:, GMLP_W:GMLP_W + MEM_W] = c.astype(BF16)


def _diff_attn_kernel(q_ref, k_ref, v_ref, lq1_ref, lk1_ref, lq2_ref, lk2_ref, gsub_ref, o_ref,
                      s0_ref, s1_ref, m_ref, acc_ref, vT_ref):
    tq = q_ref.shape[0] // 2
    seq = k_ref.shape[0]
    pair_id = pl.program_id(1)
    feat = lax.broadcasted_iota(I32, (DIFF_V_DIM, tq), 0)
    heads = range(N_DIFF_HEADS)
    hl = lambda h: slice(h * DIFF_V_DIM, (h + 1) * DIFF_V_DIM)
    ext_rows = vT_ref.shape[1]

    @pl.when(pair_id == 0)
    def _():
        ones_row = jnp.where(lax.broadcasted_iota(I32, (ext_rows - DIFF_V_DIM, seq), 0) == 0, 1.0, 0.0).astype(BF16)
        for h in heads:
            for c in range(seq // tq):
                vT_ref[h, 0:DIFF_V_DIM, c * tq:(c + 1) * tq] = v_ref[c * tq:(c + 1) * tq, hl(h)].T
            vT_ref[h, DIFF_V_DIM:ext_rows, :] = ones_row

    @pl.when(jnp.logical_and(pl.program_id(0) == 0, pair_id == 0))
    def _():
        acc_ref[...] = jnp.zeros(acc_ref.shape, F32)

    lam = (jnp.exp(jnp.sum(lq1_ref[...] * lk1_ref[...], axis=-1, keepdims=True))
           - jnp.exp(jnp.sum(lq2_ref[...] * lk2_ref[...], axis=-1, keepdims=True)) + LAMBDA_INIT)

    def q_tile(j, first_ref, other_ref):
        i = 2 * pair_id + j
        q_rows = slice(j * tq, (j + 1) * tq)

        def stacked_qT(h):
            qT = q_ref[q_rows, hl(h)].T
            zero = jnp.zeros_like(qT)
            return jnp.concatenate(
                [jnp.where(feat < DIFF_HEAD_DIM, qT, zero), jnp.where(feat >= DIFF_HEAD_DIM, qT, zero)], axis=1)

        qsT = [stacked_qT(h) for h in heads]

        def scores(t, s_ref):
            rows = pl.ds(pl.multiple_of(t * tq, tq), tq)
            for h in heads:
                s_ref[h] = _dot(k_ref[rows, hl(h)], qsT[h])

        def update(t, s_ref, causal):
            cols = pl.ds(pl.multiple_of(t * tq, tq), tq)
            for h in heads:
                s = s_ref[h]
                if causal:
                    key = lax.broadcasted_iota(I32, (tq, 2 * tq), 0)
                    qry = lax.broadcasted_iota(I32, (tq, 2 * tq), 1) % tq
                    s = jnp.where(key <= qry, s, NEG_BIG)
                m = m_ref[h]
                m_new = jnp.maximum(m, jnp.max(s, axis=0, keepdims=True))
                alpha = jnp.exp2(m - m_new)
                p = jnp.exp2(s - m_new)
                m_ref[h] = m_new
                acc_ref[h] = alpha * acc_ref[h] + _dot(vT_ref[h, :, cols], p.astype(BF16))

        m_ref[...] = jnp.full(m_ref.shape, NEG_BIG, F32)
        scores(0, first_ref)

        def two_tiles(pidx, carry):
            t = 2 * pidx
            scores(t + 1, other_ref)
            update(t, first_ref, False)
            scores(t + 2, first_ref)
            update(t + 1, other_ref, False)
            return carry

        lax.fori_loop(0, pair_id, two_tiles, 0)

        if j == 0:
            update(i, first_ref, True)
        else:
            scores(i, other_ref)
            update(i - 1, first_ref, False)
            update(i, other_ref, True)

        for h in heads:
            on = acc_ref[h, 0:DIFF_V_DIM, :] * (1.0 / acc_ref[h, DIFF_V_DIM:DIFF_V_DIM + 1, :])
            o = on[:, :tq] - lam * on[:, tq:]
            ms = jnp.mean(o * o, axis=0, keepdims=True)
            o = o * lax.rsqrt(ms + EPS) * gsub_ref[...] * (1.0 - LAMBDA_INIT)
            o_ref[q_rows, hl(h)] = o.T.astype(BF16)

    q_tile(0, s0_ref, s1_ref)
    q_tile(1, s1_ref, s0_ref)


def _out_router_kernel(x_ref, a_ref, gc_ref, wo_ref, gffn_ref, wrh_ref, wrl_ref, br_ref, before_ref,
                       x1_out, hpa_out, hpb_out, idx_out, gate_out, pos_out, cnt_out, carry_ref):
    tm = x_ref.shape[0]

    @pl.when(pl.program_id(0) == 0)
    def _():
        carry_ref[...] = jnp.zeros_like(carry_ref)

    sub = before_ref.shape[0]
    wr_stack = jnp.concatenate([wrh_ref[...], wrl_ref[...]], axis=0)
    eio = lax.broadcasted_iota(I32, (N_EXPERTS, sub), 0)
    before_b = before_ref[...]
    carry = carry_ref[...]
    for r in range(tm // sub):
        rows = slice(r * sub, (r + 1) * sub)
        mix = jnp.concatenate([a_ref[rows, :], gc_ref[rows, :]], axis=1)
        x1 = x_ref[rows, :] + _dot(mix, wo_ref[...])
        x1_out[rows, :] = x1
        h2 = _rms(x1, gffn_ref[...])
        hb = h2.astype(BF16)
        hpa_out[rows, :] = _pack_bf16_pairs(h2[:, :D_MODEL // 2])
        hpb_out[rows, :] = _pack_bf16_pairs(h2[:, D_MODEL // 2:])

        h_lo = (h2 - hb.astype(F32)).astype(BF16)
        both = _dot_nt(wr_stack, hb)
        logits = (both[:N_EXPERTS] + both[N_EXPERTS:]) + _dot_nt(wrh_ref[...], h_lo) + br_ref[...]

        vals, idxs, sels = [], [], []
        cur = logits
        for _ in range(TOP_K):
            m = jnp.max(cur, axis=0, keepdims=True)
            ik = jnp.min(jnp.where(cur == m, eio, N_EXPERTS), axis=0, keepdims=True)
            sel = eio == ik
            cur = jnp.where(sel, -jnp.inf, cur)
            vals.append(m)
            idxs.append(ik)
            sels.append(sel)
        es = [jnp.exp(v - vals[0]) for v in vals]
        tot = es[0] + es[1] + es[2] + es[3]
        gates = jnp.concatenate([e / tot for e in es], axis=0)
        g_hi = gates.astype(BF16)
        g_lo = (gates - g_hi.astype(F32)).astype(BF16)
        gate_out[:, rows] = jnp.concatenate(
            [g_hi, g_lo, jnp.zeros((GATE_ROWS - 2 * TOP_K, sub), BF16)], axis=0)
        idx_out[:, rows] = jnp.concatenate(idxs, axis=0)

        cnt = jnp.zeros((N_EXPERTS, sub), F32)
        for sel in sels:
            cnt = cnt + jnp.where(sel, 1.0, 0.0)
        base = carry + _dot(cnt.astype(BF16), before_b)
        pos_out[:, rows] = jnp.concatenate(
            [jnp.sum(jnp.where(sel, base, 0.0), axis=0, keepdims=True) for sel in sels], axis=0).astype(I32)
        carry = carry + jnp.sum(cnt, axis=1, keepdims=True)
    carry_ref[...] = carry
    cnt_out[...] = carry


def _dest_kernel(cnt_ref, idx_ref, pos_ref, dest_out, be_out, valid_out, nbu_out):
    idx = idx_ref[...]
    dest = pos_ref[...]
    bidx = lax.broadcasted_iota(I32, be_out.shape, 1)
    be = jnp.zeros(be_out.shape, I32)
    valid = jnp.zeros(be_out.shape, I32)
    run = jnp.int32(0)
    for e in range(N_EXPERTS):
        dest = dest + jnp.where(idx == e, run, 0)
        first_block = run // ROW_BLOCK
        run = run + ((cnt_ref[e] + (ROW_BLOCK - 1)) // ROW_BLOCK) * ROW_BLOCK
        mine = jnp.logical_and(bidx >= first_block, bidx < run // ROW_BLOCK)
        valid = jnp.where(mine, jnp.clip(cnt_ref[e] - (bidx - first_block) * ROW_BLOCK, 0, ROW_BLOCK), valid)
        be = be + jnp.where(bidx >= run // ROW_BLOCK, 1, 0)
    dest_out[...] = dest
    be_out[...] = jnp.minimum(be, N_EXPERTS - 1)
    valid_out[...] = valid
    nbu_out[...] = jnp.zeros(nbu_out.shape, I32) + run // ROW_BLOCK


def _expert_ffn_kernel(be_ref, nbu_ref, nxt_ref, valid_ref, xa_ref, xb_ref, w1_hbm, b1_ref, w2_hbm, b2_ref,
                       y_ref, w1f_ref, w2f_ref, w1s_ref, w2s_ref, sem_ref):
    b = pl.program_id(0)

    def weight_copies(e):
        return (pltpu.make_async_copy(w1_hbm.at[e], w1f_ref, sem_ref.at[0]),
                pltpu.make_async_copy(w2_hbm.at[e], w2f_ref, sem_ref.at[1]))

    def mlp(rows):
        parts = _unpack_bf16_pairs(xa_ref[0:rows, :]) + _unpack_bf16_pairs(xb_ref[0:rows, :])
        xrow = jnp.concatenate([p.astype(BF16) for p in parts], axis=1)
        hm = _dot(xrow, w1s_ref[...]) + b1_ref[0]
        glu = jnp.minimum(hm[:, :D_FF], SWIGLU_LIMIT)
        lin = jnp.clip(hm[:, D_FF:], -SWIGLU_LIMIT, SWIGLU_LIMIT)
        act = glu * jax.nn.sigmoid(SWIGLU_ALPHA * glu) * (lin + 1.0)
        y = _dot(act.astype(BF16), w2s_ref[...]) + b2_ref[0]
        hw = D_MODEL // 4
        y_ref[0:rows, 0:hw] = _pack_bf16_pairs(y[:, :D_MODEL // 2])
        y_ref[0:rows, hw:] = _pack_bf16_pairs(y[:, D_MODEL // 2:])

    @pl.when(b < nbu_ref[0])
    def _():
        e = be_ref[b]

        @pl.when(b == 0)
        def _():
            for cp in weight_copies(e):
                cp.start()

        @pl.when(jnp.logical_or(b == 0, e != be_ref[jnp.maximum(b - 1, 0)]))
        def _():
            for cp in weight_copies(e):
                cp.wait()
            w1s_ref[...] = w1f_ref[...].astype(BF16)
            w2s_ref[...] = w2f_ref[...].astype(BF16)
            nxt = nxt_ref[e]

            @pl.when(nxt >= 0)
            def _():
                for cp in weight_copies(nxt):
                    cp.start()

        n_sub = (valid_ref[b] + (FFN_SUB_ROWS - 1)) // FFN_SUB_ROWS
        for k in range(1, ROW_BLOCK // FFN_SUB_ROWS + 1):
            pl.when(n_sub == k)(functools.partial(mlp, k * FFN_SUB_ROWS))


SC_WINDOW = 128
SC_GATHER_WINDOW = 64


def _sc_mesh():
    return plsc.VectorSubcoreMesh(core_axis_name="c", subcore_axis_name="s")


def _sc_gather_rows(table, idx):
    n = idx.shape[0]
    width = table.shape[1]
    win = SC_GATHER_WINDOW

    @functools.partial(pl.kernel, out_type=jax.ShapeDtypeStruct((n, width), table.dtype), mesh=_sc_mesh(),
                       scratch_types=[])
    def gather_kernel(t_hbm, i_hbm, o_hbm):
        def body(i_vmem, o_vmem):
            pltpu.sync_copy(t_hbm.at[i_vmem.at[0]], o_vmem)

        pltpu.emit_pipeline(
            body,
            grid=(n // win,),
            in_specs=[pl.BlockSpec((1, win), lambda i: (i, 0))],
            out_specs=[pl.BlockSpec((win, width), lambda i: (i, 0))],
            core_axis_name=("c", "s"),
            dimension_semantics=(pltpu.PARALLEL,),
        )(i_hbm, o_hbm)

    return gather_kernel(table, idx.reshape(n // win, win))


def _sc_scatter_rows(rows, idx_rows, n_out):
    n, width = rows.shape

    @functools.partial(pl.kernel, out_type=jax.ShapeDtypeStruct((n_out, width), rows.dtype), mesh=_sc_mesh(),
                       scratch_types=[])
    def scatter_kernel(r_hbm, *refs):
        i_hbms, o_hbm = refs[:-1], refs[-1]

        def body(r_vmem, *i_vmems):
            for i_vmem in i_vmems:
                pltpu.sync_copy(r_vmem, o_hbm.at[i_vmem.at[0]])

        pltpu.emit_pipeline(
            body,
            grid=(n // SC_WINDOW,),
            in_specs=[pl.BlockSpec((SC_WINDOW, width), lambda i: (i, 0))]
            + [pl.BlockSpec((1, SC_WINDOW), lambda i: (0, i)) for _ in i_hbms],
            out_specs=[],
            core_axis_name=("c", "s"),
            dimension_semantics=(pltpu.PARALLEL,),
        )(r_hbm, *i_hbms)

    return scatter_kernel(rows, *idx_rows)


def _combine_kernel(x1_ref, yg_ref, gate_ref, spread_ref, o_ref):
    q = D_MODEL // 4
    x1 = x1_ref[...]
    acc = [x1[:, j * q:(j + 1) * q] for j in range(4)]
    gfull = lax.dot_general(gate_ref[...], spread_ref[...], (((0,), (0,)), ((), ())), preferred_element_type=F32)
    for k in range(TOP_K):
        parts = _unpack_bf16_pairs(yg_ref[k, :, 0:q]) + _unpack_bf16_pairs(yg_ref[k, :, q:])
        g = jnp.tile(gfull[:, k * LANES:(k + 1) * LANES], (1, q // LANES))
        acc = [a + g * p for a, p in zip(acc, parts)]
    for j in range(4):
        o_ref[:, j * q:(j + 1) * q] = acc[j]


def _combine_into_kernel(x1_ref, yg_ref, gate_ref, spread_ref, prev_ref, o_ref):
    del prev_ref
    _combine_kernel(x1_ref, yg_ref, gate_ref, spread_ref, o_ref)


def _block_diag_ones(width):
    r = jnp.arange(width) // HEAD_GROUP
    return (r[:, None] == r[None, :]).astype(BF16)


def _rope_tables(positions):
    half = ROT_DIM // 2
    inv_freq = ROPE_THETA ** (-jnp.arange(0, ROT_DIM, 2, dtype=F32) / ROT_DIM)
    ang = inv_freq[:, None] * positions.astype(F32).reshape(1, -1)
    cs = jnp.concatenate([jnp.cos(ang), jnp.sin(ang)], axis=0)
    cs_hi = cs.astype(BF16)
    cs_lo = (cs - cs_hi.astype(F32)).astype(BF16)
    lane = jnp.arange(LANES) % HEAD_GROUP
    j = jnp.arange(half)[:, None]
    lo_half = (lane[None, :] == j).astype(F32)
    hi_half = (lane[None, :] == j + half).astype(F32)
    spread = jnp.concatenate([
        jnp.concatenate([lo_half + hi_half, jnp.zeros((half, LANES), F32)], axis=1),
        jnp.concatenate([jnp.zeros((half, LANES), F32), hi_half - lo_half], axis=1)], axis=0)
    return jnp.concatenate([cs_hi, cs_lo], axis=0), jnp.concatenate([spread, spread], axis=0).astype(BF16)


def _full(shape):
    return pl.BlockSpec(shape, lambda *_: (0,) * len(shape))


def kernel(x, mem, positions, g_mix_norm, w_in, g_dq, g_dk, lambda_q1, lambda_k1, lambda_q2, lambda_k2, g_subln, g_sgu, w_spatial, b_spatial, g_mem_norm, w_mem_kv, g_cq, g_ck, w_out, g_ffn_norm, w_router, b_router, w_mlp1, b_mlp1, w_mlp2, b_mlp2):
    B, S, D = x.shape
    M = mem.shape[1]
    tm = TOKEN_TILE
    tr = ROUTER_TILE
    assert D == D_MODEL and S % tm == 0 and S % (2 * ATTN_TILE) == 0 and g_mix_norm.shape[0] == 1
    last_batches = B // LAST_PART_DIVISOR
    part_batches = (B - last_batches, last_batches) if last_batches > 0 and S % tr == 0 else (B,)

    xf = x.reshape(B * S, D)
    rope_cs, rope_spread = _rope_tables(positions)
    ones256 = _block_diag_ones(MEM_W)
    row = lambda v: v.reshape(1, -1).astype(F32)
    tile_row = lambda v, reps: jnp.tile(v.reshape(1, -1).astype(F32), (1, reps))
    w_in_b, w_out_b, w_kv_b = w_in[0].astype(BF16), w_out[0].astype(BF16), w_mem_kv[0].astype(BF16)
    w_sp_lanes = jnp.transpose(w_spatial[0], (1, 0, 2)).reshape(CHUNK, GMLP_GROUPS * CHUNK)
    b_sp_lanes = jnp.repeat(b_spatial[0].T, HEAD_GROUP, axis=1)
    wr = w_router[0].T.astype(F32)
    wr_hi = wr.astype(BF16)
    wr_lo = (wr - wr_hi.astype(F32)).astype(BF16)
    before = (jnp.arange(tr)[:, None] < jnp.arange(tr)[None, :]).astype(BF16)
    gate_row = jnp.arange(GATE_ROWS)[:, None]
    gate_spread = ((gate_row < 2 * TOP_K) & (gate_row % TOP_K == jnp.arange(TOP_K * LANES)[None, :] // LANES)).astype(BF16)
    b1r = b_mlp1[0].reshape(N_EXPERTS, 1, 2 * D_FF)
    b2r = b_mlp2[0].reshape(N_EXPERTS, 1, D)

    out = None
    b_off = 0
    for Bp in part_batches:
        N = Bp * S
        assert N % tr == 0 and N % tm == 0
        n_assign = N * TOP_K
        n_blocks = -(-n_assign // ROW_BLOCK) + N_EXPERTS
        n_rows = n_blocks * ROW_BLOCK
        nb_pad = -(-n_blocks // LANES) * LANES
        t_off = b_off * S // tm
        r_off = b_off * S // tr

        mb = MEM_BATCHES if Bp % MEM_BATCHES == 0 and b_off % MEM_BATCHES == 0 else 1
        kT, vm = pl.pallas_call(
            _mem_kv_kernel,
            grid=(Bp // mb,),
            in_specs=[pl.BlockSpec((mb, M, D), lambda b: (b + b_off // mb, 0, 0)), _full((1, D)), _full((D, 2 * MEM_W)),
                      _full((1, MEM_W)), _full((MEM_W, MEM_W))],
            out_specs=[pl.BlockSpec((mb, MEM_W, M), lambda b: (b, 0, 0)),
                       pl.BlockSpec((mb, M, MEM_W), lambda b: (b, 0, 0))],
            out_shape=[jax.ShapeDtypeStruct((Bp, MEM_W, M), BF16), jax.ShapeDtypeStruct((Bp, M, MEM_W), BF16)],
            compiler_params=_cparams(("parallel",)),
            name="mem_kv",
        )(mem, row(g_mem_norm[0]), w_kv_b, tile_row(g_ck[0], N_MEM_HEADS), ones256)

        tiles_per_batch = S // tm
        tok = lambda w: pl.BlockSpec((tm, w), lambda i: (i, 0))
        tok_in = lambda w: pl.BlockSpec((tm, w), lambda i: (i + t_off, 0))
        qn, kn, vv, gc = pl.pallas_call(
            _mixer_in_kernel,
            grid=(N // tm,),
            in_specs=[tok_in(D), pl.BlockSpec((ROPE_STACK_ROWS, tm), lambda i: (0, i + t_off)),
                      _full((ROPE_STACK_ROWS, 2 * LANES)),
                      _full((1, D)), _full((D, D_IN_PROJ)),
                      _full((1, DIFF_W)), _full((1, DIFF_W)),
                      _full((1, GMLP_W)), _full((CHUNK, GMLP_GROUPS * CHUNK)), _full((CHUNK, GMLP_W)),
                      _full((1, MEM_W)), _full((MEM_W, MEM_W)),
                      pl.BlockSpec((1, MEM_W, M), lambda i: (i // tiles_per_batch, 0, 0)),
                      pl.BlockSpec((1, M, MEM_W), lambda i: (i // tiles_per_batch, 0, 0))],
            out_specs=[tok(DIFF_W), tok(DIFF_W), tok(DIFF_W), tok(GMLP_W + MEM_W)],
            out_shape=[jax.ShapeDtypeStruct((N, DIFF_W), BF16)] * 3 + [jax.ShapeDtypeStruct((N, GMLP_W + MEM_W), BF16)],
            compiler_params=_cparams(("parallel",)),
            name="mixer_in",
        )(xf, rope_cs, rope_spread, row(g_mix_norm[0]), w_in_b,
          tile_row(g_dq[0], 2 * N_DIFF_HEADS), tile_row(g_dk[0], 2 * N_DIFF_HEADS),
          row(g_sgu[0]), w_sp_lanes, b_sp_lanes, tile_row(g_cq[0], N_MEM_HEADS), ones256, kT, vm)

        tq = ATTN_TILE
        nqp = S // (2 * tq)
        head_q = pl.BlockSpec((2 * tq, DIFF_W), lambda b, i: (b * nqp + i, 0))
        head_kv = pl.BlockSpec((S, DIFF_W), lambda b, i: (b, 0))
        lam_spec = pl.BlockSpec((1, DIFF_HEAD_DIM), lambda b, i: (0, 0))
        nh = N_DIFF_HEADS
        attn = pl.pallas_call(
            _diff_attn_kernel,
            grid=(Bp, nqp),
            in_specs=[head_q, head_kv, head_kv, lam_spec, lam_spec, lam_spec, lam_spec,
                      pl.BlockSpec((DIFF_V_DIM, 1), lambda b, i: (0, 0))],
            out_specs=head_q,
            out_shape=jax.ShapeDtypeStruct((N, DIFF_W), BF16),
            scratch_shapes=[pltpu.VMEM((nh, tq, 2 * tq), F32), pltpu.VMEM((nh, tq, 2 * tq), F32),
                            pltpu.VMEM((nh, 1, 2 * tq), F32), pltpu.VMEM((nh, ATTN_EXT_ROWS, 2 * tq), F32),
                            pltpu.VMEM((nh, ATTN_EXT_ROWS, S), BF16)],
            compiler_params=_cparams(("arbitrary", "arbitrary")),
            name="diff_attn",
        )(qn, kn, vv, row(lambda_q1[0]), row(lambda_k1[0]), row(lambda_q2[0]), row(lambda_k2[0]),
          g_subln[0].reshape(DIFF_V_DIM, 1).astype(F32))

        rtok = lambda w: pl.BlockSpec((tr, w), lambda i: (i, 0))
        rtok_in = lambda w: pl.BlockSpec((tr, w), lambda i: (i + r_off, 0))
        tokT = lambda: pl.BlockSpec((TOP_K, tr), lambda i: (0, i))
        hw = D // 4
        x1, hpa, hpb, idxT, gateT, posT, counts = pl.pallas_call(
            _out_router_kernel,
            grid=(N // tr,),
            in_specs=[rtok_in(D), rtok(DIFF_W), rtok(GMLP_W + MEM_W), _full((D, D)), _full((1, D)),
                      _full((N_EXPERTS, D)), _full((N_EXPERTS, D)), _full((N_EXPERTS, 1)), _full((tr, tr))],
            out_specs=[rtok(D), rtok(hw), rtok(hw), tokT(), pl.BlockSpec((GATE_ROWS, tr), lambda i: (0, i)), tokT(),
                       _full((N_EXPERTS, 1))],
            out_shape=[jax.ShapeDtypeStruct((N, D), F32), jax.ShapeDtypeStruct((N, hw), U32),
                       jax.ShapeDtypeStruct((N, hw), U32), jax.ShapeDtypeStruct((TOP_K, N), I32),
                       jax.ShapeDtypeStruct((GATE_ROWS, N), BF16),
                       jax.ShapeDtypeStruct((TOP_K, N), I32), jax.ShapeDtypeStruct((N_EXPERTS, 1), F32)],
            scratch_shapes=[pltpu.VMEM((N_EXPERTS, 1), F32)],
            compiler_params=_cparams(("arbitrary",)),
            name="out_router",
        )(xf, attn, gc, w_out_b, row(g_ffn_norm[0]), wr_hi, wr_lo, b_router[0].reshape(N_EXPERTS, 1).astype(F32), before)

        destT, block_expert, block_valid, nb_used = pl.pallas_call(
            _dest_kernel,
            grid_spec=pltpu.PrefetchScalarGridSpec(
                num_scalar_prefetch=1,
                grid=(1,),
                in_specs=[pl.BlockSpec((TOP_K, N), lambda i, c: (0, 0)), pl.BlockSpec((TOP_K, N), lambda i, c: (0, 0))],
                out_specs=[pl.BlockSpec((TOP_K, N), lambda i, c: (0, 0)), pl.BlockSpec((1, nb_pad), lambda i, c: (0, 0)),
                           pl.BlockSpec((1, nb_pad), lambda i, c: (0, 0)), pl.BlockSpec((1, LANES), lambda i, c: (0, 0))],
            ),
            out_shape=[jax.ShapeDtypeStruct((TOP_K, N), I32), jax.ShapeDtypeStruct((1, nb_pad), I32),
                       jax.ShapeDtypeStruct((1, nb_pad), I32), jax.ShapeDtypeStruct((1, LANES), I32)],
            compiler_params=_cparams(("arbitrary",)),
            name="dest",
        )(counts.reshape(N_EXPERTS).astype(I32), idxT, posT)

        dest_rows = [destT[k].reshape(1, N) for k in range(TOP_K)]
        xa_buf = _sc_scatter_rows(hpa, dest_rows, n_rows)
        xb_buf = _sc_scatter_rows(hpb, dest_rows, n_rows)

        cnt_i = counts.reshape(N_EXPERTS).astype(I32)
        owner = jnp.where(cnt_i > 0, jnp.arange(N_EXPERTS, dtype=I32), N_EXPERTS)
        later = jnp.concatenate([lax.cummin(owner[::-1])[::-1][1:], jnp.full((1,), N_EXPERTS, I32)])
        next_expert = jnp.where(later < N_EXPERTS, later, -1)
        last = lambda b, be, nbu, nxt, valid: jnp.minimum(b, nbu[0] - 1)
        row_blk = lambda: pl.BlockSpec((ROW_BLOCK, hw), lambda b, be, nbu, nxt, valid: (last(b, be, nbu, nxt, valid), 0))
        y_blk = pl.BlockSpec((ROW_BLOCK, 2 * hw), lambda b, be, nbu, nxt, valid: (last(b, be, nbu, nxt, valid), 0))
        y_buf = pl.pallas_call(
            _expert_ffn_kernel,
            grid_spec=pltpu.PrefetchScalarGridSpec(
                num_scalar_prefetch=4,
                grid=(n_blocks,),
                in_specs=[row_blk(), row_blk(),
                          pl.BlockSpec(memory_space=pl.ANY),
                          pl.BlockSpec((1, 1, 2 * D_FF), lambda b, be, nbu, nxt, valid: (be[b], 0, 0)),
                          pl.BlockSpec(memory_space=pl.ANY),
                          pl.BlockSpec((1, 1, D), lambda b, be, nbu, nxt, valid: (be[b], 0, 0))],
                out_specs=y_blk,
                scratch_shapes=[pltpu.VMEM((D, 2 * D_FF), F32), pltpu.VMEM((D_FF, D), F32),
                                pltpu.VMEM((D, 2 * D_FF), BF16), pltpu.VMEM((D_FF, D), BF16),
                                pltpu.SemaphoreType.DMA((2,))],
            ),
            out_shape=jax.ShapeDtypeStruct((n_rows, 2 * hw), U32),
            compiler_params=_cparams(("arbitrary",)),
            name="expert_ffn",
        )(block_expert[0, :n_blocks], nb_used[0, :1], next_expert, block_valid[0, :n_blocks], xa_buf, xb_buf,
          w_mlp1[0], b1r, w_mlp2[0], b2r)

        yg = _sc_gather_rows(y_buf, destT.reshape(n_assign)).reshape(TOP_K, N, 2 * hw)
        tc = COMBINE_TILE
        prev = () if out is None else (out,)
        out = pl.pallas_call(
            _combine_kernel if out is None else _combine_into_kernel,
            grid=(N // tc,),
            in_specs=[pl.BlockSpec((tc, D), lambda i: (i, 0)),
                      pl.BlockSpec((TOP_K, tc, 2 * hw), lambda i: (0, i, 0)),
                      pl.BlockSpec((GATE_ROWS, tc), lambda i: (0, i)),
                      _full((GATE_ROWS, TOP_K * LANES))] + [pl.BlockSpec(memory_space=pl.ANY)] * len(prev),
            out_specs=pl.BlockSpec((tc, D), lambda i: (i + b_off * S // tc, 0)),
            out_shape=jax.ShapeDtypeStruct((B * S, D), F32),
            input_output_aliases={4: 0} if prev else {},
            compiler_params=_cparams(("parallel",)),
            name="combine",
        )(x1, yg, gateT, gate_spread, *prev)
        b_off += Bp
    return out.reshape(B, S, D)
```

```python
import functools

import jax
import jax.numpy as jnp
from jax import lax
from jax.experimental import pallas as pl
from jax.experimental.pallas import tpu as pltpu
from jax.experimental.pallas import tpu_sc as plsc

F32 = jnp.float32
BF16 = jnp.bfloat16
I32 = jnp.int32
U32 = jnp.uint32

D_MODEL = 1024
N_DIFF_HEADS = 4
DIFF_HEAD_DIM = 64
DIFF_V_DIM = 128
DIFF_W = 512
GMLP_W = 256
GMLP_GROUPS = 4
CHUNK = 128
MEM_W = 256
N_MEM_HEADS = 4
HEAD_GROUP = 64
D_IN_PROJ = 2304
ROPE_THETA = 500000.0
ROT_DIM = 16
ROPE_STACK_ROWS = 2 * ROT_DIM
N_EXPERTS = 32
TOP_K = 4
D_FF = 1024
SWIGLU_LIMIT = 7.0
SWIGLU_ALPHA = 1.702
EPS = 1e-6
LAMBDA_INIT = 0.8 - 0.6

LANES = 128
ROW_BLOCK = 1024
FFN_SUB_ROWS = 256
TOKEN_TILE = 1024
COMBINE_TILE = 1024
MEM_BATCHES = 4
LAST_PART_DIVISOR = 2
ROUTER_TILE = 1024
GATE_ROWS = 16
ATTN_TILE = 256
ATTN_Q_TILES_PER_STEP = 8
ATTN_EXT_ROWS = 128 + 16
VMEM_LIMIT = 56 * 1024 * 1024
NEG_BIG = -1e30
LOG2_E = 1.4426950408889634


def _cparams(sem):
    return pltpu.CompilerParams(dimension_semantics=sem, vmem_limit_bytes=VMEM_LIMIT)


def _dot(a, b):
    return jnp.dot(a, b, preferred_element_type=F32)


def _dot_nt(a, b):
    return lax.dot_general(a, b, (((1,), (1,)), ((), ())), preferred_element_type=F32)


def _rms(x, gain):
    ms = jnp.mean(x * x, axis=-1, keepdims=True)
    return x * lax.rsqrt(ms + EPS) * gain


def _group_rms(t, ones_bd, gain):
    w = ones_bd.shape[0]
    chunks = []
    for j in range(t.shape[1] // w):
        c = t[:, j * w:(j + 1) * w]
        ss = _dot((c * c).astype(BF16), ones_bd)
        chunks.append(c * lax.rsqrt(ss * (1.0 / HEAD_GROUP) + EPS))
    return (chunks[0] if len(chunks) == 1 else jnp.concatenate(chunks, axis=1)) * gain


def _pack_bf16_pairs(v):
    w = v.shape[1] // 2
    bits = lax.bitcast_convert_type(v.astype(BF16).astype(F32), U32)
    return (bits[:, :w] & jnp.uint32(0xFFFF0000)) | (bits[:, w:] >> jnp.uint32(16))


def _unpack_bf16_pairs(words):
    hi = lax.bitcast_convert_type(words & jnp.uint32(0xFFFF0000), F32)
    lo = lax.bitcast_convert_type(words << jnp.uint32(16), F32)
    return hi, lo


def _mem_kv_kernel(mem_ref, gmem_ref, wkv_ref, gck_ref, ones_ref, kT_ref, v_ref):
    for j in range(mem_ref.shape[0]):
        m = _rms(mem_ref[j], gmem_ref[...]).astype(BF16)
        kv = _dot(m, wkv_ref[...])
        k = _group_rms(kv[:, :MEM_W], ones_ref[...], gck_ref[...])
        kT_ref[j] = k.T.astype(BF16)
        v_ref[j] = kv[:, MEM_W:].astype(BF16)


def _gelu_tanh(x):
    return 0.5 * x * (1.0 + jnp.tanh(0.7978845608028654 * (x + 0.044715 * (x * x * x))))


def _mixer_in_kernel(x_ref, cs_ref, spread_ref, gmix_ref, win_ref, gq_ref, gk_ref,
                     gsgu_ref, wsp_ref, bsp_ref, gcq_ref, ones256_ref, kT_ref, vm_ref,
                     q_out, k_out, v_out, gc_out):
    tm = x_ref.shape[0]
    hb = _rms(x_ref[...], gmix_ref[...]).astype(BF16)

    def proj(lo, hi):
        return _dot(hb, win_ref[:, lo:hi])

    lane = lax.broadcasted_iota(I32, (tm, LANES), 1)
    first_half = (lane % HEAD_GROUP) < (ROT_DIM // 2)
    tab = lax.dot_general(cs_ref[...], spread_ref[...], (((0,), (0,)), ((), ())), preferred_element_type=F32)
    cosb = tab[:, :LANES] + jnp.where((lane % HEAD_GROUP) >= ROT_DIM, 1.0, 0.0)
    sinb = tab[:, LANES:]

    def norm_rope(t, gain, out_ref):
        tn = _group_rms(t, ones256_ref[...], gain)
        for j in range(DIFF_W // LANES):
            c = tn[:, j * LANES:(j + 1) * LANES]
            partner = jnp.where(first_half, pltpu.roll(c, LANES - ROT_DIM // 2, 1), pltpu.roll(c, ROT_DIM // 2, 1))
            out_ref[:, j * LANES:(j + 1) * LANES] = (c * cosb + partner * sinb).astype(BF16)

    norm_rope(proj(0, DIFF_W), gq_ref[...] * (DIFF_HEAD_DIM ** -0.5 * LOG2_E), q_out)
    norm_rope(proj(DIFF_W, 2 * DIFF_W), gk_ref[...], k_out)
    v_out[...] = proj(2 * DIFF_W, 3 * DIFF_W).astype(BF16)

    z = _gelu_tanh(proj(3 * DIFF_W, 3 * DIFF_W + 2 * GMLP_W))
    u = z[:, :GMLP_W]
    vg = z[:, GMLP_W:]
    vc = vg - jnp.mean(vg, axis=-1, keepdims=True)
    vgn = (vc * lax.rsqrt(jnp.mean(vc * vc, axis=-1, keepdims=True) + EPS) * gsgu_ref[...]).astype(BF16)
    wrow = lax.broadcasted_iota(I32, (CHUNK, GMLP_GROUPS * CHUNK), 0)
    wcol = lax.broadcasted_iota(I32, (CHUNK, GMLP_GROUPS * CHUNK), 1) % CHUNK
    w_causal = jnp.where(wcol <= wrow, wsp_ref[...], 0.0).astype(BF16)
    grp = lax.broadcasted_iota(I32, (CHUNK, GMLP_W), 1) // HEAD_GROUP
    zero_b = jnp.zeros((CHUNK, GMLP_W), BF16)
    for r in range(tm // CHUNK):
        vchunk = vgn[r * CHUNK:(r + 1) * CHUNK, :]
        v_bd = jnp.concatenate([jnp.where(grp == g, vchunk, zero_b) for g in range(GMLP_GROUPS)], axis=0)
        mixed = _dot(w_causal, v_bd) + bsp_ref[...]
        gc_out[r * CHUNK:(r + 1) * CHUNK, 0:GMLP_W] = (u[r * CHUNK:(r + 1) * CHUNK, :] * mixed).astype(BF16)

    pc = proj(3 * DIFF_W + 2 * GMLP_W, D_IN_PROJ)
    qc = _group_rms(pc, ones256_ref[...], gcq_ref[...] * (HEAD_GROUP ** -0.5)).astype(BF16)
    hgrp = lax.broadcasted_iota(I32, (tm, MEM_W), 1) // HEAD_GROUP
    zero_q = jnp.zeros((tm, MEM_W), BF16)
    q_st = jnp.concatenate([jnp.where(hgrp == h, qc, zero_q) for h in range(N_MEM_HEADS)], axis=0)
    s = _dot(q_st, kT_ref[0])
    p = jnp.exp(s - jnp.max(s, axis=-1, keepdims=True))
    o = _dot(p.astype(BF16), vm_ref[0]) / jnp.sum(p, axis=-1, keepdims=True)
    c = jnp.zeros((tm, MEM_W), F32)
    for h in range(N_MEM_HEADS):
        c = c + jnp.where(hgrp == h, o[h * tm:(h + 1) * tm, :], 0.0)
    gc_out[:, GMLP_W:GMLP_W + MEM_W] = c.astype(BF16)


def _diff_attn_kernel(q_ref, k_ref, v_ref, lq1_ref, lk1_ref, lq2_ref, lk2_ref, gsub_ref, o_ref,
                      s0_ref, s1_ref, m_ref, acc_ref, vT_ref):
    tq = ATTN_TILE
    tiles_per_step = q_ref.shape[0] // tq
    assert tiles_per_step % 2 == 0
    seq = k_ref.shape[0]
    step = pl.program_id(1)
    feat = lax.broadcasted_iota(I32, (DIFF_V_DIM, tq), 0)
    heads = range(N_DIFF_HEADS)
    hl = lambda h: slice(h * DIFF_V_DIM, (h + 1) * DIFF_V_DIM)
    ext_rows = vT_ref.shape[1]

    @pl.when(step == 0)
    def _():
        ones_row = jnp.where(lax.broadcasted_iota(I32, (ext_rows - DIFF_V_DIM, seq), 0) == 0, 1.0, 0.0).astype(BF16)
        for h in heads:
            for c in range(seq // tq):
                vT_ref[h, 0:DIFF_V_DIM, c * tq:(c + 1) * tq] = v_ref[c * tq:(c + 1) * tq, hl(h)].T
            vT_ref[h, DIFF_V_DIM:ext_rows, :] = ones_row

    @pl.when(jnp.logical_and(pl.program_id(0) == 0, step == 0))
    def _():
        acc_ref[...] = jnp.zeros(acc_ref.shape, F32)

    lam = (jnp.exp(jnp.sum(lq1_ref[...] * lk1_ref[...], axis=-1, keepdims=True))
           - jnp.exp(jnp.sum(lq2_ref[...] * lk2_ref[...], axis=-1, keepdims=True)) + LAMBDA_INIT)

    def q_tile(j, first_ref, other_ref):
        i = tiles_per_step * step + j
        q_rows = slice(j * tq, (j + 1) * tq)

        def stacked_qT(h):
            qT = q_ref[q_rows, hl(h)].T
            zero = jnp.zeros_like(qT)
            return jnp.concatenate(
                [jnp.where(feat < DIFF_HEAD_DIM, qT, zero), jnp.where(feat >= DIFF_HEAD_DIM, qT, zero)], axis=1)

        qsT = [stacked_qT(h) for h in heads]

        def scores(t, s_ref):
            rows = pl.ds(pl.multiple_of(t * tq, tq), tq)
            for h in heads:
                s_ref[h] = _dot(k_ref[rows, hl(h)], qsT[h])

        def update(t, s_ref, causal):
            cols = pl.ds(pl.multiple_of(t * tq, tq), tq)
            for h in heads:
                s = s_ref[h]
                if causal:
                    key = lax.broadcasted_iota(I32, (tq, 2 * tq), 0)
                    qry = lax.broadcasted_iota(I32, (tq, 2 * tq), 1) % tq
                    s = jnp.where(key <= qry, s, NEG_BIG)
                m = m_ref[h]
                m_new = jnp.maximum(m, jnp.max(s, axis=0, keepdims=True))
                alpha = jnp.exp2(m - m_new)
                p = jnp.exp2(s - m_new)
                m_ref[h] = m_new
                acc_ref[h] = alpha * acc_ref[h] + _dot(vT_ref[h, :, cols], p.astype(BF16))

        m_ref[...] = jnp.full(m_ref.shape, NEG_BIG, F32)
        scores(0, first_ref)

        def two_tiles(pidx, carry):
            t = 2 * pidx
            scores(t + 1, other_ref)
            update(t, first_ref, False)
            scores(t + 2, first_ref)
            update(t + 1, other_ref, False)
            return carry

        lax.fori_loop(0, (tiles_per_step // 2) * step + j // 2, two_tiles, 0)

        if j % 2 == 0:
            update(i, first_ref, True)
        else:
            scores(i, other_ref)
            update(i - 1, first_ref, False)
            update(i, other_ref, True)

        for h in heads:
            on = acc_ref[h, 0:DIFF_V_DIM, :] * (1.0 / acc_ref[h, DIFF_V_DIM:DIFF_V_DIM + 1, :])
            o = on[:, :tq] - lam * on[:, tq:]
            ms = jnp.mean(o * o, axis=0, keepdims=True)
            o = o * lax.rsqrt(ms + EPS) * gsub_ref[...] * (1.0 - LAMBDA_INIT)
            o_ref[q_rows, hl(h)] = o.T.astype(BF16)

    first_ref, other_ref = s0_ref, s1_ref
    for j in range(tiles_per_step):
        q_tile(j, first_ref, other_ref)
        last_read = first_ref if j % 2 == 0 else other_ref
        first_ref, other_ref = (other_ref, first_ref) if last_read is first_ref else (first_ref, other_ref)


def _out_router_kernel(x_ref, a_ref, gc_ref, wo_ref, gffn_ref, wrh_ref, wrl_ref, br_ref, before_ref,
                       x1_out, hpa_out, hpb_out, idx_out, gate_out, pos_out, cnt_out, carry_ref):
    tm = x_ref.shape[0]

    @pl.when(pl.program_id(0) == 0)
    def _():
        carry_ref[...] = jnp.zeros_like(carry_ref)

    sub = before_ref.shape[0]
    wr_stack = jnp.concatenate([wrh_ref[...], wrl_ref[...]], axis=0)
    eio = lax.broadcasted_iota(I32, (N_EXPERTS, sub), 0)
    before_b = before_ref[...]
    carry = carry_ref[...]
    for r in range(tm // sub):
        rows = slice(r * sub, (r + 1) * sub)
        mix = jnp.concatenate([a_ref[rows, :], gc_ref[rows, :]], axis=1)
        x1 = x_ref[rows, :] + _dot(mix, wo_ref[...])
        x1_out[rows, :] = x1
        h2 = _rms(x1, gffn_ref[...])
        hb = h2.astype(BF16)
        hpa_out[rows, :] = _pack_bf16_pairs(h2[:, :D_MODEL // 2])
        hpb_out[rows, :] = _pack_bf16_pairs(h2[:, D_MODEL // 2:])

        h_lo = (h2 - hb.astype(F32)).astype(BF16)
        both = _dot_nt(wr_stack, hb)
        logits = (both[:N_EXPERTS] + both[N_EXPERTS:]) + _dot_nt(wrh_ref[...], h_lo) + br_ref[...]

        vals, idxs, sels = [], [], []
        cur = logits
        for _ in range(TOP_K):
            m = jnp.max(cur, axis=0, keepdims=True)
            ik = jnp.min(jnp.where(cur == m, eio, N_EXPERTS), axis=0, keepdims=True)
            sel = eio == ik
            cur = jnp.where(sel, -jnp.inf, cur)
            vals.append(m)
            idxs.append(ik)
            sels.append(sel)
        es = [jnp.exp(v - vals[0]) for v in vals]
        tot = es[0] + es[1] + es[2] + es[3]
        gates = jnp.concatenate([e / tot for e in es], axis=0)
        g_hi = gates.astype(BF16)
        g_lo = (gates - g_hi.astype(F32)).astype(BF16)
        gate_out[:, rows] = jnp.concatenate(
            [g_hi, g_lo, jnp.zeros((GATE_ROWS - 2 * TOP_K, sub), BF16)], axis=0)
        idx_out[:, rows] = jnp.concatenate(idxs, axis=0)

        cnt = jnp.zeros((N_EXPERTS, sub), F32)
        for sel in sels:
            cnt = cnt + jnp.where(sel, 1.0, 0.0)
        base = carry + _dot(cnt.astype(BF16), before_b)
        pos_out[:, rows] = jnp.concatenate(
            [jnp.sum(jnp.where(sel, base, 0.0), axis=0, keepdims=True) for sel in sels], axis=0).astype(I32)
        carry = carry + jnp.sum(cnt, axis=1, keepdims=True)
    carry_ref[...] = carry
    cnt_out[...] = carry


def _dest_kernel(cnt_ref, idx_ref, pos_ref, dest_out, be_out, valid_out, nbu_out):
    idx = idx_ref[...]
    dest = pos_ref[...]
    bidx = lax.broadcasted_iota(I32, be_out.shape, 1)
    be = jnp.zeros(be_out.shape, I32)
    valid = jnp.zeros(be_out.shape, I32)
    run = jnp.int32(0)
    for e in range(N_EXPERTS):
        dest = dest + jnp.where(idx == e, run, 0)
        first_block = run // ROW_BLOCK
        run = run + ((cnt_ref[e] + (ROW_BLOCK - 1)) // ROW_BLOCK) * ROW_BLOCK
        mine = jnp.logical_and(bidx >= first_block, bidx < run // ROW_BLOCK)
        valid = jnp.where(mine, jnp.clip(cnt_ref[e] - (bidx - first_block) * ROW_BLOCK, 0, ROW_BLOCK), valid)
        be = be + jnp.where(bidx >= run // ROW_BLOCK, 1, 0)
    dest_out[...] = dest
    be_out[...] = jnp.minimum(be, N_EXPERTS - 1)
    valid_out[...] = valid
    nbu_out[...] = jnp.zeros(nbu_out.shape, I32) + run // ROW_BLOCK


def _expert_ffn_kernel(be_ref, nbu_ref, nxt_ref, valid_ref, xa_ref, xb_ref, w1_hbm, b1_ref, w2_hbm, b2_ref,
                       y_ref, w1f_ref, w2f_ref, w1s_ref, w2s_ref, sem_ref):
    b = pl.program_id(0)

    def weight_copies(e):
        return (pltpu.make_async_copy(w1_hbm.at[e], w1f_ref, sem_ref.at[0]),
                pltpu.make_async_copy(w2_hbm.at[e], w2f_ref, sem_ref.at[1]))

    def mlp(rows):
        parts = _unpack_bf16_pairs(xa_ref[0:rows, :]) + _unpack_bf16_pairs(xb_ref[0:rows, :])
        xrow = jnp.concatenate([p.astype(BF16) for p in parts], axis=1)
        hm = _dot(xrow, w1s_ref[...]) + b1_ref[0]
        glu = jnp.minimum(hm[:, :D_FF], SWIGLU_LIMIT)
        lin = jnp.clip(hm[:, D_FF:], -SWIGLU_LIMIT, SWIGLU_LIMIT)
        act = glu * jax.nn.sigmoid(SWIGLU_ALPHA * glu) * (lin + 1.0)
        y = _dot(act.astype(BF16), w2s_ref[...]) + b2_ref[0]
        hw = D_MODEL // 4
        y_ref[0:rows, 0:hw] = _pack_bf16_pairs(y[:, :D_MODEL // 2])
        y_ref[0:rows, hw:] = _pack_bf16_pairs(y[:, D_MODEL // 2:])

    @pl.when(b < nbu_ref[0])
    def _():
        e = be_ref[b]

        @pl.when(b == 0)
        def _():
            for cp in weight_copies(e):
                cp.start()

        @pl.when(jnp.logical_or(b == 0, e != be_ref[jnp.maximum(b - 1, 0)]))
        def _():
            for cp in weight_copies(e):
                cp.wait()
            w1s_ref[...] = w1f_ref[...].astype(BF16)
            w2s_ref[...] = w2f_ref[...].astype(BF16)
            nxt = nxt_ref[e]

            @pl.when(nxt >= 0)
            def _():
                for cp in weight_copies(nxt):
                    cp.start()

        n_sub = (valid_ref[b] + (FFN_SUB_ROWS - 1)) // FFN_SUB_ROWS
        for k in range(1, ROW_BLOCK // FFN_SUB_ROWS + 1):
            pl.when(n_sub == k)(functools.partial(mlp, k * FFN_SUB_ROWS))


SC_WINDOW = 128
SC_GATHER_WINDOW = 64


def _sc_mesh():
    return plsc.VectorSubcoreMesh(core_axis_name="c", subcore_axis_name="s")


def _sc_gather_rows(table, idx):
    n = idx.shape[0]
    width = table.shape[1]
    win = SC_GATHER_WINDOW

    @functools.partial(pl.kernel, out_type=jax.ShapeDtypeStruct((n, width), table.dtype), mesh=_sc_mesh(),
                       scratch_types=[])
    def gather_kernel(t_hbm, i_hbm, o_hbm):
        def body(i_vmem, o_vmem):
            pltpu.sync_copy(t_hbm.at[i_vmem.at[0]], o_vmem)

        pltpu.emit_pipeline(
            body,
            grid=(n // win,),
            in_specs=[pl.BlockSpec((1, win), lambda i: (i, 0))],
            out_specs=[pl.BlockSpec((win, width), lambda i: (i, 0))],
            core_axis_name=("c", "s"),
            dimension_semantics=(pltpu.PARALLEL,),
        )(i_hbm, o_hbm)

    return gather_kernel(table, idx.reshape(n // win, win))


def _sc_scatter_rows(rows, idx_rows, n_out):
    n, width = rows.shape

    @functools.partial(pl.kernel, out_type=jax.ShapeDtypeStruct((n_out, width), rows.dtype), mesh=_sc_mesh(),
                       scratch_types=[])
    def scatter_kernel(r_hbm, *refs):
        i_hbms, o_hbm = refs[:-1], refs[-1]

        def body(r_vmem, *i_vmems):
            for i_vmem in i_vmems:
                pltpu.sync_copy(r_vmem, o_hbm.at[i_vmem.at[0]])

        pltpu.emit_pipeline(
            body,
            grid=(n // SC_WINDOW,),
            in_specs=[pl.BlockSpec((SC_WINDOW, width), lambda i: (i, 0))]
            + [pl.BlockSpec((1, SC_WINDOW), lambda i: (0, i)) for _ in i_hbms],
            out_specs=[],
            core_axis_name=("c", "s"),
            dimension_semantics=(pltpu.PARALLEL,),
        )(r_hbm, *i_hbms)

    return scatter_kernel(rows, *idx_rows)


def _combine_kernel(x1_ref, yg_ref, gate_ref, spread_ref, o_ref):
    q = D_MODEL // 4
    x1 = x1_ref[...]
    acc = [x1[:, j * q:(j + 1) * q] for j in range(4)]
    gfull = lax.dot_general(gate_ref[...], spread_ref[...], (((0,), (0,)), ((), ())), preferred_element_type=F32)
    for k in range(TOP_K):
        parts = _unpack_bf16_pairs(yg_ref[k, :, 0:q]) + _unpack_bf16_pairs(yg_ref[k, :, q:])
        g = jnp.tile(gfull[:, k * LANES:(k + 1) * LANES], (1, q // LANES))
        acc = [a + g * p for a, p in zip(acc, parts)]
    for j in range(4):
        o_ref[:, j * q:(j + 1) * q] = acc[j]


def _combine_into_kernel(x1_ref, yg_ref, gate_ref, spread_ref, prev_ref, o_ref):
    del prev_ref
    _combine_kernel(x1_ref, yg_ref, gate_ref, spread_ref, o_ref)


def _block_diag_ones(width):
    r = jnp.arange(width) // HEAD_GROUP
    return (r[:, None] == r[None, :]).astype(BF16)


def _rope_tables(positions):
    half = ROT_DIM // 2
    inv_freq = ROPE_THETA ** (-jnp.arange(0, ROT_DIM, 2, dtype=F32) / ROT_DIM)
    ang = inv_freq[:, None] * positions.astype(F32).reshape(1, -1)
    cs = jnp.concatenate([jnp.cos(ang), jnp.sin(ang)], axis=0)
    cs_hi = cs.astype(BF16)
    cs_lo = (cs - cs_hi.astype(F32)).astype(BF16)
    lane = jnp.arange(LANES) % HEAD_GROUP
    j = jnp.arange(half)[:, None]
    lo_half = (lane[None, :] == j).astype(F32)
    hi_half = (lane[None, :] == j + half).astype(F32)
    spread = jnp.concatenate([
        jnp.concatenate([lo_half + hi_half, jnp.zeros((half, LANES), F32)], axis=1),
        jnp.concatenate([jnp.zeros((half, LANES), F32), hi_half - lo_half], axis=1)], axis=0)
    return jnp.concatenate([cs_hi, cs_lo], axis=0), jnp.concatenate([spread, spread], axis=0).astype(BF16)


def _full(shape):
    return pl.BlockSpec(shape, lambda *_: (0,) * len(shape))


def kernel(x, mem, positions, g_mix_norm, w_in, g_dq, g_dk, lambda_q1, lambda_k1, lambda_q2, lambda_k2, g_subln, g_sgu, w_spatial, b_spatial, g_mem_norm, w_mem_kv, g_cq, g_ck, w_out, g_ffn_norm, w_router, b_router, w_mlp1, b_mlp1, w_mlp2, b_mlp2):
    B, S, D = x.shape
    M = mem.shape[1]
    tm = TOKEN_TILE
    tr = ROUTER_TILE
    assert D == D_MODEL and S % tm == 0 and S % (ATTN_Q_TILES_PER_STEP * ATTN_TILE) == 0 and g_mix_norm.shape[0] == 1
    last_batches = B // LAST_PART_DIVISOR
    part_batches = (B - last_batches, last_batches) if last_batches > 0 and S % tr == 0 else (B,)

    xf = x.reshape(B * S, D)
    rope_cs, rope_spread = _rope_tables(positions)
    ones256 = _block_diag_ones(MEM_W)
    row = lambda v: v.reshape(1, -1).astype(F32)
    tile_row = lambda v, reps: jnp.tile(v.reshape(1, -1).astype(F32), (1, reps))
    w_in_b, w_out_b, w_kv_b = w_in[0].astype(BF16), w_out[0].astype(BF16), w_mem_kv[0].astype(BF16)
    w_sp_lanes = jnp.transpose(w_spatial[0], (1, 0, 2)).reshape(CHUNK, GMLP_GROUPS * CHUNK)
    b_sp_lanes = jnp.repeat(b_spatial[0].T, HEAD_GROUP, axis=1)
    wr = w_router[0].T.astype(F32)
    wr_hi = wr.astype(BF16)
    wr_lo = (wr - wr_hi.astype(F32)).astype(BF16)
    before = (jnp.arange(tr)[:, None] < jnp.arange(tr)[None, :]).astype(BF16)
    gate_row = jnp.arange(GATE_ROWS)[:, None]
    gate_spread = ((gate_row < 2 * TOP_K) & (gate_row % TOP_K == jnp.arange(TOP_K * LANES)[None, :] // LANES)).astype(BF16)
    b1r = b_mlp1[0].reshape(N_EXPERTS, 1, 2 * D_FF)
    b2r = b_mlp2[0].reshape(N_EXPERTS, 1, D)

    out = None
    b_off = 0
    for Bp in part_batches:
        N = Bp * S
        assert N % tr == 0 and N % tm == 0
        n_assign = N * TOP_K
        n_blocks = -(-n_assign // ROW_BLOCK) + N_EXPERTS
        n_rows = n_blocks * ROW_BLOCK
        nb_pad = -(-n_blocks // LANES) * LANES
        t_off = b_off * S // tm
        r_off = b_off * S // tr

        mb = MEM_BATCHES if Bp % MEM_BATCHES == 0 and b_off % MEM_BATCHES == 0 else 1
        kT, vm = pl.pallas_call(
            _mem_kv_kernel,
            grid=(Bp // mb,),
            in_specs=[pl.BlockSpec((mb, M, D), lambda b: (b + b_off // mb, 0, 0)), _full((1, D)), _full((D, 2 * MEM_W)),
                      _full((1, MEM_W)), _full((MEM_W, MEM_W))],
            out_specs=[pl.BlockSpec((mb, MEM_W, M), lambda b: (b, 0, 0)),
                       pl.BlockSpec((mb, M, MEM_W), lambda b: (b, 0, 0))],
            out_shape=[jax.ShapeDtypeStruct((Bp, MEM_W, M), BF16), jax.ShapeDtypeStruct((Bp, M, MEM_W), BF16)],
            compiler_params=_cparams(("parallel",)),
            name="mem_kv",
        )(mem, row(g_mem_norm[0]), w_kv_b, tile_row(g_ck[0], N_MEM_HEADS), ones256)

        tiles_per_batch = S // tm
        tok = lambda w: pl.BlockSpec((tm, w), lambda i: (i, 0))
        tok_in = lambda w: pl.BlockSpec((tm, w), lambda i: (i + t_off, 0))
        qn, kn, vv, gc = pl.pallas_call(
            _mixer_in_kernel,
            grid=(N // tm,),
            in_specs=[tok_in(D), pl.BlockSpec((ROPE_STACK_ROWS, tm), lambda i: (0, i + t_off)),
                      _full((ROPE_STACK_ROWS, 2 * LANES)),
                      _full((1, D)), _full((D, D_IN_PROJ)),
                      _full((1, DIFF_W)), _full((1, DIFF_W)),
                      _full((1, GMLP_W)), _full((CHUNK, GMLP_GROUPS * CHUNK)), _full((CHUNK, GMLP_W)),
                      _full((1, MEM_W)), _full((MEM_W, MEM_W)),
                      pl.BlockSpec((1, MEM_W, M), lambda i: (i // tiles_per_batch, 0, 0)),
                      pl.BlockSpec((1, M, MEM_W), lambda i: (i // tiles_per_batch, 0, 0))],
            out_specs=[tok(DIFF_W), tok(DIFF_W), tok(DIFF_W), tok(GMLP_W + MEM_W)],
            out_shape=[jax.ShapeDtypeStruct((N, DIFF_W), BF16)] * 3 + [jax.ShapeDtypeStruct((N, GMLP_W + MEM_W), BF16)],
            compiler_params=_cparams(("parallel",)),
            name="mixer_in",
        )(xf, rope_cs, rope_spread, row(g_mix_norm[0]), w_in_b,
          tile_row(g_dq[0], 2 * N_DIFF_HEADS), tile_row(g_dk[0], 2 * N_DIFF_HEADS),
          row(g_sgu[0]), w_sp_lanes, b_sp_lanes, tile_row(g_cq[0], N_MEM_HEADS), ones256, kT, vm)

        tq = ATTN_TILE
        q_rows = ATTN_Q_TILES_PER_STEP * tq
        nqp = S // q_rows
        head_q = pl.BlockSpec((q_rows, DIFF_W), lambda b, i: (b * nqp + i, 0))
        head_kv = pl.BlockSpec((S, DIFF_W), lambda b, i: (b, 0))
        lam_spec = pl.BlockSpec((1, DIFF_HEAD_DIM), lambda b, i: (0, 0))
        nh = N_DIFF_HEADS
        attn = pl.pallas_call(
            _diff_attn_kernel,
            grid=(Bp, nqp),
            in_specs=[head_q, head_kv, head_kv, lam_spec, lam_spec, lam_spec, lam_spec,
                      pl.BlockSpec((DIFF_V_DIM, 1), lambda b, i: (0, 0))],
            out_specs=head_q,
            out_shape=jax.ShapeDtypeStruct((N, DIFF_W), BF16),
            scratch_shapes=[pltpu.VMEM((nh, tq, 2 * tq), F32), pltpu.VMEM((nh, tq, 2 * tq), F32),
                            pltpu.VMEM((nh, 1, 2 * tq), F32), pltpu.VMEM((nh, ATTN_EXT_ROWS, 2 * tq), F32),
                            pltpu.VMEM((nh, ATTN_EXT_ROWS, S), BF16)],
            compiler_params=_cparams(("arbitrary", "arbitrary")),
            name="diff_attn",
        )(qn, kn, vv, row(lambda_q1[0]), row(lambda_k1[0]), row(lambda_q2[0]), row(lambda_k2[0]),
          g_subln[0].reshape(DIFF_V_DIM, 1).astype(F32))

        rtok = lambda w: pl.BlockSpec((tr, w), lambda i: (i, 0))
        rtok_in = lambda w: pl.BlockSpec((tr, w), lambda i: (i + r_off, 0))
        tokT = lambda: pl.BlockSpec((TOP_K, tr), lambda i: (0, i))
        hw = D // 4
        x1, hpa, hpb, idxT, gateT, posT, counts = pl.pallas_call(
            _out_router_kernel,
            grid=(N // tr,),
            in_specs=[rtok_in(D), rtok(DIFF_W), rtok(GMLP_W + MEM_W), _full((D, D)), _full((1, D)),
                      _full((N_EXPERTS, D)), _full((N_EXPERTS, D)), _full((N_EXPERTS, 1)), _full((tr, tr))],
            out_specs=[rtok(D), rtok(hw), rtok(hw), tokT(), pl.BlockSpec((GATE_ROWS, tr), lambda i: (0, i)), tokT(),
                       _full((N_EXPERTS, 1))],
            out_shape=[jax.ShapeDtypeStruct((N, D), F32), jax.ShapeDtypeStruct((N, hw), U32),
                       jax.ShapeDtypeStruct((N, hw), U32), jax.ShapeDtypeStruct((TOP_K, N), I32),
                       jax.ShapeDtypeStruct((GATE_ROWS, N), BF16),
                       jax.ShapeDtypeStruct((TOP_K, N), I32), jax.ShapeDtypeStruct((N_EXPERTS, 1), F32)],
            scratch_shapes=[pltpu.VMEM((N_EXPERTS, 1), F32)],
            compiler_params=_cparams(("arbitrary",)),
            name="out_router",
        )(xf, attn, gc, w_out_b, row(g_ffn_norm[0]), wr_hi, wr_lo, b_router[0].reshape(N_EXPERTS, 1).astype(F32), before)

        destT, block_expert, block_valid, nb_used = pl.pallas_call(
            _dest_kernel,
            grid_spec=pltpu.PrefetchScalarGridSpec(
                num_scalar_prefetch=1,
                grid=(1,),
                in_specs=[pl.BlockSpec((TOP_K, N), lambda i, c: (0, 0)), pl.BlockSpec((TOP_K, N), lambda i, c: (0, 0))],
                out_specs=[pl.BlockSpec((TOP_K, N), lambda i, c: (0, 0)), pl.BlockSpec((1, nb_pad), lambda i, c: (0, 0)),
                           pl.BlockSpec((1, nb_pad), lambda i, c: (0, 0)), pl.BlockSpec((1, LANES), lambda i, c: (0, 0))],
            ),
            out_shape=[jax.ShapeDtypeStruct((TOP_K, N), I32), jax.ShapeDtypeStruct((1, nb_pad), I32),
                       jax.ShapeDtypeStruct((1, nb_pad), I32), jax.ShapeDtypeStruct((1, LANES), I32)],
            compiler_params=_cparams(("arbitrary",)),
            name="dest",
        )(counts.reshape(N_EXPERTS).astype(I32), idxT, posT)

        dest_rows = [destT[k].reshape(1, N) for k in range(TOP_K)]
        xa_buf = _sc_scatter_rows(hpa, dest_rows, n_rows)
        xb_buf = _sc_scatter_rows(hpb, dest_rows, n_rows)

        cnt_i = counts.reshape(N_EXPERTS).astype(I32)
        owner = jnp.where(cnt_i > 0, jnp.arange(N_EXPERTS, dtype=I32), N_EXPERTS)
        later = jnp.concatenate([lax.cummin(owner[::-1])[::-1][1:], jnp.full((1,), N_EXPERTS, I32)])
        next_expert = jnp.where(later < N_EXPERTS, later, -1)
        last = lambda b, be, nbu, nxt, valid: jnp.minimum(b, nbu[0] - 1)
        row_blk = lambda: pl.BlockSpec((ROW_BLOCK, hw), lambda b, be, nbu, nxt, valid: (last(b, be, nbu, nxt, valid), 0))
        y_blk = pl.BlockSpec((ROW_BLOCK, 2 * hw), lambda b, be, nbu, nxt, valid: (last(b, be, nbu, nxt, valid), 0))
        y_buf = pl.pallas_call(
            _expert_ffn_kernel,
            grid_spec=pltpu.PrefetchScalarGridSpec(
                num_scalar_prefetch=4,
                grid=(n_blocks,),
                in_specs=[row_blk(), row_blk(),
                          pl.BlockSpec(memory_space=pl.ANY),
                          pl.BlockSpec((1, 1, 2 * D_FF), lambda b, be, nbu, nxt, valid: (be[b], 0, 0)),
                          pl.BlockSpec(memory_space=pl.ANY),
                          pl.BlockSpec((1, 1, D), lambda b, be, nbu, nxt, valid: (be[b], 0, 0))],
                out_specs=y_blk,
                scratch_shapes=[pltpu.VMEM((D, 2 * D_FF), F32), pltpu.VMEM((D_FF, D), F32),
                                pltpu.VMEM((D, 2 * D_FF), BF16), pltpu.VMEM((D_FF, D), BF16),
                                pltpu.SemaphoreType.DMA((2,))],
            ),
            out_shape=jax.ShapeDtypeStruct((n_rows, 2 * hw), U32),
            compiler_params=_cparams(("arbitrary",)),
            name="expert_ffn",
        )(block_expert[0, :n_blocks], nb_used[0, :1], next_expert, block_valid[0, :n_blocks], xa_buf, xb_buf,
          w_mlp1[0], b1r, w_mlp2[0], b2r)

        yg = _sc_gather_rows(y_buf, destT.reshape(n_assign)).reshape(TOP_K, N, 2 * hw)
        tc = COMBINE_TILE
        prev = () if out is None else (out,)
        out = pl.pallas_call(
            _combine_kernel if out is None else _combine_into_kernel,
            grid=(N // tc,),
            in_specs=[pl.BlockSpec((tc, D), lambda i: (i, 0)),
                      pl.BlockSpec((TOP_K, tc, 2 * hw), lambda i: (0, i, 0)),
                      pl.BlockSpec((GATE_ROWS, tc), lambda i: (0, i)),
                      _full((GATE_ROWS, TOP_K * LANES))] + [pl.BlockSpec(memory_space=pl.ANY)] * len(prev),
            out_specs=pl.BlockSpec((tc, D), lambda i: (i + b_off * S // tc, 0)),
            out_shape=jax.ShapeDtypeStruct((B * S, D), F32),
            input_output_aliases={4: 0} if prev else {},
            compiler_params=_cparams(("parallel",)),
            name="combine",
        )(x1, yg, gateT, gate_spread, *prev)
        b_off += Bp
    return out.reshape(B, S, D)
```

```python
import functools

import jax
import jax.numpy as jnp
from jax import lax
from jax.experimental import pallas as pl
from jax.experimental.pallas import tpu as pltpu
from jax.experimental.pallas import tpu_sc as plsc

F32 = jnp.float32
BF16 = jnp.bfloat16
I32 = jnp.int32
U32 = jnp.uint32

D_MODEL = 1024
N_DIFF_HEADS = 4
DIFF_HEAD_DIM = 64
DIFF_V_DIM = 128
DIFF_W = 512
GMLP_W = 256
GMLP_GROUPS = 4
CHUNK = 128
MEM_W = 256
N_MEM_HEADS = 4
HEAD_GROUP = 64
D_IN_PROJ = 2304
ROPE_THETA = 500000.0
ROT_DIM = 16
ROPE_STACK_ROWS = 2 * ROT_DIM
N_EXPERTS = 32
TOP_K = 4
D_FF = 1024
SWIGLU_LIMIT = 7.0
SWIGLU_ALPHA = 1.702
EPS = 1e-6
LAMBDA_INIT = 0.8 - 0.6

LANES = 128
ROW_BLOCK = 1024
FFN_SUB_ROWS = 128
TOKEN_TILE = 1024
COMBINE_TILE = 1024
MEM_BATCHES = 4
LAST_PART_DIVISOR = 2
ROUTER_TILE = 1024
GATE_ROWS = 16
ATTN_TILE = 256
ATTN_Q_TILES_PER_STEP = 8
ATTN_EXT_ROWS = 128 + 16
VMEM_LIMIT = 56 * 1024 * 1024
NEG_BIG = -1e30
LOG2_E = 1.4426950408889634


def _cparams(sem):
    return pltpu.CompilerParams(dimension_semantics=sem, vmem_limit_bytes=VMEM_LIMIT)


def _dot(a, b):
    return jnp.dot(a, b, preferred_element_type=F32)


def _dot_nt(a, b):
    return lax.dot_general(a, b, (((1,), (1,)), ((), ())), preferred_element_type=F32)


def _rms(x, gain):
    ms = jnp.mean(x * x, axis=-1, keepdims=True)
    return x * lax.rsqrt(ms + EPS) * gain


def _group_rms(t, ones_bd, gain):
    w = ones_bd.shape[0]
    chunks = []
    for j in range(t.shape[1] // w):
        c = t[:, j * w:(j + 1) * w]
        ss = _dot((c * c).astype(BF16), ones_bd)
        chunks.append(c * lax.rsqrt(ss * (1.0 / HEAD_GROUP) + EPS))
    return (chunks[0] if len(chunks) == 1 else jnp.concatenate(chunks, axis=1)) * gain


def _pack_bf16_pairs(v):
    w = v.shape[1] // 2
    bits = lax.bitcast_convert_type(v.astype(BF16).astype(F32), U32)
    return (bits[:, :w] & jnp.uint32(0xFFFF0000)) | (bits[:, w:] >> jnp.uint32(16))


def _unpack_bf16_pairs(words):
    hi = lax.bitcast_convert_type(words & jnp.uint32(0xFFFF0000), F32)
    lo = lax.bitcast_convert_type(words << jnp.uint32(16), F32)
    return hi, lo


def _mem_kv_kernel(mem_ref, gmem_ref, wkv_ref, gck_ref, ones_ref, kT_ref, v_ref):
    for j in range(mem_ref.shape[0]):
        m = _rms(mem_ref[j], gmem_ref[...]).astype(BF16)
        kv = _dot(m, wkv_ref[...])
        k = _group_rms(kv[:, :MEM_W], ones_ref[...], gck_ref[...])
        kT_ref[j] = k.T.astype(BF16)
        v_ref[j] = kv[:, MEM_W:].astype(BF16)


def _gelu_tanh(x):
    return 0.5 * x * (1.0 + jnp.tanh(0.7978845608028654 * (x + 0.044715 * (x * x * x))))


def _mixer_in_kernel(x_ref, cs_ref, spread_ref, gmix_ref, win_ref, gq_ref, gk_ref,
                     gsgu_ref, wsp_ref, bsp_ref, gcq_ref, ones256_ref, kT_ref, vm_ref,
                     q_out, k_out, v_out, gc_out):
    tm = x_ref.shape[0]
    hb = _rms(x_ref[...], gmix_ref[...]).astype(BF16)

    def proj(lo, hi):
        return _dot(hb, win_ref[:, lo:hi])

    lane = lax.broadcasted_iota(I32, (tm, LANES), 1)
    first_half = (lane % HEAD_GROUP) < (ROT_DIM // 2)
    tab = lax.dot_general(cs_ref[...], spread_ref[...], (((0,), (0,)), ((), ())), preferred_element_type=F32)
    cosb = tab[:, :LANES] + jnp.where((lane % HEAD_GROUP) >= ROT_DIM, 1.0, 0.0)
    sinb = tab[:, LANES:]

    def norm_rope(t, gain, out_ref):
        tn = _group_rms(t, ones256_ref[...], gain)
        for j in range(DIFF_W // LANES):
            c = tn[:, j * LANES:(j + 1) * LANES]
            partner = jnp.where(first_half, pltpu.roll(c, LANES - ROT_DIM // 2, 1), pltpu.roll(c, ROT_DIM // 2, 1))
            out_ref[:, j * LANES:(j + 1) * LANES] = (c * cosb + partner * sinb).astype(BF16)

    norm_rope(proj(0, DIFF_W), gq_ref[...] * (DIFF_HEAD_DIM ** -0.5 * LOG2_E), q_out)
    norm_rope(proj(DIFF_W, 2 * DIFF_W), gk_ref[...], k_out)
    v_out[...] = proj(2 * DIFF_W, 3 * DIFF_W).astype(BF16)

    z = _gelu_tanh(proj(3 * DIFF_W, 3 * DIFF_W + 2 * GMLP_W))
    u = z[:, :GMLP_W]
    vg = z[:, GMLP_W:]
    vc = vg - jnp.mean(vg, axis=-1, keepdims=True)
    vgn = (vc * lax.rsqrt(jnp.mean(vc * vc, axis=-1, keepdims=True) + EPS) * gsgu_ref[...]).astype(BF16)
    wrow = lax.broadcasted_iota(I32, (CHUNK, GMLP_GROUPS * CHUNK), 0)
    wcol = lax.broadcasted_iota(I32, (CHUNK, GMLP_GROUPS * CHUNK), 1) % CHUNK
    w_causal = jnp.where(wcol <= wrow, wsp_ref[...], 0.0).astype(BF16)
    grp = lax.broadcasted_iota(I32, (CHUNK, GMLP_W), 1) // HEAD_GROUP
    zero_b = jnp.zeros((CHUNK, GMLP_W), BF16)
    for r in range(tm // CHUNK):
        vchunk = vgn[r * CHUNK:(r + 1) * CHUNK, :]
        v_bd = jnp.concatenate([jnp.where(grp == g, vchunk, zero_b) for g in range(GMLP_GROUPS)], axis=0)
        mixed = _dot(w_causal, v_bd) + bsp_ref[...]
        gc_out[r * CHUNK:(r + 1) * CHUNK, 0:GMLP_W] = (u[r * CHUNK:(r + 1) * CHUNK, :] * mixed).astype(BF16)

    pc = proj(3 * DIFF_W + 2 * GMLP_W, D_IN_PROJ)
    qc = _group_rms(pc, ones256_ref[...], gcq_ref[...] * (HEAD_GROUP ** -0.5)).astype(BF16)
    hgrp = lax.broadcasted_iota(I32, (tm, MEM_W), 1) // HEAD_GROUP
    zero_q = jnp.zeros((tm, MEM_W), BF16)
    q_st = jnp.concatenate([jnp.where(hgrp == h, qc, zero_q) for h in range(N_MEM_HEADS)], axis=0)
    s = _dot(q_st, kT_ref[0])
    p = jnp.exp(s - jnp.max(s, axis=-1, keepdims=True))
    o = _dot(p.astype(BF16), vm_ref[0]) / jnp.sum(p, axis=-1, keepdims=True)
    c = jnp.zeros((tm, MEM_W), F32)
    for h in range(N_MEM_HEADS):
        c = c + jnp.where(hgrp == h, o[h * tm:(h + 1) * tm, :], 0.0)
    gc_out[:, GMLP_W:GMLP_W + MEM_W] = c.astype(BF16)


def _diff_attn_kernel(q_ref, k_ref, v_ref, lq1_ref, lk1_ref, lq2_ref, lk2_ref, gsub_ref, o_ref,
                      s0_ref, s1_ref, m_ref, acc_ref, vT_ref):
    tq = ATTN_TILE
    tiles_per_step = q_ref.shape[0] // tq
    assert tiles_per_step % 2 == 0
    seq = k_ref.shape[0]
    step = pl.program_id(1)
    feat = lax.broadcasted_iota(I32, (DIFF_V_DIM, tq), 0)
    heads = range(N_DIFF_HEADS)
    hl = lambda h: slice(h * DIFF_V_DIM, (h + 1) * DIFF_V_DIM)
    ext_rows = vT_ref.shape[1]

    @pl.when(step == 0)
    def _():
        ones_row = jnp.where(lax.broadcasted_iota(I32, (ext_rows - DIFF_V_DIM, seq), 0) == 0, 1.0, 0.0).astype(BF16)
        for h in heads:
            for c in range(seq // tq):
                vT_ref[h, 0:DIFF_V_DIM, c * tq:(c + 1) * tq] = v_ref[c * tq:(c + 1) * tq, hl(h)].T
            vT_ref[h, DIFF_V_DIM:ext_rows, :] = ones_row

    @pl.when(jnp.logical_and(pl.program_id(0) == 0, step == 0))
    def _():
        acc_ref[...] = jnp.zeros(acc_ref.shape, F32)

    lam = (jnp.exp(jnp.sum(lq1_ref[...] * lk1_ref[...], axis=-1, keepdims=True))
           - jnp.exp(jnp.sum(lq2_ref[...] * lk2_ref[...], axis=-1, keepdims=True)) + LAMBDA_INIT)

    def q_tile(j, first_ref, other_ref):
        i = tiles_per_step * step + j
        q_rows = slice(j * tq, (j + 1) * tq)

        def stacked_qT(h):
            qT = q_ref[q_rows, hl(h)].T
            zero = jnp.zeros_like(qT)
            return jnp.concatenate(
                [jnp.where(feat < DIFF_HEAD_DIM, qT, zero), jnp.where(feat >= DIFF_HEAD_DIM, qT, zero)], axis=1)

        qsT = [stacked_qT(h) for h in heads]

        def scores(t, s_ref):
            rows = pl.ds(pl.multiple_of(t * tq, tq), tq)
            for h in heads:
                s_ref[h] = _dot(k_ref[rows, hl(h)], qsT[h])

        def update(t, s_ref, causal):
            cols = pl.ds(pl.multiple_of(t * tq, tq), tq)
            for h in heads:
                s = s_ref[h]
                if causal:
                    key = lax.broadcasted_iota(I32, (tq, 2 * tq), 0)
                    qry = lax.broadcasted_iota(I32, (tq, 2 * tq), 1) % tq
                    s = jnp.where(key <= qry, s, NEG_BIG)
                m = m_ref[h]
                m_new = jnp.maximum(m, jnp.max(s, axis=0, keepdims=True))
                alpha = jnp.exp2(m - m_new)
                p = jnp.exp2(s - m_new)
                m_ref[h] = m_new
                acc_ref[h] = alpha * acc_ref[h] + _dot(vT_ref[h, :, cols], p.astype(BF16))

        m_ref[...] = jnp.full(m_ref.shape, NEG_BIG, F32)
        scores(0, first_ref)

        def two_tiles(pidx, carry):
            t = 2 * pidx
            scores(t + 1, other_ref)
            update(t, first_ref, False)
            scores(t + 2, first_ref)
            update(t + 1, other_ref, False)
            return carry

        lax.fori_loop(0, (tiles_per_step // 2) * step + j // 2, two_tiles, 0)

        if j % 2 == 0:
            update(i, first_ref, True)
        else:
            scores(i, other_ref)
            update(i - 1, first_ref, False)
            update(i, other_ref, True)

        for h in heads:
            on = acc_ref[h, 0:DIFF_V_DIM, :] * (1.0 / acc_ref[h, DIFF_V_DIM:DIFF_V_DIM + 1, :])
            o = on[:, :tq] - lam * on[:, tq:]
            ms = jnp.mean(o * o, axis=0, keepdims=True)
            o = o * lax.rsqrt(ms + EPS) * gsub_ref[...] * (1.0 - LAMBDA_INIT)
            o_ref[q_rows, hl(h)] = o.T.astype(BF16)

    first_ref, other_ref = s0_ref, s1_ref
    for j in range(tiles_per_step):
        q_tile(j, first_ref, other_ref)
        last_read = first_ref if j % 2 == 0 else other_ref
        first_ref, other_ref = (other_ref, first_ref) if last_read is first_ref else (first_ref, other_ref)


def _out_router_kernel(x_ref, a_ref, gc_ref, wo_ref, gffn_ref, wrh_ref, wrl_ref, br_ref, before_ref,
                       x1_out, hpa_out, hpb_out, idx_out, gate_out, pos_out, cnt_out, carry_ref):
    tm = x_ref.shape[0]

    @pl.when(pl.program_id(0) == 0)
    def _():
        carry_ref[...] = jnp.zeros_like(carry_ref)

    sub = before_ref.shape[0]
    wr_stack = jnp.concatenate([wrh_ref[...], wrl_ref[...]], axis=0)
    eio = lax.broadcasted_iota(I32, (N_EXPERTS, sub), 0)
    before_b = before_ref[...]
    carry = carry_ref[...]
    for r in range(tm // sub):
        rows = slice(r * sub, (r + 1) * sub)
        mix = jnp.concatenate([a_ref[rows, :], gc_ref[rows, :]], axis=1)
        x1 = x_ref[rows, :] + _dot(mix, wo_ref[...])
        x1_out[rows, :] = x1
        h2 = _rms(x1, gffn_ref[...])
        hb = h2.astype(BF16)
        hpa_out[rows, :] = _pack_bf16_pairs(h2[:, :D_MODEL // 2])
        hpb_out[rows, :] = _pack_bf16_pairs(h2[:, D_MODEL // 2:])

        h_lo = (h2 - hb.astype(F32)).astype(BF16)
        both = _dot_nt(wr_stack, hb)
        logits = (both[:N_EXPERTS] + both[N_EXPERTS:]) + _dot_nt(wrh_ref[...], h_lo) + br_ref[...]

        vals, idxs, sels = [], [], []
        cur = logits
        for _ in range(TOP_K):
            m = jnp.max(cur, axis=0, keepdims=True)
            ik = jnp.min(jnp.where(cur == m, eio, N_EXPERTS), axis=0, keepdims=True)
            sel = eio == ik
            cur = jnp.where(sel, -jnp.inf, cur)
            vals.append(m)
            idxs.append(ik)
            sels.append(sel)
        es = [jnp.exp(v - vals[0]) for v in vals]
        tot = es[0] + es[1] + es[2] + es[3]
        gates = jnp.concatenate([e / tot for e in es], axis=0)
        g_hi = gates.astype(BF16)
        g_lo = (gates - g_hi.astype(F32)).astype(BF16)
        gate_out[:, rows] = jnp.concatenate(
            [g_hi, g_lo, jnp.zeros((GATE_ROWS - 2 * TOP_K, sub), BF16)], axis=0)
        idx_out[:, rows] = jnp.concatenate(idxs, axis=0)

        cnt = jnp.zeros((N_EXPERTS, sub), F32)
        for sel in sels:
            cnt = cnt + jnp.where(sel, 1.0, 0.0)
        base = carry + _dot(cnt.astype(BF16), before_b)
        pos_out[:, rows] = jnp.concatenate(
            [jnp.sum(jnp.where(sel, base, 0.0), axis=0, keepdims=True) for sel in sels], axis=0).astype(I32)
        carry = carry + jnp.sum(cnt, axis=1, keepdims=True)
    carry_ref[...] = carry
    cnt_out[...] = carry


def _dest_kernel(cnt_ref, idx_ref, pos_ref, dest_out, be_out, valid_out, nbu_out):
    idx = idx_ref[...]
    dest = pos_ref[...]
    bidx = lax.broadcasted_iota(I32, be_out.shape, 1)
    be = jnp.zeros(be_out.shape, I32)
    valid = jnp.zeros(be_out.shape, I32)
    run = jnp.int32(0)
    for e in range(N_EXPERTS):
        dest = dest + jnp.where(idx == e, run, 0)
        first_block = run // ROW_BLOCK
        run = run + ((cnt_ref[e] + (ROW_BLOCK - 1)) // ROW_BLOCK) * ROW_BLOCK
        mine = jnp.logical_and(bidx >= first_block, bidx < run // ROW_BLOCK)
        valid = jnp.where(mine, jnp.clip(cnt_ref[e] - (bidx - first_block) * ROW_BLOCK, 0, ROW_BLOCK), valid)
        be = be + jnp.where(bidx >= run // ROW_BLOCK, 1, 0)
    dest_out[...] = dest
    be_out[...] = jnp.minimum(be, N_EXPERTS - 1)
    valid_out[...] = valid
    nbu_out[...] = jnp.zeros(nbu_out.shape, I32) + run // ROW_BLOCK


def _expert_ffn_kernel(be_ref, nbu_ref, nxt_ref, valid_ref, xa_ref, xb_ref, w1_hbm, b1_ref, w2_hbm, b2_ref,
                       y_ref, w1f_ref, w2f_ref, w1s_ref, w2s_ref, sem_ref):
    b = pl.program_id(0)

    def weight_copies(e):
        return (pltpu.make_async_copy(w1_hbm.at[e], w1f_ref, sem_ref.at[0]),
                pltpu.make_async_copy(w2_hbm.at[e], w2f_ref, sem_ref.at[1]))

    def mlp(rows):
        parts = _unpack_bf16_pairs(xa_ref[0:rows, :]) + _unpack_bf16_pairs(xb_ref[0:rows, :])
        xrow = jnp.concatenate([p.astype(BF16) for p in parts], axis=1)
        hm = _dot(xrow, w1s_ref[...]) + b1_ref[0]
        glu = jnp.minimum(hm[:, :D_FF], SWIGLU_LIMIT)
        lin = jnp.clip(hm[:, D_FF:], -SWIGLU_LIMIT, SWIGLU_LIMIT)
        act = glu * jax.nn.sigmoid(SWIGLU_ALPHA * glu) * (lin + 1.0)
        y = _dot(act.astype(BF16), w2s_ref[...]) + b2_ref[0]
        hw = D_MODEL // 4
        y_ref[0:rows, 0:hw] = _pack_bf16_pairs(y[:, :D_MODEL // 2])
        y_ref[0:rows, hw:] = _pack_bf16_pairs(y[:, D_MODEL // 2:])

    @pl.when(b < nbu_ref[0])
    def _():
        e = be_ref[b]

        @pl.when(b == 0)
        def _():
            for cp in weight_copies(e):
                cp.start()

        @pl.when(jnp.logical_or(b == 0, e != be_ref[jnp.maximum(b - 1, 0)]))
        def _():
            for cp in weight_copies(e):
                cp.wait()
            w1s_ref[...] = w1f_ref[...].astype(BF16)
            w2s_ref[...] = w2f_ref[...].astype(BF16)
            nxt = nxt_ref[e]

            @pl.when(nxt >= 0)
            def _():
                for cp in weight_copies(nxt):
                    cp.start()

        n_sub = (valid_ref[b] + (FFN_SUB_ROWS - 1)) // FFN_SUB_ROWS
        for k in range(1, ROW_BLOCK // FFN_SUB_ROWS + 1):
            pl.when(n_sub == k)(functools.partial(mlp, k * FFN_SUB_ROWS))


SC_WINDOW = 128
SC_GATHER_WINDOW = 64


def _sc_mesh():
    return plsc.VectorSubcoreMesh(core_axis_name="c", subcore_axis_name="s")


def _sc_gather_rows(table, idx):
    n = idx.shape[0]
    width = table.shape[1]
    win = SC_GATHER_WINDOW

    @functools.partial(pl.kernel, out_type=jax.ShapeDtypeStruct((n, width), table.dtype), mesh=_sc_mesh(),
                       scratch_types=[])
    def gather_kernel(t_hbm, i_hbm, o_hbm):
        def body(i_vmem, o_vmem):
            pltpu.sync_copy(t_hbm.at[i_vmem.at[0]], o_vmem)

        pltpu.emit_pipeline(
            body,
            grid=(n // win,),
            in_specs=[pl.BlockSpec((1, win), lambda i: (i, 0))],
            out_specs=[pl.BlockSpec((win, width), lambda i: (i, 0))],
            core_axis_name=("c", "s"),
            dimension_semantics=(pltpu.PARALLEL,),
        )(i_hbm, o_hbm)

    return gather_kernel(table, idx.reshape(n // win, win))


def _sc_scatter_rows(rows, idx_rows, n_out):
    n, width = rows.shape

    @functools.partial(pl.kernel, out_type=jax.ShapeDtypeStruct((n_out, width), rows.dtype), mesh=_sc_mesh(),
                       scratch_types=[])
    def scatter_kernel(r_hbm, *refs):
        i_hbms, o_hbm = refs[:-1], refs[-1]

        def body(r_vmem, *i_vmems):
            for i_vmem in i_vmems:
                pltpu.sync_copy(r_vmem, o_hbm.at[i_vmem.at[0]])

        pltpu.emit_pipeline(
            body,
            grid=(n // SC_WINDOW,),
            in_specs=[pl.BlockSpec((SC_WINDOW, width), lambda i: (i, 0))]
            + [pl.BlockSpec((1, SC_WINDOW), lambda i: (0, i)) for _ in i_hbms],
            out_specs=[],
            core_axis_name=("c", "s"),
            dimension_semantics=(pltpu.PARALLEL,),
        )(r_hbm, *i_hbms)

    return scatter_kernel(rows, *idx_rows)


def _combine_kernel(x1_ref, yg_ref, gate_ref, spread_ref, o_ref):
    q = D_MODEL // 4
    x1 = x1_ref[...]
    acc = [x1[:, j * q:(j + 1) * q] for j in range(4)]
    gfull = lax.dot_general(gate_ref[...], spread_ref[...], (((0,), (0,)), ((), ())), preferred_element_type=F32)
    for k in range(TOP_K):
        parts = _unpack_bf16_pairs(yg_ref[k, :, 0:q]) + _unpack_bf16_pairs(yg_ref[k, :, q:])
        g = jnp.tile(gfull[:, k * LANES:(k + 1) * LANES], (1, q // LANES))
        acc = [a + g * p for a, p in zip(acc, parts)]
    for j in range(4):
        o_ref[:, j * q:(j + 1) * q] = acc[j]


def _combine_into_kernel(x1_ref, yg_ref, gate_ref, spread_ref, prev_ref, o_ref):
    del prev_ref
    _combine_kernel(x1_ref, yg_ref, gate_ref, spread_ref, o_ref)


def _block_diag_ones(width):
    r = jnp.arange(width) // HEAD_GROUP
    return (r[:, None] == r[None, :]).astype(BF16)


def _rope_tables(positions):
    half = ROT_DIM // 2
    inv_freq = ROPE_THETA ** (-jnp.arange(0, ROT_DIM, 2, dtype=F32) / ROT_DIM)
    ang = inv_freq[:, None] * positions.astype(F32).reshape(1, -1)
    cs = jnp.concatenate([jnp.cos(ang), jnp.sin(ang)], axis=0)
    cs_hi = cs.astype(BF16)
    cs_lo = (cs - cs_hi.astype(F32)).astype(BF16)
    lane = jnp.arange(LANES) % HEAD_GROUP
    j = jnp.arange(half)[:, None]
    lo_half = (lane[None, :] == j).astype(F32)
    hi_half = (lane[None, :] == j + half).astype(F32)
    spread = jnp.concatenate([
        jnp.concatenate([lo_half + hi_half, jnp.zeros((half, LANES), F32)], axis=1),
        jnp.concatenate([jnp.zeros((half, LANES), F32), hi_half - lo_half], axis=1)], axis=0)
    return jnp.concatenate([cs_hi, cs_lo], axis=0), jnp.concatenate([spread, spread], axis=0).astype(BF16)


def _full(shape):
    return pl.BlockSpec(shape, lambda *_: (0,) * len(shape))


def kernel(x, mem, positions, g_mix_norm, w_in, g_dq, g_dk, lambda_q1, lambda_k1, lambda_q2, lambda_k2, g_subln, g_sgu, w_spatial, b_spatial, g_mem_norm, w_mem_kv, g_cq, g_ck, w_out, g_ffn_norm, w_router, b_router, w_mlp1, b_mlp1, w_mlp2, b_mlp2):
    B, S, D = x.shape
    M = mem.shape[1]
    tm = TOKEN_TILE
    tr = ROUTER_TILE
    assert D == D_MODEL and S % tm == 0 and S % (ATTN_Q_TILES_PER_STEP * ATTN_TILE) == 0 and g_mix_norm.shape[0] == 1
    last_batches = B // LAST_PART_DIVISOR
    part_batches = (B - last_batches, last_batches) if last_batches > 0 and S % tr == 0 else (B,)

    xf = x.reshape(B * S, D)
    rope_cs, rope_spread = _rope_tables(positions)
    ones256 = _block_diag_ones(MEM_W)
    row = lambda v: v.reshape(1, -1).astype(F32)
    tile_row = lambda v, reps: jnp.tile(v.reshape(1, -1).astype(F32), (1, reps))
    w_in_b, w_out_b, w_kv_b = w_in[0].astype(BF16), w_out[0].astype(BF16), w_mem_kv[0].astype(BF16)
    w_sp_lanes = jnp.transpose(w_spatial[0], (1, 0, 2)).reshape(CHUNK, GMLP_GROUPS * CHUNK)
    b_sp_lanes = jnp.repeat(b_spatial[0].T, HEAD_GROUP, axis=1)
    wr = w_router[0].T.astype(F32)
    wr_hi = wr.astype(BF16)
    wr_lo = (wr - wr_hi.astype(F32)).astype(BF16)
    before = (jnp.arange(tr)[:, None] < jnp.arange(tr)[None, :]).astype(BF16)
    gate_row = jnp.arange(GATE_ROWS)[:, None]
    gate_spread = ((gate_row < 2 * TOP_K) & (gate_row % TOP_K == jnp.arange(TOP_K * LANES)[None, :] // LANES)).astype(BF16)
    b1r = b_mlp1[0].reshape(N_EXPERTS, 1, 2 * D_FF)
    b2r = b_mlp2[0].reshape(N_EXPERTS, 1, D)

    out = None
    b_off = 0
    for Bp in part_batches:
        N = Bp * S
        assert N % tr == 0 and N % tm == 0
        n_assign = N * TOP_K
        n_blocks = -(-n_assign // ROW_BLOCK) + N_EXPERTS
        n_rows = n_blocks * ROW_BLOCK
        nb_pad = -(-n_blocks // LANES) * LANES
        t_off = b_off * S // tm
        r_off = b_off * S // tr

        mb = MEM_BATCHES if Bp % MEM_BATCHES == 0 and b_off % MEM_BATCHES == 0 else 1
        kT, vm = pl.pallas_call(
            _mem_kv_kernel,
            grid=(Bp // mb,),
            in_specs=[pl.BlockSpec((mb, M, D), lambda b: (b + b_off // mb, 0, 0)), _full((1, D)), _full((D, 2 * MEM_W)),
                      _full((1, MEM_W)), _full((MEM_W, MEM_W))],
            out_specs=[pl.BlockSpec((mb, MEM_W, M), lambda b: (b, 0, 0)),
                       pl.BlockSpec((mb, M, MEM_W), lambda b: (b, 0, 0))],
            out_shape=[jax.ShapeDtypeStruct((Bp, MEM_W, M), BF16), jax.ShapeDtypeStruct((Bp, M, MEM_W), BF16)],
            compiler_params=_cparams(("parallel",)),
            name="mem_kv",
        )(mem, row(g_mem_norm[0]), w_kv_b, tile_row(g_ck[0], N_MEM_HEADS), ones256)

        tiles_per_batch = S // tm
        tok = lambda w: pl.BlockSpec((tm, w), lambda i: (i, 0))
        tok_in = lambda w: pl.BlockSpec((tm, w), lambda i: (i + t_off, 0))
        qn, kn, vv, gc = pl.pallas_call(
            _mixer_in_kernel,
            grid=(N // tm,),
            in_specs=[tok_in(D), pl.BlockSpec((ROPE_STACK_ROWS, tm), lambda i: (0, i + t_off)),
                      _full((ROPE_STACK_ROWS, 2 * LANES)),
                      _full((1, D)), _full((D, D_IN_PROJ)),
                      _full((1, DIFF_W)), _full((1, DIFF_W)),
                      _full((1, GMLP_W)), _full((CHUNK, GMLP_GROUPS * CHUNK)), _full((CHUNK, GMLP_W)),
                      _full((1, MEM_W)), _full((MEM_W, MEM_W)),
                      pl.BlockSpec((1, MEM_W, M), lambda i: (i // tiles_per_batch, 0, 0)),
                      pl.BlockSpec((1, M, MEM_W), lambda i: (i // tiles_per_batch, 0, 0))],
            out_specs=[tok(DIFF_W), tok(DIFF_W), tok(DIFF_W), tok(GMLP_W + MEM_W)],
            out_shape=[jax.ShapeDtypeStruct((N, DIFF_W), BF16)] * 3 + [jax.ShapeDtypeStruct((N, GMLP_W + MEM_W), BF16)],
            compiler_params=_cparams(("parallel",)),
            name="mixer_in",
        )(xf, rope_cs, rope_spread, row(g_mix_norm[0]), w_in_b,
          tile_row(g_dq[0], 2 * N_DIFF_HEADS), tile_row(g_dk[0], 2 * N_DIFF_HEADS),
          row(g_sgu[0]), w_sp_lanes, b_sp_lanes, tile_row(g_cq[0], N_MEM_HEADS), ones256, kT, vm)

        tq = ATTN_TILE
        q_rows = ATTN_Q_TILES_PER_STEP * tq
        nqp = S // q_rows
        head_q = pl.BlockSpec((q_rows, DIFF_W), lambda b, i: (b * nqp + i, 0))
        head_kv = pl.BlockSpec((S, DIFF_W), lambda b, i: (b, 0))
        lam_spec = pl.BlockSpec((1, DIFF_HEAD_DIM), lambda b, i: (0, 0))
        nh = N_DIFF_HEADS
        attn = pl.pallas_call(
            _diff_attn_kernel,
            grid=(Bp, nqp),
            in_specs=[head_q, head_kv, head_kv, lam_spec, lam_spec, lam_spec, lam_spec,
                      pl.BlockSpec((DIFF_V_DIM, 1), lambda b, i: (0, 0))],
            out_specs=head_q,
            out_shape=jax.ShapeDtypeStruct((N, DIFF_W), BF16),
            scratch_shapes=[pltpu.VMEM((nh, tq, 2 * tq), F32), pltpu.VMEM((nh, tq, 2 * tq), F32),
                            pltpu.VMEM((nh, 1, 2 * tq), F32), pltpu.VMEM((nh, ATTN_EXT_ROWS, 2 * tq), F32),
                            pltpu.VMEM((nh, ATTN_EXT_ROWS, S), BF16)],
            compiler_params=_cparams(("arbitrary", "arbitrary")),
            name="diff_attn",
        )(qn, kn, vv, row(lambda_q1[0]), row(lambda_k1[0]), row(lambda_q2[0]), row(lambda_k2[0]),
          g_subln[0].reshape(DIFF_V_DIM, 1).astype(F32))

        rtok = lambda w: pl.BlockSpec((tr, w), lambda i: (i, 0))
        rtok_in = lambda w: pl.BlockSpec((tr, w), lambda i: (i + r_off, 0))
        tokT = lambda: pl.BlockSpec((TOP_K, tr), lambda i: (0, i))
        hw = D // 4
        x1, hpa, hpb, idxT, gateT, posT, counts = pl.pallas_call(
            _out_router_kernel,
            grid=(N // tr,),
            in_specs=[rtok_in(D), rtok(DIFF_W), rtok(GMLP_W + MEM_W), _full((D, D)), _full((1, D)),
                      _full((N_EXPERTS, D)), _full((N_EXPERTS, D)), _full((N_EXPERTS, 1)), _full((tr, tr))],
            out_specs=[rtok(D), rtok(hw), rtok(hw), tokT(), pl.BlockSpec((GATE_ROWS, tr), lambda i: (0, i)), tokT(),
                       _full((N_EXPERTS, 1))],
            out_shape=[jax.ShapeDtypeStruct((N, D), F32), jax.ShapeDtypeStruct((N, hw), U32),
                       jax.ShapeDtypeStruct((N, hw), U32), jax.ShapeDtypeStruct((TOP_K, N), I32),
                       jax.ShapeDtypeStruct((GATE_ROWS, N), BF16),
                       jax.ShapeDtypeStruct((TOP_K, N), I32), jax.ShapeDtypeStruct((N_EXPERTS, 1), F32)],
            scratch_shapes=[pltpu.VMEM((N_EXPERTS, 1), F32)],
            compiler_params=_cparams(("arbitrary",)),
            name="out_router",
        )(xf, attn, gc, w_out_b, row(g_ffn_norm[0]), wr_hi, wr_lo, b_router[0].reshape(N_EXPERTS, 1).astype(F32), before)

        destT, block_expert, block_valid, nb_used = pl.pallas_call(
            _dest_kernel,
            grid_spec=pltpu.PrefetchScalarGridSpec(
                num_scalar_prefetch=1,
                grid=(1,),
                in_specs=[pl.BlockSpec((TOP_K, N), lambda i, c: (0, 0)), pl.BlockSpec((TOP_K, N), lambda i, c: (0, 0))],
                out_specs=[pl.BlockSpec((TOP_K, N), lambda i, c: (0, 0)), pl.BlockSpec((1, nb_pad), lambda i, c: (0, 0)),
                           pl.BlockSpec((1, nb_pad), lambda i, c: (0, 0)), pl.BlockSpec((1, LANES), lambda i, c: (0, 0))],
            ),
            out_shape=[jax.ShapeDtypeStruct((TOP_K, N), I32), jax.ShapeDtypeStruct((1, nb_pad), I32),
                       jax.ShapeDtypeStruct((1, nb_pad), I32), jax.ShapeDtypeStruct((1, LANES), I32)],
            compiler_params=_cparams(("arbitrary",)),
            name="dest",
        )(counts.reshape(N_EXPERTS).astype(I32), idxT, posT)

        dest_rows = [destT[k].reshape(1, N) for k in range(TOP_K)]
        xa_buf = _sc_scatter_rows(hpa, dest_rows, n_rows)
        xb_buf = _sc_scatter_rows(hpb, dest_rows, n_rows)

        cnt_i = counts.reshape(N_EXPERTS).astype(I32)
        owner = jnp.where(cnt_i > 0, jnp.arange(N_EXPERTS, dtype=I32), N_EXPERTS)
        later = jnp.concatenate([lax.cummin(owner[::-1])[::-1][1:], jnp.full((1,), N_EXPERTS, I32)])
        next_expert = jnp.where(later < N_EXPERTS, later, -1)
        last = lambda b, be, nbu, nxt, valid: jnp.minimum(b, nbu[0] - 1)
        row_blk = lambda: pl.BlockSpec((ROW_BLOCK, hw), lambda b, be, nbu, nxt, valid: (last(b, be, nbu, nxt, valid), 0))
        y_blk = pl.BlockSpec((ROW_BLOCK, 2 * hw), lambda b, be, nbu, nxt, valid: (last(b, be, nbu, nxt, valid), 0))
        y_buf = pl.pallas_call(
            _expert_ffn_kernel,
            grid_spec=pltpu.PrefetchScalarGridSpec(
                num_scalar_prefetch=4,
                grid=(n_blocks,),
                in_specs=[row_blk(), row_blk(),
                          pl.BlockSpec(memory_space=pl.ANY),
                          pl.BlockSpec((1, 1, 2 * D_FF), lambda b, be, nbu, nxt, valid: (be[b], 0, 0)),
                          pl.BlockSpec(memory_space=pl.ANY),
                          pl.BlockSpec((1, 1, D), lambda b, be, nbu, nxt, valid: (be[b], 0, 0))],
                out_specs=y_blk,
                scratch_shapes=[pltpu.VMEM((D, 2 * D_FF), F32), pltpu.VMEM((D_FF, D), F32),
                                pltpu.VMEM((D, 2 * D_FF), BF16), pltpu.VMEM((D_FF, D), BF16),
                                pltpu.SemaphoreType.DMA((2,))],
            ),
            out_shape=jax.ShapeDtypeStruct((n_rows, 2 * hw), U32),
            compiler_params=_cparams(("arbitrary",)),
            name="expert_ffn",
        )(block_expert[0, :n_blocks], nb_used[0, :1], next_expert, block_valid[0, :n_blocks], xa_buf, xb_buf,
          w_mlp1[0], b1r, w_mlp2[0], b2r)

        yg = _sc_gather_rows(y_buf, destT.reshape(n_assign)).reshape(TOP_K, N, 2 * hw)
        tc = COMBINE_TILE
        prev = () if out is None else (out,)
        out = pl.pallas_call(
            _combine_kernel if out is None else _combine_into_kernel,
            grid=(N // tc,),
            in_specs=[pl.BlockSpec((tc, D), lambda i: (i, 0)),
                      pl.BlockSpec((TOP_K, tc, 2 * hw), lambda i: (0, i, 0)),
                      pl.BlockSpec((GATE_ROWS, tc), lambda i: (0, i)),
                      _full((GATE_ROWS, TOP_K * LANES))] + [pl.BlockSpec(memory_space=pl.ANY)] * len(prev),
            out_specs=pl.BlockSpec((tc, D), lambda i: (i + b_off * S // tc, 0)),
            out_shape=jax.ShapeDtypeStruct((B * S, D), F32),
            input_output_aliases={4: 0} if prev else {},
            compiler_params=_cparams(("parallel",)),
            name="combine",
        )(x1, yg, gateT, gate_spread, *prev)
        b_off += Bp
    return out.reshape(B, S, D)
```

```python
import functools

import jax
import jax.numpy as jnp
from jax import lax
from jax.experimental import pallas as pl
from jax.experimental.pallas import tpu as pltpu
from jax.experimental.pallas import tpu_sc as plsc

F32 = jnp.float32
BF16 = jnp.bfloat16
I32 = jnp.int32
U32 = jnp.uint32

D_MODEL = 1024
N_DIFF_HEADS = 4
DIFF_HEAD_DIM = 64
DIFF_V_DIM = 128
DIFF_W = 512
GMLP_W = 256
GMLP_GROUPS = 4
CHUNK = 128
MEM_W = 256
N_MEM_HEADS = 4
HEAD_GROUP = 64
D_IN_PROJ = 2304
ROPE_THETA = 500000.0
ROT_DIM = 16
ROPE_STACK_ROWS = 2 * ROT_DIM
N_EXPERTS = 32
TOP_K = 4
D_FF = 1024
SWIGLU_LIMIT = 7.0
SWIGLU_ALPHA = 1.702
EPS = 1e-6
LAMBDA_INIT = 0.8 - 0.6

LANES = 128
ROW_BLOCK = 1024
FFN_SUB_ROWS = 256
TOKEN_TILE = 1024
COMBINE_TILE = 1024
MEM_BATCHES = 4
LAST_PART_DIVISOR = 2
ROUTER_TILE = 1024
GATE_ROWS = 16
ATTN_TILE = 256
ATTN_Q_TILES_PER_STEP = 8
ATTN_EXT_ROWS = 128 + 16
VMEM_LIMIT = 56 * 1024 * 1024
NEG_BIG = -1e30
LOG2_E = 1.4426950408889634


def _cparams(sem):
    return pltpu.CompilerParams(dimension_semantics=sem, vmem_limit_bytes=VMEM_LIMIT)


def _dot(a, b):
    return jnp.dot(a, b, preferred_element_type=F32)


def _dot_nt(a, b):
    return lax.dot_general(a, b, (((1,), (1,)), ((), ())), preferred_element_type=F32)


def _rms(x, gain):
    ms = jnp.mean(x * x, axis=-1, keepdims=True)
    return x * lax.rsqrt(ms + EPS) * gain


def _group_rms(t, ones_bd, gain):
    w = ones_bd.shape[0]
    chunks = []
    for j in range(t.shape[1] // w):
        c = t[:, j * w:(j + 1) * w]
        ss = _dot((c * c).astype(BF16), ones_bd)
        chunks.append(c * lax.rsqrt(ss * (1.0 / HEAD_GROUP) + EPS))
    return (chunks[0] if len(chunks) == 1 else jnp.concatenate(chunks, axis=1)) * gain


def _pack_bf16_pairs(v):
    w = v.shape[1] // 2
    bits = lax.bitcast_convert_type(v.astype(BF16).astype(F32), U32)
    return (bits[:, :w] & jnp.uint32(0xFFFF0000)) | (bits[:, w:] >> jnp.uint32(16))


def _unpack_bf16_pairs(words):
    hi = lax.bitcast_convert_type(words & jnp.uint32(0xFFFF0000), F32)
    lo = lax.bitcast_convert_type(words << jnp.uint32(16), F32)
    return hi, lo


def _mem_kv_kernel(mem_ref, gmem_ref, wkv_ref, gck_ref, ones_ref, kT_ref, v_ref):
    for j in range(mem_ref.shape[0]):
        m = _rms(mem_ref[j], gmem_ref[...]).astype(BF16)
        kv = _dot(m, wkv_ref[...])
        k = _group_rms(kv[:, :MEM_W], ones_ref[...], gck_ref[...])
        kT_ref[j] = k.T.astype(BF16)
        v_ref[j] = kv[:, MEM_W:].astype(BF16)


def _gelu_tanh(x):
    return 0.5 * x * (1.0 + jnp.tanh(0.7978845608028654 * (x + 0.044715 * (x * x * x))))


def _mixer_in_kernel(x_ref, cs_ref, spread_ref, gmix_ref, win_ref, gq_ref, gk_ref,
                     gsgu_ref, wsp_ref, bsp_ref, gcq_ref, ones256_ref, kT_ref, vm_ref,
                     q_out, k_out, v_out, gc_out):
    tm = x_ref.shape[0]
    hb = _rms(x_ref[...], gmix_ref[...]).astype(BF16)

    def proj(lo, hi):
        return _dot(hb, win_ref[:, lo:hi])

    lane = lax.broadcasted_iota(I32, (tm, LANES), 1)
    first_half = (lane % HEAD_GROUP) < (ROT_DIM // 2)
    tab = lax.dot_general(cs_ref[...], spread_ref[...], (((0,), (0,)), ((), ())), preferred_element_type=F32)
    cosb = tab[:, :LANES] + jnp.where((lane % HEAD_GROUP) >= ROT_DIM, 1.0, 0.0)
    sinb = tab[:, LANES:]

    def norm_rope(t, gain, out_ref):
        tn = _group_rms(t, ones256_ref[...], gain)
        for j in range(DIFF_W // LANES):
            c = tn[:, j * LANES:(j + 1) * LANES]
            partner = jnp.where(first_half, pltpu.roll(c, LANES - ROT_DIM // 2, 1), pltpu.roll(c, ROT_DIM // 2, 1))
            out_ref[:, j * LANES:(j + 1) * LANES] = (c * cosb + partner * sinb).astype(BF16)

    norm_rope(proj(0, DIFF_W), gq_ref[...] * (DIFF_HEAD_DIM ** -0.5 * LOG2_E), q_out)
    norm_rope(proj(DIFF_W, 2 * DIFF_W), gk_ref[...], k_out)
    v_out[...] = proj(2 * DIFF_W, 3 * DIFF_W).astype(BF16)

    z = _gelu_tanh(proj(3 * DIFF_W, 3 * DIFF_W + 2 * GMLP_W))
    u = z[:, :GMLP_W]
    vg = z[:, GMLP_W:]
    vc = vg - jnp.mean(vg, axis=-1, keepdims=True)
    vgn = (vc * lax.rsqrt(jnp.mean(vc * vc, axis=-1, keepdims=True) + EPS) * gsgu_ref[...]).astype(BF16)
    wrow = lax.broadcasted_iota(I32, (CHUNK, GMLP_GROUPS * CHUNK), 0)
    wcol = lax.broadcasted_iota(I32, (CHUNK, GMLP_GROUPS * CHUNK), 1) % CHUNK
    w_causal = jnp.where(wcol <= wrow, wsp_ref[...], 0.0).astype(BF16)
    grp = lax.broadcasted_iota(I32, (CHUNK, GMLP_W), 1) // HEAD_GROUP
    zero_b = jnp.zeros((CHUNK, GMLP_W), BF16)
    for r in range(tm // CHUNK):
        vchunk = vgn[r * CHUNK:(r + 1) * CHUNK, :]
        v_bd = jnp.concatenate([jnp.where(grp == g, vchunk, zero_b) for g in range(GMLP_GROUPS)], axis=0)
        mixed = _dot(w_causal, v_bd) + bsp_ref[...]
        gc_out[r * CHUNK:(r + 1) * CHUNK, 0:GMLP_W] = (u[r * CHUNK:(r + 1) * CHUNK, :] * mixed).astype(BF16)

    pc = proj(3 * DIFF_W + 2 * GMLP_W, D_IN_PROJ)
    qc = _group_rms(pc, ones256_ref[...], gcq_ref[...] * (HEAD_GROUP ** -0.5)).astype(BF16)
    hgrp = lax.broadcasted_iota(I32, (tm, MEM_W), 1) // HEAD_GROUP
    zero_q = jnp.zeros((tm, MEM_W), BF16)
    q_st = jnp.concatenate([jnp.where(hgrp == h, qc, zero_q) for h in range(N_MEM_HEADS)], axis=0)
    s = _dot(q_st, kT_ref[0])
    p = jnp.exp(s - jnp.max(s, axis=-1, keepdims=True))
    o = _dot(p.astype(BF16), vm_ref[0]) / jnp.sum(p, axis=-1, keepdims=True)
    c = jnp.zeros((tm, MEM_W), F32)
    for h in range(N_MEM_HEADS):
        c = c + jnp.where(hgrp == h, o[h * tm:(h + 1) * tm, :], 0.0)
    gc_out[:, GMLP_W:GMLP_W + MEM_W] = c.astype(BF16)


def _diff_attn_kernel(q_ref, k_ref, v_ref, lq1_ref, lk1_ref, lq2_ref, lk2_ref, gsub_ref, o_ref,
                      s0_ref, s1_ref, m_ref, acc_ref, vT_ref):
    tq = ATTN_TILE
    tiles_per_step = q_ref.shape[0] // tq
    assert tiles_per_step % 2 == 0
    seq = k_ref.shape[0]
    step = pl.program_id(1)
    feat = lax.broadcasted_iota(I32, (DIFF_V_DIM, tq), 0)
    heads = range(N_DIFF_HEADS)
    hl = lambda h: slice(h * DIFF_V_DIM, (h + 1) * DIFF_V_DIM)
    ext_rows = vT_ref.shape[1]

    @pl.when(step == 0)
    def _():
        ones_row = jnp.where(lax.broadcasted_iota(I32, (ext_rows - DIFF_V_DIM, seq), 0) == 0, 1.0, 0.0).astype(BF16)
        for h in heads:
            for c in range(seq // tq):
                vT_ref[h, 0:DIFF_V_DIM, c * tq:(c + 1) * tq] = v_ref[c * tq:(c + 1) * tq, hl(h)].T
            vT_ref[h, DIFF_V_DIM:ext_rows, :] = ones_row

    @pl.when(jnp.logical_and(pl.program_id(0) == 0, step == 0))
    def _():
        acc_ref[...] = jnp.zeros(acc_ref.shape, F32)

    lam = (jnp.exp(jnp.sum(lq1_ref[...] * lk1_ref[...], axis=-1, keepdims=True))
           - jnp.exp(jnp.sum(lq2_ref[...] * lk2_ref[...], axis=-1, keepdims=True)) + LAMBDA_INIT)

    def q_tile(j, first_ref, other_ref):
        i = tiles_per_step * step + j
        q_rows = slice(j * tq, (j + 1) * tq)

        def stacked_qT(h):
            qT = q_ref[q_rows, hl(h)].T
            zero = jnp.zeros_like(qT)
            return jnp.concatenate(
                [jnp.where(feat < DIFF_HEAD_DIM, qT, zero), jnp.where(feat >= DIFF_HEAD_DIM, qT, zero)], axis=1)

        qsT = [stacked_qT(h) for h in heads]

        def scores(t, s_ref):
            rows = pl.ds(pl.multiple_of(t * tq, tq), tq)
            for h in heads:
                s_ref[h] = _dot(k_ref[rows, hl(h)], qsT[h])

        def update(t, s_ref, causal):
            cols = pl.ds(pl.multiple_of(t * tq, tq), tq)
            for h in heads:
                s = s_ref[h]
                if causal:
                    key = lax.broadcasted_iota(I32, (tq, 2 * tq), 0)
                    qry = lax.broadcasted_iota(I32, (tq, 2 * tq), 1) % tq
                    s = jnp.where(key <= qry, s, NEG_BIG)
                m = m_ref[h]
                m_new = jnp.maximum(m, jnp.max(s, axis=0, keepdims=True))
                alpha = jnp.exp2(m - m_new)
                p = jnp.exp2(s - m_new)
                m_ref[h] = m_new
                acc_ref[h] = alpha * acc_ref[h] + _dot(vT_ref[h, :, cols], p.astype(BF16))

        m_ref[...] = jnp.full(m_ref.shape, NEG_BIG, F32)
        scores(0, first_ref)

        def two_tiles(pidx, carry):
            t = 2 * pidx
            scores(t + 1, other_ref)
            update(t, first_ref, False)
            scores(t + 2, first_ref)
            update(t + 1, other_ref, False)
            return carry

        lax.fori_loop(0, (tiles_per_step // 2) * step + j // 2, two_tiles, 0)

        if j % 2 == 0:
            update(i, first_ref, True)
        else:
            scores(i, other_ref)
            update(i - 1, first_ref, False)
            update(i, other_ref, True)

        for h in heads:
            on = acc_ref[h, 0:DIFF_V_DIM, :] * (1.0 / acc_ref[h, DIFF_V_DIM:DIFF_V_DIM + 1, :])
            o = on[:, :tq] - lam * on[:, tq:]
            ms = jnp.mean(o * o, axis=0, keepdims=True)
            o = o * lax.rsqrt(ms + EPS) * gsub_ref[...] * (1.0 - LAMBDA_INIT)
            o_ref[q_rows, hl(h)] = o.T.astype(BF16)

    first_ref, other_ref = s0_ref, s1_ref
    for j in range(tiles_per_step):
        q_tile(j, first_ref, other_ref)
        last_read = first_ref if j % 2 == 0 else other_ref
        first_ref, other_ref = (other_ref, first_ref) if last_read is first_ref else (first_ref, other_ref)


def _out_router_kernel(x_ref, a_ref, gc_ref, wo_ref, gffn_ref, wrh_ref, wrl_ref, br_ref, before_ref,
                       x1_out, hpa_out, hpb_out, idx_out, gate_out, pos_out, cnt_out, carry_ref):
    tm = x_ref.shape[0]

    @pl.when(pl.program_id(0) == 0)
    def _():
        carry_ref[...] = jnp.zeros_like(carry_ref)

    sub = before_ref.shape[0]
    wr_stack = jnp.concatenate([wrh_ref[...], wrl_ref[...]], axis=0)
    eio = lax.broadcasted_iota(I32, (N_EXPERTS, sub), 0)
    before_b = before_ref[...]
    carry = carry_ref[...]
    for r in range(tm // sub):
        rows = slice(r * sub, (r + 1) * sub)
        mix = jnp.concatenate([a_ref[rows, :], gc_ref[rows, :]], axis=1)
        x1 = x_ref[rows, :] + _dot(mix, wo_ref[...])
        x1_out[rows, :] = x1
        h2 = _rms(x1, gffn_ref[...])
        hb = h2.astype(BF16)
        hpa_out[rows, :] = _pack_bf16_pairs(h2[:, :D_MODEL // 2])
        hpb_out[rows, :] = _pack_bf16_pairs(h2[:, D_MODEL // 2:])

        h_lo = (h2 - hb.astype(F32)).astype(BF16)
        both = _dot_nt(wr_stack, hb)
        logits = (both[:N_EXPERTS] + both[N_EXPERTS:]) + _dot_nt(wrh_ref[...], h_lo) + br_ref[...]

        vals, idxs, sels = [], [], []
        cur = logits
        for _ in range(TOP_K):
            m = jnp.max(cur, axis=0, keepdims=True)
            ik = jnp.min(jnp.where(cur == m, eio, N_EXPERTS), axis=0, keepdims=True)
            sel = eio == ik
            cur = jnp.where(sel, -jnp.inf, cur)
            vals.append(m)
            idxs.append(ik)
            sels.append(sel)
        es = [jnp.exp(v - vals[0]) for v in vals]
        tot = es[0] + es[1] + es[2] + es[3]
        gates = jnp.concatenate([e / tot for e in es], axis=0)
        g_hi = gates.astype(BF16)
        g_lo = (gates - g_hi.astype(F32)).astype(BF16)
        gate_out[:, rows] = jnp.concatenate(
            [g_hi, g_lo, jnp.zeros((GATE_ROWS - 2 * TOP_K, sub), BF16)], axis=0)
        idx_out[:, rows] = jnp.concatenate(idxs, axis=0)

        cnt = jnp.zeros((N_EXPERTS, sub), F32)
        for sel in sels:
            cnt = cnt + jnp.where(sel, 1.0, 0.0)
        base = carry + _dot(cnt.astype(BF16), before_b)
        pos_out[:, rows] = jnp.concatenate(
            [jnp.sum(jnp.where(sel, base, 0.0), axis=0, keepdims=True) for sel in sels], axis=0).astype(I32)
        carry = carry + jnp.sum(cnt, axis=1, keepdims=True)
    carry_ref[...] = carry
    cnt_out[...] = carry


def _dest_kernel(cnt_ref, idx_ref, pos_ref, dest_out, be_out, valid_out, nbu_out):
    idx = idx_ref[...]
    dest = pos_ref[...]
    bidx = lax.broadcasted_iota(I32, be_out.shape, 1)
    be = jnp.zeros(be_out.shape, I32)
    valid = jnp.zeros(be_out.shape, I32)
    run = jnp.int32(0)
    for e in range(N_EXPERTS):
        dest = dest + jnp.where(idx == e, run, 0)
        first_block = run // ROW_BLOCK
        run = run + ((cnt_ref[e] + (ROW_BLOCK - 1)) // ROW_BLOCK) * ROW_BLOCK
        mine = jnp.logical_and(bidx >= first_block, bidx < run // ROW_BLOCK)
        valid = jnp.where(mine, jnp.clip(cnt_ref[e] - (bidx - first_block) * ROW_BLOCK, 0, ROW_BLOCK), valid)
        be = be + jnp.where(bidx >= run // ROW_BLOCK, 1, 0)
    dest_out[...] = dest
    be_out[...] = jnp.minimum(be, N_EXPERTS - 1)
    valid_out[...] = valid
    nbu_out[...] = jnp.zeros(nbu_out.shape, I32) + run // ROW_BLOCK


def _expert_ffn_kernel(be_ref, nbu_ref, nxt_ref, valid_ref, xa_ref, xb_ref, w1_hbm, b1_ref, w2_hbm, b2_ref,
                       y_ref, w1f_ref, w2f_ref, w1s_ref, w2s_ref, sem_ref):
    b = pl.program_id(0)

    def weight_copies(e):
        return (pltpu.make_async_copy(w1_hbm.at[e], w1f_ref, sem_ref.at[0]),
                pltpu.make_async_copy(w2_hbm.at[e], w2f_ref, sem_ref.at[1]))

    def mlp(rows):
        parts = _unpack_bf16_pairs(xa_ref[0:rows, :]) + _unpack_bf16_pairs(xb_ref[0:rows, :])
        xrow = jnp.concatenate([p.astype(BF16) for p in parts], axis=1)
        hm = _dot(xrow, w1s_ref[...]) + b1_ref[0]
        glu = jnp.minimum(hm[:, :D_FF], SWIGLU_LIMIT)
        lin = jnp.clip(hm[:, D_FF:], -SWIGLU_LIMIT, SWIGLU_LIMIT)
        act = glu * jax.nn.sigmoid(SWIGLU_ALPHA * glu) * (lin + 1.0)
        y = _dot(act.astype(BF16), w2s_ref[...]) + b2_ref[0]
        hw = D_MODEL // 4
        y_ref[0:rows, 0:hw] = _pack_bf16_pairs(y[:, :D_MODEL // 2])
        y_ref[0:rows, hw:] = _pack_bf16_pairs(y[:, D_MODEL // 2:])

    @pl.when(b < nbu_ref[0])
    def _():
        e = be_ref[b]

        @pl.when(b == 0)
        def _():
            for cp in weight_copies(e):
                cp.start()

        @pl.when(jnp.logical_or(b == 0, e != be_ref[jnp.maximum(b - 1, 0)]))
        def _():
            for cp in weight_copies(e):
                cp.wait()
            w1s_ref[...] = w1f_ref[...].astype(BF16)
            w2s_ref[...] = w2f_ref[...].astype(BF16)
            nxt = nxt_ref[e]

            @pl.when(nxt >= 0)
            def _():
                for cp in weight_copies(nxt):
                    cp.start()

        n_sub = (valid_ref[b] + (FFN_SUB_ROWS - 1)) // FFN_SUB_ROWS
        for k in range(1, ROW_BLOCK // FFN_SUB_ROWS + 1):
            pl.when(n_sub == k)(functools.partial(mlp, k * FFN_SUB_ROWS))


SC_WINDOW = 128
SC_GATHER_WINDOW = 32


def _sc_mesh():
    return plsc.VectorSubcoreMesh(core_axis_name="c", subcore_axis_name="s")


def _sc_gather_rows(table, idx):
    n = idx.shape[0]
    width = table.shape[1]
    win = SC_GATHER_WINDOW

    @functools.partial(pl.kernel, out_type=jax.ShapeDtypeStruct((n, width), table.dtype), mesh=_sc_mesh(),
                       scratch_types=[])
    def gather_kernel(t_hbm, i_hbm, o_hbm):
        def body(i_vmem, o_vmem):
            pltpu.sync_copy(t_hbm.at[i_vmem.at[0]], o_vmem)

        pltpu.emit_pipeline(
            body,
            grid=(n // win,),
            in_specs=[pl.BlockSpec((1, win), lambda i: (i, 0))],
            out_specs=[pl.BlockSpec((win, width), lambda i: (i, 0))],
            core_axis_name=("c", "s"),
            dimension_semantics=(pltpu.PARALLEL,),
        )(i_hbm, o_hbm)

    return gather_kernel(table, idx.reshape(n // win, win))


def _sc_scatter_rows(rows, idx_rows, n_out):
    n, width = rows.shape

    @functools.partial(pl.kernel, out_type=jax.ShapeDtypeStruct((n_out, width), rows.dtype), mesh=_sc_mesh(),
                       scratch_types=[])
    def scatter_kernel(r_hbm, *refs):
        i_hbms, o_hbm = refs[:-1], refs[-1]

        def body(r_vmem, *i_vmems):
            for i_vmem in i_vmems:
                pltpu.sync_copy(r_vmem, o_hbm.at[i_vmem.at[0]])

        pltpu.emit_pipeline(
            body,
            grid=(n // SC_WINDOW,),
            in_specs=[pl.BlockSpec((SC_WINDOW, width), lambda i: (i, 0))]
            + [pl.BlockSpec((1, SC_WINDOW), lambda i: (0, i)) for _ in i_hbms],
            out_specs=[],
            core_axis_name=("c", "s"),
            dimension_semantics=(pltpu.PARALLEL,),
        )(r_hbm, *i_hbms)

    return scatter_kernel(rows, *idx_rows)


def _combine_kernel(x1_ref, yg_ref, gate_ref, spread_ref, o_ref):
    q = D_MODEL // 4
    x1 = x1_ref[...]
    acc = [x1[:, j * q:(j + 1) * q] for j in range(4)]
    gfull = lax.dot_general(gate_ref[...], spread_ref[...], (((0,), (0,)), ((), ())), preferred_element_type=F32)
    for k in range(TOP_K):
        parts = _unpack_bf16_pairs(yg_ref[k, :, 0:q]) + _unpack_bf16_pairs(yg_ref[k, :, q:])
        g = jnp.tile(gfull[:, k * LANES:(k + 1) * LANES], (1, q // LANES))
        acc = [a + g * p for a, p in zip(acc, parts)]
    for j in range(4):
        o_ref[:, j * q:(j + 1) * q] = acc[j]


def _combine_into_kernel(x1_ref, yg_ref, gate_ref, spread_ref, prev_ref, o_ref):
    del prev_ref
    _combine_kernel(x1_ref, yg_ref, gate_ref, spread_ref, o_ref)


def _block_diag_ones(width):
    r = jnp.arange(width) // HEAD_GROUP
    return (r[:, None] == r[None, :]).astype(BF16)


def _rope_tables(positions):
    half = ROT_DIM // 2
    inv_freq = ROPE_THETA ** (-jnp.arange(0, ROT_DIM, 2, dtype=F32) / ROT_DIM)
    ang = inv_freq[:, None] * positions.astype(F32).reshape(1, -1)
    cs = jnp.concatenate([jnp.cos(ang), jnp.sin(ang)], axis=0)
    cs_hi = cs.astype(BF16)
    cs_lo = (cs - cs_hi.astype(F32)).astype(BF16)
    lane = jnp.arange(LANES) % HEAD_GROUP
    j = jnp.arange(half)[:, None]
    lo_half = (lane[None, :] == j).astype(F32)
    hi_half = (lane[None, :] == j + half).astype(F32)
    spread = jnp.concatenate([
        jnp.concatenate([lo_half + hi_half, jnp.zeros((half, LANES), F32)], axis=1),
        jnp.concatenate([jnp.zeros((half, LANES), F32), hi_half - lo_half], axis=1)], axis=0)
    return jnp.concatenate([cs_hi, cs_lo], axis=0), jnp.concatenate([spread, spread], axis=0).astype(BF16)


def _full(shape):
    return pl.BlockSpec(shape, lambda *_: (0,) * len(shape))


def kernel(x, mem, positions, g_mix_norm, w_in, g_dq, g_dk, lambda_q1, lambda_k1, lambda_q2, lambda_k2, g_subln, g_sgu, w_spatial, b_spatial, g_mem_norm, w_mem_kv, g_cq, g_ck, w_out, g_ffn_norm, w_router, b_router, w_mlp1, b_mlp1, w_mlp2, b_mlp2):
    B, S, D = x.shape
    M = mem.shape[1]
    tm = TOKEN_TILE
    tr = ROUTER_TILE
    assert D == D_MODEL and S % tm == 0 and S % (ATTN_Q_TILES_PER_STEP * ATTN_TILE) == 0 and g_mix_norm.shape[0] == 1
    last_batches = B // LAST_PART_DIVISOR
    part_batches = (B - last_batches, last_batches) if last_batches > 0 and S % tr == 0 else (B,)

    xf = x.reshape(B * S, D)
    rope_cs, rope_spread = _rope_tables(positions)
    ones256 = _block_diag_ones(MEM_W)
    row = lambda v: v.reshape(1, -1).astype(F32)
    tile_row = lambda v, reps: jnp.tile(v.reshape(1, -1).astype(F32), (1, reps))
    w_in_b, w_out_b, w_kv_b = w_in[0].astype(BF16), w_out[0].astype(BF16), w_mem_kv[0].astype(BF16)
    w_sp_lanes = jnp.transpose(w_spatial[0], (1, 0, 2)).reshape(CHUNK, GMLP_GROUPS * CHUNK)
    b_sp_lanes = jnp.repeat(b_spatial[0].T, HEAD_GROUP, axis=1)
    wr = w_router[0].T.astype(F32)
    wr_hi = wr.astype(BF16)
    wr_lo = (wr - wr_hi.astype(F32)).astype(BF16)
    before = (jnp.arange(tr)[:, None] < jnp.arange(tr)[None, :]).astype(BF16)
    gate_row = jnp.arange(GATE_ROWS)[:, None]
    gate_spread = ((gate_row < 2 * TOP_K) & (gate_row % TOP_K == jnp.arange(TOP_K * LANES)[None, :] // LANES)).astype(BF16)
    b1r = b_mlp1[0].reshape(N_EXPERTS, 1, 2 * D_FF)
    b2r = b_mlp2[0].reshape(N_EXPERTS, 1, D)

    out = None
    b_off = 0
    for Bp in part_batches:
        N = Bp * S
        assert N % tr == 0 and N % tm == 0
        n_assign = N * TOP_K
        n_blocks = -(-n_assign // ROW_BLOCK) + N_EXPERTS
        n_rows = n_blocks * ROW_BLOCK
        nb_pad = -(-n_blocks // LANES) * LANES
        t_off = b_off * S // tm
        r_off = b_off * S // tr

        mb = MEM_BATCHES if Bp % MEM_BATCHES == 0 and b_off % MEM_BATCHES == 0 else 1
        kT, vm = pl.pallas_call(
            _mem_kv_kernel,
            grid=(Bp // mb,),
            in_specs=[pl.BlockSpec((mb, M, D), lambda b: (b + b_off // mb, 0, 0)), _full((1, D)), _full((D, 2 * MEM_W)),
                      _full((1, MEM_W)), _full((MEM_W, MEM_W))],
            out_specs=[pl.BlockSpec((mb, MEM_W, M), lambda b: (b, 0, 0)),
                       pl.BlockSpec((mb, M, MEM_W), lambda b: (b, 0, 0))],
            out_shape=[jax.ShapeDtypeStruct((Bp, MEM_W, M), BF16), jax.ShapeDtypeStruct((Bp, M, MEM_W), BF16)],
            compiler_params=_cparams(("parallel",)),
            name="mem_kv",
        )(mem, row(g_mem_norm[0]), w_kv_b, tile_row(g_ck[0], N_MEM_HEADS), ones256)

        tiles_per_batch = S // tm
        tok = lambda w: pl.BlockSpec((tm, w), lambda i: (i, 0))
        tok_in = lambda w: pl.BlockSpec((tm, w), lambda i: (i + t_off, 0))
        qn, kn, vv, gc = pl.pallas_call(
            _mixer_in_kernel,
            grid=(N // tm,),
            in_specs=[tok_in(D), pl.BlockSpec((ROPE_STACK_ROWS, tm), lambda i: (0, i + t_off)),
                      _full((ROPE_STACK_ROWS, 2 * LANES)),
                      _full((1, D)), _full((D, D_IN_PROJ)),
                      _full((1, DIFF_W)), _full((1, DIFF_W)),
                      _full((1, GMLP_W)), _full((CHUNK, GMLP_GROUPS * CHUNK)), _full((CHUNK, GMLP_W)),
                      _full((1, MEM_W)), _full((MEM_W, MEM_W)),
                      pl.BlockSpec((1, MEM_W, M), lambda i: (i // tiles_per_batch, 0, 0)),
                      pl.BlockSpec((1, M, MEM_W), lambda i: (i // tiles_per_batch, 0, 0))],
            out_specs=[tok(DIFF_W), tok(DIFF_W), tok(DIFF_W), tok(GMLP_W + MEM_W)],
            out_shape=[jax.ShapeDtypeStruct((N, DIFF_W), BF16)] * 3 + [jax.ShapeDtypeStruct((N, GMLP_W + MEM_W), BF16)],
            compiler_params=_cparams(("parallel",)),
            name="mixer_in",
        )(xf, rope_cs, rope_spread, row(g_mix_norm[0]), w_in_b,
          tile_row(g_dq[0], 2 * N_DIFF_HEADS), tile_row(g_dk[0], 2 * N_DIFF_HEADS),
          row(g_sgu[0]), w_sp_lanes, b_sp_lanes, tile_row(g_cq[0], N_MEM_HEADS), ones256, kT, vm)

        tq = ATTN_TILE
        q_rows = ATTN_Q_TILES_PER_STEP * tq
        nqp = S // q_rows
        head_q = pl.BlockSpec((q_rows, DIFF_W), lambda b, i: (b * nqp + i, 0))
        head_kv = pl.BlockSpec((S, DIFF_W), lambda b, i: (b, 0))
        lam_spec = pl.BlockSpec((1, DIFF_HEAD_DIM), lambda b, i: (0, 0))
        nh = N_DIFF_HEADS
        attn = pl.pallas_call(
            _diff_attn_kernel,
            grid=(Bp, nqp),
            in_specs=[head_q, head_kv, head_kv, lam_spec, lam_spec, lam_spec, lam_spec,
                      pl.BlockSpec((DIFF_V_DIM, 1), lambda b, i: (0, 0))],
            out_specs=head_q,
            out_shape=jax.ShapeDtypeStruct((N, DIFF_W), BF16),
            scratch_shapes=[pltpu.VMEM((nh, tq, 2 * tq), F32), pltpu.VMEM((nh, tq, 2 * tq), F32),
                            pltpu.VMEM((nh, 1, 2 * tq), F32), pltpu.VMEM((nh, ATTN_EXT_ROWS, 2 * tq), F32),
                            pltpu.VMEM((nh, ATTN_EXT_ROWS, S), BF16)],
            compiler_params=_cparams(("arbitrary", "arbitrary")),
            name="diff_attn",
        )(qn, kn, vv, row(lambda_q1[0]), row(lambda_k1[0]), row(lambda_q2[0]), row(lambda_k2[0]),
          g_subln[0].reshape(DIFF_V_DIM, 1).astype(F32))

        rtok = lambda w: pl.BlockSpec((tr, w), lambda i: (i, 0))
        rtok_in = lambda w: pl.BlockSpec((tr, w), lambda i: (i + r_off, 0))
        tokT = lambda: pl.BlockSpec((TOP_K, tr), lambda i: (0, i))
        hw = D // 4
        x1, hpa, hpb, idxT, gateT, posT, counts = pl.pallas_call(
            _out_router_kernel,
            grid=(N // tr,),
            in_specs=[rtok_in(D), rtok(DIFF_W), rtok(GMLP_W + MEM_W), _full((D, D)), _full((1, D)),
                      _full((N_EXPERTS, D)), _full((N_EXPERTS, D)), _full((N_EXPERTS, 1)), _full((tr, tr))],
            out_specs=[rtok(D), rtok(hw), rtok(hw), tokT(), pl.BlockSpec((GATE_ROWS, tr), lambda i: (0, i)), tokT(),
                       _full((N_EXPERTS, 1))],
            out_shape=[jax.ShapeDtypeStruct((N, D), F32), jax.ShapeDtypeStruct((N, hw), U32),
                       jax.ShapeDtypeStruct((N, hw), U32), jax.ShapeDtypeStruct((TOP_K, N), I32),
                       jax.ShapeDtypeStruct((GATE_ROWS, N), BF16),
                       jax.ShapeDtypeStruct((TOP_K, N), I32), jax.ShapeDtypeStruct((N_EXPERTS, 1), F32)],
            scratch_shapes=[pltpu.VMEM((N_EXPERTS, 1), F32)],
            compiler_params=_cparams(("arbitrary",)),
            name="out_router",
        )(xf, attn, gc, w_out_b, row(g_ffn_norm[0]), wr_hi, wr_lo, b_router[0].reshape(N_EXPERTS, 1).astype(F32), before)

        destT, block_expert, block_valid, nb_used = pl.pallas_call(
            _dest_kernel,
            grid_spec=pltpu.PrefetchScalarGridSpec(
                num_scalar_prefetch=1,
                grid=(1,),
                in_specs=[pl.BlockSpec((TOP_K, N), lambda i, c: (0, 0)), pl.BlockSpec((TOP_K, N), lambda i, c: (0, 0))],
                out_specs=[pl.BlockSpec((TOP_K, N), lambda i, c: (0, 0)), pl.BlockSpec((1, nb_pad), lambda i, c: (0, 0)),
                           pl.BlockSpec((1, nb_pad), lambda i, c: (0, 0)), pl.BlockSpec((1, LANES), lambda i, c: (0, 0))],
            ),
            out_shape=[jax.ShapeDtypeStruct((TOP_K, N), I32), jax.ShapeDtypeStruct((1, nb_pad), I32),
                       jax.ShapeDtypeStruct((1, nb_pad), I32), jax.ShapeDtypeStruct((1, LANES), I32)],
            compiler_params=_cparams(("arbitrary",)),
            name="dest",
        )(counts.reshape(N_EXPERTS).astype(I32), idxT, posT)

        dest_rows = [destT[k].reshape(1, N) for k in range(TOP_K)]
        xa_buf = _sc_scatter_rows(hpa, dest_rows, n_rows)
        xb_buf = _sc_scatter_rows(hpb, dest_rows, n_rows)

        cnt_i = counts.reshape(N_EXPERTS).astype(I32)
        owner = jnp.where(cnt_i > 0, jnp.arange(N_EXPERTS, dtype=I32), N_EXPERTS)
        later = jnp.concatenate([lax.cummin(owner[::-1])[::-1][1:], jnp.full((1,), N_EXPERTS, I32)])
        next_expert = jnp.where(later < N_EXPERTS, later, -1)
        last = lambda b, be, nbu, nxt, valid: jnp.minimum(b, nbu[0] - 1)
        row_blk = lambda: pl.BlockSpec((ROW_BLOCK, hw), lambda b, be, nbu, nxt, valid: (last(b, be, nbu, nxt, valid), 0))
        y_blk = pl.BlockSpec((ROW_BLOCK, 2 * hw), lambda b, be, nbu, nxt, valid: (last(b, be, nbu, nxt, valid), 0))
        y_buf = pl.pallas_call(
            _expert_ffn_kernel,
            grid_spec=pltpu.PrefetchScalarGridSpec(
                num_scalar_prefetch=4,
                grid=(n_blocks,),
                in_specs=[row_blk(), row_blk(),
                          pl.BlockSpec(memory_space=pl.ANY),
                          pl.BlockSpec((1, 1, 2 * D_FF), lambda b, be, nbu, nxt, valid: (be[b], 0, 0)),
                          pl.BlockSpec(memory_space=pl.ANY),
                          pl.BlockSpec((1, 1, D), lambda b, be, nbu, nxt, valid: (be[b], 0, 0))],
                out_specs=y_blk,
                scratch_shapes=[pltpu.VMEM((D, 2 * D_FF), F32), pltpu.VMEM((D_FF, D), F32),
                                pltpu.VMEM((D, 2 * D_FF), BF16), pltpu.VMEM((D_FF, D), BF16),
                                pltpu.SemaphoreType.DMA((2,))],
            ),
            out_shape=jax.ShapeDtypeStruct((n_rows, 2 * hw), U32),
            compiler_params=_cparams(("arbitrary",)),
            name="expert_ffn",
        )(block_expert[0, :n_blocks], nb_used[0, :1], next_expert, block_valid[0, :n_blocks], xa_buf, xb_buf,
          w_mlp1[0], b1r, w_mlp2[0], b2r)

        yg = _sc_gather_rows(y_buf, destT.reshape(n_assign)).reshape(TOP_K, N, 2 * hw)
        tc = COMBINE_TILE
        prev = () if out is None else (out,)
        out = pl.pallas_call(
            _combine_kernel if out is None else _combine_into_kernel,
            grid=(N // tc,),
            in_specs=[pl.BlockSpec((tc, D), lambda i: (i, 0)),
                      pl.BlockSpec((TOP_K, tc, 2 * hw), lambda i: (0, i, 0)),
                      pl.BlockSpec((GATE_ROWS, tc), lambda i: (0, i)),
                      _full((GATE_ROWS, TOP_K * LANES))] + [pl.BlockSpec(memory_space=pl.ANY)] * len(prev),
            out_specs=pl.BlockSpec((tc, D), lambda i: (i + b_off * S // tc, 0)),
            out_shape=jax.ShapeDtypeStruct((B * S, D), F32),
            input_output_aliases={4: 0} if prev else {},
            compiler_params=_cparams(("parallel",)),
            name="combine",
        )(x1, yg, gateT, gate_spread, *prev)
        b_off += Bp
    return out.reshape(B, S, D)
```

```python
import functools

import jax
import jax.numpy as jnp
from jax import lax
from jax.experimental import pallas as pl
from jax.experimental.pallas import tpu as pltpu
from jax.experimental.pallas import tpu_sc as plsc

F32 = jnp.float32
BF16 = jnp.bfloat16
I32 = jnp.int32
U32 = jnp.uint32

D_MODEL = 1024
N_DIFF_HEADS = 4
DIFF_HEAD_DIM = 64
DIFF_V_DIM = 128
DIFF_W = 512
GMLP_W = 256
GMLP_GROUPS = 4
CHUNK = 128
MEM_W = 256
N_MEM_HEADS = 4
HEAD_GROUP = 64
D_IN_PROJ = 2304
ROPE_THETA = 500000.0
ROT_DIM = 16
ROPE_STACK_ROWS = 2 * ROT_DIM
N_EXPERTS = 32
TOP_K = 4
D_FF = 1024
SWIGLU_LIMIT = 7.0
SWIGLU_ALPHA = 1.702
EPS = 1e-6
LAMBDA_INIT = 0.8 - 0.6

LANES = 128
ROW_BLOCK = 1024
FFN_SUB_ROWS = 256
TOKEN_TILE = 1024
COMBINE_TILE = 1024
MEM_BATCHES = 4
LAST_PART_DIVISOR = 2
ROUTER_TILE = 1024
GATE_ROWS = 16
ATTN_TILE = 256
ATTN_Q_TILES_PER_STEP = 8
ATTN_EXT_ROWS = 128 + 16
VMEM_LIMIT = 56 * 1024 * 1024
NEG_BIG = -1e30
LOG2_E = 1.4426950408889634


def _cparams(sem):
    return pltpu.CompilerParams(dimension_semantics=sem, vmem_limit_bytes=VMEM_LIMIT)


def _dot(a, b):
    return jnp.dot(a, b, preferred_element_type=F32)


def _dot_nt(a, b):
    return lax.dot_general(a, b, (((1,), (1,)), ((), ())), preferred_element_type=F32)


def _rms(x, gain):
    ms = jnp.mean(x * x, axis=-1, keepdims=True)
    return x * lax.rsqrt(ms + EPS) * gain


def _group_rms(t, ones_bd, gain):
    w = ones_bd.shape[0]
    chunks = []
    for j in range(t.shape[1] // w):
        c = t[:, j * w:(j + 1) * w]
        ss = _dot((c * c).astype(BF16), ones_bd)
        chunks.append(c * lax.rsqrt(ss * (1.0 / HEAD_GROUP) + EPS))
    return (chunks[0] if len(chunks) == 1 else jnp.concatenate(chunks, axis=1)) * gain


def _pack_bf16_pairs(v):
    w = v.shape[1] // 2
    bits = lax.bitcast_convert_type(v.astype(BF16).astype(F32), U32)
    return (bits[:, :w] & jnp.uint32(0xFFFF0000)) | (bits[:, w:] >> jnp.uint32(16))


def _unpack_bf16_pairs(words):
    hi = lax.bitcast_convert_type(words & jnp.uint32(0xFFFF0000), F32)
    lo = lax.bitcast_convert_type(words << jnp.uint32(16), F32)
    return hi, lo


def _mem_kv_kernel(mem_ref, gmem_ref, wkv_ref, gck_ref, ones_ref, kT_ref, v_ref):
    for j in range(mem_ref.shape[0]):
        m = _rms(mem_ref[j], gmem_ref[...]).astype(BF16)
        kv = _dot(m, wkv_ref[...])
        k = _group_rms(kv[:, :MEM_W], ones_ref[...], gck_ref[...])
        kT_ref[j] = k.T.astype(BF16)
        v_ref[j] = kv[:, MEM_W:].astype(BF16)


def _gelu_tanh(x):
    return 0.5 * x * (1.0 + jnp.tanh(0.7978845608028654 * (x + 0.044715 * (x * x * x))))


def _mixer_in_kernel(x_ref, cs_ref, spread_ref, gmix_ref, win_ref, gq_ref, gk_ref,
                     gsgu_ref, wsp_ref, bsp_ref, gcq_ref, ones256_ref, kT_ref, vm_ref,
                     q_out, k_out, v_out, gc_out):
    tm = x_ref.shape[0]
    hb = _rms(x_ref[...], gmix_ref[...]).astype(BF16)

    def proj(lo, hi):
        return _dot(hb, win_ref[:, lo:hi])

    lane = lax.broadcasted_iota(I32, (tm, LANES), 1)
    first_half = (lane % HEAD_GROUP) < (ROT_DIM // 2)
    tab = lax.dot_general(cs_ref[...], spread_ref[...], (((0,), (0,)), ((), ())), preferred_element_type=F32)
    cosb = tab[:, :LANES] + jnp.where((lane % HEAD_GROUP) >= ROT_DIM, 1.0, 0.0)
    sinb = tab[:, LANES:]

    def norm_rope(t, gain, out_ref):
        tn = _group_rms(t, ones256_ref[...], gain)
        for j in range(DIFF_W // LANES):
            c = tn[:, j * LANES:(j + 1) * LANES]
            partner = jnp.where(first_half, pltpu.roll(c, LANES - ROT_DIM // 2, 1), pltpu.roll(c, ROT_DIM // 2, 1))
            out_ref[:, j * LANES:(j + 1) * LANES] = (c * cosb + partner * sinb).astype(BF16)

    norm_rope(proj(0, DIFF_W), gq_ref[...] * (DIFF_HEAD_DIM ** -0.5 * LOG2_E), q_out)
    norm_rope(proj(DIFF_W, 2 * DIFF_W), gk_ref[...], k_out)
    v_out[...] = proj(2 * DIFF_W, 3 * DIFF_W).astype(BF16)

    z = _gelu_tanh(proj(3 * DIFF_W, 3 * DIFF_W + 2 * GMLP_W))
    u = z[:, :GMLP_W]
    vg = z[:, GMLP_W:]
    vc = vg - jnp.mean(vg, axis=-1, keepdims=True)
    vgn = (vc * lax.rsqrt(jnp.mean(vc * vc, axis=-1, keepdims=True) + EPS) * gsgu_ref[...]).astype(BF16)
    wrow = lax.broadcasted_iota(I32, (CHUNK, GMLP_GROUPS * CHUNK), 0)
    wcol = lax.broadcasted_iota(I32, (CHUNK, GMLP_GROUPS * CHUNK), 1) % CHUNK
    w_causal = jnp.where(wcol <= wrow, wsp_ref[...], 0.0).astype(BF16)
    grp = lax.broadcasted_iota(I32, (CHUNK, GMLP_W), 1) // HEAD_GROUP
    zero_b = jnp.zeros((CHUNK, GMLP_W), BF16)
    for r in range(tm // CHUNK):
        vchunk = vgn[r * CHUNK:(r + 1) * CHUNK, :]
        v_bd = jnp.concatenate([jnp.where(grp == g, vchunk, zero_b) for g in range(GMLP_GROUPS)], axis=0)
        mixed = _dot(w_causal, v_bd) + bsp_ref[...]
        gc_out[r * CHUNK:(r + 1) * CHUNK, 0:GMLP_W] = (u[r * CHUNK:(r + 1) * CHUNK, :] * mixed).astype(BF16)

    pc = proj(3 * DIFF_W + 2 * GMLP_W, D_IN_PROJ)
    qc = _group_rms(pc, ones256_ref[...], gcq_ref[...] * (HEAD_GROUP ** -0.5)).astype(BF16)
    hgrp = lax.broadcasted_iota(I32, (tm, MEM_W), 1) // HEAD_GROUP
    zero_q = jnp.zeros((tm, MEM_W), BF16)
    q_st = jnp.concatenate([jnp.where(hgrp == h, qc, zero_q) for h in range(N_MEM_HEADS)], axis=0)
    s = _dot(q_st, kT_ref[0])
    p = jnp.exp(s - jnp.max(s, axis=-1, keepdims=True))
    o = _dot(p.astype(BF16), vm_ref[0]) / jnp.sum(p, axis=-1, keepdims=True)
    c = jnp.zeros((tm, MEM_W), F32)
    for h in range(N_MEM_HEADS):
        c = c + jnp.where(hgrp == h, o[h * tm:(h + 1) * tm, :], 0.0)
    gc_out[:, GMLP_W:GMLP_W + MEM_W] = c.astype(BF16)


def _diff_attn_kernel(q_ref, k_ref, v_ref, lq1_ref, lk1_ref, lq2_ref, lk2_ref, gsub_ref, o_ref,
                      s0_ref, s1_ref, m_ref, acc_ref, vT_ref):
    tq = ATTN_TILE
    tiles_per_step = q_ref.shape[0] // tq
    assert tiles_per_step % 2 == 0
    seq = k_ref.shape[0]
    step = pl.program_id(1)
    feat = lax.broadcasted_iota(I32, (DIFF_V_DIM, tq), 0)
    heads = range(N_DIFF_HEADS)
    hl = lambda h: slice(h * DIFF_V_DIM, (h + 1) * DIFF_V_DIM)
    ext_rows = vT_ref.shape[1]

    @pl.when(step == 0)
    def _():
        ones_row = jnp.where(lax.broadcasted_iota(I32, (ext_rows - DIFF_V_DIM, seq), 0) == 0, 1.0, 0.0).astype(BF16)
        for h in heads:
            for c in range(seq // tq):
                vT_ref[h, 0:DIFF_V_DIM, c * tq:(c + 1) * tq] = v_ref[c * tq:(c + 1) * tq, hl(h)].T
            vT_ref[h, DIFF_V_DIM:ext_rows, :] = ones_row

    @pl.when(jnp.logical_and(pl.program_id(0) == 0, step == 0))
    def _():
        acc_ref[...] = jnp.zeros(acc_ref.shape, F32)

    lam = (jnp.exp(jnp.sum(lq1_ref[...] * lk1_ref[...], axis=-1, keepdims=True))
           - jnp.exp(jnp.sum(lq2_ref[...] * lk2_ref[...], axis=-1, keepdims=True)) + LAMBDA_INIT)

    def q_tile(j, first_ref, other_ref):
        i = tiles_per_step * step + j
        q_rows = slice(j * tq, (j + 1) * tq)

        def stacked_qT(h):
            qT = q_ref[q_rows, hl(h)].T
            zero = jnp.zeros_like(qT)
            return jnp.concatenate(
                [jnp.where(feat < DIFF_HEAD_DIM, qT, zero), jnp.where(feat >= DIFF_HEAD_DIM, qT, zero)], axis=1)

        qsT = [stacked_qT(h) for h in heads]

        def scores(t, s_ref):
            rows = pl.ds(pl.multiple_of(t * tq, tq), tq)
            for h in heads:
                s_ref[h] = _dot(k_ref[rows, hl(h)], qsT[h])

        def update(t, s_ref, causal):
            cols = pl.ds(pl.multiple_of(t * tq, tq), tq)
            for h in heads:
                s = s_ref[h]
                if causal:
                    key = lax.broadcasted_iota(I32, (tq, 2 * tq), 0)
                    qry = lax.broadcasted_iota(I32, (tq, 2 * tq), 1) % tq
                    s = jnp.where(key <= qry, s, NEG_BIG)
                m = m_ref[h]
                m_new = jnp.maximum(m, jnp.max(s, axis=0, keepdims=True))
                alpha = jnp.exp2(m - m_new)
                p = jnp.exp2(s - m_new)
                m_ref[h] = m_new
                acc_ref[h] = alpha * acc_ref[h] + _dot(vT_ref[h, :, cols], p.astype(BF16))

        m_ref[...] = jnp.full(m_ref.shape, NEG_BIG, F32)
        scores(0, first_ref)

        def two_tiles(pidx, carry):
            t = 2 * pidx
            scores(t + 1, other_ref)
            update(t, first_ref, False)
            scores(t + 2, first_ref)
            update(t + 1, other_ref, False)
            return carry

        lax.fori_loop(0, (tiles_per_step // 2) * step + j // 2, two_tiles, 0)

        if j % 2 == 0:
            update(i, first_ref, True)
        else:
            scores(i, other_ref)
            update(i - 1, first_ref, False)
            update(i, other_ref, True)

        for h in heads:
            on = acc_ref[h, 0:DIFF_V_DIM, :] * (1.0 / acc_ref[h, DIFF_V_DIM:DIFF_V_DIM + 1, :])
            o = on[:, :tq] - lam * on[:, tq:]
            ms = jnp.mean(o * o, axis=0, keepdims=True)
            o = o * lax.rsqrt(ms + EPS) * gsub_ref[...] * (1.0 - LAMBDA_INIT)
            o_ref[q_rows, hl(h)] = o.T.astype(BF16)

    first_ref, other_ref = s0_ref, s1_ref
    for j in range(tiles_per_step):
        q_tile(j, first_ref, other_ref)
        last_read = first_ref if j % 2 == 0 else other_ref
        first_ref, other_ref = (other_ref, first_ref) if last_read is first_ref else (first_ref, other_ref)


def _out_router_kernel(x_ref, a_ref, gc_ref, wo_ref, gffn_ref, wrh_ref, wrl_ref, br_ref, before_ref,
                       x1_out, hpa_out, hpb_out, idx_out, gate_out, pos_out, cnt_out, carry_ref):
    tm = x_ref.shape[0]

    @pl.when(pl.program_id(0) == 0)
    def _():
        carry_ref[...] = jnp.zeros_like(carry_ref)

    sub = before_ref.shape[0]
    wr_stack = jnp.concatenate([wrh_ref[...], wrl_ref[...]], axis=0)
    eio = lax.broadcasted_iota(I32, (N_EXPERTS, sub), 0)
    before_b = before_ref[...]
    carry = carry_ref[...]
    for r in range(tm // sub):
        rows = slice(r * sub, (r + 1) * sub)
        mix = jnp.concatenate([a_ref[rows, :], gc_ref[rows, :]], axis=1)
        x1 = x_ref[rows, :] + _dot(mix, wo_ref[...])
        x1_out[rows, :] = x1
        h2 = _rms(x1, gffn_ref[...])
        hb = h2.astype(BF16)
        hpa_out[rows, :] = _pack_bf16_pairs(h2[:, :D_MODEL // 2])
        hpb_out[rows, :] = _pack_bf16_pairs(h2[:, D_MODEL // 2:])

        h_lo = (h2 - hb.astype(F32)).astype(BF16)
        both = _dot_nt(wr_stack, hb)
        logits = (both[:N_EXPERTS] + both[N_EXPERTS:]) + _dot_nt(wrh_ref[...], h_lo) + br_ref[...]

        vals, idxs, sels = [], [], []
        cur = logits
        for _ in range(TOP_K):
            m = jnp.max(cur, axis=0, keepdims=True)
            ik = jnp.min(jnp.where(cur == m, eio, N_EXPERTS), axis=0, keepdims=True)
            sel = eio == ik
            cur = jnp.where(sel, -jnp.inf, cur)
            vals.append(m)
            idxs.append(ik)
            sels.append(sel)
        es = [jnp.exp(v - vals[0]) for v in vals]
        tot = es[0] + es[1] + es[2] + es[3]
        gates = jnp.concatenate([e / tot for e in es], axis=0)
        g_hi = gates.astype(BF16)
        g_lo = (gates - g_hi.astype(F32)).astype(BF16)
        gate_out[:, rows] = jnp.concatenate(
            [g_hi, g_lo, jnp.zeros((GATE_ROWS - 2 * TOP_K, sub), BF16)], axis=0)
        idx_out[:, rows] = jnp.concatenate(idxs, axis=0)

        cnt = jnp.zeros((N_EXPERTS, sub), F32)
        for sel in sels:
            cnt = cnt + jnp.where(sel, 1.0, 0.0)
        base = carry + _dot(cnt.astype(BF16), before_b)
        pos_out[:, rows] = jnp.concatenate(
            [jnp.sum(jnp.where(sel, base, 0.0), axis=0, keepdims=True) for sel in sels], axis=0).astype(I32)
        carry = carry + jnp.sum(cnt, axis=1, keepdims=True)
    carry_ref[...] = carry
    cnt_out[...] = carry


def _dest_kernel(cnt_ref, idx_ref, pos_ref, dest_out, be_out, valid_out, nbu_out):
    idx = idx_ref[...]
    dest = pos_ref[...]
    bidx = lax.broadcasted_iota(I32, be_out.shape, 1)
    be = jnp.zeros(be_out.shape, I32)
    valid = jnp.zeros(be_out.shape, I32)
    run = jnp.int32(0)
    for e in range(N_EXPERTS):
        dest = dest + jnp.where(idx == e, run, 0)
        first_block = run // ROW_BLOCK
        run = run + ((cnt_ref[e] + (ROW_BLOCK - 1)) // ROW_BLOCK) * ROW_BLOCK
        mine = jnp.logical_and(bidx >= first_block, bidx < run // ROW_BLOCK)
        valid = jnp.where(mine, jnp.clip(cnt_ref[e] - (bidx - first_block) * ROW_BLOCK, 0, ROW_BLOCK), valid)
        be = be + jnp.where(bidx >= run // ROW_BLOCK, 1, 0)
    dest_out[...] = dest
    be_out[...] = jnp.minimum(be, N_EXPERTS - 1)
    valid_out[...] = valid
    nbu_out[...] = jnp.zeros(nbu_out.shape, I32) + run // ROW_BLOCK


def _expert_ffn_kernel(be_ref, nbu_ref, nxt_ref, valid_ref, xa_ref, xb_ref, w1_hbm, b1_ref, w2_hbm, b2_ref,
                       y_ref, w1f_ref, w2f_ref, w1s_ref, w2s_ref, sem_ref):
    b = pl.program_id(0)

    def weight_copies(e):
        return (pltpu.make_async_copy(w1_hbm.at[e], w1f_ref, sem_ref.at[0]),
                pltpu.make_async_copy(w2_hbm.at[e], w2f_ref, sem_ref.at[1]))

    def mlp(rows):
        parts = _unpack_bf16_pairs(xa_ref[0:rows, :]) + _unpack_bf16_pairs(xb_ref[0:rows, :])
        xrow = jnp.concatenate([p.astype(BF16) for p in parts], axis=1)
        hm = _dot(xrow, w1s_ref[...]) + b1_ref[0]
        glu = jnp.minimum(hm[:, :D_FF], SWIGLU_LIMIT)
        lin = jnp.clip(hm[:, D_FF:], -SWIGLU_LIMIT, SWIGLU_LIMIT)
        act = glu * jax.nn.sigmoid(SWIGLU_ALPHA * glu) * (lin + 1.0)
        y = _dot(act.astype(BF16), w2s_ref[...]) + b2_ref[0]
        hw = D_MODEL // 4
        y_ref[0:rows, 0:hw] = _pack_bf16_pairs(y[:, :D_MODEL // 2])
        y_ref[0:rows, hw:] = _pack_bf16_pairs(y[:, D_MODEL // 2:])

    @pl.when(b < nbu_ref[0])
    def _():
        e = be_ref[b]

        @pl.when(b == 0)
        def _():
            for cp in weight_copies(e):
                cp.start()

        @pl.when(jnp.logical_or(b == 0, e != be_ref[jnp.maximum(b - 1, 0)]))
        def _():
            for cp in weight_copies(e):
                cp.wait()
            w1s_ref[...] = w1f_ref[...].astype(BF16)
            w2s_ref[...] = w2f_ref[...].astype(BF16)
            nxt = nxt_ref[e]

            @pl.when(nxt >= 0)
            def _():
                for cp in weight_copies(nxt):
                    cp.start()

        n_sub = (valid_ref[b] + (FFN_SUB_ROWS - 1)) // FFN_SUB_ROWS
        for k in range(1, ROW_BLOCK // FFN_SUB_ROWS + 1):
            pl.when(n_sub == k)(functools.partial(mlp, k * FFN_SUB_ROWS))


SC_WINDOW = 128
SC_GATHER_WINDOW = 64
SC_GATHER_WINDOW_COVERED = 32


def _sc_mesh():
    return plsc.VectorSubcoreMesh(core_axis_name="c", subcore_axis_name="s")


def _sc_gather_rows(table, idx, win):
    n = idx.shape[0]
    width = table.shape[1]

    @functools.partial(pl.kernel, out_type=jax.ShapeDtypeStruct((n, width), table.dtype), mesh=_sc_mesh(),
                       scratch_types=[])
    def gather_kernel(t_hbm, i_hbm, o_hbm):
        def body(i_vmem, o_vmem):
            pltpu.sync_copy(t_hbm.at[i_vmem.at[0]], o_vmem)

        pltpu.emit_pipeline(
            body,
            grid=(n // win,),
            in_specs=[pl.BlockSpec((1, win), lambda i: (i, 0))],
            out_specs=[pl.BlockSpec((win, width), lambda i: (i, 0))],
            core_axis_name=("c", "s"),
            dimension_semantics=(pltpu.PARALLEL,),
        )(i_hbm, o_hbm)

    return gather_kernel(table, idx.reshape(n // win, win))


def _sc_scatter_rows(rows, idx_rows, n_out):
    n, width = rows.shape

    @functools.partial(pl.kernel, out_type=jax.ShapeDtypeStruct((n_out, width), rows.dtype), mesh=_sc_mesh(),
                       scratch_types=[])
    def scatter_kernel(r_hbm, *refs):
        i_hbms, o_hbm = refs[:-1], refs[-1]

        def body(r_vmem, *i_vmems):
            for i_vmem in i_vmems:
                pltpu.sync_copy(r_vmem, o_hbm.at[i_vmem.at[0]])

        pltpu.emit_pipeline(
            body,
            grid=(n // SC_WINDOW,),
            in_specs=[pl.BlockSpec((SC_WINDOW, width), lambda i: (i, 0))]
            + [pl.BlockSpec((1, SC_WINDOW), lambda i: (0, i)) for _ in i_hbms],
            out_specs=[],
            core_axis_name=("c", "s"),
            dimension_semantics=(pltpu.PARALLEL,),
        )(r_hbm, *i_hbms)

    return scatter_kernel(rows, *idx_rows)


def _combine_kernel(x1_ref, yg_ref, gate_ref, spread_ref, o_ref):
    q = D_MODEL // 4
    x1 = x1_ref[...]
    acc = [x1[:, j * q:(j + 1) * q] for j in range(4)]
    gfull = lax.dot_general(gate_ref[...], spread_ref[...], (((0,), (0,)), ((), ())), preferred_element_type=F32)
    for k in range(TOP_K):
        parts = _unpack_bf16_pairs(yg_ref[k, :, 0:q]) + _unpack_bf16_pairs(yg_ref[k, :, q:])
        g = jnp.tile(gfull[:, k * LANES:(k + 1) * LANES], (1, q // LANES))
        acc = [a + g * p for a, p in zip(acc, parts)]
    for j in range(4):
        o_ref[:, j * q:(j + 1) * q] = acc[j]


def _combine_into_kernel(x1_ref, yg_ref, gate_ref, spread_ref, prev_ref, o_ref):
    del prev_ref
    _combine_kernel(x1_ref, yg_ref, gate_ref, spread_ref, o_ref)


def _block_diag_ones(width):
    r = jnp.arange(width) // HEAD_GROUP
    return (r[:, None] == r[None, :]).astype(BF16)


def _rope_tables(positions):
    half = ROT_DIM // 2
    inv_freq = ROPE_THETA ** (-jnp.arange(0, ROT_DIM, 2, dtype=F32) / ROT_DIM)
    ang = inv_freq[:, None] * positions.astype(F32).reshape(1, -1)
    cs = jnp.concatenate([jnp.cos(ang), jnp.sin(ang)], axis=0)
    cs_hi = cs.astype(BF16)
    cs_lo = (cs - cs_hi.astype(F32)).astype(BF16)
    lane = jnp.arange(LANES) % HEAD_GROUP
    j = jnp.arange(half)[:, None]
    lo_half = (lane[None, :] == j).astype(F32)
    hi_half = (lane[None, :] == j + half).astype(F32)
    spread = jnp.concatenate([
        jnp.concatenate([lo_half + hi_half, jnp.zeros((half, LANES), F32)], axis=1),
        jnp.concatenate([jnp.zeros((half, LANES), F32), hi_half - lo_half], axis=1)], axis=0)
    return jnp.concatenate([cs_hi, cs_lo], axis=0), jnp.concatenate([spread, spread], axis=0).astype(BF16)


def _full(shape):
    return pl.BlockSpec(shape, lambda *_: (0,) * len(shape))


def kernel(x, mem, positions, g_mix_norm, w_in, g_dq, g_dk, lambda_q1, lambda_k1, lambda_q2, lambda_k2, g_subln, g_sgu, w_spatial, b_spatial, g_mem_norm, w_mem_kv, g_cq, g_ck, w_out, g_ffn_norm, w_router, b_router, w_mlp1, b_mlp1, w_mlp2, b_mlp2):
    B, S, D = x.shape
    M = mem.shape[1]
    tm = TOKEN_TILE
    tr = ROUTER_TILE
    assert D == D_MODEL and S % tm == 0 and S % (ATTN_Q_TILES_PER_STEP * ATTN_TILE) == 0 and g_mix_norm.shape[0] == 1
    last_batches = B // LAST_PART_DIVISOR
    part_batches = (B - last_batches, last_batches) if last_batches > 0 and S % tr == 0 else (B,)

    xf = x.reshape(B * S, D)
    rope_cs, rope_spread = _rope_tables(positions)
    ones256 = _block_diag_ones(MEM_W)
    row = lambda v: v.reshape(1, -1).astype(F32)
    tile_row = lambda v, reps: jnp.tile(v.reshape(1, -1).astype(F32), (1, reps))
    w_in_b, w_out_b, w_kv_b = w_in[0].astype(BF16), w_out[0].astype(BF16), w_mem_kv[0].astype(BF16)
    w_sp_lanes = jnp.transpose(w_spatial[0], (1, 0, 2)).reshape(CHUNK, GMLP_GROUPS * CHUNK)
    b_sp_lanes = jnp.repeat(b_spatial[0].T, HEAD_GROUP, axis=1)
    wr = w_router[0].T.astype(F32)
    wr_hi = wr.astype(BF16)
    wr_lo = (wr - wr_hi.astype(F32)).astype(BF16)
    before = (jnp.arange(tr)[:, None] < jnp.arange(tr)[None, :]).astype(BF16)
    gate_row = jnp.arange(GATE_ROWS)[:, None]
    gate_spread = ((gate_row < 2 * TOP_K) & (gate_row % TOP_K == jnp.arange(TOP_K * LANES)[None, :] // LANES)).astype(BF16)
    b1r = b_mlp1[0].reshape(N_EXPERTS, 1, 2 * D_FF)
    b2r = b_mlp2[0].reshape(N_EXPERTS, 1, D)

    out = None
    b_off = 0
    for Bp in part_batches:
        N = Bp * S
        assert N % tr == 0 and N % tm == 0
        n_assign = N * TOP_K
        n_blocks = -(-n_assign // ROW_BLOCK) + N_EXPERTS
        n_rows = n_blocks * ROW_BLOCK
        nb_pad = -(-n_blocks // LANES) * LANES
        t_off = b_off * S // tm
        r_off = b_off * S // tr

        mb = MEM_BATCHES if Bp % MEM_BATCHES == 0 and b_off % MEM_BATCHES == 0 else 1
        kT, vm = pl.pallas_call(
            _mem_kv_kernel,
            grid=(Bp // mb,),
            in_specs=[pl.BlockSpec((mb, M, D), lambda b: (b + b_off // mb, 0, 0)), _full((1, D)), _full((D, 2 * MEM_W)),
                      _full((1, MEM_W)), _full((MEM_W, MEM_W))],
            out_specs=[pl.BlockSpec((mb, MEM_W, M), lambda b: (b, 0, 0)),
                       pl.BlockSpec((mb, M, MEM_W), lambda b: (b, 0, 0))],
            out_shape=[jax.ShapeDtypeStruct((Bp, MEM_W, M), BF16), jax.ShapeDtypeStruct((Bp, M, MEM_W), BF16)],
            compiler_params=_cparams(("parallel",)),
            name="mem_kv",
        )(mem, row(g_mem_norm[0]), w_kv_b, tile_row(g_ck[0], N_MEM_HEADS), ones256)

        tiles_per_batch = S // tm
        tok = lambda w: pl.BlockSpec((tm, w), lambda i: (i, 0))
        tok_in = lambda w: pl.BlockSpec((tm, w), lambda i: (i + t_off, 0))
        qn, kn, vv, gc = pl.pallas_call(
            _mixer_in_kernel,
            grid=(N // tm,),
            in_specs=[tok_in(D), pl.BlockSpec((ROPE_STACK_ROWS, tm), lambda i: (0, i + t_off)),
                      _full((ROPE_STACK_ROWS, 2 * LANES)),
                      _full((1, D)), _full((D, D_IN_PROJ)),
                      _full((1, DIFF_W)), _full((1, DIFF_W)),
                      _full((1, GMLP_W)), _full((CHUNK, GMLP_GROUPS * CHUNK)), _full((CHUNK, GMLP_W)),
                      _full((1, MEM_W)), _full((MEM_W, MEM_W)),
                      pl.BlockSpec((1, MEM_W, M), lambda i: (i // tiles_per_batch, 0, 0)),
                      pl.BlockSpec((1, M, MEM_W), lambda i: (i // tiles_per_batch, 0, 0))],
            out_specs=[tok(DIFF_W), tok(DIFF_W), tok(DIFF_W), tok(GMLP_W + MEM_W)],
            out_shape=[jax.ShapeDtypeStruct((N, DIFF_W), BF16)] * 3 + [jax.ShapeDtypeStruct((N, GMLP_W + MEM_W), BF16)],
            compiler_params=_cparams(("parallel",)),
            name="mixer_in",
        )(xf, rope_cs, rope_spread, row(g_mix_norm[0]), w_in_b,
          tile_row(g_dq[0], 2 * N_DIFF_HEADS), tile_row(g_dk[0], 2 * N_DIFF_HEADS),
          row(g_sgu[0]), w_sp_lanes, b_sp_lanes, tile_row(g_cq[0], N_MEM_HEADS), ones256, kT, vm)

        tq = ATTN_TILE
        q_rows = ATTN_Q_TILES_PER_STEP * tq
        nqp = S // q_rows
        head_q = pl.BlockSpec((q_rows, DIFF_W), lambda b, i: (b * nqp + i, 0))
        head_kv = pl.BlockSpec((S, DIFF_W), lambda b, i: (b, 0))
        lam_spec = pl.BlockSpec((1, DIFF_HEAD_DIM), lambda b, i: (0, 0))
        nh = N_DIFF_HEADS
        attn = pl.pallas_call(
            _diff_attn_kernel,
            grid=(Bp, nqp),
            in_specs=[head_q, head_kv, head_kv, lam_spec, lam_spec, lam_spec, lam_spec,
                      pl.BlockSpec((DIFF_V_DIM, 1), lambda b, i: (0, 0))],
            out_specs=head_q,
            out_shape=jax.ShapeDtypeStruct((N, DIFF_W), BF16),
            scratch_shapes=[pltpu.VMEM((nh, tq, 2 * tq), F32), pltpu.VMEM((nh, tq, 2 * tq), F32),
                            pltpu.VMEM((nh, 1, 2 * tq), F32), pltpu.VMEM((nh, ATTN_EXT_ROWS, 2 * tq), F32),
                            pltpu.VMEM((nh, ATTN_EXT_ROWS, S), BF16)],
            compiler_params=_cparams(("arbitrary", "arbitrary")),
            name="diff_attn",
        )(qn, kn, vv, row(lambda_q1[0]), row(lambda_k1[0]), row(lambda_q2[0]), row(lambda_k2[0]),
          g_subln[0].reshape(DIFF_V_DIM, 1).astype(F32))

        rtok = lambda w: pl.BlockSpec((tr, w), lambda i: (i, 0))
        rtok_in = lambda w: pl.BlockSpec((tr, w), lambda i: (i + r_off, 0))
        tokT = lambda: pl.BlockSpec((TOP_K, tr), lambda i: (0, i))
        hw = D // 4
        x1, hpa, hpb, idxT, gateT, posT, counts = pl.pallas_call(
            _out_router_kernel,
            grid=(N // tr,),
            in_specs=[rtok_in(D), rtok(DIFF_W), rtok(GMLP_W + MEM_W), _full((D, D)), _full((1, D)),
                      _full((N_EXPERTS, D)), _full((N_EXPERTS, D)), _full((N_EXPERTS, 1)), _full((tr, tr))],
            out_specs=[rtok(D), rtok(hw), rtok(hw), tokT(), pl.BlockSpec((GATE_ROWS, tr), lambda i: (0, i)), tokT(),
                       _full((N_EXPERTS, 1))],
            out_shape=[jax.ShapeDtypeStruct((N, D), F32), jax.ShapeDtypeStruct((N, hw), U32),
                       jax.ShapeDtypeStruct((N, hw), U32), jax.ShapeDtypeStruct((TOP_K, N), I32),
                       jax.ShapeDtypeStruct((GATE_ROWS, N), BF16),
                       jax.ShapeDtypeStruct((TOP_K, N), I32), jax.ShapeDtypeStruct((N_EXPERTS, 1), F32)],
            scratch_shapes=[pltpu.VMEM((N_EXPERTS, 1), F32)],
            compiler_params=_cparams(("arbitrary",)),
            name="out_router",
        )(xf, attn, gc, w_out_b, row(g_ffn_norm[0]), wr_hi, wr_lo, b_router[0].reshape(N_EXPERTS, 1).astype(F32), before)

        destT, block_expert, block_valid, nb_used = pl.pallas_call(
            _dest_kernel,
            grid_spec=pltpu.PrefetchScalarGridSpec(
                num_scalar_prefetch=1,
                grid=(1,),
                in_specs=[pl.BlockSpec((TOP_K, N), lambda i, c: (0, 0)), pl.BlockSpec((TOP_K, N), lambda i, c: (0, 0))],
                out_specs=[pl.BlockSpec((TOP_K, N), lambda i, c: (0, 0)), pl.BlockSpec((1, nb_pad), lambda i, c: (0, 0)),
                           pl.BlockSpec((1, nb_pad), lambda i, c: (0, 0)), pl.BlockSpec((1, LANES), lambda i, c: (0, 0))],
            ),
            out_shape=[jax.ShapeDtypeStruct((TOP_K, N), I32), jax.ShapeDtypeStruct((1, nb_pad), I32),
                       jax.ShapeDtypeStruct((1, nb_pad), I32), jax.ShapeDtypeStruct((1, LANES), I32)],
            compiler_params=_cparams(("arbitrary",)),
            name="dest",
        )(counts.reshape(N_EXPERTS).astype(I32), idxT, posT)

        dest_rows = [destT[k].reshape(1, N) for k in range(TOP_K)]
        xa_buf = _sc_scatter_rows(hpa, dest_rows, n_rows)
        xb_buf = _sc_scatter_rows(hpb, dest_rows, n_rows)

        cnt_i = counts.reshape(N_EXPERTS).astype(I32)
        owner = jnp.where(cnt_i > 0, jnp.arange(N_EXPERTS, dtype=I32), N_EXPERTS)
        later = jnp.concatenate([lax.cummin(owner[::-1])[::-1][1:], jnp.full((1,), N_EXPERTS, I32)])
        next_expert = jnp.where(later < N_EXPERTS, later, -1)
        last = lambda b, be, nbu, nxt, valid: jnp.minimum(b, nbu[0] - 1)
        row_blk = lambda: pl.BlockSpec((ROW_BLOCK, hw), lambda b, be, nbu, nxt, valid: (last(b, be, nbu, nxt, valid), 0))
        y_blk = pl.BlockSpec((ROW_BLOCK, 2 * hw), lambda b, be, nbu, nxt, valid: (last(b, be, nbu, nxt, valid), 0))
        y_buf = pl.pallas_call(
            _expert_ffn_kernel,
            grid_spec=pltpu.PrefetchScalarGridSpec(
                num_scalar_prefetch=4,
                grid=(n_blocks,),
                in_specs=[row_blk(), row_blk(),
                          pl.BlockSpec(memory_space=pl.ANY),
                          pl.BlockSpec((1, 1, 2 * D_FF), lambda b, be, nbu, nxt, valid: (be[b], 0, 0)),
                          pl.BlockSpec(memory_space=pl.ANY),
                          pl.BlockSpec((1, 1, D), lambda b, be, nbu, nxt, valid: (be[b], 0, 0))],
                out_specs=y_blk,
                scratch_shapes=[pltpu.VMEM((D, 2 * D_FF), F32), pltpu.VMEM((D_FF, D), F32),
                                pltpu.VMEM((D, 2 * D_FF), BF16), pltpu.VMEM((D_FF, D), BF16),
                                pltpu.SemaphoreType.DMA((2,))],
            ),
            out_shape=jax.ShapeDtypeStruct((n_rows, 2 * hw), U32),
            compiler_params=_cparams(("arbitrary",)),
            name="expert_ffn",
        )(block_expert[0, :n_blocks], nb_used[0, :1], next_expert, block_valid[0, :n_blocks], xa_buf, xb_buf,
          w_mlp1[0], b1r, w_mlp2[0], b2r)

        last_part = b_off + Bp == B
        win = SC_GATHER_WINDOW if last_part else SC_GATHER_WINDOW_COVERED
        yg = _sc_gather_rows(y_buf, destT.reshape(n_assign), win).reshape(TOP_K, N, 2 * hw)
        tc = COMBINE_TILE
        prev = () if out is None else (out,)
        out = pl.pallas_call(
            _combine_kernel if out is None else _combine_into_kernel,
            grid=(N // tc,),
            in_specs=[pl.BlockSpec((tc, D), lambda i: (i, 0)),
                      pl.BlockSpec((TOP_K, tc, 2 * hw), lambda i: (0, i, 0)),
                      pl.BlockSpec((GATE_ROWS, tc), lambda i: (0, i)),
                      _full((GATE_ROWS, TOP_K * LANES))] + [pl.BlockSpec(memory_space=pl.ANY)] * len(prev),
            out_specs=pl.BlockSpec((tc, D), lambda i: (i + b_off * S // tc, 0)),
            out_shape=jax.ShapeDtypeStruct((B * S, D), F32),
            input_output_aliases={4: 0} if prev else {},
            compiler_params=_cparams(("parallel",)),
            name="combine",
        )(x1, yg, gateT, gate_spread, *prev)
        b_off += Bp
    return out.reshape(B, S, D)
```

```python
import functools

import jax
import jax.numpy as jnp
from jax import lax
from jax.experimental import pallas as pl
from jax.experimental.pallas import tpu as pltpu
from jax.experimental.pallas import tpu_sc as plsc

F32 = jnp.float32
BF16 = jnp.bfloat16
I32 = jnp.int32
U32 = jnp.uint32

D_MODEL = 1024
N_DIFF_HEADS = 4
DIFF_HEAD_DIM = 64
DIFF_V_DIM = 128
DIFF_W = 512
GMLP_W = 256
GMLP_GROUPS = 4
CHUNK = 128
MEM_W = 256
N_MEM_HEADS = 4
HEAD_GROUP = 64
D_IN_PROJ = 2304
ROPE_THETA = 500000.0
ROT_DIM = 16
ROPE_STACK_ROWS = 2 * ROT_DIM
N_EXPERTS = 32
TOP_K = 4
D_FF = 1024
SWIGLU_LIMIT = 7.0
SWIGLU_ALPHA = 1.702
EPS = 1e-6
LAMBDA_INIT = 0.8 - 0.6

LANES = 128
ROW_BLOCK = 1024
FFN_SUB_ROWS = 256
TOKEN_TILE = 1024
COMBINE_TILE = 1024
MEM_BATCHES = 4
LAYER_GROUPS = 3
ROUTER_TILE = 1024
GATE_ROWS = 16
ATTN_TILE = 256
ATTN_Q_TILES_PER_STEP = 8
ATTN_EXT_ROWS = 128 + 16
VMEM_LIMIT = 56 * 1024 * 1024
NEG_BIG = -1e30
LOG2_E = 1.4426950408889634


def _cparams(sem):
    return pltpu.CompilerParams(dimension_semantics=sem, vmem_limit_bytes=VMEM_LIMIT)


def _dot(a, b):
    return jnp.dot(a, b, preferred_element_type=F32)


def _dot_nt(a, b):
    return lax.dot_general(a, b, (((1,), (1,)), ((), ())), preferred_element_type=F32)


def _rms(x, gain):
    ms = jnp.mean(x * x, axis=-1, keepdims=True)
    return x * lax.rsqrt(ms + EPS) * gain


def _group_rms(t, ones_bd, gain):
    w = ones_bd.shape[0]
    chunks = []
    for j in range(t.shape[1] // w):
        c = t[:, j * w:(j + 1) * w]
        ss = _dot((c * c).astype(BF16), ones_bd)
        chunks.append(c * lax.rsqrt(ss * (1.0 / HEAD_GROUP) + EPS))
    return (chunks[0] if len(chunks) == 1 else jnp.concatenate(chunks, axis=1)) * gain


def _pack_bf16_pairs(v):
    w = v.shape[1] // 2
    bits = lax.bitcast_convert_type(v.astype(BF16).astype(F32), U32)
    return (bits[:, :w] & jnp.uint32(0xFFFF0000)) | (bits[:, w:] >> jnp.uint32(16))


def _unpack_bf16_pairs(words):
    hi = lax.bitcast_convert_type(words & jnp.uint32(0xFFFF0000), F32)
    lo = lax.bitcast_convert_type(words << jnp.uint32(16), F32)
    return hi, lo


def _mem_kv_kernel(mem_ref, gmem_ref, wkv_ref, gck_ref, ones_ref, kT_ref, v_ref):
    for j in range(mem_ref.shape[0]):
        m = _rms(mem_ref[j], gmem_ref[...]).astype(BF16)
        kv = _dot(m, wkv_ref[...])
        k = _group_rms(kv[:, :MEM_W], ones_ref[...], gck_ref[...])
        kT_ref[j] = k.T.astype(BF16)
        v_ref[j] = kv[:, MEM_W:].astype(BF16)


def _gelu_tanh(x):
    return 0.5 * x * (1.0 + jnp.tanh(0.7978845608028654 * (x + 0.044715 * (x * x * x))))


def _mixer_in_kernel(x_ref, cs_ref, spread_ref, gmix_ref, win_ref, gq_ref, gk_ref,
                     gsgu_ref, wsp_ref, bsp_ref, gcq_ref, ones256_ref, kT_ref, vm_ref,
                     q_out, k_out, v_out, gc_out):
    tm = x_ref.shape[0]
    hb = _rms(x_ref[...], gmix_ref[...]).astype(BF16)

    def proj(lo, hi):
        return _dot(hb, win_ref[:, lo:hi])

    lane = lax.broadcasted_iota(I32, (tm, LANES), 1)
    first_half = (lane % HEAD_GROUP) < (ROT_DIM // 2)
    tab = lax.dot_general(cs_ref[...], spread_ref[...], (((0,), (0,)), ((), ())), preferred_element_type=F32)
    cosb = tab[:, :LANES] + jnp.where((lane % HEAD_GROUP) >= ROT_DIM, 1.0, 0.0)
    sinb = tab[:, LANES:]

    def norm_rope(t, gain, out_ref):
        tn = _group_rms(t, ones256_ref[...], gain)
        for j in range(DIFF_W // LANES):
            c = tn[:, j * LANES:(j + 1) * LANES]
            partner = jnp.where(first_half, pltpu.roll(c, LANES - ROT_DIM // 2, 1), pltpu.roll(c, ROT_DIM // 2, 1))
            out_ref[:, j * LANES:(j + 1) * LANES] = (c * cosb + partner * sinb).astype(BF16)

    norm_rope(proj(0, DIFF_W), gq_ref[...] * (DIFF_HEAD_DIM ** -0.5 * LOG2_E), q_out)
    norm_rope(proj(DIFF_W, 2 * DIFF_W), gk_ref[...], k_out)
    v_out[...] = proj(2 * DIFF_W, 3 * DIFF_W).astype(BF16)

    z = _gelu_tanh(proj(3 * DIFF_W, 3 * DIFF_W + 2 * GMLP_W))
    u = z[:, :GMLP_W]
    vg = z[:, GMLP_W:]
    vc = vg - jnp.mean(vg, axis=-1, keepdims=True)
    vgn = (vc * lax.rsqrt(jnp.mean(vc * vc, axis=-1, keepdims=True) + EPS) * gsgu_ref[...]).astype(BF16)
    wrow = lax.broadcasted_iota(I32, (CHUNK, GMLP_GROUPS * CHUNK), 0)
    wcol = lax.broadcasted_iota(I32, (CHUNK, GMLP_GROUPS * CHUNK), 1) % CHUNK
    w_causal = jnp.where(wcol <= wrow, wsp_ref[...], 0.0).astype(BF16)
    grp = lax.broadcasted_iota(I32, (CHUNK, GMLP_W), 1) // HEAD_GROUP
    zero_b = jnp.zeros((CHUNK, GMLP_W), BF16)
    for r in range(tm // CHUNK):
        vchunk = vgn[r * CHUNK:(r + 1) * CHUNK, :]
        v_bd = jnp.concatenate([jnp.where(grp == g, vchunk, zero_b) for g in range(GMLP_GROUPS)], axis=0)
        mixed = _dot(w_causal, v_bd) + bsp_ref[...]
        gc_out[r * CHUNK:(r + 1) * CHUNK, 0:GMLP_W] = (u[r * CHUNK:(r + 1) * CHUNK, :] * mixed).astype(BF16)

    pc = proj(3 * DIFF_W + 2 * GMLP_W, D_IN_PROJ)
    qc = _group_rms(pc, ones256_ref[...], gcq_ref[...] * (HEAD_GROUP ** -0.5)).astype(BF16)
    hgrp = lax.broadcasted_iota(I32, (tm, MEM_W), 1) // HEAD_GROUP
    zero_q = jnp.zeros((tm, MEM_W), BF16)
    q_st = jnp.concatenate([jnp.where(hgrp == h, qc, zero_q) for h in range(N_MEM_HEADS)], axis=0)
    s = _dot(q_st, kT_ref[0])
    p = jnp.exp(s - jnp.max(s, axis=-1, keepdims=True))
    o = _dot(p.astype(BF16), vm_ref[0]) / jnp.sum(p, axis=-1, keepdims=True)
    c = jnp.zeros((tm, MEM_W), F32)
    for h in range(N_MEM_HEADS):
        c = c + jnp.where(hgrp == h, o[h * tm:(h + 1) * tm, :], 0.0)
    gc_out[:, GMLP_W:GMLP_W + MEM_W] = c.astype(BF16)


def _diff_attn_kernel(q_ref, k_ref, v_ref, lq1_ref, lk1_ref, lq2_ref, lk2_ref, gsub_ref, o_ref,
                      s0_ref, s1_ref, m_ref, acc_ref, vT_ref):
    tq = ATTN_TILE
    tiles_per_step = q_ref.shape[0] // tq
    assert tiles_per_step % 2 == 0
    seq = k_ref.shape[0]
    step = pl.program_id(1)
    feat = lax.broadcasted_iota(I32, (DIFF_V_DIM, tq), 0)
    heads = range(N_DIFF_HEADS)
    hl = lambda h: slice(h * DIFF_V_DIM, (h + 1) * DIFF_V_DIM)
    ext_rows = vT_ref.shape[1]

    @pl.when(step == 0)
    def _():
        ones_row = jnp.where(lax.broadcasted_iota(I32, (ext_rows - DIFF_V_DIM, seq), 0) == 0, 1.0, 0.0).astype(BF16)
        for h in heads:
            for c in range(seq // tq):
                vT_ref[h, 0:DIFF_V_DIM, c * tq:(c + 1) * tq] = v_ref[c * tq:(c + 1) * tq, hl(h)].T
            vT_ref[h, DIFF_V_DIM:ext_rows, :] = ones_row

    @pl.when(jnp.logical_and(pl.program_id(0) == 0, step == 0))
    def _():
        acc_ref[...] = jnp.zeros(acc_ref.shape, F32)

    lam = (jnp.exp(jnp.sum(lq1_ref[...] * lk1_ref[...], axis=-1, keepdims=True))
           - jnp.exp(jnp.sum(lq2_ref[...] * lk2_ref[...], axis=-1, keepdims=True)) + LAMBDA_INIT)

    def q_tile(j, first_ref, other_ref):
        i = tiles_per_step * step + j
        q_rows = slice(j * tq, (j + 1) * tq)

        def stacked_qT(h):
            qT = q_ref[q_rows, hl(h)].T
            zero = jnp.zeros_like(qT)
            return jnp.concatenate(
                [jnp.where(feat < DIFF_HEAD_DIM, qT, zero), jnp.where(feat >= DIFF_HEAD_DIM, qT, zero)], axis=1)

        qsT = [stacked_qT(h) for h in heads]

        def scores(t, s_ref):
            rows = pl.ds(pl.multiple_of(t * tq, tq), tq)
            for h in heads:
                s_ref[h] = _dot(k_ref[rows, hl(h)], qsT[h])

        def update(t, s_ref, causal):
            cols = pl.ds(pl.multiple_of(t * tq, tq), tq)
            for h in heads:
                s = s_ref[h]
                if causal:
                    key = lax.broadcasted_iota(I32, (tq, 2 * tq), 0)
                    qry = lax.broadcasted_iota(I32, (tq, 2 * tq), 1) % tq
                    s = jnp.where(key <= qry, s, NEG_BIG)
                m = m_ref[h]
                m_new = jnp.maximum(m, jnp.max(s, axis=0, keepdims=True))
                alpha = jnp.exp2(m - m_new)
                p = jnp.exp2(s - m_new)
                m_ref[h] = m_new
                acc_ref[h] = alpha * acc_ref[h] + _dot(vT_ref[h, :, cols], p.astype(BF16))

        m_ref[...] = jnp.full(m_ref.shape, NEG_BIG, F32)
        scores(0, first_ref)

        def two_tiles(pidx, carry):
            t = 2 * pidx
            scores(t + 1, other_ref)
            update(t, first_ref, False)
            scores(t + 2, first_ref)
            update(t + 1, other_ref, False)
            return carry

        lax.fori_loop(0, (tiles_per_step // 2) * step + j // 2, two_tiles, 0)

        if j % 2 == 0:
            update(i, first_ref, True)
        else:
            scores(i, other_ref)
            update(i - 1, first_ref, False)
            update(i, other_ref, True)

        for h in heads:
            on = acc_ref[h, 0:DIFF_V_DIM, :] * (1.0 / acc_ref[h, DIFF_V_DIM:DIFF_V_DIM + 1, :])
            o = on[:, :tq] - lam * on[:, tq:]
            ms = jnp.mean(o * o, axis=0, keepdims=True)
            o = o * lax.rsqrt(ms + EPS) * gsub_ref[...] * (1.0 - LAMBDA_INIT)
            o_ref[q_rows, hl(h)] = o.T.astype(BF16)

    first_ref, other_ref = s0_ref, s1_ref
    for j in range(tiles_per_step):
        q_tile(j, first_ref, other_ref)
        last_read = first_ref if j % 2 == 0 else other_ref
        first_ref, other_ref = (other_ref, first_ref) if last_read is first_ref else (first_ref, other_ref)


def _out_router_kernel(x_ref, a_ref, gc_ref, wo_ref, gffn_ref, wrh_ref, wrl_ref, br_ref, before_ref,
                       x1_out, hpa_out, hpb_out, idx_out, gate_out, pos_out, cnt_out, carry_ref):
    tm = x_ref.shape[0]

    @pl.when(pl.program_id(0) == 0)
    def _():
        carry_ref[...] = jnp.zeros_like(carry_ref)

    sub = before_ref.shape[0]
    wr_stack = jnp.concatenate([wrh_ref[...], wrl_ref[...]], axis=0)
    eio = lax.broadcasted_iota(I32, (N_EXPERTS, sub), 0)
    before_b = before_ref[...]
    carry = carry_ref[...]
    for r in range(tm // sub):
        rows = slice(r * sub, (r + 1) * sub)
        mix = jnp.concatenate([a_ref[rows, :], gc_ref[rows, :]], axis=1)
        x1 = x_ref[rows, :] + _dot(mix, wo_ref[...])
        x1_out[rows, :] = x1
        h2 = _rms(x1, gffn_ref[...])
        hb = h2.astype(BF16)
        hpa_out[rows, :] = _pack_bf16_pairs(h2[:, :D_MODEL // 2])
        hpb_out[rows, :] = _pack_bf16_pairs(h2[:, D_MODEL // 2:])

        h_lo = (h2 - hb.astype(F32)).astype(BF16)
        both = _dot_nt(wr_stack, hb)
        logits = (both[:N_EXPERTS] + both[N_EXPERTS:]) + _dot_nt(wrh_ref[...], h_lo) + br_ref[...]

        vals, idxs, sels = [], [], []
        cur = logits
        for _ in range(TOP_K):
            m = jnp.max(cur, axis=0, keepdims=True)
            ik = jnp.min(jnp.where(cur == m, eio, N_EXPERTS), axis=0, keepdims=True)
            sel = eio == ik
            cur = jnp.where(sel, -jnp.inf, cur)
            vals.append(m)
            idxs.append(ik)
            sels.append(sel)
        es = [jnp.exp(v - vals[0]) for v in vals]
        tot = es[0] + es[1] + es[2] + es[3]
        gates = jnp.concatenate([e / tot for e in es], axis=0)
        g_hi = gates.astype(BF16)
        g_lo = (gates - g_hi.astype(F32)).astype(BF16)
        gate_out[:, rows] = jnp.concatenate(
            [g_hi, g_lo, jnp.zeros((GATE_ROWS - 2 * TOP_K, sub), BF16)], axis=0)
        idx_out[:, rows] = jnp.concatenate(idxs, axis=0)

        cnt = jnp.zeros((N_EXPERTS, sub), F32)
        for sel in sels:
            cnt = cnt + jnp.where(sel, 1.0, 0.0)
        base = carry + _dot(cnt.astype(BF16), before_b)
        pos_out[:, rows] = jnp.concatenate(
            [jnp.sum(jnp.where(sel, base, 0.0), axis=0, keepdims=True) for sel in sels], axis=0).astype(I32)
        carry = carry + jnp.sum(cnt, axis=1, keepdims=True)
    carry_ref[...] = carry
    cnt_out[...] = carry


def _dest_kernel(cnt_ref, idx_ref, pos_ref, dest_out, be_out, valid_out, nbu_out):
    idx = idx_ref[...]
    dest = pos_ref[...]
    bidx = lax.broadcasted_iota(I32, be_out.shape, 1)
    be = jnp.zeros(be_out.shape, I32)
    valid = jnp.zeros(be_out.shape, I32)
    run = jnp.int32(0)
    for e in range(N_EXPERTS):
        dest = dest + jnp.where(idx == e, run, 0)
        first_block = run // ROW_BLOCK
        run = run + ((cnt_ref[e] + (ROW_BLOCK - 1)) // ROW_BLOCK) * ROW_BLOCK
        mine = jnp.logical_and(bidx >= first_block, bidx < run // ROW_BLOCK)
        valid = jnp.where(mine, jnp.clip(cnt_ref[e] - (bidx - first_block) * ROW_BLOCK, 0, ROW_BLOCK), valid)
        be = be + jnp.where(bidx >= run // ROW_BLOCK, 1, 0)
    dest_out[...] = dest
    be_out[...] = jnp.minimum(be, N_EXPERTS - 1)
    valid_out[...] = valid
    nbu_out[...] = jnp.zeros(nbu_out.shape, I32) + run // ROW_BLOCK


def _expert_ffn_kernel(be_ref, nbu_ref, nxt_ref, valid_ref, xa_ref, xb_ref, w1_hbm, b1_ref, w2_hbm, b2_ref,
                       y_ref, w1f_ref, w2f_ref, w1s_ref, w2s_ref, sem_ref):
    b = pl.program_id(0)

    def weight_copies(e):
        return (pltpu.make_async_copy(w1_hbm.at[e], w1f_ref, sem_ref.at[0]),
                pltpu.make_async_copy(w2_hbm.at[e], w2f_ref, sem_ref.at[1]))

    def mlp(rows):
        parts = _unpack_bf16_pairs(xa_ref[0:rows, :]) + _unpack_bf16_pairs(xb_ref[0:rows, :])
        xrow = jnp.concatenate([p.astype(BF16) for p in parts], axis=1)
        hm = _dot(xrow, w1s_ref[...]) + b1_ref[0]
        glu = jnp.minimum(hm[:, :D_FF], SWIGLU_LIMIT)
        lin = jnp.clip(hm[:, D_FF:], -SWIGLU_LIMIT, SWIGLU_LIMIT)
        act = glu * jax.nn.sigmoid(SWIGLU_ALPHA * glu) * (lin + 1.0)
        y = _dot(act.astype(BF16), w2s_ref[...]) + b2_ref[0]
        hw = D_MODEL // 4
        y_ref[0:rows, 0:hw] = _pack_bf16_pairs(y[:, :D_MODEL // 2])
        y_ref[0:rows, hw:] = _pack_bf16_pairs(y[:, D_MODEL // 2:])

    @pl.when(b < nbu_ref[0])
    def _():
        e = be_ref[b]

        @pl.when(b == 0)
        def _():
            for cp in weight_copies(e):
                cp.start()

        @pl.when(jnp.logical_or(b == 0, e != be_ref[jnp.maximum(b - 1, 0)]))
        def _():
            for cp in weight_copies(e):
                cp.wait()
            w1s_ref[...] = w1f_ref[...].astype(BF16)
            w2s_ref[...] = w2f_ref[...].astype(BF16)
            nxt = nxt_ref[e]

            @pl.when(nxt >= 0)
            def _():
                for cp in weight_copies(nxt):
                    cp.start()

        n_sub = (valid_ref[b] + (FFN_SUB_ROWS - 1)) // FFN_SUB_ROWS
        for k in range(1, ROW_BLOCK // FFN_SUB_ROWS + 1):
            pl.when(n_sub == k)(functools.partial(mlp, k * FFN_SUB_ROWS))


SC_WINDOW = 128
SC_GATHER_WINDOW = 64
SC_GATHER_WINDOW_COVERED = 32


def _sc_mesh():
    return plsc.VectorSubcoreMesh(core_axis_name="c", subcore_axis_name="s")


def _sc_gather_rows(table, idx, win):
    n = idx.shape[0]
    width = table.shape[1]

    @functools.partial(pl.kernel, out_type=jax.ShapeDtypeStruct((n, width), table.dtype), mesh=_sc_mesh(),
                       scratch_types=[])
    def gather_kernel(t_hbm, i_hbm, o_hbm):
        def body(i_vmem, o_vmem):
            pltpu.sync_copy(t_hbm.at[i_vmem.at[0]], o_vmem)

        pltpu.emit_pipeline(
            body,
            grid=(n // win,),
            in_specs=[pl.BlockSpec((1, win), lambda i: (i, 0))],
            out_specs=[pl.BlockSpec((win, width), lambda i: (i, 0))],
            core_axis_name=("c", "s"),
            dimension_semantics=(pltpu.PARALLEL,),
        )(i_hbm, o_hbm)

    return gather_kernel(table, idx.reshape(n // win, win))


def _sc_scatter_rows(rows, idx_rows, n_out):
    n, width = rows.shape

    @functools.partial(pl.kernel, out_type=jax.ShapeDtypeStruct((n_out, width), rows.dtype), mesh=_sc_mesh(),
                       scratch_types=[])
    def scatter_kernel(r_hbm, *refs):
        i_hbms, o_hbm = refs[:-1], refs[-1]

        def body(r_vmem, *i_vmems):
            for i_vmem in i_vmems:
                pltpu.sync_copy(r_vmem, o_hbm.at[i_vmem.at[0]])

        pltpu.emit_pipeline(
            body,
            grid=(n // SC_WINDOW,),
            in_specs=[pl.BlockSpec((SC_WINDOW, width), lambda i: (i, 0))]
            + [pl.BlockSpec((1, SC_WINDOW), lambda i: (0, i)) for _ in i_hbms],
            out_specs=[],
            core_axis_name=("c", "s"),
            dimension_semantics=(pltpu.PARALLEL,),
        )(r_hbm, *i_hbms)

    return scatter_kernel(rows, *idx_rows)


def _combine_kernel(x1_ref, yg_ref, gate_ref, spread_ref, o_ref):
    q = D_MODEL // 4
    x1 = x1_ref[...]
    acc = [x1[:, j * q:(j + 1) * q] for j in range(4)]
    gfull = lax.dot_general(gate_ref[...], spread_ref[...], (((0,), (0,)), ((), ())), preferred_element_type=F32)
    for k in range(TOP_K):
        parts = _unpack_bf16_pairs(yg_ref[k, :, 0:q]) + _unpack_bf16_pairs(yg_ref[k, :, q:])
        g = jnp.tile(gfull[:, k * LANES:(k + 1) * LANES], (1, q // LANES))
        acc = [a + g * p for a, p in zip(acc, parts)]
    for j in range(4):
        o_ref[:, j * q:(j + 1) * q] = acc[j]


def _combine_into_kernel(x1_ref, yg_ref, gate_ref, spread_ref, prev_ref, o_ref):
    del prev_ref
    _combine_kernel(x1_ref, yg_ref, gate_ref, spread_ref, o_ref)


def _block_diag_ones(width):
    r = jnp.arange(width) // HEAD_GROUP
    return (r[:, None] == r[None, :]).astype(BF16)


def _rope_tables(positions):
    half = ROT_DIM // 2
    inv_freq = ROPE_THETA ** (-jnp.arange(0, ROT_DIM, 2, dtype=F32) / ROT_DIM)
    ang = inv_freq[:, None] * positions.astype(F32).reshape(1, -1)
    cs = jnp.concatenate([jnp.cos(ang), jnp.sin(ang)], axis=0)
    cs_hi = cs.astype(BF16)
    cs_lo = (cs - cs_hi.astype(F32)).astype(BF16)
    lane = jnp.arange(LANES) % HEAD_GROUP
    j = jnp.arange(half)[:, None]
    lo_half = (lane[None, :] == j).astype(F32)
    hi_half = (lane[None, :] == j + half).astype(F32)
    spread = jnp.concatenate([
        jnp.concatenate([lo_half + hi_half, jnp.zeros((half, LANES), F32)], axis=1),
        jnp.concatenate([jnp.zeros((half, LANES), F32), hi_half - lo_half], axis=1)], axis=0)
    return jnp.concatenate([cs_hi, cs_lo], axis=0), jnp.concatenate([spread, spread], axis=0).astype(BF16)


def _full(shape):
    return pl.BlockSpec(shape, lambda *_: (0,) * len(shape))


def kernel(x, mem, positions, g_mix_norm, w_in, g_dq, g_dk, lambda_q1, lambda_k1, lambda_q2, lambda_k2, g_subln, g_sgu, w_spatial, b_spatial, g_mem_norm, w_mem_kv, g_cq, g_ck, w_out, g_ffn_norm, w_router, b_router, w_mlp1, b_mlp1, w_mlp2, b_mlp2):
    B, S, D = x.shape
    M = mem.shape[1]
    tm = TOKEN_TILE
    tr = ROUTER_TILE
    assert D == D_MODEL and S % tm == 0 and S % (ATTN_Q_TILES_PER_STEP * ATTN_TILE) == 0 and g_mix_norm.shape[0] == 1
    n_groups = LAYER_GROUPS if B >= LAYER_GROUPS and S % tr == 0 else 1
    part_batches = tuple(B // n_groups + (1 if g < B % n_groups else 0) for g in range(n_groups))

    xf = x.reshape(B * S, D)
    rope_cs, rope_spread = _rope_tables(positions)
    ones256 = _block_diag_ones(MEM_W)
    row = lambda v: v.reshape(1, -1).astype(F32)
    tile_row = lambda v, reps: jnp.tile(v.reshape(1, -1).astype(F32), (1, reps))
    w_in_b, w_out_b, w_kv_b = w_in[0].astype(BF16), w_out[0].astype(BF16), w_mem_kv[0].astype(BF16)
    w_sp_lanes = jnp.transpose(w_spatial[0], (1, 0, 2)).reshape(CHUNK, GMLP_GROUPS * CHUNK)
    b_sp_lanes = jnp.repeat(b_spatial[0].T, HEAD_GROUP, axis=1)
    wr = w_router[0].T.astype(F32)
    wr_hi = wr.astype(BF16)
    wr_lo = (wr - wr_hi.astype(F32)).astype(BF16)
    before = (jnp.arange(tr)[:, None] < jnp.arange(tr)[None, :]).astype(BF16)
    gate_row = jnp.arange(GATE_ROWS)[:, None]
    gate_spread = ((gate_row < 2 * TOP_K) & (gate_row % TOP_K == jnp.arange(TOP_K * LANES)[None, :] // LANES)).astype(BF16)
    b1r = b_mlp1[0].reshape(N_EXPERTS, 1, 2 * D_FF)
    b2r = b_mlp2[0].reshape(N_EXPERTS, 1, D)

    out = None
    b_off = 0
    for Bp in part_batches:
        N = Bp * S
        assert N % tr == 0 and N % tm == 0
        n_assign = N * TOP_K
        n_blocks = -(-n_assign // ROW_BLOCK) + N_EXPERTS
        n_rows = n_blocks * ROW_BLOCK
        nb_pad = -(-n_blocks // LANES) * LANES
        t_off = b_off * S // tm
        r_off = b_off * S // tr

        mb = MEM_BATCHES if Bp % MEM_BATCHES == 0 and b_off % MEM_BATCHES == 0 else 1
        kT, vm = pl.pallas_call(
            _mem_kv_kernel,
            grid=(Bp // mb,),
            in_specs=[pl.BlockSpec((mb, M, D), lambda b: (b + b_off // mb, 0, 0)), _full((1, D)), _full((D, 2 * MEM_W)),
                      _full((1, MEM_W)), _full((MEM_W, MEM_W))],
            out_specs=[pl.BlockSpec((mb, MEM_W, M), lambda b: (b, 0, 0)),
                       pl.BlockSpec((mb, M, MEM_W), lambda b: (b, 0, 0))],
            out_shape=[jax.ShapeDtypeStruct((Bp, MEM_W, M), BF16), jax.ShapeDtypeStruct((Bp, M, MEM_W), BF16)],
            compiler_params=_cparams(("parallel",)),
            name="mem_kv",
        )(mem, row(g_mem_norm[0]), w_kv_b, tile_row(g_ck[0], N_MEM_HEADS), ones256)

        tiles_per_batch = S // tm
        tok = lambda w: pl.BlockSpec((tm, w), lambda i: (i, 0))
        tok_in = lambda w: pl.BlockSpec((tm, w), lambda i: (i + t_off, 0))
        qn, kn, vv, gc = pl.pallas_call(
            _mixer_in_kernel,
            grid=(N // tm,),
            in_specs=[tok_in(D), pl.BlockSpec((ROPE_STACK_ROWS, tm), lambda i: (0, i + t_off)),
                      _full((ROPE_STACK_ROWS, 2 * LANES)),
                      _full((1, D)), _full((D, D_IN_PROJ)),
                      _full((1, DIFF_W)), _full((1, DIFF_W)),
                      _full((1, GMLP_W)), _full((CHUNK, GMLP_GROUPS * CHUNK)), _full((CHUNK, GMLP_W)),
                      _full((1, MEM_W)), _full((MEM_W, MEM_W)),
                      pl.BlockSpec((1, MEM_W, M), lambda i: (i // tiles_per_batch, 0, 0)),
                      pl.BlockSpec((1, M, MEM_W), lambda i: (i // tiles_per_batch, 0, 0))],
            out_specs=[tok(DIFF_W), tok(DIFF_W), tok(DIFF_W), tok(GMLP_W + MEM_W)],
            out_shape=[jax.ShapeDtypeStruct((N, DIFF_W), BF16)] * 3 + [jax.ShapeDtypeStruct((N, GMLP_W + MEM_W), BF16)],
            compiler_params=_cparams(("parallel",)),
            name="mixer_in",
        )(xf, rope_cs, rope_spread, row(g_mix_norm[0]), w_in_b,
          tile_row(g_dq[0], 2 * N_DIFF_HEADS), tile_row(g_dk[0], 2 * N_DIFF_HEADS),
          row(g_sgu[0]), w_sp_lanes, b_sp_lanes, tile_row(g_cq[0], N_MEM_HEADS), ones256, kT, vm)

        tq = ATTN_TILE
        q_rows = ATTN_Q_TILES_PER_STEP * tq
        nqp = S // q_rows
        head_q = pl.BlockSpec((q_rows, DIFF_W), lambda b, i: (b * nqp + i, 0))
        head_kv = pl.BlockSpec((S, DIFF_W), lambda b, i: (b, 0))
        lam_spec = pl.BlockSpec((1, DIFF_HEAD_DIM), lambda b, i: (0, 0))
        nh = N_DIFF_HEADS
        attn = pl.pallas_call(
            _diff_attn_kernel,
            grid=(Bp, nqp),
            in_specs=[head_q, head_kv, head_kv, lam_spec, lam_spec, lam_spec, lam_spec,
                      pl.BlockSpec((DIFF_V_DIM, 1), lambda b, i: (0, 0))],
            out_specs=head_q,
            out_shape=jax.ShapeDtypeStruct((N, DIFF_W), BF16),
            scratch_shapes=[pltpu.VMEM((nh, tq, 2 * tq), F32), pltpu.VMEM((nh, tq, 2 * tq), F32),
                            pltpu.VMEM((nh, 1, 2 * tq), F32), pltpu.VMEM((nh, ATTN_EXT_ROWS, 2 * tq), F32),
                            pltpu.VMEM((nh, ATTN_EXT_ROWS, S), BF16)],
            compiler_params=_cparams(("arbitrary", "arbitrary")),
            name="diff_attn",
        )(qn, kn, vv, row(lambda_q1[0]), row(lambda_k1[0]), row(lambda_q2[0]), row(lambda_k2[0]),
          g_subln[0].reshape(DIFF_V_DIM, 1).astype(F32))

        rtok = lambda w: pl.BlockSpec((tr, w), lambda i: (i, 0))
        rtok_in = lambda w: pl.BlockSpec((tr, w), lambda i: (i + r_off, 0))
        tokT = lambda: pl.BlockSpec((TOP_K, tr), lambda i: (0, i))
        hw = D // 4
        x1, hpa, hpb, idxT, gateT, posT, counts = pl.pallas_call(
            _out_router_kernel,
            grid=(N // tr,),
            in_specs=[rtok_in(D), rtok(DIFF_W), rtok(GMLP_W + MEM_W), _full((D, D)), _full((1, D)),
                      _full((N_EXPERTS, D)), _full((N_EXPERTS, D)), _full((N_EXPERTS, 1)), _full((tr, tr))],
            out_specs=[rtok(D), rtok(hw), rtok(hw), tokT(), pl.BlockSpec((GATE_ROWS, tr), lambda i: (0, i)), tokT(),
                       _full((N_EXPERTS, 1))],
            out_shape=[jax.ShapeDtypeStruct((N, D), F32), jax.ShapeDtypeStruct((N, hw), U32),
                       jax.ShapeDtypeStruct((N, hw), U32), jax.ShapeDtypeStruct((TOP_K, N), I32),
                       jax.ShapeDtypeStruct((GATE_ROWS, N), BF16),
                       jax.ShapeDtypeStruct((TOP_K, N), I32), jax.ShapeDtypeStruct((N_EXPERTS, 1), F32)],
            scratch_shapes=[pltpu.VMEM((N_EXPERTS, 1), F32)],
            compiler_params=_cparams(("arbitrary",)),
            name="out_router",
        )(xf, attn, gc, w_out_b, row(g_ffn_norm[0]), wr_hi, wr_lo, b_router[0].reshape(N_EXPERTS, 1).astype(F32), before)

        destT, block_expert, block_valid, nb_used = pl.pallas_call(
            _dest_kernel,
            grid_spec=pltpu.PrefetchScalarGridSpec(
                num_scalar_prefetch=1,
                grid=(1,),
                in_specs=[pl.BlockSpec((TOP_K, N), lambda i, c: (0, 0)), pl.BlockSpec((TOP_K, N), lambda i, c: (0, 0))],
                out_specs=[pl.BlockSpec((TOP_K, N), lambda i, c: (0, 0)), pl.BlockSpec((1, nb_pad), lambda i, c: (0, 0)),
                           pl.BlockSpec((1, nb_pad), lambda i, c: (0, 0)), pl.BlockSpec((1, LANES), lambda i, c: (0, 0))],
            ),
            out_shape=[jax.ShapeDtypeStruct((TOP_K, N), I32), jax.ShapeDtypeStruct((1, nb_pad), I32),
                       jax.ShapeDtypeStruct((1, nb_pad), I32), jax.ShapeDtypeStruct((1, LANES), I32)],
            compiler_params=_cparams(("arbitrary",)),
            name="dest",
        )(counts.reshape(N_EXPERTS).astype(I32), idxT, posT)

        dest_rows = [destT[k].reshape(1, N) for k in range(TOP_K)]
        xa_buf = _sc_scatter_rows(hpa, dest_rows, n_rows)
        xb_buf = _sc_scatter_rows(hpb, dest_rows, n_rows)

        cnt_i = counts.reshape(N_EXPERTS).astype(I32)
        owner = jnp.where(cnt_i > 0, jnp.arange(N_EXPERTS, dtype=I32), N_EXPERTS)
        later = jnp.concatenate([lax.cummin(owner[::-1])[::-1][1:], jnp.full((1,), N_EXPERTS, I32)])
        next_expert = jnp.where(later < N_EXPERTS, later, -1)
        last = lambda b, be, nbu, nxt, valid: jnp.minimum(b, nbu[0] - 1)
        row_blk = lambda: pl.BlockSpec((ROW_BLOCK, hw), lambda b, be, nbu, nxt, valid: (last(b, be, nbu, nxt, valid), 0))
        y_blk = pl.BlockSpec((ROW_BLOCK, 2 * hw), lambda b, be, nbu, nxt, valid: (last(b, be, nbu, nxt, valid), 0))
        y_buf = pl.pallas_call(
            _expert_ffn_kernel,
            grid_spec=pltpu.PrefetchScalarGridSpec(
                num_scalar_prefetch=4,
                grid=(n_blocks,),
                in_specs=[row_blk(), row_blk(),
                          pl.BlockSpec(memory_space=pl.ANY),
                          pl.BlockSpec((1, 1, 2 * D_FF), lambda b, be, nbu, nxt, valid: (be[b], 0, 0)),
                          pl.BlockSpec(memory_space=pl.ANY),
                          pl.BlockSpec((1, 1, D), lambda b, be, nbu, nxt, valid: (be[b], 0, 0))],
                out_specs=y_blk,
                scratch_shapes=[pltpu.VMEM((D, 2 * D_FF), F32), pltpu.VMEM((D_FF, D), F32),
                                pltpu.VMEM((D, 2 * D_FF), BF16), pltpu.VMEM((D_FF, D), BF16),
                                pltpu.SemaphoreType.DMA((2,))],
            ),
            out_shape=jax.ShapeDtypeStruct((n_rows, 2 * hw), U32),
            compiler_params=_cparams(("arbitrary",)),
            name="expert_ffn",
        )(block_expert[0, :n_blocks], nb_used[0, :1], next_expert, block_valid[0, :n_blocks], xa_buf, xb_buf,
          w_mlp1[0], b1r, w_mlp2[0], b2r)

        last_part = b_off + Bp == B
        win = SC_GATHER_WINDOW if last_part else SC_GATHER_WINDOW_COVERED
        yg = _sc_gather_rows(y_buf, destT.reshape(n_assign), win).reshape(TOP_K, N, 2 * hw)
        tc = COMBINE_TILE
        prev = () if out is None else (out,)
        out = pl.pallas_call(
            _combine_kernel if out is None else _combine_into_kernel,
            grid=(N // tc,),
            in_specs=[pl.BlockSpec((tc, D), lambda i: (i, 0)),
                      pl.BlockSpec((TOP_K, tc, 2 * hw), lambda i: (0, i, 0)),
                      pl.BlockSpec((GATE_ROWS, tc), lambda i: (0, i)),
                      _full((GATE_ROWS, TOP_K * LANES))] + [pl.BlockSpec(memory_space=pl.ANY)] * len(prev),
            out_specs=pl.BlockSpec((tc, D), lambda i: (i + b_off * S // tc, 0)),
            out_shape=jax.ShapeDtypeStruct((B * S, D), F32),
            input_output_aliases={4: 0} if prev else {},
            compiler_params=_cparams(("parallel",)),
            name="combine",
        )(x1, yg, gateT, gate_spread, *prev)
        b_off += Bp
    return out.reshape(B, S, D)
```

```python
import functools

import jax
import jax.numpy as jnp
from jax import lax
from jax.experimental import pallas as pl
from jax.experimental.pallas import tpu as pltpu
from jax.experimental.pallas import tpu_sc as plsc

F32 = jnp.float32
BF16 = jnp.bfloat16
I32 = jnp.int32
U32 = jnp.uint32

D_MODEL = 1024
N_DIFF_HEADS = 4
DIFF_HEAD_DIM = 64
DIFF_V_DIM = 128
DIFF_W = 512
GMLP_W = 256
GMLP_GROUPS = 4
CHUNK = 128
MEM_W = 256
N_MEM_HEADS = 4
HEAD_GROUP = 64
D_IN_PROJ = 2304
ROPE_THETA = 500000.0
ROT_DIM = 16
ROPE_STACK_ROWS = 2 * ROT_DIM
N_EXPERTS = 32
TOP_K = 4
D_FF = 1024
SWIGLU_LIMIT = 7.0
SWIGLU_ALPHA = 1.702
EPS = 1e-6
LAMBDA_INIT = 0.8 - 0.6

LANES = 128
ROW_BLOCK = 1024
FFN_SUB_ROWS = 256
TOKEN_TILE = 1024
COMBINE_TILE = 1024
MEM_BATCHES = 4
LAST_PART_DIVISOR = 2
ROUTER_TILE = 1024
ROUTER_X_SLOTS = 3
GATE_ROWS = 16
ATTN_TILE = 256
ATTN_Q_TILES_PER_STEP = 8
ATTN_EXT_ROWS = 128 + 16
VMEM_LIMIT = 56 * 1024 * 1024
NEG_BIG = -1e30
LOG2_E = 1.4426950408889634


def _cparams(sem):
    return pltpu.CompilerParams(dimension_semantics=sem, vmem_limit_bytes=VMEM_LIMIT)


def _dot(a, b):
    return jnp.dot(a, b, preferred_element_type=F32)


def _dot_nt(a, b):
    return lax.dot_general(a, b, (((1,), (1,)), ((), ())), preferred_element_type=F32)


def _rms(x, gain):
    ms = jnp.mean(x * x, axis=-1, keepdims=True)
    return x * lax.rsqrt(ms + EPS) * gain


def _group_rms(t, ones_bd, gain):
    w = ones_bd.shape[0]
    chunks = []
    for j in range(t.shape[1] // w):
        c = t[:, j * w:(j + 1) * w]
        ss = _dot((c * c).astype(BF16), ones_bd)
        chunks.append(c * lax.rsqrt(ss * (1.0 / HEAD_GROUP) + EPS))
    return (chunks[0] if len(chunks) == 1 else jnp.concatenate(chunks, axis=1)) * gain


def _pack_bf16_pairs(v):
    w = v.shape[1] // 2
    bits = lax.bitcast_convert_type(v.astype(BF16).astype(F32), U32)
    return (bits[:, :w] & jnp.uint32(0xFFFF0000)) | (bits[:, w:] >> jnp.uint32(16))


def _unpack_bf16_pairs(words):
    hi = lax.bitcast_convert_type(words & jnp.uint32(0xFFFF0000), F32)
    lo = lax.bitcast_convert_type(words << jnp.uint32(16), F32)
    return hi, lo


def _mem_kv_kernel(mem_ref, gmem_ref, wkv_ref, gck_ref, ones_ref, kT_ref, v_ref):
    for j in range(mem_ref.shape[0]):
        m = _rms(mem_ref[j], gmem_ref[...]).astype(BF16)
        kv = _dot(m, wkv_ref[...])
        k = _group_rms(kv[:, :MEM_W], ones_ref[...], gck_ref[...])
        kT_ref[j] = k.T.astype(BF16)
        v_ref[j] = kv[:, MEM_W:].astype(BF16)


def _gelu_tanh(x):
    return 0.5 * x * (1.0 + jnp.tanh(0.7978845608028654 * (x + 0.044715 * (x * x * x))))


def _mixer_in_kernel(x_ref, cs_ref, spread_ref, gmix_ref, win_ref, gq_ref, gk_ref,
                     gsgu_ref, wsp_ref, bsp_ref, gcq_ref, ones256_ref, kT_ref, vm_ref,
                     q_out, k_out, v_out, gc_out):
    tm = x_ref.shape[0]
    hb = _rms(x_ref[...], gmix_ref[...]).astype(BF16)

    def proj(lo, hi):
        return _dot(hb, win_ref[:, lo:hi])

    lane = lax.broadcasted_iota(I32, (tm, LANES), 1)
    first_half = (lane % HEAD_GROUP) < (ROT_DIM // 2)
    tab = lax.dot_general(cs_ref[...], spread_ref[...], (((0,), (0,)), ((), ())), preferred_element_type=F32)
    cosb = tab[:, :LANES] + jnp.where((lane % HEAD_GROUP) >= ROT_DIM, 1.0, 0.0)
    sinb = tab[:, LANES:]

    def norm_rope(t, gain, out_ref):
        tn = _group_rms(t, ones256_ref[...], gain)
        for j in range(DIFF_W // LANES):
            c = tn[:, j * LANES:(j + 1) * LANES]
            partner = jnp.where(first_half, pltpu.roll(c, LANES - ROT_DIM // 2, 1), pltpu.roll(c, ROT_DIM // 2, 1))
            out_ref[:, j * LANES:(j + 1) * LANES] = (c * cosb + partner * sinb).astype(BF16)

    norm_rope(proj(0, DIFF_W), gq_ref[...] * (DIFF_HEAD_DIM ** -0.5 * LOG2_E), q_out)
    norm_rope(proj(DIFF_W, 2 * DIFF_W), gk_ref[...], k_out)
    v_out[...] = proj(2 * DIFF_W, 3 * DIFF_W).astype(BF16)

    z = _gelu_tanh(proj(3 * DIFF_W, 3 * DIFF_W + 2 * GMLP_W))
    u = z[:, :GMLP_W]
    vg = z[:, GMLP_W:]
    vc = vg - jnp.mean(vg, axis=-1, keepdims=True)
    vgn = (vc * lax.rsqrt(jnp.mean(vc * vc, axis=-1, keepdims=True) + EPS) * gsgu_ref[...]).astype(BF16)
    wrow = lax.broadcasted_iota(I32, (CHUNK, GMLP_GROUPS * CHUNK), 0)
    wcol = lax.broadcasted_iota(I32, (CHUNK, GMLP_GROUPS * CHUNK), 1) % CHUNK
    w_causal = jnp.where(wcol <= wrow, wsp_ref[...], 0.0).astype(BF16)
    grp = lax.broadcasted_iota(I32, (CHUNK, GMLP_W), 1) // HEAD_GROUP
    zero_b = jnp.zeros((CHUNK, GMLP_W), BF16)
    for r in range(tm // CHUNK):
        vchunk = vgn[r * CHUNK:(r + 1) * CHUNK, :]
        v_bd = jnp.concatenate([jnp.where(grp == g, vchunk, zero_b) for g in range(GMLP_GROUPS)], axis=0)
        mixed = _dot(w_causal, v_bd) + bsp_ref[...]
        gc_out[r * CHUNK:(r + 1) * CHUNK, 0:GMLP_W] = (u[r * CHUNK:(r + 1) * CHUNK, :] * mixed).astype(BF16)

    pc = proj(3 * DIFF_W + 2 * GMLP_W, D_IN_PROJ)
    qc = _group_rms(pc, ones256_ref[...], gcq_ref[...] * (HEAD_GROUP ** -0.5)).astype(BF16)
    hgrp = lax.broadcasted_iota(I32, (tm, MEM_W), 1) // HEAD_GROUP
    zero_q = jnp.zeros((tm, MEM_W), BF16)
    q_st = jnp.concatenate([jnp.where(hgrp == h, qc, zero_q) for h in range(N_MEM_HEADS)], axis=0)
    s = _dot(q_st, kT_ref[0])
    p = jnp.exp(s - jnp.max(s, axis=-1, keepdims=True))
    o = _dot(p.astype(BF16), vm_ref[0]) / jnp.sum(p, axis=-1, keepdims=True)
    c = jnp.zeros((tm, MEM_W), F32)
    for h in range(N_MEM_HEADS):
        c = c + jnp.where(hgrp == h, o[h * tm:(h + 1) * tm, :], 0.0)
    gc_out[:, GMLP_W:GMLP_W + MEM_W] = c.astype(BF16)


def _diff_attn_kernel(q_ref, k_ref, v_ref, lq1_ref, lk1_ref, lq2_ref, lk2_ref, gsub_ref, o_ref,
                      s0_ref, s1_ref, m_ref, acc_ref, vT_ref):
    tq = ATTN_TILE
    tiles_per_step = q_ref.shape[0] // tq
    assert tiles_per_step % 2 == 0
    seq = k_ref.shape[0]
    step = pl.program_id(1)
    feat = lax.broadcasted_iota(I32, (DIFF_V_DIM, tq), 0)
    heads = range(N_DIFF_HEADS)
    hl = lambda h: slice(h * DIFF_V_DIM, (h + 1) * DIFF_V_DIM)
    ext_rows = vT_ref.shape[1]

    @pl.when(step == 0)
    def _():
        ones_row = jnp.where(lax.broadcasted_iota(I32, (ext_rows - DIFF_V_DIM, seq), 0) == 0, 1.0, 0.0).astype(BF16)
        for h in heads:
            for c in range(seq // tq):
                vT_ref[h, 0:DIFF_V_DIM, c * tq:(c + 1) * tq] = v_ref[c * tq:(c + 1) * tq, hl(h)].T
            vT_ref[h, DIFF_V_DIM:ext_rows, :] = ones_row

    @pl.when(jnp.logical_and(pl.program_id(0) == 0, step == 0))
    def _():
        acc_ref[...] = jnp.zeros(acc_ref.shape, F32)

    lam = (jnp.exp(jnp.sum(lq1_ref[...] * lk1_ref[...], axis=-1, keepdims=True))
           - jnp.exp(jnp.sum(lq2_ref[...] * lk2_ref[...], axis=-1, keepdims=True)) + LAMBDA_INIT)

    def q_tile(j, first_ref, other_ref):
        i = tiles_per_step * step + j
        q_rows = slice(j * tq, (j + 1) * tq)

        def stacked_qT(h):
            qT = q_ref[q_rows, hl(h)].T
            zero = jnp.zeros_like(qT)
            return jnp.concatenate(
                [jnp.where(feat < DIFF_HEAD_DIM, qT, zero), jnp.where(feat >= DIFF_HEAD_DIM, qT, zero)], axis=1)

        qsT = [stacked_qT(h) for h in heads]

        def scores(t, s_ref):
            rows = pl.ds(pl.multiple_of(t * tq, tq), tq)
            for h in heads:
                s_ref[h] = _dot(k_ref[rows, hl(h)], qsT[h])

        def update(t, s_ref, causal):
            cols = pl.ds(pl.multiple_of(t * tq, tq), tq)
            for h in heads:
                s = s_ref[h]
                if causal:
                    key = lax.broadcasted_iota(I32, (tq, 2 * tq), 0)
                    qry = lax.broadcasted_iota(I32, (tq, 2 * tq), 1) % tq
                    s = jnp.where(key <= qry, s, NEG_BIG)
                m = m_ref[h]
                m_new = jnp.maximum(m, jnp.max(s, axis=0, keepdims=True))
                alpha = jnp.exp2(m - m_new)
                p = jnp.exp2(s - m_new)
                m_ref[h] = m_new
                acc_ref[h] = alpha * acc_ref[h] + _dot(vT_ref[h, :, cols], p.astype(BF16))

        m_ref[...] = jnp.full(m_ref.shape, NEG_BIG, F32)
        scores(0, first_ref)

        def two_tiles(pidx, carry):
            t = 2 * pidx
            scores(t + 1, other_ref)
            update(t, first_ref, False)
            scores(t + 2, first_ref)
            update(t + 1, other_ref, False)
            return carry

        lax.fori_loop(0, (tiles_per_step // 2) * step + j // 2, two_tiles, 0)

        if j % 2 == 0:
            update(i, first_ref, True)
        else:
            scores(i, other_ref)
            update(i - 1, first_ref, False)
            update(i, other_ref, True)

        for h in heads:
            on = acc_ref[h, 0:DIFF_V_DIM, :] * (1.0 / acc_ref[h, DIFF_V_DIM:DIFF_V_DIM + 1, :])
            o = on[:, :tq] - lam * on[:, tq:]
            ms = jnp.mean(o * o, axis=0, keepdims=True)
            o = o * lax.rsqrt(ms + EPS) * gsub_ref[...] * (1.0 - LAMBDA_INIT)
            o_ref[q_rows, hl(h)] = o.T.astype(BF16)

    first_ref, other_ref = s0_ref, s1_ref
    for j in range(tiles_per_step):
        q_tile(j, first_ref, other_ref)
        last_read = first_ref if j % 2 == 0 else other_ref
        first_ref, other_ref = (other_ref, first_ref) if last_read is first_ref else (first_ref, other_ref)


def _out_router_kernel(x_hbm, a_ref, gc_ref, wo_ref, gffn_ref, wrh_ref, wrl_ref, br_ref, before_ref,
                       x1_out, hpa_out, hpb_out, idx_out, gate_out, pos_out, cnt_out, carry_ref, xbuf_ref, xsem_ref,
                       *, tile_offset, n_steps):
    tm = a_ref.shape[0]
    step = pl.program_id(0)
    n_slots = xbuf_ref.shape[0]

    def x_copy(s, slot):
        rows = pl.ds(pl.multiple_of((s + tile_offset) * tm, tm), tm)
        return pltpu.make_async_copy(x_hbm.at[rows, :], xbuf_ref.at[slot], xsem_ref.at[slot])

    @pl.when(step == 0)
    def _():
        carry_ref[...] = jnp.zeros_like(carry_ref)
        for s in range(min(n_slots - 1, n_steps)):
            x_copy(s, s).start()

    @pl.when(step + (n_slots - 1) < n_steps)
    def _():
        ahead = step + (n_slots - 1)
        x_copy(ahead, ahead % n_slots).start()

    x_copy(step, step % n_slots).wait()
    x_ref = xbuf_ref.at[step % n_slots]

    sub = before_ref.shape[0]
    wr_stack = jnp.concatenate([wrh_ref[...], wrl_ref[...]], axis=0)
    eio = lax.broadcasted_iota(I32, (N_EXPERTS, sub), 0)
    before_b = before_ref[...]
    carry = carry_ref[...]
    for r in range(tm // sub):
        rows = slice(r * sub, (r + 1) * sub)
        mix = jnp.concatenate([a_ref[rows, :], gc_ref[rows, :]], axis=1)
        x1 = x_ref[rows, :] + _dot(mix, wo_ref[...])
        x1_out[rows, :] = x1
        h2 = _rms(x1, gffn_ref[...])
        hb = h2.astype(BF16)
        hpa_out[rows, :] = _pack_bf16_pairs(h2[:, :D_MODEL // 2])
        hpb_out[rows, :] = _pack_bf16_pairs(h2[:, D_MODEL // 2:])

        h_lo = (h2 - hb.astype(F32)).astype(BF16)
        both = _dot_nt(wr_stack, hb)
        logits = (both[:N_EXPERTS] + both[N_EXPERTS:]) + _dot_nt(wrh_ref[...], h_lo) + br_ref[...]

        vals, idxs, sels = [], [], []
        cur = logits
        for _ in range(TOP_K):
            m = jnp.max(cur, axis=0, keepdims=True)
            ik = jnp.min(jnp.where(cur == m, eio, N_EXPERTS), axis=0, keepdims=True)
            sel = eio == ik
            cur = jnp.where(sel, -jnp.inf, cur)
            vals.append(m)
            idxs.append(ik)
            sels.append(sel)
        es = [jnp.exp(v - vals[0]) for v in vals]
        tot = es[0] + es[1] + es[2] + es[3]
        gates = jnp.concatenate([e / tot for e in es], axis=0)
        g_hi = gates.astype(BF16)
        g_lo = (gates - g_hi.astype(F32)).astype(BF16)
        gate_out[:, rows] = jnp.concatenate(
            [g_hi, g_lo, jnp.zeros((GATE_ROWS - 2 * TOP_K, sub), BF16)], axis=0)
        idx_out[:, rows] = jnp.concatenate(idxs, axis=0)

        cnt = jnp.zeros((N_EXPERTS, sub), F32)
        for sel in sels:
            cnt = cnt + jnp.where(sel, 1.0, 0.0)
        base = carry + _dot(cnt.astype(BF16), before_b)
        pos_out[:, rows] = jnp.concatenate(
            [jnp.sum(jnp.where(sel, base, 0.0), axis=0, keepdims=True) for sel in sels], axis=0).astype(I32)
        carry = carry + jnp.sum(cnt, axis=1, keepdims=True)
    carry_ref[...] = carry
    cnt_out[...] = carry


def _dest_kernel(cnt_ref, idx_ref, pos_ref, dest_out, be_out, valid_out, nbu_out):
    idx = idx_ref[...]
    dest = pos_ref[...]
    bidx = lax.broadcasted_iota(I32, be_out.shape, 1)
    be = jnp.zeros(be_out.shape, I32)
    valid = jnp.zeros(be_out.shape, I32)
    run = jnp.int32(0)
    for e in range(N_EXPERTS):
        dest = dest + jnp.where(idx == e, run, 0)
        first_block = run // ROW_BLOCK
        run = run + ((cnt_ref[e] + (ROW_BLOCK - 1)) // ROW_BLOCK) * ROW_BLOCK
        mine = jnp.logical_and(bidx >= first_block, bidx < run // ROW_BLOCK)
        valid = jnp.where(mine, jnp.clip(cnt_ref[e] - (bidx - first_block) * ROW_BLOCK, 0, ROW_BLOCK), valid)
        be = be + jnp.where(bidx >= run // ROW_BLOCK, 1, 0)
    dest_out[...] = dest
    be_out[...] = jnp.minimum(be, N_EXPERTS - 1)
    valid_out[...] = valid
    nbu_out[...] = jnp.zeros(nbu_out.shape, I32) + run // ROW_BLOCK


def _expert_ffn_kernel(be_ref, nbu_ref, nxt_ref, valid_ref, xa_ref, xb_ref, w1_hbm, b1_ref, w2_hbm, b2_ref,
                       y_ref, w1f_ref, w2f_ref, w1s_ref, w2s_ref, sem_ref):
    b = pl.program_id(0)

    def weight_copies(e):
        return (pltpu.make_async_copy(w1_hbm.at[e], w1f_ref, sem_ref.at[0]),
                pltpu.make_async_copy(w2_hbm.at[e], w2f_ref, sem_ref.at[1]))

    def mlp(rows):
        parts = _unpack_bf16_pairs(xa_ref[0:rows, :]) + _unpack_bf16_pairs(xb_ref[0:rows, :])
        xrow = jnp.concatenate([p.astype(BF16) for p in parts], axis=1)
        hm = _dot(xrow, w1s_ref[...]) + b1_ref[0]
        glu = jnp.minimum(hm[:, :D_FF], SWIGLU_LIMIT)
        lin = jnp.clip(hm[:, D_FF:], -SWIGLU_LIMIT, SWIGLU_LIMIT)
        act = glu * jax.nn.sigmoid(SWIGLU_ALPHA * glu) * (lin + 1.0)
        y = _dot(act.astype(BF16), w2s_ref[...]) + b2_ref[0]
        hw = D_MODEL // 4
        y_ref[0:rows, 0:hw] = _pack_bf16_pairs(y[:, :D_MODEL // 2])
        y_ref[0:rows, hw:] = _pack_bf16_pairs(y[:, D_MODEL // 2:])

    @pl.when(b < nbu_ref[0])
    def _():
        e = be_ref[b]

        @pl.when(b == 0)
        def _():
            for cp in weight_copies(e):
                cp.start()

        @pl.when(jnp.logical_or(b == 0, e != be_ref[jnp.maximum(b - 1, 0)]))
        def _():
            for cp in weight_copies(e):
                cp.wait()
            w1s_ref[...] = w1f_ref[...].astype(BF16)
            w2s_ref[...] = w2f_ref[...].astype(BF16)
            nxt = nxt_ref[e]

            @pl.when(nxt >= 0)
            def _():
                for cp in weight_copies(nxt):
                    cp.start()

        n_sub = (valid_ref[b] + (FFN_SUB_ROWS - 1)) // FFN_SUB_ROWS
        for k in range(1, ROW_BLOCK // FFN_SUB_ROWS + 1):
            pl.when(n_sub == k)(functools.partial(mlp, k * FFN_SUB_ROWS))


SC_WINDOW = 128
SC_GATHER_WINDOW = 64
SC_GATHER_WINDOW_COVERED = 32


def _sc_mesh():
    return plsc.VectorSubcoreMesh(core_axis_name="c", subcore_axis_name="s")


def _sc_gather_rows(table, idx, win):
    n = idx.shape[0]
    width = table.shape[1]

    @functools.partial(pl.kernel, out_type=jax.ShapeDtypeStruct((n, width), table.dtype), mesh=_sc_mesh(),
                       scratch_types=[])
    def gather_kernel(t_hbm, i_hbm, o_hbm):
        def body(i_vmem, o_vmem):
            pltpu.sync_copy(t_hbm.at[i_vmem.at[0]], o_vmem)

        pltpu.emit_pipeline(
            body,
            grid=(n // win,),
            in_specs=[pl.BlockSpec((1, win), lambda i: (i, 0))],
            out_specs=[pl.BlockSpec((win, width), lambda i: (i, 0))],
            core_axis_name=("c", "s"),
            dimension_semantics=(pltpu.PARALLEL,),
        )(i_hbm, o_hbm)

    return gather_kernel(table, idx.reshape(n // win, win))


def _sc_scatter_rows(rows, idx_rows, n_out):
    n, width = rows.shape

    @functools.partial(pl.kernel, out_type=jax.ShapeDtypeStruct((n_out, width), rows.dtype), mesh=_sc_mesh(),
                       scratch_types=[])
    def scatter_kernel(r_hbm, *refs):
        i_hbms, o_hbm = refs[:-1], refs[-1]

        def body(r_vmem, *i_vmems):
            for i_vmem in i_vmems:
                pltpu.sync_copy(r_vmem, o_hbm.at[i_vmem.at[0]])

        pltpu.emit_pipeline(
            body,
            grid=(n // SC_WINDOW,),
            in_specs=[pl.BlockSpec((SC_WINDOW, width), lambda i: (i, 0))]
            + [pl.BlockSpec((1, SC_WINDOW), lambda i: (0, i)) for _ in i_hbms],
            out_specs=[],
            core_axis_name=("c", "s"),
            dimension_semantics=(pltpu.PARALLEL,),
        )(r_hbm, *i_hbms)

    return scatter_kernel(rows, *idx_rows)


def _combine_kernel(x1_ref, yg_ref, gate_ref, spread_ref, o_ref):
    q = D_MODEL // 4
    x1 = x1_ref[...]
    acc = [x1[:, j * q:(j + 1) * q] for j in range(4)]
    gfull = lax.dot_general(gate_ref[...], spread_ref[...], (((0,), (0,)), ((), ())), preferred_element_type=F32)
    for k in range(TOP_K):
        parts = _unpack_bf16_pairs(yg_ref[k, :, 0:q]) + _unpack_bf16_pairs(yg_ref[k, :, q:])
        g = jnp.tile(gfull[:, k * LANES:(k + 1) * LANES], (1, q // LANES))
        acc = [a + g * p for a, p in zip(acc, parts)]
    for j in range(4):
        o_ref[:, j * q:(j + 1) * q] = acc[j]


def _combine_into_kernel(x1_ref, yg_ref, gate_ref, spread_ref, prev_ref, o_ref):
    del prev_ref
    _combine_kernel(x1_ref, yg_ref, gate_ref, spread_ref, o_ref)


def _block_diag_ones(width):
    r = jnp.arange(width) // HEAD_GROUP
    return (r[:, None] == r[None, :]).astype(BF16)


def _rope_tables(positions):
    half = ROT_DIM // 2
    inv_freq = ROPE_THETA ** (-jnp.arange(0, ROT_DIM, 2, dtype=F32) / ROT_DIM)
    ang = inv_freq[:, None] * positions.astype(F32).reshape(1, -1)
    cs = jnp.concatenate([jnp.cos(ang), jnp.sin(ang)], axis=0)
    cs_hi = cs.astype(BF16)
    cs_lo = (cs - cs_hi.astype(F32)).astype(BF16)
    lane = jnp.arange(LANES) % HEAD_GROUP
    j = jnp.arange(half)[:, None]
    lo_half = (lane[None, :] == j).astype(F32)
    hi_half = (lane[None, :] == j + half).astype(F32)
    spread = jnp.concatenate([
        jnp.concatenate([lo_half + hi_half, jnp.zeros((half, LANES), F32)], axis=1),
        jnp.concatenate([jnp.zeros((half, LANES), F32), hi_half - lo_half], axis=1)], axis=0)
    return jnp.concatenate([cs_hi, cs_lo], axis=0), jnp.concatenate([spread, spread], axis=0).astype(BF16)


def _full(shape):
    return pl.BlockSpec(shape, lambda *_: (0,) * len(shape))


def kernel(x, mem, positions, g_mix_norm, w_in, g_dq, g_dk, lambda_q1, lambda_k1, lambda_q2, lambda_k2, g_subln, g_sgu, w_spatial, b_spatial, g_mem_norm, w_mem_kv, g_cq, g_ck, w_out, g_ffn_norm, w_router, b_router, w_mlp1, b_mlp1, w_mlp2, b_mlp2):
    B, S, D = x.shape
    M = mem.shape[1]
    tm = TOKEN_TILE
    tr = ROUTER_TILE
    assert D == D_MODEL and S % tm == 0 and S % (ATTN_Q_TILES_PER_STEP * ATTN_TILE) == 0 and g_mix_norm.shape[0] == 1
    last_batches = B // LAST_PART_DIVISOR
    part_batches = (B - last_batches, last_batches) if last_batches > 0 and S % tr == 0 else (B,)

    xf = x.reshape(B * S, D)
    rope_cs, rope_spread = _rope_tables(positions)
    ones256 = _block_diag_ones(MEM_W)
    row = lambda v: v.reshape(1, -1).astype(F32)
    tile_row = lambda v, reps: jnp.tile(v.reshape(1, -1).astype(F32), (1, reps))
    w_in_b, w_out_b, w_kv_b = w_in[0].astype(BF16), w_out[0].astype(BF16), w_mem_kv[0].astype(BF16)
    w_sp_lanes = jnp.transpose(w_spatial[0], (1, 0, 2)).reshape(CHUNK, GMLP_GROUPS * CHUNK)
    b_sp_lanes = jnp.repeat(b_spatial[0].T, HEAD_GROUP, axis=1)
    wr = w_router[0].T.astype(F32)
    wr_hi = wr.astype(BF16)
    wr_lo = (wr - wr_hi.astype(F32)).astype(BF16)
    before = (jnp.arange(tr)[:, None] < jnp.arange(tr)[None, :]).astype(BF16)
    gate_row = jnp.arange(GATE_ROWS)[:, None]
    gate_spread = ((gate_row < 2 * TOP_K) & (gate_row % TOP_K == jnp.arange(TOP_K * LANES)[None, :] // LANES)).astype(BF16)
    b1r = b_mlp1[0].reshape(N_EXPERTS, 1, 2 * D_FF)
    b2r = b_mlp2[0].reshape(N_EXPERTS, 1, D)

    out = None
    b_off = 0
    for Bp in part_batches:
        N = Bp * S
        assert N % tr == 0 and N % tm == 0
        n_assign = N * TOP_K
        n_blocks = -(-n_assign // ROW_BLOCK) + N_EXPERTS
        n_rows = n_blocks * ROW_BLOCK
        nb_pad = -(-n_blocks // LANES) * LANES
        t_off = b_off * S // tm
        r_off = b_off * S // tr

        mb = MEM_BATCHES if Bp % MEM_BATCHES == 0 and b_off % MEM_BATCHES == 0 else 1
        kT, vm = pl.pallas_call(
            _mem_kv_kernel,
            grid=(Bp // mb,),
            in_specs=[pl.BlockSpec((mb, M, D), lambda b: (b + b_off // mb, 0, 0)), _full((1, D)), _full((D, 2 * MEM_W)),
                      _full((1, MEM_W)), _full((MEM_W, MEM_W))],
            out_specs=[pl.BlockSpec((mb, MEM_W, M), lambda b: (b, 0, 0)),
                       pl.BlockSpec((mb, M, MEM_W), lambda b: (b, 0, 0))],
            out_shape=[jax.ShapeDtypeStruct((Bp, MEM_W, M), BF16), jax.ShapeDtypeStruct((Bp, M, MEM_W), BF16)],
            compiler_params=_cparams(("parallel",)),
            name="mem_kv",
        )(mem, row(g_mem_norm[0]), w_kv_b, tile_row(g_ck[0], N_MEM_HEADS), ones256)

        tiles_per_batch = S // tm
        tok = lambda w: pl.BlockSpec((tm, w), lambda i: (i, 0))
        tok_in = lambda w: pl.BlockSpec((tm, w), lambda i: (i + t_off, 0))
        qn, kn, vv, gc = pl.pallas_call(
            _mixer_in_kernel,
            grid=(N // tm,),
            in_specs=[tok_in(D), pl.BlockSpec((ROPE_STACK_ROWS, tm), lambda i: (0, i + t_off)),
                      _full((ROPE_STACK_ROWS, 2 * LANES)),
                      _full((1, D)), _full((D, D_IN_PROJ)),
                      _full((1, DIFF_W)), _full((1, DIFF_W)),
                      _full((1, GMLP_W)), _full((CHUNK, GMLP_GROUPS * CHUNK)), _full((CHUNK, GMLP_W)),
                      _full((1, MEM_W)), _full((MEM_W, MEM_W)),
                      pl.BlockSpec((1, MEM_W, M), lambda i: (i // tiles_per_batch, 0, 0)),
                      pl.BlockSpec((1, M, MEM_W), lambda i: (i // tiles_per_batch, 0, 0))],
            out_specs=[tok(DIFF_W), tok(DIFF_W), tok(DIFF_W), tok(GMLP_W + MEM_W)],
            out_shape=[jax.ShapeDtypeStruct((N, DIFF_W), BF16)] * 3 + [jax.ShapeDtypeStruct((N, GMLP_W + MEM_W), BF16)],
            compiler_params=_cparams(("parallel",)),
            name="mixer_in",
        )(xf, rope_cs, rope_spread, row(g_mix_norm[0]), w_in_b,
          tile_row(g_dq[0], 2 * N_DIFF_HEADS), tile_row(g_dk[0], 2 * N_DIFF_HEADS),
          row(g_sgu[0]), w_sp_lanes, b_sp_lanes, tile_row(g_cq[0], N_MEM_HEADS), ones256, kT, vm)

        tq = ATTN_TILE
        q_rows = ATTN_Q_TILES_PER_STEP * tq
        nqp = S // q_rows
        head_q = pl.BlockSpec((q_rows, DIFF_W), lambda b, i: (b * nqp + i, 0))
        head_kv = pl.BlockSpec((S, DIFF_W), lambda b, i: (b, 0))
        lam_spec = pl.BlockSpec((1, DIFF_HEAD_DIM), lambda b, i: (0, 0))
        nh = N_DIFF_HEADS
        attn = pl.pallas_call(
            _diff_attn_kernel,
            grid=(Bp, nqp),
            in_specs=[head_q, head_kv, head_kv, lam_spec, lam_spec, lam_spec, lam_spec,
                      pl.BlockSpec((DIFF_V_DIM, 1), lambda b, i: (0, 0))],
            out_specs=head_q,
            out_shape=jax.ShapeDtypeStruct((N, DIFF_W), BF16),
            scratch_shapes=[pltpu.VMEM((nh, tq, 2 * tq), F32), pltpu.VMEM((nh, tq, 2 * tq), F32),
                            pltpu.VMEM((nh, 1, 2 * tq), F32), pltpu.VMEM((nh, ATTN_EXT_ROWS, 2 * tq), F32),
                            pltpu.VMEM((nh, ATTN_EXT_ROWS, S), BF16)],
            compiler_params=_cparams(("arbitrary", "arbitrary")),
            name="diff_attn",
        )(qn, kn, vv, row(lambda_q1[0]), row(lambda_k1[0]), row(lambda_q2[0]), row(lambda_k2[0]),
          g_subln[0].reshape(DIFF_V_DIM, 1).astype(F32))

        rtok = lambda w: pl.BlockSpec((tr, w), lambda i: (i, 0))
        tokT = lambda: pl.BlockSpec((TOP_K, tr), lambda i: (0, i))
        hw = D // 4
        x1, hpa, hpb, idxT, gateT, posT, counts = pl.pallas_call(
            functools.partial(_out_router_kernel, tile_offset=r_off, n_steps=N // tr),
            grid=(N // tr,),
            in_specs=[pl.BlockSpec(memory_space=pl.ANY), rtok(DIFF_W), rtok(GMLP_W + MEM_W), _full((D, D)), _full((1, D)),
                      _full((N_EXPERTS, D)), _full((N_EXPERTS, D)), _full((N_EXPERTS, 1)), _full((tr, tr))],
            out_specs=[rtok(D), rtok(hw), rtok(hw), tokT(), pl.BlockSpec((GATE_ROWS, tr), lambda i: (0, i)), tokT(),
                       _full((N_EXPERTS, 1))],
            out_shape=[jax.ShapeDtypeStruct((N, D), F32), jax.ShapeDtypeStruct((N, hw), U32),
                       jax.ShapeDtypeStruct((N, hw), U32), jax.ShapeDtypeStruct((TOP_K, N), I32),
                       jax.ShapeDtypeStruct((GATE_ROWS, N), BF16),
                       jax.ShapeDtypeStruct((TOP_K, N), I32), jax.ShapeDtypeStruct((N_EXPERTS, 1), F32)],
            scratch_shapes=[pltpu.VMEM((N_EXPERTS, 1), F32), pltpu.VMEM((ROUTER_X_SLOTS, tr, D), F32),
                            pltpu.SemaphoreType.DMA((ROUTER_X_SLOTS,))],
            compiler_params=_cparams(("arbitrary",)),
            name="out_router",
        )(xf, attn, gc, w_out_b, row(g_ffn_norm[0]), wr_hi, wr_lo, b_router[0].reshape(N_EXPERTS, 1).astype(F32), before)

        destT, block_expert, block_valid, nb_used = pl.pallas_call(
            _dest_kernel,
            grid_spec=pltpu.PrefetchScalarGridSpec(
                num_scalar_prefetch=1,
                grid=(1,),
                in_specs=[pl.BlockSpec((TOP_K, N), lambda i, c: (0, 0)), pl.BlockSpec((TOP_K, N), lambda i, c: (0, 0))],
                out_specs=[pl.BlockSpec((TOP_K, N), lambda i, c: (0, 0)), pl.BlockSpec((1, nb_pad), lambda i, c: (0, 0)),
                           pl.BlockSpec((1, nb_pad), lambda i, c: (0, 0)), pl.BlockSpec((1, LANES), lambda i, c: (0, 0))],
            ),
            out_shape=[jax.ShapeDtypeStruct((TOP_K, N), I32), jax.ShapeDtypeStruct((1, nb_pad), I32),
                       jax.ShapeDtypeStruct((1, nb_pad), I32), jax.ShapeDtypeStruct((1, LANES), I32)],
            compiler_params=_cparams(("arbitrary",)),
            name="dest",
        )(counts.reshape(N_EXPERTS).astype(I32), idxT, posT)

        dest_rows = [destT[k].reshape(1, N) for k in range(TOP_K)]
        xa_buf = _sc_scatter_rows(hpa, dest_rows, n_rows)
        xb_buf = _sc_scatter_rows(hpb, dest_rows, n_rows)

        cnt_i = counts.reshape(N_EXPERTS).astype(I32)
        owner = jnp.where(cnt_i > 0, jnp.arange(N_EXPERTS, dtype=I32), N_EXPERTS)
        later = jnp.concatenate([lax.cummin(owner[::-1])[::-1][1:], jnp.full((1,), N_EXPERTS, I32)])
        next_expert = jnp.where(later < N_EXPERTS, later, -1)
        last = lambda b, be, nbu, nxt, valid: jnp.minimum(b, nbu[0] - 1)
        row_blk = lambda: pl.BlockSpec((ROW_BLOCK, hw), lambda b, be, nbu, nxt, valid: (last(b, be, nbu, nxt, valid), 0))
        y_blk = pl.BlockSpec((ROW_BLOCK, 2 * hw), lambda b, be, nbu, nxt, valid: (last(b, be, nbu, nxt, valid), 0))
        y_buf = pl.pallas_call(
            _expert_ffn_kernel,
            grid_spec=pltpu.PrefetchScalarGridSpec(
                num_scalar_prefetch=4,
                grid=(n_blocks,),
                in_specs=[row_blk(), row_blk(),
                          pl.BlockSpec(memory_space=pl.ANY),
                          pl.BlockSpec((1, 1, 2 * D_FF), lambda b, be, nbu, nxt, valid: (be[b], 0, 0)),
                          pl.BlockSpec(memory_space=pl.ANY),
                          pl.BlockSpec((1, 1, D), lambda b, be, nbu, nxt, valid: (be[b], 0, 0))],
                out_specs=y_blk,
                scratch_shapes=[pltpu.VMEM((D, 2 * D_FF), F32), pltpu.VMEM((D_FF, D), F32),
                                pltpu.VMEM((D, 2 * D_FF), BF16), pltpu.VMEM((D_FF, D), BF16),
                                pltpu.SemaphoreType.DMA((2,))],
            ),
            out_shape=jax.ShapeDtypeStruct((n_rows, 2 * hw), U32),
            compiler_params=_cparams(("arbitrary",)),
            name="expert_ffn",
        )(block_expert[0, :n_blocks], nb_used[0, :1], next_expert, block_valid[0, :n_blocks], xa_buf, xb_buf,
          w_mlp1[0], b1r, w_mlp2[0], b2r)

        last_part = b_off + Bp == B
        win = SC_GATHER_WINDOW if last_part else SC_GATHER_WINDOW_COVERED
        yg = _sc_gather_rows(y_buf, destT.reshape(n_assign), win).reshape(TOP_K, N, 2 * hw)
        tc = COMBINE_TILE
        prev = () if out is None else (out,)
        out = pl.pallas_call(
            _combine_kernel if out is None else _combine_into_kernel,
            grid=(N // tc,),
            in_specs=[pl.BlockSpec((tc, D), lambda i: (i, 0)),
                      pl.BlockSpec((TOP_K, tc, 2 * hw), lambda i: (0, i, 0)),
                      pl.BlockSpec((GATE_ROWS, tc), lambda i: (0, i)),
                      _full((GATE_ROWS, TOP_K * LANES))] + [pl.BlockSpec(memory_space=pl.ANY)] * len(prev),
            out_specs=pl.BlockSpec((tc, D), lambda i: (i + b_off * S // tc, 0)),
            out_shape=jax.ShapeDtypeStruct((B * S, D), F32),
            input_output_aliases={4: 0} if prev else {},
            compiler_params=_cparams(("parallel",)),
            name="combine",
        )(x1, yg, gateT, gate_spread, *prev)
        b_off += Bp
    return out.reshape(B, S, D)
```

```python
import functools

import jax
import jax.numpy as jnp
from jax import lax
from jax.experimental import pallas as pl
from jax.experimental.pallas import tpu as pltpu
from jax.experimental.pallas import tpu_sc as plsc

F32 = jnp.float32
BF16 = jnp.bfloat16
I32 = jnp.int32
U32 = jnp.uint32

D_MODEL = 1024
N_DIFF_HEADS = 4
DIFF_HEAD_DIM = 64
DIFF_V_DIM = 128
DIFF_W = 512
GMLP_W = 256
GMLP_GROUPS = 4
CHUNK = 128
MEM_W = 256
N_MEM_HEADS = 4
HEAD_GROUP = 64
D_IN_PROJ = 2304
ROPE_THETA = 500000.0
ROT_DIM = 16
ROPE_STACK_ROWS = 2 * ROT_DIM
N_EXPERTS = 32
TOP_K = 4
D_FF = 1024
SWIGLU_LIMIT = 7.0
SWIGLU_ALPHA = 1.702
EPS = 1e-6
LAMBDA_INIT = 0.8 - 0.6

LANES = 128
ROW_BLOCK = 1024
FFN_SUB_ROWS = 256
TOKEN_TILE = 1024
COMBINE_TILE = 1024
MEM_BATCHES = 4
LAST_PART_DIVISOR = 2
ROUTER_TILE = 1024
PREFIX_CHUNK = 256
GATE_ROWS = 16
ATTN_TILE = 256
ATTN_Q_TILES_PER_STEP = 8
ATTN_EXT_ROWS = 128 + 16
VMEM_LIMIT = 56 * 1024 * 1024
NEG_BIG = -1e30
LOG2_E = 1.4426950408889634


def _cparams(sem):
    return pltpu.CompilerParams(dimension_semantics=sem, vmem_limit_bytes=VMEM_LIMIT)


def _dot(a, b):
    return jnp.dot(a, b, preferred_element_type=F32)


def _dot_nt(a, b):
    return lax.dot_general(a, b, (((1,), (1,)), ((), ())), preferred_element_type=F32)


def _rms(x, gain):
    ms = jnp.mean(x * x, axis=-1, keepdims=True)
    return x * lax.rsqrt(ms + EPS) * gain


def _group_rms(t, ones_bd, gain):
    w = ones_bd.shape[0]
    chunks = []
    for j in range(t.shape[1] // w):
        c = t[:, j * w:(j + 1) * w]
        ss = _dot((c * c).astype(BF16), ones_bd)
        chunks.append(c * lax.rsqrt(ss * (1.0 / HEAD_GROUP) + EPS))
    return (chunks[0] if len(chunks) == 1 else jnp.concatenate(chunks, axis=1)) * gain


def _pack_bf16_pairs(v):
    w = v.shape[1] // 2
    bits = lax.bitcast_convert_type(v.astype(BF16).astype(F32), U32)
    return (bits[:, :w] & jnp.uint32(0xFFFF0000)) | (bits[:, w:] >> jnp.uint32(16))


def _unpack_bf16_pairs(words):
    hi = lax.bitcast_convert_type(words & jnp.uint32(0xFFFF0000), F32)
    lo = lax.bitcast_convert_type(words << jnp.uint32(16), F32)
    return hi, lo


def _mem_kv_kernel(mem_ref, gmem_ref, wkv_ref, gck_ref, ones_ref, kT_ref, v_ref):
    for j in range(mem_ref.shape[0]):
        m = _rms(mem_ref[j], gmem_ref[...]).astype(BF16)
        kv = _dot(m, wkv_ref[...])
        k = _group_rms(kv[:, :MEM_W], ones_ref[...], gck_ref[...])
        kT_ref[j] = k.T.astype(BF16)
        v_ref[j] = kv[:, MEM_W:].astype(BF16)


def _gelu_tanh(x):
    return 0.5 * x * (1.0 + jnp.tanh(0.7978845608028654 * (x + 0.044715 * (x * x * x))))


def _mixer_in_kernel(x_ref, cs_ref, spread_ref, gmix_ref, win_ref, gq_ref, gk_ref,
                     gsgu_ref, wsp_ref, bsp_ref, gcq_ref, ones256_ref, kT_ref, vm_ref,
                     q_out, k_out, v_out, gc_out):
    tm = x_ref.shape[0]
    hb = _rms(x_ref[...], gmix_ref[...]).astype(BF16)

    def proj(lo, hi):
        return _dot(hb, win_ref[:, lo:hi])

    lane = lax.broadcasted_iota(I32, (tm, LANES), 1)
    first_half = (lane % HEAD_GROUP) < (ROT_DIM // 2)
    tab = lax.dot_general(cs_ref[...], spread_ref[...], (((0,), (0,)), ((), ())), preferred_element_type=F32)
    cosb = tab[:, :LANES] + jnp.where((lane % HEAD_GROUP) >= ROT_DIM, 1.0, 0.0)
    sinb = tab[:, LANES:]

    def norm_rope(t, gain, out_ref):
        tn = _group_rms(t, ones256_ref[...], gain)
        for j in range(DIFF_W // LANES):
            c = tn[:, j * LANES:(j + 1) * LANES]
            partner = jnp.where(first_half, pltpu.roll(c, LANES - ROT_DIM // 2, 1), pltpu.roll(c, ROT_DIM // 2, 1))
            out_ref[:, j * LANES:(j + 1) * LANES] = (c * cosb + partner * sinb).astype(BF16)

    norm_rope(proj(0, DIFF_W), gq_ref[...] * (DIFF_HEAD_DIM ** -0.5 * LOG2_E), q_out)
    norm_rope(proj(DIFF_W, 2 * DIFF_W), gk_ref[...], k_out)
    v_out[...] = proj(2 * DIFF_W, 3 * DIFF_W).astype(BF16)

    z = _gelu_tanh(proj(3 * DIFF_W, 3 * DIFF_W + 2 * GMLP_W))
    u = z[:, :GMLP_W]
    vg = z[:, GMLP_W:]
    vc = vg - jnp.mean(vg, axis=-1, keepdims=True)
    vgn = (vc * lax.rsqrt(jnp.mean(vc * vc, axis=-1, keepdims=True) + EPS) * gsgu_ref[...]).astype(BF16)
    wrow = lax.broadcasted_iota(I32, (CHUNK, GMLP_GROUPS * CHUNK), 0)
    wcol = lax.broadcasted_iota(I32, (CHUNK, GMLP_GROUPS * CHUNK), 1) % CHUNK
    w_causal = jnp.where(wcol <= wrow, wsp_ref[...], 0.0).astype(BF16)
    grp = lax.broadcasted_iota(I32, (CHUNK, GMLP_W), 1) // HEAD_GROUP
    zero_b = jnp.zeros((CHUNK, GMLP_W), BF16)
    for r in range(tm // CHUNK):
        vchunk = vgn[r * CHUNK:(r + 1) * CHUNK, :]
        v_bd = jnp.concatenate([jnp.where(grp == g, vchunk, zero_b) for g in range(GMLP_GROUPS)], axis=0)
        mixed = _dot(w_causal, v_bd) + bsp_ref[...]
        gc_out[r * CHUNK:(r + 1) * CHUNK, 0:GMLP_W] = (u[r * CHUNK:(r + 1) * CHUNK, :] * mixed).astype(BF16)

    pc = proj(3 * DIFF_W + 2 * GMLP_W, D_IN_PROJ)
    qc = _group_rms(pc, ones256_ref[...], gcq_ref[...] * (HEAD_GROUP ** -0.5)).astype(BF16)
    hgrp = lax.broadcasted_iota(I32, (tm, MEM_W), 1) // HEAD_GROUP
    zero_q = jnp.zeros((tm, MEM_W), BF16)
    q_st = jnp.concatenate([jnp.where(hgrp == h, qc, zero_q) for h in range(N_MEM_HEADS)], axis=0)
    s = _dot(q_st, kT_ref[0])
    p = jnp.exp(s - jnp.max(s, axis=-1, keepdims=True))
    o = _dot(p.astype(BF16), vm_ref[0]) / jnp.sum(p, axis=-1, keepdims=True)
    c = jnp.zeros((tm, MEM_W), F32)
    for h in range(N_MEM_HEADS):
        c = c + jnp.where(hgrp == h, o[h * tm:(h + 1) * tm, :], 0.0)
    gc_out[:, GMLP_W:GMLP_W + MEM_W] = c.astype(BF16)


def _diff_attn_kernel(q_ref, k_ref, v_ref, lq1_ref, lk1_ref, lq2_ref, lk2_ref, gsub_ref, o_ref,
                      s0_ref, s1_ref, m_ref, acc_ref, vT_ref):
    tq = ATTN_TILE
    tiles_per_step = q_ref.shape[0] // tq
    assert tiles_per_step % 2 == 0
    seq = k_ref.shape[0]
    step = pl.program_id(1)
    feat = lax.broadcasted_iota(I32, (DIFF_V_DIM, tq), 0)
    heads = range(N_DIFF_HEADS)
    hl = lambda h: slice(h * DIFF_V_DIM, (h + 1) * DIFF_V_DIM)
    ext_rows = vT_ref.shape[1]

    @pl.when(step == 0)
    def _():
        ones_row = jnp.where(lax.broadcasted_iota(I32, (ext_rows - DIFF_V_DIM, seq), 0) == 0, 1.0, 0.0).astype(BF16)
        for h in heads:
            for c in range(seq // tq):
                vT_ref[h, 0:DIFF_V_DIM, c * tq:(c + 1) * tq] = v_ref[c * tq:(c + 1) * tq, hl(h)].T
            vT_ref[h, DIFF_V_DIM:ext_rows, :] = ones_row

    @pl.when(jnp.logical_and(pl.program_id(0) == 0, step == 0))
    def _():
        acc_ref[...] = jnp.zeros(acc_ref.shape, F32)

    lam = (jnp.exp(jnp.sum(lq1_ref[...] * lk1_ref[...], axis=-1, keepdims=True))
           - jnp.exp(jnp.sum(lq2_ref[...] * lk2_ref[...], axis=-1, keepdims=True)) + LAMBDA_INIT)

    def q_tile(j, first_ref, other_ref):
        i = tiles_per_step * step + j
        q_rows = slice(j * tq, (j + 1) * tq)

        def stacked_qT(h):
            qT = q_ref[q_rows, hl(h)].T
            zero = jnp.zeros_like(qT)
            return jnp.concatenate(
                [jnp.where(feat < DIFF_HEAD_DIM, qT, zero), jnp.where(feat >= DIFF_HEAD_DIM, qT, zero)], axis=1)

        qsT = [stacked_qT(h) for h in heads]

        def scores(t, s_ref):
            rows = pl.ds(pl.multiple_of(t * tq, tq), tq)
            for h in heads:
                s_ref[h] = _dot(k_ref[rows, hl(h)], qsT[h])

        def update(t, s_ref, causal):
            cols = pl.ds(pl.multiple_of(t * tq, tq), tq)
            for h in heads:
                s = s_ref[h]
                if causal:
                    key = lax.broadcasted_iota(I32, (tq, 2 * tq), 0)
                    qry = lax.broadcasted_iota(I32, (tq, 2 * tq), 1) % tq
                    s = jnp.where(key <= qry, s, NEG_BIG)
                m = m_ref[h]
                m_new = jnp.maximum(m, jnp.max(s, axis=0, keepdims=True))
                alpha = jnp.exp2(m - m_new)
                p = jnp.exp2(s - m_new)
                m_ref[h] = m_new
                acc_ref[h] = alpha * acc_ref[h] + _dot(vT_ref[h, :, cols], p.astype(BF16))

        m_ref[...] = jnp.full(m_ref.shape, NEG_BIG, F32)
        scores(0, first_ref)

        def two_tiles(pidx, carry):
            t = 2 * pidx
            scores(t + 1, other_ref)
            update(t, first_ref, False)
            scores(t + 2, first_ref)
            update(t + 1, other_ref, False)
            return carry

        lax.fori_loop(0, (tiles_per_step // 2) * step + j // 2, two_tiles, 0)

        if j % 2 == 0:
            update(i, first_ref, True)
        else:
            scores(i, other_ref)
            update(i - 1, first_ref, False)
            update(i, other_ref, True)

        for h in heads:
            on = acc_ref[h, 0:DIFF_V_DIM, :] * (1.0 / acc_ref[h, DIFF_V_DIM:DIFF_V_DIM + 1, :])
            o = on[:, :tq] - lam * on[:, tq:]
            ms = jnp.mean(o * o, axis=0, keepdims=True)
            o = o * lax.rsqrt(ms + EPS) * gsub_ref[...] * (1.0 - LAMBDA_INIT)
            o_ref[q_rows, hl(h)] = o.T.astype(BF16)

    first_ref, other_ref = s0_ref, s1_ref
    for j in range(tiles_per_step):
        q_tile(j, first_ref, other_ref)
        last_read = first_ref if j % 2 == 0 else other_ref
        first_ref, other_ref = (other_ref, first_ref) if last_read is first_ref else (first_ref, other_ref)


def _out_router_kernel(x_ref, a_ref, gc_ref, wo_ref, gffn_ref, wrh_ref, wrl_ref, br_ref, before_ref,
                       x1_out, hpa_out, hpb_out, idx_out, gate_out, pos_out, cnt_out, carry_ref):
    tm = x_ref.shape[0]

    @pl.when(pl.program_id(0) == 0)
    def _():
        carry_ref[...] = jnp.zeros_like(carry_ref)

    sub = tm
    pw = before_ref.shape[0]
    wr_stack = jnp.concatenate([wrh_ref[...], wrl_ref[...]], axis=0)
    eio = lax.broadcasted_iota(I32, (N_EXPERTS, sub), 0)
    before_b = before_ref[...]
    carry = carry_ref[...]
    for r in range(tm // sub):
        rows = slice(r * sub, (r + 1) * sub)
        mix = jnp.concatenate([a_ref[rows, :], gc_ref[rows, :]], axis=1)
        x1 = x_ref[rows, :] + _dot(mix, wo_ref[...])
        x1_out[rows, :] = x1
        h2 = _rms(x1, gffn_ref[...])
        hb = h2.astype(BF16)
        hpa_out[rows, :] = _pack_bf16_pairs(h2[:, :D_MODEL // 2])
        hpb_out[rows, :] = _pack_bf16_pairs(h2[:, D_MODEL // 2:])

        h_lo = (h2 - hb.astype(F32)).astype(BF16)
        both = _dot_nt(wr_stack, hb)
        logits = (both[:N_EXPERTS] + both[N_EXPERTS:]) + _dot_nt(wrh_ref[...], h_lo) + br_ref[...]

        vals, idxs, sels = [], [], []
        cur = logits
        for _ in range(TOP_K):
            m = jnp.max(cur, axis=0, keepdims=True)
            ik = jnp.min(jnp.where(cur == m, eio, N_EXPERTS), axis=0, keepdims=True)
            sel = eio == ik
            cur = jnp.where(sel, -jnp.inf, cur)
            vals.append(m)
            idxs.append(ik)
            sels.append(sel)
        es = [jnp.exp(v - vals[0]) for v in vals]
        tot = es[0] + es[1] + es[2] + es[3]
        gates = jnp.concatenate([e / tot for e in es], axis=0)
        g_hi = gates.astype(BF16)
        g_lo = (gates - g_hi.astype(F32)).astype(BF16)
        gate_out[:, rows] = jnp.concatenate(
            [g_hi, g_lo, jnp.zeros((GATE_ROWS - 2 * TOP_K, sub), BF16)], axis=0)
        idx_out[:, rows] = jnp.concatenate(idxs, axis=0)

        cnt = jnp.zeros((N_EXPERTS, sub), F32)
        for sel in sels:
            cnt = cnt + jnp.where(sel, 1.0, 0.0)
        bases = []
        for c in range(sub // pw):
            cc = cnt[:, c * pw:(c + 1) * pw]
            bases.append(carry + _dot(cc.astype(BF16), before_b))
            carry = carry + jnp.sum(cc, axis=1, keepdims=True)
        base = jnp.concatenate(bases, axis=1)
        pos_out[:, rows] = jnp.concatenate(
            [jnp.sum(jnp.where(sel, base, 0.0), axis=0, keepdims=True) for sel in sels], axis=0).astype(I32)
    carry_ref[...] = carry
    cnt_out[...] = carry


def _dest_kernel(cnt_ref, idx_ref, pos_ref, dest_out, be_out, valid_out, nbu_out):
    idx = idx_ref[...]
    dest = pos_ref[...]
    bidx = lax.broadcasted_iota(I32, be_out.shape, 1)
    be = jnp.zeros(be_out.shape, I32)
    valid = jnp.zeros(be_out.shape, I32)
    run = jnp.int32(0)
    for e in range(N_EXPERTS):
        dest = dest + jnp.where(idx == e, run, 0)
        first_block = run // ROW_BLOCK
        run = run + ((cnt_ref[e] + (ROW_BLOCK - 1)) // ROW_BLOCK) * ROW_BLOCK
        mine = jnp.logical_and(bidx >= first_block, bidx < run // ROW_BLOCK)
        valid = jnp.where(mine, jnp.clip(cnt_ref[e] - (bidx - first_block) * ROW_BLOCK, 0, ROW_BLOCK), valid)
        be = be + jnp.where(bidx >= run // ROW_BLOCK, 1, 0)
    dest_out[...] = dest
    be_out[...] = jnp.minimum(be, N_EXPERTS - 1)
    valid_out[...] = valid
    nbu_out[...] = jnp.zeros(nbu_out.shape, I32) + run // ROW_BLOCK


def _expert_ffn_kernel(be_ref, nbu_ref, nxt_ref, valid_ref, xa_ref, xb_ref, w1_hbm, b1_ref, w2_hbm, b2_ref,
                       y_ref, w1f_ref, w2f_ref, w1s_ref, w2s_ref, sem_ref):
    b = pl.program_id(0)

    def weight_copies(e):
        return (pltpu.make_async_copy(w1_hbm.at[e], w1f_ref, sem_ref.at[0]),
                pltpu.make_async_copy(w2_hbm.at[e], w2f_ref, sem_ref.at[1]))

    def mlp(rows):
        parts = _unpack_bf16_pairs(xa_ref[0:rows, :]) + _unpack_bf16_pairs(xb_ref[0:rows, :])
        xrow = jnp.concatenate([p.astype(BF16) for p in parts], axis=1)
        hm = _dot(xrow, w1s_ref[...]) + b1_ref[0]
        glu = jnp.minimum(hm[:, :D_FF], SWIGLU_LIMIT)
        lin = jnp.clip(hm[:, D_FF:], -SWIGLU_LIMIT, SWIGLU_LIMIT)
        act = glu * jax.nn.sigmoid(SWIGLU_ALPHA * glu) * (lin + 1.0)
        y = _dot(act.astype(BF16), w2s_ref[...]) + b2_ref[0]
        hw = D_MODEL // 4
        y_ref[0:rows, 0:hw] = _pack_bf16_pairs(y[:, :D_MODEL // 2])
        y_ref[0:rows, hw:] = _pack_bf16_pairs(y[:, D_MODEL // 2:])

    @pl.when(b < nbu_ref[0])
    def _():
        e = be_ref[b]

        @pl.when(b == 0)
        def _():
            for cp in weight_copies(e):
                cp.start()

        @pl.when(jnp.logical_or(b == 0, e != be_ref[jnp.maximum(b - 1, 0)]))
        def _():
            for cp in weight_copies(e):
                cp.wait()
            w1s_ref[...] = w1f_ref[...].astype(BF16)
            w2s_ref[...] = w2f_ref[...].astype(BF16)
            nxt = nxt_ref[e]

            @pl.when(nxt >= 0)
            def _():
                for cp in weight_copies(nxt):
                    cp.start()

        n_sub = (valid_ref[b] + (FFN_SUB_ROWS - 1)) // FFN_SUB_ROWS
        for k in range(1, ROW_BLOCK // FFN_SUB_ROWS + 1):
            pl.when(n_sub == k)(functools.partial(mlp, k * FFN_SUB_ROWS))


SC_WINDOW = 128
SC_GATHER_WINDOW = 64
SC_GATHER_WINDOW_COVERED = 32


def _sc_mesh():
    return plsc.VectorSubcoreMesh(core_axis_name="c", subcore_axis_name="s")


def _sc_gather_rows(table, idx, win):
    n = idx.shape[0]
    width = table.shape[1]

    @functools.partial(pl.kernel, out_type=jax.ShapeDtypeStruct((n, width), table.dtype), mesh=_sc_mesh(),
                       scratch_types=[])
    def gather_kernel(t_hbm, i_hbm, o_hbm):
        def body(i_vmem, o_vmem):
            pltpu.sync_copy(t_hbm.at[i_vmem.at[0]], o_vmem)

        pltpu.emit_pipeline(
            body,
            grid=(n // win,),
            in_specs=[pl.BlockSpec((1, win), lambda i: (i, 0))],
            out_specs=[pl.BlockSpec((win, width), lambda i: (i, 0))],
            core_axis_name=("c", "s"),
            dimension_semantics=(pltpu.PARALLEL,),
        )(i_hbm, o_hbm)

    return gather_kernel(table, idx.reshape(n // win, win))


def _sc_scatter_rows(rows, idx_rows, n_out):
    n, width = rows.shape

    @functools.partial(pl.kernel, out_type=jax.ShapeDtypeStruct((n_out, width), rows.dtype), mesh=_sc_mesh(),
                       scratch_types=[])
    def scatter_kernel(r_hbm, *refs):
        i_hbms, o_hbm = refs[:-1], refs[-1]

        def body(r_vmem, *i_vmems):
            for i_vmem in i_vmems:
                pltpu.sync_copy(r_vmem, o_hbm.at[i_vmem.at[0]])

        pltpu.emit_pipeline(
            body,
            grid=(n // SC_WINDOW,),
            in_specs=[pl.BlockSpec((SC_WINDOW, width), lambda i: (i, 0))]
            + [pl.BlockSpec((1, SC_WINDOW), lambda i: (0, i)) for _ in i_hbms],
            out_specs=[],
            core_axis_name=("c", "s"),
            dimension_semantics=(pltpu.PARALLEL,),
        )(r_hbm, *i_hbms)

    return scatter_kernel(rows, *idx_rows)


def _combine_kernel(x1_ref, yg_ref, gate_ref, spread_ref, o_ref):
    q = D_MODEL // 4
    x1 = x1_ref[...]
    acc = [x1[:, j * q:(j + 1) * q] for j in range(4)]
    gfull = lax.dot_general(gate_ref[...], spread_ref[...], (((0,), (0,)), ((), ())), preferred_element_type=F32)
    for k in range(TOP_K):
        parts = _unpack_bf16_pairs(yg_ref[k, :, 0:q]) + _unpack_bf16_pairs(yg_ref[k, :, q:])
        g = jnp.tile(gfull[:, k * LANES:(k + 1) * LANES], (1, q // LANES))
        acc = [a + g * p for a, p in zip(acc, parts)]
    for j in range(4):
        o_ref[:, j * q:(j + 1) * q] = acc[j]


def _combine_into_kernel(x1_ref, yg_ref, gate_ref, spread_ref, prev_ref, o_ref):
    del prev_ref
    _combine_kernel(x1_ref, yg_ref, gate_ref, spread_ref, o_ref)


def _block_diag_ones(width):
    r = jnp.arange(width) // HEAD_GROUP
    return (r[:, None] == r[None, :]).astype(BF16)


def _rope_tables(positions):
    half = ROT_DIM // 2
    inv_freq = ROPE_THETA ** (-jnp.arange(0, ROT_DIM, 2, dtype=F32) / ROT_DIM)
    ang = inv_freq[:, None] * positions.astype(F32).reshape(1, -1)
    cs = jnp.concatenate([jnp.cos(ang), jnp.sin(ang)], axis=0)
    cs_hi = cs.astype(BF16)
    cs_lo = (cs - cs_hi.astype(F32)).astype(BF16)
    lane = jnp.arange(LANES) % HEAD_GROUP
    j = jnp.arange(half)[:, None]
    lo_half = (lane[None, :] == j).astype(F32)
    hi_half = (lane[None, :] == j + half).astype(F32)
    spread = jnp.concatenate([
        jnp.concatenate([lo_half + hi_half, jnp.zeros((half, LANES), F32)], axis=1),
        jnp.concatenate([jnp.zeros((half, LANES), F32), hi_half - lo_half], axis=1)], axis=0)
    return jnp.concatenate([cs_hi, cs_lo], axis=0), jnp.concatenate([spread, spread], axis=0).astype(BF16)


def _full(shape):
    return pl.BlockSpec(shape, lambda *_: (0,) * len(shape))


def kernel(x, mem, positions, g_mix_norm, w_in, g_dq, g_dk, lambda_q1, lambda_k1, lambda_q2, lambda_k2, g_subln, g_sgu, w_spatial, b_spatial, g_mem_norm, w_mem_kv, g_cq, g_ck, w_out, g_ffn_norm, w_router, b_router, w_mlp1, b_mlp1, w_mlp2, b_mlp2):
    B, S, D = x.shape
    M = mem.shape[1]
    tm = TOKEN_TILE
    tr = ROUTER_TILE
    assert D == D_MODEL and S % tm == 0 and S % (ATTN_Q_TILES_PER_STEP * ATTN_TILE) == 0 and g_mix_norm.shape[0] == 1
    last_batches = B // LAST_PART_DIVISOR
    part_batches = (B - last_batches, last_batches) if last_batches > 0 and S % tr == 0 else (B,)

    xf = x.reshape(B * S, D)
    rope_cs, rope_spread = _rope_tables(positions)
    ones256 = _block_diag_ones(MEM_W)
    row = lambda v: v.reshape(1, -1).astype(F32)
    tile_row = lambda v, reps: jnp.tile(v.reshape(1, -1).astype(F32), (1, reps))
    w_in_b, w_out_b, w_kv_b = w_in[0].astype(BF16), w_out[0].astype(BF16), w_mem_kv[0].astype(BF16)
    w_sp_lanes = jnp.transpose(w_spatial[0], (1, 0, 2)).reshape(CHUNK, GMLP_GROUPS * CHUNK)
    b_sp_lanes = jnp.repeat(b_spatial[0].T, HEAD_GROUP, axis=1)
    wr = w_router[0].T.astype(F32)
    wr_hi = wr.astype(BF16)
    wr_lo = (wr - wr_hi.astype(F32)).astype(BF16)
    before = (jnp.arange(PREFIX_CHUNK)[:, None] < jnp.arange(PREFIX_CHUNK)[None, :]).astype(BF16)
    gate_row = jnp.arange(GATE_ROWS)[:, None]
    gate_spread = ((gate_row < 2 * TOP_K) & (gate_row % TOP_K == jnp.arange(TOP_K * LANES)[None, :] // LANES)).astype(BF16)
    b1r = b_mlp1[0].reshape(N_EXPERTS, 1, 2 * D_FF)
    b2r = b_mlp2[0].reshape(N_EXPERTS, 1, D)

    out = None
    b_off = 0
    for Bp in part_batches:
        N = Bp * S
        assert N % tr == 0 and N % tm == 0
        n_assign = N * TOP_K
        n_blocks = -(-n_assign // ROW_BLOCK) + N_EXPERTS
        n_rows = n_blocks * ROW_BLOCK
        nb_pad = -(-n_blocks // LANES) * LANES
        t_off = b_off * S // tm
        r_off = b_off * S // tr

        mb = MEM_BATCHES if Bp % MEM_BATCHES == 0 and b_off % MEM_BATCHES == 0 else 1
        kT, vm = pl.pallas_call(
            _mem_kv_kernel,
            grid=(Bp // mb,),
            in_specs=[pl.BlockSpec((mb, M, D), lambda b: (b + b_off // mb, 0, 0)), _full((1, D)), _full((D, 2 * MEM_W)),
                      _full((1, MEM_W)), _full((MEM_W, MEM_W))],
            out_specs=[pl.BlockSpec((mb, MEM_W, M), lambda b: (b, 0, 0)),
                       pl.BlockSpec((mb, M, MEM_W), lambda b: (b, 0, 0))],
            out_shape=[jax.ShapeDtypeStruct((Bp, MEM_W, M), BF16), jax.ShapeDtypeStruct((Bp, M, MEM_W), BF16)],
            compiler_params=_cparams(("parallel",)),
            name="mem_kv",
        )(mem, row(g_mem_norm[0]), w_kv_b, tile_row(g_ck[0], N_MEM_HEADS), ones256)

        tiles_per_batch = S // tm
        tok = lambda w: pl.BlockSpec((tm, w), lambda i: (i, 0))
        tok_in = lambda w: pl.BlockSpec((tm, w), lambda i: (i + t_off, 0))
        qn, kn, vv, gc = pl.pallas_call(
            _mixer_in_kernel,
            grid=(N // tm,),
            in_specs=[tok_in(D), pl.BlockSpec((ROPE_STACK_ROWS, tm), lambda i: (0, i + t_off)),
                      _full((ROPE_STACK_ROWS, 2 * LANES)),
                      _full((1, D)), _full((D, D_IN_PROJ)),
                      _full((1, DIFF_W)), _full((1, DIFF_W)),
                      _full((1, GMLP_W)), _full((CHUNK, GMLP_GROUPS * CHUNK)), _full((CHUNK, GMLP_W)),
                      _full((1, MEM_W)), _full((MEM_W, MEM_W)),
                      pl.BlockSpec((1, MEM_W, M), lambda i: (i // tiles_per_batch, 0, 0)),
                      pl.BlockSpec((1, M, MEM_W), lambda i: (i // tiles_per_batch, 0, 0))],
            out_specs=[tok(DIFF_W), tok(DIFF_W), tok(DIFF_W), tok(GMLP_W + MEM_W)],
            out_shape=[jax.ShapeDtypeStruct((N, DIFF_W), BF16)] * 3 + [jax.ShapeDtypeStruct((N, GMLP_W + MEM_W), BF16)],
            compiler_params=_cparams(("parallel",)),
            name="mixer_in",
        )(xf, rope_cs, rope_spread, row(g_mix_norm[0]), w_in_b,
          tile_row(g_dq[0], 2 * N_DIFF_HEADS), tile_row(g_dk[0], 2 * N_DIFF_HEADS),
          row(g_sgu[0]), w_sp_lanes, b_sp_lanes, tile_row(g_cq[0], N_MEM_HEADS), ones256, kT, vm)

        tq = ATTN_TILE
        q_rows = ATTN_Q_TILES_PER_STEP * tq
        nqp = S // q_rows
        head_q = pl.BlockSpec((q_rows, DIFF_W), lambda b, i: (b * nqp + i, 0))
        head_kv = pl.BlockSpec((S, DIFF_W), lambda b, i: (b, 0))
        lam_spec = pl.BlockSpec((1, DIFF_HEAD_DIM), lambda b, i: (0, 0))
        nh = N_DIFF_HEADS
        attn = pl.pallas_call(
            _diff_attn_kernel,
            grid=(Bp, nqp),
            in_specs=[head_q, head_kv, head_kv, lam_spec, lam_spec, lam_spec, lam_spec,
                      pl.BlockSpec((DIFF_V_DIM, 1), lambda b, i: (0, 0))],
            out_specs=head_q,
            out_shape=jax.ShapeDtypeStruct((N, DIFF_W), BF16),
            scratch_shapes=[pltpu.VMEM((nh, tq, 2 * tq), F32), pltpu.VMEM((nh, tq, 2 * tq), F32),
                            pltpu.VMEM((nh, 1, 2 * tq), F32), pltpu.VMEM((nh, ATTN_EXT_ROWS, 2 * tq), F32),
                            pltpu.VMEM((nh, ATTN_EXT_ROWS, S), BF16)],
            compiler_params=_cparams(("arbitrary", "arbitrary")),
            name="diff_attn",
        )(qn, kn, vv, row(lambda_q1[0]), row(lambda_k1[0]), row(lambda_q2[0]), row(lambda_k2[0]),
          g_subln[0].reshape(DIFF_V_DIM, 1).astype(F32))

        rtok = lambda w: pl.BlockSpec((tr, w), lambda i: (i, 0))
        rtok_in = lambda w: pl.BlockSpec((tr, w), lambda i: (i + r_off, 0))
        tokT = lambda: pl.BlockSpec((TOP_K, tr), lambda i: (0, i))
        hw = D // 4
        x1, hpa, hpb, idxT, gateT, posT, counts = pl.pallas_call(
            _out_router_kernel,
            grid=(N // tr,),
            in_specs=[rtok_in(D), rtok(DIFF_W), rtok(GMLP_W + MEM_W), _full((D, D)), _full((1, D)),
                      _full((N_EXPERTS, D)), _full((N_EXPERTS, D)), _full((N_EXPERTS, 1)), _full((PREFIX_CHUNK, PREFIX_CHUNK))],
            out_specs=[rtok(D), rtok(hw), rtok(hw), tokT(), pl.BlockSpec((GATE_ROWS, tr), lambda i: (0, i)), tokT(),
                       _full((N_EXPERTS, 1))],
            out_shape=[jax.ShapeDtypeStruct((N, D), F32), jax.ShapeDtypeStruct((N, hw), U32),
                       jax.ShapeDtypeStruct((N, hw), U32), jax.ShapeDtypeStruct((TOP_K, N), I32),
                       jax.ShapeDtypeStruct((GATE_ROWS, N), BF16),
                       jax.ShapeDtypeStruct((TOP_K, N), I32), jax.ShapeDtypeStruct((N_EXPERTS, 1), F32)],
            scratch_shapes=[pltpu.VMEM((N_EXPERTS, 1), F32)],
            compiler_params=_cparams(("arbitrary",)),
            name="out_router",
        )(xf, attn, gc, w_out_b, row(g_ffn_norm[0]), wr_hi, wr_lo, b_router[0].reshape(N_EXPERTS, 1).astype(F32), before)

        destT, block_expert, block_valid, nb_used = pl.pallas_call(
            _dest_kernel,
            grid_spec=pltpu.PrefetchScalarGridSpec(
                num_scalar_prefetch=1,
                grid=(1,),
                in_specs=[pl.BlockSpec((TOP_K, N), lambda i, c: (0, 0)), pl.BlockSpec((TOP_K, N), lambda i, c: (0, 0))],
                out_specs=[pl.BlockSpec((TOP_K, N), lambda i, c: (0, 0)), pl.BlockSpec((1, nb_pad), lambda i, c: (0, 0)),
                           pl.BlockSpec((1, nb_pad), lambda i, c: (0, 0)), pl.BlockSpec((1, LANES), lambda i, c: (0, 0))],
            ),
            out_shape=[jax.ShapeDtypeStruct((TOP_K, N), I32), jax.ShapeDtypeStruct((1, nb_pad), I32),
                       jax.ShapeDtypeStruct((1, nb_pad), I32), jax.ShapeDtypeStruct((1, LANES), I32)],
            compiler_params=_cparams(("arbitrary",)),
            name="dest",
        )(counts.reshape(N_EXPERTS).astype(I32), idxT, posT)

        dest_rows = [destT[k].reshape(1, N) for k in range(TOP_K)]
        xa_buf = _sc_scatter_rows(hpa, dest_rows, n_rows)
        xb_buf = _sc_scatter_rows(hpb, dest_rows, n_rows)

        cnt_i = counts.reshape(N_EXPERTS).astype(I32)
        owner = jnp.where(cnt_i > 0, jnp.arange(N_EXPERTS, dtype=I32), N_EXPERTS)
        later = jnp.concatenate([lax.cummin(owner[::-1])[::-1][1:], jnp.full((1,), N_EXPERTS, I32)])
        next_expert = jnp.where(later < N_EXPERTS, later, -1)
        last = lambda b, be, nbu, nxt, valid: jnp.minimum(b, nbu[0] - 1)
        row_blk = lambda: pl.BlockSpec((ROW_BLOCK, hw), lambda b, be, nbu, nxt, valid: (last(b, be, nbu, nxt, valid), 0))
        y_blk = pl.BlockSpec((ROW_BLOCK, 2 * hw), lambda b, be, nbu, nxt, valid: (last(b, be, nbu, nxt, valid), 0))
        y_buf = pl.pallas_call(
            _expert_ffn_kernel,
            grid_spec=pltpu.PrefetchScalarGridSpec(
                num_scalar_prefetch=4,
                grid=(n_blocks,),
                in_specs=[row_blk(), row_blk(),
                          pl.BlockSpec(memory_space=pl.ANY),
                          pl.BlockSpec((1, 1, 2 * D_FF), lambda b, be, nbu, nxt, valid: (be[b], 0, 0)),
                          pl.BlockSpec(memory_space=pl.ANY),
                          pl.BlockSpec((1, 1, D), lambda b, be, nbu, nxt, valid: (be[b], 0, 0))],
                out_specs=y_blk,
                scratch_shapes=[pltpu.VMEM((D, 2 * D_FF), F32), pltpu.VMEM((D_FF, D), F32),
                                pltpu.VMEM((D, 2 * D_FF), BF16), pltpu.VMEM((D_FF, D), BF16),
                                pltpu.SemaphoreType.DMA((2,))],
            ),
            out_shape=jax.ShapeDtypeStruct((n_rows, 2 * hw), U32),
            compiler_params=_cparams(("arbitrary",)),
            name="expert_ffn",
        )(block_expert[0, :n_blocks], nb_used[0, :1], next_expert, block_valid[0, :n_blocks], xa_buf, xb_buf,
          w_mlp1[0], b1r, w_mlp2[0], b2r)

        last_part = b_off + Bp == B
        win = SC_GATHER_WINDOW if last_part else SC_GATHER_WINDOW_COVERED
        yg = _sc_gather_rows(y_buf, destT.reshape(n_assign), win).reshape(TOP_K, N, 2 * hw)
        tc = COMBINE_TILE
        prev = () if out is None else (out,)
        out = pl.pallas_call(
            _combine_kernel if out is None else _combine_into_kernel,
            grid=(N // tc,),
            in_specs=[pl.BlockSpec((tc, D), lambda i: (i, 0)),
                      pl.BlockSpec((TOP_K, tc, 2 * hw), lambda i: (0, i, 0)),
                      pl.BlockSpec((GATE_ROWS, tc), lambda i: (0, i)),
                      _full((GATE_ROWS, TOP_K * LANES))] + [pl.BlockSpec(memory_space=pl.ANY)] * len(prev),
            out_specs=pl.BlockSpec((tc, D), lambda i: (i + b_off * S // tc, 0)),
            out_shape=jax.ShapeDtypeStruct((B * S, D), F32),
            input_output_aliases={4: 0} if prev else {},
            compiler_params=_cparams(("parallel",)),
            name="combine",
        )(x1, yg, gateT, gate_spread, *prev)
        b_off += Bp
    return out.reshape(B, S, D)
```

```python
import functools

import jax
import jax.numpy as jnp
from jax import lax
from jax.experimental import pallas as pl
from jax.experimental.pallas import tpu as pltpu
from jax.experimental.pallas import tpu_sc as plsc

F32 = jnp.float32
BF16 = jnp.bfloat16
I32 = jnp.int32
U32 = jnp.uint32

D_MODEL = 1024
N_DIFF_HEADS = 4
DIFF_HEAD_DIM = 64
DIFF_V_DIM = 128
DIFF_W = 512
GMLP_W = 256
GMLP_GROUPS = 4
CHUNK = 128
MEM_W = 256
N_MEM_HEADS = 4
HEAD_GROUP = 64
D_IN_PROJ = 2304
ROPE_THETA = 500000.0
ROT_DIM = 16
ROPE_STACK_ROWS = 2 * ROT_DIM
N_EXPERTS = 32
TOP_K = 4
D_FF = 1024
SWIGLU_LIMIT = 7.0
SWIGLU_ALPHA = 1.702
EPS = 1e-6
LAMBDA_INIT = 0.8 - 0.6

LANES = 128
ROW_BLOCK = 1024
FFN_SUB_ROWS = 256
TOKEN_TILE = 1024
COMBINE_TILE = 1024
MEM_BATCHES = 4
LAST_PART_DIVISOR = 4
ROUTER_TILE = 1024
PREFIX_CHUNK = 256
GATE_ROWS = 16
ATTN_TILE = 256
ATTN_Q_TILES_PER_STEP = 8
ATTN_EXT_ROWS = 128 + 16
VMEM_LIMIT = 56 * 1024 * 1024
NEG_BIG = -1e30
LOG2_E = 1.4426950408889634


def _cparams(sem):
    return pltpu.CompilerParams(dimension_semantics=sem, vmem_limit_bytes=VMEM_LIMIT)


def _dot(a, b):
    return jnp.dot(a, b, preferred_element_type=F32)


def _dot_nt(a, b):
    return lax.dot_general(a, b, (((1,), (1,)), ((), ())), preferred_element_type=F32)


def _rms(x, gain):
    ms = jnp.mean(x * x, axis=-1, keepdims=True)
    return x * lax.rsqrt(ms + EPS) * gain


def _group_rms(t, ones_bd, gain):
    w = ones_bd.shape[0]
    chunks = []
    for j in range(t.shape[1] // w):
        c = t[:, j * w:(j + 1) * w]
        ss = _dot((c * c).astype(BF16), ones_bd)
        chunks.append(c * lax.rsqrt(ss * (1.0 / HEAD_GROUP) + EPS))
    return (chunks[0] if len(chunks) == 1 else jnp.concatenate(chunks, axis=1)) * gain


def _pack_bf16_pairs(v):
    w = v.shape[1] // 2
    bits = lax.bitcast_convert_type(v.astype(BF16).astype(F32), U32)
    return (bits[:, :w] & jnp.uint32(0xFFFF0000)) | (bits[:, w:] >> jnp.uint32(16))


def _unpack_bf16_pairs(words):
    hi = lax.bitcast_convert_type(words & jnp.uint32(0xFFFF0000), F32)
    lo = lax.bitcast_convert_type(words << jnp.uint32(16), F32)
    return hi, lo


def _mem_kv_kernel(mem_ref, gmem_ref, wkv_ref, gck_ref, ones_ref, kT_ref, v_ref):
    for j in range(mem_ref.shape[0]):
        m = _rms(mem_ref[j], gmem_ref[...]).astype(BF16)
        kv = _dot(m, wkv_ref[...])
        k = _group_rms(kv[:, :MEM_W], ones_ref[...], gck_ref[...])
        kT_ref[j] = k.T.astype(BF16)
        v_ref[j] = kv[:, MEM_W:].astype(BF16)


def _gelu_tanh(x):
    return 0.5 * x * (1.0 + jnp.tanh(0.7978845608028654 * (x + 0.044715 * (x * x * x))))


def _mixer_in_kernel(x_ref, cs_ref, spread_ref, gmix_ref, win_ref, gq_ref, gk_ref,
                     gsgu_ref, wsp_ref, bsp_ref, gcq_ref, ones256_ref, kT_ref, vm_ref,
                     q_out, k_out, v_out, gc_out):
    tm = x_ref.shape[0]
    hb = _rms(x_ref[...], gmix_ref[...]).astype(BF16)

    def proj(lo, hi):
        return _dot(hb, win_ref[:, lo:hi])

    lane = lax.broadcasted_iota(I32, (tm, LANES), 1)
    first_half = (lane % HEAD_GROUP) < (ROT_DIM // 2)
    tab = lax.dot_general(cs_ref[...], spread_ref[...], (((0,), (0,)), ((), ())), preferred_element_type=F32)
    cosb = tab[:, :LANES] + jnp.where((lane % HEAD_GROUP) >= ROT_DIM, 1.0, 0.0)
    sinb = tab[:, LANES:]

    def norm_rope(t, gain, out_ref):
        tn = _group_rms(t, ones256_ref[...], gain)
        for j in range(DIFF_W // LANES):
            c = tn[:, j * LANES:(j + 1) * LANES]
            partner = jnp.where(first_half, pltpu.roll(c, LANES - ROT_DIM // 2, 1), pltpu.roll(c, ROT_DIM // 2, 1))
            out_ref[:, j * LANES:(j + 1) * LANES] = (c * cosb + partner * sinb).astype(BF16)

    norm_rope(proj(0, DIFF_W), gq_ref[...] * (DIFF_HEAD_DIM ** -0.5 * LOG2_E), q_out)
    norm_rope(proj(DIFF_W, 2 * DIFF_W), gk_ref[...], k_out)
    v_out[...] = proj(2 * DIFF_W, 3 * DIFF_W).astype(BF16)

    z = _gelu_tanh(proj(3 * DIFF_W, 3 * DIFF_W + 2 * GMLP_W))
    u = z[:, :GMLP_W]
    vg = z[:, GMLP_W:]
    vc = vg - jnp.mean(vg, axis=-1, keepdims=True)
    vgn = (vc * lax.rsqrt(jnp.mean(vc * vc, axis=-1, keepdims=True) + EPS) * gsgu_ref[...]).astype(BF16)
    wrow = lax.broadcasted_iota(I32, (CHUNK, GMLP_GROUPS * CHUNK), 0)
    wcol = lax.broadcasted_iota(I32, (CHUNK, GMLP_GROUPS * CHUNK), 1) % CHUNK
    w_causal = jnp.where(wcol <= wrow, wsp_ref[...], 0.0).astype(BF16)
    grp = lax.broadcasted_iota(I32, (CHUNK, GMLP_W), 1) // HEAD_GROUP
    zero_b = jnp.zeros((CHUNK, GMLP_W), BF16)
    for r in range(tm // CHUNK):
        vchunk = vgn[r * CHUNK:(r + 1) * CHUNK, :]
        v_bd = jnp.concatenate([jnp.where(grp == g, vchunk, zero_b) for g in range(GMLP_GROUPS)], axis=0)
        mixed = _dot(w_causal, v_bd) + bsp_ref[...]
        gc_out[r * CHUNK:(r + 1) * CHUNK, 0:GMLP_W] = (u[r * CHUNK:(r + 1) * CHUNK, :] * mixed).astype(BF16)

    pc = proj(3 * DIFF_W + 2 * GMLP_W, D_IN_PROJ)
    qc = _group_rms(pc, ones256_ref[...], gcq_ref[...] * (HEAD_GROUP ** -0.5)).astype(BF16)
    hgrp = lax.broadcasted_iota(I32, (tm, MEM_W), 1) // HEAD_GROUP
    zero_q = jnp.zeros((tm, MEM_W), BF16)
    q_st = jnp.concatenate([jnp.where(hgrp == h, qc, zero_q) for h in range(N_MEM_HEADS)], axis=0)
    s = _dot(q_st, kT_ref[0])
    p = jnp.exp(s - jnp.max(s, axis=-1, keepdims=True))
    o = _dot(p.astype(BF16), vm_ref[0]) / jnp.sum(p, axis=-1, keepdims=True)
    c = jnp.zeros((tm, MEM_W), F32)
    for h in range(N_MEM_HEADS):
        c = c + jnp.where(hgrp == h, o[h * tm:(h + 1) * tm, :], 0.0)
    gc_out[:, GMLP_W:GMLP_W + MEM_W] = c.astype(BF16)


def _diff_attn_kernel(q_ref, k_ref, v_ref, lq1_ref, lk1_ref, lq2_ref, lk2_ref, gsub_ref, o_ref,
                      s0_ref, s1_ref, m_ref, acc_ref, vT_ref):
    tq = ATTN_TILE
    tiles_per_step = q_ref.shape[0] // tq
    assert tiles_per_step % 2 == 0
    seq = k_ref.shape[0]
    step = pl.program_id(1)
    feat = lax.broadcasted_iota(I32, (DIFF_V_DIM, tq), 0)
    heads = range(N_DIFF_HEADS)
    hl = lambda h: slice(h * DIFF_V_DIM, (h + 1) * DIFF_V_DIM)
    ext_rows = vT_ref.shape[1]

    @pl.when(step == 0)
    def _():
        ones_row = jnp.where(lax.broadcasted_iota(I32, (ext_rows - DIFF_V_DIM, seq), 0) == 0, 1.0, 0.0).astype(BF16)
        for h in heads:
            for c in range(seq // tq):
                vT_ref[h, 0:DIFF_V_DIM, c * tq:(c + 1) * tq] = v_ref[c * tq:(c + 1) * tq, hl(h)].T
            vT_ref[h, DIFF_V_DIM:ext_rows, :] = ones_row

    @pl.when(jnp.logical_and(pl.program_id(0) == 0, step == 0))
    def _():
        acc_ref[...] = jnp.zeros(acc_ref.shape, F32)

    lam = (jnp.exp(jnp.sum(lq1_ref[...] * lk1_ref[...], axis=-1, keepdims=True))
           - jnp.exp(jnp.sum(lq2_ref[...] * lk2_ref[...], axis=-1, keepdims=True)) + LAMBDA_INIT)

    def q_tile(j, first_ref, other_ref):
        i = tiles_per_step * step + j
        q_rows = slice(j * tq, (j + 1) * tq)

        def stacked_qT(h):
            qT = q_ref[q_rows, hl(h)].T
            zero = jnp.zeros_like(qT)
            return jnp.concatenate(
                [jnp.where(feat < DIFF_HEAD_DIM, qT, zero), jnp.where(feat >= DIFF_HEAD_DIM, qT, zero)], axis=1)

        qsT = [stacked_qT(h) for h in heads]

        def scores(t, s_ref):
            rows = pl.ds(pl.multiple_of(t * tq, tq), tq)
            for h in heads:
                s_ref[h] = _dot(k_ref[rows, hl(h)], qsT[h])

        def update(t, s_ref, causal):
            cols = pl.ds(pl.multiple_of(t * tq, tq), tq)
            for h in heads:
                s = s_ref[h]
                if causal:
                    key = lax.broadcasted_iota(I32, (tq, 2 * tq), 0)
                    qry = lax.broadcasted_iota(I32, (tq, 2 * tq), 1) % tq
                    s = jnp.where(key <= qry, s, NEG_BIG)
                m = m_ref[h]
                m_new = jnp.maximum(m, jnp.max(s, axis=0, keepdims=True))
                alpha = jnp.exp2(m - m_new)
                p = jnp.exp2(s - m_new)
                m_ref[h] = m_new
                acc_ref[h] = alpha * acc_ref[h] + _dot(vT_ref[h, :, cols], p.astype(BF16))

        m_ref[...] = jnp.full(m_ref.shape, NEG_BIG, F32)
        scores(0, first_ref)

        def two_tiles(pidx, carry):
            t = 2 * pidx
            scores(t + 1, other_ref)
            update(t, first_ref, False)
            scores(t + 2, first_ref)
            update(t + 1, other_ref, False)
            return carry

        lax.fori_loop(0, (tiles_per_step // 2) * step + j // 2, two_tiles, 0)

        if j % 2 == 0:
            update(i, first_ref, True)
        else:
            scores(i, other_ref)
            update(i - 1, first_ref, False)
            update(i, other_ref, True)

        for h in heads:
            on = acc_ref[h, 0:DIFF_V_DIM, :] * (1.0 / acc_ref[h, DIFF_V_DIM:DIFF_V_DIM + 1, :])
            o = on[:, :tq] - lam * on[:, tq:]
            ms = jnp.mean(o * o, axis=0, keepdims=True)
            o = o * lax.rsqrt(ms + EPS) * gsub_ref[...] * (1.0 - LAMBDA_INIT)
            o_ref[q_rows, hl(h)] = o.T.astype(BF16)

    first_ref, other_ref = s0_ref, s1_ref
    for j in range(tiles_per_step):
        q_tile(j, first_ref, other_ref)
        last_read = first_ref if j % 2 == 0 else other_ref
        first_ref, other_ref = (other_ref, first_ref) if last_read is first_ref else (first_ref, other_ref)


def _out_router_kernel(x_ref, a_ref, gc_ref, wo_ref, gffn_ref, wrh_ref, wrl_ref, br_ref, before_ref,
                       x1_out, hpa_out, hpb_out, idx_out, gate_out, pos_out, cnt_out, carry_ref):
    tm = x_ref.shape[0]

    @pl.when(pl.program_id(0) == 0)
    def _():
        carry_ref[...] = jnp.zeros_like(carry_ref)

    sub = tm
    pw = before_ref.shape[0]
    wr_stack = jnp.concatenate([wrh_ref[...], wrl_ref[...]], axis=0)
    eio = lax.broadcasted_iota(I32, (N_EXPERTS, sub), 0)
    before_b = before_ref[...]
    carry = carry_ref[...]
    for r in range(tm // sub):
        rows = slice(r * sub, (r + 1) * sub)
        mix = jnp.concatenate([a_ref[rows, :], gc_ref[rows, :]], axis=1)
        x1 = x_ref[rows, :] + _dot(mix, wo_ref[...])
        x1_out[rows, :] = x1
        h2 = _rms(x1, gffn_ref[...])
        hb = h2.astype(BF16)
        hpa_out[rows, :] = _pack_bf16_pairs(h2[:, :D_MODEL // 2])
        hpb_out[rows, :] = _pack_bf16_pairs(h2[:, D_MODEL // 2:])

        h_lo = (h2 - hb.astype(F32)).astype(BF16)
        both = _dot_nt(wr_stack, hb)
        logits = (both[:N_EXPERTS] + both[N_EXPERTS:]) + _dot_nt(wrh_ref[...], h_lo) + br_ref[...]

        vals, idxs, sels = [], [], []
        cur = logits
        for _ in range(TOP_K):
            m = jnp.max(cur, axis=0, keepdims=True)
            ik = jnp.min(jnp.where(cur == m, eio, N_EXPERTS), axis=0, keepdims=True)
            sel = eio == ik
            cur = jnp.where(sel, -jnp.inf, cur)
            vals.append(m)
            idxs.append(ik)
            sels.append(sel)
        es = [jnp.exp(v - vals[0]) for v in vals]
        tot = es[0] + es[1] + es[2] + es[3]
        gates = jnp.concatenate([e / tot for e in es], axis=0)
        g_hi = gates.astype(BF16)
        g_lo = (gates - g_hi.astype(F32)).astype(BF16)
        gate_out[:, rows] = jnp.concatenate(
            [g_hi, g_lo, jnp.zeros((GATE_ROWS - 2 * TOP_K, sub), BF16)], axis=0)
        idx_out[:, rows] = jnp.concatenate(idxs, axis=0)

        cnt = jnp.zeros((N_EXPERTS, sub), F32)
        for sel in sels:
            cnt = cnt + jnp.where(sel, 1.0, 0.0)
        bases = []
        for c in range(sub // pw):
            cc = cnt[:, c * pw:(c + 1) * pw]
            bases.append(carry + _dot(cc.astype(BF16), before_b))
            carry = carry + jnp.sum(cc, axis=1, keepdims=True)
        base = jnp.concatenate(bases, axis=1)
        pos_out[:, rows] = jnp.concatenate(
            [jnp.sum(jnp.where(sel, base, 0.0), axis=0, keepdims=True) for sel in sels], axis=0).astype(I32)
    carry_ref[...] = carry
    cnt_out[...] = carry


def _dest_kernel(cnt_ref, idx_ref, pos_ref, dest_out, be_out, valid_out, nbu_out):
    idx = idx_ref[...]
    dest = pos_ref[...]
    bidx = lax.broadcasted_iota(I32, be_out.shape, 1)
    be = jnp.zeros(be_out.shape, I32)
    valid = jnp.zeros(be_out.shape, I32)
    run = jnp.int32(0)
    for e in range(N_EXPERTS):
        dest = dest + jnp.where(idx == e, run, 0)
        first_block = run // ROW_BLOCK
        run = run + ((cnt_ref[e] + (ROW_BLOCK - 1)) // ROW_BLOCK) * ROW_BLOCK
        mine = jnp.logical_and(bidx >= first_block, bidx < run // ROW_BLOCK)
        valid = jnp.where(mine, jnp.clip(cnt_ref[e] - (bidx - first_block) * ROW_BLOCK, 0, ROW_BLOCK), valid)
        be = be + jnp.where(bidx >= run // ROW_BLOCK, 1, 0)
    dest_out[...] = dest
    be_out[...] = jnp.minimum(be, N_EXPERTS - 1)
    valid_out[...] = valid
    nbu_out[...] = jnp.zeros(nbu_out.shape, I32) + run // ROW_BLOCK


def _expert_ffn_kernel(be_ref, nbu_ref, nxt_ref, valid_ref, xa_ref, xb_ref, w1_hbm, b1_ref, w2_hbm, b2_ref,
                       y_ref, w1f_ref, w2f_ref, w1s_ref, w2s_ref, sem_ref):
    b = pl.program_id(0)

    def weight_copies(e):
        return (pltpu.make_async_copy(w1_hbm.at[e], w1f_ref, sem_ref.at[0]),
                pltpu.make_async_copy(w2_hbm.at[e], w2f_ref, sem_ref.at[1]))

    def mlp(rows):
        parts = _unpack_bf16_pairs(xa_ref[0:rows, :]) + _unpack_bf16_pairs(xb_ref[0:rows, :])
        xrow = jnp.concatenate([p.astype(BF16) for p in parts], axis=1)
        hm = _dot(xrow, w1s_ref[...]) + b1_ref[0]
        glu = jnp.minimum(hm[:, :D_FF], SWIGLU_LIMIT)
        lin = jnp.clip(hm[:, D_FF:], -SWIGLU_LIMIT, SWIGLU_LIMIT)
        act = glu * jax.nn.sigmoid(SWIGLU_ALPHA * glu) * (lin + 1.0)
        y = _dot(act.astype(BF16), w2s_ref[...]) + b2_ref[0]
        hw = D_MODEL // 4
        y_ref[0:rows, 0:hw] = _pack_bf16_pairs(y[:, :D_MODEL // 2])
        y_ref[0:rows, hw:] = _pack_bf16_pairs(y[:, D_MODEL // 2:])

    @pl.when(b < nbu_ref[0])
    def _():
        e = be_ref[b]

        @pl.when(b == 0)
        def _():
            for cp in weight_copies(e):
                cp.start()

        @pl.when(jnp.logical_or(b == 0, e != be_ref[jnp.maximum(b - 1, 0)]))
        def _():
            for cp in weight_copies(e):
                cp.wait()
            w1s_ref[...] = w1f_ref[...].astype(BF16)
            w2s_ref[...] = w2f_ref[...].astype(BF16)
            nxt = nxt_ref[e]

            @pl.when(nxt >= 0)
            def _():
                for cp in weight_copies(nxt):
                    cp.start()

        n_sub = (valid_ref[b] + (FFN_SUB_ROWS - 1)) // FFN_SUB_ROWS
        for k in range(1, ROW_BLOCK // FFN_SUB_ROWS + 1):
            pl.when(n_sub == k)(functools.partial(mlp, k * FFN_SUB_ROWS))


SC_WINDOW = 128
SC_GATHER_WINDOW = 64
SC_GATHER_WINDOW_COVERED = 32


def _sc_mesh():
    return plsc.VectorSubcoreMesh(core_axis_name="c", subcore_axis_name="s")


def _sc_gather_rows(table, idx, win):
    n = idx.shape[0]
    width = table.shape[1]

    @functools.partial(pl.kernel, out_type=jax.ShapeDtypeStruct((n, width), table.dtype), mesh=_sc_mesh(),
                       scratch_types=[])
    def gather_kernel(t_hbm, i_hbm, o_hbm):
        def body(i_vmem, o_vmem):
            pltpu.sync_copy(t_hbm.at[i_vmem.at[0]], o_vmem)

        pltpu.emit_pipeline(
            body,
            grid=(n // win,),
            in_specs=[pl.BlockSpec((1, win), lambda i: (i, 0))],
            out_specs=[pl.BlockSpec((win, width), lambda i: (i, 0))],
            core_axis_name=("c", "s"),
            dimension_semantics=(pltpu.PARALLEL,),
        )(i_hbm, o_hbm)

    return gather_kernel(table, idx.reshape(n // win, win))


def _sc_scatter_rows(rows, idx_rows, n_out):
    n, width = rows.shape

    @functools.partial(pl.kernel, out_type=jax.ShapeDtypeStruct((n_out, width), rows.dtype), mesh=_sc_mesh(),
                       scratch_types=[])
    def scatter_kernel(r_hbm, *refs):
        i_hbms, o_hbm = refs[:-1], refs[-1]

        def body(r_vmem, *i_vmems):
            for i_vmem in i_vmems:
                pltpu.sync_copy(r_vmem, o_hbm.at[i_vmem.at[0]])

        pltpu.emit_pipeline(
            body,
            grid=(n // SC_WINDOW,),
            in_specs=[pl.BlockSpec((SC_WINDOW, width), lambda i: (i, 0))]
            + [pl.BlockSpec((1, SC_WINDOW), lambda i: (0, i)) for _ in i_hbms],
            out_specs=[],
            core_axis_name=("c", "s"),
            dimension_semantics=(pltpu.PARALLEL,),
        )(r_hbm, *i_hbms)

    return scatter_kernel(rows, *idx_rows)


def _combine_kernel(x1_ref, yg_ref, gate_ref, spread_ref, o_ref):
    q = D_MODEL // 4
    x1 = x1_ref[...]
    acc = [x1[:, j * q:(j + 1) * q] for j in range(4)]
    gfull = lax.dot_general(gate_ref[...], spread_ref[...], (((0,), (0,)), ((), ())), preferred_element_type=F32)
    for k in range(TOP_K):
        parts = _unpack_bf16_pairs(yg_ref[k, :, 0:q]) + _unpack_bf16_pairs(yg_ref[k, :, q:])
        g = jnp.tile(gfull[:, k * LANES:(k + 1) * LANES], (1, q // LANES))
        acc = [a + g * p for a, p in zip(acc, parts)]
    for j in range(4):
        o_ref[:, j * q:(j + 1) * q] = acc[j]


def _combine_into_kernel(x1_ref, yg_ref, gate_ref, spread_ref, prev_ref, o_ref):
    del prev_ref
    _combine_kernel(x1_ref, yg_ref, gate_ref, spread_ref, o_ref)


def _block_diag_ones(width):
    r = jnp.arange(width) // HEAD_GROUP
    return (r[:, None] == r[None, :]).astype(BF16)


def _rope_tables(positions):
    half = ROT_DIM // 2
    inv_freq = ROPE_THETA ** (-jnp.arange(0, ROT_DIM, 2, dtype=F32) / ROT_DIM)
    ang = inv_freq[:, None] * positions.astype(F32).reshape(1, -1)
    cs = jnp.concatenate([jnp.cos(ang), jnp.sin(ang)], axis=0)
    cs_hi = cs.astype(BF16)
    cs_lo = (cs - cs_hi.astype(F32)).astype(BF16)
    lane = jnp.arange(LANES) % HEAD_GROUP
    j = jnp.arange(half)[:, None]
    lo_half = (lane[None, :] == j).astype(F32)
    hi_half = (lane[None, :] == j + half).astype(F32)
    spread = jnp.concatenate([
        jnp.concatenate([lo_half + hi_half, jnp.zeros((half, LANES), F32)], axis=1),
        jnp.concatenate([jnp.zeros((half, LANES), F32), hi_half - lo_half], axis=1)], axis=0)
    return jnp.concatenate([cs_hi, cs_lo], axis=0), jnp.concatenate([spread, spread], axis=0).astype(BF16)


def _full(shape):
    return pl.BlockSpec(shape, lambda *_: (0,) * len(shape))


def kernel(x, mem, positions, g_mix_norm, w_in, g_dq, g_dk, lambda_q1, lambda_k1, lambda_q2, lambda_k2, g_subln, g_sgu, w_spatial, b_spatial, g_mem_norm, w_mem_kv, g_cq, g_ck, w_out, g_ffn_norm, w_router, b_router, w_mlp1, b_mlp1, w_mlp2, b_mlp2):
    B, S, D = x.shape
    M = mem.shape[1]
    tm = TOKEN_TILE
    tr = ROUTER_TILE
    assert D == D_MODEL and S % tm == 0 and S % (ATTN_Q_TILES_PER_STEP * ATTN_TILE) == 0 and g_mix_norm.shape[0] == 1
    last_batches = B // LAST_PART_DIVISOR
    part_batches = (B - last_batches, last_batches) if last_batches > 0 and S % tr == 0 else (B,)

    xf = x.reshape(B * S, D)
    rope_cs, rope_spread = _rope_tables(positions)
    ones256 = _block_diag_ones(MEM_W)
    row = lambda v: v.reshape(1, -1).astype(F32)
    tile_row = lambda v, reps: jnp.tile(v.reshape(1, -1).astype(F32), (1, reps))
    w_in_b, w_out_b, w_kv_b = w_in[0].astype(BF16), w_out[0].astype(BF16), w_mem_kv[0].astype(BF16)
    w_sp_lanes = jnp.transpose(w_spatial[0], (1, 0, 2)).reshape(CHUNK, GMLP_GROUPS * CHUNK)
    b_sp_lanes = jnp.repeat(b_spatial[0].T, HEAD_GROUP, axis=1)
    wr = w_router[0].T.astype(F32)
    wr_hi = wr.astype(BF16)
    wr_lo = (wr - wr_hi.astype(F32)).astype(BF16)
    before = (jnp.arange(PREFIX_CHUNK)[:, None] < jnp.arange(PREFIX_CHUNK)[None, :]).astype(BF16)
    gate_row = jnp.arange(GATE_ROWS)[:, None]
    gate_spread = ((gate_row < 2 * TOP_K) & (gate_row % TOP_K == jnp.arange(TOP_K * LANES)[None, :] // LANES)).astype(BF16)
    b1r = b_mlp1[0].reshape(N_EXPERTS, 1, 2 * D_FF)
    b2r = b_mlp2[0].reshape(N_EXPERTS, 1, D)

    out = None
    b_off = 0
    for Bp in part_batches:
        N = Bp * S
        assert N % tr == 0 and N % tm == 0
        n_assign = N * TOP_K
        n_blocks = -(-n_assign // ROW_BLOCK) + N_EXPERTS
        n_rows = n_blocks * ROW_BLOCK
        nb_pad = -(-n_blocks // LANES) * LANES
        t_off = b_off * S // tm
        r_off = b_off * S // tr

        mb = MEM_BATCHES if Bp % MEM_BATCHES == 0 and b_off % MEM_BATCHES == 0 else 1
        kT, vm = pl.pallas_call(
            _mem_kv_kernel,
            grid=(Bp // mb,),
            in_specs=[pl.BlockSpec((mb, M, D), lambda b: (b + b_off // mb, 0, 0)), _full((1, D)), _full((D, 2 * MEM_W)),
                      _full((1, MEM_W)), _full((MEM_W, MEM_W))],
            out_specs=[pl.BlockSpec((mb, MEM_W, M), lambda b: (b, 0, 0)),
                       pl.BlockSpec((mb, M, MEM_W), lambda b: (b, 0, 0))],
            out_shape=[jax.ShapeDtypeStruct((Bp, MEM_W, M), BF16), jax.ShapeDtypeStruct((Bp, M, MEM_W), BF16)],
            compiler_params=_cparams(("parallel",)),
            name="mem_kv",
        )(mem, row(g_mem_norm[0]), w_kv_b, tile_row(g_ck[0], N_MEM_HEADS), ones256)

        tiles_per_batch = S // tm
        tok = lambda w: pl.BlockSpec((tm, w), lambda i: (i, 0))
        tok_in = lambda w: pl.BlockSpec((tm, w), lambda i: (i + t_off, 0))
        qn, kn, vv, gc = pl.pallas_call(
            _mixer_in_kernel,
            grid=(N // tm,),
            in_specs=[tok_in(D), pl.BlockSpec((ROPE_STACK_ROWS, tm), lambda i: (0, i + t_off)),
                      _full((ROPE_STACK_ROWS, 2 * LANES)),
                      _full((1, D)), _full((D, D_IN_PROJ)),
                      _full((1, DIFF_W)), _full((1, DIFF_W)),
                      _full((1, GMLP_W)), _full((CHUNK, GMLP_GROUPS * CHUNK)), _full((CHUNK, GMLP_W)),
                      _full((1, MEM_W)), _full((MEM_W, MEM_W)),
                      pl.BlockSpec((1, MEM_W, M), lambda i: (i // tiles_per_batch, 0, 0)),
                      pl.BlockSpec((1, M, MEM_W), lambda i: (i // tiles_per_batch, 0, 0))],
            out_specs=[tok(DIFF_W), tok(DIFF_W), tok(DIFF_W), tok(GMLP_W + MEM_W)],
            out_shape=[jax.ShapeDtypeStruct((N, DIFF_W), BF16)] * 3 + [jax.ShapeDtypeStruct((N, GMLP_W + MEM_W), BF16)],
            compiler_params=_cparams(("parallel",)),
            name="mixer_in",
        )(xf, rope_cs, rope_spread, row(g_mix_norm[0]), w_in_b,
          tile_row(g_dq[0], 2 * N_DIFF_HEADS), tile_row(g_dk[0], 2 * N_DIFF_HEADS),
          row(g_sgu[0]), w_sp_lanes, b_sp_lanes, tile_row(g_cq[0], N_MEM_HEADS), ones256, kT, vm)

        tq = ATTN_TILE
        q_rows = ATTN_Q_TILES_PER_STEP * tq
        nqp = S // q_rows
        head_q = pl.BlockSpec((q_rows, DIFF_W), lambda b, i: (b * nqp + i, 0))
        head_kv = pl.BlockSpec((S, DIFF_W), lambda b, i: (b, 0))
        lam_spec = pl.BlockSpec((1, DIFF_HEAD_DIM), lambda b, i: (0, 0))
        nh = N_DIFF_HEADS
        attn = pl.pallas_call(
            _diff_attn_kernel,
            grid=(Bp, nqp),
            in_specs=[head_q, head_kv, head_kv, lam_spec, lam_spec, lam_spec, lam_spec,
                      pl.BlockSpec((DIFF_V_DIM, 1), lambda b, i: (0, 0))],
            out_specs=head_q,
            out_shape=jax.ShapeDtypeStruct((N, DIFF_W), BF16),
            scratch_shapes=[pltpu.VMEM((nh, tq, 2 * tq), F32), pltpu.VMEM((nh, tq, 2 * tq), F32),
                            pltpu.VMEM((nh, 1, 2 * tq), F32), pltpu.VMEM((nh, ATTN_EXT_ROWS, 2 * tq), F32),
                            pltpu.VMEM((nh, ATTN_EXT_ROWS, S), BF16)],
            compiler_params=_cparams(("arbitrary", "arbitrary")),
            name="diff_attn",
        )(qn, kn, vv, row(lambda_q1[0]), row(lambda_k1[0]), row(lambda_q2[0]), row(lambda_k2[0]),
          g_subln[0].reshape(DIFF_V_DIM, 1).astype(F32))

        rtok = lambda w: pl.BlockSpec((tr, w), lambda i: (i, 0))
        rtok_in = lambda w: pl.BlockSpec((tr, w), lambda i: (i + r_off, 0))
        tokT = lambda: pl.BlockSpec((TOP_K, tr), lambda i: (0, i))
        hw = D // 4
        x1, hpa, hpb, idxT, gateT, posT, counts = pl.pallas_call(
            _out_router_kernel,
            grid=(N // tr,),
            in_specs=[rtok_in(D), rtok(DIFF_W), rtok(GMLP_W + MEM_W), _full((D, D)), _full((1, D)),
                      _full((N_EXPERTS, D)), _full((N_EXPERTS, D)), _full((N_EXPERTS, 1)), _full((PREFIX_CHUNK, PREFIX_CHUNK))],
            out_specs=[rtok(D), rtok(hw), rtok(hw), tokT(), pl.BlockSpec((GATE_ROWS, tr), lambda i: (0, i)), tokT(),
                       _full((N_EXPERTS, 1))],
            out_shape=[jax.ShapeDtypeStruct((N, D), F32), jax.ShapeDtypeStruct((N, hw), U32),
                       jax.ShapeDtypeStruct((N, hw), U32), jax.ShapeDtypeStruct((TOP_K, N), I32),
                       jax.ShapeDtypeStruct((GATE_ROWS, N), BF16),
                       jax.ShapeDtypeStruct((TOP_K, N), I32), jax.ShapeDtypeStruct((N_EXPERTS, 1), F32)],
            scratch_shapes=[pltpu.VMEM((N_EXPERTS, 1), F32)],
            compiler_params=_cparams(("arbitrary",)),
            name="out_router",
        )(xf, attn, gc, w_out_b, row(g_ffn_norm[0]), wr_hi, wr_lo, b_router[0].reshape(N_EXPERTS, 1).astype(F32), before)

        destT, block_expert, block_valid, nb_used = pl.pallas_call(
            _dest_kernel,
            grid_spec=pltpu.PrefetchScalarGridSpec(
                num_scalar_prefetch=1,
                grid=(1,),
                in_specs=[pl.BlockSpec((TOP_K, N), lambda i, c: (0, 0)), pl.BlockSpec((TOP_K, N), lambda i, c: (0, 0))],
                out_specs=[pl.BlockSpec((TOP_K, N), lambda i, c: (0, 0)), pl.BlockSpec((1, nb_pad), lambda i, c: (0, 0)),
                           pl.BlockSpec((1, nb_pad), lambda i, c: (0, 0)), pl.BlockSpec((1, LANES), lambda i, c: (0, 0))],
            ),
            out_shape=[jax.ShapeDtypeStruct((TOP_K, N), I32), jax.ShapeDtypeStruct((1, nb_pad), I32),
                       jax.ShapeDtypeStruct((1, nb_pad), I32), jax.ShapeDtypeStruct((1, LANES), I32)],
            compiler_params=_cparams(("arbitrary",)),
            name="dest",
        )(counts.reshape(N_EXPERTS).astype(I32), idxT, posT)

        dest_rows = [destT[k].reshape(1, N) for k in range(TOP_K)]
        xa_buf = _sc_scatter_rows(hpa, dest_rows, n_rows)
        xb_buf = _sc_scatter_rows(hpb, dest_rows, n_rows)

        cnt_i = counts.reshape(N_EXPERTS).astype(I32)
        owner = jnp.where(cnt_i > 0, jnp.arange(N_EXPERTS, dtype=I32), N_EXPERTS)
        later = jnp.concatenate([lax.cummin(owner[::-1])[::-1][1:], jnp.full((1,), N_EXPERTS, I32)])
        next_expert = jnp.where(later < N_EXPERTS, later, -1)
        last = lambda b, be, nbu, nxt, valid: jnp.minimum(b, nbu[0] - 1)
        row_blk = lambda: pl.BlockSpec((ROW_BLOCK, hw), lambda b, be, nbu, nxt, valid: (last(b, be, nbu, nxt, valid), 0))
        y_blk = pl.BlockSpec((ROW_BLOCK, 2 * hw), lambda b, be, nbu, nxt, valid: (last(b, be, nbu, nxt, valid), 0))
        y_buf = pl.pallas_call(
            _expert_ffn_kernel,
            grid_spec=pltpu.PrefetchScalarGridSpec(
                num_scalar_prefetch=4,
                grid=(n_blocks,),
                in_specs=[row_blk(), row_blk(),
                          pl.BlockSpec(memory_space=pl.ANY),
                          pl.BlockSpec((1, 1, 2 * D_FF), lambda b, be, nbu, nxt, valid: (be[b], 0, 0)),
                          pl.BlockSpec(memory_space=pl.ANY),
                          pl.BlockSpec((1, 1, D), lambda b, be, nbu, nxt, valid: (be[b], 0, 0))],
                out_specs=y_blk,
                scratch_shapes=[pltpu.VMEM((D, 2 * D_FF), F32), pltpu.VMEM((D_FF, D), F32),
                                pltpu.VMEM((D, 2 * D_FF), BF16), pltpu.VMEM((D_FF, D), BF16),
                                pltpu.SemaphoreType.DMA((2,))],
            ),
            out_shape=jax.ShapeDtypeStruct((n_rows, 2 * hw), U32),
            compiler_params=_cparams(("arbitrary",)),
            name="expert_ffn",
        )(block_expert[0, :n_blocks], nb_used[0, :1], next_expert, block_valid[0, :n_blocks], xa_buf, xb_buf,
          w_mlp1[0], b1r, w_mlp2[0], b2r)

        last_part = b_off + Bp == B
        win = SC_GATHER_WINDOW if last_part else SC_GATHER_WINDOW_COVERED
        yg = _sc_gather_rows(y_buf, destT.reshape(n_assign), win).reshape(TOP_K, N, 2 * hw)
        tc = COMBINE_TILE
        prev = () if out is None else (out,)
        out = pl.pallas_call(
            _combine_kernel if out is None else _combine_into_kernel,
            grid=(N // tc,),
            in_specs=[pl.BlockSpec((tc, D), lambda i: (i, 0)),
                      pl.BlockSpec((TOP_K, tc, 2 * hw), lambda i: (0, i, 0)),
                      pl.BlockSpec((GATE_ROWS, tc), lambda i: (0, i)),
                      _full((GATE_ROWS, TOP_K * LANES))] + [pl.BlockSpec(memory_space=pl.ANY)] * len(prev),
            out_specs=pl.BlockSpec((tc, D), lambda i: (i + b_off * S // tc, 0)),
            out_shape=jax.ShapeDtypeStruct((B * S, D), F32),
            input_output_aliases={4: 0} if prev else {},
            compiler_params=_cparams(("parallel",)),
            name="combine",
        )(x1, yg, gateT, gate_spread, *prev)
        b_off += Bp
    return out.reshape(B, S, D)
```

```python
import functools

import jax
import jax.numpy as jnp
from jax import lax
from jax.experimental import pallas as pl
from jax.experimental.pallas import tpu as pltpu
from jax.experimental.pallas import tpu_sc as plsc

F32 = jnp.float32
BF16 = jnp.bfloat16
I32 = jnp.int32
U32 = jnp.uint32

D_MODEL = 1024
N_DIFF_HEADS = 4
DIFF_HEAD_DIM = 64
DIFF_V_DIM = 128
DIFF_W = 512
GMLP_W = 256
GMLP_GROUPS = 4
CHUNK = 128
MEM_W = 256
N_MEM_HEADS = 4
HEAD_GROUP = 64
D_IN_PROJ = 2304
ROPE_THETA = 500000.0
ROT_DIM = 16
ROPE_STACK_ROWS = 2 * ROT_DIM
N_EXPERTS = 32
TOP_K = 4
D_FF = 1024
SWIGLU_LIMIT = 7.0
SWIGLU_ALPHA = 1.702
EPS = 1e-6
LAMBDA_INIT = 0.8 - 0.6

LANES = 128
ROW_BLOCK = 1024
FFN_SUB_ROWS = 256
TOKEN_TILE = 1024
COMBINE_TILE = 1024
MEM_BATCHES = 4
LAST_PART_DIVISOR = 2
ROUTER_TILE = 1024
GATE_ROWS = 16
ATTN_TILE = 256
ATTN_Q_TILES_PER_STEP = 8
ATTN_EXT_ROWS = 128 + 16
VMEM_LIMIT = 56 * 1024 * 1024
NEG_BIG = -1e30
LOG2_E = 1.4426950408889634


def _cparams(sem):
    return pltpu.CompilerParams(dimension_semantics=sem, vmem_limit_bytes=VMEM_LIMIT)


def _dot(a, b):
    return jnp.dot(a, b, preferred_element_type=F32)


def _dot_nt(a, b):
    return lax.dot_general(a, b, (((1,), (1,)), ((), ())), preferred_element_type=F32)


def _rms(x, gain):
    ms = jnp.mean(x * x, axis=-1, keepdims=True)
    return x * lax.rsqrt(ms + EPS) * gain


def _group_rms(t, ones_bd, gain):
    w = ones_bd.shape[0]
    chunks = []
    for j in range(t.shape[1] // w):
        c = t[:, j * w:(j + 1) * w]
        ss = _dot((c * c).astype(BF16), ones_bd)
        chunks.append(c * lax.rsqrt(ss * (1.0 / HEAD_GROUP) + EPS))
    return (chunks[0] if len(chunks) == 1 else jnp.concatenate(chunks, axis=1)) * gain


def _pack_bf16_pairs(v):
    w = v.shape[1] // 2
    bits = lax.bitcast_convert_type(v.astype(BF16).astype(F32), U32)
    return (bits[:, :w] & jnp.uint32(0xFFFF0000)) | (bits[:, w:] >> jnp.uint32(16))


def _unpack_bf16_pairs(words):
    hi = lax.bitcast_convert_type(words & jnp.uint32(0xFFFF0000), F32)
    lo = lax.bitcast_convert_type(words << jnp.uint32(16), F32)
    return hi, lo


def _mem_kv_kernel(mem_ref, gmem_ref, wkv_ref, gck_ref, ones_ref, kT_ref, v_ref):
    for j in range(mem_ref.shape[0]):
        m = _rms(mem_ref[j], gmem_ref[...]).astype(BF16)
        kv = _dot(m, wkv_ref[...])
        k = _group_rms(kv[:, :MEM_W], ones_ref[...], gck_ref[...])
        kT_ref[j] = k.T.astype(BF16)
        v_ref[j] = kv[:, MEM_W:].astype(BF16)


def _gelu_tanh(x):
    return 0.5 * x * (1.0 + jnp.tanh(0.7978845608028654 * (x + 0.044715 * (x * x * x))))


def _mixer_in_kernel(x_ref, cs_ref, spread_ref, gmix_ref, win_ref, gq_ref, gk_ref,
                     gsgu_ref, wsp_ref, bsp_ref, gcq_ref, ones256_ref, kT_ref, vm_ref,
                     q_out, k_out, v_out, gc_out):
    tm = x_ref.shape[0]
    hb = _rms(x_ref[...], gmix_ref[...]).astype(BF16)

    def proj(lo, hi):
        return _dot(hb, win_ref[:, lo:hi])

    lane = lax.broadcasted_iota(I32, (tm, LANES), 1)
    first_half = (lane % HEAD_GROUP) < (ROT_DIM // 2)
    tab = lax.dot_general(cs_ref[...], spread_ref[...], (((0,), (0,)), ((), ())), preferred_element_type=F32)
    cosb = tab[:, :LANES] + jnp.where((lane % HEAD_GROUP) >= ROT_DIM, 1.0, 0.0)
    sinb = tab[:, LANES:]

    def norm_rope(t, gain, out_ref):
        tn = _group_rms(t, ones256_ref[...], gain)
        for j in range(DIFF_W // LANES):
            c = tn[:, j * LANES:(j + 1) * LANES]
            partner = jnp.where(first_half, pltpu.roll(c, LANES - ROT_DIM // 2, 1), pltpu.roll(c, ROT_DIM // 2, 1))
            out_ref[:, j * LANES:(j + 1) * LANES] = (c * cosb + partner * sinb).astype(BF16)

    norm_rope(proj(0, DIFF_W), gq_ref[...] * (DIFF_HEAD_DIM ** -0.5 * LOG2_E), q_out)
    norm_rope(proj(DIFF_W, 2 * DIFF_W), gk_ref[...], k_out)
    v_out[...] = proj(2 * DIFF_W, 3 * DIFF_W).astype(BF16)

    z = _gelu_tanh(proj(3 * DIFF_W, 3 * DIFF_W + 2 * GMLP_W))
    u = z[:, :GMLP_W]
    vg = z[:, GMLP_W:]
    vc = vg - jnp.mean(vg, axis=-1, keepdims=True)
    vgn = (vc * lax.rsqrt(jnp.mean(vc * vc, axis=-1, keepdims=True) + EPS) * gsgu_ref[...]).astype(BF16)
    wrow = lax.broadcasted_iota(I32, (CHUNK, GMLP_GROUPS * CHUNK), 0)
    wcol = lax.broadcasted_iota(I32, (CHUNK, GMLP_GROUPS * CHUNK), 1) % CHUNK
    w_causal = jnp.where(wcol <= wrow, wsp_ref[...], 0.0).astype(BF16)
    grp = lax.broadcasted_iota(I32, (CHUNK, GMLP_W), 1) // HEAD_GROUP
    zero_b = jnp.zeros((CHUNK, GMLP_W), BF16)
    for r in range(tm // CHUNK):
        vchunk = vgn[r * CHUNK:(r + 1) * CHUNK, :]
        v_bd = jnp.concatenate([jnp.where(grp == g, vchunk, zero_b) for g in range(GMLP_GROUPS)], axis=0)
        mixed = _dot(w_causal, v_bd) + bsp_ref[...]
        gc_out[r * CHUNK:(r + 1) * CHUNK, 0:GMLP_W] = (u[r * CHUNK:(r + 1) * CHUNK, :] * mixed).astype(BF16)

    pc = proj(3 * DIFF_W + 2 * GMLP_W, D_IN_PROJ)
    qc = _group_rms(pc, ones256_ref[...], gcq_ref[...] * (HEAD_GROUP ** -0.5)).astype(BF16)
    hgrp = lax.broadcasted_iota(I32, (tm, MEM_W), 1) // HEAD_GROUP
    zero_q = jnp.zeros((tm, MEM_W), BF16)
    q_st = jnp.concatenate([jnp.where(hgrp == h, qc, zero_q) for h in range(N_MEM_HEADS)], axis=0)
    s = _dot(q_st, kT_ref[0])
    p = jnp.exp(s - jnp.max(s, axis=-1, keepdims=True))
    o = _dot(p.astype(BF16), vm_ref[0]) / jnp.sum(p, axis=-1, keepdims=True)
    c = jnp.zeros((tm, MEM_W), F32)
    for h in range(N_MEM_HEADS):
        c = c + jnp.where(hgrp == h, o[h * tm:(h + 1) * tm, :], 0.0)
    gc_out[:, GMLP_W:GMLP_W + MEM_W] = c.astype(BF16)


def _diff_attn_kernel(q_ref, k_ref, v_ref, lq1_ref, lk1_ref, lq2_ref, lk2_ref, gsub_ref, o_ref,
                      s0_ref, s1_ref, m_ref, acc_ref, vT_ref):
    tq = ATTN_TILE
    tiles_per_step = q_ref.shape[0] // tq
    assert tiles_per_step % 2 == 0
    seq = k_ref.shape[0]
    step = pl.program_id(1)
    feat = lax.broadcasted_iota(I32, (DIFF_V_DIM, tq), 0)
    heads = range(N_DIFF_HEADS)
    hl = lambda h: slice(h * DIFF_V_DIM, (h + 1) * DIFF_V_DIM)
    ext_rows = vT_ref.shape[1]

    @pl.when(step == 0)
    def _():
        ones_row = jnp.where(lax.broadcasted_iota(I32, (ext_rows - DIFF_V_DIM, seq), 0) == 0, 1.0, 0.0).astype(BF16)
        for h in heads:
            for c in range(seq // tq):
                vT_ref[h, 0:DIFF_V_DIM, c * tq:(c + 1) * tq] = v_ref[c * tq:(c + 1) * tq, hl(h)].T
            vT_ref[h, DIFF_V_DIM:ext_rows, :] = ones_row

    @pl.when(jnp.logical_and(pl.program_id(0) == 0, step == 0))
    def _():
        acc_ref[...] = jnp.zeros(acc_ref.shape, F32)

    lam = (jnp.exp(jnp.sum(lq1_ref[...] * lk1_ref[...], axis=-1, keepdims=True))
           - jnp.exp(jnp.sum(lq2_ref[...] * lk2_ref[...], axis=-1, keepdims=True)) + LAMBDA_INIT)

    def q_tile(j, first_ref, other_ref):
        i = tiles_per_step * step + j
        q_rows = slice(j * tq, (j + 1) * tq)

        def stacked_qT(h):
            qT = q_ref[q_rows, hl(h)].T
            zero = jnp.zeros_like(qT)
            return jnp.concatenate(
                [jnp.where(feat < DIFF_HEAD_DIM, qT, zero), jnp.where(feat >= DIFF_HEAD_DIM, qT, zero)], axis=1)

        qsT = [stacked_qT(h) for h in heads]

        def scores(t, s_ref):
            rows = pl.ds(pl.multiple_of(t * tq, tq), tq)
            for h in heads:
                s_ref[h] = _dot(k_ref[rows, hl(h)], qsT[h])

        def update(t, s_ref, causal):
            cols = pl.ds(pl.multiple_of(t * tq, tq), tq)
            for h in heads:
                s = s_ref[h]
                if causal:
                    key = lax.broadcasted_iota(I32, (tq, 2 * tq), 0)
                    qry = lax.broadcasted_iota(I32, (tq, 2 * tq), 1) % tq
                    s = jnp.where(key <= qry, s, NEG_BIG)
                m = m_ref[h]
                m_new = jnp.maximum(m, jnp.max(s, axis=0, keepdims=True))
                alpha = jnp.exp2(m - m_new)
                p = jnp.exp2(s - m_new)
                m_ref[h] = m_new
                acc_ref[h] = alpha * acc_ref[h] + _dot(vT_ref[h, :, cols], p.astype(BF16))

        m_ref[...] = jnp.full(m_ref.shape, NEG_BIG, F32)
        scores(0, first_ref)

        def two_tiles(pidx, carry):
            t = 2 * pidx
            scores(t + 1, other_ref)
            update(t, first_ref, False)
            scores(t + 2, first_ref)
            update(t + 1, other_ref, False)
            return carry

        lax.fori_loop(0, (tiles_per_step // 2) * step + j // 2, two_tiles, 0)

        if j % 2 == 0:
            update(i, first_ref, True)
        else:
            scores(i, other_ref)
            update(i - 1, first_ref, False)
            update(i, other_ref, True)

        for h in heads:
            on = acc_ref[h, 0:DIFF_V_DIM, :] * (1.0 / acc_ref[h, DIFF_V_DIM:DIFF_V_DIM + 1, :])
            o = on[:, :tq] - lam * on[:, tq:]
            ms = jnp.mean(o * o, axis=0, keepdims=True)
            o = o * lax.rsqrt(ms + EPS) * gsub_ref[...] * (1.0 - LAMBDA_INIT)
            o_ref[q_rows, hl(h)] = o.T.astype(BF16)

    first_ref, other_ref = s0_ref, s1_ref
    for j in range(tiles_per_step):
        q_tile(j, first_ref, other_ref)
        last_read = first_ref if j % 2 == 0 else other_ref
        first_ref, other_ref = (other_ref, first_ref) if last_read is first_ref else (first_ref, other_ref)


def _out_router_kernel(x_ref, a_ref, gc_ref, wo_ref, gffn_ref, wrh_ref, wrl_ref, br_ref, before_ref,
                       x1_out, hpa_out, hpb_out, idx_out, gate_out, pos_out, cnt_out, carry_ref):
    tm = x_ref.shape[0]

    @pl.when(pl.program_id(0) == 0)
    def _():
        carry_ref[...] = jnp.zeros_like(carry_ref)

    sub = before_ref.shape[0]
    wr_stack = jnp.concatenate([wrh_ref[...], wrl_ref[...]], axis=0)
    eio = lax.broadcasted_iota(I32, (N_EXPERTS, sub), 0)
    before_b = before_ref[...]
    carry = carry_ref[...]
    for r in range(tm // sub):
        rows = slice(r * sub, (r + 1) * sub)
        mix = jnp.concatenate([a_ref[rows, :], gc_ref[rows, :]], axis=1)
        x1 = x_ref[rows, :] + _dot(mix, wo_ref[...])
        x1_out[rows, :] = x1
        h2 = _rms(x1, gffn_ref[...])
        hb = h2.astype(BF16)
        hpa_out[rows, :] = _pack_bf16_pairs(h2[:, :D_MODEL // 2])
        hpb_out[rows, :] = _pack_bf16_pairs(h2[:, D_MODEL // 2:])

        h_lo = (h2 - hb.astype(F32)).astype(BF16)
        both = _dot_nt(wr_stack, hb)
        logits = (both[:N_EXPERTS] + both[N_EXPERTS:]) + _dot_nt(wrh_ref[...], h_lo) + br_ref[...]

        vals, idxs, sels = [], [], []
        cur = logits
        for _ in range(TOP_K):
            m = jnp.max(cur, axis=0, keepdims=True)
            ik = jnp.min(jnp.where(cur == m, eio, N_EXPERTS), axis=0, keepdims=True)
            sel = eio == ik
            cur = jnp.where(sel, -jnp.inf, cur)
            vals.append(m)
            idxs.append(ik)
            sels.append(sel)
        es = [jnp.exp(v - vals[0]) for v in vals]
        tot = es[0] + es[1] + es[2] + es[3]
        gates = jnp.concatenate([e / tot for e in es], axis=0)
        g_hi = gates.astype(BF16)
        g_lo = (gates - g_hi.astype(F32)).astype(BF16)
        gate_out[:, rows] = jnp.concatenate(
            [g_hi, g_lo, jnp.zeros((GATE_ROWS - 2 * TOP_K, sub), BF16)], axis=0)
        idx_out[:, rows] = jnp.concatenate(idxs, axis=0)

        cnt = jnp.zeros((N_EXPERTS, sub), F32)
        for sel in sels:
            cnt = cnt + jnp.where(sel, 1.0, 0.0)
        base = carry + _dot(cnt.astype(BF16), before_b)
        pos_out[:, rows] = jnp.concatenate(
            [jnp.sum(jnp.where(sel, base, 0.0), axis=0, keepdims=True) for sel in sels], axis=0).astype(I32)
        carry = carry + jnp.sum(cnt, axis=1, keepdims=True)
    carry_ref[...] = carry
    cnt_out[...] = carry


def _dest_kernel(cnt_ref, idx_ref, pos_ref, dest_out, be_out, valid_out, nbu_out):
    idx = idx_ref[...]
    dest = pos_ref[...]
    bidx = lax.broadcasted_iota(I32, be_out.shape, 1)
    be = jnp.zeros(be_out.shape, I32)
    valid = jnp.zeros(be_out.shape, I32)
    run = jnp.int32(0)
    for e in range(N_EXPERTS):
        dest = dest + jnp.where(idx == e, run, 0)
        first_block = run // ROW_BLOCK
        run = run + ((cnt_ref[e] + (ROW_BLOCK - 1)) // ROW_BLOCK) * ROW_BLOCK
        mine = jnp.logical_and(bidx >= first_block, bidx < run // ROW_BLOCK)
        valid = jnp.where(mine, jnp.clip(cnt_ref[e] - (bidx - first_block) * ROW_BLOCK, 0, ROW_BLOCK), valid)
        be = be + jnp.where(bidx >= run // ROW_BLOCK, 1, 0)
    dest_out[...] = dest
    be_out[...] = jnp.minimum(be, N_EXPERTS - 1)
    valid_out[...] = valid
    nbu_out[...] = jnp.zeros(nbu_out.shape, I32) + run // ROW_BLOCK


def _expert_ffn_kernel(be_ref, nbu_ref, nxt_ref, valid_ref, xa_ref, xb_ref, w1_hbm, b1_ref, w2_hbm, b2_ref,
                       y_ref, w1f_ref, w2f_ref, w1s_ref, w2s_ref, sem_ref):
    b = pl.program_id(0)

    def weight_copies(e):
        return (pltpu.make_async_copy(w1_hbm.at[e], w1f_ref, sem_ref.at[0]),
                pltpu.make_async_copy(w2_hbm.at[e], w2f_ref, sem_ref.at[1]))

    def mlp(rows):
        parts = _unpack_bf16_pairs(xa_ref[0:rows, :]) + _unpack_bf16_pairs(xb_ref[0:rows, :])
        xrow = jnp.concatenate([p.astype(BF16) for p in parts], axis=1)
        hm = _dot(xrow, w1s_ref[...]) + b1_ref[0]
        glu = jnp.minimum(hm[:, :D_FF], SWIGLU_LIMIT)
        lin = jnp.clip(hm[:, D_FF:], -SWIGLU_LIMIT, SWIGLU_LIMIT)
        act = glu * jax.nn.sigmoid(SWIGLU_ALPHA * glu) * (lin + 1.0)
        y = _dot(act.astype(BF16), w2s_ref[...]) + b2_ref[0]
        hw = D_MODEL // 4
        y_ref[0:rows, 0:hw] = _pack_bf16_pairs(y[:, :D_MODEL // 2])
        y_ref[0:rows, hw:] = _pack_bf16_pairs(y[:, D_MODEL // 2:])

    @pl.when(b < nbu_ref[0])
    def _():
        e = be_ref[b]

        @pl.when(b == 0)
        def _():
            for cp in weight_copies(e):
                cp.start()

        @pl.when(jnp.logical_or(b == 0, e != be_ref[jnp.maximum(b - 1, 0)]))
        def _():
            for cp in weight_copies(e):
                cp.wait()
            w1s_ref[...] = w1f_ref[...].astype(BF16)
            w2s_ref[...] = w2f_ref[...].astype(BF16)
            nxt = nxt_ref[e]

            @pl.when(nxt >= 0)
            def _():
                for cp in weight_copies(nxt):
                    cp.start()

        n_sub = (valid_ref[b] + (FFN_SUB_ROWS - 1)) // FFN_SUB_ROWS
        for k in range(1, ROW_BLOCK // FFN_SUB_ROWS + 1):
            pl.when(n_sub == k)(functools.partial(mlp, k * FFN_SUB_ROWS))


SC_WINDOW = 128
SC_GATHER_WINDOW = 64
SC_GATHER_WINDOW_COVERED = 32


def _sc_mesh():
    return plsc.VectorSubcoreMesh(core_axis_name="c", subcore_axis_name="s")


def _sc_gather_rows(table, idx, win):
    n = idx.shape[0]
    width = table.shape[1]

    @functools.partial(pl.kernel, out_type=jax.ShapeDtypeStruct((n, width), table.dtype), mesh=_sc_mesh(),
                       scratch_types=[])
    def gather_kernel(t_hbm, i_hbm, o_hbm):
        def body(i_vmem, o_vmem):
            pltpu.sync_copy(t_hbm.at[i_vmem.at[0]], o_vmem)

        pltpu.emit_pipeline(
            body,
            grid=(n // win,),
            in_specs=[pl.BlockSpec((1, win), lambda i: (i, 0))],
            out_specs=[pl.BlockSpec((win, width), lambda i: (i, 0))],
            core_axis_name=("c", "s"),
            dimension_semantics=(pltpu.PARALLEL,),
        )(i_hbm, o_hbm)

    return gather_kernel(table, idx.reshape(n // win, win))


def _sc_scatter_rows(rows, idx_rows, n_out):
    n, width = rows.shape

    @functools.partial(pl.kernel, out_type=jax.ShapeDtypeStruct((n_out, width), rows.dtype), mesh=_sc_mesh(),
                       scratch_types=[])
    def scatter_kernel(r_hbm, *refs):
        i_hbms, o_hbm = refs[:-1], refs[-1]

        def body(r_vmem, *i_vmems):
            for i_vmem in i_vmems:
                pltpu.sync_copy(r_vmem, o_hbm.at[i_vmem.at[0]])

        pltpu.emit_pipeline(
            body,
            grid=(n // SC_WINDOW,),
            in_specs=[pl.BlockSpec((SC_WINDOW, width), lambda i: (i, 0))]
            + [pl.BlockSpec((1, SC_WINDOW), lambda i: (0, i)) for _ in i_hbms],
            out_specs=[],
            core_axis_name=("c", "s"),
            dimension_semantics=(pltpu.PARALLEL,),
        )(r_hbm, *i_hbms)

    return scatter_kernel(rows, *idx_rows)


def _combine_kernel(x1_ref, yg_ref, gate_ref, spread_ref, o_ref):
    q = D_MODEL // 4
    x1 = x1_ref[...]
    acc = [x1[:, j * q:(j + 1) * q] for j in range(4)]
    gfull = lax.dot_general(gate_ref[...], spread_ref[...], (((0,), (0,)), ((), ())), preferred_element_type=F32)
    for k in range(TOP_K):
        parts = _unpack_bf16_pairs(yg_ref[k, :, 0:q]) + _unpack_bf16_pairs(yg_ref[k, :, q:])
        g = jnp.tile(gfull[:, k * LANES:(k + 1) * LANES], (1, q // LANES))
        acc = [a + g * p for a, p in zip(acc, parts)]
    for j in range(4):
        o_ref[:, j * q:(j + 1) * q] = acc[j]


def _combine_into_kernel(x1_ref, yg_ref, gate_ref, spread_ref, prev_ref, o_ref):
    del prev_ref
    _combine_kernel(x1_ref, yg_ref, gate_ref, spread_ref, o_ref)


def _block_diag_ones(width):
    r = jnp.arange(width) // HEAD_GROUP
    return (r[:, None] == r[None, :]).astype(BF16)


def _rope_tables(positions):
    half = ROT_DIM // 2
    inv_freq = ROPE_THETA ** (-jnp.arange(0, ROT_DIM, 2, dtype=F32) / ROT_DIM)
    ang = inv_freq[:, None] * positions.astype(F32).reshape(1, -1)
    cs = jnp.concatenate([jnp.cos(ang), jnp.sin(ang)], axis=0)
    cs_hi = cs.astype(BF16)
    cs_lo = (cs - cs_hi.astype(F32)).astype(BF16)
    lane = jnp.arange(LANES) % HEAD_GROUP
    j = jnp.arange(half)[:, None]
    lo_half = (lane[None, :] == j).astype(F32)
    hi_half = (lane[None, :] == j + half).astype(F32)
    spread = jnp.concatenate([
        jnp.concatenate([lo_half + hi_half, jnp.zeros((half, LANES), F32)], axis=1),
        jnp.concatenate([jnp.zeros((half, LANES), F32), hi_half - lo_half], axis=1)], axis=0)
    return jnp.concatenate([cs_hi, cs_lo], axis=0), jnp.concatenate([spread, spread], axis=0).astype(BF16)


def _full(shape):
    return pl.BlockSpec(shape, lambda *_: (0,) * len(shape))


def kernel(x, mem, positions, g_mix_norm, w_in, g_dq, g_dk, lambda_q1, lambda_k1, lambda_q2, lambda_k2, g_subln, g_sgu, w_spatial, b_spatial, g_mem_norm, w_mem_kv, g_cq, g_ck, w_out, g_ffn_norm, w_router, b_router, w_mlp1, b_mlp1, w_mlp2, b_mlp2):
    B, S, D = x.shape
    M = mem.shape[1]
    tm = TOKEN_TILE
    tr = ROUTER_TILE
    assert D == D_MODEL and S % tm == 0 and S % (ATTN_Q_TILES_PER_STEP * ATTN_TILE) == 0 and g_mix_norm.shape[0] == 1
    last_batches = B // LAST_PART_DIVISOR
    part_batches = (B - last_batches, last_batches) if last_batches > 0 and S % tr == 0 else (B,)

    xf = x.reshape(B * S, D)
    rope_cs, rope_spread = _rope_tables(positions)
    ones256 = _block_diag_ones(MEM_W)
    row = lambda v: v.reshape(1, -1).astype(F32)
    tile_row = lambda v, reps: jnp.tile(v.reshape(1, -1).astype(F32), (1, reps))
    w_in_b, w_out_b, w_kv_b = w_in[0].astype(BF16), w_out[0].astype(BF16), w_mem_kv[0].astype(BF16)
    w_sp_lanes = jnp.transpose(w_spatial[0], (1, 0, 2)).reshape(CHUNK, GMLP_GROUPS * CHUNK)
    b_sp_lanes = jnp.repeat(b_spatial[0].T, HEAD_GROUP, axis=1)
    wr = w_router[0].T.astype(F32)
    wr_hi = wr.astype(BF16)
    wr_lo = (wr - wr_hi.astype(F32)).astype(BF16)
    before = (jnp.arange(tr)[:, None] < jnp.arange(tr)[None, :]).astype(BF16)
    gate_row = jnp.arange(GATE_ROWS)[:, None]
    gate_spread = ((gate_row < 2 * TOP_K) & (gate_row % TOP_K == jnp.arange(TOP_K * LANES)[None, :] // LANES)).astype(BF16)
    b1r = b_mlp1[0].reshape(N_EXPERTS, 1, 2 * D_FF)
    b2r = b_mlp2[0].reshape(N_EXPERTS, 1, D)

    mb = MEM_BATCHES if B % MEM_BATCHES == 0 else 1
    kT, vm = pl.pallas_call(
        _mem_kv_kernel,
        grid=(B // mb,),
        in_specs=[pl.BlockSpec((mb, M, D), lambda b: (b, 0, 0)), _full((1, D)), _full((D, 2 * MEM_W)),
                  _full((1, MEM_W)), _full((MEM_W, MEM_W))],
        out_specs=[pl.BlockSpec((mb, MEM_W, M), lambda b: (b, 0, 0)),
                   pl.BlockSpec((mb, M, MEM_W), lambda b: (b, 0, 0))],
        out_shape=[jax.ShapeDtypeStruct((B, MEM_W, M), BF16), jax.ShapeDtypeStruct((B, M, MEM_W), BF16)],
        compiler_params=_cparams(("parallel",)),
        name="mem_kv",
    )(mem, row(g_mem_norm[0]), w_kv_b, tile_row(g_ck[0], N_MEM_HEADS), ones256)

    out = None
    b_off = 0
    for Bp in part_batches:
        N = Bp * S
        assert N % tr == 0 and N % tm == 0
        n_assign = N * TOP_K
        n_blocks = -(-n_assign // ROW_BLOCK) + N_EXPERTS
        n_rows = n_blocks * ROW_BLOCK
        nb_pad = -(-n_blocks // LANES) * LANES
        t_off = b_off * S // tm
        r_off = b_off * S // tr

        tiles_per_batch = S // tm
        tok = lambda w: pl.BlockSpec((tm, w), lambda i: (i, 0))
        tok_in = lambda w: pl.BlockSpec((tm, w), lambda i: (i + t_off, 0))
        qn, kn, vv, gc = pl.pallas_call(
            _mixer_in_kernel,
            grid=(N // tm,),
            in_specs=[tok_in(D), pl.BlockSpec((ROPE_STACK_ROWS, tm), lambda i: (0, i + t_off)),
                      _full((ROPE_STACK_ROWS, 2 * LANES)),
                      _full((1, D)), _full((D, D_IN_PROJ)),
                      _full((1, DIFF_W)), _full((1, DIFF_W)),
                      _full((1, GMLP_W)), _full((CHUNK, GMLP_GROUPS * CHUNK)), _full((CHUNK, GMLP_W)),
                      _full((1, MEM_W)), _full((MEM_W, MEM_W)),
                      pl.BlockSpec((1, MEM_W, M), lambda i: (i // tiles_per_batch + b_off, 0, 0)),
                      pl.BlockSpec((1, M, MEM_W), lambda i: (i // tiles_per_batch + b_off, 0, 0))],
            out_specs=[tok(DIFF_W), tok(DIFF_W), tok(DIFF_W), tok(GMLP_W + MEM_W)],
            out_shape=[jax.ShapeDtypeStruct((N, DIFF_W), BF16)] * 3 + [jax.ShapeDtypeStruct((N, GMLP_W + MEM_W), BF16)],
            compiler_params=_cparams(("parallel",)),
            name="mixer_in",
        )(xf, rope_cs, rope_spread, row(g_mix_norm[0]), w_in_b,
          tile_row(g_dq[0], 2 * N_DIFF_HEADS), tile_row(g_dk[0], 2 * N_DIFF_HEADS),
          row(g_sgu[0]), w_sp_lanes, b_sp_lanes, tile_row(g_cq[0], N_MEM_HEADS), ones256, kT, vm)

        tq = ATTN_TILE
        q_rows = ATTN_Q_TILES_PER_STEP * tq
        nqp = S // q_rows
        head_q = pl.BlockSpec((q_rows, DIFF_W), lambda b, i: (b * nqp + i, 0))
        head_kv = pl.BlockSpec((S, DIFF_W), lambda b, i: (b, 0))
        lam_spec = pl.BlockSpec((1, DIFF_HEAD_DIM), lambda b, i: (0, 0))
        nh = N_DIFF_HEADS
        attn = pl.pallas_call(
            _diff_attn_kernel,
            grid=(Bp, nqp),
            in_specs=[head_q, head_kv, head_kv, lam_spec, lam_spec, lam_spec, lam_spec,
                      pl.BlockSpec((DIFF_V_DIM, 1), lambda b, i: (0, 0))],
            out_specs=head_q,
            out_shape=jax.ShapeDtypeStruct((N, DIFF_W), BF16),
            scratch_shapes=[pltpu.VMEM((nh, tq, 2 * tq), F32), pltpu.VMEM((nh, tq, 2 * tq), F32),
                            pltpu.VMEM((nh, 1, 2 * tq), F32), pltpu.VMEM((nh, ATTN_EXT_ROWS, 2 * tq), F32),
                            pltpu.VMEM((nh, ATTN_EXT_ROWS, S), BF16)],
            compiler_params=_cparams(("arbitrary", "arbitrary")),
            name="diff_attn",
        )(qn, kn, vv, row(lambda_q1[0]), row(lambda_k1[0]), row(lambda_q2[0]), row(lambda_k2[0]),
          g_subln[0].reshape(DIFF_V_DIM, 1).astype(F32))

        rtok = lambda w: pl.BlockSpec((tr, w), lambda i: (i, 0))
        rtok_in = lambda w: pl.BlockSpec((tr, w), lambda i: (i + r_off, 0))
        tokT = lambda: pl.BlockSpec((TOP_K, tr), lambda i: (0, i))
        hw = D // 4
        x1, hpa, hpb, idxT, gateT, posT, counts = pl.pallas_call(
            _out_router_kernel,
            grid=(N // tr,),
            in_specs=[rtok_in(D), rtok(DIFF_W), rtok(GMLP_W + MEM_W), _full((D, D)), _full((1, D)),
                      _full((N_EXPERTS, D)), _full((N_EXPERTS, D)), _full((N_EXPERTS, 1)), _full((tr, tr))],
            out_specs=[rtok(D), rtok(hw), rtok(hw), tokT(), pl.BlockSpec((GATE_ROWS, tr), lambda i: (0, i)), tokT(),
                       _full((N_EXPERTS, 1))],
            out_shape=[jax.ShapeDtypeStruct((N, D), F32), jax.ShapeDtypeStruct((N, hw), U32),
                       jax.ShapeDtypeStruct((N, hw), U32), jax.ShapeDtypeStruct((TOP_K, N), I32),
                       jax.ShapeDtypeStruct((GATE_ROWS, N), BF16),
                       jax.ShapeDtypeStruct((TOP_K, N), I32), jax.ShapeDtypeStruct((N_EXPERTS, 1), F32)],
            scratch_shapes=[pltpu.VMEM((N_EXPERTS, 1), F32)],
            compiler_params=_cparams(("arbitrary",)),
            name="out_router",
        )(xf, attn, gc, w_out_b, row(g_ffn_norm[0]), wr_hi, wr_lo, b_router[0].reshape(N_EXPERTS, 1).astype(F32), before)

        destT, block_expert, block_valid, nb_used = pl.pallas_call(
            _dest_kernel,
            grid_spec=pltpu.PrefetchScalarGridSpec(
                num_scalar_prefetch=1,
                grid=(1,),
                in_specs=[pl.BlockSpec((TOP_K, N), lambda i, c: (0, 0)), pl.BlockSpec((TOP_K, N), lambda i, c: (0, 0))],
                out_specs=[pl.BlockSpec((TOP_K, N), lambda i, c: (0, 0)), pl.BlockSpec((1, nb_pad), lambda i, c: (0, 0)),
                           pl.BlockSpec((1, nb_pad), lambda i, c: (0, 0)), pl.BlockSpec((1, LANES), lambda i, c: (0, 0))],
            ),
            out_shape=[jax.ShapeDtypeStruct((TOP_K, N), I32), jax.ShapeDtypeStruct((1, nb_pad), I32),
                       jax.ShapeDtypeStruct((1, nb_pad), I32), jax.ShapeDtypeStruct((1, LANES), I32)],
            compiler_params=_cparams(("arbitrary",)),
            name="dest",
        )(counts.reshape(N_EXPERTS).astype(I32), idxT, posT)

        dest_rows = [destT[k].reshape(1, N) for k in range(TOP_K)]
        xa_buf = _sc_scatter_rows(hpa, dest_rows, n_rows)
        xb_buf = _sc_scatter_rows(hpb, dest_rows, n_rows)

        cnt_i = counts.reshape(N_EXPERTS).astype(I32)
        owner = jnp.where(cnt_i > 0, jnp.arange(N_EXPERTS, dtype=I32), N_EXPERTS)
        later = jnp.concatenate([lax.cummin(owner[::-1])[::-1][1:], jnp.full((1,), N_EXPERTS, I32)])
        next_expert = jnp.where(later < N_EXPERTS, later, -1)
        last = lambda b, be, nbu, nxt, valid: jnp.minimum(b, nbu[0] - 1)
        row_blk = lambda: pl.BlockSpec((ROW_BLOCK, hw), lambda b, be, nbu, nxt, valid: (last(b, be, nbu, nxt, valid), 0))
        y_blk = pl.BlockSpec((ROW_BLOCK, 2 * hw), lambda b, be, nbu, nxt, valid: (last(b, be, nbu, nxt, valid), 0))
        y_buf = pl.pallas_call(
            _expert_ffn_kernel,
            grid_spec=pltpu.PrefetchScalarGridSpec(
                num_scalar_prefetch=4,
                grid=(n_blocks,),
                in_specs=[row_blk(), row_blk(),
                          pl.BlockSpec(memory_space=pl.ANY),
                          pl.BlockSpec((1, 1, 2 * D_FF), lambda b, be, nbu, nxt, valid: (be[b], 0, 0)),
                          pl.BlockSpec(memory_space=pl.ANY),
                          pl.BlockSpec((1, 1, D), lambda b, be, nbu, nxt, valid: (be[b], 0, 0))],
                out_specs=y_blk,
                scratch_shapes=[pltpu.VMEM((D, 2 * D_FF), F32), pltpu.VMEM((D_FF, D), F32),
                                pltpu.VMEM((D, 2 * D_FF), BF16), pltpu.VMEM((D_FF, D), BF16),
                                pltpu.SemaphoreType.DMA((2,))],
            ),
            out_shape=jax.ShapeDtypeStruct((n_rows, 2 * hw), U32),
            compiler_params=_cparams(("arbitrary",)),
            name="expert_ffn",
        )(block_expert[0, :n_blocks], nb_used[0, :1], next_expert, block_valid[0, :n_blocks], xa_buf, xb_buf,
          w_mlp1[0], b1r, w_mlp2[0], b2r)

        last_part = b_off + Bp == B
        win = SC_GATHER_WINDOW if last_part else SC_GATHER_WINDOW_COVERED
        yg = _sc_gather_rows(y_buf, destT.reshape(n_assign), win).reshape(TOP_K, N, 2 * hw)
        tc = COMBINE_TILE
        prev = () if out is None else (out,)
        out = pl.pallas_call(
            _combine_kernel if out is None else _combine_into_kernel,
            grid=(N // tc,),
            in_specs=[pl.BlockSpec((tc, D), lambda i: (i, 0)),
                      pl.BlockSpec((TOP_K, tc, 2 * hw), lambda i: (0, i, 0)),
                      pl.BlockSpec((GATE_ROWS, tc), lambda i: (0, i)),
                      _full((GATE_ROWS, TOP_K * LANES))] + [pl.BlockSpec(memory_space=pl.ANY)] * len(prev),
            out_specs=pl.BlockSpec((tc, D), lambda i: (i + b_off * S // tc, 0)),
            out_shape=jax.ShapeDtypeStruct((B * S, D), F32),
            input_output_aliases={4: 0} if prev else {},
            compiler_params=_cparams(("parallel",)),
            name="combine",
        )(x1, yg, gateT, gate_spread, *prev)
        b_off += Bp
    return out.reshape(B, S, D)
```

```python
import functools

import jax
import jax.numpy as jnp
from jax import lax
from jax.experimental import pallas as pl
from jax.experimental.pallas import tpu as pltpu
from jax.experimental.pallas import tpu_sc as plsc

F32 = jnp.float32
BF16 = jnp.bfloat16
I32 = jnp.int32
U32 = jnp.uint32

D_MODEL = 1024
N_DIFF_HEADS = 4
DIFF_HEAD_DIM = 64
DIFF_V_DIM = 128
DIFF_W = 512
GMLP_W = 256
GMLP_GROUPS = 4
CHUNK = 128
MEM_W = 256
N_MEM_HEADS = 4
HEAD_GROUP = 64
D_IN_PROJ = 2304
ROPE_THETA = 500000.0
ROT_DIM = 16
ROPE_STACK_ROWS = 2 * ROT_DIM
N_EXPERTS = 32
TOP_K = 4
D_FF = 1024
SWIGLU_LIMIT = 7.0
SWIGLU_ALPHA = 1.702
EPS = 1e-6
LAMBDA_INIT = 0.8 - 0.6

LANES = 128
ROW_BLOCK = 1024
FFN_SUB_ROWS = 256
TOKEN_TILE = 1024
COMBINE_TILE = 1024
MEM_BATCHES = 4
LAST_PART_DIVISOR = 2
ROUTER_TILE = 1024
PREFIX_CHUNK = 256
GATE_ROWS = 16
ATTN_TILE = 256
ATTN_Q_TILES_PER_STEP = 8
ATTN_EXT_ROWS = 128 + 16
VMEM_LIMIT = 56 * 1024 * 1024
NEG_BIG = -1e30
LOG2_E = 1.4426950408889634


def _cparams(sem):
    return pltpu.CompilerParams(dimension_semantics=sem, vmem_limit_bytes=VMEM_LIMIT)


def _dot(a, b):
    return jnp.dot(a, b, preferred_element_type=F32)


def _dot_nt(a, b):
    return lax.dot_general(a, b, (((1,), (1,)), ((), ())), preferred_element_type=F32)


def _rms(x, gain):
    ms = jnp.mean(x * x, axis=-1, keepdims=True)
    return x * lax.rsqrt(ms + EPS) * gain


def _group_rms(t, ones_bd, gain):
    w = ones_bd.shape[0]
    chunks = []
    for j in range(t.shape[1] // w):
        c = t[:, j * w:(j + 1) * w]
        ss = _dot((c * c).astype(BF16), ones_bd)
        chunks.append(c * lax.rsqrt(ss * (1.0 / HEAD_GROUP) + EPS))
    return (chunks[0] if len(chunks) == 1 else jnp.concatenate(chunks, axis=1)) * gain


def _pack_bf16_pairs(v):
    w = v.shape[1] // 2
    bits = lax.bitcast_convert_type(v.astype(BF16).astype(F32), U32)
    return (bits[:, :w] & jnp.uint32(0xFFFF0000)) | (bits[:, w:] >> jnp.uint32(16))


def _unpack_bf16_pairs(words):
    hi = lax.bitcast_convert_type(words & jnp.uint32(0xFFFF0000), F32)
    lo = lax.bitcast_convert_type(words << jnp.uint32(16), F32)
    return hi, lo


def _mem_kv_kernel(mem_ref, gmem_ref, wkv_ref, gck_ref, ones_ref, kT_ref, v_ref):
    for j in range(mem_ref.shape[0]):
        m = _rms(mem_ref[j], gmem_ref[...]).astype(BF16)
        kv = _dot(m, wkv_ref[...])
        k = _group_rms(kv[:, :MEM_W], ones_ref[...], gck_ref[...])
        kT_ref[j] = k.T.astype(BF16)
        v_ref[j] = kv[:, MEM_W:].astype(BF16)


def _gelu_tanh(x):
    return 0.5 * x * (1.0 + jnp.tanh(0.7978845608028654 * (x + 0.044715 * (x * x * x))))


def _mixer_in_kernel(x_ref, cs_ref, spread_ref, gmix_ref, win_ref, gq_ref, gk_ref,
                     gsgu_ref, wsp_ref, bsp_ref, gcq_ref, ones256_ref, kT_ref, vm_ref,
                     q_out, k_out, v_out, gc_out):
    tm = x_ref.shape[0]
    hb = _rms(x_ref[...], gmix_ref[...]).astype(BF16)

    def proj(lo, hi):
        return _dot(hb, win_ref[:, lo:hi])

    lane = lax.broadcasted_iota(I32, (tm, LANES), 1)
    first_half = (lane % HEAD_GROUP) < (ROT_DIM // 2)
    tab = lax.dot_general(cs_ref[...], spread_ref[...], (((0,), (0,)), ((), ())), preferred_element_type=F32)
    cosb = tab[:, :LANES] + jnp.where((lane % HEAD_GROUP) >= ROT_DIM, 1.0, 0.0)
    sinb = tab[:, LANES:]

    def norm_rope(t, gain, out_ref):
        tn = _group_rms(t, ones256_ref[...], gain)
        for j in range(DIFF_W // LANES):
            c = tn[:, j * LANES:(j + 1) * LANES]
            partner = jnp.where(first_half, pltpu.roll(c, LANES - ROT_DIM // 2, 1), pltpu.roll(c, ROT_DIM // 2, 1))
            out_ref[:, j * LANES:(j + 1) * LANES] = (c * cosb + partner * sinb).astype(BF16)

    norm_rope(proj(0, DIFF_W), gq_ref[...] * (DIFF_HEAD_DIM ** -0.5 * LOG2_E), q_out)
    norm_rope(proj(DIFF_W, 2 * DIFF_W), gk_ref[...], k_out)
    v_out[...] = proj(2 * DIFF_W, 3 * DIFF_W).astype(BF16)

    z = _gelu_tanh(proj(3 * DIFF_W, 3 * DIFF_W + 2 * GMLP_W))
    u = z[:, :GMLP_W]
    vg = z[:, GMLP_W:]
    vc = vg - jnp.mean(vg, axis=-1, keepdims=True)
    vgn = (vc * lax.rsqrt(jnp.mean(vc * vc, axis=-1, keepdims=True) + EPS) * gsgu_ref[...]).astype(BF16)
    wrow = lax.broadcasted_iota(I32, (CHUNK, GMLP_GROUPS * CHUNK), 0)
    wcol = lax.broadcasted_iota(I32, (CHUNK, GMLP_GROUPS * CHUNK), 1) % CHUNK
    w_causal = jnp.where(wcol <= wrow, wsp_ref[...], 0.0).astype(BF16)
    grp = lax.broadcasted_iota(I32, (CHUNK, GMLP_W), 1) // HEAD_GROUP
    zero_b = jnp.zeros((CHUNK, GMLP_W), BF16)
    for r in range(tm // CHUNK):
        vchunk = vgn[r * CHUNK:(r + 1) * CHUNK, :]
        v_bd = jnp.concatenate([jnp.where(grp == g, vchunk, zero_b) for g in range(GMLP_GROUPS)], axis=0)
        mixed = _dot(w_causal, v_bd) + bsp_ref[...]
        gc_out[r * CHUNK:(r + 1) * CHUNK, 0:GMLP_W] = (u[r * CHUNK:(r + 1) * CHUNK, :] * mixed).astype(BF16)

    pc = proj(3 * DIFF_W + 2 * GMLP_W, D_IN_PROJ)
    qc = _group_rms(pc, ones256_ref[...], gcq_ref[...] * (HEAD_GROUP ** -0.5)).astype(BF16)
    hgrp = lax.broadcasted_iota(I32, (tm, MEM_W), 1) // HEAD_GROUP
    zero_q = jnp.zeros((tm, MEM_W), BF16)
    q_st = jnp.concatenate([jnp.where(hgrp == h, qc, zero_q) for h in range(N_MEM_HEADS)], axis=0)
    s = _dot(q_st, kT_ref[0])
    p = jnp.exp(s - jnp.max(s, axis=-1, keepdims=True))
    o = _dot(p.astype(BF16), vm_ref[0]) / jnp.sum(p, axis=-1, keepdims=True)
    c = jnp.zeros((tm, MEM_W), F32)
    for h in range(N_MEM_HEADS):
        c = c + jnp.where(hgrp == h, o[h * tm:(h + 1) * tm, :], 0.0)
    gc_out[:, GMLP_W:GMLP_W + MEM_W] = c.astype(BF16)


def _diff_attn_kernel(q_ref, k_ref, v_ref, lq1_ref, lk1_ref, lq2_ref, lk2_ref, gsub_ref, o_ref,
                      s0_ref, s1_ref, m_ref, acc_ref, vT_ref):
    tq = ATTN_TILE
    tiles_per_step = q_ref.shape[0] // tq
    assert tiles_per_step % 2 == 0
    seq = k_ref.shape[0]
    step = pl.program_id(1)
    feat = lax.broadcasted_iota(I32, (DIFF_V_DIM, tq), 0)
    heads = range(N_DIFF_HEADS)
    hl = lambda h: slice(h * DIFF_V_DIM, (h + 1) * DIFF_V_DIM)
    ext_rows = vT_ref.shape[1]

    @pl.when(step == 0)
    def _():
        ones_row = jnp.where(lax.broadcasted_iota(I32, (ext_rows - DIFF_V_DIM, seq), 0) == 0, 1.0, 0.0).astype(BF16)
        for h in heads:
            for c in range(seq // tq):
                vT_ref[h, 0:DIFF_V_DIM, c * tq:(c + 1) * tq] = v_ref[c * tq:(c + 1) * tq, hl(h)].T
            vT_ref[h, DIFF_V_DIM:ext_rows, :] = ones_row

    @pl.when(jnp.logical_and(pl.program_id(0) == 0, step == 0))
    def _():
        acc_ref[...] = jnp.zeros(acc_ref.shape, F32)

    lam = (jnp.exp(jnp.sum(lq1_ref[...] * lk1_ref[...], axis=-1, keepdims=True))
           - jnp.exp(jnp.sum(lq2_ref[...] * lk2_ref[...], axis=-1, keepdims=True)) + LAMBDA_INIT)

    def q_tile(j, first_ref, other_ref):
        i = tiles_per_step * step + j
        q_rows = slice(j * tq, (j + 1) * tq)

        def stacked_qT(h):
            qT = q_ref[q_rows, hl(h)].T
            zero = jnp.zeros_like(qT)
            return jnp.concatenate(
                [jnp.where(feat < DIFF_HEAD_DIM, qT, zero), jnp.where(feat >= DIFF_HEAD_DIM, qT, zero)], axis=1)

        qsT = [stacked_qT(h) for h in heads]

        def scores(t, s_ref):
            rows = pl.ds(pl.multiple_of(t * tq, tq), tq)
            for h in heads:
                s_ref[h] = _dot(k_ref[rows, hl(h)], qsT[h])

        def update(t, s_ref, causal):
            cols = pl.ds(pl.multiple_of(t * tq, tq), tq)
            for h in heads:
                s = s_ref[h]
                if causal:
                    key = lax.broadcasted_iota(I32, (tq, 2 * tq), 0)
                    qry = lax.broadcasted_iota(I32, (tq, 2 * tq), 1) % tq
                    s = jnp.where(key <= qry, s, NEG_BIG)
                m = m_ref[h]
                m_new = jnp.maximum(m, jnp.max(s, axis=0, keepdims=True))
                alpha = jnp.exp2(m - m_new)
                p = jnp.exp2(s - m_new)
                m_ref[h] = m_new
                acc_ref[h] = alpha * acc_ref[h] + _dot(vT_ref[h, :, cols], p.astype(BF16))

        m_ref[...] = jnp.full(m_ref.shape, NEG_BIG, F32)
        scores(0, first_ref)

        def two_tiles(pidx, carry):
            t = 2 * pidx
            scores(t + 1, other_ref)
            update(t, first_ref, False)
            scores(t + 2, first_ref)
            update(t + 1, other_ref, False)
            return carry

        lax.fori_loop(0, (tiles_per_step // 2) * step + j // 2, two_tiles, 0)

        if j % 2 == 0:
            update(i, first_ref, True)
        else:
            scores(i, other_ref)
            update(i - 1, first_ref, False)
            update(i, other_ref, True)

        for h in heads:
            on = acc_ref[h, 0:DIFF_V_DIM, :] * (1.0 / acc_ref[h, DIFF_V_DIM:DIFF_V_DIM + 1, :])
            o = on[:, :tq] - lam * on[:, tq:]
            ms = jnp.mean(o * o, axis=0, keepdims=True)
            o = o * lax.rsqrt(ms + EPS) * gsub_ref[...] * (1.0 - LAMBDA_INIT)
            o_ref[q_rows, hl(h)] = o.T.astype(BF16)

    first_ref, other_ref = s0_ref, s1_ref
    for j in range(tiles_per_step):
        q_tile(j, first_ref, other_ref)
        last_read = first_ref if j % 2 == 0 else other_ref
        first_ref, other_ref = (other_ref, first_ref) if last_read is first_ref else (first_ref, other_ref)


def _out_router_kernel(x_ref, a_ref, gc_ref, wo_ref, gffn_ref, wrh_ref, wrl_ref, br_ref, before_ref,
                       x1_out, hpa_out, hpb_out, idx_out, gate_out, pos_out, cnt_out, carry_ref):
    tm = x_ref.shape[0]

    @pl.when(pl.program_id(0) == 0)
    def _():
        carry_ref[...] = jnp.zeros_like(carry_ref)

    sub = tm
    pw = before_ref.shape[0]
    wr_stack = jnp.concatenate([wrh_ref[...], wrl_ref[...]], axis=0)
    eio = lax.broadcasted_iota(I32, (N_EXPERTS, sub), 0)
    before_b = before_ref[...]
    carry = carry_ref[...]
    for r in range(tm // sub):
        rows = slice(r * sub, (r + 1) * sub)
        mix = jnp.concatenate([a_ref[rows, :], gc_ref[rows, :]], axis=1)
        x1 = x_ref[rows, :] + _dot(mix, wo_ref[...])
        x1_out[rows, :] = x1
        h2 = _rms(x1, gffn_ref[...])
        hb = h2.astype(BF16)
        hpa_out[rows, :] = _pack_bf16_pairs(h2[:, :D_MODEL // 2])
        hpb_out[rows, :] = _pack_bf16_pairs(h2[:, D_MODEL // 2:])

        h_lo = (h2 - hb.astype(F32)).astype(BF16)
        both = _dot_nt(wr_stack, hb)
        logits = (both[:N_EXPERTS] + both[N_EXPERTS:]) + _dot_nt(wrh_ref[...], h_lo) + br_ref[...]

        vals, idxs, sels = [], [], []
        cur = logits
        for _ in range(TOP_K):
            m = jnp.max(cur, axis=0, keepdims=True)
            ik = jnp.min(jnp.where(cur == m, eio, N_EXPERTS), axis=0, keepdims=True)
            sel = eio == ik
            cur = jnp.where(sel, -jnp.inf, cur)
            vals.append(m)
            idxs.append(ik)
            sels.append(sel)
        es = [jnp.exp(v - vals[0]) for v in vals]
        tot = es[0] + es[1] + es[2] + es[3]
        gates = jnp.concatenate([e / tot for e in es], axis=0)
        g_hi = gates.astype(BF16)
        g_lo = (gates - g_hi.astype(F32)).astype(BF16)
        gate_out[:, rows] = jnp.concatenate(
            [g_hi, g_lo, jnp.zeros((GATE_ROWS - 2 * TOP_K, sub), BF16)], axis=0)
        idx_out[:, rows] = jnp.concatenate(idxs, axis=0)

        cnt = jnp.zeros((N_EXPERTS, sub), F32)
        for sel in sels:
            cnt = cnt + jnp.where(sel, 1.0, 0.0)
        bases = []
        for c in range(sub // pw):
            cc = cnt[:, c * pw:(c + 1) * pw]
            bases.append(carry + _dot(cc.astype(BF16), before_b))
            carry = carry + jnp.sum(cc, axis=1, keepdims=True)
        base = jnp.concatenate(bases, axis=1)
        pos_out[:, rows] = jnp.concatenate(
            [jnp.sum(jnp.where(sel, base, 0.0), axis=0, keepdims=True) for sel in sels], axis=0).astype(I32)
    carry_ref[...] = carry
    cnt_out[...] = carry


def _dest_kernel(cnt_ref, idx_ref, pos_ref, dest_out, be_out, valid_out, nbu_out):
    idx = idx_ref[...]
    dest = pos_ref[...]
    bidx = lax.broadcasted_iota(I32, be_out.shape, 1)
    be = jnp.zeros(be_out.shape, I32)
    valid = jnp.zeros(be_out.shape, I32)
    run = jnp.int32(0)
    for e in range(N_EXPERTS):
        dest = dest + jnp.where(idx == e, run, 0)
        first_block = run // ROW_BLOCK
        run = run + ((cnt_ref[e] + (ROW_BLOCK - 1)) // ROW_BLOCK) * ROW_BLOCK
        mine = jnp.logical_and(bidx >= first_block, bidx < run // ROW_BLOCK)
        valid = jnp.where(mine, jnp.clip(cnt_ref[e] - (bidx - first_block) * ROW_BLOCK, 0, ROW_BLOCK), valid)
        be = be + jnp.where(bidx >= run // ROW_BLOCK, 1, 0)
    dest_out[...] = dest
    be_out[...] = jnp.minimum(be, N_EXPERTS - 1)
    valid_out[...] = valid
    nbu_out[...] = jnp.zeros(nbu_out.shape, I32) + run // ROW_BLOCK


def _expert_ffn_kernel(be_ref, nbu_ref, nxt_ref, valid_ref, xa_ref, xb_ref, w1_hbm, b1_ref, w2_hbm, b2_ref,
                       y_ref, w1f_ref, w2f_ref, w1s_ref, w2s_ref, sem_ref):
    b = pl.program_id(0)

    def weight_copies(e):
        return (pltpu.make_async_copy(w1_hbm.at[e], w1f_ref, sem_ref.at[0]),
                pltpu.make_async_copy(w2_hbm.at[e], w2f_ref, sem_ref.at[1]))

    def mlp(rows):
        parts = _unpack_bf16_pairs(xa_ref[0:rows, :]) + _unpack_bf16_pairs(xb_ref[0:rows, :])
        xrow = jnp.concatenate([p.astype(BF16) for p in parts], axis=1)
        hm = _dot(xrow, w1s_ref[...]) + b1_ref[0]
        glu = jnp.minimum(hm[:, :D_FF], SWIGLU_LIMIT)
        lin = jnp.clip(hm[:, D_FF:], -SWIGLU_LIMIT, SWIGLU_LIMIT)
        act = glu * jax.nn.sigmoid(SWIGLU_ALPHA * glu) * (lin + 1.0)
        y = _dot(act.astype(BF16), w2s_ref[...]) + b2_ref[0]
        hw = D_MODEL // 4
        y_ref[0:rows, 0:hw] = _pack_bf16_pairs(y[:, :D_MODEL // 2])
        y_ref[0:rows, hw:] = _pack_bf16_pairs(y[:, D_MODEL // 2:])

    @pl.when(b < nbu_ref[0])
    def _():
        e = be_ref[b]

        @pl.when(b == 0)
        def _():
            for cp in weight_copies(e):
                cp.start()

        @pl.when(jnp.logical_or(b == 0, e != be_ref[jnp.maximum(b - 1, 0)]))
        def _():
            for cp in weight_copies(e):
                cp.wait()
            w1s_ref[...] = w1f_ref[...].astype(BF16)
            w2s_ref[...] = w2f_ref[...].astype(BF16)
            nxt = nxt_ref[e]

            @pl.when(nxt >= 0)
            def _():
                for cp in weight_copies(nxt):
                    cp.start()

        n_sub = (valid_ref[b] + (FFN_SUB_ROWS - 1)) // FFN_SUB_ROWS
        for k in range(1, ROW_BLOCK // FFN_SUB_ROWS + 1):
            pl.when(n_sub == k)(functools.partial(mlp, k * FFN_SUB_ROWS))


SC_WINDOW = 128
SC_GATHER_WINDOW = 64
SC_GATHER_WINDOW_COVERED = 32


def _sc_mesh():
    return plsc.VectorSubcoreMesh(core_axis_name="c", subcore_axis_name="s")


def _sc_gather_rows(table, idx, win):
    n = idx.shape[0]
    width = table.shape[1]

    @functools.partial(pl.kernel, out_type=jax.ShapeDtypeStruct((n, width), table.dtype), mesh=_sc_mesh(),
                       scratch_types=[])
    def gather_kernel(t_hbm, i_hbm, o_hbm):
        def body(i_vmem, o_vmem):
            pltpu.sync_copy(t_hbm.at[i_vmem.at[0]], o_vmem)

        pltpu.emit_pipeline(
            body,
            grid=(n // win,),
            in_specs=[pl.BlockSpec((1, win), lambda i: (i, 0))],
            out_specs=[pl.BlockSpec((win, width), lambda i: (i, 0))],
            core_axis_name=("c", "s"),
            dimension_semantics=(pltpu.PARALLEL,),
        )(i_hbm, o_hbm)

    return gather_kernel(table, idx.reshape(n // win, win))


def _sc_scatter_rows(rows, idx_rows, n_out):
    n, width = rows.shape

    @functools.partial(pl.kernel, out_type=jax.ShapeDtypeStruct((n_out, width), rows.dtype), mesh=_sc_mesh(),
                       scratch_types=[])
    def scatter_kernel(r_hbm, *refs):
        i_hbms, o_hbm = refs[:-1], refs[-1]

        def body(r_vmem, *i_vmems):
            for i_vmem in i_vmems:
                pltpu.sync_copy(r_vmem, o_hbm.at[i_vmem.at[0]])

        pltpu.emit_pipeline(
            body,
            grid=(n // SC_WINDOW,),
            in_specs=[pl.BlockSpec((SC_WINDOW, width), lambda i: (i, 0))]
            + [pl.BlockSpec((1, SC_WINDOW), lambda i: (0, i)) for _ in i_hbms],
            out_specs=[],
            core_axis_name=("c", "s"),
            dimension_semantics=(pltpu.PARALLEL,),
        )(r_hbm, *i_hbms)

    return scatter_kernel(rows, *idx_rows)


def _combine_kernel(x1_ref, yg_ref, gate_ref, spread_ref, o_ref):
    q = D_MODEL // 4
    x1 = x1_ref[...]
    acc = [x1[:, j * q:(j + 1) * q] for j in range(4)]
    gfull = lax.dot_general(gate_ref[...], spread_ref[...], (((0,), (0,)), ((), ())), preferred_element_type=F32)
    for k in range(TOP_K):
        parts = _unpack_bf16_pairs(yg_ref[k, :, 0:q]) + _unpack_bf16_pairs(yg_ref[k, :, q:])
        g = jnp.tile(gfull[:, k * LANES:(k + 1) * LANES], (1, q // LANES))
        acc = [a + g * p for a, p in zip(acc, parts)]
    for j in range(4):
        o_ref[:, j * q:(j + 1) * q] = acc[j]


def _combine_into_kernel(x1_ref, yg_ref, gate_ref, spread_ref, prev_ref, o_ref):
    del prev_ref
    _combine_kernel(x1_ref, yg_ref, gate_ref, spread_ref, o_ref)


def _block_diag_ones(width):
    r = jnp.arange(width) // HEAD_GROUP
    return (r[:, None] == r[None, :]).astype(BF16)


def _rope_tables(positions):
    half = ROT_DIM // 2
    inv_freq = ROPE_THETA ** (-jnp.arange(0, ROT_DIM, 2, dtype=F32) / ROT_DIM)
    ang = inv_freq[:, None] * positions.astype(F32).reshape(1, -1)
    cs = jnp.concatenate([jnp.cos(ang), jnp.sin(ang)], axis=0)
    cs_hi = cs.astype(BF16)
    cs_lo = (cs - cs_hi.astype(F32)).astype(BF16)
    lane = jnp.arange(LANES) % HEAD_GROUP
    j = jnp.arange(half)[:, None]
    lo_half = (lane[None, :] == j).astype(F32)
    hi_half = (lane[None, :] == j + half).astype(F32)
    spread = jnp.concatenate([
        jnp.concatenate([lo_half + hi_half, jnp.zeros((half, LANES), F32)], axis=1),
        jnp.concatenate([jnp.zeros((half, LANES), F32), hi_half - lo_half], axis=1)], axis=0)
    return jnp.concatenate([cs_hi, cs_lo], axis=0), jnp.concatenate([spread, spread], axis=0).astype(BF16)


def _full(shape):
    return pl.BlockSpec(shape, lambda *_: (0,) * len(shape))


def kernel(x, mem, positions, g_mix_norm, w_in, g_dq, g_dk, lambda_q1, lambda_k1, lambda_q2, lambda_k2, g_subln, g_sgu, w_spatial, b_spatial, g_mem_norm, w_mem_kv, g_cq, g_ck, w_out, g_ffn_norm, w_router, b_router, w_mlp1, b_mlp1, w_mlp2, b_mlp2):
    B, S, D = x.shape
    M = mem.shape[1]
    tm = TOKEN_TILE
    tr = ROUTER_TILE
    assert D == D_MODEL and S % tm == 0 and S % (ATTN_Q_TILES_PER_STEP * ATTN_TILE) == 0 and g_mix_norm.shape[0] == 1
    last_batches = B // LAST_PART_DIVISOR
    part_batches = (B - last_batches, last_batches) if last_batches > 0 and S % tr == 0 else (B,)

    xf = x.reshape(B * S, D)
    rope_cs, rope_spread = _rope_tables(positions)
    ones256 = _block_diag_ones(MEM_W)
    row = lambda v: v.reshape(1, -1).astype(F32)
    tile_row = lambda v, reps: jnp.tile(v.reshape(1, -1).astype(F32), (1, reps))
    w_in_b, w_out_b, w_kv_b = w_in[0].astype(BF16), w_out[0].astype(BF16), w_mem_kv[0].astype(BF16)
    w_sp_lanes = jnp.transpose(w_spatial[0], (1, 0, 2)).reshape(CHUNK, GMLP_GROUPS * CHUNK)
    b_sp_lanes = jnp.repeat(b_spatial[0].T, HEAD_GROUP, axis=1)
    wr = w_router[0].T.astype(F32)
    wr_hi = wr.astype(BF16)
    wr_lo = (wr - wr_hi.astype(F32)).astype(BF16)
    before = (jnp.arange(PREFIX_CHUNK)[:, None] < jnp.arange(PREFIX_CHUNK)[None, :]).astype(BF16)
    gate_row = jnp.arange(GATE_ROWS)[:, None]
    gate_spread = ((gate_row < 2 * TOP_K) & (gate_row % TOP_K == jnp.arange(TOP_K * LANES)[None, :] // LANES)).astype(BF16)
    b1r = b_mlp1[0].reshape(N_EXPERTS, 1, 2 * D_FF)
    b2r = b_mlp2[0].reshape(N_EXPERTS, 1, D)

    mb = MEM_BATCHES if B % MEM_BATCHES == 0 else 1
    kT, vm = pl.pallas_call(
        _mem_kv_kernel,
        grid=(B // mb,),
        in_specs=[pl.BlockSpec((mb, M, D), lambda b: (b, 0, 0)), _full((1, D)), _full((D, 2 * MEM_W)),
                  _full((1, MEM_W)), _full((MEM_W, MEM_W))],
        out_specs=[pl.BlockSpec((mb, MEM_W, M), lambda b: (b, 0, 0)),
                   pl.BlockSpec((mb, M, MEM_W), lambda b: (b, 0, 0))],
        out_shape=[jax.ShapeDtypeStruct((B, MEM_W, M), BF16), jax.ShapeDtypeStruct((B, M, MEM_W), BF16)],
        compiler_params=_cparams(("parallel",)),
        name="mem_kv",
    )(mem, row(g_mem_norm[0]), w_kv_b, tile_row(g_ck[0], N_MEM_HEADS), ones256)

    out = None
    b_off = 0
    for Bp in part_batches:
        N = Bp * S
        assert N % tr == 0 and N % tm == 0
        n_assign = N * TOP_K
        n_blocks = -(-n_assign // ROW_BLOCK) + N_EXPERTS
        n_rows = n_blocks * ROW_BLOCK
        nb_pad = -(-n_blocks // LANES) * LANES
        t_off = b_off * S // tm
        r_off = b_off * S // tr

        tiles_per_batch = S // tm
        tok = lambda w: pl.BlockSpec((tm, w), lambda i: (i, 0))
        tok_in = lambda w: pl.BlockSpec((tm, w), lambda i: (i + t_off, 0))
        qn, kn, vv, gc = pl.pallas_call(
            _mixer_in_kernel,
            grid=(N // tm,),
            in_specs=[tok_in(D), pl.BlockSpec((ROPE_STACK_ROWS, tm), lambda i: (0, i + t_off)),
                      _full((ROPE_STACK_ROWS, 2 * LANES)),
                      _full((1, D)), _full((D, D_IN_PROJ)),
                      _full((1, DIFF_W)), _full((1, DIFF_W)),
                      _full((1, GMLP_W)), _full((CHUNK, GMLP_GROUPS * CHUNK)), _full((CHUNK, GMLP_W)),
                      _full((1, MEM_W)), _full((MEM_W, MEM_W)),
                      pl.BlockSpec((1, MEM_W, M), lambda i: (i // tiles_per_batch + b_off, 0, 0)),
                      pl.BlockSpec((1, M, MEM_W), lambda i: (i // tiles_per_batch + b_off, 0, 0))],
            out_specs=[tok(DIFF_W), tok(DIFF_W), tok(DIFF_W), tok(GMLP_W + MEM_W)],
            out_shape=[jax.ShapeDtypeStruct((N, DIFF_W), BF16)] * 3 + [jax.ShapeDtypeStruct((N, GMLP_W + MEM_W), BF16)],
            compiler_params=_cparams(("parallel",)),
            name="mixer_in",
        )(xf, rope_cs, rope_spread, row(g_mix_norm[0]), w_in_b,
          tile_row(g_dq[0], 2 * N_DIFF_HEADS), tile_row(g_dk[0], 2 * N_DIFF_HEADS),
          row(g_sgu[0]), w_sp_lanes, b_sp_lanes, tile_row(g_cq[0], N_MEM_HEADS), ones256, kT, vm)

        tq = ATTN_TILE
        q_rows = ATTN_Q_TILES_PER_STEP * tq
        nqp = S // q_rows
        head_q = pl.BlockSpec((q_rows, DIFF_W), lambda b, i: (b * nqp + i, 0))
        head_kv = pl.BlockSpec((S, DIFF_W), lambda b, i: (b, 0))
        lam_spec = pl.BlockSpec((1, DIFF_HEAD_DIM), lambda b, i: (0, 0))
        nh = N_DIFF_HEADS
        attn = pl.pallas_call(
            _diff_attn_kernel,
            grid=(Bp, nqp),
            in_specs=[head_q, head_kv, head_kv, lam_spec, lam_spec, lam_spec, lam_spec,
                      pl.BlockSpec((DIFF_V_DIM, 1), lambda b, i: (0, 0))],
            out_specs=head_q,
            out_shape=jax.ShapeDtypeStruct((N, DIFF_W), BF16),
            scratch_shapes=[pltpu.VMEM((nh, tq, 2 * tq), F32), pltpu.VMEM((nh, tq, 2 * tq), F32),
                            pltpu.VMEM((nh, 1, 2 * tq), F32), pltpu.VMEM((nh, ATTN_EXT_ROWS, 2 * tq), F32),
                            pltpu.VMEM((nh, ATTN_EXT_ROWS, S), BF16)],
            compiler_params=_cparams(("arbitrary", "arbitrary")),
            name="diff_attn",
        )(qn, kn, vv, row(lambda_q1[0]), row(lambda_k1[0]), row(lambda_q2[0]), row(lambda_k2[0]),
          g_subln[0].reshape(DIFF_V_DIM, 1).astype(F32))

        rtok = lambda w: pl.BlockSpec((tr, w), lambda i: (i, 0))
        rtok_in = lambda w: pl.BlockSpec((tr, w), lambda i: (i + r_off, 0))
        tokT = lambda: pl.BlockSpec((TOP_K, tr), lambda i: (0, i))
        hw = D // 4
        x1, hpa, hpb, idxT, gateT, posT, counts = pl.pallas_call(
            _out_router_kernel,
            grid=(N // tr,),
            in_specs=[rtok_in(D), rtok(DIFF_W), rtok(GMLP_W + MEM_W), _full((D, D)), _full((1, D)),
                      _full((N_EXPERTS, D)), _full((N_EXPERTS, D)), _full((N_EXPERTS, 1)), _full((PREFIX_CHUNK, PREFIX_CHUNK))],
            out_specs=[rtok(D), rtok(hw), rtok(hw), tokT(), pl.BlockSpec((GATE_ROWS, tr), lambda i: (0, i)), tokT(),
                       _full((N_EXPERTS, 1))],
            out_shape=[jax.ShapeDtypeStruct((N, D), F32), jax.ShapeDtypeStruct((N, hw), U32),
                       jax.ShapeDtypeStruct((N, hw), U32), jax.ShapeDtypeStruct((TOP_K, N), I32),
                       jax.ShapeDtypeStruct((GATE_ROWS, N), BF16),
                       jax.ShapeDtypeStruct((TOP_K, N), I32), jax.ShapeDtypeStruct((N_EXPERTS, 1), F32)],
            scratch_shapes=[pltpu.VMEM((N_EXPERTS, 1), F32)],
            compiler_params=_cparams(("arbitrary",)),
            name="out_router",
        )(xf, attn, gc, w_out_b, row(g_ffn_norm[0]), wr_hi, wr_lo, b_router[0].reshape(N_EXPERTS, 1).astype(F32), before)

        destT, block_expert, block_valid, nb_used = pl.pallas_call(
            _dest_kernel,
            grid_spec=pltpu.PrefetchScalarGridSpec(
                num_scalar_prefetch=1,
                grid=(1,),
                in_specs=[pl.BlockSpec((TOP_K, N), lambda i, c: (0, 0)), pl.BlockSpec((TOP_K, N), lambda i, c: (0, 0))],
                out_specs=[pl.BlockSpec((TOP_K, N), lambda i, c: (0, 0)), pl.BlockSpec((1, nb_pad), lambda i, c: (0, 0)),
                           pl.BlockSpec((1, nb_pad), lambda i, c: (0, 0)), pl.BlockSpec((1, LANES), lambda i, c: (0, 0))],
            ),
            out_shape=[jax.ShapeDtypeStruct((TOP_K, N), I32), jax.ShapeDtypeStruct((1, nb_pad), I32),
                       jax.ShapeDtypeStruct((1, nb_pad), I32), jax.ShapeDtypeStruct((1, LANES), I32)],
            compiler_params=_cparams(("arbitrary",)),
            name="dest",
        )(counts.reshape(N_EXPERTS).astype(I32), idxT, posT)

        dest_rows = [destT[k].reshape(1, N) for k in range(TOP_K)]
        xa_buf = _sc_scatter_rows(hpa, dest_rows, n_rows)
        xb_buf = _sc_scatter_rows(hpb, dest_rows, n_rows)

        cnt_i = counts.reshape(N_EXPERTS).astype(I32)
        owner = jnp.where(cnt_i > 0, jnp.arange(N_EXPERTS, dtype=I32), N_EXPERTS)
        later = jnp.concatenate([lax.cummin(owner[::-1])[::-1][1:], jnp.full((1,), N_EXPERTS, I32)])
        next_expert = jnp.where(later < N_EXPERTS, later, -1)
        last = lambda b, be, nbu, nxt, valid: jnp.minimum(b, nbu[0] - 1)
        row_blk = lambda: pl.BlockSpec((ROW_BLOCK, hw), lambda b, be, nbu, nxt, valid: (last(b, be, nbu, nxt, valid), 0))
        y_blk = pl.BlockSpec((ROW_BLOCK, 2 * hw), lambda b, be, nbu, nxt, valid: (last(b, be, nbu, nxt, valid), 0))
        y_buf = pl.pallas_call(
            _expert_ffn_kernel,
            grid_spec=pltpu.PrefetchScalarGridSpec(
                num_scalar_prefetch=4,
                grid=(n_blocks,),
                in_specs=[row_blk(), row_blk(),
                          pl.BlockSpec(memory_space=pl.ANY),
                          pl.BlockSpec((1, 1, 2 * D_FF), lambda b, be, nbu, nxt, valid: (be[b], 0, 0)),
                          pl.BlockSpec(memory_space=pl.ANY),
                          pl.BlockSpec((1, 1, D), lambda b, be, nbu, nxt, valid: (be[b], 0, 0))],
                out_specs=y_blk,
                scratch_shapes=[pltpu.VMEM((D, 2 * D_FF), F32), pltpu.VMEM((D_FF, D), F32),
                                pltpu.VMEM((D, 2 * D_FF), BF16), pltpu.VMEM((D_FF, D), BF16),
                                pltpu.SemaphoreType.DMA((2,))],
            ),
            out_shape=jax.ShapeDtypeStruct((n_rows, 2 * hw), U32),
            compiler_params=_cparams(("arbitrary",)),
            name="expert_ffn",
        )(block_expert[0, :n_blocks], nb_used[0, :1], next_expert, block_valid[0, :n_blocks], xa_buf, xb_buf,
          w_mlp1[0], b1r, w_mlp2[0], b2r)

        last_part = b_off + Bp == B
        win = SC_GATHER_WINDOW if last_part else SC_GATHER_WINDOW_COVERED
        yg = _sc_gather_rows(y_buf, destT.reshape(n_assign), win).reshape(TOP_K, N, 2 * hw)
        tc = COMBINE_TILE
        prev = () if out is None else (out,)
        out = pl.pallas_call(
            _combine_kernel if out is None else _combine_into_kernel,
            grid=(N // tc,),
            in_specs=[pl.BlockSpec((tc, D), lambda i: (i, 0)),
                      pl.BlockSpec((TOP_K, tc, 2 * hw), lambda i: (0, i, 0)),
                      pl.BlockSpec((GATE_ROWS, tc), lambda i: (0, i)),
                      _full((GATE_ROWS, TOP_K * LANES))] + [pl.BlockSpec(memory_space=pl.ANY)] * len(prev),
            out_specs=pl.BlockSpec((tc, D), lambda i: (i + b_off * S // tc, 0)),
            out_shape=jax.ShapeDtypeStruct((B * S, D), F32),
            input_output_aliases={4: 0} if prev else {},
            compiler_params=_cparams(("parallel",)),
            name="combine",
        )(x1, yg, gateT, gate_spread, *prev)
        b_off += Bp
    return out.reshape(B, S, D)
```
